```python
import math
import jax, jax.numpy as jnp
from jax import lax
import numpy as np

D_MODEL = 1024
BATCH = 32
SEQ = 2048
DEPTH = 1

SB_HEADS = 16
SB_HEAD_DIM = 64
SB_WIDTH = SB_HEADS * SB_HEAD_DIM
CONV_CHANNELS = D_MODEL
CONV_WIDTH = 31
D_FF = 2816
Q_BLOCK = 128
N_SUBLAYERS = 3
N_MOD = 3
MACARON_WEIGHT = 0.5
DEEPNORM_ALPHA = (2.0 * DEPTH) ** 0.25
DEEPNORM_BETA = (8.0 * DEPTH) ** -0.25
LN_EPS = 1e-5
IN_SPLITS = (SB_WIDTH, SB_WIDTH, SB_WIDTH, CONV_CHANNELS, CONV_CHANNELS, D_MODEL, D_MODEL)
IN_WIDTH = sum(IN_SPLITS)

kernel_name = "hybrid_stickbreak_conformer_macaron_deepnorm_adaln"


def layer_norm(x, g, b):
    xf = x.astype(jnp.float32)
    mu = jnp.mean(xf, axis=-1, keepdims=True)
    var = jnp.mean(jnp.square(xf - mu), axis=-1, keepdims=True)
    y = (xf - mu) * lax.rsqrt(var + LN_EPS) * g.astype(jnp.float32) + b.astype(jnp.float32)
    return y.astype(x.dtype)


def swiglu(u, w_gu, w_down):
    a, g = jnp.split(u @ w_gu, 2, axis=-1)
    return (jax.nn.silu(a) * g) @ w_down


def stick_breaking_attention(q, k, v):
    seq = q.shape[2]
    scale = 1.0 / math.sqrt(q.shape[-1])
    outs = []
    for blk in range(seq // Q_BLOCK):
        start = blk * Q_BLOCK
        end = start + Q_BLOCK
        qb = q[:, :, start:end]
        kb = k[:, :, :end]
        vb = v[:, :, :end]
        z = jnp.einsum('bhqd,bhkd->bhqk', qb, kb).astype(jnp.float32) * scale
        t_idx = start + jnp.arange(Q_BLOCK)[:, None]
        s_idx = jnp.arange(end)[None, :]
        mask = s_idx < t_idx
        log_keep = jnp.where(mask, jax.nn.log_sigmoid(-z), 0.0)
        between = lax.cumsum(log_keep, axis=3, reverse=True) - log_keep
        log_w = jax.nn.log_sigmoid(z) + between
        w = jnp.where(mask, jnp.exp(log_w), 0.0)
        outs.append(jnp.einsum('bhqk,bhkd->bhqd', w.astype(vb.dtype), vb))
    return jnp.concatenate(outs, axis=2)


def conformer_conv(a, b, w_dw, b_dw, g, beta):
    h = a * jax.nn.sigmoid(b)
    h = lax.conv_general_dilated(
        h, w_dw[:, None, :].astype(h.dtype), window_strides=(1,),
        padding=[(CONV_WIDTH - 1, 0)], dimension_numbers=('NWC', 'WIO', 'NWC'),
        feature_group_count=CONV_CHANNELS) + b_dw
    return jax.nn.silu(layer_norm(h, g, beta))


def _fwd_setup_inputs(seed: int = 0) -> dict:
    key = jax.random.key(seed)
    ks = jax.random.split(key, 32)
    L, D, C = DEPTH, D_MODEL, CONV_CHANNELS

    def nrm(k, shape, std):
        return jax.random.normal(k, shape, jnp.float32) * std

    x = nrm(ks[0], (BATCH, SEQ, D), 1.0)
    c = nrm(ks[1], (BATCH, D), 1.0)
    w_ada = nrm(ks[2], (L, D, N_SUBLAYERS * N_MOD * D), 0.5 * D ** -0.5)
    b_ada = nrm(ks[3], (L, N_SUBLAYERS * N_MOD * D), 0.02)
    ffn1_w_gu = nrm(ks[4], (L, D, 2 * D_FF), D ** -0.5)
    ffn1_w_down = nrm(ks[5], (L, D_FF, D), D_FF ** -0.5 * DEEPNORM_BETA)
    ln1_g = 1.0 + nrm(ks[6], (L, D), 0.02)
    ln1_b = nrm(ks[7], (L, D), 0.02)
    w_qk = nrm(ks[8], (L, D, 2 * SB_WIDTH), D ** -0.5)
    w_v = nrm(ks[9], (L, D, SB_WIDTH), D ** -0.5 * DEEPNORM_BETA)
    w_rest = nrm(ks[10], (L, D, 2 * C + 2 * D), D ** -0.5)
    w_in = jnp.concatenate([w_qk, w_v, w_rest], axis=-1)
    w_sb_out = nrm(ks[11], (L, SB_WIDTH, D), SB_WIDTH ** -0.5 * DEEPNORM_BETA)
    conv_w = nrm(ks[12], (L, CONV_WIDTH, C), CONV_WIDTH ** -0.5)
    conv_b = nrm(ks[13], (L, C), 0.02)
    conv_ln_g = 1.0 + nrm(ks[14], (L, C), 0.02)
    conv_ln_b = nrm(ks[15], (L, C), 0.02)
    w_conv_out = nrm(ks[16], (L, C, D), C ** -0.5 * DEEPNORM_BETA)
    w_out = nrm(ks[17], (L, D, D), D ** -0.5 * DEEPNORM_BETA)
    ln2_g = 1.0 + nrm(ks[18], (L, D), 0.02)
    ln2_b = nrm(ks[19], (L, D), 0.02)
    ffn2_w_gu = nrm(ks[20], (L, D, 2 * D_FF), D ** -0.5)
    ffn2_w_down = nrm(ks[21], (L, D_FF, D), D_FF ** -0.5 * DEEPNORM_BETA)
    ln3_g = 1.0 + nrm(ks[22], (L, D), 0.02)
    ln3_b = nrm(ks[23], (L, D), 0.02)
    return {"x": x, "c": c, "w_ada": w_ada, "b_ada": b_ada,
            "ffn1_w_gu": ffn1_w_gu, "ffn1_w_down": ffn1_w_down, "ln1_g": ln1_g, "ln1_b": ln1_b,
            "w_in": w_in, "w_sb_out": w_sb_out, "conv_w": conv_w, "conv_b": conv_b,
            "conv_ln_g": conv_ln_g, "conv_ln_b": conv_ln_b, "w_conv_out": w_conv_out,
            "w_out": w_out, "ln2_g": ln2_g, "ln2_b": ln2_b,
            "ffn2_w_gu": ffn2_w_gu, "ffn2_w_down": ffn2_w_down, "ln3_g": ln3_g, "ln3_b": ln3_b}


def _fwd_reference(x, c, w_ada, b_ada, ffn1_w_gu, ffn1_w_down, ln1_g, ln1_b, w_in, w_sb_out,
              conv_w, conv_b, conv_ln_g, conv_ln_b, w_conv_out, w_out, ln2_g, ln2_b,
              ffn2_w_gu, ffn2_w_down, ln3_g, ln3_b):
    bsz, seq, _ = x.shape
    split_idx = list(np.cumsum(IN_SPLITS)[:-1])
    for l in range(DEPTH):
        mod = (jax.nn.silu(c) @ w_ada[l] + b_ada[l]).reshape(bsz, N_SUBLAYERS * N_MOD, 1, D_MODEL)
        sh1, sc1, g1, sh2, sc2, g2, sh3, sc3, g3 = [mod[:, i] for i in range(N_SUBLAYERS * N_MOD)]

        u = x * (1.0 + sc1) + sh1
        x = layer_norm(DEEPNORM_ALPHA * x + g1 * (MACARON_WEIGHT * swiglu(u, ffn1_w_gu[l], ffn1_w_down[l])),
                       ln1_g[l], ln1_b[l])

        u = x * (1.0 + sc2) + sh2
        q, k, v, glu_a, glu_b, gate_a, gate_b = jnp.split(u @ w_in[l], split_idx, axis=-1)
        heads = lambda t: t.reshape(bsz, seq, SB_HEADS, SB_HEAD_DIM).transpose(0, 2, 1, 3)
        y_sb = stick_breaking_attention(heads(q), heads(k), heads(v))
        y_sb = y_sb.transpose(0, 2, 1, 3).reshape(bsz, seq, SB_WIDTH) @ w_sb_out[l]
        y_conv = conformer_conv(glu_a, glu_b, conv_w[l], conv_b[l], conv_ln_g[l], conv_ln_b[l]) @ w_conv_out[l]
        merged = jax.nn.sigmoid(gate_a) * y_sb + jax.nn.sigmoid(gate_b) * y_conv
        x = layer_norm(DEEPNORM_ALPHA * x + g2 * (merged @ w_out[l]), ln2_g[l], ln2_b[l])

        u = x * (1.0 + sc3) + sh3
        x = layer_norm(DEEPNORM_ALPHA * x + g3 * (MACARON_WEIGHT * swiglu(u, ffn2_w_gu[l], ffn2_w_down[l])),
                       ln3_g[l], ln3_b[l])
    return x


import jax as _jax
import jax.numpy as _jnp

TWIN_FORMAT = 'train_step'
FWD_PARAMS = ['x', 'c', 'w_ada', 'b_ada', 'ffn1_w_gu', 'ffn1_w_down', 'ln1_g', 'ln1_b', 'w_in', 'w_sb_out', 'conv_w', 'conv_b', 'conv_ln_g', 'conv_ln_b', 'w_conv_out', 'w_out', 'ln2_g', 'ln2_b', 'ffn2_w_gu', 'ffn2_w_down', 'ln3_g', 'ln3_b']
TWIN_WEIGHTS = ['w_ada', 'b_ada', 'ffn1_w_gu', 'ffn1_w_down', 'ln1_g', 'ln1_b', 'w_in', 'w_sb_out', 'conv_w', 'conv_b', 'conv_ln_g', 'conv_ln_b', 'w_conv_out', 'w_out', 'ln2_g', 'ln2_b', 'ffn2_w_gu', 'ffn2_w_down', 'ln3_g', 'ln3_b']
TWIN_DIFF_INPUT = 'x'
TWIN_INPUTS = ['x', 'c', 'w_ada', 'b_ada', 'ffn1_w_gu', 'ffn1_w_down', 'ln1_g', 'ln1_b', 'w_in', 'w_sb_out', 'conv_w', 'conv_b', 'conv_ln_g', 'conv_ln_b', 'w_conv_out', 'w_out', 'ln2_g', 'ln2_b', 'ffn2_w_gu', 'ffn2_w_down', 'ln3_g', 'ln3_b', 'loss_target', 'm_w_ada', 'm_b_ada', 'm_ffn1_w_gu', 'm_ffn1_w_down', 'm_ln1_g', 'm_ln1_b', 'm_w_in', 'm_w_sb_out', 'm_conv_w', 'm_conv_b', 'm_conv_ln_g', 'm_conv_ln_b', 'm_w_conv_out', 'm_w_out', 'm_ln2_g', 'm_ln2_b', 'm_ffn2_w_gu', 'm_ffn2_w_down', 'm_ln3_g', 'm_ln3_b', 'v_w_ada', 'v_b_ada', 'v_ffn1_w_gu', 'v_ffn1_w_down', 'v_ln1_g', 'v_ln1_b', 'v_w_in', 'v_w_sb_out', 'v_conv_w', 'v_conv_b', 'v_conv_ln_g', 'v_conv_ln_b', 'v_w_conv_out', 'v_w_out', 'v_ln2_g', 'v_ln2_b', 'v_ffn2_w_gu', 'v_ffn2_w_down', 'v_ln3_g', 'v_ln3_b']
TWIN_OUTPUTS = ['loss', 'grad_x', 'grad_w_ada', 'grad_b_ada', 'grad_ffn1_w_gu', 'grad_ffn1_w_down', 'grad_ln1_g', 'grad_ln1_b', 'grad_w_in', 'grad_w_sb_out', 'grad_conv_w', 'grad_conv_b', 'grad_conv_ln_g', 'grad_conv_ln_b', 'grad_w_conv_out', 'grad_w_out', 'grad_ln2_g', 'grad_ln2_b', 'grad_ffn2_w_gu', 'grad_ffn2_w_down', 'grad_ln3_g', 'grad_ln3_b', 'delta_w_ada', 'delta_b_ada', 'delta_ffn1_w_gu', 'delta_ffn1_w_down', 'delta_ln1_g', 'delta_ln1_b', 'delta_w_in', 'delta_w_sb_out', 'delta_conv_w', 'delta_conv_b', 'delta_conv_ln_g', 'delta_conv_ln_b', 'delta_w_conv_out', 'delta_w_out', 'delta_ln2_g', 'delta_ln2_b', 'delta_ffn2_w_gu', 'delta_ffn2_w_down', 'delta_ln3_g', 'delta_ln3_b', 'new_m_w_ada', 'new_m_b_ada', 'new_m_ffn1_w_gu', 'new_m_ffn1_w_down', 'new_m_ln1_g', 'new_m_ln1_b', 'new_m_w_in', 'new_m_w_sb_out', 'new_m_conv_w', 'new_m_conv_b', 'new_m_conv_ln_g', 'new_m_conv_ln_b', 'new_m_w_conv_out', 'new_m_w_out', 'new_m_ln2_g', 'new_m_ln2_b', 'new_m_ffn2_w_gu', 'new_m_ffn2_w_down', 'new_m_ln3_g', 'new_m_ln3_b', 'new_v_w_ada', 'new_v_b_ada', 'new_v_ffn1_w_gu', 'new_v_ffn1_w_down', 'new_v_ln1_g', 'new_v_ln1_b', 'new_v_w_in', 'new_v_w_sb_out', 'new_v_conv_w', 'new_v_conv_b', 'new_v_conv_ln_g', 'new_v_conv_ln_b', 'new_v_w_conv_out', 'new_v_w_out', 'new_v_ln2_g', 'new_v_ln2_b', 'new_v_ffn2_w_gu', 'new_v_ffn2_w_down', 'new_v_ln3_g', 'new_v_ln3_b']
TWIN_LEAF_KINDS = {'loss': 'loss', 'grad_x': 'grad_x', 'grad_w_ada': 'grad_w', 'grad_b_ada': 'grad_w', 'grad_ffn1_w_gu': 'grad_w', 'grad_ffn1_w_down': 'grad_w', 'grad_ln1_g': 'grad_w', 'grad_ln1_b': 'grad_w', 'grad_w_in': 'grad_w', 'grad_w_sb_out': 'grad_w', 'grad_conv_w': 'grad_w', 'grad_conv_b': 'grad_w', 'grad_conv_ln_g': 'grad_w', 'grad_conv_ln_b': 'grad_w', 'grad_w_conv_out': 'grad_w', 'grad_w_out': 'grad_w', 'grad_ln2_g': 'grad_w', 'grad_ln2_b': 'grad_w', 'grad_ffn2_w_gu': 'grad_w', 'grad_ffn2_w_down': 'grad_w', 'grad_ln3_g': 'grad_w', 'grad_ln3_b': 'grad_w', 'delta_w_ada': 'delta_w', 'delta_b_ada': 'delta_w', 'delta_ffn1_w_gu': 'delta_w', 'delta_ffn1_w_down': 'delta_w', 'delta_ln1_g': 'delta_w', 'delta_ln1_b': 'delta_w', 'delta_w_in': 'delta_w', 'delta_w_sb_out': 'delta_w', 'delta_conv_w': 'delta_w', 'delta_conv_b': 'delta_w', 'delta_conv_ln_g': 'delta_w', 'delta_conv_ln_b': 'delta_w', 'delta_w_conv_out': 'delta_w', 'delta_w_out': 'delta_w', 'delta_ln2_g': 'delta_w', 'delta_ln2_b': 'delta_w', 'delta_ffn2_w_gu': 'delta_w', 'delta_ffn2_w_down': 'delta_w', 'delta_ln3_g': 'delta_w', 'delta_ln3_b': 'delta_w', 'new_m_w_ada': 'new_m', 'new_m_b_ada': 'new_m', 'new_m_ffn1_w_gu': 'new_m', 'new_m_ffn1_w_down': 'new_m', 'new_m_ln1_g': 'new_m', 'new_m_ln1_b': 'new_m', 'new_m_w_in': 'new_m', 'new_m_w_sb_out': 'new_m', 'new_m_conv_w': 'new_m', 'new_m_conv_b': 'new_m', 'new_m_conv_ln_g': 'new_m', 'new_m_conv_ln_b': 'new_m', 'new_m_w_conv_out': 'new_m', 'new_m_w_out': 'new_m', 'new_m_ln2_g': 'new_m', 'new_m_ln2_b': 'new_m', 'new_m_ffn2_w_gu': 'new_m', 'new_m_ffn2_w_down': 'new_m', 'new_m_ln3_g': 'new_m', 'new_m_ln3_b': 'new_m', 'new_v_w_ada': 'new_v', 'new_v_b_ada': 'new_v', 'new_v_ffn1_w_gu': 'new_v', 'new_v_ffn1_w_down': 'new_v', 'new_v_ln1_g': 'new_v', 'new_v_ln1_b': 'new_v', 'new_v_w_in': 'new_v', 'new_v_w_sb_out': 'new_v', 'new_v_conv_w': 'new_v', 'new_v_conv_b': 'new_v', 'new_v_conv_ln_g': 'new_v', 'new_v_conv_ln_b': 'new_v', 'new_v_w_conv_out': 'new_v', 'new_v_w_out': 'new_v', 'new_v_ln2_g': 'new_v', 'new_v_ln2_b': 'new_v', 'new_v_ffn2_w_gu': 'new_v', 'new_v_ffn2_w_down': 'new_v', 'new_v_ln3_g': 'new_v', 'new_v_ln3_b': 'new_v'}


def _forward(args):
    return _fwd_reference(*[args[k] for k in FWD_PARAMS])


def _output_shape():
    out = _jax.eval_shape(lambda: _forward(_fwd_setup_inputs(0)))
    return out.shape, out.dtype

N_MICROBATCH = 1
ADAM_LR = 0.001
ADAM_B1 = 0.9
ADAM_B2 = 0.999
ADAM_EPS = 1e-08
ADAM_WD = 0.01
ADAM_STEP = 10
PER_EXAMPLE_BATCH_AXIS = {'x': 0, 'c': 0, 'loss_target': 0}
SHARED_INPUTS = []
_WEIGHT_DTYPES = {'w_ada': _jnp.float32, 'b_ada': _jnp.float32, 'ffn1_w_gu': _jnp.float32, 'ffn1_w_down': _jnp.float32, 'ln1_g': _jnp.float32, 'ln1_b': _jnp.float32, 'w_in': _jnp.float32, 'w_sb_out': _jnp.float32, 'conv_w': _jnp.float32, 'conv_b': _jnp.float32, 'conv_ln_g': _jnp.float32, 'conv_ln_b': _jnp.float32, 'w_conv_out': _jnp.float32, 'w_out': _jnp.float32, 'ln2_g': _jnp.float32, 'ln2_b': _jnp.float32, 'ffn2_w_gu': _jnp.float32, 'ffn2_w_down': _jnp.float32, 'ln3_g': _jnp.float32, 'ln3_b': _jnp.float32}
MOMENT_SCALE = {'w_ada': 1.705465e-02, 'b_ada': 2.807092e-02, 'ffn1_w_gu': 8.386618e-03, 'ffn1_w_down': 2.313511e-02, 'ln1_g': 1.915132e+00, 'ln1_b': 7.301119e-01, 'w_in': 5.512558e-03, 'w_sb_out': 1.025296e-02, 'conv_w': 8.150519e-03, 'conv_b': 1.669784e-02, 'conv_ln_g': 1.173027e-02, 'conv_ln_b': 1.183940e-02, 'w_conv_out': 1.420856e-02, 'w_out': 1.749106e-02, 'ln2_g': 1.907262e+00, 'ln2_b': 7.287659e-01, 'ffn2_w_gu': 8.533140e-03, 'ffn2_w_down': 2.355121e-02, 'ln3_g': 6.397660e+01, 'ln3_b': 1.422484e+00}


def _to_microbatches(a, axis):
    t = _jnp.moveaxis(a, axis, 0)
    t = t.reshape((N_MICROBATCH, t.shape[0] // N_MICROBATCH) + t.shape[1:])
    return _jnp.moveaxis(t, 1, axis + 1)


def setup_inputs(seed: int = 0) -> dict:
    inp = _fwd_setup_inputs(seed)
    key = _jax.random.fold_in(_jax.random.key(seed), 7919)
    shape, _ = _output_shape()
    out = dict(inp)
    out["loss_target"] = _jax.random.normal(_jax.random.fold_in(key, 0), shape, _jnp.float32)
    for i, name in enumerate(TWIN_WEIGHTS):
        w = inp[name].astype(_jnp.float32)
        if MOMENT_SCALE is None:
            s = _jnp.sqrt(_jnp.mean(_jnp.square(w)) + 1e-30)
        else:
            s = MOMENT_SCALE[name]
        km, kv = _jax.random.split(_jax.random.fold_in(key, i + 1))
        out[name] = w
        out["m_" + name] = s * _jax.random.normal(km, w.shape, _jnp.float32)
        out["v_" + name] = (s * s) * _jax.random.uniform(kv, w.shape, _jnp.float32, 0.5, 1.5)
    if N_MICROBATCH > 1:
        for name, axis in PER_EXAMPLE_BATCH_AXIS.items():
            out[name] = _to_microbatches(out[name], axis)
    return {'x': out['x'], 'c': out['c'], 'w_ada': out['w_ada'], 'b_ada': out['b_ada'], 'ffn1_w_gu': out['ffn1_w_gu'], 'ffn1_w_down': out['ffn1_w_down'], 'ln1_g': out['ln1_g'], 'ln1_b': out['ln1_b'], 'w_in': out['w_in'], 'w_sb_out': out['w_sb_out'], 'conv_w': out['conv_w'], 'conv_b': out['conv_b'], 'conv_ln_g': out['conv_ln_g'], 'conv_ln_b': out['conv_ln_b'], 'w_conv_out': out['w_conv_out'], 'w_out': out['w_out'], 'ln2_g': out['ln2_g'], 'ln2_b': out['ln2_b'], 'ffn2_w_gu': out['ffn2_w_gu'], 'ffn2_w_down': out['ffn2_w_down'], 'ln3_g': out['ln3_g'], 'ln3_b': out['ln3_b'], 'loss_target': out['loss_target'], 'm_w_ada': out['m_w_ada'], 'm_b_ada': out['m_b_ada'], 'm_ffn1_w_gu': out['m_ffn1_w_gu'], 'm_ffn1_w_down': out['m_ffn1_w_down'], 'm_ln1_g': out['m_ln1_g'], 'm_ln1_b': out['m_ln1_b'], 'm_w_in': out['m_w_in'], 'm_w_sb_out': out['m_w_sb_out'], 'm_conv_w': out['m_conv_w'], 'm_conv_b': out['m_conv_b'], 'm_conv_ln_g': out['m_conv_ln_g'], 'm_conv_ln_b': out['m_conv_ln_b'], 'm_w_conv_out': out['m_w_conv_out'], 'm_w_out': out['m_w_out'], 'm_ln2_g': out['m_ln2_g'], 'm_ln2_b': out['m_ln2_b'], 'm_ffn2_w_gu': out['m_ffn2_w_gu'], 'm_ffn2_w_down': out['m_ffn2_w_down'], 'm_ln3_g': out['m_ln3_g'], 'm_ln3_b': out['m_ln3_b'], 'v_w_ada': out['v_w_ada'], 'v_b_ada': out['v_b_ada'], 'v_ffn1_w_gu': out['v_ffn1_w_gu'], 'v_ffn1_w_down': out['v_ffn1_w_down'], 'v_ln1_g': out['v_ln1_g'], 'v_ln1_b': out['v_ln1_b'], 'v_w_in': out['v_w_in'], 'v_w_sb_out': out['v_w_sb_out'], 'v_conv_w': out['v_conv_w'], 'v_conv_b': out['v_conv_b'], 'v_conv_ln_g': out['v_conv_ln_g'], 'v_conv_ln_b': out['v_conv_ln_b'], 'v_w_conv_out': out['v_w_conv_out'], 'v_w_out': out['v_w_out'], 'v_ln2_g': out['v_ln2_g'], 'v_ln2_b': out['v_ln2_b'], 'v_ffn2_w_gu': out['v_ffn2_w_gu'], 'v_ffn2_w_down': out['v_ffn2_w_down'], 'v_ln3_g': out['v_ln3_g'], 'v_ln3_b': out['v_ln3_b']}


def _loss(weights, diff, rest, loss_target):
    with _jax.named_scope("forward"):
        args = {**rest, TWIN_DIFF_INPUT: diff, **{k: w.astype(_WEIGHT_DTYPES[k]) for k, w in weights.items()}}
        y = _forward(args)
    with _jax.named_scope("loss_head"):
        err = _jnp.square(y.astype(_jnp.float32) - loss_target)
        return 0.5 * _jnp.sum(_jnp.mean(err, axis=-1)) if err.ndim else 0.5 * err


def _adamw(w, g, m, v):
    m = ADAM_B1 * m + (1.0 - ADAM_B1) * g
    v = ADAM_B2 * v + (1.0 - ADAM_B2) * _jnp.square(g)
    m_hat = m / (1.0 - ADAM_B1 ** ADAM_STEP)
    v_hat = v / (1.0 - ADAM_B2 ** ADAM_STEP)
    delta = -ADAM_LR * (m_hat / (_jnp.sqrt(v_hat) + ADAM_EPS) + ADAM_WD * w)
    return delta, m, v


def reference(x, c, w_ada, b_ada, ffn1_w_gu, ffn1_w_down, ln1_g, ln1_b, w_in, w_sb_out, conv_w, conv_b, conv_ln_g, conv_ln_b, w_conv_out, w_out, ln2_g, ln2_b, ffn2_w_gu, ffn2_w_down, ln3_g, ln3_b, loss_target, m_w_ada, m_b_ada, m_ffn1_w_gu, m_ffn1_w_down, m_ln1_g, m_ln1_b, m_w_in, m_w_sb_out, m_conv_w, m_conv_b, m_conv_ln_g, m_conv_ln_b, m_w_conv_out, m_w_out, m_ln2_g, m_ln2_b, m_ffn2_w_gu, m_ffn2_w_down, m_ln3_g, m_ln3_b, v_w_ada, v_b_ada, v_ffn1_w_gu, v_ffn1_w_down, v_ln1_g, v_ln1_b, v_w_in, v_w_sb_out, v_conv_w, v_conv_b, v_conv_ln_g, v_conv_ln_b, v_w_conv_out, v_w_out, v_ln2_g, v_ln2_b, v_ffn2_w_gu, v_ffn2_w_down, v_ln3_g, v_ln3_b):
    given = dict(x=x, c=c, w_ada=w_ada, b_ada=b_ada, ffn1_w_gu=ffn1_w_gu, ffn1_w_down=ffn1_w_down, ln1_g=ln1_g, ln1_b=ln1_b, w_in=w_in, w_sb_out=w_sb_out, conv_w=conv_w, conv_b=conv_b, conv_ln_g=conv_ln_g, conv_ln_b=conv_ln_b, w_conv_out=w_conv_out, w_out=w_out, ln2_g=ln2_g, ln2_b=ln2_b, ffn2_w_gu=ffn2_w_gu, ffn2_w_down=ffn2_w_down, ln3_g=ln3_g, ln3_b=ln3_b, loss_target=loss_target, m_w_ada=m_w_ada, m_b_ada=m_b_ada, m_ffn1_w_gu=m_ffn1_w_gu, m_ffn1_w_down=m_ffn1_w_down, m_ln1_g=m_ln1_g, m_ln1_b=m_ln1_b, m_w_in=m_w_in, m_w_sb_out=m_w_sb_out, m_conv_w=m_conv_w, m_conv_b=m_conv_b, m_conv_ln_g=m_conv_ln_g, m_conv_ln_b=m_conv_ln_b, m_w_conv_out=m_w_conv_out, m_w_out=m_w_out, m_ln2_g=m_ln2_g, m_ln2_b=m_ln2_b, m_ffn2_w_gu=m_ffn2_w_gu, m_ffn2_w_down=m_ffn2_w_down, m_ln3_g=m_ln3_g, m_ln3_b=m_ln3_b, v_w_ada=v_w_ada, v_b_ada=v_b_ada, v_ffn1_w_gu=v_ffn1_w_gu, v_ffn1_w_down=v_ffn1_w_down, v_ln1_g=v_ln1_g, v_ln1_b=v_ln1_b, v_w_in=v_w_in, v_w_sb_out=v_w_sb_out, v_conv_w=v_conv_w, v_conv_b=v_conv_b, v_conv_ln_g=v_conv_ln_g, v_conv_ln_b=v_conv_ln_b, v_w_conv_out=v_w_conv_out, v_w_out=v_w_out, v_ln2_g=v_ln2_g, v_ln2_b=v_ln2_b, v_ffn2_w_gu=v_ffn2_w_gu, v_ffn2_w_down=v_ffn2_w_down, v_ln3_g=v_ln3_g, v_ln3_b=v_ln3_b)
    weights = {n: given[n] for n in TWIN_WEIGHTS}
    shared = {n: given[n] for n in SHARED_INPUTS}
    per_example = {n: given[n] for n in ['x', 'c']}
    grad_fn = _jax.value_and_grad(_loss, argnums=(0, 1))

    def one_microbatch(ex, loss_target):
        ex = dict(ex)
        diff = ex.pop(TWIN_DIFF_INPUT)
        return grad_fn(weights, diff, {**shared, **ex}, loss_target)

    if N_MICROBATCH == 1:
        loss, (grad_w, grad_x) = one_microbatch(per_example, given["loss_target"])
    else:
        def body(carry, xs):
            loss_sum, grad_sum = carry
            l_k, (gw_k, gx_k) = one_microbatch(xs[0], xs[1])
            with _jax.named_scope("update"):
                return (loss_sum + l_k, _jax.tree.map(_jnp.add, grad_sum, gw_k)), gx_k

        init = (_jnp.zeros((), _jnp.float32), _jax.tree.map(_jnp.zeros_like, weights))
        (loss, grad_w), grad_x = _jax.lax.scan(body, init, (per_example, given["loss_target"]))
    with _jax.named_scope("update"):
        delta_w, new_m, new_v = {}, {}, {}
        for n in TWIN_WEIGHTS:
            delta_w[n], new_m[n], new_v[n] = _adamw(weights[n], grad_w[n], given["m_" + n], given["v_" + n])
    return (loss, grad_x, *[grad_w[n] for n in TWIN_WEIGHTS], *[delta_w[n] for n in TWIN_WEIGHTS],
            *[new_m[n] for n in TWIN_WEIGHTS], *[new_v[n] for n in TWIN_WEIGHTS])
```

```python
import functools
import math

import jax
import jax.numpy as jnp
from jax import lax
from jax.experimental import pallas as pl
from jax.experimental.pallas import tpu as pltpu

F32 = jnp.float32
BF = jnp.bfloat16
SDS = jax.ShapeDtypeStruct
MESH = pl.DeviceIdType.MESH

N_DEV = 8
SB_HEAD_DIM = 64
N_MOD_ROWS = 9
MACARON_WEIGHT = 0.5
DEEPNORM_ALPHA = 2.0 ** 0.25
LN_EPS = 1e-5
ADAM_LR = 0.001
ADAM_B1 = 0.9
ADAM_B2 = 0.999
ADAM_EPS = 1e-08
ADAM_WD = 0.01
ADAM_STEP = 10

V7X_VMEM_LIMIT = 52 * 1024 * 1024
LANE = 128
HALO = 32


def _cparams(sem=None):
    return pltpu.CompilerParams(dimension_semantics=sem, vmem_limit_bytes=V7X_VMEM_LIMIT)


def _dot_nn(a, b):
    return lax.dot_general(a, b, (((1,), (0,)), ((), ())), preferred_element_type=F32)


def _dot_nt(a, b):
    return lax.dot_general(a, b, (((1,), (1,)), ((), ())), preferred_element_type=F32)


def _dot_tn(a, b):
    return lax.dot_general(a, b, (((0,), (0,)), ((), ())), preferred_element_type=F32)


def _sig(x):
    return 1.0 / (1.0 + jnp.exp(-x))


def _ln_stats(r):
    mu = jnp.mean(r, axis=-1, keepdims=True)
    d = r - mu
    var = jnp.mean(d * d, axis=-1, keepdims=True)
    rstd = lax.rsqrt(var + LN_EPS)
    return d * rstd, rstd


def _ln_bwd(dxh, xh, rstd):
    m1 = jnp.mean(dxh, axis=-1, keepdims=True)
    m2 = jnp.mean(dxh * xh, axis=-1, keepdims=True)
    return rstd * (dxh - m1 - xh * m2)


def _rowsum(v):
    return jnp.sum(v, axis=0, keepdims=True)


def _row_tile(n, cap):
    if n <= cap:
        return n
    best = None
    for t in range(8, cap + 1, 8):
        if n % t == 0:
            best = t
    assert best is not None, (n, cap)
    return best


def _coords():
    x, y, c = lax.axis_index("x"), lax.axis_index("y"), lax.axis_index("c")
    return x, y, c


def _flip(v, bit):
    return 1 - v if bit else v


def small_allgather(blk, name):
    r, n = blk.shape

    def body(x_ref, out_ref, send_sems, recv_sems):
        x, y, c = _coords()
        me = 4 * x + 2 * y + c
        out_ref[me] = x_ref[...]
        copies = []
        for k in range(1, N_DEV):
            peer = (_flip(x, k & 4), _flip(y, k & 2), _flip(c, k & 1))
            cp = pltpu.make_async_remote_copy(
                src_ref=x_ref, dst_ref=out_ref.at[me], send_sem=send_sems.at[k - 1],
                recv_sem=recv_sems.at[k - 1], device_id=peer, device_id_type=MESH)
            cp.start()
            copies.append(cp)
        for k in range(1, N_DEV):
            px, py, pc = _flip(x, k & 4), _flip(y, k & 2), _flip(c, k & 1)
            slot = 4 * px + 2 * py + pc
            pltpu.make_async_remote_copy(
                src_ref=x_ref, dst_ref=out_ref.at[slot], send_sem=send_sems.at[k - 1],
                recv_sem=recv_sems.at[k - 1], device_id=(px, py, pc), device_id_type=MESH).wait_recv()
        for cp in copies:
            cp.wait_send()

    return pl.pallas_call(
        body, name=name,
        out_shape=SDS((N_DEV, r, n), blk.dtype),
        in_specs=[pl.BlockSpec(memory_space=pltpu.VMEM)],
        out_specs=pl.BlockSpec(memory_space=pltpu.VMEM),
        scratch_shapes=[pltpu.SemaphoreType.DMA((N_DEV - 1,)), pltpu.SemaphoreType.DMA((N_DEV - 1,))],
    )(blk)


def allgather_weights(shards, name):
    n = len(shards)
    per = 7

    def body(*refs):
        ins, outs = refs[:n], refs[n:2 * n]
        send_sems, recv_sems, local_sems = refs[2 * n:]
        x, y, c = _coords()
        me = 4 * x + 2 * y + c
        sibling = (x, y, 1 - c)
        chips = [(1 - x, y), (x, 1 - y), (1 - x, 1 - y)]

        def slot(px, py, pc):
            return 4 * px + 2 * py + pc

        def copy(t, k, block, to, src=None):
            dst = outs[t].at[slot(*block)]
            return pltpu.make_async_remote_copy(
                src_ref=dst if src is None else src, dst_ref=dst,
                send_sem=send_sems.at[per * t + k], recv_sem=recv_sems.at[per * t + k],
                device_id=to, device_id_type=MESH)

        local = []
        first = []
        for t in range(n):
            lc = pltpu.make_async_copy(ins[t], outs[t].at[me], local_sems.at[t])
            lc.start()
            local.append(lc)
            cp = copy(t, 0, (x, y, c), sibling, src=ins[t])
            cp.start()
            first.append(cp)
            for j, chip in enumerate(chips):
                cp = copy(t, 1 + j, (x, y, c), (*chip, c), src=ins[t])
                cp.start()
                first.append(cp)
        passed = []
        for t in range(n):
            for j, chip in enumerate(chips):
                copy(t, 1 + j, (*chip, c), (x, y, c)).wait_recv()
                cp = copy(t, 4 + j, (*chip, c), sibling)
                cp.start()
                passed.append(cp)
        for t in range(n):
            copy(t, 0, (x, y, 1 - c), (x, y, c)).wait_recv()
            for j, chip in enumerate(chips):
                copy(t, 4 + j, (*chip, 1 - c), (x, y, c)).wait_recv()
        for cp in first + passed:
            cp.wait_send()
        for lc in local:
            lc.wait()

    anyspec = pl.BlockSpec(memory_space=pl.ANY)
    return pl.pallas_call(
        body, name=name,
        out_shape=[SDS((N_DEV,) + s.shape, s.dtype) for s in shards],
        in_specs=[anyspec] * n, out_specs=[anyspec] * n,
        scratch_shapes=[pltpu.SemaphoreType.DMA((per * n,)), pltpu.SemaphoreType.DMA((per * n,)),
                        pltpu.SemaphoreType.DMA((n,))],
    )(*shards)


def exchange_sibling(grads, name):
    n = len(grads)

    def body(*refs):
        ins, outs = refs[:n], refs[n:2 * n]
        send_sems, recv_sems = refs[2 * n:]
        x, y, c = _coords()
        copies = []
        for t in range(n):
            cp = pltpu.make_async_remote_copy(
                src_ref=ins[t].at[:, 1 - c], dst_ref=outs[t], send_sem=send_sems.at[t],
                recv_sem=recv_sems.at[t], device_id=(x, y, 1 - c), device_id_type=MESH)
            cp.start()
            copies.append(cp)
        for cp in copies:
            cp.wait_recv()
        for cp in copies:
            cp.wait_send()

    anyspec = pl.BlockSpec(memory_space=pl.ANY)
    return pl.pallas_call(
        body, name=name,
        out_shape=[SDS((4,) + g.shape[2:], g.dtype) for g in grads],
        in_specs=[anyspec] * n, out_specs=[anyspec] * n,
        scratch_shapes=[pltpu.SemaphoreType.DMA((n,)), pltpu.SemaphoreType.DMA((n,))],
    )(*grads)


def exchange_chips(sums, name):
    n = len(sums)

    def body(*refs):
        ins, outs = refs[:n], refs[n:2 * n]
        send_sems, recv_sems = refs[2 * n:]
        x, y, c = _coords()
        copies = []
        for t in range(n):
            for j in range(1, 4):
                peer = (_flip(x, j & 2), _flip(y, j & 1), c)
                cp = pltpu.make_async_remote_copy(
                    src_ref=ins[t].at[j], dst_ref=outs[t].at[j - 1], send_sem=send_sems.at[3 * t + j - 1],
                    recv_sem=recv_sems.at[3 * t + j - 1], device_id=peer, device_id_type=MESH)
                cp.start()
                copies.append(cp)
        for cp in copies:
            cp.wait_recv()
        for cp in copies:
            cp.wait_send()

    anyspec = pl.BlockSpec(memory_space=pl.ANY)
    return pl.pallas_call(
        body, name=name,
        out_shape=[SDS((3,) + s.shape[1:], s.dtype) for s in sums],
        in_specs=[anyspec] * n, out_specs=[anyspec] * n,
        scratch_shapes=[pltpu.SemaphoreType.DMA((3 * n,)), pltpu.SemaphoreType.DMA((3 * n,))],
    )(*sums)


def chip_sum(g42, recv, qc, name):
    _, _, R, C = g42.shape
    tr = _row_tile(R, 256)

    def body(qc_ref, a_ref, b_ref, o_ref):
        o_ref[...] = a_ref[...] + b_ref[...]

    gs = pltpu.PrefetchScalarGridSpec(
        num_scalar_prefetch=1, grid=(4, R // tr),
        in_specs=[pl.BlockSpec((None, None, tr, C), lambda j, i, s: (jnp.bitwise_xor(s[0], j), s[1], i, 0)),
                  pl.BlockSpec((None, tr, C), lambda j, i, s: (jnp.bitwise_xor(s[0], j), i, 0))],
        out_specs=pl.BlockSpec((None, tr, C), lambda j, i, s: (j, i, 0)))
    return pl.pallas_call(body, name=name, grid_spec=gs, out_shape=SDS((4, R, C), F32),
                          compiler_params=_cparams(("arbitrary", "arbitrary")))(qc, g42, recv)


def small_sum(g8, name):
    def body(g_ref, o_ref):
        acc = g_ref[0]
        for k in range(1, N_DEV):
            acc = acc + g_ref[k]
        o_ref[...] = acc
    return pl.pallas_call(body, name=name, out_shape=SDS(g8.shape[1:], F32))(g8)


def adamw(w, m, v, parts, name):
    R, C = w.shape
    tr = _row_tile(R, 256)
    npart = len(parts)
    c1 = 1.0 / (1.0 - ADAM_B1 ** ADAM_STEP)
    c2 = 1.0 / (1.0 - ADAM_B2 ** ADAM_STEP)

    def body(*refs):
        w_ref, m_ref, v_ref = refs[:3]
        p_refs = refs[3:3 + npart]
        g_ref, d_ref, nm_ref, nv_ref = refs[3 + npart:]
        g = p_refs[0][...]
        for p in p_refs[1:]:
            g = g + p[...]
        nm = ADAM_B1 * m_ref[...] + (1.0 - ADAM_B1) * g
        nv = ADAM_B2 * v_ref[...] + (1.0 - ADAM_B2) * (g * g)
        mh = nm * c1
        vh = nv * c2
        g_ref[...] = g
        nm_ref[...] = nm
        nv_ref[...] = nv
        d_ref[...] = -ADAM_LR * (mh / (jnp.sqrt(vh) + ADAM_EPS) + ADAM_WD * w_ref[...])

    wspec = pl.BlockSpec((tr, C), lambda i: (i, 0))
    pspecs = [pl.BlockSpec(bs(tr, C), im) for (_, bs, im) in parts]
    outs = pl.pallas_call(
        body, name=name, grid=(R // tr,),
        in_specs=[wspec] * 3 + pspecs, out_specs=[wspec] * 4,
        out_shape=[SDS((R, C), F32)] * 4,
        compiler_params=_cparams(("parallel",)))(w, m, v, *[p[0] for p in parts])
    return outs


def _plain_part(g):
    return (g, lambda tr, C: (tr, C), lambda i: (i, 0))


def _slot_part(g, slot):
    return (g, lambda tr, C: (None, tr, C), lambda i, s=slot: (s, i, 0))


def ada_fwd(c_all, w_cols, b_cols, name):
    Bg, D = c_all.shape
    n = w_cols.shape[1]

    def body(c_ref, w_ref, b_ref, o_ref, s_ref):
        cc = c_ref[...]
        s = cc * _sig(cc)
        s_ref[...] = s
        o_ref[...] = jnp.dot(s, w_ref[...], preferred_element_type=F32, precision=lax.Precision.HIGHEST) + b_ref[...]

    return pl.pallas_call(body, name=name, out_shape=[SDS((Bg, n), F32), SDS((Bg, D), F32)],
                          compiler_params=_cparams())(c_all, w_cols, b_cols)


def ada_bwd(s_all, dmod_cols, dmod_all, name):
    Bg, D = s_all.shape
    n = dmod_cols.shape[1]

    def body(s_ref, dc_ref, da_ref, gw_ref, gb_ref):
        gw_ref[...] = lax.dot_general(s_ref[...], dc_ref[...], (((0,), (0,)), ((), ())),
                                      preferred_element_type=F32, precision=lax.Precision.HIGHEST)
        acc = da_ref[0:1, :]
        for r in range(1, Bg):
            acc = acc + da_ref[r:r + 1, :]
        gb_ref[...] = acc

    return pl.pallas_call(body, name=name, out_shape=[SDS((D, n), F32), SDS((1, dmod_all.shape[1]), F32)],
                          compiler_params=_cparams())(s_all, dmod_cols, dmod_all)


def _vec(D, rank):
    return pl.BlockSpec((1, D), (lambda i: (0, 0)) if rank == 1 else (lambda i, j: (0, 0)))


def _modspec(D, tpb, rank):
    if rank == 1:
        return pl.BlockSpec((None, 1, D), lambda i: (i // tpb, 0, 0))
    return pl.BlockSpec((None, 1, D), lambda i, j: (i // tpb, 0, 0))


def ffn_up(xs, pg, pb, sc, sh, wg8, name):
    T, D = xs.shape
    n2, _, nb = wg8.shape
    nj = n2 // 2
    S = T // sc.shape[0]
    tm = min(512, S)
    tpb = S // tm

    def body(x_ref, pg_ref, pb_ref, sc_ref, sh_ref, wa_ref, wg_ref, u_ref, a_ref, g_ref, p_ref, u_s):
        @pl.when(pl.program_id(1) == 0)
        def _():
            xin = x_ref[...] * pg_ref[...] + pb_ref[...]
            u = (xin * (1.0 + sc_ref[...]) + sh_ref[...]).astype(BF)
            u_s[...] = u
            u_ref[...] = u
        u = u_s[...]
        a = _dot_nn(u, wa_ref[...])
        g = _dot_nn(u, wg_ref[...])
        a_ref[...] = a.astype(BF)
        g_ref[...] = g.astype(BF)
        p_ref[...] = ((a * _sig(a)) * g).astype(BF)

    blk = pl.BlockSpec((None, tm, nb), lambda i, j: (j, i, 0))
    row = pl.BlockSpec((tm, D), lambda i, j: (i, 0))
    return pl.pallas_call(
        body, name=name, grid=(T // tm, nj),
        in_specs=[row, _vec(D, 2), _vec(D, 2), _modspec(D, tpb, 2), _modspec(D, tpb, 2),
                  pl.BlockSpec((None, D, nb), lambda i, j: (j, 0, 0)),
                  pl.BlockSpec((None, D, nb), lambda i, j: (j + nj, 0, 0))],
        out_specs=[row, blk, blk, blk],
        out_shape=[SDS((T, D), BF)] + [SDS((nj, T, nb), BF)] * 3,
        scratch_shapes=[pltpu.VMEM((tm, D), BF)],
        compiler_params=_cparams(("parallel", "arbitrary")))(xs, pg, pb, sc, sh, wg8, wg8)


def ffn_down_ln(p4, wd, xs, pg, pb, gate, name):
    nj, T, nb = p4.shape
    D = wd.shape[1]
    S = T // gate.shape[0]
    tm = min(512, S)
    tpb = S // tm

    def body(p_ref, wd_ref, x_ref, pg_ref, pb_ref, gate_ref, xh_ref, rs_ref, f_ref, acc):
        k = pl.program_id(1)

        @pl.when(k == 0)
        def _():
            acc[...] = jnp.zeros_like(acc)
        acc[...] += _dot_nn(p_ref[...], wd_ref[...])

        @pl.when(k == nj - 1)
        def _():
            f = acc[...]
            xin = x_ref[...] * pg_ref[...] + pb_ref[...]
            r = DEEPNORM_ALPHA * xin + gate_ref[...] * (MACARON_WEIGHT * f)
            xh, rstd = _ln_stats(r)
            xh_ref[...] = xh
            rs_ref[...] = rstd
            f_ref[...] = f.astype(BF)

    row = pl.BlockSpec((tm, D), lambda i, k: (i, 0))
    return pl.pallas_call(
        body, name=name, grid=(T // tm, nj),
        in_specs=[pl.BlockSpec((None, tm, nb), lambda i, k: (k, i, 0)),
                  pl.BlockSpec((nb, D), lambda i, k: (k, 0)),
                  row, _vec(D, 2), _vec(D, 2), _modspec(D, tpb, 2)],
        out_specs=[row, pl.BlockSpec((tm, 1), lambda i, k: (i, 0)), row],
        out_shape=[SDS((T, D), F32), SDS((T, 1), F32), SDS((T, D), BF)],
        scratch_shapes=[pltpu.VMEM((tm, D), F32)],
        compiler_params=_cparams(("parallel", "arbitrary")))(p4, wd, xs, pg, pb, gate)


def mod_matmul(xs, pg, pb, sc, sh, w, name):
    T, D = xs.shape
    N = w.shape[1]
    S = T // sc.shape[0]
    tm = min(512, S)
    tpb = S // tm
    tn = D

    def body(x_ref, pg_ref, pb_ref, sc_ref, sh_ref, w_ref, u_ref, o_ref, u_s):
        @pl.when(pl.program_id(1) == 0)
        def _():
            xin = x_ref[...] * pg_ref[...] + pb_ref[...]
            u = (xin * (1.0 + sc_ref[...]) + sh_ref[...]).astype(BF)
            u_s[...] = u
            u_ref[...] = u
        o_ref[...] = _dot_nn(u_s[...], w_ref[...])

    row = pl.BlockSpec((tm, D), lambda i, j: (i, 0))
    return pl.pallas_call(
        body, name=name, grid=(T // tm, N // tn),
        in_specs=[row, _vec(D, 2), _vec(D, 2), _modspec(D, tpb, 2), _modspec(D, tpb, 2),
                  pl.BlockSpec((D, tn), lambda i, j: (0, j))],
        out_specs=[row, pl.BlockSpec((tm, tn), lambda i, j: (i, j))],
        out_shape=[SDS((T, D), BF), SDS((T, N), F32)],
        scratch_shapes=[pltpu.VMEM((tm, D), BF)],
        compiler_params=_cparams(("parallel", "arbitrary")))(xs, pg, pb, sc, sh, w)


def _att_consts(tq):
    row = lax.broadcasted_iota(jnp.int32, (tq, tq), 0)
    col = lax.broadcasted_iota(jnp.int32, (tq, tq), 1)
    mask = col < row
    r2 = lax.broadcasted_iota(jnp.int32, (tq, 2 * tq), 0)
    c2 = lax.broadcasted_iota(jnp.int32, (tq, 2 * tq), 1)
    usum = jnp.where((c2 >= tq) | (r2 > c2), 1.0, 0.0).astype(BF)
    lsum = jnp.where((c2 >= tq) | (r2 < c2), 1.0, 0.0).astype(BF)
    return mask, usum, lsum


def _split_dot(v, m):
    hi = v.astype(BF)
    lo = (v - hi.astype(F32)).astype(BF)
    return _dot_nn(hi, m) + _dot_nn(lo, m)


def _softplus(z):
    return jnp.maximum(z, 0.0) + jnp.log(1.0 + jnp.exp(-jnp.abs(z)))


def att_fwd(proj, Bl, S, D, name):
    dh = SB_HEAD_DIM
    cw = min(LANE, D)
    hp = cw // dh
    nblk = D // cw
    tq = min(128, S)
    nq = S // tq
    scale = 1.0 / math.sqrt(dh)

    def body(q_ref, k_ref, v_ref, o_ref, qs, ks, vs):
        mask, usum, _ = _att_consts(tq)
        for hh in range(hp):
            sl = slice(hh * dh, (hh + 1) * dh)
            qs[hh] = q_ref[:, sl].astype(BF)
            ks[hh] = k_ref[:, sl].astype(BF)
            vs[hh] = v_ref[:, sl].astype(BF)
        for hh in range(hp):
            sl = slice(hh * dh, (hh + 1) * dh)

            def qloop(qb, _, hh=hh, sl=sl):
                qo = pl.multiple_of(qb * tq, tq)
                q = qs[hh, pl.ds(qo, tq), :]

                def blk(kb, carry, acc, diag):
                    ko = pl.multiple_of(kb * tq, tq)
                    k = ks[hh, pl.ds(ko, tq), :]
                    v = vs[hh, pl.ds(ko, tq), :]
                    z = _dot_nt(q, k) * scale
                    sp = _softplus(z)
                    lk = -sp
                    if diag:
                        lk = jnp.where(mask, lk, 0.0)
                    cs = _split_dot(lk, usum)
                    w = jnp.exp((z - sp) + carry + cs[:, :tq])
                    if diag:
                        w = jnp.where(mask, w, 0.0)
                    acc = acc + _dot_nn(w.astype(BF), v)
                    return carry + cs[:, tq:], acc

                carry, acc = blk(qb, jnp.zeros((tq, tq), F32), jnp.zeros((tq, dh), F32), True)
                carry, acc = lax.fori_loop(
                    0, qb, lambda i, ca: blk(qb - 1 - i, ca[0], ca[1], False), (carry, acc))
                o_ref[pl.ds(qo, tq), sl] = acc.astype(BF)
                return 0

            lax.fori_loop(0, nq, qloop, 0)

    def seg(s):
        return pl.BlockSpec((S, cw), lambda b, h: (b, s * nblk + h))

    return pl.pallas_call(
        body, name=name, grid=(Bl, nblk),
        in_specs=[seg(0), seg(1), seg(2)],
        out_specs=pl.BlockSpec((S, cw), lambda b, h: (b, h)),
        out_shape=SDS((Bl * S, D), BF),
        scratch_shapes=[pltpu.VMEM((hp, S, dh), BF)] * 3,
        compiler_params=_cparams(("parallel", "parallel")))(proj, proj, proj)


def conv_fwd(proj, cw32, cb, cg, cbeta, Bl, S, D, kw, name):
    T = Bl * S
    ts = min(128, S)
    ns = S // ts
    off = HALO - (kw - 1)
    rc = min(64, ts)
    cw = min(LANE, D)

    def body(a_ref, b_ref, ha_ref, hb_ref, w_ref, cb_ref, g_ref, be_ref, cs_ref, xh_ref, rs_ref, hext, conv_s):
        i = pl.program_id(1)
        hext[pl.ds(HALO, ts), :] = a_ref[...] * _sig(b_ref[...])
        hh = ha_ref[...] * _sig(hb_ref[...])
        hext[pl.ds(0, HALO), :] = jnp.where(i == 0, 0.0, hh)
        for cb_ in range(D // cw):
            cols = slice(cb_ * cw, (cb_ + 1) * cw)
            accs = [jnp.zeros((rc, cw), F32) for _ in range(ts // rc)]
            for k in range(kw):
                wk = w_ref[k:k + 1, cols]
                for r in range(ts // rc):
                    accs[r] = accs[r] + wk * hext[pl.ds(r * rc + off + k, rc), cols]
            for r in range(ts // rc):
                conv_s[pl.ds(r * rc, rc), cols] = accs[r]
        conv = conv_s[...] + cb_ref[...]
        xh, rstd = _ln_stats(conv)
        xh_ref[...] = xh
        rs_ref[...] = rstd
        cl = xh * g_ref[...] + be_ref[...]
        cs_ref[...] = (cl * _sig(cl)).astype(BF)

    hpb = ts // HALO

    def tile(seg):
        return pl.BlockSpec((ts, D), lambda b, i: (b * ns + i, seg))

    def halo(seg):
        return pl.BlockSpec((HALO, D), lambda b, i: (jnp.maximum((b * ns + i) * hpb - 1, 0), seg))

    row = pl.BlockSpec((ts, D), lambda b, i: (b * ns + i, 0))
    vec = pl.BlockSpec((1, D), lambda b, i: (0, 0))
    return pl.pallas_call(
        body, name=name, grid=(Bl, ns),
        in_specs=[tile(3), tile(4), halo(3), halo(4), pl.BlockSpec((HALO, D), lambda b, i: (0, 0)), vec, vec, vec],
        out_specs=[row, row, pl.BlockSpec((ts, 1), lambda b, i: (b * ns + i, 0))],
        out_shape=[SDS((T, D), BF), SDS((T, D), F32), SDS((T, 1), F32)],
        scratch_shapes=[pltpu.VMEM((ts + HALO, D), F32), pltpu.VMEM((ts, D), F32)],
        compiler_params=_cparams(("parallel", "arbitrary")))(proj, proj, proj, proj, cw32, cb, cg, cbeta)


def mix_fwd(yatt, cs, proj, wsb, wco, wout, xs, pg, pb, gate, name):
    T, D = yatt.shape
    S = T // gate.shape[0]
    tm = min(256, S)
    tpb = S // tm

    def body(ya_ref, cs_ref, ga_ref, gb_ref, wsb_ref, wco_ref, wout_ref, x_ref, pg_ref, pb_ref, gate_ref,
             xh_ref, rs_ref, ysb_ref, yco_ref, mg_ref, o_ref):
        ysb = _dot_nn(ya_ref[...], wsb_ref[...])
        yco = _dot_nn(cs_ref[...], wco_ref[...])
        merged = _sig(ga_ref[...]) * ysb + _sig(gb_ref[...]) * yco
        mg = merged.astype(BF)
        o = _dot_nn(mg, wout_ref[...])
        xin = x_ref[...] * pg_ref[...] + pb_ref[...]
        r = DEEPNORM_ALPHA * xin + gate_ref[...] * o
        xh, rstd = _ln_stats(r)
        xh_ref[...] = xh
        rs_ref[...] = rstd
        ysb_ref[...] = ysb.astype(BF)
        yco_ref[...] = yco.astype(BF)
        mg_ref[...] = mg
        o_ref[...] = o.astype(BF)

    row = pl.BlockSpec((tm, D), lambda i: (i, 0))
    wfull = pl.BlockSpec((D, D), lambda i: (0, 0))
    return pl.pallas_call(
        body, name=name, grid=(T // tm,),
        in_specs=[row, row, pl.BlockSpec((tm, D), lambda i: (i, 5)), pl.BlockSpec((tm, D), lambda i: (i, 6)),
                  wfull, wfull, wfull, row, _vec(D, 1), _vec(D, 1), _modspec(D, tpb, 1)],
        out_specs=[row, pl.BlockSpec((tm, 1), lambda i: (i, 0)), row, row, row, row],
        out_shape=[SDS((T, D), F32), SDS((T, 1), F32)] + [SDS((T, D), BF)] * 4,
        compiler_params=_cparams(("parallel",)))(yatt, cs, proj, proj, wsb, wco, wout, xs, pg, pb, gate)


def ln_bwd(dout, xh, rstd, lng, lnb, gate, sub, res_w, name, target=None):
    T, D = xh.shape
    Bl = gate.shape[0]
    S = T // Bl
    tm = min(256, S)
    tpb = S // tm
    first = target is not None

    def body(*refs):
        if first:
            tg_ref, xh_ref, rs_ref, g_ref, b_ref, gate_ref, sub_ref = refs[:7]
            dr_ref, ds_ref, dg_ref, db_ref, dgate_ref, loss_ref = refs[7:]
        else:
            do_ref, xh_ref, rs_ref, g_ref, b_ref, gate_ref, sub_ref = refs[:7]
            dr_ref, ds_ref, dg_ref, db_ref, dgate_ref = refs[7:]
        i = pl.program_id(0)
        xh_ = xh_ref[...]
        if first:
            diff = (xh_ * g_ref[...] + b_ref[...]) - tg_ref[...]
            lsum = jnp.sum(jnp.sum(diff * diff, axis=1, keepdims=True), axis=0, keepdims=True) * (0.5 / D)
            do = diff * (1.0 / D)
        else:
            do = do_ref[...]

        @pl.when(i == 0)
        def _():
            dg_ref[...] = jnp.zeros_like(dg_ref)
            db_ref[...] = jnp.zeros_like(db_ref)
            if first:
                loss_ref[...] = jnp.zeros_like(loss_ref)

        @pl.when(i % tpb == 0)
        def _():
            dgate_ref[...] = jnp.zeros_like(dgate_ref)

        if first:
            loss_ref[...] += jnp.broadcast_to(lsum, loss_ref.shape)
        dg_ref[...] += _rowsum(do * xh_)
        db_ref[...] += _rowsum(do)
        dr = _ln_bwd(do * g_ref[...], xh_, rs_ref[...])
        dr_ref[...] = dr
        ds_ref[...] = (dr * gate_ref[...] * res_w).astype(BF)
        dgate_ref[...] += _rowsum(dr * (res_w * sub_ref[...].astype(F32)))

    row = pl.BlockSpec((tm, D), lambda i: (i, 0))
    vec = _vec(D, 1)
    mod = _modspec(D, tpb, 1)
    out_specs = [row, row, vec, vec, mod]
    out_shape = [SDS((T, D), F32), SDS((T, D), BF), SDS((1, D), F32), SDS((1, D), F32), SDS((Bl, 1, D), F32)]
    if first:
        out_specs.append(pl.BlockSpec((8, LANE), lambda i: (0, 0)))
        out_shape.append(SDS((8, LANE), F32))
    return pl.pallas_call(
        body, name=name, grid=(T // tm,),
        in_specs=[row, row, pl.BlockSpec((tm, 1), lambda i: (i, 0)), vec, vec, mod, row],
        out_specs=out_specs, out_shape=out_shape,
        compiler_params=_cparams(("arbitrary",)))(target if first else dout, xh, rstd, lng, lnb, gate, sub)


def swiglu_bwd(df, wd, a4, g4, name):
    nj, T, nb = a4.shape
    D = df.shape[1]
    tm = min(512, T)

    def body(df_ref, wd_ref, a_ref, g_ref, o_ref):
        dp = _dot_nt(df_ref[...], wd_ref[...])
        a = a_ref[...].astype(F32)
        g = g_ref[...].astype(F32)
        s = _sig(a)
        o_ref[0] = (dp * g * (s * (1.0 + a * (1.0 - s)))).astype(BF)
        o_ref[1] = (dp * (a * s)).astype(BF)

    blk = pl.BlockSpec((None, tm, nb), lambda i, j: (j, i, 0))
    return pl.pallas_call(
        body, name=name, grid=(T // tm, nj),
        in_specs=[pl.BlockSpec((tm, D), lambda i, j: (i, 0)), pl.BlockSpec((nb, D), lambda i, j: (j, 0)), blk, blk],
        out_specs=pl.BlockSpec((2, None, tm, nb), lambda i, j: (0, j, i, 0)),
        out_shape=SDS((2, nj, T, nb), BF),
        compiler_params=_cparams(("parallel", "arbitrary")))(df, wd, a4, g4)


def mod_bwd(dh, w, dr, xs, pg, pb, sc, blocked, name):
    T, D = dr.shape
    Bl = sc.shape[0]
    S = T // Bl
    tm = min(512, S)
    tpb = S // tm
    if blocked:
        nk, _, kb = dh.shape
        dh_spec = pl.BlockSpec((None, tm, kb), lambda i, k: (k, i, 0))
        w_spec = pl.BlockSpec((None, D, kb), lambda i, k: (k, 0, 0))
    else:
        kb = D
        nk = dh.shape[1] // kb
        dh_spec = pl.BlockSpec((tm, kb), lambda i, k: (i, k))
        w_spec = pl.BlockSpec((D, kb), lambda i, k: (0, k))

    def body(dh_ref, w_ref, dr_ref, x_ref, pg_ref, pb_ref, sc_ref, dx_ref, dsc_ref, dsh_ref, acc):
        i = pl.program_id(0)
        k = pl.program_id(1)

        @pl.when(k == 0)
        def _():
            acc[...] = jnp.zeros_like(acc)
        acc[...] += _dot_nt(dh_ref[...], w_ref[...])

        @pl.when((k == nk - 1) & (i % tpb == 0))
        def _():
            dsc_ref[...] = jnp.zeros_like(dsc_ref)
            dsh_ref[...] = jnp.zeros_like(dsh_ref)

        @pl.when(k == nk - 1)
        def _():
            du = acc[...]
            xin = x_ref[...] * pg_ref[...] + pb_ref[...]
            dx_ref[...] = DEEPNORM_ALPHA * dr_ref[...] + du * (1.0 + sc_ref[...])
            dsc_ref[...] += _rowsum(du * xin)
            dsh_ref[...] += _rowsum(du)

    row = pl.BlockSpec((tm, D), lambda i, k: (i, 0))
    mod = _modspec(D, tpb, 2)
    return pl.pallas_call(
        body, name=name, grid=(T // tm, nk),
        in_specs=[dh_spec, w_spec, row, row, _vec(D, 2), _vec(D, 2), mod],
        out_specs=[row, mod, mod],
        out_shape=[SDS((T, D), F32), SDS((Bl, 1, D), F32), SDS((Bl, 1, D), F32)],
        scratch_shapes=[pltpu.VMEM((tm, D), F32)],
        compiler_params=_cparams(("arbitrary", "arbitrary")))(dh, w, dr, xs, pg, pb, sc)


def merge_bwd(do2, proj, ysb, yco, wsb, wco, wout, name):
    T, D = do2.shape
    tm = min(256, T)

    def body(do_ref, ga_ref, gb_ref, ysb_ref, yco_ref, wsb_ref, wco_ref, wout_ref,
             dysb_ref, dyco_ref, dga_ref, dgb_ref, dya_ref, dcs_ref):
        dm = _dot_nt(do_ref[...], wout_ref[...])
        sa = _sig(ga_ref[...])
        sb = _sig(gb_ref[...])
        dysb = (dm * sa).astype(BF)
        dyco = (dm * sb).astype(BF)
        dysb_ref[...] = dysb
        dyco_ref[...] = dyco
        dga_ref[...] = (dm * ysb_ref[...].astype(F32) * (sa * (1.0 - sa))).astype(BF)
        dgb_ref[...] = (dm * yco_ref[...].astype(F32) * (sb * (1.0 - sb))).astype(BF)
        dya_ref[...] = _dot_nt(dysb, wsb_ref[...]).astype(BF)
        dcs_ref[...] = _dot_nt(dyco, wco_ref[...])

    row = pl.BlockSpec((tm, D), lambda i: (i, 0))
    wfull = pl.BlockSpec((D, D), lambda i: (0, 0))
    return pl.pallas_call(
        body, name=name, grid=(T // tm,),
        in_specs=[row, pl.BlockSpec((tm, D), lambda i: (i, 5)), pl.BlockSpec((tm, D), lambda i: (i, 6)),
                  row, row, wfull, wfull, wfull],
        out_specs=[row] * 6,
        out_shape=[SDS((T, D), BF)] * 5 + [SDS((T, D), F32)],
        compiler_params=_cparams(("parallel",)))(do2, proj, proj, ysb, yco, wsb, wco, wout)


def att_bwd(proj, dyatt, Bl, S, D, name):
    dh = SB_HEAD_DIM
    cw = min(LANE, D)
    hp = cw // dh
    nblk = D // cw
    tq = min(128, S)
    nq = S // tq
    scale = 1.0 / math.sqrt(dh)

    def body(q_ref, k_ref, v_ref, do_ref, dq_ref, dk_ref, dv_ref, qs, ks, vs, dos, z_s, sp_s, car_s, dk_acc, dv_acc):
        mask, usum, lsum = _att_consts(tq)
        for hh in range(hp):
            sl = slice(hh * dh, (hh + 1) * dh)
            qs[hh] = q_ref[:, sl].astype(BF)
            ks[hh] = k_ref[:, sl].astype(BF)
            vs[hh] = v_ref[:, sl].astype(BF)
            dos[hh] = do_ref[:, sl]
        dk_acc[...] = jnp.zeros_like(dk_acc)
        dv_acc[...] = jnp.zeros_like(dv_acc)
        for hh in range(hp):
            sl = slice(hh * dh, (hh + 1) * dh)

            def qloop(qb, _, hh=hh, sl=sl):
                qo = pl.multiple_of(qb * tq, tq)
                q = qs[hh, pl.ds(qo, tq), :]
                do = dos[hh, pl.ds(qo, tq), :]

                def sweep1(kb, carry, diag):
                    ko = pl.multiple_of(kb * tq, tq)
                    z = _dot_nt(q, ks[hh, pl.ds(ko, tq), :]) * scale
                    sp = _softplus(z)
                    lk = -sp
                    if diag:
                        lk = jnp.where(mask, lk, 0.0)
                    z_s[kb] = z
                    sp_s[kb] = sp
                    car_s[kb] = carry
                    return carry + _split_dot(lk, usum)[:, tq:]

                carry = sweep1(qb, jnp.zeros((tq, tq), F32), True)
                lax.fori_loop(0, qb, lambda i, ca: sweep1(qb - 1 - i, ca, False), carry)

                def sweep2(kb, gcar, dq, diag):
                    ko = pl.multiple_of(kb * tq, tq)
                    k = ks[hh, pl.ds(ko, tq), :]
                    v = vs[hh, pl.ds(ko, tq), :]
                    z = z_s[kb]
                    sp = sp_s[kb]
                    lk = -sp
                    if diag:
                        lk = jnp.where(mask, lk, 0.0)
                    cs = _split_dot(lk, usum)[:, :tq]
                    w = jnp.exp((z - sp) + car_s[kb] + cs)
                    if diag:
                        w = jnp.where(mask, w, 0.0)
                    dlw = _dot_nt(do, v) * w
                    gs = _split_dot(dlw, lsum)
                    gsum = gcar + gs[:, :tq]
                    sg = jnp.exp(z - sp)
                    dz = (dlw * (1.0 - sg) - sg * gsum) * scale
                    if diag:
                        dz = jnp.where(mask, dz, 0.0)
                    dzb = dz.astype(BF)
                    dk_acc[hh, pl.ds(ko, tq), :] += _dot_tn(dzb, q)
                    dv_acc[hh, pl.ds(ko, tq), :] += _dot_tn(w.astype(BF), do)
                    return gcar + gs[:, tq:], dq + _dot_nn(dzb, k)

                gcar, dq = lax.fori_loop(
                    0, qb, lambda kb, c2: sweep2(kb, c2[0], c2[1], False),
                    (jnp.zeros((tq, tq), F32), jnp.zeros((tq, dh), F32)))
                gcar, dq = sweep2(qb, gcar, dq, True)
                dq_ref[pl.ds(qo, tq), sl] = dq.astype(BF)
                return 0

            lax.fori_loop(0, nq, qloop, 0)
        for hh in range(hp):
            sl = slice(hh * dh, (hh + 1) * dh)
            dk_ref[:, sl] = dk_acc[hh].astype(BF)
            dv_ref[:, sl] = dv_acc[hh].astype(BF)

    def seg(s):
        return pl.BlockSpec((S, cw), lambda b, h: (b, s * nblk + h))

    blk = pl.BlockSpec((S, cw), lambda b, h: (b, h))
    return pl.pallas_call(
        body, name=name, grid=(Bl, nblk),
        in_specs=[seg(0), seg(1), seg(2), blk],
        out_specs=[blk, blk, blk],
        out_shape=[SDS((Bl * S, D), BF)] * 3,
        scratch_shapes=[pltpu.VMEM((hp, S, dh), BF)] * 4 + [pltpu.VMEM((nq, tq, tq), F32)] * 3
        + [pltpu.VMEM((hp, S, dh), F32)] * 2,
        compiler_params=_cparams(("parallel", "parallel")))(proj, proj, proj, dyatt)


def conv_ln_bwd(dcs, xhc, rstd_c, cg, cbeta, name):
    T, D = dcs.shape
    tm = min(256, T)

    def body(dcs_ref, xh_ref, rs_ref, g_ref, b_ref, dconv_ref, dg_ref, db_ref, dcb_ref):
        @pl.when(pl.program_id(0) == 0)
        def _():
            dg_ref[...] = jnp.zeros_like(dg_ref)
            db_ref[...] = jnp.zeros_like(db_ref)
            dcb_ref[...] = jnp.zeros_like(dcb_ref)
        xh = xh_ref[...]
        cl = xh * g_ref[...] + b_ref[...]
        s = _sig(cl)
        dcl = dcs_ref[...] * (s * (1.0 + cl * (1.0 - s)))
        dg_ref[...] += _rowsum(dcl * xh)
        db_ref[...] += _rowsum(dcl)
        dconv = _ln_bwd(dcl * g_ref[...], xh, rs_ref[...])
        dconv_ref[...] = dconv
        dcb_ref[...] += _rowsum(dconv)

    row = pl.BlockSpec((tm, D), lambda i: (i, 0))
    vec = _vec(D, 1)
    return pl.pallas_call(
        body, name=name, grid=(T // tm,),
        in_specs=[row, row, pl.BlockSpec((tm, 1), lambda i: (i, 0)), vec, vec],
        out_specs=[row, vec, vec, vec],
        out_shape=[SDS((T, D), F32)] + [SDS((1, D), F32)] * 3,
        compiler_params=_cparams(("arbitrary",)))(dcs, xhc, rstd_c, cg, cbeta)


def conv_bwd(dconv, proj, cw32, Bl, S, D, kw, name):
    T = Bl * S
    ts = min(128, S)
    ns = S // ts
    off = HALO - (kw - 1)
    rc = min(64, ts)
    cw = min(LANE, D)
    hpb = ts // HALO
    nhb = T // HALO

    def body(dc_ref, dcn_ref, a_ref, b_ref, ha_ref, hb_ref, w_ref, da_ref, db_ref, dw_ref, hext, dext, dh_s):
        b_ = pl.program_id(0)
        i = pl.program_id(1)

        @pl.when((b_ == 0) & (i == 0))
        def _():
            dw_ref[...] = jnp.zeros_like(dw_ref)
        a = a_ref[...]
        sb = _sig(b_ref[...])
        hext[pl.ds(HALO, ts), :] = a * sb
        hh = ha_ref[...] * _sig(hb_ref[...])
        hext[pl.ds(0, HALO), :] = jnp.where(i == 0, 0.0, hh)
        dext[pl.ds(0, ts), :] = dc_ref[...]
        dext[pl.ds(ts, HALO), :] = jnp.where(i == ns - 1, 0.0, dcn_ref[...])
        for cb_ in range(D // cw):
            cols = slice(cb_ * cw, (cb_ + 1) * cw)
            accs = [jnp.zeros((rc, cw), F32) for _ in range(ts // rc)]
            for k in range(kw):
                wk = w_ref[k:k + 1, cols]
                wsum = jnp.zeros((rc, cw), F32)
                for r in range(ts // rc):
                    accs[r] = accs[r] + wk * dext[pl.ds(r * rc + (kw - 1) - k, rc), cols]
                    wsum = wsum + dext[pl.ds(r * rc, rc), cols] * hext[pl.ds(r * rc + off + k, rc), cols]
                dw_ref[k:k + 1, cols] += _rowsum(wsum)
            for r in range(ts // rc):
                dh_s[pl.ds(r * rc, rc), cols] = accs[r]
        dhc = dh_s[...]
        da_ref[...] = (dhc * sb).astype(BF)
        db_ref[...] = (dhc * a * (sb * (1.0 - sb))).astype(BF)

    def tile(seg):
        return pl.BlockSpec((ts, D), lambda b, i: (b * ns + i, seg))

    def halo(seg):
        return pl.BlockSpec((HALO, D), lambda b, i: (jnp.maximum((b * ns + i) * hpb - 1, 0), seg))

    row = pl.BlockSpec((ts, D), lambda b, i: (b * ns + i, 0))
    nxt = pl.BlockSpec((HALO, D), lambda b, i: (jnp.minimum((b * ns + i + 1) * hpb, nhb - 1), 0))
    wspec = pl.BlockSpec((HALO, D), lambda b, i: (0, 0))
    return pl.pallas_call(
        body, name=name, grid=(Bl, ns),
        in_specs=[row, nxt, tile(3), tile(4), halo(3), halo(4), wspec],
        out_specs=[row, row, wspec],
        out_shape=[SDS((T, D), BF), SDS((T, D), BF), SDS((HALO, D), F32)],
        scratch_shapes=[pltpu.VMEM((ts + HALO, D), F32), pltpu.VMEM((ts + HALO, D), F32), pltpu.VMEM((ts, D), F32)],
        compiler_params=_cparams(("arbitrary", "arbitrary")))(dconv, dconv, proj, proj, proj, proj, cw32)


def matmul_tn(xa, ga, x_spec, g_spec, out_shape, out_spec, acc_shape, grid, name):
    nk = grid[-1]

    def body(x_ref, g_ref, o_ref, acc):
        k = pl.program_id(len(grid) - 1)

        @pl.when(k == 0)
        def _():
            acc[...] = jnp.zeros_like(acc)
        acc[...] += _dot_tn(x_ref[...], g_ref[...])

        @pl.when(k == nk - 1)
        def _():
            o_ref[...] = acc[...]

    return pl.pallas_call(
        body, name=name, grid=grid, in_specs=[x_spec, g_spec], out_specs=out_spec,
        out_shape=SDS(out_shape, F32), scratch_shapes=[pltpu.VMEM(acc_shape, F32)],
        compiler_params=_cparams(("parallel",) * (len(grid) - 1) + ("arbitrary",)))(xa, ga)


def wgrad_std(xa, ga, name):
    T, M = xa.shape
    N = ga.shape[1]
    tk = min(512, T)
    return matmul_tn(xa, ga, pl.BlockSpec((tk, M), lambda n, k: (k, 0)), pl.BlockSpec((tk, N), lambda n, k: (k, 0)),
                     (M, N), pl.BlockSpec((M, N), lambda n, k: (0, 0)), (M, N), (1, T // tk), name)


def wgrad_down(p4, df, name):
    nj, T, nb = p4.shape
    D = df.shape[1]
    tk = min(512, T)
    return matmul_tn(p4, df, pl.BlockSpec((None, tk, nb), lambda j, k: (j, k, 0)),
                     pl.BlockSpec((tk, D), lambda j, k: (k, 0)),
                     (nj * nb, D), pl.BlockSpec((nb, D), lambda j, k: (j, 0)), (nb, D), (nj, T // tk), name)


def wgrad_gu(u, dh8, name):
    n8, T, nb = dh8.shape
    D = u.shape[1]
    tk = min(512, T)
    return matmul_tn(u, dh8, pl.BlockSpec((tk, D), lambda j, k: (k, 0)),
                     pl.BlockSpec((None, tk, nb), lambda j, k: (j, k, 0)),
                     (n8, D, nb), pl.BlockSpec((None, D, nb), lambda j, k: (j, 0, 0)), (D, nb), (n8, T // tk), name)


def wgrad_in(u, dproj, name):
    T, D = u.shape
    bw = dproj.shape[1] // N_DEV
    tk = min(512, T)
    return matmul_tn(u, dproj, pl.BlockSpec((tk, D), lambda j, k: (k, 0)),
                     pl.BlockSpec((tk, bw), lambda j, k: (k, j)),
                     (N_DEV, D, bw), pl.BlockSpec((None, D, bw), lambda j, k: (j, 0, 0)), (D, bw),
                     (N_DEV, T // tk), name)


def kernel(x, c, w_ada, b_ada, ffn1_w_gu, ffn1_w_down, ln1_g, ln1_b, w_in, w_sb_out, conv_w, conv_b, conv_ln_g, conv_ln_b, w_conv_out, w_out, ln2_g, ln2_b, ffn2_w_gu, ffn2_w_down, ln3_g, ln3_b, loss_target, m_w_ada, m_b_ada, m_ffn1_w_gu, m_ffn1_w_down, m_ln1_g, m_ln1_b, m_w_in, m_w_sb_out, m_conv_w, m_conv_b, m_conv_ln_g, m_conv_ln_b, m_w_conv_out, m_w_out, m_ln2_g, m_ln2_b, m_ffn2_w_gu, m_ffn2_w_down, m_ln3_g, m_ln3_b, v_w_ada, v_b_ada, v_ffn1_w_gu, v_ffn1_w_down, v_ln1_g, v_ln1_b, v_w_in, v_w_sb_out, v_conv_w, v_conv_b, v_conv_ln_g, v_conv_ln_b, v_w_conv_out, v_w_out, v_ln2_g, v_ln2_b, v_ffn2_w_gu, v_ffn2_w_down, v_ln3_g, v_ln3_b):
    Bl, S, D = x.shape
    T = Bl * S
    kw = conv_w.shape[1]
    ax, ay, ac = lax.axis_index("x"), lax.axis_index("y"), lax.axis_index("c")
    me = 4 * ax + 2 * ay + ac
    qc = jnp.stack([2 * ax + ay, ac]).astype(jnp.int32)

    big = dict(ffn1_w_gu=ffn1_w_gu[0], ffn1_w_down=ffn1_w_down[0], w_in=w_in[0], w_sb_out=w_sb_out[0],
               w_conv_out=w_conv_out[0], w_out=w_out[0], ffn2_w_gu=ffn2_w_gu[0], ffn2_w_down=ffn2_w_down[0])
    big_m = dict(ffn1_w_gu=m_ffn1_w_gu[0], ffn1_w_down=m_ffn1_w_down[0], w_in=m_w_in[0], w_sb_out=m_w_sb_out[0],
                 w_conv_out=m_w_conv_out[0], w_out=m_w_out[0], ffn2_w_gu=m_ffn2_w_gu[0], ffn2_w_down=m_ffn2_w_down[0])
    big_v = dict(ffn1_w_gu=v_ffn1_w_gu[0], ffn1_w_down=v_ffn1_w_down[0], w_in=v_w_in[0], w_sb_out=v_w_sb_out[0],
                 w_conv_out=v_w_conv_out[0], w_out=v_w_out[0], ffn2_w_gu=v_ffn2_w_gu[0], ffn2_w_down=v_ffn2_w_down[0])
    names = list(big)

    gathered = allgather_weights([big[n].astype(BF) for n in names], "allgather_weights")
    G = dict(zip(names, gathered))
    wg1 = G["ffn1_w_gu"]
    wg2 = G["ffn2_w_gu"]
    wd1 = G["ffn1_w_down"].reshape(-1, D)
    wd2 = G["ffn2_w_down"].reshape(-1, D)
    win = jnp.transpose(G["w_in"], (1, 0, 2)).reshape(D, -1)
    wsb = G["w_sb_out"].reshape(D, D)
    wco = G["w_conv_out"].reshape(-1, D)
    wout = G["w_out"].reshape(D, D)

    cw8 = small_allgather(conv_w[0], "allgather_conv_w")
    cw_full = jnp.transpose(cw8, (1, 0, 2)).reshape(kw, D)
    cw32 = jnp.concatenate([cw_full, jnp.zeros((HALO - kw, D), F32)], axis=0)

    c_all = small_allgather(c, "allgather_c").reshape(N_DEV * Bl, D)
    ncol = w_ada.shape[2]
    b_cols = lax.dynamic_slice(b_ada, (0, me * ncol), (1, ncol))
    mod_cols, s_all = ada_fwd(c_all, w_ada[0], b_cols, "ada_fwd")
    mod8 = small_allgather(mod_cols, "allgather_mod")
    mod_mine = lax.dynamic_slice(mod8, (0, me * Bl, 0), (N_DEV, Bl, ncol))
    mod = jnp.transpose(mod_mine, (1, 0, 2)).reshape(Bl, N_MOD_ROWS, 1, D)
    sh1, sc1, g1, sh2, sc2, g2, sh3, sc3, g3 = [mod[:, i] for i in range(N_MOD_ROWS)]

    ones = jnp.ones((1, D), F32)
    zeros = jnp.zeros((1, D), F32)
    xf = x.reshape(T, D)
    tgt = loss_target.reshape(T, D)

    u1, a1, gg1, p1 = ffn_up(xf, ones, zeros, sc1, sh1, wg1, "ffn1_up")
    xh1, rs1, f1 = ffn_down_ln(p1, wd1, xf, ones, zeros, g1, "ffn1_down_ln")
    u2, proj = mod_matmul(xh1, ln1_g, ln1_b, sc2, sh2, win, "in_proj")
    yatt = att_fwd(proj, Bl, S, D, "att_fwd")
    cs, xhc, rsc = conv_fwd(proj, cw32, conv_b, conv_ln_g, conv_ln_b, Bl, S, D, kw, "conv_fwd")
    xh2, rs2, ysb, yco, merged, o2 = mix_fwd(yatt, cs, proj, wsb, wco, wout, xh1, ln1_g, ln1_b, g2, "mix_fwd")
    u3, a3, gg3, p3 = ffn_up(xh2, ln2_g, ln2_b, sc3, sh3, wg2, "ffn2_up")
    xh3, rs3, f3 = ffn_down_ln(p3, wd2, xh2, ln2_g, ln2_b, g3, "ffn2_down_ln")

    dr3, df3, dln3g, dln3b, dg3, lossp = ln_bwd(None, xh3, rs3, ln3_g, ln3_b, g3, f3, MACARON_WEIGHT,
                                                 "ln3_bwd", target=tgt)
    dh3 = swiglu_bwd(df3, wd2, a3, gg3, "ffn2_swiglu_bwd").reshape((-1,) + a3.shape[1:])
    gw_d2 = wgrad_down(p3, df3, "wgrad_ffn2_down")
    gw_g2 = wgrad_gu(u3, dh3, "wgrad_ffn2_gu")
    dx2, dsc3, dsh3 = mod_bwd(dh3, wg2, dr3, xh2, ln2_g, ln2_b, sc3, True, "ffn2_mod_bwd")

    dr2, do2, dln2g, dln2b, dg2 = ln_bwd(dx2, xh2, rs2, ln2_g, ln2_b, g2, o2, 1.0, "ln2_bwd")
    gw_out = wgrad_std(merged, do2, "wgrad_out")
    dysb, dyco, dga, dgb, dyatt, dcs = merge_bwd(do2, proj, ysb, yco, wsb, wco, wout, "merge_bwd")
    gw_sb = wgrad_std(yatt, dysb, "wgrad_sb")
    gw_co = wgrad_std(cs, dyco, "wgrad_conv_out")
    dq, dk, dv = att_bwd(proj, dyatt, Bl, S, D, "att_bwd")
    dconv, dcg, dcbeta, dcb = conv_ln_bwd(dcs, xhc, rsc, conv_ln_g, conv_ln_b, "conv_ln_bwd")
    dglu_a, dglu_b, dcw = conv_bwd(dconv, proj, cw32, Bl, S, D, kw, "conv_bwd")
    dproj = jnp.concatenate([dq, dk, dv, dglu_a, dglu_b, dga, dgb], axis=1)
    gw_in = wgrad_in(u2, dproj, "wgrad_in")
    dx1, dsc2, dsh2 = mod_bwd(dproj, win, dr2, xh1, ln1_g, ln1_b, sc2, False, "mix_mod_bwd")

    dr1, df1, dln1g, dln1b, dg1 = ln_bwd(dx1, xh1, rs1, ln1_g, ln1_b, g1, f1, MACARON_WEIGHT, "ln1_bwd")
    dh1 = swiglu_bwd(df1, wd1, a1, gg1, "ffn1_swiglu_bwd").reshape((-1,) + a1.shape[1:])
    gw_d1 = wgrad_down(p1, df1, "wgrad_ffn1_down")
    gw_g1 = wgrad_gu(u1, dh1, "wgrad_ffn1_gu")
    grad_x, dsc1, dsh1 = mod_bwd(dh1, wg1, dr1, xf, ones, zeros, sc1, True, "ffn1_mod_bwd")

    dmod = jnp.concatenate([dsh1, dsc1, dg1, dsh2, dsc2, dg2, dsh3, dsc3, dg3], axis=1).reshape(Bl, N_MOD_ROWS * D)
    dmod_all = small_allgather(dmod, "allgather_dmod").reshape(N_DEV * Bl, N_MOD_ROWS * D)
    dmod_cols = lax.dynamic_slice(dmod_all, (0, me * ncol), (N_DEV * Bl, ncol))
    g_w_ada, g_b_ada = ada_bwd(s_all, dmod_cols, dmod_all, "ada_bwd")

    npad = 16
    small_rows = [dln1g, dln1b, dcb, dcg, dcbeta, dln2g, dln2b, dln3g, dln3b,
                  jnp.broadcast_to(lossp[0:1, 0:1], (1, D))]
    pack = jnp.concatenate(small_rows + [jnp.zeros((npad - len(small_rows), D), F32), dcw], axis=0)
    small = small_sum(small_allgather(pack, "allgather_small"), "small_sum")
    loss = small[9, 0]
    small_w = [ln1_g, ln1_b, conv_b, conv_ln_g, conv_ln_b, ln2_g, ln2_b, ln3_g, ln3_b]
    small_m = [m_ln1_g, m_ln1_b, m_conv_b, m_conv_ln_g, m_conv_ln_b, m_ln2_g, m_ln2_b, m_ln3_g, m_ln3_b]
    small_v = [v_ln1_g, v_ln1_b, v_conv_b, v_conv_ln_g, v_conv_ln_b, v_ln2_g, v_ln2_b, v_ln3_g, v_ln3_b]
    padrows = jnp.zeros((npad - len(small_w), D), F32)
    s_g, s_d, s_m, s_v = adamw(jnp.concatenate(small_w + [padrows], axis=0),
                               jnp.concatenate(small_m + [padrows], axis=0),
                               jnp.concatenate(small_v + [padrows], axis=0),
                               [_plain_part(small[:npad])], "adamw_small")
    dcw_mine = lax.dynamic_slice(small[npad:npad + kw], (0, me * (D // N_DEV)), (kw, D // N_DEV))
    cw_g, cw_d, cw_m, cw_v = adamw(conv_w[0], m_conv_w[0], v_conv_w[0], [_plain_part(dcw_mine)], "adamw_conv_w")
    ada_g, ada_d, ada_m, ada_v = adamw(w_ada[0], m_w_ada[0], v_w_ada[0], [_plain_part(g_w_ada)], "adamw_w_ada")
    bada_g, bada_d, bada_m, bada_v = adamw(b_ada, m_b_ada, v_b_ada, [_plain_part(g_b_ada)], "adamw_b_ada")

    gfull = dict(ffn1_w_gu=gw_g1, ffn1_w_down=gw_d1, w_in=gw_in, w_sb_out=gw_sb, w_conv_out=gw_co, w_out=gw_out,
                 ffn2_w_gu=gw_g2, ffn2_w_down=gw_d2)
    g42 = [gfull[n].reshape((4, 2) + big[n].shape) for n in names]
    recv_a = exchange_sibling(g42, "rs_sibling")
    sums = [chip_sum(g, r, qc, "chip_sum_" + n) for g, r, n in zip(g42, recv_a, names)]
    recv_b = exchange_chips(sums, "rs_chips")
    res = {}
    for n, s4, rb in zip(names, sums, recv_b):
        parts = [_slot_part(s4, 0), _slot_part(rb, 0), _slot_part(rb, 1), _slot_part(rb, 2)]
        res[n] = adamw(big[n], big_m[n], big_v[n], parts, "adamw_" + n)

    def small_out(k):
        order = dict(ln1_g=0, ln1_b=1, conv_b=2, conv_ln_g=3, conv_ln_b=4, ln2_g=5, ln2_b=6, ln3_g=7, ln3_b=8)
        return lambda arr: arr[order[k]:order[k] + 1]

    weight_order = ["w_ada", "b_ada", "ffn1_w_gu", "ffn1_w_down", "ln1_g", "ln1_b", "w_in", "w_sb_out", "conv_w",
                    "conv_b", "conv_ln_g", "conv_ln_b", "w_conv_out", "w_out", "ln2_g", "ln2_b", "ffn2_w_gu",
                    "ffn2_w_down", "ln3_g", "ln3_b"]

    shapes = dict(w_ada=w_ada.shape, b_ada=b_ada.shape, conv_w=conv_w.shape, ln1_g=ln1_g.shape,
                  **{n: (1,) + big[n].shape for n in names})

    def pick(which):
        outs = []
        for n in weight_order:
            if n == "w_ada":
                a = (ada_g, ada_d, ada_m, ada_v)[which]
            elif n == "b_ada":
                a = (bada_g, bada_d, bada_m, bada_v)[which]
            elif n == "conv_w":
                a = (cw_g, cw_d, cw_m, cw_v)[which]
            elif n in res:
                a = res[n][which]
            else:
                a = small_out(n)((s_g, s_d, s_m, s_v)[which])
            outs.append(a.reshape(shapes.get(n, ln1_g.shape)))
        return outs

    return (loss, grad_x.reshape(Bl, S, D), *pick(0), *pick(1), *pick(2), *pick(3))
```

```python
import functools
import math

import jax
import jax.numpy as jnp
from jax import lax
from jax.experimental import pallas as pl
from jax.experimental.pallas import tpu as pltpu

F32 = jnp.float32
BF = jnp.bfloat16
SDS = jax.ShapeDtypeStruct
MESH = pl.DeviceIdType.MESH

N_DEV = 8
SB_HEAD_DIM = 64
N_MOD_ROWS = 9
MACARON_WEIGHT = 0.5
DEEPNORM_ALPHA = 2.0 ** 0.25
LN_EPS = 1e-5
ADAM_LR = 0.001
ADAM_B1 = 0.9
ADAM_B2 = 0.999
ADAM_EPS = 1e-08
ADAM_WD = 0.01
ADAM_STEP = 10

V7X_VMEM_LIMIT = 52 * 1024 * 1024
LANE = 128
HALO = 32


def _cparams(sem=None):
    return pltpu.CompilerParams(dimension_semantics=sem, vmem_limit_bytes=V7X_VMEM_LIMIT)


def _dot_nn(a, b):
    return lax.dot_general(a, b, (((1,), (0,)), ((), ())), preferred_element_type=F32)


def _dot_nt(a, b):
    return lax.dot_general(a, b, (((1,), (1,)), ((), ())), preferred_element_type=F32)


def _dot_tn(a, b):
    return lax.dot_general(a, b, (((0,), (0,)), ((), ())), preferred_element_type=F32)


def _sig(x):
    return 1.0 / (1.0 + jnp.exp(-x))


def _ln_stats(r):
    mu = jnp.mean(r, axis=-1, keepdims=True)
    d = r - mu
    var = jnp.mean(d * d, axis=-1, keepdims=True)
    rstd = lax.rsqrt(var + LN_EPS)
    return d * rstd, rstd


def _ln_bwd(dxh, xh, rstd):
    m1 = jnp.mean(dxh, axis=-1, keepdims=True)
    m2 = jnp.mean(dxh * xh, axis=-1, keepdims=True)
    return rstd * (dxh - m1 - xh * m2)


def _rowsum(v):
    return jnp.sum(v, axis=0, keepdims=True)


def _row_tile(n, cap):
    if n <= cap:
        return n
    best = None
    for t in range(8, cap + 1, 8):
        if n % t == 0:
            best = t
    assert best is not None, (n, cap)
    return best


def _coords():
    x, y, c = lax.axis_index("x"), lax.axis_index("y"), lax.axis_index("c")
    return x, y, c


def _flip(v, bit):
    return 1 - v if bit else v


def small_allgather(blk, name):
    r, n = blk.shape

    def body(x_ref, out_ref, send_sems, recv_sems):
        x, y, c = _coords()
        me = 4 * x + 2 * y + c
        out_ref[me] = x_ref[...]
        copies = []
        for k in range(1, N_DEV):
            peer = (_flip(x, k & 4), _flip(y, k & 2), _flip(c, k & 1))
            cp = pltpu.make_async_remote_copy(
                src_ref=x_ref, dst_ref=out_ref.at[me], send_sem=send_sems.at[k - 1],
                recv_sem=recv_sems.at[k - 1], device_id=peer, device_id_type=MESH)
            cp.start()
            copies.append(cp)
        for k in range(1, N_DEV):
            px, py, pc = _flip(x, k & 4), _flip(y, k & 2), _flip(c, k & 1)
            slot = 4 * px + 2 * py + pc
            pltpu.make_async_remote_copy(
                src_ref=x_ref, dst_ref=out_ref.at[slot], send_sem=send_sems.at[k - 1],
                recv_sem=recv_sems.at[k - 1], device_id=(px, py, pc), device_id_type=MESH).wait_recv()
        for cp in copies:
            cp.wait_send()

    return pl.pallas_call(
        body, name=name,
        out_shape=SDS((N_DEV, r, n), blk.dtype),
        in_specs=[pl.BlockSpec(memory_space=pltpu.VMEM)],
        out_specs=pl.BlockSpec(memory_space=pltpu.VMEM),
        scratch_shapes=[pltpu.SemaphoreType.DMA((N_DEV - 1,)), pltpu.SemaphoreType.DMA((N_DEV - 1,))],
    )(blk)


def allgather_weights(shards, name):
    n = len(shards)
    per = 7

    def body(*refs):
        ins, outs = refs[:n], refs[n:2 * n]
        send_sems, recv_sems, local_sems = refs[2 * n:]
        x, y, c = _coords()
        me = 4 * x + 2 * y + c
        sibling = (x, y, 1 - c)
        chips = [(1 - x, y), (x, 1 - y), (1 - x, 1 - y)]

        def slot(px, py, pc):
            return 4 * px + 2 * py + pc

        def copy(t, k, block, to, src=None):
            dst = outs[t].at[slot(*block)]
            return pltpu.make_async_remote_copy(
                src_ref=dst if src is None else src, dst_ref=dst,
                send_sem=send_sems.at[per * t + k], recv_sem=recv_sems.at[per * t + k],
                device_id=to, device_id_type=MESH)

        local = []
        first = []
        for t in range(n):
            lc = pltpu.make_async_copy(ins[t], outs[t].at[me], local_sems.at[t])
            lc.start()
            local.append(lc)
            cp = copy(t, 0, (x, y, c), sibling, src=ins[t])
            cp.start()
            first.append(cp)
            for j, chip in enumerate(chips):
                cp = copy(t, 1 + j, (x, y, c), (*chip, c), src=ins[t])
                cp.start()
                first.append(cp)
        passed = []
        for t in range(n):
            for j, chip in enumerate(chips):
                copy(t, 1 + j, (*chip, c), (x, y, c)).wait_recv()
                cp = copy(t, 4 + j, (*chip, c), sibling)
                cp.start()
                passed.append(cp)
        for t in range(n):
            copy(t, 0, (x, y, 1 - c), (x, y, c)).wait_recv()
            for j, chip in enumerate(chips):
                copy(t, 4 + j, (*chip, 1 - c), (x, y, c)).wait_recv()
        for cp in first + passed:
            cp.wait_send()
        for lc in local:
            lc.wait()

    anyspec = pl.BlockSpec(memory_space=pl.ANY)
    return pl.pallas_call(
        body, name=name,
        out_shape=[SDS((N_DEV,) + s.shape, s.dtype) for s in shards],
        in_specs=[anyspec] * n, out_specs=[anyspec] * n,
        scratch_shapes=[pltpu.SemaphoreType.DMA((per * n,)), pltpu.SemaphoreType.DMA((per * n,)),
                        pltpu.SemaphoreType.DMA((n,))],
    )(*shards)


def exchange_sibling(grads, name):
    n = len(grads)

    def body(*refs):
        ins, outs = refs[:n], refs[n:2 * n]
        send_sems, recv_sems = refs[2 * n:]
        x, y, c = _coords()
        copies = []
        for t in range(n):
            cp = pltpu.make_async_remote_copy(
                src_ref=ins[t].at[:, 1 - c], dst_ref=outs[t], send_sem=send_sems.at[t],
                recv_sem=recv_sems.at[t], device_id=(x, y, 1 - c), device_id_type=MESH)
            cp.start()
            copies.append(cp)
        for cp in copies:
            cp.wait_recv()
        for cp in copies:
            cp.wait_send()

    anyspec = pl.BlockSpec(memory_space=pl.ANY)
    return pl.pallas_call(
        body, name=name,
        out_shape=[SDS((4,) + g.shape[2:], g.dtype) for g in grads],
        in_specs=[anyspec] * n, out_specs=[anyspec] * n,
        scratch_shapes=[pltpu.SemaphoreType.DMA((n,)), pltpu.SemaphoreType.DMA((n,))],
    )(*grads)


def exchange_chips(sums, name):
    n = len(sums)

    def body(*refs):
        ins, outs = refs[:n], refs[n:2 * n]
        send_sems, recv_sems = refs[2 * n:]
        x, y, c = _coords()
        copies = []
        for t in range(n):
            for j in range(1, 4):
                peer = (_flip(x, j & 2), _flip(y, j & 1), c)
                cp = pltpu.make_async_remote_copy(
                    src_ref=ins[t].at[j], dst_ref=outs[t].at[j - 1], send_sem=send_sems.at[3 * t + j - 1],
                    recv_sem=recv_sems.at[3 * t + j - 1], device_id=peer, device_id_type=MESH)
                cp.start()
                copies.append(cp)
        for cp in copies:
            cp.wait_recv()
        for cp in copies:
            cp.wait_send()

    anyspec = pl.BlockSpec(memory_space=pl.ANY)
    return pl.pallas_call(
        body, name=name,
        out_shape=[SDS((3,) + s.shape[1:], s.dtype) for s in sums],
        in_specs=[anyspec] * n, out_specs=[anyspec] * n,
        scratch_shapes=[pltpu.SemaphoreType.DMA((3 * n,)), pltpu.SemaphoreType.DMA((3 * n,))],
    )(*sums)


def chip_sum(g42, recv, qc, name):
    _, _, R, C = g42.shape
    tr = _row_tile(R, 256)

    def body(qc_ref, a_ref, b_ref, o_ref):
        o_ref[...] = a_ref[...] + b_ref[...]

    gs = pltpu.PrefetchScalarGridSpec(
        num_scalar_prefetch=1, grid=(4, R // tr),
        in_specs=[pl.BlockSpec((None, None, tr, C), lambda j, i, s: (jnp.bitwise_xor(s[0], j), s[1], i, 0)),
                  pl.BlockSpec((None, tr, C), lambda j, i, s: (jnp.bitwise_xor(s[0], j), i, 0))],
        out_specs=pl.BlockSpec((None, tr, C), lambda j, i, s: (j, i, 0)))
    return pl.pallas_call(body, name=name, grid_spec=gs, out_shape=SDS((4, R, C), F32),
                          compiler_params=_cparams(("arbitrary", "arbitrary")))(qc, g42, recv)


def small_sum(g8, name):
    def body(g_ref, o_ref):
        acc = g_ref[0]
        for k in range(1, N_DEV):
            acc = acc + g_ref[k]
        o_ref[...] = acc
    return pl.pallas_call(body, name=name, out_shape=SDS(g8.shape[1:], F32))(g8)


def adamw(w, m, v, parts, name):
    R, C = w.shape
    tr = _row_tile(R, 256)
    npart = len(parts)
    c1 = 1.0 / (1.0 - ADAM_B1 ** ADAM_STEP)
    c2 = 1.0 / (1.0 - ADAM_B2 ** ADAM_STEP)

    def body(*refs):
        w_ref, m_ref, v_ref = refs[:3]
        p_refs = refs[3:3 + npart]
        g_ref, d_ref, nm_ref, nv_ref = refs[3 + npart:]
        g = p_refs[0][...]
        for p in p_refs[1:]:
            g = g + p[...]
        nm = ADAM_B1 * m_ref[...] + (1.0 - ADAM_B1) * g
        nv = ADAM_B2 * v_ref[...] + (1.0 - ADAM_B2) * (g * g)
        mh = nm * c1
        vh = nv * c2
        g_ref[...] = g
        nm_ref[...] = nm
        nv_ref[...] = nv
        d_ref[...] = -ADAM_LR * (mh / (jnp.sqrt(vh) + ADAM_EPS) + ADAM_WD * w_ref[...])

    wspec = pl.BlockSpec((tr, C), lambda i: (i, 0))
    pspecs = [pl.BlockSpec(bs(tr, C), im) for (_, bs, im) in parts]
    outs = pl.pallas_call(
        body, name=name, grid=(R // tr,),
        in_specs=[wspec] * 3 + pspecs, out_specs=[wspec] * 4,
        out_shape=[SDS((R, C), F32)] * 4,
        compiler_params=_cparams(("parallel",)))(w, m, v, *[p[0] for p in parts])
    return outs


def _plain_part(g):
    return (g, lambda tr, C: (tr, C), lambda i: (i, 0))


def _slot_part(g, slot):
    return (g, lambda tr, C: (None, tr, C), lambda i, s=slot: (s, i, 0))


def ada_fwd(c_all, w_cols, b_cols, name):
    Bg, D = c_all.shape
    n = w_cols.shape[1]

    def body(c_ref, w_ref, b_ref, o_ref, s_ref):
        cc = c_ref[...]
        s = cc * _sig(cc)
        s_ref[...] = s
        o_ref[...] = jnp.dot(s, w_ref[...], preferred_element_type=F32, precision=lax.Precision.HIGHEST) + b_ref[...]

    return pl.pallas_call(body, name=name, out_shape=[SDS((Bg, n), F32), SDS((Bg, D), F32)],
                          compiler_params=_cparams())(c_all, w_cols, b_cols)


def ada_bwd(s_all, dmod_cols, dmod_all, name):
    Bg, D = s_all.shape
    n = dmod_cols.shape[1]

    def body(s_ref, dc_ref, da_ref, gw_ref, gb_ref):
        gw_ref[...] = lax.dot_general(s_ref[...], dc_ref[...], (((0,), (0,)), ((), ())),
                                      preferred_element_type=F32, precision=lax.Precision.HIGHEST)
        acc = da_ref[0:1, :]
        for r in range(1, Bg):
            acc = acc + da_ref[r:r + 1, :]
        gb_ref[...] = acc

    return pl.pallas_call(body, name=name, out_shape=[SDS((D, n), F32), SDS((1, dmod_all.shape[1]), F32)],
                          compiler_params=_cparams())(s_all, dmod_cols, dmod_all)


def _vec(D, rank):
    return pl.BlockSpec((1, D), (lambda i: (0, 0)) if rank == 1 else (lambda i, j: (0, 0)))


def _modspec(D, tpb, rank):
    if rank == 1:
        return pl.BlockSpec((None, 1, D), lambda i: (i // tpb, 0, 0))
    return pl.BlockSpec((None, 1, D), lambda i, j: (i // tpb, 0, 0))


def ffn_up(xs, pg, pb, sc, sh, wg8, name):
    T, D = xs.shape
    n2, _, nb = wg8.shape
    nj = n2 // 2
    S = T // sc.shape[0]
    tm = min(512, S)
    tpb = S // tm

    def body(x_ref, pg_ref, pb_ref, sc_ref, sh_ref, wa_ref, wg_ref, u_ref, a_ref, g_ref, p_ref, u_s):
        @pl.when(pl.program_id(1) == 0)
        def _():
            xin = x_ref[...] * pg_ref[...] + pb_ref[...]
            u = (xin * (1.0 + sc_ref[...]) + sh_ref[...]).astype(BF)
            u_s[...] = u
            u_ref[...] = u
        u = u_s[...]
        a = _dot_nn(u, wa_ref[...])
        g = _dot_nn(u, wg_ref[...])
        a_ref[...] = a.astype(BF)
        g_ref[...] = g.astype(BF)
        p_ref[...] = ((a * _sig(a)) * g).astype(BF)

    blk = pl.BlockSpec((None, tm, nb), lambda i, j: (j, i, 0))
    row = pl.BlockSpec((tm, D), lambda i, j: (i, 0))
    return pl.pallas_call(
        body, name=name, grid=(T // tm, nj),
        in_specs=[row, _vec(D, 2), _vec(D, 2), _modspec(D, tpb, 2), _modspec(D, tpb, 2),
                  pl.BlockSpec((None, D, nb), lambda i, j: (j, 0, 0)),
                  pl.BlockSpec((None, D, nb), lambda i, j: (j + nj, 0, 0))],
        out_specs=[row, blk, blk, blk],
        out_shape=[SDS((T, D), BF)] + [SDS((nj, T, nb), BF)] * 3,
        scratch_shapes=[pltpu.VMEM((tm, D), BF)],
        compiler_params=_cparams(("parallel", "arbitrary")))(xs, pg, pb, sc, sh, wg8, wg8)


def ffn_down_ln(p4, wd, xs, pg, pb, gate, name):
    nj, T, nb = p4.shape
    D = wd.shape[1]
    S = T // gate.shape[0]
    tm = min(512, S)
    tpb = S // tm

    def body(p_ref, wd_ref, x_ref, pg_ref, pb_ref, gate_ref, xh_ref, rs_ref, f_ref, acc):
        k = pl.program_id(1)

        @pl.when(k == 0)
        def _():
            acc[...] = jnp.zeros_like(acc)
        acc[...] += _dot_nn(p_ref[...], wd_ref[...])

        @pl.when(k == nj - 1)
        def _():
            f = acc[...]
            xin = x_ref[...] * pg_ref[...] + pb_ref[...]
            r = DEEPNORM_ALPHA * xin + gate_ref[...] * (MACARON_WEIGHT * f)
            xh, rstd = _ln_stats(r)
            xh_ref[...] = xh
            rs_ref[...] = rstd
            f_ref[...] = f.astype(BF)

    row = pl.BlockSpec((tm, D), lambda i, k: (i, 0))
    return pl.pallas_call(
        body, name=name, grid=(T // tm, nj),
        in_specs=[pl.BlockSpec((None, tm, nb), lambda i, k: (k, i, 0)),
                  pl.BlockSpec((nb, D), lambda i, k: (k, 0)),
                  row, _vec(D, 2), _vec(D, 2), _modspec(D, tpb, 2)],
        out_specs=[row, pl.BlockSpec((tm, 1), lambda i, k: (i, 0)), row],
        out_shape=[SDS((T, D), F32), SDS((T, 1), F32), SDS((T, D), BF)],
        scratch_shapes=[pltpu.VMEM((tm, D), F32)],
        compiler_params=_cparams(("parallel", "arbitrary")))(p4, wd, xs, pg, pb, gate)


def mod_matmul(xs, pg, pb, sc, sh, w, name):
    T, D = xs.shape
    N = w.shape[1]
    S = T // sc.shape[0]
    tm = min(512, S)
    tpb = S // tm
    tn = D

    def body(x_ref, pg_ref, pb_ref, sc_ref, sh_ref, w_ref, u_ref, o_ref, u_s):
        @pl.when(pl.program_id(1) == 0)
        def _():
            xin = x_ref[...] * pg_ref[...] + pb_ref[...]
            u = (xin * (1.0 + sc_ref[...]) + sh_ref[...]).astype(BF)
            u_s[...] = u
            u_ref[...] = u
        o_ref[...] = _dot_nn(u_s[...], w_ref[...])

    row = pl.BlockSpec((tm, D), lambda i, j: (i, 0))
    return pl.pallas_call(
        body, name=name, grid=(T // tm, N // tn),
        in_specs=[row, _vec(D, 2), _vec(D, 2), _modspec(D, tpb, 2), _modspec(D, tpb, 2),
                  pl.BlockSpec((D, tn), lambda i, j: (0, j))],
        out_specs=[row, pl.BlockSpec((tm, tn), lambda i, j: (i, j))],
        out_shape=[SDS((T, D), BF), SDS((T, N), F32)],
        scratch_shapes=[pltpu.VMEM((tm, D), BF)],
        compiler_params=_cparams(("parallel", "arbitrary")))(xs, pg, pb, sc, sh, w)


ATT_TQ = 512
ATT_TK = 256


def _att_consts(tk):
    r = lax.broadcasted_iota(jnp.int32, (tk + 8, tk), 0)
    c = lax.broadcasted_iota(jnp.int32, (tk + 8, tk), 1)
    usum = jnp.where((r >= tk) | (c > r), 1.0, 0.0).astype(BF)
    lsum = jnp.where((r >= tk) | (c < r), 1.0, 0.0).astype(BF)
    dmask = lax.broadcasted_iota(jnp.int32, (tk, tk), 0) < lax.broadcasted_iota(jnp.int32, (tk, tk), 1)
    return usum, lsum, dmask


def _split_dot(m, v):
    hi = v.astype(BF)
    lo = (v - hi.astype(F32)).astype(BF)
    return _dot_nn(m, hi) + _dot_nn(m, lo)


def _softplus(z):
    return jnp.maximum(z, 0.0) + jnp.log(1.0 + jnp.exp(-jnp.abs(z)))


def _att_dims(S, D):
    dh = SB_HEAD_DIM
    cw = min(LANE, D)
    tq = min(ATT_TQ, S)
    tk = min(ATT_TK, tq)
    assert tq % tk == 0 and S % tq == 0
    return dh, cw, cw // dh, D // cw, tq, tk, S // tq, S // tk


def att_fwd(proj, Bl, S, D, name):
    dh, cw, hp, nblk, tq, tk, nq, nk = _att_dims(S, D)
    scale = 1.0 / math.sqrt(dh)
    assert math.log2(scale) == int(math.log2(scale))
    H = D // dh

    def body(q_ref, k_ref, v_ref, o_ref, car_ref, qs, ks, vts):
        usum, _, dmask = _att_consts(tk)
        for hh in range(hp):
            sl = slice(hh * dh, (hh + 1) * dh)
            qs[hh] = (q_ref[:, sl] * scale).astype(BF)
            ks[hh] = k_ref[:, sl].astype(BF)
            for kb in range(nk):
                vts[hh, kb] = v_ref[kb * tk:(kb + 1) * tk, sl].T.astype(BF)
        nch = tq // tk

        def qloop(qb, _):
            qo = pl.multiple_of(qb * tq, tq)
            n_full = qb * nch

            def blk(kb, state, diag):
                ko = pl.multiple_of(kb * tk, tk)
                chains = [(hh, c) for hh in range(hp) for c in range(0 if diag is None else diag, nch)]

                def masked(ch, val):
                    return jnp.where(dmask, val, 0.0) if ch[1] == diag else val

                z = {ch: _dot_nt(ks[ch[0], pl.ds(ko, tk), :], qs[ch[0], pl.ds(pl.multiple_of(qo + ch[1] * tk, tk), tk), :])
                     for ch in chains}
                sp = {ch: _softplus(z[ch]) for ch in chains}
                lk = {ch: masked(ch, -sp[ch]) for ch in chains}
                for hh, c in chains:
                    car_ref[hh, qb * nk + kb, :, c * tk:(c + 1) * tk] = state[hh][c][0]
                cs = {ch: _split_dot(usum, lk[ch]) for ch in chains}
                w = {ch: masked(ch, jnp.exp((z[ch] - sp[ch]) + state[ch[0]][ch[1]][0][0:1, :] + cs[ch][:tk]))
                     for ch in chains}
                pv = {ch: _dot_nn(vts[ch[0], kb], w[ch].astype(BF)) for ch in chains}
                return tuple(tuple(
                    (state[hh][c][0] + cs[(hh, c)][tk:], state[hh][c][1] + pv[(hh, c)]) if (hh, c) in z else state[hh][c]
                    for c in range(nch)) for hh in range(hp))

            state = tuple(tuple((jnp.zeros((8, tk), F32), jnp.zeros((dh, tk), F32)) for _ in range(nch))
                          for _ in range(hp))
            for i in reversed(range(nch)):
                state = blk(n_full + i, state, i)
            state = lax.fori_loop(0, n_full, lambda j, st: blk(n_full - 1 - j, st, None), state)
            for hh in range(hp):
                for c in range(nch):
                    o_ref[pl.ds(pl.multiple_of(qo + c * tk, tk), tk), hh * dh:(hh + 1) * dh] = (
                        state[hh][c][1].T.astype(BF))
            return 0

        lax.fori_loop(0, nq, qloop, 0)

    def seg(s):
        return pl.BlockSpec((S, cw), lambda b, h: (b, s * nblk + h))

    return pl.pallas_call(
        body, name=name, grid=(Bl, nblk),
        in_specs=[seg(0), seg(1), seg(2)],
        out_specs=[pl.BlockSpec((S, cw), lambda b, h: (b, h)),
                   pl.BlockSpec((None, hp, nq * nk, 8, tq), lambda b, h: (b, h, 0, 0, 0))],
        out_shape=[SDS((Bl * S, D), BF), SDS((Bl, H, nq * nk, 8, tq), F32)],
        scratch_shapes=[pltpu.VMEM((hp, S, dh), BF)] * 2 + [pltpu.VMEM((hp, nk, dh, tk), BF)],
        compiler_params=_cparams(("parallel", "parallel")))(proj, proj, proj)


def conv_fwd(proj, cw32, cb, cg, cbeta, Bl, S, D, kw, name):
    T = Bl * S
    ts = min(128, S)
    ns = S // ts
    off = HALO - (kw - 1)
    rc = min(64, ts)
    cw = min(LANE, D)

    def body(a_ref, b_ref, ha_ref, hb_ref, w_ref, cb_ref, g_ref, be_ref, cs_ref, xh_ref, rs_ref, hext, conv_s):
        i = pl.program_id(1)
        hext[pl.ds(HALO, ts), :] = a_ref[...] * _sig(b_ref[...])
        hh = ha_ref[...] * _sig(hb_ref[...])
        hext[pl.ds(0, HALO), :] = jnp.where(i == 0, 0.0, hh)
        for cb_ in range(D // cw):
            cols = slice(cb_ * cw, (cb_ + 1) * cw)
            accs = [jnp.zeros((rc, cw), F32) for _ in range(ts // rc)]
            for k in range(kw):
                wk = w_ref[k:k + 1, cols]
                for r in range(ts // rc):
                    accs[r] = accs[r] + wk * hext[pl.ds(r * rc + off + k, rc), cols]
            for r in range(ts // rc):
                conv_s[pl.ds(r * rc, rc), cols] = accs[r]
        conv = conv_s[...] + cb_ref[...]
        xh, rstd = _ln_stats(conv)
        xh_ref[...] = xh
        rs_ref[...] = rstd
        cl = xh * g_ref[...] + be_ref[...]
        cs_ref[...] = (cl * _sig(cl)).astype(BF)

    hpb = ts // HALO

    def tile(seg):
        return pl.BlockSpec((ts, D), lambda b, i: (b * ns + i, seg))

    def halo(seg):
        return pl.BlockSpec((HALO, D), lambda b, i: (jnp.maximum((b * ns + i) * hpb - 1, 0), seg))

    row = pl.BlockSpec((ts, D), lambda b, i: (b * ns + i, 0))
    vec = pl.BlockSpec((1, D), lambda b, i: (0, 0))
    return pl.pallas_call(
        body, name=name, grid=(Bl, ns),
        in_specs=[tile(3), tile(4), halo(3), halo(4), pl.BlockSpec((HALO, D), lambda b, i: (0, 0)), vec, vec, vec],
        out_specs=[row, row, pl.BlockSpec((ts, 1), lambda b, i: (b * ns + i, 0))],
        out_shape=[SDS((T, D), BF), SDS((T, D), F32), SDS((T, 1), F32)],
        scratch_shapes=[pltpu.VMEM((ts + HALO, D), F32), pltpu.VMEM((ts, D), F32)],
        compiler_params=_cparams(("parallel", "arbitrary")))(proj, proj, proj, proj, cw32, cb, cg, cbeta)


def mix_fwd(yatt, cs, proj, wsb, wco, wout, xs, pg, pb, gate, name):
    T, D = yatt.shape
    S = T // gate.shape[0]
    tm = min(256, S)
    tpb = S // tm

    def body(ya_ref, cs_ref, ga_ref, gb_ref, wsb_ref, wco_ref, wout_ref, x_ref, pg_ref, pb_ref, gate_ref,
             xh_ref, rs_ref, ysb_ref, yco_ref, mg_ref, o_ref):
        ysb = _dot_nn(ya_ref[...], wsb_ref[...])
        yco = _dot_nn(cs_ref[...], wco_ref[...])
        merged = _sig(ga_ref[...]) * ysb + _sig(gb_ref[...]) * yco
        mg = merged.astype(BF)
        o = _dot_nn(mg, wout_ref[...])
        xin = x_ref[...] * pg_ref[...] + pb_ref[...]
        r = DEEPNORM_ALPHA * xin + gate_ref[...] * o
        xh, rstd = _ln_stats(r)
        xh_ref[...] = xh
        rs_ref[...] = rstd
        ysb_ref[...] = ysb.astype(BF)
        yco_ref[...] = yco.astype(BF)
        mg_ref[...] = mg
        o_ref[...] = o.astype(BF)

    row = pl.BlockSpec((tm, D), lambda i: (i, 0))
    wfull = pl.BlockSpec((D, D), lambda i: (0, 0))
    return pl.pallas_call(
        body, name=name, grid=(T // tm,),
        in_specs=[row, row, pl.BlockSpec((tm, D), lambda i: (i, 5)), pl.BlockSpec((tm, D), lambda i: (i, 6)),
                  wfull, wfull, wfull, row, _vec(D, 1), _vec(D, 1), _modspec(D, tpb, 1)],
        out_specs=[row, pl.BlockSpec((tm, 1), lambda i: (i, 0)), row, row, row, row],
        out_shape=[SDS((T, D), F32), SDS((T, 1), F32)] + [SDS((T, D), BF)] * 4,
        compiler_params=_cparams(("parallel",)))(yatt, cs, proj, proj, wsb, wco, wout, xs, pg, pb, gate)


def ln_bwd(dout, xh, rstd, lng, lnb, gate, sub, res_w, name, target=None):
    T, D = xh.shape
    Bl = gate.shape[0]
    S = T // Bl
    tm = min(256, S)
    tpb = S // tm
    first = target is not None

    def body(*refs):
        if first:
            tg_ref, xh_ref, rs_ref, g_ref, b_ref, gate_ref, sub_ref = refs[:7]
            dr_ref, ds_ref, dg_ref, db_ref, dgate_ref, loss_ref = refs[7:]
        else:
            do_ref, xh_ref, rs_ref, g_ref, b_ref, gate_ref, sub_ref = refs[:7]
            dr_ref, ds_ref, dg_ref, db_ref, dgate_ref = refs[7:]
        i = pl.program_id(0)
        xh_ = xh_ref[...]
        if first:
            diff = (xh_ * g_ref[...] + b_ref[...]) - tg_ref[...]
            lsum = jnp.sum(jnp.sum(diff * diff, axis=1, keepdims=True), axis=0, keepdims=True) * (0.5 / D)
            do = diff * (1.0 / D)
        else:
            do = do_ref[...]

        @pl.when(i == 0)
        def _():
            dg_ref[...] = jnp.zeros_like(dg_ref)
            db_ref[...] = jnp.zeros_like(db_ref)
            if first:
                loss_ref[...] = jnp.zeros_like(loss_ref)

        @pl.when(i % tpb == 0)
        def _():
            dgate_ref[...] = jnp.zeros_like(dgate_ref)

        if first:
            loss_ref[...] += jnp.broadcast_to(lsum, loss_ref.shape)
        dg_ref[...] += _rowsum(do * xh_)
        db_ref[...] += _rowsum(do)
        dr = _ln_bwd(do * g_ref[...], xh_, rs_ref[...])
        dr_ref[...] = dr
        ds_ref[...] = (dr * gate_ref[...] * res_w).astype(BF)
        dgate_ref[...] += _rowsum(dr * (res_w * sub_ref[...].astype(F32)))

    row = pl.BlockSpec((tm, D), lambda i: (i, 0))
    vec = _vec(D, 1)
    mod = _modspec(D, tpb, 1)
    out_specs = [row, row, vec, vec, mod]
    out_shape = [SDS((T, D), F32), SDS((T, D), BF), SDS((1, D), F32), SDS((1, D), F32), SDS((Bl, 1, D), F32)]
    if first:
        out_specs.append(pl.BlockSpec((8, LANE), lambda i: (0, 0)))
        out_shape.append(SDS((8, LANE), F32))
    return pl.pallas_call(
        body, name=name, grid=(T // tm,),
        in_specs=[row, row, pl.BlockSpec((tm, 1), lambda i: (i, 0)), vec, vec, mod, row],
        out_specs=out_specs, out_shape=out_shape,
        compiler_params=_cparams(("arbitrary",)))(target if first else dout, xh, rstd, lng, lnb, gate, sub)


def swiglu_bwd(df, wd, a4, g4, name):
    nj, T, nb = a4.shape
    D = df.shape[1]
    tm = min(512, T)

    def body(df_ref, wd_ref, a_ref, g_ref, o_ref):
        dp = _dot_nt(df_ref[...], wd_ref[...])
        a = a_ref[...].astype(F32)
        g = g_ref[...].astype(F32)
        s = _sig(a)
        o_ref[0] = (dp * g * (s * (1.0 + a * (1.0 - s)))).astype(BF)
        o_ref[1] = (dp * (a * s)).astype(BF)

    blk = pl.BlockSpec((None, tm, nb), lambda i, j: (j, i, 0))
    return pl.pallas_call(
        body, name=name, grid=(T // tm, nj),
        in_specs=[pl.BlockSpec((tm, D), lambda i, j: (i, 0)), pl.BlockSpec((nb, D), lambda i, j: (j, 0)), blk, blk],
        out_specs=pl.BlockSpec((2, None, tm, nb), lambda i, j: (0, j, i, 0)),
        out_shape=SDS((2, nj, T, nb), BF),
        compiler_params=_cparams(("parallel", "arbitrary")))(df, wd, a4, g4)


def mod_bwd(dh, w, dr, xs, pg, pb, sc, blocked, name):
    T, D = dr.shape
    Bl = sc.shape[0]
    S = T // Bl
    tm = min(512, S)
    tpb = S // tm
    if blocked:
        nk, _, kb = dh.shape
        dh_spec = pl.BlockSpec((None, tm, kb), lambda i, k: (k, i, 0))
        w_spec = pl.BlockSpec((None, D, kb), lambda i, k: (k, 0, 0))
    else:
        kb = D
        nk = dh.shape[1] // kb
        dh_spec = pl.BlockSpec((tm, kb), lambda i, k: (i, k))
        w_spec = pl.BlockSpec((D, kb), lambda i, k: (0, k))

    def body(dh_ref, w_ref, dr_ref, x_ref, pg_ref, pb_ref, sc_ref, dx_ref, dsc_ref, dsh_ref, acc):
        i = pl.program_id(0)
        k = pl.program_id(1)

        @pl.when(k == 0)
        def _():
            acc[...] = jnp.zeros_like(acc)
        acc[...] += _dot_nt(dh_ref[...], w_ref[...])

        @pl.when((k == nk - 1) & (i % tpb == 0))
        def _():
            dsc_ref[...] = jnp.zeros_like(dsc_ref)
            dsh_ref[...] = jnp.zeros_like(dsh_ref)

        @pl.when(k == nk - 1)
        def _():
            du = acc[...]
            xin = x_ref[...] * pg_ref[...] + pb_ref[...]
            dx_ref[...] = DEEPNORM_ALPHA * dr_ref[...] + du * (1.0 + sc_ref[...])
            dsc_ref[...] += _rowsum(du * xin)
            dsh_ref[...] += _rowsum(du)

    row = pl.BlockSpec((tm, D), lambda i, k: (i, 0))
    mod = _modspec(D, tpb, 2)
    return pl.pallas_call(
        body, name=name, grid=(T // tm, nk),
        in_specs=[dh_spec, w_spec, row, row, _vec(D, 2), _vec(D, 2), mod],
        out_specs=[row, mod, mod],
        out_shape=[SDS((T, D), F32), SDS((Bl, 1, D), F32), SDS((Bl, 1, D), F32)],
        scratch_shapes=[pltpu.VMEM((tm, D), F32)],
        compiler_params=_cparams(("arbitrary", "arbitrary")))(dh, w, dr, xs, pg, pb, sc)


def merge_bwd(do2, proj, ysb, yco, wsb, wco, wout, name):
    T, D = do2.shape
    tm = min(256, T)

    def body(do_ref, ga_ref, gb_ref, ysb_ref, yco_ref, wsb_ref, wco_ref, wout_ref,
             dysb_ref, dyco_ref, dga_ref, dgb_ref, dya_ref, dcs_ref):
        dm = _dot_nt(do_ref[...], wout_ref[...])
        sa = _sig(ga_ref[...])
        sb = _sig(gb_ref[...])
        dysb = (dm * sa).astype(BF)
        dyco = (dm * sb).astype(BF)
        dysb_ref[...] = dysb
        dyco_ref[...] = dyco
        dga_ref[...] = (dm * ysb_ref[...].astype(F32) * (sa * (1.0 - sa))).astype(BF)
        dgb_ref[...] = (dm * yco_ref[...].astype(F32) * (sb * (1.0 - sb))).astype(BF)
        dya_ref[...] = _dot_nt(dysb, wsb_ref[...]).astype(BF)
        dcs_ref[...] = _dot_nt(dyco, wco_ref[...])

    row = pl.BlockSpec((tm, D), lambda i: (i, 0))
    wfull = pl.BlockSpec((D, D), lambda i: (0, 0))
    return pl.pallas_call(
        body, name=name, grid=(T // tm,),
        in_specs=[row, pl.BlockSpec((tm, D), lambda i: (i, 5)), pl.BlockSpec((tm, D), lambda i: (i, 6)),
                  row, row, wfull, wfull, wfull],
        out_specs=[row] * 6,
        out_shape=[SDS((T, D), BF)] * 5 + [SDS((T, D), F32)],
        compiler_params=_cparams(("parallel",)))(do2, proj, proj, ysb, yco, wsb, wco, wout)


def att_bwd(proj, dyatt, car, Bl, S, D, name):
    dh, cw, hp, nblk, tq, tk, nq, nk = _att_dims(S, D)
    scale = 1.0 / math.sqrt(dh)

    def body(q_ref, k_ref, v_ref, do_ref, car_ref, dq_ref, dk_ref, dv_ref, qs, ks, vs, dos, kts, dk_acc, dv_acc):
        usum, lsum, dmask = _att_consts(tk)
        for hh in range(hp):
            sl = slice(hh * dh, (hh + 1) * dh)
            qs[hh] = (q_ref[:, sl] * scale).astype(BF)
            ks[hh] = k_ref[:, sl].astype(BF)
            vs[hh] = v_ref[:, sl].astype(BF)
            dos[hh] = do_ref[:, sl]
            for kb in range(nk):
                kts[hh, kb] = k_ref[kb * tk:(kb + 1) * tk, sl].T.astype(BF)
        dk_acc[...] = jnp.zeros_like(dk_acc)
        dv_acc[...] = jnp.zeros_like(dv_acc)
        nch = tq // tk

        def qloop(qb, _):
            qo = pl.multiple_of(qb * tq, tq)
            n_full = qb * nch

            def blk(kb, state, diag):
                ko = pl.multiple_of(kb * tk, tk)
                chains = [(hh, c) for hh in range(hp) for c in range(0 if diag is None else diag, nch)]

                def masked(ch, val):
                    return jnp.where(dmask, val, 0.0) if ch[1] == diag else val

                def qrows(ref, ch):
                    return ref[ch[0], pl.ds(pl.multiple_of(qo + ch[1] * tk, tk), tk), :]

                k = [ks[hh, pl.ds(ko, tk), :] for hh in range(hp)]
                v = [vs[hh, pl.ds(ko, tk), :] for hh in range(hp)]
                z = {ch: _dot_nt(k[ch[0]], qrows(qs, ch)) for ch in chains}
                dw = {ch: _dot_nt(v[ch[0]], qrows(dos, ch)) for ch in chains}
                sp = {ch: _softplus(z[ch]) for ch in chains}
                lk = {ch: masked(ch, -sp[ch]) for ch in chains}
                cs = {ch: _split_dot(usum, lk[ch]) for ch in chains}
                w = {ch: masked(ch, jnp.exp((z[ch] - sp[ch])
                                            + car_ref[ch[0], qb * nk + kb, 0:1, ch[1] * tk:(ch[1] + 1) * tk]
                                            + cs[ch][:tk])) for ch in chains}
                dlw = {ch: dw[ch] * w[ch] for ch in chains}
                gs = {ch: _split_dot(lsum, dlw[ch]) for ch in chains}
                sg = {ch: jnp.exp(z[ch] - sp[ch]) for ch in chains}
                dzb = {ch: masked(ch, dlw[ch] * (1.0 - sg[ch])
                                  - sg[ch] * (state[ch[0]][ch[1]][0][0:1, :] + gs[ch][:tk])).astype(BF)
                       for ch in chains}
                wb = {ch: w[ch].astype(BF) for ch in chains}
                for hh in range(hp):
                    mine = [ch for ch in chains if ch[0] == hh]
                    dk_acc[hh, kb] += sum(_dot_nn(dzb[ch], qrows(qs, ch)) for ch in mine)
                    dv_acc[hh, kb] += sum(_dot_nn(wb[ch], qrows(dos, ch)) for ch in mine)
                dq = {ch: _dot_nn(kts[ch[0], kb], dzb[ch]) for ch in chains}
                return tuple(tuple(
                    (state[hh][c][0] + gs[(hh, c)][tk:], state[hh][c][1] + dq[(hh, c)]) if (hh, c) in z else state[hh][c]
                    for c in range(nch)) for hh in range(hp))

            state = tuple(tuple((jnp.zeros((8, tk), F32), jnp.zeros((dh, tk), F32)) for _ in range(nch))
                          for _ in range(hp))
            state = lax.fori_loop(0, n_full, lambda kb, st: blk(kb, st, None), state)
            for i in range(nch):
                state = blk(n_full + i, state, i)
            for hh in range(hp):
                for c in range(nch):
                    dq_ref[pl.ds(pl.multiple_of(qo + c * tk, tk), tk), hh * dh:(hh + 1) * dh] = (
                        (state[hh][c][1].T * scale).astype(BF))
            return 0

        lax.fori_loop(0, nq, qloop, 0)
        for hh in range(hp):
            sl = slice(hh * dh, (hh + 1) * dh)
            for kb in range(nk):
                dk_ref[kb * tk:(kb + 1) * tk, sl] = dk_acc[hh, kb].astype(BF)
                dv_ref[kb * tk:(kb + 1) * tk, sl] = dv_acc[hh, kb].astype(BF)

    def seg(s):
        return pl.BlockSpec((S, cw), lambda b, h: (b, s * nblk + h))

    blk_spec = pl.BlockSpec((S, cw), lambda b, h: (b, h))
    return pl.pallas_call(
        body, name=name, grid=(Bl, nblk),
        in_specs=[seg(0), seg(1), seg(2), blk_spec,
                  pl.BlockSpec((None, hp, nq * nk, 8, tq), lambda b, h: (b, h, 0, 0, 0))],
        out_specs=[blk_spec, blk_spec, blk_spec],
        out_shape=[SDS((Bl * S, D), BF)] * 3,
        scratch_shapes=[pltpu.VMEM((hp, S, dh), BF)] * 4 + [pltpu.VMEM((hp, nk, dh, tk), BF)]
        + [pltpu.VMEM((hp, nk, tk, dh), F32)] * 2,
        compiler_params=_cparams(("parallel", "parallel")))(proj, proj, proj, dyatt, car)


def conv_ln_bwd(dcs, xhc, rstd_c, cg, cbeta, name):
    T, D = dcs.shape
    tm = min(256, T)

    def body(dcs_ref, xh_ref, rs_ref, g_ref, b_ref, dconv_ref, dg_ref, db_ref, dcb_ref):
        @pl.when(pl.program_id(0) == 0)
        def _():
            dg_ref[...] = jnp.zeros_like(dg_ref)
            db_ref[...] = jnp.zeros_like(db_ref)
            dcb_ref[...] = jnp.zeros_like(dcb_ref)
        xh = xh_ref[...]
        cl = xh * g_ref[...] + b_ref[...]
        s = _sig(cl)
        dcl = dcs_ref[...] * (s * (1.0 + cl * (1.0 - s)))
        dg_ref[...] += _rowsum(dcl * xh)
        db_ref[...] += _rowsum(dcl)
        dconv = _ln_bwd(dcl * g_ref[...], xh, rs_ref[...])
        dconv_ref[...] = dconv
        dcb_ref[...] += _rowsum(dconv)

    row = pl.BlockSpec((tm, D), lambda i: (i, 0))
    vec = _vec(D, 1)
    return pl.pallas_call(
        body, name=name, grid=(T // tm,),
        in_specs=[row, row, pl.BlockSpec((tm, 1), lambda i: (i, 0)), vec, vec],
        out_specs=[row, vec, vec, vec],
        out_shape=[SDS((T, D), F32)] + [SDS((1, D), F32)] * 3,
        compiler_params=_cparams(("arbitrary",)))(dcs, xhc, rstd_c, cg, cbeta)


def conv_bwd(dconv, proj, cw32, Bl, S, D, kw, name):
    T = Bl * S
    ts = min(128, S)
    ns = S // ts
    off = HALO - (kw - 1)
    rc = min(64, ts)
    cw = min(LANE, D)
    hpb = ts // HALO
    nhb = T // HALO

    def body(dc_ref, dcn_ref, a_ref, b_ref, ha_ref, hb_ref, w_ref, da_ref, db_ref, dw_ref, hext, dext, dh_s):
        b_ = pl.program_id(0)
        i = pl.program_id(1)

        @pl.when((b_ == 0) & (i == 0))
        def _():
            dw_ref[...] = jnp.zeros_like(dw_ref)
        a = a_ref[...]
        sb = _sig(b_ref[...])
        hext[pl.ds(HALO, ts), :] = a * sb
        hh = ha_ref[...] * _sig(hb_ref[...])
        hext[pl.ds(0, HALO), :] = jnp.where(i == 0, 0.0, hh)
        dext[pl.ds(0, ts), :] = dc_ref[...]
        dext[pl.ds(ts, HALO), :] = jnp.where(i == ns - 1, 0.0, dcn_ref[...])
        for cb_ in range(D // cw):
            cols = slice(cb_ * cw, (cb_ + 1) * cw)
            accs = [jnp.zeros((rc, cw), F32) for _ in range(ts // rc)]
            for k in range(kw):
                wk = w_ref[k:k + 1, cols]
                wsum = jnp.zeros((rc, cw), F32)
                for r in range(ts // rc):
                    accs[r] = accs[r] + wk * dext[pl.ds(r * rc + (kw - 1) - k, rc), cols]
                    wsum = wsum + dext[pl.ds(r * rc, rc), cols] * hext[pl.ds(r * rc + off + k, rc), cols]
                dw_ref[k:k + 1, cols] += _rowsum(wsum)
            for r in range(ts // rc):
                dh_s[pl.ds(r * rc, rc), cols] = accs[r]
        dhc = dh_s[...]
        da_ref[...] = (dhc * sb).astype(BF)
        db_ref[...] = (dhc * a * (sb * (1.0 - sb))).astype(BF)

    def tile(seg):
        return pl.BlockSpec((ts, D), lambda b, i: (b * ns + i, seg))

    def halo(seg):
        return pl.BlockSpec((HALO, D), lambda b, i: (jnp.maximum((b * ns + i) * hpb - 1, 0), seg))

    row = pl.BlockSpec((ts, D), lambda b, i: (b * ns + i, 0))
    nxt = pl.BlockSpec((HALO, D), lambda b, i: (jnp.minimum((b * ns + i + 1) * hpb, nhb - 1), 0))
    wspec = pl.BlockSpec((HALO, D), lambda b, i: (0, 0))
    return pl.pallas_call(
        body, name=name, grid=(Bl, ns),
        in_specs=[row, nxt, tile(3), tile(4), halo(3), halo(4), wspec],
        out_specs=[row, row, wspec],
        out_shape=[SDS((T, D), BF), SDS((T, D), BF), SDS((HALO, D), F32)],
        scratch_shapes=[pltpu.VMEM((ts + HALO, D), F32), pltpu.VMEM((ts + HALO, D), F32), pltpu.VMEM((ts, D), F32)],
        compiler_params=_cparams(("arbitrary", "arbitrary")))(dconv, dconv, proj, proj, proj, proj, cw32)


def matmul_tn(xa, ga, x_spec, g_spec, out_shape, out_spec, acc_shape, grid, name):
    nk = grid[-1]

    def body(x_ref, g_ref, o_ref, acc):
        k = pl.program_id(len(grid) - 1)

        @pl.when(k == 0)
        def _():
            acc[...] = jnp.zeros_like(acc)
        acc[...] += _dot_tn(x_ref[...], g_ref[...])

        @pl.when(k == nk - 1)
        def _():
            o_ref[...] = acc[...]

    return pl.pallas_call(
        body, name=name, grid=grid, in_specs=[x_spec, g_spec], out_specs=out_spec,
        out_shape=SDS(out_shape, F32), scratch_shapes=[pltpu.VMEM(acc_shape, F32)],
        compiler_params=_cparams(("parallel",) * (len(grid) - 1) + ("arbitrary",)))(xa, ga)


def wgrad_std(xa, ga, name):
    T, M = xa.shape
    N = ga.shape[1]
    tk = min(512, T)
    return matmul_tn(xa, ga, pl.BlockSpec((tk, M), lambda n, k: (k, 0)), pl.BlockSpec((tk, N), lambda n, k: (k, 0)),
                     (M, N), pl.BlockSpec((M, N), lambda n, k: (0, 0)), (M, N), (1, T // tk), name)


def wgrad_down(p4, df, name):
    nj, T, nb = p4.shape
    D = df.shape[1]
    tk = min(512, T)
    return matmul_tn(p4, df, pl.BlockSpec((None, tk, nb), lambda j, k: (j, k, 0)),
                     pl.BlockSpec((tk, D), lambda j, k: (k, 0)),
                     (nj * nb, D), pl.BlockSpec((nb, D), lambda j, k: (j, 0)), (nb, D), (nj, T // tk), name)


def wgrad_gu(u, dh8, name):
    n8, T, nb = dh8.shape
    D = u.shape[1]
    tk = min(512, T)
    return matmul_tn(u, dh8, pl.BlockSpec((tk, D), lambda j, k: (k, 0)),
                     pl.BlockSpec((None, tk, nb), lambda j, k: (j, k, 0)),
                     (n8, D, nb), pl.BlockSpec((None, D, nb), lambda j, k: (j, 0, 0)), (D, nb), (n8, T // tk), name)


def wgrad_in(u, dproj, name):
    T, D = u.shape
    bw = dproj.shape[1] // N_DEV
    tk = min(512, T)
    return matmul_tn(u, dproj, pl.BlockSpec((tk, D), lambda j, k: (k, 0)),
                     pl.BlockSpec((tk, bw), lambda j, k: (k, j)),
                     (N_DEV, D, bw), pl.BlockSpec((None, D, bw), lambda j, k: (j, 0, 0)), (D, bw),
                     (N_DEV, T // tk), name)


def kernel(x, c, w_ada, b_ada, ffn1_w_gu, ffn1_w_down, ln1_g, ln1_b, w_in, w_sb_out, conv_w, conv_b, conv_ln_g, conv_ln_b, w_conv_out, w_out, ln2_g, ln2_b, ffn2_w_gu, ffn2_w_down, ln3_g, ln3_b, loss_target, m_w_ada, m_b_ada, m_ffn1_w_gu, m_ffn1_w_down, m_ln1_g, m_ln1_b, m_w_in, m_w_sb_out, m_conv_w, m_conv_b, m_conv_ln_g, m_conv_ln_b, m_w_conv_out, m_w_out, m_ln2_g, m_ln2_b, m_ffn2_w_gu, m_ffn2_w_down, m_ln3_g, m_ln3_b, v_w_ada, v_b_ada, v_ffn1_w_gu, v_ffn1_w_down, v_ln1_g, v_ln1_b, v_w_in, v_w_sb_out, v_conv_w, v_conv_b, v_conv_ln_g, v_conv_ln_b, v_w_conv_out, v_w_out, v_ln2_g, v_ln2_b, v_ffn2_w_gu, v_ffn2_w_down, v_ln3_g, v_ln3_b):
    Bl, S, D = x.shape
    T = Bl * S
    kw = conv_w.shape[1]
    ax, ay, ac = lax.axis_index("x"), lax.axis_index("y"), lax.axis_index("c")
    me = 4 * ax + 2 * ay + ac
    qc = jnp.stack([2 * ax + ay, ac]).astype(jnp.int32)

    big = dict(ffn1_w_gu=ffn1_w_gu[0], ffn1_w_down=ffn1_w_down[0], w_in=w_in[0], w_sb_out=w_sb_out[0],
               w_conv_out=w_conv_out[0], w_out=w_out[0], ffn2_w_gu=ffn2_w_gu[0], ffn2_w_down=ffn2_w_down[0])
    big_m = dict(ffn1_w_gu=m_ffn1_w_gu[0], ffn1_w_down=m_ffn1_w_down[0], w_in=m_w_in[0], w_sb_out=m_w_sb_out[0],
                 w_conv_out=m_w_conv_out[0], w_out=m_w_out[0], ffn2_w_gu=m_ffn2_w_gu[0], ffn2_w_down=m_ffn2_w_down[0])
    big_v = dict(ffn1_w_gu=v_ffn1_w_gu[0], ffn1_w_down=v_ffn1_w_down[0], w_in=v_w_in[0], w_sb_out=v_w_sb_out[0],
                 w_conv_out=v_w_conv_out[0], w_out=v_w_out[0], ffn2_w_gu=v_ffn2_w_gu[0], ffn2_w_down=v_ffn2_w_down[0])
    names = list(big)

    gathered = allgather_weights([big[n].astype(BF) for n in names], "allgather_weights")
    G = dict(zip(names, gathered))
    wg1 = G["ffn1_w_gu"]
    wg2 = G["ffn2_w_gu"]
    wd1 = G["ffn1_w_down"].reshape(-1, D)
    wd2 = G["ffn2_w_down"].reshape(-1, D)
    win = jnp.transpose(G["w_in"], (1, 0, 2)).reshape(D, -1)
    wsb = G["w_sb_out"].reshape(D, D)
    wco = G["w_conv_out"].reshape(-1, D)
    wout = G["w_out"].reshape(D, D)

    cw8 = small_allgather(conv_w[0], "allgather_conv_w")
    cw_full = jnp.transpose(cw8, (1, 0, 2)).reshape(kw, D)
    cw32 = jnp.concatenate([cw_full, jnp.zeros((HALO - kw, D), F32)], axis=0)

    c_all = small_allgather(c, "allgather_c").reshape(N_DEV * Bl, D)
    ncol = w_ada.shape[2]
    b_cols = lax.dynamic_slice(b_ada, (0, me * ncol), (1, ncol))
    mod_cols, s_all = ada_fwd(c_all, w_ada[0], b_cols, "ada_fwd")
    mod8 = small_allgather(mod_cols, "allgather_mod")
    mod_mine = lax.dynamic_slice(mod8, (0, me * Bl, 0), (N_DEV, Bl, ncol))
    mod = jnp.transpose(mod_mine, (1, 0, 2)).reshape(Bl, N_MOD_ROWS, 1, D)
    sh1, sc1, g1, sh2, sc2, g2, sh3, sc3, g3 = [mod[:, i] for i in range(N_MOD_ROWS)]

    ones = jnp.ones((1, D), F32)
    zeros = jnp.zeros((1, D), F32)
    xf = x.reshape(T, D)
    tgt = loss_target.reshape(T, D)

    u1, a1, gg1, p1 = ffn_up(xf, ones, zeros, sc1, sh1, wg1, "ffn1_up")
    xh1, rs1, f1 = ffn_down_ln(p1, wd1, xf, ones, zeros, g1, "ffn1_down_ln")
    u2, proj = mod_matmul(xh1, ln1_g, ln1_b, sc2, sh2, win, "in_proj")
    yatt, car = att_fwd(proj, Bl, S, D, "att_fwd")
    cs, xhc, rsc = conv_fwd(proj, cw32, conv_b, conv_ln_g, conv_ln_b, Bl, S, D, kw, "conv_fwd")
    xh2, rs2, ysb, yco, merged, o2 = mix_fwd(yatt, cs, proj, wsb, wco, wout, xh1, ln1_g, ln1_b, g2, "mix_fwd")
    u3, a3, gg3, p3 = ffn_up(xh2, ln2_g, ln2_b, sc3, sh3, wg2, "ffn2_up")
    xh3, rs3, f3 = ffn_down_ln(p3, wd2, xh2, ln2_g, ln2_b, g3, "ffn2_down_ln")

    dr3, df3, dln3g, dln3b, dg3, lossp = ln_bwd(None, xh3, rs3, ln3_g, ln3_b, g3, f3, MACARON_WEIGHT,
                                                 "ln3_bwd", target=tgt)
    dh3 = swiglu_bwd(df3, wd2, a3, gg3, "ffn2_swiglu_bwd").reshape((-1,) + a3.shape[1:])
    gw_d2 = wgrad_down(p3, df3, "wgrad_ffn2_down")
    gw_g2 = wgrad_gu(u3, dh3, "wgrad_ffn2_gu")
    dx2, dsc3, dsh3 = mod_bwd(dh3, wg2, dr3, xh2, ln2_g, ln2_b, sc3, True, "ffn2_mod_bwd")

    dr2, do2, dln2g, dln2b, dg2 = ln_bwd(dx2, xh2, rs2, ln2_g, ln2_b, g2, o2, 1.0, "ln2_bwd")
    gw_out = wgrad_std(merged, do2, "wgrad_out")
    dysb, dyco, dga, dgb, dyatt, dcs = merge_bwd(do2, proj, ysb, yco, wsb, wco, wout, "merge_bwd")
    gw_sb = wgrad_std(yatt, dysb, "wgrad_sb")
    gw_co = wgrad_std(cs, dyco, "wgrad_conv_out")
    dq, dk, dv = att_bwd(proj, dyatt, car, Bl, S, D, "att_bwd")
    dconv, dcg, dcbeta, dcb = conv_ln_bwd(dcs, xhc, rsc, conv_ln_g, conv_ln_b, "conv_ln_bwd")
    dglu_a, dglu_b, dcw = conv_bwd(dconv, proj, cw32, Bl, S, D, kw, "conv_bwd")
    dproj = jnp.concatenate([dq, dk, dv, dglu_a, dglu_b, dga, dgb], axis=1)
    gw_in = wgrad_in(u2, dproj, "wgrad_in")
    dx1, dsc2, dsh2 = mod_bwd(dproj, win, dr2, xh1, ln1_g, ln1_b, sc2, False, "mix_mod_bwd")

    dr1, df1, dln1g, dln1b, dg1 = ln_bwd(dx1, xh1, rs1, ln1_g, ln1_b, g1, f1, MACARON_WEIGHT, "ln1_bwd")
    dh1 = swiglu_bwd(df1, wd1, a1, gg1, "ffn1_swiglu_bwd").reshape((-1,) + a1.shape[1:])
    gw_d1 = wgrad_down(p1, df1, "wgrad_ffn1_down")
    gw_g1 = wgrad_gu(u1, dh1, "wgrad_ffn1_gu")
    grad_x, dsc1, dsh1 = mod_bwd(dh1, wg1, dr1, xf, ones, zeros, sc1, True, "ffn1_mod_bwd")

    dmod = jnp.concatenate([dsh1, dsc1, dg1, dsh2, dsc2, dg2, dsh3, dsc3, dg3], axis=1).reshape(Bl, N_MOD_ROWS * D)
    dmod_all = small_allgather(dmod, "allgather_dmod").reshape(N_DEV * Bl, N_MOD_ROWS * D)
    dmod_cols = lax.dynamic_slice(dmod_all, (0, me * ncol), (N_DEV * Bl, ncol))
    g_w_ada, g_b_ada = ada_bwd(s_all, dmod_cols, dmod_all, "ada_bwd")

    npad = 16
    small_rows = [dln1g, dln1b, dcb, dcg, dcbeta, dln2g, dln2b, dln3g, dln3b,
                  jnp.broadcast_to(lossp[0:1, 0:1], (1, D))]
    pack = jnp.concatenate(small_rows + [jnp.zeros((npad - len(small_rows), D), F32), dcw], axis=0)
    small = small_sum(small_allgather(pack, "allgather_small"), "small_sum")
    loss = small[9, 0]
    small_w = [ln1_g, ln1_b, conv_b, conv_ln_g, conv_ln_b, ln2_g, ln2_b, ln3_g, ln3_b]
    small_m = [m_ln1_g, m_ln1_b, m_conv_b, m_conv_ln_g, m_conv_ln_b, m_ln2_g, m_ln2_b, m_ln3_g, m_ln3_b]
    small_v = [v_ln1_g, v_ln1_b, v_conv_b, v_conv_ln_g, v_conv_ln_b, v_ln2_g, v_ln2_b, v_ln3_g, v_ln3_b]
    padrows = jnp.zeros((npad - len(small_w), D), F32)
    s_g, s_d, s_m, s_v = adamw(jnp.concatenate(small_w + [padrows], axis=0),
                               jnp.concatenate(small_m + [padrows], axis=0),
                               jnp.concatenate(small_v + [padrows], axis=0),
                               [_plain_part(small[:npad])], "adamw_small")
    dcw_mine = lax.dynamic_slice(small[npad:npad + kw], (0, me * (D // N_DEV)), (kw, D // N_DEV))
    cw_g, cw_d, cw_m, cw_v = adamw(conv_w[0], m_conv_w[0], v_conv_w[0], [_plain_part(dcw_mine)], "adamw_conv_w")
    ada_g, ada_d, ada_m, ada_v = adamw(w_ada[0], m_w_ada[0], v_w_ada[0], [_plain_part(g_w_ada)], "adamw_w_ada")
    bada_g, bada_d, bada_m, bada_v = adamw(b_ada, m_b_ada, v_b_ada, [_plain_part(g_b_ada)], "adamw_b_ada")

    gfull = dict(ffn1_w_gu=gw_g1, ffn1_w_down=gw_d1, w_in=gw_in, w_sb_out=gw_sb, w_conv_out=gw_co, w_out=gw_out,
                 ffn2_w_gu=gw_g2, ffn2_w_down=gw_d2)
    g42 = [gfull[n].reshape((4, 2) + big[n].shape) for n in names]
    recv_a = exchange_sibling(g42, "rs_sibling")
    sums = [chip_sum(g, r, qc, "chip_sum_" + n) for g, r, n in zip(g42, recv_a, names)]
    recv_b = exchange_chips(sums, "rs_chips")
    res = {}
    for n, s4, rb in zip(names, sums, recv_b):
        parts = [_slot_part(s4, 0), _slot_part(rb, 0), _slot_part(rb, 1), _slot_part(rb, 2)]
        res[n] = adamw(big[n], big_m[n], big_v[n], parts, "adamw_" + n)

    def small_out(k):
        order = dict(ln1_g=0, ln1_b=1, conv_b=2, conv_ln_g=3, conv_ln_b=4, ln2_g=5, ln2_b=6, ln3_g=7, ln3_b=8)
        return lambda arr: arr[order[k]:order[k] + 1]

    weight_order = ["w_ada", "b_ada", "ffn1_w_gu", "ffn1_w_down", "ln1_g", "ln1_b", "w_in", "w_sb_out", "conv_w",
                    "conv_b", "conv_ln_g", "conv_ln_b", "w_conv_out", "w_out", "ln2_g", "ln2_b", "ffn2_w_gu",
                    "ffn2_w_down", "ln3_g", "ln3_b"]

    shapes = dict(w_ada=w_ada.shape, b_ada=b_ada.shape, conv_w=conv_w.shape, ln1_g=ln1_g.shape,
                  **{n: (1,) + big[n].shape for n in names})

    def pick(which):
        outs = []
        for n in weight_order:
            if n == "w_ada":
                a = (ada_g, ada_d, ada_m, ada_v)[which]
            elif n == "b_ada":
                a = (bada_g, bada_d, bada_m, bada_v)[which]
            elif n == "conv_w":
                a = (cw_g, cw_d, cw_m, cw_v)[which]
            elif n in res:
                a = res[n][which]
            else:
                a = small_out(n)((s_g, s_d, s_m, s_v)[which])
            outs.append(a.reshape(shapes.get(n, ln1_g.shape)))
        return outs

    return (loss, grad_x.reshape(Bl, S, D), *pick(0), *pick(1), *pick(2), *pick(3))
```

```python
import functools
import math

import jax
import jax.numpy as jnp
from jax import lax
from jax.experimental import pallas as pl
from jax.experimental.pallas import tpu as pltpu

F32 = jnp.float32
BF = jnp.bfloat16
SDS = jax.ShapeDtypeStruct
MESH = pl.DeviceIdType.MESH

N_DEV = 8
SB_HEAD_DIM = 64
N_MOD_ROWS = 9
MACARON_WEIGHT = 0.5
DEEPNORM_ALPHA = 2.0 ** 0.25
LN_EPS = 1e-5
ADAM_LR = 0.001
ADAM_B1 = 0.9
ADAM_B2 = 0.999
ADAM_EPS = 1e-08
ADAM_WD = 0.01
ADAM_STEP = 10

V7X_VMEM_LIMIT = 52 * 1024 * 1024
LANE = 128
HALO = 32


def _cparams(sem=None):
    return pltpu.CompilerParams(dimension_semantics=sem, vmem_limit_bytes=V7X_VMEM_LIMIT)


def _dot_nn(a, b):
    return lax.dot_general(a, b, (((1,), (0,)), ((), ())), preferred_element_type=F32)


def _dot_nt(a, b):
    return lax.dot_general(a, b, (((1,), (1,)), ((), ())), preferred_element_type=F32)


def _dot_tn(a, b):
    return lax.dot_general(a, b, (((0,), (0,)), ((), ())), preferred_element_type=F32)


def _sig(x):
    return 1.0 / (1.0 + jnp.exp(-x))


def _ln_stats(r):
    mu = jnp.mean(r, axis=-1, keepdims=True)
    d = r - mu
    var = jnp.mean(d * d, axis=-1, keepdims=True)
    rstd = lax.rsqrt(var + LN_EPS)
    return d * rstd, rstd


def _ln_bwd(dxh, xh, rstd):
    m1 = jnp.mean(dxh, axis=-1, keepdims=True)
    m2 = jnp.mean(dxh * xh, axis=-1, keepdims=True)
    return rstd * (dxh - m1 - xh * m2)


def _rowsum(v):
    return jnp.sum(v, axis=0, keepdims=True)


def _row_tile(n, cap):
    if n <= cap:
        return n
    best = None
    for t in range(8, cap + 1, 8):
        if n % t == 0:
            best = t
    assert best is not None, (n, cap)
    return best


def _coords():
    x, y, c = lax.axis_index("x"), lax.axis_index("y"), lax.axis_index("c")
    return x, y, c


def _flip(v, bit):
    return 1 - v if bit else v


def small_allgather(blk, name):
    r, n = blk.shape

    def body(x_ref, out_ref, send_sems, recv_sems):
        x, y, c = _coords()
        me = 4 * x + 2 * y + c
        out_ref[me] = x_ref[...]
        copies = []
        for k in range(1, N_DEV):
            peer = (_flip(x, k & 4), _flip(y, k & 2), _flip(c, k & 1))
            cp = pltpu.make_async_remote_copy(
                src_ref=x_ref, dst_ref=out_ref.at[me], send_sem=send_sems.at[k - 1],
                recv_sem=recv_sems.at[k - 1], device_id=peer, device_id_type=MESH)
            cp.start()
            copies.append(cp)
        for k in range(1, N_DEV):
            px, py, pc = _flip(x, k & 4), _flip(y, k & 2), _flip(c, k & 1)
            slot = 4 * px + 2 * py + pc
            pltpu.make_async_remote_copy(
                src_ref=x_ref, dst_ref=out_ref.at[slot], send_sem=send_sems.at[k - 1],
                recv_sem=recv_sems.at[k - 1], device_id=(px, py, pc), device_id_type=MESH).wait_recv()
        for cp in copies:
            cp.wait_send()

    return pl.pallas_call(
        body, name=name,
        out_shape=SDS((N_DEV, r, n), blk.dtype),
        in_specs=[pl.BlockSpec(memory_space=pltpu.VMEM)],
        out_specs=pl.BlockSpec(memory_space=pltpu.VMEM),
        scratch_shapes=[pltpu.SemaphoreType.DMA((N_DEV - 1,)), pltpu.SemaphoreType.DMA((N_DEV - 1,))],
    )(blk)


def allgather_weights(shards, name):
    n = len(shards)
    per = 7

    def body(*refs):
        ins, outs = refs[:n], refs[n:2 * n]
        send_sems, recv_sems, local_sems = refs[2 * n:]
        x, y, c = _coords()
        me = 4 * x + 2 * y + c
        sibling = (x, y, 1 - c)
        chips = [(1 - x, y), (x, 1 - y), (1 - x, 1 - y)]

        def slot(px, py, pc):
            return 4 * px + 2 * py + pc

        def copy(t, k, block, to, src=None):
            dst = outs[t].at[slot(*block)]
            return pltpu.make_async_remote_copy(
                src_ref=dst if src is None else src, dst_ref=dst,
                send_sem=send_sems.at[per * t + k], recv_sem=recv_sems.at[per * t + k],
                device_id=to, device_id_type=MESH)

        local = []
        first = []
        for t in range(n):
            lc = pltpu.make_async_copy(ins[t], outs[t].at[me], local_sems.at[t])
            lc.start()
            local.append(lc)
            cp = copy(t, 0, (x, y, c), sibling, src=ins[t])
            cp.start()
            first.append(cp)
            for j, chip in enumerate(chips):
                cp = copy(t, 1 + j, (x, y, c), (*chip, c), src=ins[t])
                cp.start()
                first.append(cp)
        passed = []
        for t in range(n):
            for j, chip in enumerate(chips):
                copy(t, 1 + j, (*chip, c), (x, y, c)).wait_recv()
                cp = copy(t, 4 + j, (*chip, c), sibling)
                cp.start()
                passed.append(cp)
        for t in range(n):
            copy(t, 0, (x, y, 1 - c), (x, y, c)).wait_recv()
            for j, chip in enumerate(chips):
                copy(t, 4 + j, (*chip, 1 - c), (x, y, c)).wait_recv()
        for cp in first + passed:
            cp.wait_send()
        for lc in local:
            lc.wait()

    anyspec = pl.BlockSpec(memory_space=pl.ANY)
    return pl.pallas_call(
        body, name=name,
        out_shape=[SDS((N_DEV,) + s.shape, s.dtype) for s in shards],
        in_specs=[anyspec] * n, out_specs=[anyspec] * n,
        scratch_shapes=[pltpu.SemaphoreType.DMA((per * n,)), pltpu.SemaphoreType.DMA((per * n,)),
                        pltpu.SemaphoreType.DMA((n,))],
    )(*shards)


def exchange_sibling(grads, name):
    n = len(grads)

    def body(*refs):
        ins, outs = refs[:n], refs[n:2 * n]
        send_sems, recv_sems = refs[2 * n:]
        x, y, c = _coords()
        copies = []
        for t in range(n):
            cp = pltpu.make_async_remote_copy(
                src_ref=ins[t].at[:, 1 - c], dst_ref=outs[t], send_sem=send_sems.at[t],
                recv_sem=recv_sems.at[t], device_id=(x, y, 1 - c), device_id_type=MESH)
            cp.start()
            copies.append(cp)
        for cp in copies:
            cp.wait_recv()
        for cp in copies:
            cp.wait_send()

    anyspec = pl.BlockSpec(memory_space=pl.ANY)
    return pl.pallas_call(
        body, name=name,
        out_shape=[SDS((4,) + g.shape[2:], g.dtype) for g in grads],
        in_specs=[anyspec] * n, out_specs=[anyspec] * n,
        scratch_shapes=[pltpu.SemaphoreType.DMA((n,)), pltpu.SemaphoreType.DMA((n,))],
    )(*grads)


def exchange_chips(sums, name):
    n = len(sums)

    def body(*refs):
        ins, outs = refs[:n], refs[n:2 * n]
        send_sems, recv_sems = refs[2 * n:]
        x, y, c = _coords()
        copies = []
        for t in range(n):
            for j in range(1, 4):
                peer = (_flip(x, j & 2), _flip(y, j & 1), c)
                cp = pltpu.make_async_remote_copy(
                    src_ref=ins[t].at[j - 1], dst_ref=outs[t].at[j - 1], send_sem=send_sems.at[3 * t + j - 1],
                    recv_sem=recv_sems.at[3 * t + j - 1], device_id=peer, device_id_type=MESH)
                cp.start()
                copies.append(cp)
        for cp in copies:
            cp.wait_recv()
        for cp in copies:
            cp.wait_send()

    anyspec = pl.BlockSpec(memory_space=pl.ANY)
    return pl.pallas_call(
        body, name=name,
        out_shape=[SDS(s.shape, s.dtype) for s in sums],
        in_specs=[anyspec] * n, out_specs=[anyspec] * n,
        scratch_shapes=[pltpu.SemaphoreType.DMA((3 * n,)), pltpu.SemaphoreType.DMA((3 * n,))],
    )(*sums)


def chip_sum(g42, recv, qc, name):
    _, _, R, C = g42.shape
    tr = _row_tile(R, 256)

    def body(qc_ref, a_ref, b_ref, own_ref, send_ref):
        j = pl.program_id(1)
        s = a_ref[...] + b_ref[...]

        @pl.when(j == 0)
        def _():
            own_ref[...] = s

        @pl.when(j > 0)
        def _():
            send_ref[...] = s.astype(BF)

    gs = pltpu.PrefetchScalarGridSpec(
        num_scalar_prefetch=1, grid=(R // tr, 4),
        in_specs=[pl.BlockSpec((None, None, tr, C), lambda i, j, s: (jnp.bitwise_xor(s[0], j), s[1], i, 0)),
                  pl.BlockSpec((None, tr, C), lambda i, j, s: (jnp.bitwise_xor(s[0], j), i, 0))],
        out_specs=[pl.BlockSpec((tr, C), lambda i, j, s: (i, 0)),
                   pl.BlockSpec((None, tr, C), lambda i, j, s: (jnp.maximum(j - 1, 0), i, 0))])
    return pl.pallas_call(body, name=name, grid_spec=gs, out_shape=[SDS((R, C), F32), SDS((3, R, C), BF)],
                          compiler_params=_cparams(("arbitrary", "arbitrary")))(qc, g42, recv)


def small_sum(g8, name):
    def body(g_ref, o_ref):
        acc = g_ref[0]
        for k in range(1, N_DEV):
            acc = acc + g_ref[k]
        o_ref[...] = acc
    return pl.pallas_call(body, name=name, out_shape=SDS(g8.shape[1:], F32))(g8)


def adamw(w, m, v, parts, name):
    R, C = w.shape
    tr = _row_tile(R, 256)
    npart = len(parts)
    c1 = 1.0 / (1.0 - ADAM_B1 ** ADAM_STEP)
    c2 = 1.0 / (1.0 - ADAM_B2 ** ADAM_STEP)

    def body(*refs):
        w_ref, m_ref, v_ref = refs[:3]
        p_refs = refs[3:3 + npart]
        g_ref, d_ref, nm_ref, nv_ref = refs[3 + npart:]
        g = p_refs[0][...].astype(F32)
        for p in p_refs[1:]:
            g = g + p[...].astype(F32)
        nm = ADAM_B1 * m_ref[...] + (1.0 - ADAM_B1) * g
        nv = ADAM_B2 * v_ref[...] + (1.0 - ADAM_B2) * (g * g)
        mh = nm * c1
        vh = nv * c2
        g_ref[...] = g
        nm_ref[...] = nm
        nv_ref[...] = nv
        d_ref[...] = -ADAM_LR * (mh / (jnp.sqrt(vh) + ADAM_EPS) + ADAM_WD * w_ref[...])

    wspec = pl.BlockSpec((tr, C), lambda i: (i, 0))
    pspecs = [pl.BlockSpec(bs(tr, C), im) for (_, bs, im) in parts]
    outs = pl.pallas_call(
        body, name=name, grid=(R // tr,),
        in_specs=[wspec] * 3 + pspecs, out_specs=[wspec] * 4,
        out_shape=[SDS((R, C), F32)] * 4,
        compiler_params=_cparams(("parallel",)))(w, m, v, *[p[0] for p in parts])
    return outs


def _plain_part(g):
    return (g, lambda tr, C: (tr, C), lambda i: (i, 0))


def _slot_part(g, slot):
    return (g, lambda tr, C: (None, tr, C), lambda i, s=slot: (s, i, 0))


def ada_fwd(c_all, w_cols, b_cols, name):
    Bg, D = c_all.shape
    n = w_cols.shape[1]

    def body(c_ref, w_ref, b_ref, o_ref, s_ref):
        cc = c_ref[...]
        s = cc * _sig(cc)
        s_ref[...] = s
        o_ref[...] = jnp.dot(s, w_ref[...], preferred_element_type=F32, precision=lax.Precision.HIGHEST) + b_ref[...]

    return pl.pallas_call(body, name=name, out_shape=[SDS((Bg, n), F32), SDS((Bg, D), F32)],
                          compiler_params=_cparams())(c_all, w_cols, b_cols)


def ada_bwd(s_all, dmod_cols, dmod_all, name):
    Bg, D = s_all.shape
    n = dmod_cols.shape[1]

    def body(s_ref, dc_ref, da_ref, gw_ref, gb_ref):
        gw_ref[...] = lax.dot_general(s_ref[...], dc_ref[...], (((0,), (0,)), ((), ())),
                                      preferred_element_type=F32, precision=lax.Precision.HIGHEST)
        acc = da_ref[0:1, :]
        for r in range(1, Bg):
            acc = acc + da_ref[r:r + 1, :]
        gb_ref[...] = acc

    return pl.pallas_call(body, name=name, out_shape=[SDS((D, n), F32), SDS((1, dmod_all.shape[1]), F32)],
                          compiler_params=_cparams())(s_all, dmod_cols, dmod_all)


def _vec(D, rank):
    return pl.BlockSpec((1, D), (lambda i: (0, 0)) if rank == 1 else (lambda i, j: (0, 0)))


def _modspec(D, tpb, rank):
    if rank == 1:
        return pl.BlockSpec((None, 1, D), lambda i: (i // tpb, 0, 0))
    return pl.BlockSpec((None, 1, D), lambda i, j: (i // tpb, 0, 0))


def ffn_up(xs, pg, pb, sc, sh, wg8, name):
    T, D = xs.shape
    n2, _, nb = wg8.shape
    nj = n2 // 2
    S = T // sc.shape[0]
    tm = min(512, S)
    tpb = S // tm

    def body(x_ref, pg_ref, pb_ref, sc_ref, sh_ref, wa_ref, wg_ref, u_ref, a_ref, g_ref, p_ref, u_s):
        @pl.when(pl.program_id(1) == 0)
        def _():
            xin = x_ref[...] * pg_ref[...] + pb_ref[...]
            u = (xin * (1.0 + sc_ref[...]) + sh_ref[...]).astype(BF)
            u_s[...] = u
            u_ref[...] = u
        u = u_s[...]
        a = _dot_nn(u, wa_ref[...])
        g = _dot_nn(u, wg_ref[...])
        a_ref[...] = a.astype(BF)
        g_ref[...] = g.astype(BF)
        p_ref[...] = ((a * _sig(a)) * g).astype(BF)

    blk = pl.BlockSpec((None, tm, nb), lambda i, j: (j, i, 0))
    row = pl.BlockSpec((tm, D), lambda i, j: (i, 0))
    return pl.pallas_call(
        body, name=name, grid=(T // tm, nj),
        in_specs=[row, _vec(D, 2), _vec(D, 2), _modspec(D, tpb, 2), _modspec(D, tpb, 2),
                  pl.BlockSpec((None, D, nb), lambda i, j: (j, 0, 0)),
                  pl.BlockSpec((None, D, nb), lambda i, j: (j + nj, 0, 0))],
        out_specs=[row, blk, blk, blk],
        out_shape=[SDS((T, D), BF)] + [SDS((nj, T, nb), BF)] * 3,
        scratch_shapes=[pltpu.VMEM((tm, D), BF)],
        compiler_params=_cparams(("parallel", "arbitrary")))(xs, pg, pb, sc, sh, wg8, wg8)


def ffn_down_ln(p4, wd, xs, pg, pb, gate, name):
    nj, T, nb = p4.shape
    D = wd.shape[1]
    S = T // gate.shape[0]
    tm = min(512, S)
    tpb = S // tm

    def body(p_ref, wd_ref, x_ref, pg_ref, pb_ref, gate_ref, xh_ref, rs_ref, f_ref, acc):
        k = pl.program_id(1)

        @pl.when(k == 0)
        def _():
            acc[...] = jnp.zeros_like(acc)
        acc[...] += _dot_nn(p_ref[...], wd_ref[...])

        @pl.when(k == nj - 1)
        def _():
            f = acc[...]
            xin = x_ref[...] * pg_ref[...] + pb_ref[...]
            r = DEEPNORM_ALPHA * xin + gate_ref[...] * (MACARON_WEIGHT * f)
            xh, rstd = _ln_stats(r)
            xh_ref[...] = xh
            rs_ref[...] = rstd
            f_ref[...] = f.astype(BF)

    row = pl.BlockSpec((tm, D), lambda i, k: (i, 0))
    return pl.pallas_call(
        body, name=name, grid=(T // tm, nj),
        in_specs=[pl.BlockSpec((None, tm, nb), lambda i, k: (k, i, 0)),
                  pl.BlockSpec((nb, D), lambda i, k: (k, 0)),
                  row, _vec(D, 2), _vec(D, 2), _modspec(D, tpb, 2)],
        out_specs=[row, pl.BlockSpec((tm, 1), lambda i, k: (i, 0)), row],
        out_shape=[SDS((T, D), F32), SDS((T, 1), F32), SDS((T, D), BF)],
        scratch_shapes=[pltpu.VMEM((tm, D), F32)],
        compiler_params=_cparams(("parallel", "arbitrary")))(p4, wd, xs, pg, pb, gate)


def mod_matmul(xs, pg, pb, sc, sh, w, name):
    T, D = xs.shape
    N = w.shape[1]
    S = T // sc.shape[0]
    tm = min(512, S)
    tpb = S // tm
    tn = D

    def body(x_ref, pg_ref, pb_ref, sc_ref, sh_ref, w_ref, u_ref, o_ref, u_s):
        @pl.when(pl.program_id(1) == 0)
        def _():
            xin = x_ref[...] * pg_ref[...] + pb_ref[...]
            u = (xin * (1.0 + sc_ref[...]) + sh_ref[...]).astype(BF)
            u_s[...] = u
            u_ref[...] = u
        o_ref[...] = _dot_nn(u_s[...], w_ref[...])

    row = pl.BlockSpec((tm, D), lambda i, j: (i, 0))
    return pl.pallas_call(
        body, name=name, grid=(T // tm, N // tn),
        in_specs=[row, _vec(D, 2), _vec(D, 2), _modspec(D, tpb, 2), _modspec(D, tpb, 2),
                  pl.BlockSpec((D, tn), lambda i, j: (0, j))],
        out_specs=[row, pl.BlockSpec((tm, tn), lambda i, j: (i, j))],
        out_shape=[SDS((T, D), BF), SDS((T, N), F32)],
        scratch_shapes=[pltpu.VMEM((tm, D), BF)],
        compiler_params=_cparams(("parallel", "arbitrary")))(xs, pg, pb, sc, sh, w)


ATT_TQ = 1024
ATT_TK = 256


def _att_consts(tk):
    r = lax.broadcasted_iota(jnp.int32, (tk + 8, tk), 0)
    c = lax.broadcasted_iota(jnp.int32, (tk + 8, tk), 1)
    usum = jnp.where((r >= tk) | (c > r), 1.0, 0.0).astype(BF)
    lsum = jnp.where((r >= tk) | (c < r), 1.0, 0.0).astype(BF)
    dmask = lax.broadcasted_iota(jnp.int32, (tk, tk), 0) < lax.broadcasted_iota(jnp.int32, (tk, tk), 1)
    return usum, lsum, dmask


def _split_dot(m, v):
    hi = v.astype(BF)
    lo = (v - hi.astype(F32)).astype(BF)
    return _dot_nn(m, hi) + _dot_nn(m, lo)


def _softplus(z):
    return jnp.maximum(z, 0.0) + jnp.log(1.0 + jnp.exp(-jnp.abs(z)))


def _att_dims(S, D):
    dh = SB_HEAD_DIM
    cw = min(LANE, D)
    tq = min(ATT_TQ, S)
    tk = min(ATT_TK, tq)
    assert tq % tk == 0 and S % tq == 0
    return dh, cw, cw // dh, D // cw, tq, tk, S // tq, S // tk


def att_fwd(proj, Bl, S, D, name):
    dh, cw, hp, nblk, tq, tk, nq, nk = _att_dims(S, D)
    scale = 1.0 / math.sqrt(dh)
    assert math.log2(scale) == int(math.log2(scale))
    H = D // dh

    def body(q_ref, k_ref, v_ref, o_ref, car_ref, qs, ks, vts):
        usum, _, dmask = _att_consts(tk)
        for hh in range(hp):
            sl = slice(hh * dh, (hh + 1) * dh)
            qs[hh] = (q_ref[:, sl] * scale).astype(BF)
            ks[hh] = k_ref[:, sl].astype(BF)
            for kb in range(nk):
                vts[hh, kb] = v_ref[kb * tk:(kb + 1) * tk, sl].T.astype(BF)
        nch = tq // tk

        def qloop(qb, _):
            qo = pl.multiple_of(qb * tq, tq)
            n_full = qb * nch

            def blk(kb, state, diag):
                ko = pl.multiple_of(kb * tk, tk)
                chains = [(hh, c) for hh in range(hp) for c in range(0 if diag is None else diag, nch)]

                def masked(ch, val):
                    return jnp.where(dmask, val, 0.0) if ch[1] == diag else val

                z = {ch: _dot_nt(ks[ch[0], pl.ds(ko, tk), :], qs[ch[0], pl.ds(pl.multiple_of(qo + ch[1] * tk, tk), tk), :])
                     for ch in chains}
                sp = {ch: _softplus(z[ch]) for ch in chains}
                lk = {ch: masked(ch, -sp[ch]) for ch in chains}
                for hh, c in chains:
                    car_ref[hh, qb * nk + kb, :, c * tk:(c + 1) * tk] = state[hh][c][0]
                cs = {ch: _split_dot(usum, lk[ch]) for ch in chains}
                w = {ch: masked(ch, jnp.exp((z[ch] - sp[ch]) + state[ch[0]][ch[1]][0][0:1, :] + cs[ch][:tk]))
                     for ch in chains}
                pv = {ch: _dot_nn(vts[ch[0], kb], w[ch].astype(BF)) for ch in chains}
                return tuple(tuple(
                    (state[hh][c][0] + cs[(hh, c)][tk:], state[hh][c][1] + pv[(hh, c)]) if (hh, c) in z else state[hh][c]
                    for c in range(nch)) for hh in range(hp))

            state = tuple(tuple((jnp.zeros((8, tk), F32), jnp.zeros((dh, tk), F32)) for _ in range(nch))
                          for _ in range(hp))
            for i in reversed(range(nch)):
                state = blk(n_full + i, state, i)
            state = lax.fori_loop(0, n_full, lambda j, st: blk(n_full - 1 - j, st, None), state)
            for hh in range(hp):
                for c in range(nch):
                    o_ref[pl.ds(pl.multiple_of(qo + c * tk, tk), tk), hh * dh:(hh + 1) * dh] = (
                        state[hh][c][1].T.astype(BF))
            return 0

        lax.fori_loop(0, nq, qloop, 0)

    def seg(s):
        return pl.BlockSpec((S, cw), lambda b, h: (b, s * nblk + h))

    return pl.pallas_call(
        body, name=name, grid=(Bl, nblk),
        in_specs=[seg(0), seg(1), seg(2)],
        out_specs=[pl.BlockSpec((S, cw), lambda b, h: (b, h)),
                   pl.BlockSpec((None, hp, nq * nk, 8, tq), lambda b, h: (b, h, 0, 0, 0))],
        out_shape=[SDS((Bl * S, D), BF), SDS((Bl, H, nq * nk, 8, tq), F32)],
        scratch_shapes=[pltpu.VMEM((hp, S, dh), BF)] * 2 + [pltpu.VMEM((hp, nk, dh, tk), BF)],
        compiler_params=_cparams(("parallel", "parallel")))(proj, proj, proj)


def conv_fwd(proj, cw32, cb, cg, cbeta, Bl, S, D, kw, name):
    T = Bl * S
    ts = min(128, S)
    ns = S // ts
    off = HALO - (kw - 1)
    rc = min(64, ts)
    cw = min(LANE, D)

    def body(a_ref, b_ref, ha_ref, hb_ref, w_ref, cb_ref, g_ref, be_ref, cs_ref, xh_ref, rs_ref, hext, conv_s):
        i = pl.program_id(1)
        hext[pl.ds(HALO, ts), :] = a_ref[...] * _sig(b_ref[...])
        hh = ha_ref[...] * _sig(hb_ref[...])
        hext[pl.ds(0, HALO), :] = jnp.where(i == 0, 0.0, hh)
        for cb_ in range(D // cw):
            cols = slice(cb_ * cw, (cb_ + 1) * cw)
            accs = [jnp.zeros((rc, cw), F32) for _ in range(ts // rc)]
            for k in range(kw):
                wk = w_ref[k:k + 1, cols]
                for r in range(ts // rc):
                    accs[r] = accs[r] + wk * hext[pl.ds(r * rc + off + k, rc), cols]
            for r in range(ts // rc):
                conv_s[pl.ds(r * rc, rc), cols] = accs[r]
        conv = conv_s[...] + cb_ref[...]
        xh, rstd = _ln_stats(conv)
        xh_ref[...] = xh
        rs_ref[...] = rstd
        cl = xh * g_ref[...] + be_ref[...]
        cs_ref[...] = (cl * _sig(cl)).astype(BF)

    hpb = ts // HALO

    def tile(seg):
        return pl.BlockSpec((ts, D), lambda b, i: (b * ns + i, seg))

    def halo(seg):
        return pl.BlockSpec((HALO, D), lambda b, i: (jnp.maximum((b * ns + i) * hpb - 1, 0), seg))

    row = pl.BlockSpec((ts, D), lambda b, i: (b * ns + i, 0))
    vec = pl.BlockSpec((1, D), lambda b, i: (0, 0))
    return pl.pallas_call(
        body, name=name, grid=(Bl, ns),
        in_specs=[tile(3), tile(4), halo(3), halo(4), pl.BlockSpec((HALO, D), lambda b, i: (0, 0)), vec, vec, vec],
        out_specs=[row, row, pl.BlockSpec((ts, 1), lambda b, i: (b * ns + i, 0))],
        out_shape=[SDS((T, D), BF), SDS((T, D), F32), SDS((T, 1), F32)],
        scratch_shapes=[pltpu.VMEM((ts + HALO, D), F32), pltpu.VMEM((ts, D), F32)],
        compiler_params=_cparams(("parallel", "arbitrary")))(proj, proj, proj, proj, cw32, cb, cg, cbeta)


def mix_fwd(yatt, cs, proj, wsb, wco, wout, xs, pg, pb, gate, name):
    T, D = yatt.shape
    S = T // gate.shape[0]
    tm = min(256, S)
    tpb = S // tm

    def body(ya_ref, cs_ref, ga_ref, gb_ref, wsb_ref, wco_ref, wout_ref, x_ref, pg_ref, pb_ref, gate_ref,
             xh_ref, rs_ref, ysb_ref, yco_ref, mg_ref, o_ref):
        ysb = _dot_nn(ya_ref[...], wsb_ref[...])
        yco = _dot_nn(cs_ref[...], wco_ref[...])
        merged = _sig(ga_ref[...]) * ysb + _sig(gb_ref[...]) * yco
        mg = merged.astype(BF)
        o = _dot_nn(mg, wout_ref[...])
        xin = x_ref[...] * pg_ref[...] + pb_ref[...]
        r = DEEPNORM_ALPHA * xin + gate_ref[...] * o
        xh, rstd = _ln_stats(r)
        xh_ref[...] = xh
        rs_ref[...] = rstd
        ysb_ref[...] = ysb.astype(BF)
        yco_ref[...] = yco.astype(BF)
        mg_ref[...] = mg
        o_ref[...] = o.astype(BF)

    row = pl.BlockSpec((tm, D), lambda i: (i, 0))
    wfull = pl.BlockSpec((D, D), lambda i: (0, 0))
    return pl.pallas_call(
        body, name=name, grid=(T // tm,),
        in_specs=[row, row, pl.BlockSpec((tm, D), lambda i: (i, 5)), pl.BlockSpec((tm, D), lambda i: (i, 6)),
                  wfull, wfull, wfull, row, _vec(D, 1), _vec(D, 1), _modspec(D, tpb, 1)],
        out_specs=[row, pl.BlockSpec((tm, 1), lambda i: (i, 0)), row, row, row, row],
        out_shape=[SDS((T, D), F32), SDS((T, 1), F32)] + [SDS((T, D), BF)] * 4,
        compiler_params=_cparams(("parallel",)))(yatt, cs, proj, proj, wsb, wco, wout, xs, pg, pb, gate)


def ln_bwd(dout, xh, rstd, lng, lnb, gate, sub, res_w, name, target=None):
    T, D = xh.shape
    Bl = gate.shape[0]
    S = T // Bl
    tm = min(256, S)
    tpb = S // tm
    first = target is not None

    def body(*refs):
        if first:
            tg_ref, xh_ref, rs_ref, g_ref, b_ref, gate_ref, sub_ref = refs[:7]
            dr_ref, ds_ref, dg_ref, db_ref, dgate_ref, loss_ref = refs[7:]
        else:
            do_ref, xh_ref, rs_ref, g_ref, b_ref, gate_ref, sub_ref = refs[:7]
            dr_ref, ds_ref, dg_ref, db_ref, dgate_ref = refs[7:]
        i = pl.program_id(0)
        xh_ = xh_ref[...]
        if first:
            diff = (xh_ * g_ref[...] + b_ref[...]) - tg_ref[...]
            lsum = jnp.sum(jnp.sum(diff * diff, axis=1, keepdims=True), axis=0, keepdims=True) * (0.5 / D)
            do = diff * (1.0 / D)
        else:
            do = do_ref[...]

        @pl.when(i == 0)
        def _():
            dg_ref[...] = jnp.zeros_like(dg_ref)
            db_ref[...] = jnp.zeros_like(db_ref)
            if first:
                loss_ref[...] = jnp.zeros_like(loss_ref)

        @pl.when(i % tpb == 0)
        def _():
            dgate_ref[...] = jnp.zeros_like(dgate_ref)

        if first:
            loss_ref[...] += jnp.broadcast_to(lsum, loss_ref.shape)
        dg_ref[...] += _rowsum(do * xh_)
        db_ref[...] += _rowsum(do)
        dr = _ln_bwd(do * g_ref[...], xh_, rs_ref[...])
        dr_ref[...] = dr
        ds_ref[...] = (dr * gate_ref[...] * res_w).astype(BF)
        dgate_ref[...] += _rowsum(dr * (res_w * sub_ref[...].astype(F32)))

    row = pl.BlockSpec((tm, D), lambda i: (i, 0))
    vec = _vec(D, 1)
    mod = _modspec(D, tpb, 1)
    out_specs = [row, row, vec, vec, mod]
    out_shape = [SDS((T, D), F32), SDS((T, D), BF), SDS((1, D), F32), SDS((1, D), F32), SDS((Bl, 1, D), F32)]
    if first:
        out_specs.append(pl.BlockSpec((8, LANE), lambda i: (0, 0)))
        out_shape.append(SDS((8, LANE), F32))
    return pl.pallas_call(
        body, name=name, grid=(T // tm,),
        in_specs=[row, row, pl.BlockSpec((tm, 1), lambda i: (i, 0)), vec, vec, mod, row],
        out_specs=out_specs, out_shape=out_shape,
        compiler_params=_cparams(("arbitrary",)))(target if first else dout, xh, rstd, lng, lnb, gate, sub)


def swiglu_bwd(df, wd, a4, g4, name):
    nj, T, nb = a4.shape
    D = df.shape[1]
    tm = min(512, T)

    def body(df_ref, wd_ref, a_ref, g_ref, o_ref):
        dp = _dot_nt(df_ref[...], wd_ref[...])
        a = a_ref[...].astype(F32)
        g = g_ref[...].astype(F32)
        s = _sig(a)
        o_ref[0] = (dp * g * (s * (1.0 + a * (1.0 - s)))).astype(BF)
        o_ref[1] = (dp * (a * s)).astype(BF)

    blk = pl.BlockSpec((None, tm, nb), lambda i, j: (j, i, 0))
    return pl.pallas_call(
        body, name=name, grid=(T // tm, nj),
        in_specs=[pl.BlockSpec((tm, D), lambda i, j: (i, 0)), pl.BlockSpec((nb, D), lambda i, j: (j, 0)), blk, blk],
        out_specs=pl.BlockSpec((2, None, tm, nb), lambda i, j: (0, j, i, 0)),
        out_shape=SDS((2, nj, T, nb), BF),
        compiler_params=_cparams(("parallel", "arbitrary")))(df, wd, a4, g4)


def mod_bwd(dh, w, dr, xs, pg, pb, sc, blocked, name):
    T, D = dr.shape
    Bl = sc.shape[0]
    S = T // Bl
    tm = min(512, S)
    tpb = S // tm
    if blocked:
        nk, _, kb = dh.shape
        dh_spec = pl.BlockSpec((None, tm, kb), lambda i, k: (k, i, 0))
        w_spec = pl.BlockSpec((None, D, kb), lambda i, k: (k, 0, 0))
    else:
        kb = D
        nk = dh.shape[1] // kb
        dh_spec = pl.BlockSpec((tm, kb), lambda i, k: (i, k))
        w_spec = pl.BlockSpec((D, kb), lambda i, k: (0, k))

    def body(dh_ref, w_ref, dr_ref, x_ref, pg_ref, pb_ref, sc_ref, dx_ref, dsc_ref, dsh_ref, acc):
        i = pl.program_id(0)
        k = pl.program_id(1)

        @pl.when(k == 0)
        def _():
            acc[...] = jnp.zeros_like(acc)
        acc[...] += _dot_nt(dh_ref[...], w_ref[...])

        @pl.when((k == nk - 1) & (i % tpb == 0))
        def _():
            dsc_ref[...] = jnp.zeros_like(dsc_ref)
            dsh_ref[...] = jnp.zeros_like(dsh_ref)

        @pl.when(k == nk - 1)
        def _():
            du = acc[...]
            xin = x_ref[...] * pg_ref[...] + pb_ref[...]
            dx_ref[...] = DEEPNORM_ALPHA * dr_ref[...] + du * (1.0 + sc_ref[...])
            dsc_ref[...] += _rowsum(du * xin)
            dsh_ref[...] += _rowsum(du)

    row = pl.BlockSpec((tm, D), lambda i, k: (i, 0))
    mod = _modspec(D, tpb, 2)
    return pl.pallas_call(
        body, name=name, grid=(T // tm, nk),
        in_specs=[dh_spec, w_spec, row, row, _vec(D, 2), _vec(D, 2), mod],
        out_specs=[row, mod, mod],
        out_shape=[SDS((T, D), F32), SDS((Bl, 1, D), F32), SDS((Bl, 1, D), F32)],
        scratch_shapes=[pltpu.VMEM((tm, D), F32)],
        compiler_params=_cparams(("arbitrary", "arbitrary")))(dh, w, dr, xs, pg, pb, sc)


def merge_bwd(do2, proj, ysb, yco, wsb, wco, wout, name):
    T, D = do2.shape
    tm = min(256, T)

    def body(do_ref, ga_ref, gb_ref, ysb_ref, yco_ref, wsb_ref, wco_ref, wout_ref,
             dysb_ref, dyco_ref, dga_ref, dgb_ref, dya_ref, dcs_ref):
        dm = _dot_nt(do_ref[...], wout_ref[...])
        sa = _sig(ga_ref[...])
        sb = _sig(gb_ref[...])
        dysb = (dm * sa).astype(BF)
        dyco = (dm * sb).astype(BF)
        dysb_ref[...] = dysb
        dyco_ref[...] = dyco
        dga_ref[...] = (dm * ysb_ref[...].astype(F32) * (sa * (1.0 - sa))).astype(BF)
        dgb_ref[...] = (dm * yco_ref[...].astype(F32) * (sb * (1.0 - sb))).astype(BF)
        dya_ref[...] = _dot_nt(dysb, wsb_ref[...]).astype(BF)
        dcs_ref[...] = _dot_nt(dyco, wco_ref[...])

    row = pl.BlockSpec((tm, D), lambda i: (i, 0))
    wfull = pl.BlockSpec((D, D), lambda i: (0, 0))
    return pl.pallas_call(
        body, name=name, grid=(T // tm,),
        in_specs=[row, pl.BlockSpec((tm, D), lambda i: (i, 5)), pl.BlockSpec((tm, D), lambda i: (i, 6)),
                  row, row, wfull, wfull, wfull],
        out_specs=[row] * 6,
        out_shape=[SDS((T, D), BF)] * 5 + [SDS((T, D), F32)],
        compiler_params=_cparams(("parallel",)))(do2, proj, proj, ysb, yco, wsb, wco, wout)


def att_bwd(proj, dyatt, car, Bl, S, D, name):
    dh, cw, hp, nblk, tq, tk, nq, nk = _att_dims(S, D)
    scale = 1.0 / math.sqrt(dh)

    def body(q_ref, k_ref, v_ref, do_ref, car_ref, dq_ref, dk_ref, dv_ref, qs, ks, vs, dos, kts, dk_acc, dv_acc):
        usum, lsum, dmask = _att_consts(tk)
        for hh in range(hp):
            sl = slice(hh * dh, (hh + 1) * dh)
            qs[hh] = (q_ref[:, sl] * scale).astype(BF)
            ks[hh] = k_ref[:, sl].astype(BF)
            vs[hh] = v_ref[:, sl].astype(BF)
            dos[hh] = do_ref[:, sl]
            for kb in range(nk):
                kts[hh, kb] = k_ref[kb * tk:(kb + 1) * tk, sl].T.astype(BF)
        dk_acc[...] = jnp.zeros_like(dk_acc)
        dv_acc[...] = jnp.zeros_like(dv_acc)
        nch = tq // tk

        def qloop(qb, _):
            qo = pl.multiple_of(qb * tq, tq)
            n_full = qb * nch

            def blk(kb, state, diag):
                ko = pl.multiple_of(kb * tk, tk)
                chains = [(hh, c) for hh in range(hp) for c in range(0 if diag is None else diag, nch)]

                def masked(ch, val):
                    return jnp.where(dmask, val, 0.0) if ch[1] == diag else val

                def qrows(ref, ch):
                    return ref[ch[0], pl.ds(pl.multiple_of(qo + ch[1] * tk, tk), tk), :]

                k = [ks[hh, pl.ds(ko, tk), :] for hh in range(hp)]
                v = [vs[hh, pl.ds(ko, tk), :] for hh in range(hp)]
                z = {ch: _dot_nt(k[ch[0]], qrows(qs, ch)) for ch in chains}
                dw = {ch: _dot_nt(v[ch[0]], qrows(dos, ch)) for ch in chains}
                sp = {ch: _softplus(z[ch]) for ch in chains}
                lk = {ch: masked(ch, -sp[ch]) for ch in chains}
                cs = {ch: _split_dot(usum, lk[ch]) for ch in chains}
                w = {ch: masked(ch, jnp.exp((z[ch] - sp[ch])
                                            + car_ref[ch[0], qb * nk + kb, 0:1, ch[1] * tk:(ch[1] + 1) * tk]
                                            + cs[ch][:tk])) for ch in chains}
                dlw = {ch: dw[ch] * w[ch] for ch in chains}
                gs = {ch: _split_dot(lsum, dlw[ch]) for ch in chains}
                sg = {ch: jnp.exp(z[ch] - sp[ch]) for ch in chains}
                dzb = {ch: masked(ch, dlw[ch] * (1.0 - sg[ch])
                                  - sg[ch] * (state[ch[0]][ch[1]][0][0:1, :] + gs[ch][:tk])).astype(BF)
                       for ch in chains}
                wb = {ch: w[ch].astype(BF) for ch in chains}
                for hh in range(hp):
                    mine = [ch for ch in chains if ch[0] == hh]
                    dk_acc[hh, kb] += sum(_dot_nn(dzb[ch], qrows(qs, ch)) for ch in mine)
                    dv_acc[hh, kb] += sum(_dot_nn(wb[ch], qrows(dos, ch)) for ch in mine)
                dq = {ch: _dot_nn(kts[ch[0], kb], dzb[ch]) for ch in chains}
                return tuple(tuple(
                    (state[hh][c][0] + gs[(hh, c)][tk:], state[hh][c][1] + dq[(hh, c)]) if (hh, c) in z else state[hh][c]
                    for c in range(nch)) for hh in range(hp))

            state = tuple(tuple((jnp.zeros((8, tk), F32), jnp.zeros((dh, tk), F32)) for _ in range(nch))
                          for _ in range(hp))
            state = lax.fori_loop(0, n_full, lambda kb, st: blk(kb, st, None), state)
            for i in range(nch):
                state = blk(n_full + i, state, i)
            for hh in range(hp):
                for c in range(nch):
                    dq_ref[pl.ds(pl.multiple_of(qo + c * tk, tk), tk), hh * dh:(hh + 1) * dh] = (
                        (state[hh][c][1].T * scale).astype(BF))
            return 0

        lax.fori_loop(0, nq, qloop, 0)
        for hh in range(hp):
            sl = slice(hh * dh, (hh + 1) * dh)
            for kb in range(nk):
                dk_ref[kb * tk:(kb + 1) * tk, sl] = dk_acc[hh, kb].astype(BF)
                dv_ref[kb * tk:(kb + 1) * tk, sl] = dv_acc[hh, kb].astype(BF)

    def seg(s):
        return pl.BlockSpec((S, cw), lambda b, h: (b, s * nblk + h))

    blk_spec = pl.BlockSpec((S, cw), lambda b, h: (b, h))
    return pl.pallas_call(
        body, name=name, grid=(Bl, nblk),
        in_specs=[seg(0), seg(1), seg(2), blk_spec,
                  pl.BlockSpec((None, hp, nq * nk, 8, tq), lambda b, h: (b, h, 0, 0, 0))],
        out_specs=[blk_spec, blk_spec, blk_spec],
        out_shape=[SDS((Bl * S, D), BF)] * 3,
        scratch_shapes=[pltpu.VMEM((hp, S, dh), BF)] * 4 + [pltpu.VMEM((hp, nk, dh, tk), BF)]
        + [pltpu.VMEM((hp, nk, tk, dh), F32)] * 2,
        compiler_params=_cparams(("parallel", "parallel")))(proj, proj, proj, dyatt, car)


def conv_ln_bwd(dcs, xhc, rstd_c, cg, cbeta, name):
    T, D = dcs.shape
    tm = min(256, T)

    def body(dcs_ref, xh_ref, rs_ref, g_ref, b_ref, dconv_ref, dg_ref, db_ref, dcb_ref):
        @pl.when(pl.program_id(0) == 0)
        def _():
            dg_ref[...] = jnp.zeros_like(dg_ref)
            db_ref[...] = jnp.zeros_like(db_ref)
            dcb_ref[...] = jnp.zeros_like(dcb_ref)
        xh = xh_ref[...]
        cl = xh * g_ref[...] + b_ref[...]
        s = _sig(cl)
        dcl = dcs_ref[...] * (s * (1.0 + cl * (1.0 - s)))
        dg_ref[...] += _rowsum(dcl * xh)
        db_ref[...] += _rowsum(dcl)
        dconv = _ln_bwd(dcl * g_ref[...], xh, rs_ref[...])
        dconv_ref[...] = dconv
        dcb_ref[...] += _rowsum(dconv)

    row = pl.BlockSpec((tm, D), lambda i: (i, 0))
    vec = _vec(D, 1)
    return pl.pallas_call(
        body, name=name, grid=(T // tm,),
        in_specs=[row, row, pl.BlockSpec((tm, 1), lambda i: (i, 0)), vec, vec],
        out_specs=[row, vec, vec, vec],
        out_shape=[SDS((T, D), F32)] + [SDS((1, D), F32)] * 3,
        compiler_params=_cparams(("arbitrary",)))(dcs, xhc, rstd_c, cg, cbeta)


def conv_bwd(dconv, proj, cw32, Bl, S, D, kw, name):
    T = Bl * S
    ts = min(128, S)
    ns = S // ts
    off = HALO - (kw - 1)
    rc = min(64, ts)
    cw = min(LANE, D)
    hpb = ts // HALO
    nhb = T // HALO

    def body(dc_ref, dcn_ref, a_ref, b_ref, ha_ref, hb_ref, w_ref, da_ref, db_ref, dw_ref, hext, dext, dh_s):
        b_ = pl.program_id(0)
        i = pl.program_id(1)

        @pl.when((b_ == 0) & (i == 0))
        def _():
            dw_ref[...] = jnp.zeros_like(dw_ref)
        a = a_ref[...]
        sb = _sig(b_ref[...])
        hext[pl.ds(HALO, ts), :] = a * sb
        hh = ha_ref[...] * _sig(hb_ref[...])
        hext[pl.ds(0, HALO), :] = jnp.where(i == 0, 0.0, hh)
        dext[pl.ds(0, ts), :] = dc_ref[...]
        dext[pl.ds(ts, HALO), :] = jnp.where(i == ns - 1, 0.0, dcn_ref[...])
        for cb_ in range(D // cw):
            cols = slice(cb_ * cw, (cb_ + 1) * cw)
            accs = [jnp.zeros((rc, cw), F32) for _ in range(ts // rc)]
            for k in range(kw):
                wk = w_ref[k:k + 1, cols]
                wsum = jnp.zeros((rc, cw), F32)
                for r in range(ts // rc):
                    accs[r] = accs[r] + wk * dext[pl.ds(r * rc + (kw - 1) - k, rc), cols]
                    wsum = wsum + dext[pl.ds(r * rc, rc), cols] * hext[pl.ds(r * rc + off + k, rc), cols]
                dw_ref[k:k + 1, cols] += _rowsum(wsum)
            for r in range(ts // rc):
                dh_s[pl.ds(r * rc, rc), cols] = accs[r]
        dhc = dh_s[...]
        da_ref[...] = (dhc * sb).astype(BF)
        db_ref[...] = (dhc * a * (sb * (1.0 - sb))).astype(BF)

    def tile(seg):
        return pl.BlockSpec((ts, D), lambda b, i: (b * ns + i, seg))

    def halo(seg):
        return pl.BlockSpec((HALO, D), lambda b, i: (jnp.maximum((b * ns + i) * hpb - 1, 0), seg))

    row = pl.BlockSpec((ts, D), lambda b, i: (b * ns + i, 0))
    nxt = pl.BlockSpec((HALO, D), lambda b, i: (jnp.minimum((b * ns + i + 1) * hpb, nhb - 1), 0))
    wspec = pl.BlockSpec((HALO, D), lambda b, i: (0, 0))
    return pl.pallas_call(
        body, name=name, grid=(Bl, ns),
        in_specs=[row, nxt, tile(3), tile(4), halo(3), halo(4), wspec],
        out_specs=[row, row, wspec],
        out_shape=[SDS((T, D), BF), SDS((T, D), BF), SDS((HALO, D), F32)],
        scratch_shapes=[pltpu.VMEM((ts + HALO, D), F32), pltpu.VMEM((ts + HALO, D), F32), pltpu.VMEM((ts, D), F32)],
        compiler_params=_cparams(("arbitrary", "arbitrary")))(dconv, dconv, proj, proj, proj, proj, cw32)


def matmul_tn(xa, ga, x_spec, g_spec, out_shape, out_spec, acc_shape, grid, name):
    nk = grid[-1]

    def body(x_ref, g_ref, o_ref, acc):
        k = pl.program_id(len(grid) - 1)

        @pl.when(k == 0)
        def _():
            acc[...] = jnp.zeros_like(acc)
        acc[...] += _dot_tn(x_ref[...], g_ref[...])

        @pl.when(k == nk - 1)
        def _():
            o_ref[...] = acc[...]

    return pl.pallas_call(
        body, name=name, grid=grid, in_specs=[x_spec, g_spec], out_specs=out_spec,
        out_shape=SDS(out_shape, F32), scratch_shapes=[pltpu.VMEM(acc_shape, F32)],
        compiler_params=_cparams(("parallel",) * (len(grid) - 1) + ("arbitrary",)))(xa, ga)


def wgrad_std(xa, ga, name):
    T, M = xa.shape
    N = ga.shape[1]
    tk = min(512, T)
    return matmul_tn(xa, ga, pl.BlockSpec((tk, M), lambda n, k: (k, 0)), pl.BlockSpec((tk, N), lambda n, k: (k, 0)),
                     (M, N), pl.BlockSpec((M, N), lambda n, k: (0, 0)), (M, N), (1, T // tk), name)


def wgrad_down(p4, df, name):
    nj, T, nb = p4.shape
    D = df.shape[1]
    tk = min(512, T)
    return matmul_tn(p4, df, pl.BlockSpec((None, tk, nb), lambda j, k: (j, k, 0)),
                     pl.BlockSpec((tk, D), lambda j, k: (k, 0)),
                     (nj * nb, D), pl.BlockSpec((nb, D), lambda j, k: (j, 0)), (nb, D), (nj, T // tk), name)


def wgrad_gu(u, dh8, name):
    n8, T, nb = dh8.shape
    D = u.shape[1]
    tk = min(512, T)
    return matmul_tn(u, dh8, pl.BlockSpec((tk, D), lambda j, k: (k, 0)),
                     pl.BlockSpec((None, tk, nb), lambda j, k: (j, k, 0)),
                     (n8, D, nb), pl.BlockSpec((None, D, nb), lambda j, k: (j, 0, 0)), (D, nb), (n8, T // tk), name)


def wgrad_in(u, dproj, name):
    T, D = u.shape
    bw = dproj.shape[1] // N_DEV
    tk = min(512, T)
    return matmul_tn(u, dproj, pl.BlockSpec((tk, D), lambda j, k: (k, 0)),
                     pl.BlockSpec((tk, bw), lambda j, k: (k, j)),
                     (N_DEV, D, bw), pl.BlockSpec((None, D, bw), lambda j, k: (j, 0, 0)), (D, bw),
                     (N_DEV, T // tk), name)


def kernel(x, c, w_ada, b_ada, ffn1_w_gu, ffn1_w_down, ln1_g, ln1_b, w_in, w_sb_out, conv_w, conv_b, conv_ln_g, conv_ln_b, w_conv_out, w_out, ln2_g, ln2_b, ffn2_w_gu, ffn2_w_down, ln3_g, ln3_b, loss_target, m_w_ada, m_b_ada, m_ffn1_w_gu, m_ffn1_w_down, m_ln1_g, m_ln1_b, m_w_in, m_w_sb_out, m_conv_w, m_conv_b, m_conv_ln_g, m_conv_ln_b, m_w_conv_out, m_w_out, m_ln2_g, m_ln2_b, m_ffn2_w_gu, m_ffn2_w_down, m_ln3_g, m_ln3_b, v_w_ada, v_b_ada, v_ffn1_w_gu, v_ffn1_w_down, v_ln1_g, v_ln1_b, v_w_in, v_w_sb_out, v_conv_w, v_conv_b, v_conv_ln_g, v_conv_ln_b, v_w_conv_out, v_w_out, v_ln2_g, v_ln2_b, v_ffn2_w_gu, v_ffn2_w_down, v_ln3_g, v_ln3_b):
    Bl, S, D = x.shape
    T = Bl * S
    kw = conv_w.shape[1]
    ax, ay, ac = lax.axis_index("x"), lax.axis_index("y"), lax.axis_index("c")
    me = 4 * ax + 2 * ay + ac
    qc = jnp.stack([2 * ax + ay, ac]).astype(jnp.int32)

    big = dict(ffn1_w_gu=ffn1_w_gu[0], ffn1_w_down=ffn1_w_down[0], w_in=w_in[0], w_sb_out=w_sb_out[0],
               w_conv_out=w_conv_out[0], w_out=w_out[0], ffn2_w_gu=ffn2_w_gu[0], ffn2_w_down=ffn2_w_down[0])
    big_m = dict(ffn1_w_gu=m_ffn1_w_gu[0], ffn1_w_down=m_ffn1_w_down[0], w_in=m_w_in[0], w_sb_out=m_w_sb_out[0],
                 w_conv_out=m_w_conv_out[0], w_out=m_w_out[0], ffn2_w_gu=m_ffn2_w_gu[0], ffn2_w_down=m_ffn2_w_down[0])
    big_v = dict(ffn1_w_gu=v_ffn1_w_gu[0], ffn1_w_down=v_ffn1_w_down[0], w_in=v_w_in[0], w_sb_out=v_w_sb_out[0],
                 w_conv_out=v_w_conv_out[0], w_out=v_w_out[0], ffn2_w_gu=v_ffn2_w_gu[0], ffn2_w_down=v_ffn2_w_down[0])
    names = list(big)

    gathered = allgather_weights([big[n].astype(BF) for n in names], "allgather_weights")
    G = dict(zip(names, gathered))
    wg1 = G["ffn1_w_gu"]
    wg2 = G["ffn2_w_gu"]
    wd1 = G["ffn1_w_down"].reshape(-1, D)
    wd2 = G["ffn2_w_down"].reshape(-1, D)
    win = jnp.transpose(G["w_in"], (1, 0, 2)).reshape(D, -1)
    wsb = G["w_sb_out"].reshape(D, D)
    wco = G["w_conv_out"].reshape(-1, D)
    wout = G["w_out"].reshape(D, D)

    cw8 = small_allgather(conv_w[0], "allgather_conv_w")
    cw_full = jnp.transpose(cw8, (1, 0, 2)).reshape(kw, D)
    cw32 = jnp.concatenate([cw_full, jnp.zeros((HALO - kw, D), F32)], axis=0)

    c_all = small_allgather(c, "allgather_c").reshape(N_DEV * Bl, D)
    ncol = w_ada.shape[2]
    b_cols = lax.dynamic_slice(b_ada, (0, me * ncol), (1, ncol))
    mod_cols, s_all = ada_fwd(c_all, w_ada[0], b_cols, "ada_fwd")
    mod8 = small_allgather(mod_cols, "allgather_mod")
    mod_mine = lax.dynamic_slice(mod8, (0, me * Bl, 0), (N_DEV, Bl, ncol))
    mod = jnp.transpose(mod_mine, (1, 0, 2)).reshape(Bl, N_MOD_ROWS, 1, D)
    sh1, sc1, g1, sh2, sc2, g2, sh3, sc3, g3 = [mod[:, i] for i in range(N_MOD_ROWS)]

    ones = jnp.ones((1, D), F32)
    zeros = jnp.zeros((1, D), F32)
    xf = x.reshape(T, D)
    tgt = loss_target.reshape(T, D)

    u1, a1, gg1, p1 = ffn_up(xf, ones, zeros, sc1, sh1, wg1, "ffn1_up")
    xh1, rs1, f1 = ffn_down_ln(p1, wd1, xf, ones, zeros, g1, "ffn1_down_ln")
    u2, proj = mod_matmul(xh1, ln1_g, ln1_b, sc2, sh2, win, "in_proj")
    yatt, car = att_fwd(proj, Bl, S, D, "att_fwd")
    cs, xhc, rsc = conv_fwd(proj, cw32, conv_b, conv_ln_g, conv_ln_b, Bl, S, D, kw, "conv_fwd")
    xh2, rs2, ysb, yco, merged, o2 = mix_fwd(yatt, cs, proj, wsb, wco, wout, xh1, ln1_g, ln1_b, g2, "mix_fwd")
    u3, a3, gg3, p3 = ffn_up(xh2, ln2_g, ln2_b, sc3, sh3, wg2, "ffn2_up")
    xh3, rs3, f3 = ffn_down_ln(p3, wd2, xh2, ln2_g, ln2_b, g3, "ffn2_down_ln")

    dr3, df3, dln3g, dln3b, dg3, lossp = ln_bwd(None, xh3, rs3, ln3_g, ln3_b, g3, f3, MACARON_WEIGHT,
                                                 "ln3_bwd", target=tgt)
    dh3 = swiglu_bwd(df3, wd2, a3, gg3, "ffn2_swiglu_bwd").reshape((-1,) + a3.shape[1:])
    gw_d2 = wgrad_down(p3, df3, "wgrad_ffn2_down")
    gw_g2 = wgrad_gu(u3, dh3, "wgrad_ffn2_gu")
    dx2, dsc3, dsh3 = mod_bwd(dh3, wg2, dr3, xh2, ln2_g, ln2_b, sc3, True, "ffn2_mod_bwd")

    dr2, do2, dln2g, dln2b, dg2 = ln_bwd(dx2, xh2, rs2, ln2_g, ln2_b, g2, o2, 1.0, "ln2_bwd")
    gw_out = wgrad_std(merged, do2, "wgrad_out")
    dysb, dyco, dga, dgb, dyatt, dcs = merge_bwd(do2, proj, ysb, yco, wsb, wco, wout, "merge_bwd")
    gw_sb = wgrad_std(yatt, dysb, "wgrad_sb")
    gw_co = wgrad_std(cs, dyco, "wgrad_conv_out")
    dq, dk, dv = att_bwd(proj, dyatt, car, Bl, S, D, "att_bwd")
    dconv, dcg, dcbeta, dcb = conv_ln_bwd(dcs, xhc, rsc, conv_ln_g, conv_ln_b, "conv_ln_bwd")
    dglu_a, dglu_b, dcw = conv_bwd(dconv, proj, cw32, Bl, S, D, kw, "conv_bwd")
    dproj = jnp.concatenate([dq, dk, dv, dglu_a, dglu_b, dga, dgb], axis=1)
    gw_in = wgrad_in(u2, dproj, "wgrad_in")
    dx1, dsc2, dsh2 = mod_bwd(dproj, win, dr2, xh1, ln1_g, ln1_b, sc2, False, "mix_mod_bwd")

    dr1, df1, dln1g, dln1b, dg1 = ln_bwd(dx1, xh1, rs1, ln1_g, ln1_b, g1, f1, MACARON_WEIGHT, "ln1_bwd")
    dh1 = swiglu_bwd(df1, wd1, a1, gg1, "ffn1_swiglu_bwd").reshape((-1,) + a1.shape[1:])
    gw_d1 = wgrad_down(p1, df1, "wgrad_ffn1_down")
    gw_g1 = wgrad_gu(u1, dh1, "wgrad_ffn1_gu")
    grad_x, dsc1, dsh1 = mod_bwd(dh1, wg1, dr1, xf, ones, zeros, sc1, True, "ffn1_mod_bwd")

    dmod = jnp.concatenate([dsh1, dsc1, dg1, dsh2, dsc2, dg2, dsh3, dsc3, dg3], axis=1).reshape(Bl, N_MOD_ROWS * D)
    dmod_all = small_allgather(dmod, "allgather_dmod").reshape(N_DEV * Bl, N_MOD_ROWS * D)
    dmod_cols = lax.dynamic_slice(dmod_all, (0, me * ncol), (N_DEV * Bl, ncol))
    g_w_ada, g_b_ada = ada_bwd(s_all, dmod_cols, dmod_all, "ada_bwd")

    npad = 16
    small_rows = [dln1g, dln1b, dcb, dcg, dcbeta, dln2g, dln2b, dln3g, dln3b,
                  jnp.broadcast_to(lossp[0:1, 0:1], (1, D))]
    pack = jnp.concatenate(small_rows + [jnp.zeros((npad - len(small_rows), D), F32), dcw], axis=0)
    small = small_sum(small_allgather(pack, "allgather_small"), "small_sum")
    loss = small[9, 0]
    small_w = [ln1_g, ln1_b, conv_b, conv_ln_g, conv_ln_b, ln2_g, ln2_b, ln3_g, ln3_b]
    small_m = [m_ln1_g, m_ln1_b, m_conv_b, m_conv_ln_g, m_conv_ln_b, m_ln2_g, m_ln2_b, m_ln3_g, m_ln3_b]
    small_v = [v_ln1_g, v_ln1_b, v_conv_b, v_conv_ln_g, v_conv_ln_b, v_ln2_g, v_ln2_b, v_ln3_g, v_ln3_b]
    padrows = jnp.zeros((npad - len(small_w), D), F32)
    s_g, s_d, s_m, s_v = adamw(jnp.concatenate(small_w + [padrows], axis=0),
                               jnp.concatenate(small_m + [padrows], axis=0),
                               jnp.concatenate(small_v + [padrows], axis=0),
                               [_plain_part(small[:npad])], "adamw_small")
    dcw_mine = lax.dynamic_slice(small[npad:npad + kw], (0, me * (D // N_DEV)), (kw, D // N_DEV))
    cw_g, cw_d, cw_m, cw_v = adamw(conv_w[0], m_conv_w[0], v_conv_w[0], [_plain_part(dcw_mine)], "adamw_conv_w")
    ada_g, ada_d, ada_m, ada_v = adamw(w_ada[0], m_w_ada[0], v_w_ada[0], [_plain_part(g_w_ada)], "adamw_w_ada")
    bada_g, bada_d, bada_m, bada_v = adamw(b_ada, m_b_ada, v_b_ada, [_plain_part(g_b_ada)], "adamw_b_ada")

    gfull = dict(ffn1_w_gu=gw_g1, ffn1_w_down=gw_d1, w_in=gw_in, w_sb_out=gw_sb, w_conv_out=gw_co, w_out=gw_out,
                 ffn2_w_gu=gw_g2, ffn2_w_down=gw_d2)
    g42 = [gfull[n].reshape((4, 2) + big[n].shape) for n in names]
    recv_a = exchange_sibling(g42, "rs_sibling")
    sums = [chip_sum(g, r, qc, "chip_sum_" + n) for g, r, n in zip(g42, recv_a, names)]
    recv_b = exchange_chips([s[1] for s in sums], "rs_chips")
    res = {}
    for n, (s_own, _), rb in zip(names, sums, recv_b):
        parts = [_plain_part(s_own), _slot_part(rb, 0), _slot_part(rb, 1), _slot_part(rb, 2)]
        res[n] = adamw(big[n], big_m[n], big_v[n], parts, "adamw_" + n)

    def small_out(k):
        order = dict(ln1_g=0, ln1_b=1, conv_b=2, conv_ln_g=3, conv_ln_b=4, ln2_g=5, ln2_b=6, ln3_g=7, ln3_b=8)
        return lambda arr: arr[order[k]:order[k] + 1]

    weight_order = ["w_ada", "b_ada", "ffn1_w_gu", "ffn1_w_down", "ln1_g", "ln1_b", "w_in", "w_sb_out", "conv_w",
                    "conv_b", "conv_ln_g", "conv_ln_b", "w_conv_out", "w_out", "ln2_g", "ln2_b", "ffn2_w_gu",
                    "ffn2_w_down", "ln3_g", "ln3_b"]

    shapes = dict(w_ada=w_ada.shape, b_ada=b_ada.shape, conv_w=conv_w.shape, ln1_g=ln1_g.shape,
                  **{n: (1,) + big[n].shape for n in names})

    def pick(which):
        outs = []
        for n in weight_order:
            if n == "w_ada":
                a = (ada_g, ada_d, ada_m, ada_v)[which]
            elif n == "b_ada":
                a = (bada_g, bada_d, bada_m, bada_v)[which]
            elif n == "conv_w":
                a = (cw_g, cw_d, cw_m, cw_v)[which]
            elif n in res:
                a = res[n][which]
            else:
                a = small_out(n)((s_g, s_d, s_m, s_v)[which])
            outs.append(a.reshape(shapes.get(n, ln1_g.shape)))
        return outs

    return (loss, grad_x.reshape(Bl, S, D), *pick(0), *pick(1), *pick(2), *pick(3))
```

```python
import functools
import math

import jax
import jax.numpy as jnp
from jax import lax
from jax.experimental import pallas as pl
from jax.experimental.pallas import tpu as pltpu

F32 = jnp.float32
BF = jnp.bfloat16
SDS = jax.ShapeDtypeStruct
MESH = pl.DeviceIdType.MESH

N_DEV = 8
SB_HEAD_DIM = 64
N_MOD_ROWS = 9
MACARON_WEIGHT = 0.5
DEEPNORM_ALPHA = 2.0 ** 0.25
LN_EPS = 1e-5
ADAM_LR = 0.001
ADAM_B1 = 0.9
ADAM_B2 = 0.999
ADAM_EPS = 1e-08
ADAM_WD = 0.01
ADAM_STEP = 10

V7X_VMEM_LIMIT = 52 * 1024 * 1024
LANE = 128
HALO = 32


def _cparams(sem=None):
    return pltpu.CompilerParams(dimension_semantics=sem, vmem_limit_bytes=V7X_VMEM_LIMIT)


def _dot_nn(a, b):
    return lax.dot_general(a, b, (((1,), (0,)), ((), ())), preferred_element_type=F32)


def _dot_nt(a, b):
    return lax.dot_general(a, b, (((1,), (1,)), ((), ())), preferred_element_type=F32)


def _dot_tn(a, b):
    return lax.dot_general(a, b, (((0,), (0,)), ((), ())), preferred_element_type=F32)


def _sig(x):
    return 1.0 / (1.0 + jnp.exp(-x))


def _ln_stats(r):
    mu = jnp.mean(r, axis=-1, keepdims=True)
    d = r - mu
    var = jnp.mean(d * d, axis=-1, keepdims=True)
    rstd = lax.rsqrt(var + LN_EPS)
    return d * rstd, rstd


def _ln_bwd(dxh, xh, rstd):
    m1 = jnp.mean(dxh, axis=-1, keepdims=True)
    m2 = jnp.mean(dxh * xh, axis=-1, keepdims=True)
    return rstd * (dxh - m1 - xh * m2)


def _rowsum(v):
    return jnp.sum(v, axis=0, keepdims=True)


def _row_tile(n, cap):
    if n <= cap:
        return n
    best = None
    for t in range(8, cap + 1, 8):
        if n % t == 0:
            best = t
    assert best is not None, (n, cap)
    return best


def _coords():
    x, y, c = lax.axis_index("x"), lax.axis_index("y"), lax.axis_index("c")
    return x, y, c


def _flip(v, bit):
    return 1 - v if bit else v


def small_allgather(blk, name):
    r, n = blk.shape

    def body(x_ref, out_ref, send_sems, recv_sems):
        x, y, c = _coords()
        me = 4 * x + 2 * y + c
        out_ref[me] = x_ref[...]
        copies = []
        for k in range(1, N_DEV):
            peer = (_flip(x, k & 4), _flip(y, k & 2), _flip(c, k & 1))
            cp = pltpu.make_async_remote_copy(
                src_ref=x_ref, dst_ref=out_ref.at[me], send_sem=send_sems.at[k - 1],
                recv_sem=recv_sems.at[k - 1], device_id=peer, device_id_type=MESH)
            cp.start()
            copies.append(cp)
        for k in range(1, N_DEV):
            px, py, pc = _flip(x, k & 4), _flip(y, k & 2), _flip(c, k & 1)
            slot = 4 * px + 2 * py + pc
            pltpu.make_async_remote_copy(
                src_ref=x_ref, dst_ref=out_ref.at[slot], send_sem=send_sems.at[k - 1],
                recv_sem=recv_sems.at[k - 1], device_id=(px, py, pc), device_id_type=MESH).wait_recv()
        for cp in copies:
            cp.wait_send()

    return pl.pallas_call(
        body, name=name,
        out_shape=SDS((N_DEV, r, n), blk.dtype),
        in_specs=[pl.BlockSpec(memory_space=pltpu.VMEM)],
        out_specs=pl.BlockSpec(memory_space=pltpu.VMEM),
        scratch_shapes=[pltpu.SemaphoreType.DMA((N_DEV - 1,)), pltpu.SemaphoreType.DMA((N_DEV - 1,))],
    )(blk)


def allgather_weights(shards, name):
    n = len(shards)
    per = 7

    def body(*refs):
        ins, outs = refs[:n], refs[n:2 * n]
        send_sems, recv_sems, local_sems = refs[2 * n:]
        x, y, c = _coords()
        me = 4 * x + 2 * y + c
        sibling = (x, y, 1 - c)
        chips = [(1 - x, y), (x, 1 - y), (1 - x, 1 - y)]

        def slot(px, py, pc):
            return 4 * px + 2 * py + pc

        def copy(t, k, block, to, src=None):
            dst = outs[t].at[slot(*block)]
            return pltpu.make_async_remote_copy(
                src_ref=dst if src is None else src, dst_ref=dst,
                send_sem=send_sems.at[per * t + k], recv_sem=recv_sems.at[per * t + k],
                device_id=to, device_id_type=MESH)

        local = []
        first = []
        for t in range(n):
            lc = pltpu.make_async_copy(ins[t], outs[t].at[me], local_sems.at[t])
            lc.start()
            local.append(lc)
            cp = copy(t, 0, (x, y, c), sibling, src=ins[t])
            cp.start()
            first.append(cp)
            for j, chip in enumerate(chips):
                cp = copy(t, 1 + j, (x, y, c), (*chip, c), src=ins[t])
                cp.start()
                first.append(cp)
        passed = []
        for t in range(n):
            for j, chip in enumerate(chips):
                copy(t, 1 + j, (*chip, c), (x, y, c)).wait_recv()
                cp = copy(t, 4 + j, (*chip, c), sibling)
                cp.start()
                passed.append(cp)
        for t in range(n):
            copy(t, 0, (x, y, 1 - c), (x, y, c)).wait_recv()
            for j, chip in enumerate(chips):
                copy(t, 4 + j, (*chip, 1 - c), (x, y, c)).wait_recv()
        for cp in first + passed:
            cp.wait_send()
        for lc in local:
            lc.wait()

    anyspec = pl.BlockSpec(memory_space=pl.ANY)
    return pl.pallas_call(
        body, name=name,
        out_shape=[SDS((N_DEV,) + s.shape, s.dtype) for s in shards],
        in_specs=[anyspec] * n, out_specs=[anyspec] * n,
        scratch_shapes=[pltpu.SemaphoreType.DMA((per * n,)), pltpu.SemaphoreType.DMA((per * n,)),
                        pltpu.SemaphoreType.DMA((n,))],
    )(*shards)


def exchange_sibling(grads, name):
    n = len(grads)

    def body(*refs):
        ins, outs = refs[:n], refs[n:2 * n]
        send_sems, recv_sems = refs[2 * n:]
        x, y, c = _coords()
        copies = []
        for t in range(n):
            cp = pltpu.make_async_remote_copy(
                src_ref=ins[t].at[:, 1 - c], dst_ref=outs[t], send_sem=send_sems.at[t],
                recv_sem=recv_sems.at[t], device_id=(x, y, 1 - c), device_id_type=MESH)
            cp.start()
            copies.append(cp)
        for cp in copies:
            cp.wait_recv()
        for cp in copies:
            cp.wait_send()

    anyspec = pl.BlockSpec(memory_space=pl.ANY)
    return pl.pallas_call(
        body, name=name,
        out_shape=[SDS((4,) + g.shape[2:], g.dtype) for g in grads],
        in_specs=[anyspec] * n, out_specs=[anyspec] * n,
        scratch_shapes=[pltpu.SemaphoreType.DMA((n,)), pltpu.SemaphoreType.DMA((n,))],
    )(*grads)


def exchange_chips(sums, name):
    n = len(sums)

    def body(*refs):
        ins, outs = refs[:n], refs[n:2 * n]
        send_sems, recv_sems = refs[2 * n:]
        x, y, c = _coords()
        copies = []
        for t in range(n):
            for j in range(1, 4):
                peer = (_flip(x, j & 2), _flip(y, j & 1), c)
                cp = pltpu.make_async_remote_copy(
                    src_ref=ins[t].at[j - 1], dst_ref=outs[t].at[j - 1], send_sem=send_sems.at[3 * t + j - 1],
                    recv_sem=recv_sems.at[3 * t + j - 1], device_id=peer, device_id_type=MESH)
                cp.start()
                copies.append(cp)
        for cp in copies:
            cp.wait_recv()
        for cp in copies:
            cp.wait_send()

    anyspec = pl.BlockSpec(memory_space=pl.ANY)
    return pl.pallas_call(
        body, name=name,
        out_shape=[SDS(s.shape, s.dtype) for s in sums],
        in_specs=[anyspec] * n, out_specs=[anyspec] * n,
        scratch_shapes=[pltpu.SemaphoreType.DMA((3 * n,)), pltpu.SemaphoreType.DMA((3 * n,))],
    )(*sums)


def chip_sum(g42, recv, qc, name):
    _, _, R, C = g42.shape
    tr = _row_tile(R, 256)

    def body(qc_ref, a_ref, b_ref, own_ref, send_ref):
        j = pl.program_id(1)
        s = a_ref[...] + b_ref[...]

        @pl.when(j == 0)
        def _():
            own_ref[...] = s

        @pl.when(j > 0)
        def _():
            send_ref[...] = s.astype(BF)

    gs = pltpu.PrefetchScalarGridSpec(
        num_scalar_prefetch=1, grid=(R // tr, 4),
        in_specs=[pl.BlockSpec((None, None, tr, C), lambda i, j, s: (jnp.bitwise_xor(s[0], j), s[1], i, 0)),
                  pl.BlockSpec((None, tr, C), lambda i, j, s: (jnp.bitwise_xor(s[0], j), i, 0))],
        out_specs=[pl.BlockSpec((tr, C), lambda i, j, s: (i, 0)),
                   pl.BlockSpec((None, tr, C), lambda i, j, s: (jnp.maximum(j - 1, 0), i, 0))])
    return pl.pallas_call(body, name=name, grid_spec=gs, out_shape=[SDS((R, C), F32), SDS((3, R, C), BF)],
                          compiler_params=_cparams(("arbitrary", "arbitrary")))(qc, g42, recv)


def small_sum(g8, name):
    def body(g_ref, o_ref):
        acc = g_ref[0]
        for k in range(1, N_DEV):
            acc = acc + g_ref[k]
        o_ref[...] = acc
    return pl.pallas_call(body, name=name, out_shape=SDS(g8.shape[1:], F32))(g8)


def adamw(w, m, v, parts, name):
    R, C = w.shape
    tr = _row_tile(R, 256)
    npart = len(parts)
    c1 = 1.0 / (1.0 - ADAM_B1 ** ADAM_STEP)
    c2 = 1.0 / (1.0 - ADAM_B2 ** ADAM_STEP)

    def body(*refs):
        w_ref, m_ref, v_ref = refs[:3]
        p_refs = refs[3:3 + npart]
        g_ref, d_ref, nm_ref, nv_ref = refs[3 + npart:]
        g = p_refs[0][...].astype(F32)
        for p in p_refs[1:]:
            g = g + p[...].astype(F32)
        nm = ADAM_B1 * m_ref[...] + (1.0 - ADAM_B1) * g
        nv = ADAM_B2 * v_ref[...] + (1.0 - ADAM_B2) * (g * g)
        mh = nm * c1
        vh = nv * c2
        g_ref[...] = g
        nm_ref[...] = nm
        nv_ref[...] = nv
        d_ref[...] = -ADAM_LR * (mh / (jnp.sqrt(vh) + ADAM_EPS) + ADAM_WD * w_ref[...])

    wspec = pl.BlockSpec((tr, C), lambda i: (i, 0))
    pspecs = [pl.BlockSpec(bs(tr, C), im) for (_, bs, im) in parts]
    outs = pl.pallas_call(
        body, name=name, grid=(R // tr,),
        in_specs=[wspec] * 3 + pspecs, out_specs=[wspec] * 4,
        out_shape=[SDS((R, C), F32)] * 4,
        compiler_params=_cparams(("parallel",)))(w, m, v, *[p[0] for p in parts])
    return outs


def _plain_part(g):
    return (g, lambda tr, C: (tr, C), lambda i: (i, 0))


def _slot_part(g, slot):
    return (g, lambda tr, C: (None, tr, C), lambda i, s=slot: (s, i, 0))


def ada_fwd(c_all, w_cols, b_cols, name):
    Bg, D = c_all.shape
    n = w_cols.shape[1]

    def body(c_ref, w_ref, b_ref, o_ref, s_ref):
        cc = c_ref[...]
        s = cc * _sig(cc)
        s_ref[...] = s
        o_ref[...] = jnp.dot(s, w_ref[...], preferred_element_type=F32, precision=lax.Precision.HIGHEST) + b_ref[...]

    return pl.pallas_call(body, name=name, out_shape=[SDS((Bg, n), F32), SDS((Bg, D), F32)],
                          compiler_params=_cparams())(c_all, w_cols, b_cols)


def ada_bwd(s_all, dmod_cols, dmod_all, name):
    Bg, D = s_all.shape
    n = dmod_cols.shape[1]

    def body(s_ref, dc_ref, da_ref, gw_ref, gb_ref):
        gw_ref[...] = lax.dot_general(s_ref[...], dc_ref[...], (((0,), (0,)), ((), ())),
                                      preferred_element_type=F32, precision=lax.Precision.HIGHEST)
        acc = da_ref[0:1, :]
        for r in range(1, Bg):
            acc = acc + da_ref[r:r + 1, :]
        gb_ref[...] = acc

    return pl.pallas_call(body, name=name, out_shape=[SDS((D, n), F32), SDS((1, dmod_all.shape[1]), F32)],
                          compiler_params=_cparams())(s_all, dmod_cols, dmod_all)


def _vec(D, rank):
    return pl.BlockSpec((1, D), (lambda i: (0, 0)) if rank == 1 else (lambda i, j: (0, 0)))


def _modspec(D, tpb, rank):
    if rank == 1:
        return pl.BlockSpec((None, 1, D), lambda i: (i // tpb, 0, 0))
    return pl.BlockSpec((None, 1, D), lambda i, j: (i // tpb, 0, 0))


def _resident(shape):
    return pl.BlockSpec(shape, lambda *_: (0,) * len(shape), pipeline_mode=pl.Buffered(1))


def ffn_up(xs, pg, pb, sc, sh, wg8, name):
    T, D = xs.shape
    n2, _, nb = wg8.shape
    nj = n2 // 2
    S = T // sc.shape[0]
    tm = min(512, S)
    tpb = S // tm

    def body(x_ref, pg_ref, pb_ref, sc_ref, sh_ref, w_ref, u_ref, a_ref, g_ref, p_ref):
        xin = x_ref[...] * pg_ref[...] + pb_ref[...]
        u_ref[...] = (xin * (1.0 + sc_ref[...]) + sh_ref[...]).astype(BF)

        def col_block(j, _):
            u = u_ref[...]
            a = _dot_nn(u, w_ref[j])
            g = _dot_nn(u, w_ref[j + nj])
            a_ref[j] = a.astype(BF)
            g_ref[j] = g.astype(BF)
            p_ref[j] = ((a * _sig(a)) * g).astype(BF)
            return 0

        lax.fori_loop(0, nj, col_block, 0)

    blk = pl.BlockSpec((nj, tm, nb), lambda i: (0, i, 0))
    row = pl.BlockSpec((tm, D), lambda i: (i, 0))
    return pl.pallas_call(
        body, name=name, grid=(T // tm,),
        in_specs=[row, _vec(D, 1), _vec(D, 1), _modspec(D, tpb, 1), _modspec(D, tpb, 1), _resident(wg8.shape)],
        out_specs=[row, blk, blk, blk],
        out_shape=[SDS((T, D), BF)] + [SDS((nj, T, nb), BF)] * 3,
        compiler_params=_cparams(("parallel",)))(xs, pg, pb, sc, sh, wg8)


def ffn_down_ln(p4, wd3, xs, pg, pb, gate, name):
    nj, T, nb = p4.shape
    D = wd3.shape[2]
    S = T // gate.shape[0]
    tm = min(512, S)
    tpb = S // tm

    def body(p_ref, wd_ref, x_ref, pg_ref, pb_ref, gate_ref, xh_ref, rs_ref, f_ref):
        f = _dot_nn(p_ref[0], wd_ref[0])
        for k in range(1, nj):
            f = f + _dot_nn(p_ref[k], wd_ref[k])
        xin = x_ref[...] * pg_ref[...] + pb_ref[...]
        r = DEEPNORM_ALPHA * xin + gate_ref[...] * (MACARON_WEIGHT * f)
        xh, rstd = _ln_stats(r)
        xh_ref[...] = xh
        rs_ref[...] = rstd
        f_ref[...] = f.astype(BF)

    row = pl.BlockSpec((tm, D), lambda i: (i, 0))
    return pl.pallas_call(
        body, name=name, grid=(T // tm,),
        in_specs=[pl.BlockSpec((nj, tm, nb), lambda i: (0, i, 0)), _resident(wd3.shape),
                  row, _vec(D, 1), _vec(D, 1), _modspec(D, tpb, 1)],
        out_specs=[row, pl.BlockSpec((tm, 1), lambda i: (i, 0)), row],
        out_shape=[SDS((T, D), F32), SDS((T, 1), F32), SDS((T, D), BF)],
        compiler_params=_cparams(("parallel",)))(p4, wd3, xs, pg, pb, gate)


def mod_matmul(xs, pg, pb, sc, sh, w, name):
    T, D = xs.shape
    N = w.shape[1]
    S = T // sc.shape[0]
    tm = min(256, S)
    tpb = S // tm

    def body(x_ref, pg_ref, pb_ref, sc_ref, sh_ref, w_ref, u_ref, o_ref):
        xin = x_ref[...] * pg_ref[...] + pb_ref[...]
        u = (xin * (1.0 + sc_ref[...]) + sh_ref[...]).astype(BF)
        u_ref[...] = u
        for n in range(N // D):
            o_ref[:, n * D:(n + 1) * D] = _dot_nn(u, w_ref[:, n * D:(n + 1) * D])

    row = pl.BlockSpec((tm, D), lambda i: (i, 0))
    return pl.pallas_call(
        body, name=name, grid=(T // tm,),
        in_specs=[row, _vec(D, 1), _vec(D, 1), _modspec(D, tpb, 1), _modspec(D, tpb, 1), _resident(w.shape)],
        out_specs=[row, pl.BlockSpec((tm, N), lambda i: (i, 0))],
        out_shape=[SDS((T, D), BF), SDS((T, N), F32)],
        compiler_params=_cparams(("parallel",)))(xs, pg, pb, sc, sh, w)


ATT_TQ = 1024
ATT_TK = 256


def _att_consts(tk):
    r = lax.broadcasted_iota(jnp.int32, (tk + 8, tk), 0)
    c = lax.broadcasted_iota(jnp.int32, (tk + 8, tk), 1)
    usum = jnp.where((r >= tk) | (c > r), 1.0, 0.0).astype(BF)
    lsum = jnp.where((r >= tk) | (c < r), 1.0, 0.0).astype(BF)
    dmask = lax.broadcasted_iota(jnp.int32, (tk, tk), 0) < lax.broadcasted_iota(jnp.int32, (tk, tk), 1)
    return usum, lsum, dmask


def _split_dot(m, v):
    hi = v.astype(BF)
    lo = (v - hi.astype(F32)).astype(BF)
    return _dot_nn(m, hi) + _dot_nn(m, lo)


def _softplus(z):
    return jnp.maximum(z, 0.0) + jnp.log(1.0 + jnp.exp(-jnp.abs(z)))


def _att_dims(S, D):
    dh = SB_HEAD_DIM
    cw = min(LANE, D)
    tq = min(ATT_TQ, S)
    tk = min(ATT_TK, tq)
    assert tq % tk == 0 and S % tq == 0
    return dh, cw, cw // dh, D // cw, tq, tk, S // tq, S // tk


def att_fwd(proj, Bl, S, D, name):
    dh, cw, hp, nblk, tq, tk, nq, nk = _att_dims(S, D)
    scale = 1.0 / math.sqrt(dh)
    assert math.log2(scale) == int(math.log2(scale))
    H = D // dh

    def body(q_ref, k_ref, v_ref, o_ref, car_ref, qs, ks, vts):
        usum, _, dmask = _att_consts(tk)
        for hh in range(hp):
            sl = slice(hh * dh, (hh + 1) * dh)
            qs[hh] = (q_ref[:, sl] * scale).astype(BF)
            ks[hh] = k_ref[:, sl].astype(BF)
            for kb in range(nk):
                vts[hh, kb] = v_ref[kb * tk:(kb + 1) * tk, sl].T.astype(BF)
        nch = tq // tk

        def qloop(qb, _):
            qo = pl.multiple_of(qb * tq, tq)
            n_full = qb * nch

            def blk(kb, state, diag):
                ko = pl.multiple_of(kb * tk, tk)
                chains = [(hh, c) for hh in range(hp) for c in range(0 if diag is None else diag, nch)]

                def masked(ch, val):
                    return jnp.where(dmask, val, 0.0) if ch[1] == diag else val

                z = {ch: _dot_nt(ks[ch[0], pl.ds(ko, tk), :], qs[ch[0], pl.ds(pl.multiple_of(qo + ch[1] * tk, tk), tk), :])
                     for ch in chains}
                sp = {ch: _softplus(z[ch]) for ch in chains}
                lk = {ch: masked(ch, -sp[ch]) for ch in chains}
                for hh, c in chains:
                    car_ref[hh, qb * nk + kb, :, c * tk:(c + 1) * tk] = state[hh][c][0]
                cs = {ch: _split_dot(usum, lk[ch]) for ch in chains}
                w = {ch: masked(ch, jnp.exp((z[ch] - sp[ch]) + state[ch[0]][ch[1]][0][0:1, :] + cs[ch][:tk]))
                     for ch in chains}
                pv = {ch: _dot_nn(vts[ch[0], kb], w[ch].astype(BF)) for ch in chains}
                return tuple(tuple(
                    (state[hh][c][0] + cs[(hh, c)][tk:], state[hh][c][1] + pv[(hh, c)]) if (hh, c) in z else state[hh][c]
                    for c in range(nch)) for hh in range(hp))

            state = tuple(tuple((jnp.zeros((8, tk), F32), jnp.zeros((dh, tk), F32)) for _ in range(nch))
                          for _ in range(hp))
            for i in reversed(range(nch)):
                state = blk(n_full + i, state, i)
            state = lax.fori_loop(0, n_full, lambda j, st: blk(n_full - 1 - j, st, None), state)
            for hh in range(hp):
                for c in range(nch):
                    o_ref[pl.ds(pl.multiple_of(qo + c * tk, tk), tk), hh * dh:(hh + 1) * dh] = (
                        state[hh][c][1].T.astype(BF))
            return 0

        lax.fori_loop(0, nq, qloop, 0)

    def seg(s):
        return pl.BlockSpec((S, cw), lambda b, h: (b, s * nblk + h))

    return pl.pallas_call(
        body, name=name, grid=(Bl, nblk),
        in_specs=[seg(0), seg(1), seg(2)],
        out_specs=[pl.BlockSpec((S, cw), lambda b, h: (b, h)),
                   pl.BlockSpec((None, hp, nq * nk, 8, tq), lambda b, h: (b, h, 0, 0, 0))],
        out_shape=[SDS((Bl * S, D), BF), SDS((Bl, H, nq * nk, 8, tq), F32)],
        scratch_shapes=[pltpu.VMEM((hp, S, dh), BF)] * 2 + [pltpu.VMEM((hp, nk, dh, tk), BF)],
        compiler_params=_cparams(("parallel", "parallel")))(proj, proj, proj)


def conv_fwd(proj, cw32, cb, cg, cbeta, Bl, S, D, kw, name):
    T = Bl * S
    ts = min(128, S)
    ns = S // ts
    off = HALO - (kw - 1)
    rc = min(64, ts)
    cw = min(LANE, D)

    def body(a_ref, b_ref, ha_ref, hb_ref, w_ref, cb_ref, g_ref, be_ref, cs_ref, xh_ref, rs_ref, hext, conv_s):
        i = pl.program_id(1)
        hext[pl.ds(HALO, ts), :] = a_ref[...] * _sig(b_ref[...])
        hh = ha_ref[...] * _sig(hb_ref[...])
        hext[pl.ds(0, HALO), :] = jnp.where(i == 0, 0.0, hh)
        for cb_ in range(D // cw):
            cols = slice(cb_ * cw, (cb_ + 1) * cw)
            accs = [jnp.zeros((rc, cw), F32) for _ in range(ts // rc)]
            for k in range(kw):
                wk = w_ref[k:k + 1, cols]
                for r in range(ts // rc):
                    accs[r] = accs[r] + wk * hext[pl.ds(r * rc + off + k, rc), cols]
            for r in range(ts // rc):
                conv_s[pl.ds(r * rc, rc), cols] = accs[r]
        conv = conv_s[...] + cb_ref[...]
        xh, rstd = _ln_stats(conv)
        xh_ref[...] = xh
        rs_ref[...] = rstd
        cl = xh * g_ref[...] + be_ref[...]
        cs_ref[...] = (cl * _sig(cl)).astype(BF)

    hpb = ts // HALO

    def tile(seg):
        return pl.BlockSpec((ts, D), lambda b, i: (b * ns + i, seg))

    def halo(seg):
        return pl.BlockSpec((HALO, D), lambda b, i: (jnp.maximum((b * ns + i) * hpb - 1, 0), seg))

    row = pl.BlockSpec((ts, D), lambda b, i: (b * ns + i, 0))
    vec = pl.BlockSpec((1, D), lambda b, i: (0, 0))
    return pl.pallas_call(
        body, name=name, grid=(Bl, ns),
        in_specs=[tile(3), tile(4), halo(3), halo(4), pl.BlockSpec((HALO, D), lambda b, i: (0, 0)), vec, vec, vec],
        out_specs=[row, row, pl.BlockSpec((ts, 1), lambda b, i: (b * ns + i, 0))],
        out_shape=[SDS((T, D), BF), SDS((T, D), F32), SDS((T, 1), F32)],
        scratch_shapes=[pltpu.VMEM((ts + HALO, D), F32), pltpu.VMEM((ts, D), F32)],
        compiler_params=_cparams(("parallel", "arbitrary")))(proj, proj, proj, proj, cw32, cb, cg, cbeta)


def mix_fwd(yatt, cs, proj, wsb, wco, wout, xs, pg, pb, gate, name):
    T, D = yatt.shape
    S = T // gate.shape[0]
    tm = min(256, S)
    tpb = S // tm

    def body(ya_ref, cs_ref, ga_ref, gb_ref, wsb_ref, wco_ref, wout_ref, x_ref, pg_ref, pb_ref, gate_ref,
             xh_ref, rs_ref, ysb_ref, yco_ref, mg_ref, o_ref):
        ysb = _dot_nn(ya_ref[...], wsb_ref[...])
        yco = _dot_nn(cs_ref[...], wco_ref[...])
        merged = _sig(ga_ref[...]) * ysb + _sig(gb_ref[...]) * yco
        mg = merged.astype(BF)
        o = _dot_nn(mg, wout_ref[...])
        xin = x_ref[...] * pg_ref[...] + pb_ref[...]
        r = DEEPNORM_ALPHA * xin + gate_ref[...] * o
        xh, rstd = _ln_stats(r)
        xh_ref[...] = xh
        rs_ref[...] = rstd
        ysb_ref[...] = ysb.astype(BF)
        yco_ref[...] = yco.astype(BF)
        mg_ref[...] = mg
        o_ref[...] = o.astype(BF)

    row = pl.BlockSpec((tm, D), lambda i: (i, 0))
    wfull = pl.BlockSpec((D, D), lambda i: (0, 0))
    return pl.pallas_call(
        body, name=name, grid=(T // tm,),
        in_specs=[row, row, pl.BlockSpec((tm, D), lambda i: (i, 5)), pl.BlockSpec((tm, D), lambda i: (i, 6)),
                  wfull, wfull, wfull, row, _vec(D, 1), _vec(D, 1), _modspec(D, tpb, 1)],
        out_specs=[row, pl.BlockSpec((tm, 1), lambda i: (i, 0)), row, row, row, row],
        out_shape=[SDS((T, D), F32), SDS((T, 1), F32)] + [SDS((T, D), BF)] * 4,
        compiler_params=_cparams(("parallel",)))(yatt, cs, proj, proj, wsb, wco, wout, xs, pg, pb, gate)


def ln_bwd(dout, xh, rstd, lng, lnb, gate, sub, res_w, name, target=None):
    T, D = xh.shape
    Bl = gate.shape[0]
    S = T // Bl
    tm = min(256, S)
    tpb = S // tm
    first = target is not None

    def body(*refs):
        if first:
            tg_ref, xh_ref, rs_ref, g_ref, b_ref, gate_ref, sub_ref = refs[:7]
            dr_ref, ds_ref, dg_ref, db_ref, dgate_ref, loss_ref = refs[7:]
        else:
            do_ref, xh_ref, rs_ref, g_ref, b_ref, gate_ref, sub_ref = refs[:7]
            dr_ref, ds_ref, dg_ref, db_ref, dgate_ref = refs[7:]
        i = pl.program_id(0)
        xh_ = xh_ref[...]
        if first:
            diff = (xh_ * g_ref[...] + b_ref[...]) - tg_ref[...]
            lsum = jnp.sum(jnp.sum(diff * diff, axis=1, keepdims=True), axis=0, keepdims=True) * (0.5 / D)
            do = diff * (1.0 / D)
        else:
            do = do_ref[...]

        @pl.when(i == 0)
        def _():
            dg_ref[...] = jnp.zeros_like(dg_ref)
            db_ref[...] = jnp.zeros_like(db_ref)
            if first:
                loss_ref[...] = jnp.zeros_like(loss_ref)

        @pl.when(i % tpb == 0)
        def _():
            dgate_ref[...] = jnp.zeros_like(dgate_ref)

        if first:
            loss_ref[...] += jnp.broadcast_to(lsum, loss_ref.shape)
        dg_ref[...] += _rowsum(do * xh_)
        db_ref[...] += _rowsum(do)
        dr = _ln_bwd(do * g_ref[...], xh_, rs_ref[...])
        dr_ref[...] = dr
        ds_ref[...] = (dr * gate_ref[...] * res_w).astype(BF)
        dgate_ref[...] += _rowsum(dr * (res_w * sub_ref[...].astype(F32)))

    row = pl.BlockSpec((tm, D), lambda i: (i, 0))
    vec = _vec(D, 1)
    mod = _modspec(D, tpb, 1)
    out_specs = [row, row, vec, vec, mod]
    out_shape = [SDS((T, D), F32), SDS((T, D), BF), SDS((1, D), F32), SDS((1, D), F32), SDS((Bl, 1, D), F32)]
    if first:
        out_specs.append(pl.BlockSpec((8, LANE), lambda i: (0, 0)))
        out_shape.append(SDS((8, LANE), F32))
    return pl.pallas_call(
        body, name=name, grid=(T // tm,),
        in_specs=[row, row, pl.BlockSpec((tm, 1), lambda i: (i, 0)), vec, vec, mod, row],
        out_specs=out_specs, out_shape=out_shape,
        compiler_params=_cparams(("arbitrary",)))(target if first else dout, xh, rstd, lng, lnb, gate, sub)


def swiglu_bwd(df, wd3, a4, g4, name):
    nj, T, nb = a4.shape
    D = df.shape[1]
    tm = min(512, T)

    def body(df_ref, wd_ref, a_ref, g_ref, o_ref):
        def col_block(j, _):
            dp = _dot_nt(df_ref[...], wd_ref[j])
            a = a_ref[j].astype(F32)
            g = g_ref[j].astype(F32)
            s = _sig(a)
            o_ref[0, j] = (dp * g * (s * (1.0 + a * (1.0 - s)))).astype(BF)
            o_ref[1, j] = (dp * (a * s)).astype(BF)
            return 0

        lax.fori_loop(0, nj, col_block, 0)

    blk = pl.BlockSpec((nj, tm, nb), lambda i: (0, i, 0))
    return pl.pallas_call(
        body, name=name, grid=(T // tm,),
        in_specs=[pl.BlockSpec((tm, D), lambda i: (i, 0)), _resident(wd3.shape), blk, blk],
        out_specs=pl.BlockSpec((2, nj, tm, nb), lambda i: (0, 0, i, 0)),
        out_shape=SDS((2, nj, T, nb), BF),
        compiler_params=_cparams(("parallel",)))(df, wd3, a4, g4)


def mod_bwd(dh, w, dr, xs, pg, pb, sc, blocked, name):
    T, D = dr.shape
    Bl = sc.shape[0]
    S = T // Bl
    tm = min(512 if blocked else 256, S)
    tpb = S // tm
    if blocked:
        nk, _, kb = dh.shape
        dh_spec = pl.BlockSpec((nk, tm, kb), lambda i: (0, i, 0))
    else:
        kb = D
        nk = dh.shape[1] // kb
        dh_spec = pl.BlockSpec((tm, nk * kb), lambda i: (i, 0))

    def body(dh_ref, w_ref, dr_ref, x_ref, pg_ref, pb_ref, sc_ref, dx_ref, dsc_ref, dsh_ref):
        i = pl.program_id(0)

        def part(k):
            if blocked:
                return _dot_nt(dh_ref[k], w_ref[k])
            return _dot_nt(dh_ref[:, k * kb:(k + 1) * kb], w_ref[:, k * kb:(k + 1) * kb])

        du = part(0)
        for k in range(1, nk):
            du = du + part(k)

        @pl.when(i % tpb == 0)
        def _():
            dsc_ref[...] = jnp.zeros_like(dsc_ref)
            dsh_ref[...] = jnp.zeros_like(dsh_ref)

        xin = x_ref[...] * pg_ref[...] + pb_ref[...]
        dx_ref[...] = DEEPNORM_ALPHA * dr_ref[...] + du * (1.0 + sc_ref[...])
        dsc_ref[...] += _rowsum(du * xin)
        dsh_ref[...] += _rowsum(du)

    row = pl.BlockSpec((tm, D), lambda i: (i, 0))
    mod = _modspec(D, tpb, 1)
    return pl.pallas_call(
        body, name=name, grid=(T // tm,),
        in_specs=[dh_spec, _resident(w.shape), row, row, _vec(D, 1), _vec(D, 1), mod],
        out_specs=[row, mod, mod],
        out_shape=[SDS((T, D), F32), SDS((Bl, 1, D), F32), SDS((Bl, 1, D), F32)],
        compiler_params=_cparams(("arbitrary",)))(dh, w, dr, xs, pg, pb, sc)


def merge_bwd(do2, proj, ysb, yco, wsb, wco, wout, name):
    T, D = do2.shape
    tm = min(256, T)

    def body(do_ref, ga_ref, gb_ref, ysb_ref, yco_ref, wsb_ref, wco_ref, wout_ref,
             dysb_ref, dyco_ref, dga_ref, dgb_ref, dya_ref, dcs_ref):
        dm = _dot_nt(do_ref[...], wout_ref[...])
        sa = _sig(ga_ref[...])
        sb = _sig(gb_ref[...])
        dysb = (dm * sa).astype(BF)
        dyco = (dm * sb).astype(BF)
        dysb_ref[...] = dysb
        dyco_ref[...] = dyco
        dga_ref[...] = (dm * ysb_ref[...].astype(F32) * (sa * (1.0 - sa))).astype(BF)
        dgb_ref[...] = (dm * yco_ref[...].astype(F32) * (sb * (1.0 - sb))).astype(BF)
        dya_ref[...] = _dot_nt(dysb, wsb_ref[...]).astype(BF)
        dcs_ref[...] = _dot_nt(dyco, wco_ref[...])

    row = pl.BlockSpec((tm, D), lambda i: (i, 0))
    wfull = pl.BlockSpec((D, D), lambda i: (0, 0))
    return pl.pallas_call(
        body, name=name, grid=(T // tm,),
        in_specs=[row, pl.BlockSpec((tm, D), lambda i: (i, 5)), pl.BlockSpec((tm, D), lambda i: (i, 6)),
                  row, row, wfull, wfull, wfull],
        out_specs=[row] * 6,
        out_shape=[SDS((T, D), BF)] * 5 + [SDS((T, D), F32)],
        compiler_params=_cparams(("parallel",)))(do2, proj, proj, ysb, yco, wsb, wco, wout)


def att_bwd(proj, dyatt, car, Bl, S, D, name):
    dh, cw, hp, nblk, tq, tk, nq, nk = _att_dims(S, D)
    scale = 1.0 / math.sqrt(dh)

    def body(q_ref, k_ref, v_ref, do_ref, car_ref, dq_ref, dk_ref, dv_ref, qs, ks, vs, dos, kts, dk_acc, dv_acc):
        usum, lsum, dmask = _att_consts(tk)
        for hh in range(hp):
            sl = slice(hh * dh, (hh + 1) * dh)
            qs[hh] = (q_ref[:, sl] * scale).astype(BF)
            ks[hh] = k_ref[:, sl].astype(BF)
            vs[hh] = v_ref[:, sl].astype(BF)
            dos[hh] = do_ref[:, sl]
            for kb in range(nk):
                kts[hh, kb] = k_ref[kb * tk:(kb + 1) * tk, sl].T.astype(BF)
        dk_acc[...] = jnp.zeros_like(dk_acc)
        dv_acc[...] = jnp.zeros_like(dv_acc)
        nch = tq // tk

        def qloop(qb, _):
            qo = pl.multiple_of(qb * tq, tq)
            n_full = qb * nch

            def blk(kb, state, diag):
                ko = pl.multiple_of(kb * tk, tk)
                chains = [(hh, c) for hh in range(hp) for c in range(0 if diag is None else diag, nch)]

                def masked(ch, val):
                    return jnp.where(dmask, val, 0.0) if ch[1] == diag else val

                def qrows(ref, ch):
                    return ref[ch[0], pl.ds(pl.multiple_of(qo + ch[1] * tk, tk), tk), :]

                k = [ks[hh, pl.ds(ko, tk), :] for hh in range(hp)]
                v = [vs[hh, pl.ds(ko, tk), :] for hh in range(hp)]
                z = {ch: _dot_nt(k[ch[0]], qrows(qs, ch)) for ch in chains}
                dw = {ch: _dot_nt(v[ch[0]], qrows(dos, ch)) for ch in chains}
                sp = {ch: _softplus(z[ch]) for ch in chains}
                lk = {ch: masked(ch, -sp[ch]) for ch in chains}
                cs = {ch: _split_dot(usum, lk[ch]) for ch in chains}
                w = {ch: masked(ch, jnp.exp((z[ch] - sp[ch])
                                            + car_ref[ch[0], qb * nk + kb, 0:1, ch[1] * tk:(ch[1] + 1) * tk]
                                            + cs[ch][:tk])) for ch in chains}
                dlw = {ch: dw[ch] * w[ch] for ch in chains}
                gs = {ch: _split_dot(lsum, dlw[ch]) for ch in chains}
                sg = {ch: jnp.exp(z[ch] - sp[ch]) for ch in chains}
                dzb = {ch: masked(ch, dlw[ch] * (1.0 - sg[ch])
                                  - sg[ch] * (state[ch[0]][ch[1]][0][0:1, :] + gs[ch][:tk])).astype(BF)
                       for ch in chains}
                wb = {ch: w[ch].astype(BF) for ch in chains}
                for hh in range(hp):
                    mine = [ch for ch in chains if ch[0] == hh]
                    dk_acc[hh, kb] += sum(_dot_nn(dzb[ch], qrows(qs, ch)) for ch in mine)
                    dv_acc[hh, kb] += sum(_dot_nn(wb[ch], qrows(dos, ch)) for ch in mine)
                dq = {ch: _dot_nn(kts[ch[0], kb], dzb[ch]) for ch in chains}
                return tuple(tuple(
                    (state[hh][c][0] + gs[(hh, c)][tk:], state[hh][c][1] + dq[(hh, c)]) if (hh, c) in z else state[hh][c]
                    for c in range(nch)) for hh in range(hp))

            state = tuple(tuple((jnp.zeros((8, tk), F32), jnp.zeros((dh, tk), F32)) for _ in range(nch))
                          for _ in range(hp))
            state = lax.fori_loop(0, n_full, lambda kb, st: blk(kb, st, None), state)
            for i in range(nch):
                state = blk(n_full + i, state, i)
            for hh in range(hp):
                for c in range(nch):
                    dq_ref[pl.ds(pl.multiple_of(qo + c * tk, tk), tk), hh * dh:(hh + 1) * dh] = (
                        (state[hh][c][1].T * scale).astype(BF))
            return 0

        lax.fori_loop(0, nq, qloop, 0)
        for hh in range(hp):
            sl = slice(hh * dh, (hh + 1) * dh)
            for kb in range(nk):
                dk_ref[kb * tk:(kb + 1) * tk, sl] = dk_acc[hh, kb].astype(BF)
                dv_ref[kb * tk:(kb + 1) * tk, sl] = dv_acc[hh, kb].astype(BF)

    def seg(s):
        return pl.BlockSpec((S, cw), lambda b, h: (b, s * nblk + h))

    blk_spec = pl.BlockSpec((S, cw), lambda b, h: (b, h))
    return pl.pallas_call(
        body, name=name, grid=(Bl, nblk),
        in_specs=[seg(0), seg(1), seg(2), blk_spec,
                  pl.BlockSpec((None, hp, nq * nk, 8, tq), lambda b, h: (b, h, 0, 0, 0))],
        out_specs=[blk_spec, blk_spec, blk_spec],
        out_shape=[SDS((Bl * S, D), BF)] * 3,
        scratch_shapes=[pltpu.VMEM((hp, S, dh), BF)] * 4 + [pltpu.VMEM((hp, nk, dh, tk), BF)]
        + [pltpu.VMEM((hp, nk, tk, dh), F32)] * 2,
        compiler_params=_cparams(("parallel", "parallel")))(proj, proj, proj, dyatt, car)


def conv_ln_bwd(dcs, xhc, rstd_c, cg, cbeta, name):
    T, D = dcs.shape
    tm = min(256, T)

    def body(dcs_ref, xh_ref, rs_ref, g_ref, b_ref, dconv_ref, dg_ref, db_ref, dcb_ref):
        @pl.when(pl.program_id(0) == 0)
        def _():
            dg_ref[...] = jnp.zeros_like(dg_ref)
            db_ref[...] = jnp.zeros_like(db_ref)
            dcb_ref[...] = jnp.zeros_like(dcb_ref)
        xh = xh_ref[...]
        cl = xh * g_ref[...] + b_ref[...]
        s = _sig(cl)
        dcl = dcs_ref[...] * (s * (1.0 + cl * (1.0 - s)))
        dg_ref[...] += _rowsum(dcl * xh)
        db_ref[...] += _rowsum(dcl)
        dconv = _ln_bwd(dcl * g_ref[...], xh, rs_ref[...])
        dconv_ref[...] = dconv
        dcb_ref[...] += _rowsum(dconv)

    row = pl.BlockSpec((tm, D), lambda i: (i, 0))
    vec = _vec(D, 1)
    return pl.pallas_call(
        body, name=name, grid=(T // tm,),
        in_specs=[row, row, pl.BlockSpec((tm, 1), lambda i: (i, 0)), vec, vec],
        out_specs=[row, vec, vec, vec],
        out_shape=[SDS((T, D), F32)] + [SDS((1, D), F32)] * 3,
        compiler_params=_cparams(("arbitrary",)))(dcs, xhc, rstd_c, cg, cbeta)


def conv_bwd(dconv, proj, cw32, Bl, S, D, kw, name):
    T = Bl * S
    ts = min(128, S)
    ns = S // ts
    off = HALO - (kw - 1)
    rc = min(64, ts)
    cw = min(LANE, D)
    hpb = ts // HALO
    nhb = T // HALO

    def body(dc_ref, dcn_ref, a_ref, b_ref, ha_ref, hb_ref, w_ref, da_ref, db_ref, dw_ref, hext, dext, dh_s):
        b_ = pl.program_id(0)
        i = pl.program_id(1)

        @pl.when((b_ == 0) & (i == 0))
        def _():
            dw_ref[...] = jnp.zeros_like(dw_ref)
        a = a_ref[...]
        sb = _sig(b_ref[...])
        hext[pl.ds(HALO, ts), :] = a * sb
        hh = ha_ref[...] * _sig(hb_ref[...])
        hext[pl.ds(0, HALO), :] = jnp.where(i == 0, 0.0, hh)
        dext[pl.ds(0, ts), :] = dc_ref[...]
        dext[pl.ds(ts, HALO), :] = jnp.where(i == ns - 1, 0.0, dcn_ref[...])
        for cb_ in range(D // cw):
            cols = slice(cb_ * cw, (cb_ + 1) * cw)
            accs = [jnp.zeros((rc, cw), F32) for _ in range(ts // rc)]
            for k in range(kw):
                wk = w_ref[k:k + 1, cols]
                wsum = jnp.zeros((rc, cw), F32)
                for r in range(ts // rc):
                    accs[r] = accs[r] + wk * dext[pl.ds(r * rc + (kw - 1) - k, rc), cols]
                    wsum = wsum + dext[pl.ds(r * rc, rc), cols] * hext[pl.ds(r * rc + off + k, rc), cols]
                dw_ref[k:k + 1, cols] += _rowsum(wsum)
            for r in range(ts // rc):
                dh_s[pl.ds(r * rc, rc), cols] = accs[r]
        dhc = dh_s[...]
        da_ref[...] = (dhc * sb).astype(BF)
        db_ref[...] = (dhc * a * (sb * (1.0 - sb))).astype(BF)

    def tile(seg):
        return pl.BlockSpec((ts, D), lambda b, i: (b * ns + i, seg))

    def halo(seg):
        return pl.BlockSpec((HALO, D), lambda b, i: (jnp.maximum((b * ns + i) * hpb - 1, 0), seg))

    row = pl.BlockSpec((ts, D), lambda b, i: (b * ns + i, 0))
    nxt = pl.BlockSpec((HALO, D), lambda b, i: (jnp.minimum((b * ns + i + 1) * hpb, nhb - 1), 0))
    wspec = pl.BlockSpec((HALO, D), lambda b, i: (0, 0))
    return pl.pallas_call(
        body, name=name, grid=(Bl, ns),
        in_specs=[row, nxt, tile(3), tile(4), halo(3), halo(4), wspec],
        out_specs=[row, row, wspec],
        out_shape=[SDS((T, D), BF), SDS((T, D), BF), SDS((HALO, D), F32)],
        scratch_shapes=[pltpu.VMEM((ts + HALO, D), F32), pltpu.VMEM((ts + HALO, D), F32), pltpu.VMEM((ts, D), F32)],
        compiler_params=_cparams(("arbitrary", "arbitrary")))(dconv, dconv, proj, proj, proj, proj, cw32)


def matmul_tn(xa, ga, x_spec, g_spec, out_shape, out_spec, acc_shape, grid, name):
    nk = grid[-1]

    def body(x_ref, g_ref, o_ref, acc):
        k = pl.program_id(len(grid) - 1)

        @pl.when(k == 0)
        def _():
            acc[...] = jnp.zeros_like(acc)
        acc[...] += _dot_tn(x_ref[...], g_ref[...])

        @pl.when(k == nk - 1)
        def _():
            o_ref[...] = acc[...]

    return pl.pallas_call(
        body, name=name, grid=grid, in_specs=[x_spec, g_spec], out_specs=out_spec,
        out_shape=SDS(out_shape, F32), scratch_shapes=[pltpu.VMEM(acc_shape, F32)],
        compiler_params=_cparams(("parallel",) * (len(grid) - 1) + ("arbitrary",)))(xa, ga)


def wgrad_std(xa, ga, name):
    T, M = xa.shape
    N = ga.shape[1]
    tk = min(1024, T)
    return matmul_tn(xa, ga, pl.BlockSpec((tk, M), lambda n, k: (k, 0)), pl.BlockSpec((tk, N), lambda n, k: (k, 0)),
                     (M, N), pl.BlockSpec((M, N), lambda n, k: (0, 0)), (M, N), (1, T // tk), name)


def wgrad_down(p4, df, name):
    nj, T, nb = p4.shape
    D = df.shape[1]
    tk = min(1024, T)
    return matmul_tn(p4, df, pl.BlockSpec((None, tk, nb), lambda j, k: (j, k, 0)),
                     pl.BlockSpec((tk, D), lambda j, k: (k, 0)),
                     (nj * nb, D), pl.BlockSpec((nb, D), lambda j, k: (j, 0)), (nb, D), (nj, T // tk), name)


def wgrad_gu(u, dh8, name):
    n8, T, nb = dh8.shape
    D = u.shape[1]
    tk = min(1024, T)
    return matmul_tn(u, dh8, pl.BlockSpec((tk, D), lambda j, k: (k, 0)),
                     pl.BlockSpec((None, tk, nb), lambda j, k: (j, k, 0)),
                     (n8, D, nb), pl.BlockSpec((None, D, nb), lambda j, k: (j, 0, 0)), (D, nb), (n8, T // tk), name)


def wgrad_in(u, dproj, name):
    T, D = u.shape
    bw = dproj.shape[1] // N_DEV
    tk = min(1024, T)
    return matmul_tn(u, dproj, pl.BlockSpec((tk, D), lambda j, k: (k, 0)),
                     pl.BlockSpec((tk, bw), lambda j, k: (k, j)),
                     (N_DEV, D, bw), pl.BlockSpec((None, D, bw), lambda j, k: (j, 0, 0)), (D, bw),
                     (N_DEV, T // tk), name)


def kernel(x, c, w_ada, b_ada, ffn1_w_gu, ffn1_w_down, ln1_g, ln1_b, w_in, w_sb_out, conv_w, conv_b, conv_ln_g, conv_ln_b, w_conv_out, w_out, ln2_g, ln2_b, ffn2_w_gu, ffn2_w_down, ln3_g, ln3_b, loss_target, m_w_ada, m_b_ada, m_ffn1_w_gu, m_ffn1_w_down, m_ln1_g, m_ln1_b, m_w_in, m_w_sb_out, m_conv_w, m_conv_b, m_conv_ln_g, m_conv_ln_b, m_w_conv_out, m_w_out, m_ln2_g, m_ln2_b, m_ffn2_w_gu, m_ffn2_w_down, m_ln3_g, m_ln3_b, v_w_ada, v_b_ada, v_ffn1_w_gu, v_ffn1_w_down, v_ln1_g, v_ln1_b, v_w_in, v_w_sb_out, v_conv_w, v_conv_b, v_conv_ln_g, v_conv_ln_b, v_w_conv_out, v_w_out, v_ln2_g, v_ln2_b, v_ffn2_w_gu, v_ffn2_w_down, v_ln3_g, v_ln3_b):
    Bl, S, D = x.shape
    T = Bl * S
    kw = conv_w.shape[1]
    ax, ay, ac = lax.axis_index("x"), lax.axis_index("y"), lax.axis_index("c")
    me = 4 * ax + 2 * ay + ac
    qc = jnp.stack([2 * ax + ay, ac]).astype(jnp.int32)

    big = dict(ffn1_w_gu=ffn1_w_gu[0], ffn1_w_down=ffn1_w_down[0], w_in=w_in[0], w_sb_out=w_sb_out[0],
               w_conv_out=w_conv_out[0], w_out=w_out[0], ffn2_w_gu=ffn2_w_gu[0], ffn2_w_down=ffn2_w_down[0])
    big_m = dict(ffn1_w_gu=m_ffn1_w_gu[0], ffn1_w_down=m_ffn1_w_down[0], w_in=m_w_in[0], w_sb_out=m_w_sb_out[0],
                 w_conv_out=m_w_conv_out[0], w_out=m_w_out[0], ffn2_w_gu=m_ffn2_w_gu[0], ffn2_w_down=m_ffn2_w_down[0])
    big_v = dict(ffn1_w_gu=v_ffn1_w_gu[0], ffn1_w_down=v_ffn1_w_down[0], w_in=v_w_in[0], w_sb_out=v_w_sb_out[0],
                 w_conv_out=v_w_conv_out[0], w_out=v_w_out[0], ffn2_w_gu=v_ffn2_w_gu[0], ffn2_w_down=v_ffn2_w_down[0])
    names = list(big)

    gathered = allgather_weights([big[n].astype(BF) for n in names], "allgather_weights")
    G = dict(zip(names, gathered))
    wg1 = G["ffn1_w_gu"]
    wg2 = G["ffn2_w_gu"]
    wd1 = G["ffn1_w_down"].reshape(wg1.shape[0] // 2, wg1.shape[2], D)
    wd2 = G["ffn2_w_down"].reshape(wg2.shape[0] // 2, wg2.shape[2], D)
    win = jnp.transpose(G["w_in"], (1, 0, 2)).reshape(D, -1)
    wsb = G["w_sb_out"].reshape(D, D)
    wco = G["w_conv_out"].reshape(-1, D)
    wout = G["w_out"].reshape(D, D)

    cw8 = small_allgather(conv_w[0], "allgather_conv_w")
    cw_full = jnp.transpose(cw8, (1, 0, 2)).reshape(kw, D)
    cw32 = jnp.concatenate([cw_full, jnp.zeros((HALO - kw, D), F32)], axis=0)

    c_all = small_allgather(c, "allgather_c").reshape(N_DEV * Bl, D)
    ncol = w_ada.shape[2]
    b_cols = lax.dynamic_slice(b_ada, (0, me * ncol), (1, ncol))
    mod_cols, s_all = ada_fwd(c_all, w_ada[0], b_cols, "ada_fwd")
    mod8 = small_allgather(mod_cols, "allgather_mod")
    mod_mine = lax.dynamic_slice(mod8, (0, me * Bl, 0), (N_DEV, Bl, ncol))
    mod = jnp.transpose(mod_mine, (1, 0, 2)).reshape(Bl, N_MOD_ROWS, 1, D)
    sh1, sc1, g1, sh2, sc2, g2, sh3, sc3, g3 = [mod[:, i] for i in range(N_MOD_ROWS)]

    ones = jnp.ones((1, D), F32)
    zeros = jnp.zeros((1, D), F32)
    xf = x.reshape(T, D)
    tgt = loss_target.reshape(T, D)

    u1, a1, gg1, p1 = ffn_up(xf, ones, zeros, sc1, sh1, wg1, "ffn1_up")
    xh1, rs1, f1 = ffn_down_ln(p1, wd1, xf, ones, zeros, g1, "ffn1_down_ln")
    u2, proj = mod_matmul(xh1, ln1_g, ln1_b, sc2, sh2, win, "in_proj")
    yatt, car = att_fwd(proj, Bl, S, D, "att_fwd")
    cs, xhc, rsc = conv_fwd(proj, cw32, conv_b, conv_ln_g, conv_ln_b, Bl, S, D, kw, "conv_fwd")
    xh2, rs2, ysb, yco, merged, o2 = mix_fwd(yatt, cs, proj, wsb, wco, wout, xh1, ln1_g, ln1_b, g2, "mix_fwd")
    u3, a3, gg3, p3 = ffn_up(xh2, ln2_g, ln2_b, sc3, sh3, wg2, "ffn2_up")
    xh3, rs3, f3 = ffn_down_ln(p3, wd2, xh2, ln2_g, ln2_b, g3, "ffn2_down_ln")

    dr3, df3, dln3g, dln3b, dg3, lossp = ln_bwd(None, xh3, rs3, ln3_g, ln3_b, g3, f3, MACARON_WEIGHT,
                                                 "ln3_bwd", target=tgt)
    dh3 = swiglu_bwd(df3, wd2, a3, gg3, "ffn2_swiglu_bwd").reshape((-1,) + a3.shape[1:])
    gw_d2 = wgrad_down(p3, df3, "wgrad_ffn2_down")
    gw_g2 = wgrad_gu(u3, dh3, "wgrad_ffn2_gu")
    dx2, dsc3, dsh3 = mod_bwd(dh3, wg2, dr3, xh2, ln2_g, ln2_b, sc3, True, "ffn2_mod_bwd")

    dr2, do2, dln2g, dln2b, dg2 = ln_bwd(dx2, xh2, rs2, ln2_g, ln2_b, g2, o2, 1.0, "ln2_bwd")
    gw_out = wgrad_std(merged, do2, "wgrad_out")
    dysb, dyco, dga, dgb, dyatt, dcs = merge_bwd(do2, proj, ysb, yco, wsb, wco, wout, "merge_bwd")
    gw_sb = wgrad_std(yatt, dysb, "wgrad_sb")
    gw_co = wgrad_std(cs, dyco, "wgrad_conv_out")
    dq, dk, dv = att_bwd(proj, dyatt, car, Bl, S, D, "att_bwd")
    dconv, dcg, dcbeta, dcb = conv_ln_bwd(dcs, xhc, rsc, conv_ln_g, conv_ln_b, "conv_ln_bwd")
    dglu_a, dglu_b, dcw = conv_bwd(dconv, proj, cw32, Bl, S, D, kw, "conv_bwd")
    dproj = jnp.concatenate([dq, dk, dv, dglu_a, dglu_b, dga, dgb], axis=1)
    gw_in = wgrad_in(u2, dproj, "wgrad_in")
    dx1, dsc2, dsh2 = mod_bwd(dproj, win, dr2, xh1, ln1_g, ln1_b, sc2, False, "mix_mod_bwd")

    dr1, df1, dln1g, dln1b, dg1 = ln_bwd(dx1, xh1, rs1, ln1_g, ln1_b, g1, f1, MACARON_WEIGHT, "ln1_bwd")
    dh1 = swiglu_bwd(df1, wd1, a1, gg1, "ffn1_swiglu_bwd").reshape((-1,) + a1.shape[1:])
    gw_d1 = wgrad_down(p1, df1, "wgrad_ffn1_down")
    gw_g1 = wgrad_gu(u1, dh1, "wgrad_ffn1_gu")
    grad_x, dsc1, dsh1 = mod_bwd(dh1, wg1, dr1, xf, ones, zeros, sc1, True, "ffn1_mod_bwd")

    dmod = jnp.concatenate([dsh1, dsc1, dg1, dsh2, dsc2, dg2, dsh3, dsc3, dg3], axis=1).reshape(Bl, N_MOD_ROWS * D)
    dmod_all = small_allgather(dmod, "allgather_dmod").reshape(N_DEV * Bl, N_MOD_ROWS * D)
    dmod_cols = lax.dynamic_slice(dmod_all, (0, me * ncol), (N_DEV * Bl, ncol))
    g_w_ada, g_b_ada = ada_bwd(s_all, dmod_cols, dmod_all, "ada_bwd")

    npad = 16
    small_rows = [dln1g, dln1b, dcb, dcg, dcbeta, dln2g, dln2b, dln3g, dln3b,
                  jnp.broadcast_to(lossp[0:1, 0:1], (1, D))]
    pack = jnp.concatenate(small_rows + [jnp.zeros((npad - len(small_rows), D), F32), dcw], axis=0)
    small = small_sum(small_allgather(pack, "allgather_small"), "small_sum")
    loss = small[9, 0]
    small_w = [ln1_g, ln1_b, conv_b, conv_ln_g, conv_ln_b, ln2_g, ln2_b, ln3_g, ln3_b]
    small_m = [m_ln1_g, m_ln1_b, m_conv_b, m_conv_ln_g, m_conv_ln_b, m_ln2_g, m_ln2_b, m_ln3_g, m_ln3_b]
    small_v = [v_ln1_g, v_ln1_b, v_conv_b, v_conv_ln_g, v_conv_ln_b, v_ln2_g, v_ln2_b, v_ln3_g, v_ln3_b]
    padrows = jnp.zeros((npad - len(small_w), D), F32)
    s_g, s_d, s_m, s_v = adamw(jnp.concatenate(small_w + [padrows], axis=0),
                               jnp.concatenate(small_m + [padrows], axis=0),
                               jnp.concatenate(small_v + [padrows], axis=0),
                               [_plain_part(small[:npad])], "adamw_small")
    dcw_mine = lax.dynamic_slice(small[npad:npad + kw], (0, me * (D // N_DEV)), (kw, D // N_DEV))
    cw_g, cw_d, cw_m, cw_v = adamw(conv_w[0], m_conv_w[0], v_conv_w[0], [_plain_part(dcw_mine)], "adamw_conv_w")
    ada_g, ada_d, ada_m, ada_v = adamw(w_ada[0], m_w_ada[0], v_w_ada[0], [_plain_part(g_w_ada)], "adamw_w_ada")
    bada_g, bada_d, bada_m, bada_v = adamw(b_ada, m_b_ada, v_b_ada, [_plain_part(g_b_ada)], "adamw_b_ada")

    gfull = dict(ffn1_w_gu=gw_g1, ffn1_w_down=gw_d1, w_in=gw_in, w_sb_out=gw_sb, w_conv_out=gw_co, w_out=gw_out,
                 ffn2_w_gu=gw_g2, ffn2_w_down=gw_d2)
    g42 = [gfull[n].reshape((4, 2) + big[n].shape) for n in names]
    recv_a = exchange_sibling(g42, "rs_sibling")
    sums = [chip_sum(g, r, qc, "chip_sum_" + n) for g, r, n in zip(g42, recv_a, names)]
    recv_b = exchange_chips([s[1] for s in sums], "rs_chips")
    res = {}
    for n, (s_own, _), rb in zip(names, sums, recv_b):
        parts = [_plain_part(s_own), _slot_part(rb, 0), _slot_part(rb, 1), _slot_part(rb, 2)]
        res[n] = adamw(big[n], big_m[n], big_v[n], parts, "adamw_" + n)

    def small_out(k):
        order = dict(ln1_g=0, ln1_b=1, conv_b=2, conv_ln_g=3, conv_ln_b=4, ln2_g=5, ln2_b=6, ln3_g=7, ln3_b=8)
        return lambda arr: arr[order[k]:order[k] + 1]

    weight_order = ["w_ada", "b_ada", "ffn1_w_gu", "ffn1_w_down", "ln1_g", "ln1_b", "w_in", "w_sb_out", "conv_w",
                    "conv_b", "conv_ln_g", "conv_ln_b", "w_conv_out", "w_out", "ln2_g", "ln2_b", "ffn2_w_gu",
                    "ffn2_w_down", "ln3_g", "ln3_b"]

    shapes = dict(w_ada=w_ada.shape, b_ada=b_ada.shape, conv_w=conv_w.shape, ln1_g=ln1_g.shape,
                  **{n: (1,) + big[n].shape for n in names})

    def pick(which):
        outs = []
        for n in weight_order:
            if n == "w_ada":
                a = (ada_g, ada_d, ada_m, ada_v)[which]
            elif n == "b_ada":
                a = (bada_g, bada_d, bada_m, bada_v)[which]
            elif n == "conv_w":
                a = (cw_g, cw_d, cw_m, cw_v)[which]
            elif n in res:
                a = res[n][which]
            else:
                a = small_out(n)((s_g, s_d, s_m, s_v)[which])
            outs.append(a.reshape(shapes.get(n, ln1_g.shape)))
        return outs

    return (loss, grad_x.reshape(Bl, S, D), *pick(0), *pick(1), *pick(2), *pick(3))
```

```python
import functools
import math

import jax
import jax.numpy as jnp
from jax import lax
from jax.experimental import pallas as pl
from jax.experimental.pallas import tpu as pltpu

F32 = jnp.float32
BF = jnp.bfloat16
SDS = jax.ShapeDtypeStruct
MESH = pl.DeviceIdType.MESH

N_DEV = 8
SB_HEAD_DIM = 64
N_MOD_ROWS = 9
MACARON_WEIGHT = 0.5
DEEPNORM_ALPHA = 2.0 ** 0.25
LN_EPS = 1e-5
ADAM_LR = 0.001
ADAM_B1 = 0.9
ADAM_B2 = 0.999
ADAM_EPS = 1e-08
ADAM_WD = 0.01
ADAM_STEP = 10

V7X_VMEM_LIMIT = 52 * 1024 * 1024
LANE = 128
HALO = 32


def _cparams(sem=None):
    return pltpu.CompilerParams(dimension_semantics=sem, vmem_limit_bytes=V7X_VMEM_LIMIT)


def _dot_nn(a, b):
    return lax.dot_general(a, b, (((1,), (0,)), ((), ())), preferred_element_type=F32)


def _dot_nt(a, b):
    return lax.dot_general(a, b, (((1,), (1,)), ((), ())), preferred_element_type=F32)


def _dot_tn(a, b):
    return lax.dot_general(a, b, (((0,), (0,)), ((), ())), preferred_element_type=F32)


def _sig(x):
    return 1.0 / (1.0 + jnp.exp(-x))


def _ln_stats(r):
    mu = jnp.mean(r, axis=-1, keepdims=True)
    d = r - mu
    var = jnp.mean(d * d, axis=-1, keepdims=True)
    rstd = lax.rsqrt(var + LN_EPS)
    return d * rstd, rstd


def _ln_bwd(dxh, xh, rstd):
    m1 = jnp.mean(dxh, axis=-1, keepdims=True)
    m2 = jnp.mean(dxh * xh, axis=-1, keepdims=True)
    return rstd * (dxh - m1 - xh * m2)


def _rowsum(v):
    return jnp.sum(v, axis=0, keepdims=True)


def _row_tile(n, cap):
    if n <= cap:
        return n
    best = None
    for t in range(8, cap + 1, 8):
        if n % t == 0:
            best = t
    assert best is not None, (n, cap)
    return best


def _coords():
    x, y, c = lax.axis_index("x"), lax.axis_index("y"), lax.axis_index("c")
    return x, y, c


def _flip(v, bit):
    return 1 - v if bit else v


def small_allgather(blk, name):
    r, n = blk.shape

    def body(x_ref, out_ref, send_sems, recv_sems):
        x, y, c = _coords()
        me = 4 * x + 2 * y + c
        out_ref[me] = x_ref[...]
        copies = []
        for k in range(1, N_DEV):
            peer = (_flip(x, k & 4), _flip(y, k & 2), _flip(c, k & 1))
            cp = pltpu.make_async_remote_copy(
                src_ref=x_ref, dst_ref=out_ref.at[me], send_sem=send_sems.at[k - 1],
                recv_sem=recv_sems.at[k - 1], device_id=peer, device_id_type=MESH)
            cp.start()
            copies.append(cp)
        for k in range(1, N_DEV):
            px, py, pc = _flip(x, k & 4), _flip(y, k & 2), _flip(c, k & 1)
            slot = 4 * px + 2 * py + pc
            pltpu.make_async_remote_copy(
                src_ref=x_ref, dst_ref=out_ref.at[slot], send_sem=send_sems.at[k - 1],
                recv_sem=recv_sems.at[k - 1], device_id=(px, py, pc), device_id_type=MESH).wait_recv()
        for cp in copies:
            cp.wait_send()

    return pl.pallas_call(
        body, name=name,
        out_shape=SDS((N_DEV, r, n), blk.dtype),
        in_specs=[pl.BlockSpec(memory_space=pltpu.VMEM)],
        out_specs=pl.BlockSpec(memory_space=pltpu.VMEM),
        scratch_shapes=[pltpu.SemaphoreType.DMA((N_DEV - 1,)), pltpu.SemaphoreType.DMA((N_DEV - 1,))],
    )(blk)


class CommPlan:
    def __init__(self, ins, out_shape, scratch, emit):
        self.ins, self.out_shape, self.scratch, self.emit = list(ins), list(out_shape), list(scratch), emit


def _phase(step, at, fn):
    if step is None:
        fn()
    else:
        pl.when(step == at)(fn)


def gather_plan(shards):
    n = len(shards)
    per = 7

    def emit(ins, outs, sems, step, nsteps):
        send_sems, recv_sems, local_sems = sems
        x, y, c = _coords()
        me = 4 * x + 2 * y + c
        sibling = (x, y, 1 - c)
        chips = [(1 - x, y), (x, 1 - y), (1 - x, 1 - y)]

        def slot(px, py, pc):
            return 4 * px + 2 * py + pc

        def copy(t, k, block, to, src=None):
            dst = outs[t].at[slot(*block)]
            return pltpu.make_async_remote_copy(
                src_ref=dst if src is None else src, dst_ref=dst,
                send_sem=send_sems.at[per * t + k], recv_sem=recv_sems.at[per * t + k],
                device_id=to, device_id_type=MESH)

        def local(t):
            return pltpu.make_async_copy(ins[t], outs[t].at[me], local_sems.at[t])

        def first(t):
            return [copy(t, 0, (x, y, c), sibling, src=ins[t])] + [
                copy(t, 1 + j, (x, y, c), (*chip, c), src=ins[t]) for j, chip in enumerate(chips)]

        def passed(t):
            return [copy(t, 4 + j, (*chip, c), sibling) for j, chip in enumerate(chips)]

        def start():
            for t in range(n):
                local(t).start()
                for cp in first(t):
                    cp.start()

        def forward():
            for t in range(n):
                for j, chip in enumerate(chips):
                    copy(t, 1 + j, (*chip, c), (x, y, c)).wait_recv()
                    passed(t)[j].start()

        def finish():
            for t in range(n):
                copy(t, 0, (x, y, 1 - c), (x, y, c)).wait_recv()
                for j, chip in enumerate(chips):
                    copy(t, 4 + j, (*chip, 1 - c), (x, y, c)).wait_recv()
            for t in range(n):
                for cp in first(t) + passed(t):
                    cp.wait_send()
                local(t).wait()

        _phase(step, 0, start)
        _phase(step, None if step is None else nsteps // 2, forward)
        _phase(step, None if step is None else nsteps - 1, finish)

    return CommPlan(shards, [SDS((N_DEV,) + s.shape, s.dtype) for s in shards],
                    [pltpu.SemaphoreType.DMA((per * n,)), pltpu.SemaphoreType.DMA((per * n,)),
                     pltpu.SemaphoreType.DMA((n,))], emit)


def chips_plan(sums):
    n = len(sums)

    def emit(ins, outs, sems, step, nsteps):
        send_sems, recv_sems = sems
        x, y, c = _coords()

        def copies():
            return [pltpu.make_async_remote_copy(
                src_ref=ins[t].at[j - 1], dst_ref=outs[t].at[j - 1], send_sem=send_sems.at[3 * t + j - 1],
                recv_sem=recv_sems.at[3 * t + j - 1], device_id=(_flip(x, j & 2), _flip(y, j & 1), c),
                device_id_type=MESH) for t in range(n) for j in range(1, 4)]

        def start():
            for cp in copies():
                cp.start()

        def finish():
            for cp in copies():
                cp.wait_recv()
            for cp in copies():
                cp.wait_send()

        _phase(step, 0, start)
        _phase(step, None if step is None else nsteps - 1, finish)

    return CommPlan(sums, [SDS(s.shape, s.dtype) for s in sums],
                    [pltpu.SemaphoreType.DMA((3 * n,)), pltpu.SemaphoreType.DMA((3 * n,))], emit)


def run_comm(plan, name):
    n, m = len(plan.ins), len(plan.out_shape)

    def body(*refs):
        plan.emit(refs[:n], refs[n:n + m], refs[n + m:], None, 1)

    anyspec = pl.BlockSpec(memory_space=pl.ANY)
    return pl.pallas_call(body, name=name, out_shape=plan.out_shape, in_specs=[anyspec] * n,
                          out_specs=[anyspec] * m, scratch_shapes=plan.scratch)(*plan.ins)


class _Rider:
    def __init__(self, plan):
        self.plan = plan
        anyspec = pl.BlockSpec(memory_space=pl.ANY)
        self.ins = plan.ins if plan else []
        self.in_specs = [anyspec] * len(self.ins)
        self.out_specs = [anyspec] * (len(plan.out_shape) if plan else 0)
        self.out_shape = plan.out_shape if plan else []
        self.scratch = plan.scratch if plan else []

    def split(self, refs, n_in, n_out, n_scratch=0):
        ni, no = len(self.ins), len(self.out_shape)
        own_in = refs[:n_in]
        c_in = refs[n_in:n_in + ni]
        own_out = refs[n_in + ni:n_in + ni + n_out]
        c_out = refs[n_in + ni + n_out:n_in + ni + n_out + no]
        rest = refs[n_in + ni + n_out + no:]
        own_scr, c_scr = rest[:n_scratch], rest[n_scratch:]

        def ride(step, nsteps):
            if self.plan:
                self.plan.emit(c_in, c_out, c_scr, step, nsteps)

        return tuple(own_in) + tuple(own_out) + tuple(own_scr), ride

    def result(self, outs, n_out):
        outs = list(outs) if isinstance(outs, (list, tuple)) else [outs]
        return outs[:n_out], (outs[n_out:] if self.plan else None)

    def sem(self, sem):
        return tuple("arbitrary" for _ in sem) if self.plan else sem


def exchange_sibling(grads, name):
    n = len(grads)

    def body(*refs):
        ins, outs = refs[:n], refs[n:2 * n]
        send_sems, recv_sems = refs[2 * n:]
        x, y, c = _coords()
        copies = []
        for t in range(n):
            cp = pltpu.make_async_remote_copy(
                src_ref=ins[t].at[:, 1 - c], dst_ref=outs[t], send_sem=send_sems.at[t],
                recv_sem=recv_sems.at[t], device_id=(x, y, 1 - c), device_id_type=MESH)
            cp.start()
            copies.append(cp)
        for cp in copies:
            cp.wait_recv()
        for cp in copies:
            cp.wait_send()

    anyspec = pl.BlockSpec(memory_space=pl.ANY)
    return pl.pallas_call(
        body, name=name,
        out_shape=[SDS((4,) + g.shape[2:], g.dtype) for g in grads],
        in_specs=[anyspec] * n, out_specs=[anyspec] * n,
        scratch_shapes=[pltpu.SemaphoreType.DMA((n,)), pltpu.SemaphoreType.DMA((n,))],
    )(*grads)


def chip_sum(g42, recv, qc, name):
    _, _, R, C = g42.shape
    tr = _row_tile(R, 256)

    def body(qc_ref, a_ref, b_ref, own_ref, send_ref):
        j = pl.program_id(1)
        s = a_ref[...] + b_ref[...]

        @pl.when(j == 0)
        def _():
            own_ref[...] = s

        @pl.when(j > 0)
        def _():
            send_ref[...] = s.astype(BF)

    gs = pltpu.PrefetchScalarGridSpec(
        num_scalar_prefetch=1, grid=(R // tr, 4),
        in_specs=[pl.BlockSpec((None, None, tr, C), lambda i, j, s: (jnp.bitwise_xor(s[0], j), s[1], i, 0)),
                  pl.BlockSpec((None, tr, C), lambda i, j, s: (jnp.bitwise_xor(s[0], j), i, 0))],
        out_specs=[pl.BlockSpec((tr, C), lambda i, j, s: (i, 0)),
                   pl.BlockSpec((None, tr, C), lambda i, j, s: (jnp.maximum(j - 1, 0), i, 0))])
    return pl.pallas_call(body, name=name, grid_spec=gs, out_shape=[SDS((R, C), F32), SDS((3, R, C), BF)],
                          compiler_params=_cparams(("arbitrary", "arbitrary")))(qc, g42, recv)


def small_sum(g8, name):
    def body(g_ref, o_ref):
        acc = g_ref[0]
        for k in range(1, N_DEV):
            acc = acc + g_ref[k]
        o_ref[...] = acc
    return pl.pallas_call(body, name=name, out_shape=SDS(g8.shape[1:], F32))(g8)


def adamw(w, m, v, parts, name):
    R, C = w.shape
    tr = _row_tile(R, 256)
    npart = len(parts)
    c1 = 1.0 / (1.0 - ADAM_B1 ** ADAM_STEP)
    c2 = 1.0 / (1.0 - ADAM_B2 ** ADAM_STEP)

    def body(*refs):
        w_ref, m_ref, v_ref = refs[:3]
        p_refs = refs[3:3 + npart]
        g_ref, d_ref, nm_ref, nv_ref = refs[3 + npart:]
        g = p_refs[0][...].astype(F32)
        for p in p_refs[1:]:
            g = g + p[...].astype(F32)
        nm = ADAM_B1 * m_ref[...] + (1.0 - ADAM_B1) * g
        nv = ADAM_B2 * v_ref[...] + (1.0 - ADAM_B2) * (g * g)
        mh = nm * c1
        vh = nv * c2
        g_ref[...] = g
        nm_ref[...] = nm
        nv_ref[...] = nv
        d_ref[...] = -ADAM_LR * (mh / (jnp.sqrt(vh) + ADAM_EPS) + ADAM_WD * w_ref[...])

    wspec = pl.BlockSpec((tr, C), lambda i: (i, 0))
    pspecs = [pl.BlockSpec(bs(tr, C), im) for (_, bs, im) in parts]
    outs = pl.pallas_call(
        body, name=name, grid=(R // tr,),
        in_specs=[wspec] * 3 + pspecs, out_specs=[wspec] * 4,
        out_shape=[SDS((R, C), F32)] * 4,
        compiler_params=_cparams(("parallel",)))(w, m, v, *[p[0] for p in parts])
    return outs


def _plain_part(g):
    return (g, lambda tr, C: (tr, C), lambda i: (i, 0))


def _slot_part(g, slot):
    return (g, lambda tr, C: (None, tr, C), lambda i, s=slot: (s, i, 0))


def ada_fwd(c_all, w_cols, b_cols, name):
    Bg, D = c_all.shape
    n = w_cols.shape[1]

    def body(c_ref, w_ref, b_ref, o_ref, s_ref):
        cc = c_ref[...]
        s = cc * _sig(cc)
        s_ref[...] = s
        o_ref[...] = jnp.dot(s, w_ref[...], preferred_element_type=F32, precision=lax.Precision.HIGHEST) + b_ref[...]

    return pl.pallas_call(body, name=name, out_shape=[SDS((Bg, n), F32), SDS((Bg, D), F32)],
                          compiler_params=_cparams())(c_all, w_cols, b_cols)


def ada_bwd(s_all, dmod_cols, dmod_all, name):
    Bg, D = s_all.shape
    n = dmod_cols.shape[1]

    def body(s_ref, dc_ref, da_ref, gw_ref, gb_ref):
        gw_ref[...] = lax.dot_general(s_ref[...], dc_ref[...], (((0,), (0,)), ((), ())),
                                      preferred_element_type=F32, precision=lax.Precision.HIGHEST)
        acc = da_ref[0:1, :]
        for r in range(1, Bg):
            acc = acc + da_ref[r:r + 1, :]
        gb_ref[...] = acc

    return pl.pallas_call(body, name=name, out_shape=[SDS((D, n), F32), SDS((1, dmod_all.shape[1]), F32)],
                          compiler_params=_cparams())(s_all, dmod_cols, dmod_all)


def _vec(D, rank):
    return pl.BlockSpec((1, D), (lambda i: (0, 0)) if rank == 1 else (lambda i, j: (0, 0)))


def _modspec(D, tpb, rank):
    if rank == 1:
        return pl.BlockSpec((None, 1, D), lambda i: (i // tpb, 0, 0))
    return pl.BlockSpec((None, 1, D), lambda i, j: (i // tpb, 0, 0))


def _resident(shape):
    return pl.BlockSpec(shape, lambda *_: (0,) * len(shape), pipeline_mode=pl.Buffered(1))


def ffn_up(xs, pg, pb, sc, sh, wg8, name, comm=None):
    T, D = xs.shape
    n2, _, nb = wg8.shape
    nj = n2 // 2
    S = T // sc.shape[0]
    tm = min(512, S)
    tpb = S // tm
    rider = _Rider(comm)

    def body(*refs):
        (x_ref, pg_ref, pb_ref, sc_ref, sh_ref, w_ref, u_ref, a_ref, g_ref, p_ref), ride = rider.split(refs, 6, 4)
        ride(pl.program_id(0), T // tm)
        xin = x_ref[...] * pg_ref[...] + pb_ref[...]
        u_ref[...] = (xin * (1.0 + sc_ref[...]) + sh_ref[...]).astype(BF)

        def col_block(j, _):
            u = u_ref[...]
            a = _dot_nn(u, w_ref[j])
            g = _dot_nn(u, w_ref[j + nj])
            a_ref[j] = a.astype(BF)
            g_ref[j] = g.astype(BF)
            p_ref[j] = ((a * _sig(a)) * g).astype(BF)
            return 0

        lax.fori_loop(0, nj, col_block, 0)

    blk = pl.BlockSpec((nj, tm, nb), lambda i: (0, i, 0))
    row = pl.BlockSpec((tm, D), lambda i: (i, 0))
    outs = pl.pallas_call(
        body, name=name, grid=(T // tm,),
        in_specs=[row, _vec(D, 1), _vec(D, 1), _modspec(D, tpb, 1), _modspec(D, tpb, 1), _resident(wg8.shape)]
        + rider.in_specs,
        out_specs=[row, blk, blk, blk] + rider.out_specs,
        out_shape=[SDS((T, D), BF)] + [SDS((nj, T, nb), BF)] * 3 + rider.out_shape,
        scratch_shapes=rider.scratch,
        compiler_params=_cparams(rider.sem(("parallel",))))(xs, pg, pb, sc, sh, wg8, *rider.ins)
    return rider.result(outs, 4)


def ffn_down_ln(p4, wd3, xs, pg, pb, gate, name):
    nj, T, nb = p4.shape
    D = wd3.shape[2]
    S = T // gate.shape[0]
    tm = min(512, S)
    tpb = S // tm

    def body(p_ref, wd_ref, x_ref, pg_ref, pb_ref, gate_ref, xh_ref, rs_ref, f_ref):
        f = _dot_nn(p_ref[0], wd_ref[0])
        for k in range(1, nj):
            f = f + _dot_nn(p_ref[k], wd_ref[k])
        xin = x_ref[...] * pg_ref[...] + pb_ref[...]
        r = DEEPNORM_ALPHA * xin + gate_ref[...] * (MACARON_WEIGHT * f)
        xh, rstd = _ln_stats(r)
        xh_ref[...] = xh
        rs_ref[...] = rstd
        f_ref[...] = f.astype(BF)

    row = pl.BlockSpec((tm, D), lambda i: (i, 0))
    return pl.pallas_call(
        body, name=name, grid=(T // tm,),
        in_specs=[pl.BlockSpec((nj, tm, nb), lambda i: (0, i, 0)), _resident(wd3.shape),
                  row, _vec(D, 1), _vec(D, 1), _modspec(D, tpb, 1)],
        out_specs=[row, pl.BlockSpec((tm, 1), lambda i: (i, 0)), row],
        out_shape=[SDS((T, D), F32), SDS((T, 1), F32), SDS((T, D), BF)],
        compiler_params=_cparams(("parallel",)))(p4, wd3, xs, pg, pb, gate)


def mod_matmul(xs, pg, pb, sc, sh, w, name, comm=None):
    T, D = xs.shape
    N = w.shape[1]
    S = T // sc.shape[0]
    tm = min(256, S)
    tpb = S // tm
    rider = _Rider(comm)

    def body(*refs):
        (x_ref, pg_ref, pb_ref, sc_ref, sh_ref, w_ref, u_ref, o_ref), ride = rider.split(refs, 6, 2)
        ride(pl.program_id(0), T // tm)
        xin = x_ref[...] * pg_ref[...] + pb_ref[...]
        u = (xin * (1.0 + sc_ref[...]) + sh_ref[...]).astype(BF)
        u_ref[...] = u
        for n in range(N // D):
            o_ref[:, n * D:(n + 1) * D] = _dot_nn(u, w_ref[:, n * D:(n + 1) * D])

    row = pl.BlockSpec((tm, D), lambda i: (i, 0))
    outs = pl.pallas_call(
        body, name=name, grid=(T // tm,),
        in_specs=[row, _vec(D, 1), _vec(D, 1), _modspec(D, tpb, 1), _modspec(D, tpb, 1), _resident(w.shape)]
        + rider.in_specs,
        out_specs=[row, pl.BlockSpec((tm, N), lambda i: (i, 0))] + rider.out_specs,
        out_shape=[SDS((T, D), BF), SDS((T, N), F32)] + rider.out_shape,
        scratch_shapes=rider.scratch,
        compiler_params=_cparams(rider.sem(("parallel",))))(xs, pg, pb, sc, sh, w, *rider.ins)
    return rider.result(outs, 2)


ATT_TQ = 1024
ATT_TK = 256


def _att_consts(tk):
    r = lax.broadcasted_iota(jnp.int32, (tk + 8, tk), 0)
    c = lax.broadcasted_iota(jnp.int32, (tk + 8, tk), 1)
    usum = jnp.where((r >= tk) | (c > r), 1.0, 0.0).astype(BF)
    lsum = jnp.where((r >= tk) | (c < r), 1.0, 0.0).astype(BF)
    dmask = lax.broadcasted_iota(jnp.int32, (tk, tk), 0) < lax.broadcasted_iota(jnp.int32, (tk, tk), 1)
    return usum, lsum, dmask


def _split_dot(m, v):
    hi = v.astype(BF)
    lo = (v - hi.astype(F32)).astype(BF)
    return _dot_nn(m, hi) + _dot_nn(m, lo)


def _softplus(z):
    return jnp.maximum(z, 0.0) + jnp.log(1.0 + jnp.exp(-jnp.abs(z)))


def _att_dims(S, D):
    dh = SB_HEAD_DIM
    cw = min(LANE, D)
    tq = min(ATT_TQ, S)
    tk = min(ATT_TK, tq)
    assert tq % tk == 0 and S % tq == 0
    return dh, cw, cw // dh, D // cw, tq, tk, S // tq, S // tk


def att_fwd(proj, Bl, S, D, name):
    dh, cw, hp, nblk, tq, tk, nq, nk = _att_dims(S, D)
    scale = 1.0 / math.sqrt(dh)
    assert math.log2(scale) == int(math.log2(scale))
    H = D // dh

    def body(q_ref, k_ref, v_ref, o_ref, car_ref, qs, ks, vts):
        usum, _, dmask = _att_consts(tk)
        for hh in range(hp):
            sl = slice(hh * dh, (hh + 1) * dh)
            qs[hh] = (q_ref[:, sl] * scale).astype(BF)
            ks[hh] = k_ref[:, sl].astype(BF)
            for kb in range(nk):
                vts[hh, kb] = v_ref[kb * tk:(kb + 1) * tk, sl].T.astype(BF)
        nch = tq // tk

        def qloop(qb, _):
            qo = pl.multiple_of(qb * tq, tq)
            n_full = qb * nch

            def blk(kb, state, diag):
                ko = pl.multiple_of(kb * tk, tk)
                chains = [(hh, c) for hh in range(hp) for c in range(0 if diag is None else diag, nch)]

                def masked(ch, val):
                    return jnp.where(dmask, val, 0.0) if ch[1] == diag else val

                z = {ch: _dot_nt(ks[ch[0], pl.ds(ko, tk), :], qs[ch[0], pl.ds(pl.multiple_of(qo + ch[1] * tk, tk), tk), :])
                     for ch in chains}
                sp = {ch: _softplus(z[ch]) for ch in chains}
                lk = {ch: masked(ch, -sp[ch]) for ch in chains}
                for hh, c in chains:
                    car_ref[hh, qb * nk + kb, :, c * tk:(c + 1) * tk] = state[hh][c][0]
                cs = {ch: _split_dot(usum, lk[ch]) for ch in chains}
                w = {ch: masked(ch, jnp.exp((z[ch] - sp[ch]) + state[ch[0]][ch[1]][0][0:1, :] + cs[ch][:tk]))
                     for ch in chains}
                pv = {ch: _dot_nn(vts[ch[0], kb], w[ch].astype(BF)) for ch in chains}
                return tuple(tuple(
                    (state[hh][c][0] + cs[(hh, c)][tk:], state[hh][c][1] + pv[(hh, c)]) if (hh, c) in z else state[hh][c]
                    for c in range(nch)) for hh in range(hp))

            state = tuple(tuple((jnp.zeros((8, tk), F32), jnp.zeros((dh, tk), F32)) for _ in range(nch))
                          for _ in range(hp))
            for i in reversed(range(nch)):
                state = blk(n_full + i, state, i)
            state = lax.fori_loop(0, n_full, lambda j, st: blk(n_full - 1 - j, st, None), state)
            for hh in range(hp):
                for c in range(nch):
                    o_ref[pl.ds(pl.multiple_of(qo + c * tk, tk), tk), hh * dh:(hh + 1) * dh] = (
                        state[hh][c][1].T.astype(BF))
            return 0

        lax.fori_loop(0, nq, qloop, 0)

    def seg(s):
        return pl.BlockSpec((S, cw), lambda b, h: (b, s * nblk + h))

    return pl.pallas_call(
        body, name=name, grid=(Bl, nblk),
        in_specs=[seg(0), seg(1), seg(2)],
        out_specs=[pl.BlockSpec((S, cw), lambda b, h: (b, h)),
                   pl.BlockSpec((None, hp, nq * nk, 8, tq), lambda b, h: (b, h, 0, 0, 0))],
        out_shape=[SDS((Bl * S, D), BF), SDS((Bl, H, nq * nk, 8, tq), F32)],
        scratch_shapes=[pltpu.VMEM((hp, S, dh), BF)] * 2 + [pltpu.VMEM((hp, nk, dh, tk), BF)],
        compiler_params=_cparams(("parallel", "parallel")))(proj, proj, proj)


def conv_fwd(proj, cw32, cb, cg, cbeta, Bl, S, D, kw, name):
    T = Bl * S
    ts = min(128, S)
    ns = S // ts
    off = HALO - (kw - 1)
    rc = min(64, ts)
    cw = min(LANE, D)

    def body(a_ref, b_ref, ha_ref, hb_ref, w_ref, cb_ref, g_ref, be_ref, cs_ref, xh_ref, rs_ref, hext, conv_s):
        i = pl.program_id(1)
        hext[pl.ds(HALO, ts), :] = a_ref[...] * _sig(b_ref[...])
        hh = ha_ref[...] * _sig(hb_ref[...])
        hext[pl.ds(0, HALO), :] = jnp.where(i == 0, 0.0, hh)
        for cb_ in range(D // cw):
            cols = slice(cb_ * cw, (cb_ + 1) * cw)
            accs = [jnp.zeros((rc, cw), F32) for _ in range(ts // rc)]
            for k in range(kw):
                wk = w_ref[k:k + 1, cols]
                for r in range(ts // rc):
                    accs[r] = accs[r] + wk * hext[pl.ds(r * rc + off + k, rc), cols]
            for r in range(ts // rc):
                conv_s[pl.ds(r * rc, rc), cols] = accs[r]
        conv = conv_s[...] + cb_ref[...]
        xh, rstd = _ln_stats(conv)
        xh_ref[...] = xh
        rs_ref[...] = rstd
        cl = xh * g_ref[...] + be_ref[...]
        cs_ref[...] = (cl * _sig(cl)).astype(BF)

    hpb = ts // HALO

    def tile(seg):
        return pl.BlockSpec((ts, D), lambda b, i: (b * ns + i, seg))

    def halo(seg):
        return pl.BlockSpec((HALO, D), lambda b, i: (jnp.maximum((b * ns + i) * hpb - 1, 0), seg))

    row = pl.BlockSpec((ts, D), lambda b, i: (b * ns + i, 0))
    vec = pl.BlockSpec((1, D), lambda b, i: (0, 0))
    return pl.pallas_call(
        body, name=name, grid=(Bl, ns),
        in_specs=[tile(3), tile(4), halo(3), halo(4), pl.BlockSpec((HALO, D), lambda b, i: (0, 0)), vec, vec, vec],
        out_specs=[row, row, pl.BlockSpec((ts, 1), lambda b, i: (b * ns + i, 0))],
        out_shape=[SDS((T, D), BF), SDS((T, D), F32), SDS((T, 1), F32)],
        scratch_shapes=[pltpu.VMEM((ts + HALO, D), F32), pltpu.VMEM((ts, D), F32)],
        compiler_params=_cparams(("parallel", "arbitrary")))(proj, proj, proj, proj, cw32, cb, cg, cbeta)


def mix_fwd(yatt, cs, proj, wsb, wco, wout, xs, pg, pb, gate, name):
    T, D = yatt.shape
    S = T // gate.shape[0]
    tm = min(256, S)
    tpb = S // tm

    def body(ya_ref, cs_ref, ga_ref, gb_ref, wsb_ref, wco_ref, wout_ref, x_ref, pg_ref, pb_ref, gate_ref,
             xh_ref, rs_ref, ysb_ref, yco_ref, mg_ref, o_ref):
        ysb = _dot_nn(ya_ref[...], wsb_ref[...])
        yco = _dot_nn(cs_ref[...], wco_ref[...])
        merged = _sig(ga_ref[...]) * ysb + _sig(gb_ref[...]) * yco
        mg = merged.astype(BF)
        o = _dot_nn(mg, wout_ref[...])
        xin = x_ref[...] * pg_ref[...] + pb_ref[...]
        r = DEEPNORM_ALPHA * xin + gate_ref[...] * o
        xh, rstd = _ln_stats(r)
        xh_ref[...] = xh
        rs_ref[...] = rstd
        ysb_ref[...] = ysb.astype(BF)
        yco_ref[...] = yco.astype(BF)
        mg_ref[...] = mg
        o_ref[...] = o.astype(BF)

    row = pl.BlockSpec((tm, D), lambda i: (i, 0))
    wfull = pl.BlockSpec((D, D), lambda i: (0, 0))
    return pl.pallas_call(
        body, name=name, grid=(T // tm,),
        in_specs=[row, row, pl.BlockSpec((tm, D), lambda i: (i, 5)), pl.BlockSpec((tm, D), lambda i: (i, 6)),
                  wfull, wfull, wfull, row, _vec(D, 1), _vec(D, 1), _modspec(D, tpb, 1)],
        out_specs=[row, pl.BlockSpec((tm, 1), lambda i: (i, 0)), row, row, row, row],
        out_shape=[SDS((T, D), F32), SDS((T, 1), F32)] + [SDS((T, D), BF)] * 4,
        compiler_params=_cparams(("parallel",)))(yatt, cs, proj, proj, wsb, wco, wout, xs, pg, pb, gate)


def ln_bwd(dout, xh, rstd, lng, lnb, gate, sub, res_w, name, target=None):
    T, D = xh.shape
    Bl = gate.shape[0]
    S = T // Bl
    tm = min(256, S)
    tpb = S // tm
    first = target is not None

    def body(*refs):
        if first:
            tg_ref, xh_ref, rs_ref, g_ref, b_ref, gate_ref, sub_ref = refs[:7]
            dr_ref, ds_ref, dg_ref, db_ref, dgate_ref, loss_ref = refs[7:]
        else:
            do_ref, xh_ref, rs_ref, g_ref, b_ref, gate_ref, sub_ref = refs[:7]
            dr_ref, ds_ref, dg_ref, db_ref, dgate_ref = refs[7:]
        i = pl.program_id(0)
        xh_ = xh_ref[...]
        if first:
            diff = (xh_ * g_ref[...] + b_ref[...]) - tg_ref[...]
            lsum = jnp.sum(jnp.sum(diff * diff, axis=1, keepdims=True), axis=0, keepdims=True) * (0.5 / D)
            do = diff * (1.0 / D)
        else:
            do = do_ref[...]

        @pl.when(i == 0)
        def _():
            dg_ref[...] = jnp.zeros_like(dg_ref)
            db_ref[...] = jnp.zeros_like(db_ref)
            if first:
                loss_ref[...] = jnp.zeros_like(loss_ref)

        @pl.when(i % tpb == 0)
        def _():
            dgate_ref[...] = jnp.zeros_like(dgate_ref)

        if first:
            loss_ref[...] += jnp.broadcast_to(lsum, loss_ref.shape)
        dg_ref[...] += _rowsum(do * xh_)
        db_ref[...] += _rowsum(do)
        dr = _ln_bwd(do * g_ref[...], xh_, rs_ref[...])
        dr_ref[...] = dr
        ds_ref[...] = (dr * gate_ref[...] * res_w).astype(BF)
        dgate_ref[...] += _rowsum(dr * (res_w * sub_ref[...].astype(F32)))

    row = pl.BlockSpec((tm, D), lambda i: (i, 0))
    vec = _vec(D, 1)
    mod = _modspec(D, tpb, 1)
    out_specs = [row, row, vec, vec, mod]
    out_shape = [SDS((T, D), F32), SDS((T, D), BF), SDS((1, D), F32), SDS((1, D), F32), SDS((Bl, 1, D), F32)]
    if first:
        out_specs.append(pl.BlockSpec((8, LANE), lambda i: (0, 0)))
        out_shape.append(SDS((8, LANE), F32))
    return pl.pallas_call(
        body, name=name, grid=(T // tm,),
        in_specs=[row, row, pl.BlockSpec((tm, 1), lambda i: (i, 0)), vec, vec, mod, row],
        out_specs=out_specs, out_shape=out_shape,
        compiler_params=_cparams(("arbitrary",)))(target if first else dout, xh, rstd, lng, lnb, gate, sub)


def swiglu_bwd(df, wd3, a4, g4, name):
    nj, T, nb = a4.shape
    D = df.shape[1]
    tm = min(512, T)

    def body(df_ref, wd_ref, a_ref, g_ref, o_ref):
        def col_block(j, _):
            dp = _dot_nt(df_ref[...], wd_ref[j])
            a = a_ref[j].astype(F32)
            g = g_ref[j].astype(F32)
            s = _sig(a)
            o_ref[0, j] = (dp * g * (s * (1.0 + a * (1.0 - s)))).astype(BF)
            o_ref[1, j] = (dp * (a * s)).astype(BF)
            return 0

        lax.fori_loop(0, nj, col_block, 0)

    blk = pl.BlockSpec((nj, tm, nb), lambda i: (0, i, 0))
    return pl.pallas_call(
        body, name=name, grid=(T // tm,),
        in_specs=[pl.BlockSpec((tm, D), lambda i: (i, 0)), _resident(wd3.shape), blk, blk],
        out_specs=pl.BlockSpec((2, nj, tm, nb), lambda i: (0, 0, i, 0)),
        out_shape=SDS((2, nj, T, nb), BF),
        compiler_params=_cparams(("parallel",)))(df, wd3, a4, g4)


def mod_bwd(dh, w, dr, xs, pg, pb, sc, blocked, name, comm=None):
    T, D = dr.shape
    Bl = sc.shape[0]
    S = T // Bl
    tm = min(512 if blocked else 256, S)
    tpb = S // tm
    if blocked:
        nk, _, kb = dh.shape
        dh_spec = pl.BlockSpec((nk, tm, kb), lambda i: (0, i, 0))
    else:
        kb = D
        nk = dh.shape[1] // kb
        dh_spec = pl.BlockSpec((tm, nk * kb), lambda i: (i, 0))

    rider = _Rider(comm)

    def body(*refs):
        (dh_ref, w_ref, dr_ref, x_ref, pg_ref, pb_ref, sc_ref, dx_ref, dsc_ref, dsh_ref), ride = rider.split(refs, 7, 3)
        i = pl.program_id(0)
        ride(i, T // tm)

        def part(k):
            if blocked:
                return _dot_nt(dh_ref[k], w_ref[k])
            return _dot_nt(dh_ref[:, k * kb:(k + 1) * kb], w_ref[:, k * kb:(k + 1) * kb])

        du = part(0)
        for k in range(1, nk):
            du = du + part(k)

        @pl.when(i % tpb == 0)
        def _():
            dsc_ref[...] = jnp.zeros_like(dsc_ref)
            dsh_ref[...] = jnp.zeros_like(dsh_ref)

        xin = x_ref[...] * pg_ref[...] + pb_ref[...]
        dx_ref[...] = DEEPNORM_ALPHA * dr_ref[...] + du * (1.0 + sc_ref[...])
        dsc_ref[...] += _rowsum(du * xin)
        dsh_ref[...] += _rowsum(du)

    row = pl.BlockSpec((tm, D), lambda i: (i, 0))
    mod = _modspec(D, tpb, 1)
    outs = pl.pallas_call(
        body, name=name, grid=(T // tm,),
        in_specs=[dh_spec, _resident(w.shape), row, row, _vec(D, 1), _vec(D, 1), mod] + rider.in_specs,
        out_specs=[row, mod, mod] + rider.out_specs,
        out_shape=[SDS((T, D), F32), SDS((Bl, 1, D), F32), SDS((Bl, 1, D), F32)] + rider.out_shape,
        scratch_shapes=rider.scratch,
        compiler_params=_cparams(("arbitrary",)))(dh, w, dr, xs, pg, pb, sc, *rider.ins)
    return rider.result(outs, 3)


def merge_bwd(do2, proj, ysb, yco, wsb, wco, wout, name):
    T, D = do2.shape
    tm = min(256, T)

    def body(do_ref, ga_ref, gb_ref, ysb_ref, yco_ref, wsb_ref, wco_ref, wout_ref,
             dysb_ref, dyco_ref, dga_ref, dgb_ref, dya_ref, dcs_ref):
        dm = _dot_nt(do_ref[...], wout_ref[...])
        sa = _sig(ga_ref[...])
        sb = _sig(gb_ref[...])
        dysb = (dm * sa).astype(BF)
        dyco = (dm * sb).astype(BF)
        dysb_ref[...] = dysb
        dyco_ref[...] = dyco
        dga_ref[...] = (dm * ysb_ref[...].astype(F32) * (sa * (1.0 - sa))).astype(BF)
        dgb_ref[...] = (dm * yco_ref[...].astype(F32) * (sb * (1.0 - sb))).astype(BF)
        dya_ref[...] = _dot_nt(dysb, wsb_ref[...]).astype(BF)
        dcs_ref[...] = _dot_nt(dyco, wco_ref[...])

    row = pl.BlockSpec((tm, D), lambda i: (i, 0))
    wfull = pl.BlockSpec((D, D), lambda i: (0, 0))
    return pl.pallas_call(
        body, name=name, grid=(T // tm,),
        in_specs=[row, pl.BlockSpec((tm, D), lambda i: (i, 5)), pl.BlockSpec((tm, D), lambda i: (i, 6)),
                  row, row, wfull, wfull, wfull],
        out_specs=[row] * 6,
        out_shape=[SDS((T, D), BF)] * 5 + [SDS((T, D), F32)],
        compiler_params=_cparams(("parallel",)))(do2, proj, proj, ysb, yco, wsb, wco, wout)


def att_bwd(proj, dyatt, car, Bl, S, D, name):
    dh, cw, hp, nblk, tq, tk, nq, nk = _att_dims(S, D)
    scale = 1.0 / math.sqrt(dh)

    def body(q_ref, k_ref, v_ref, do_ref, car_ref, dq_ref, dk_ref, dv_ref, qs, ks, vs, dos, kts, dk_acc, dv_acc):
        usum, lsum, dmask = _att_consts(tk)
        for hh in range(hp):
            sl = slice(hh * dh, (hh + 1) * dh)
            qs[hh] = (q_ref[:, sl] * scale).astype(BF)
            ks[hh] = k_ref[:, sl].astype(BF)
            vs[hh] = v_ref[:, sl].astype(BF)
            dos[hh] = do_ref[:, sl]
            for kb in range(nk):
                kts[hh, kb] = k_ref[kb * tk:(kb + 1) * tk, sl].T.astype(BF)
        dk_acc[...] = jnp.zeros_like(dk_acc)
        dv_acc[...] = jnp.zeros_like(dv_acc)
        nch = tq // tk

        def qloop(qb, _):
            qo = pl.multiple_of(qb * tq, tq)
            n_full = qb * nch

            def blk(kb, state, diag):
                ko = pl.multiple_of(kb * tk, tk)
                chains = [(hh, c) for hh in range(hp) for c in range(0 if diag is None else diag, nch)]

                def masked(ch, val):
                    return jnp.where(dmask, val, 0.0) if ch[1] == diag else val

                def qrows(ref, ch):
                    return ref[ch[0], pl.ds(pl.multiple_of(qo + ch[1] * tk, tk), tk), :]

                k = [ks[hh, pl.ds(ko, tk), :] for hh in range(hp)]
                v = [vs[hh, pl.ds(ko, tk), :] for hh in range(hp)]
                z = {ch: _dot_nt(k[ch[0]], qrows(qs, ch)) for ch in chains}
                dw = {ch: _dot_nt(v[ch[0]], qrows(dos, ch)) for ch in chains}
                sp = {ch: _softplus(z[ch]) for ch in chains}
                lk = {ch: masked(ch, -sp[ch]) for ch in chains}
                cs = {ch: _split_dot(usum, lk[ch]) for ch in chains}
                w = {ch: masked(ch, jnp.exp((z[ch] - sp[ch])
                                            + car_ref[ch[0], qb * nk + kb, 0:1, ch[1] * tk:(ch[1] + 1) * tk]
                                            + cs[ch][:tk])) for ch in chains}
                dlw = {ch: dw[ch] * w[ch] for ch in chains}
                gs = {ch: _split_dot(lsum, dlw[ch]) for ch in chains}
                sg = {ch: jnp.exp(z[ch] - sp[ch]) for ch in chains}
                dzb = {ch: masked(ch, dlw[ch] * (1.0 - sg[ch])
                                  - sg[ch] * (state[ch[0]][ch[1]][0][0:1, :] + gs[ch][:tk])).astype(BF)
                       for ch in chains}
                wb = {ch: w[ch].astype(BF) for ch in chains}
                for hh in range(hp):
                    mine = [ch for ch in chains if ch[0] == hh]
                    dk_acc[hh, kb] += sum(_dot_nn(dzb[ch], qrows(qs, ch)) for ch in mine)
                    dv_acc[hh, kb] += sum(_dot_nn(wb[ch], qrows(dos, ch)) for ch in mine)
                dq = {ch: _dot_nn(kts[ch[0], kb], dzb[ch]) for ch in chains}
                return tuple(tuple(
                    (state[hh][c][0] + gs[(hh, c)][tk:], state[hh][c][1] + dq[(hh, c)]) if (hh, c) in z else state[hh][c]
                    for c in range(nch)) for hh in range(hp))

            state = tuple(tuple((jnp.zeros((8, tk), F32), jnp.zeros((dh, tk), F32)) for _ in range(nch))
                          for _ in range(hp))
            state = lax.fori_loop(0, n_full, lambda kb, st: blk(kb, st, None), state)
            for i in range(nch):
                state = blk(n_full + i, state, i)
            for hh in range(hp):
                for c in range(nch):
                    dq_ref[pl.ds(pl.multiple_of(qo + c * tk, tk), tk), hh * dh:(hh + 1) * dh] = (
                        (state[hh][c][1].T * scale).astype(BF))
            return 0

        lax.fori_loop(0, nq, qloop, 0)
        for hh in range(hp):
            sl = slice(hh * dh, (hh + 1) * dh)
            for kb in range(nk):
                dk_ref[kb * tk:(kb + 1) * tk, sl] = dk_acc[hh, kb].astype(BF)
                dv_ref[kb * tk:(kb + 1) * tk, sl] = dv_acc[hh, kb].astype(BF)

    def seg(s):
        return pl.BlockSpec((S, cw), lambda b, h: (b, s * nblk + h))

    blk_spec = pl.BlockSpec((S, cw), lambda b, h: (b, h))
    return pl.pallas_call(
        body, name=name, grid=(Bl, nblk),
        in_specs=[seg(0), seg(1), seg(2), blk_spec,
                  pl.BlockSpec((None, hp, nq * nk, 8, tq), lambda b, h: (b, h, 0, 0, 0))],
        out_specs=[blk_spec, blk_spec, blk_spec],
        out_shape=[SDS((Bl * S, D), BF)] * 3,
        scratch_shapes=[pltpu.VMEM((hp, S, dh), BF)] * 4 + [pltpu.VMEM((hp, nk, dh, tk), BF)]
        + [pltpu.VMEM((hp, nk, tk, dh), F32)] * 2,
        compiler_params=_cparams(("parallel", "parallel")))(proj, proj, proj, dyatt, car)


def conv_ln_bwd(dcs, xhc, rstd_c, cg, cbeta, name):
    T, D = dcs.shape
    tm = min(256, T)

    def body(dcs_ref, xh_ref, rs_ref, g_ref, b_ref, dconv_ref, dg_ref, db_ref, dcb_ref):
        @pl.when(pl.program_id(0) == 0)
        def _():
            dg_ref[...] = jnp.zeros_like(dg_ref)
            db_ref[...] = jnp.zeros_like(db_ref)
            dcb_ref[...] = jnp.zeros_like(dcb_ref)
        xh = xh_ref[...]
        cl = xh * g_ref[...] + b_ref[...]
        s = _sig(cl)
        dcl = dcs_ref[...] * (s * (1.0 + cl * (1.0 - s)))
        dg_ref[...] += _rowsum(dcl * xh)
        db_ref[...] += _rowsum(dcl)
        dconv = _ln_bwd(dcl * g_ref[...], xh, rs_ref[...])
        dconv_ref[...] = dconv
        dcb_ref[...] += _rowsum(dconv)

    row = pl.BlockSpec((tm, D), lambda i: (i, 0))
    vec = _vec(D, 1)
    return pl.pallas_call(
        body, name=name, grid=(T // tm,),
        in_specs=[row, row, pl.BlockSpec((tm, 1), lambda i: (i, 0)), vec, vec],
        out_specs=[row, vec, vec, vec],
        out_shape=[SDS((T, D), F32)] + [SDS((1, D), F32)] * 3,
        compiler_params=_cparams(("arbitrary",)))(dcs, xhc, rstd_c, cg, cbeta)


def conv_bwd(dconv, proj, cw32, Bl, S, D, kw, name):
    T = Bl * S
    ts = min(128, S)
    ns = S // ts
    off = HALO - (kw - 1)
    rc = min(64, ts)
    cw = min(LANE, D)
    hpb = ts // HALO
    nhb = T // HALO

    def body(dc_ref, dcn_ref, a_ref, b_ref, ha_ref, hb_ref, w_ref, da_ref, db_ref, dw_ref, hext, dext, dh_s):
        b_ = pl.program_id(0)
        i = pl.program_id(1)

        @pl.when((b_ == 0) & (i == 0))
        def _():
            dw_ref[...] = jnp.zeros_like(dw_ref)
        a = a_ref[...]
        sb = _sig(b_ref[...])
        hext[pl.ds(HALO, ts), :] = a * sb
        hh = ha_ref[...] * _sig(hb_ref[...])
        hext[pl.ds(0, HALO), :] = jnp.where(i == 0, 0.0, hh)
        dext[pl.ds(0, ts), :] = dc_ref[...]
        dext[pl.ds(ts, HALO), :] = jnp.where(i == ns - 1, 0.0, dcn_ref[...])
        for cb_ in range(D // cw):
            cols = slice(cb_ * cw, (cb_ + 1) * cw)
            accs = [jnp.zeros((rc, cw), F32) for _ in range(ts // rc)]
            for k in range(kw):
                wk = w_ref[k:k + 1, cols]
                wsum = jnp.zeros((rc, cw), F32)
                for r in range(ts // rc):
                    accs[r] = accs[r] + wk * dext[pl.ds(r * rc + (kw - 1) - k, rc), cols]
                    wsum = wsum + dext[pl.ds(r * rc, rc), cols] * hext[pl.ds(r * rc + off + k, rc), cols]
                dw_ref[k:k + 1, cols] += _rowsum(wsum)
            for r in range(ts // rc):
                dh_s[pl.ds(r * rc, rc), cols] = accs[r]
        dhc = dh_s[...]
        da_ref[...] = (dhc * sb).astype(BF)
        db_ref[...] = (dhc * a * (sb * (1.0 - sb))).astype(BF)

    def tile(seg):
        return pl.BlockSpec((ts, D), lambda b, i: (b * ns + i, seg))

    def halo(seg):
        return pl.BlockSpec((HALO, D), lambda b, i: (jnp.maximum((b * ns + i) * hpb - 1, 0), seg))

    row = pl.BlockSpec((ts, D), lambda b, i: (b * ns + i, 0))
    nxt = pl.BlockSpec((HALO, D), lambda b, i: (jnp.minimum((b * ns + i + 1) * hpb, nhb - 1), 0))
    wspec = pl.BlockSpec((HALO, D), lambda b, i: (0, 0))
    return pl.pallas_call(
        body, name=name, grid=(Bl, ns),
        in_specs=[row, nxt, tile(3), tile(4), halo(3), halo(4), wspec],
        out_specs=[row, row, wspec],
        out_shape=[SDS((T, D), BF), SDS((T, D), BF), SDS((HALO, D), F32)],
        scratch_shapes=[pltpu.VMEM((ts + HALO, D), F32), pltpu.VMEM((ts + HALO, D), F32), pltpu.VMEM((ts, D), F32)],
        compiler_params=_cparams(("arbitrary", "arbitrary")))(dconv, dconv, proj, proj, proj, proj, cw32)


def matmul_tn(xa, ga, x_spec, g_spec, out_shape, out_spec, acc_shape, grid, name, comm=None):
    nk = grid[-1]
    rider = _Rider(comm)

    def body(*refs):
        (x_ref, g_ref, o_ref, acc), ride = rider.split(refs, 2, 1, 1)
        k = pl.program_id(1)
        ride(pl.program_id(0) * nk + k, grid[0] * nk)

        @pl.when(k == 0)
        def _():
            acc[...] = jnp.zeros_like(acc)
        acc[...] += _dot_tn(x_ref[...], g_ref[...])

        @pl.when(k == nk - 1)
        def _():
            o_ref[...] = acc[...]

    outs = pl.pallas_call(
        body, name=name, grid=grid, in_specs=[x_spec, g_spec] + rider.in_specs,
        out_specs=[out_spec] + rider.out_specs,
        out_shape=[SDS(out_shape, F32)] + rider.out_shape,
        scratch_shapes=[pltpu.VMEM(acc_shape, F32)] + rider.scratch,
        compiler_params=_cparams(rider.sem(("parallel", "arbitrary"))))(xa, ga, *rider.ins)
    own, landed = rider.result(outs, 1)
    return own[0] if comm is None else (own[0], landed)


def wgrad_std(xa, ga, name):
    T, M = xa.shape
    N = ga.shape[1]
    tk = min(1024, T)
    return matmul_tn(xa, ga, pl.BlockSpec((tk, M), lambda n, k: (k, 0)), pl.BlockSpec((tk, N), lambda n, k: (k, 0)),
                     (M, N), pl.BlockSpec((M, N), lambda n, k: (0, 0)), (M, N), (1, T // tk), name)


def wgrad_down(p4, df, name):
    nj, T, nb = p4.shape
    D = df.shape[1]
    tk = min(1024, T)
    return matmul_tn(p4, df, pl.BlockSpec((None, tk, nb), lambda j, k: (j, k, 0)),
                     pl.BlockSpec((tk, D), lambda j, k: (k, 0)),
                     (nj * nb, D), pl.BlockSpec((nb, D), lambda j, k: (j, 0)), (nb, D), (nj, T // tk), name)


def wgrad_gu(u, dh8, name, comm=None):
    n8, T, nb = dh8.shape
    D = u.shape[1]
    tk = min(1024, T)
    return matmul_tn(u, dh8, pl.BlockSpec((tk, D), lambda j, k: (k, 0)),
                     pl.BlockSpec((None, tk, nb), lambda j, k: (j, k, 0)),
                     (n8, D, nb), pl.BlockSpec((None, D, nb), lambda j, k: (j, 0, 0)), (D, nb), (n8, T // tk), name,
                     comm=comm)


def wgrad_in(u, dproj, name):
    T, D = u.shape
    bw = dproj.shape[1] // N_DEV
    tk = min(1024, T)
    return matmul_tn(u, dproj, pl.BlockSpec((tk, D), lambda j, k: (k, 0)),
                     pl.BlockSpec((tk, bw), lambda j, k: (k, j)),
                     (N_DEV, D, bw), pl.BlockSpec((None, D, bw), lambda j, k: (j, 0, 0)), (D, bw),
                     (N_DEV, T // tk), name)


def kernel(x, c, w_ada, b_ada, ffn1_w_gu, ffn1_w_down, ln1_g, ln1_b, w_in, w_sb_out, conv_w, conv_b, conv_ln_g, conv_ln_b, w_conv_out, w_out, ln2_g, ln2_b, ffn2_w_gu, ffn2_w_down, ln3_g, ln3_b, loss_target, m_w_ada, m_b_ada, m_ffn1_w_gu, m_ffn1_w_down, m_ln1_g, m_ln1_b, m_w_in, m_w_sb_out, m_conv_w, m_conv_b, m_conv_ln_g, m_conv_ln_b, m_w_conv_out, m_w_out, m_ln2_g, m_ln2_b, m_ffn2_w_gu, m_ffn2_w_down, m_ln3_g, m_ln3_b, v_w_ada, v_b_ada, v_ffn1_w_gu, v_ffn1_w_down, v_ln1_g, v_ln1_b, v_w_in, v_w_sb_out, v_conv_w, v_conv_b, v_conv_ln_g, v_conv_ln_b, v_w_conv_out, v_w_out, v_ln2_g, v_ln2_b, v_ffn2_w_gu, v_ffn2_w_down, v_ln3_g, v_ln3_b):
    Bl, S, D = x.shape
    T = Bl * S
    kw = conv_w.shape[1]
    ax, ay, ac = lax.axis_index("x"), lax.axis_index("y"), lax.axis_index("c")
    me = 4 * ax + 2 * ay + ac
    qc = jnp.stack([2 * ax + ay, ac]).astype(jnp.int32)

    big = dict(ffn1_w_gu=ffn1_w_gu[0], ffn1_w_down=ffn1_w_down[0], w_in=w_in[0], w_sb_out=w_sb_out[0],
               w_conv_out=w_conv_out[0], w_out=w_out[0], ffn2_w_gu=ffn2_w_gu[0], ffn2_w_down=ffn2_w_down[0])
    big_m = dict(ffn1_w_gu=m_ffn1_w_gu[0], ffn1_w_down=m_ffn1_w_down[0], w_in=m_w_in[0], w_sb_out=m_w_sb_out[0],
                 w_conv_out=m_w_conv_out[0], w_out=m_w_out[0], ffn2_w_gu=m_ffn2_w_gu[0], ffn2_w_down=m_ffn2_w_down[0])
    big_v = dict(ffn1_w_gu=v_ffn1_w_gu[0], ffn1_w_down=v_ffn1_w_down[0], w_in=v_w_in[0], w_sb_out=v_w_sb_out[0],
                 w_conv_out=v_w_conv_out[0], w_out=v_w_out[0], ffn2_w_gu=v_ffn2_w_gu[0], ffn2_w_down=v_ffn2_w_down[0])
    names = list(big)

    layer1, layer2, layer3 = ["ffn1_w_gu", "ffn1_w_down"], ["w_in", "w_sb_out", "w_conv_out", "w_out"], \
        ["ffn2_w_gu", "ffn2_w_down"]

    def shards(group):
        return [big[n].astype(BF) for n in group]

    G = dict(zip(layer1, run_comm(gather_plan(shards(layer1)), "allgather_ffn1")))
    wg1 = G["ffn1_w_gu"]
    wd1 = G["ffn1_w_down"].reshape(wg1.shape[0] // 2, wg1.shape[2], D)

    cw8 = small_allgather(conv_w[0], "allgather_conv_w")
    cw_full = jnp.transpose(cw8, (1, 0, 2)).reshape(kw, D)
    cw32 = jnp.concatenate([cw_full, jnp.zeros((HALO - kw, D), F32)], axis=0)

    c_all = small_allgather(c, "allgather_c").reshape(N_DEV * Bl, D)
    ncol = w_ada.shape[2]
    b_cols = lax.dynamic_slice(b_ada, (0, me * ncol), (1, ncol))
    mod_cols, s_all = ada_fwd(c_all, w_ada[0], b_cols, "ada_fwd")
    mod8 = small_allgather(mod_cols, "allgather_mod")
    mod_mine = lax.dynamic_slice(mod8, (0, me * Bl, 0), (N_DEV, Bl, ncol))
    mod = jnp.transpose(mod_mine, (1, 0, 2)).reshape(Bl, N_MOD_ROWS, 1, D)
    sh1, sc1, g1, sh2, sc2, g2, sh3, sc3, g3 = [mod[:, i] for i in range(N_MOD_ROWS)]

    ones = jnp.ones((1, D), F32)
    zeros = jnp.zeros((1, D), F32)
    xf = x.reshape(T, D)
    tgt = loss_target.reshape(T, D)

    (u1, a1, gg1, p1), landed = ffn_up(xf, ones, zeros, sc1, sh1, wg1, "ffn1_up", comm=gather_plan(shards(layer2)))
    G.update(zip(layer2, landed))
    win = jnp.transpose(G["w_in"], (1, 0, 2)).reshape(D, -1)
    wsb = G["w_sb_out"].reshape(D, D)
    wco = G["w_conv_out"].reshape(-1, D)
    wout = G["w_out"].reshape(D, D)
    xh1, rs1, f1 = ffn_down_ln(p1, wd1, xf, ones, zeros, g1, "ffn1_down_ln")
    (u2, proj), landed = mod_matmul(xh1, ln1_g, ln1_b, sc2, sh2, win, "in_proj", comm=gather_plan(shards(layer3)))
    G.update(zip(layer3, landed))
    wg2 = G["ffn2_w_gu"]
    wd2 = G["ffn2_w_down"].reshape(wg2.shape[0] // 2, wg2.shape[2], D)
    yatt, car = att_fwd(proj, Bl, S, D, "att_fwd")
    cs, xhc, rsc = conv_fwd(proj, cw32, conv_b, conv_ln_g, conv_ln_b, Bl, S, D, kw, "conv_fwd")
    xh2, rs2, ysb, yco, merged, o2 = mix_fwd(yatt, cs, proj, wsb, wco, wout, xh1, ln1_g, ln1_b, g2, "mix_fwd")
    (u3, a3, gg3, p3), _ = ffn_up(xh2, ln2_g, ln2_b, sc3, sh3, wg2, "ffn2_up")
    xh3, rs3, f3 = ffn_down_ln(p3, wd2, xh2, ln2_g, ln2_b, g3, "ffn2_down_ln")

    own_sum, recv_b = {}, {}

    def chip_sums(group, grads):
        g42 = [g.reshape((4, 2) + big[n].shape) for n, g in zip(group, grads)]
        recv_a = exchange_sibling(g42, "rs_sibling_" + group[0])
        sums = [chip_sum(g, r, qc, "chip_sum_" + n) for g, r, n in zip(g42, recv_a, group)]
        own_sum.update({n: s[0] for n, s in zip(group, sums)})
        return chips_plan([s[1] for s in sums])

    dr3, df3, dln3g, dln3b, dg3, lossp = ln_bwd(None, xh3, rs3, ln3_g, ln3_b, g3, f3, MACARON_WEIGHT,
                                                 "ln3_bwd", target=tgt)
    dh3 = swiglu_bwd(df3, wd2, a3, gg3, "ffn2_swiglu_bwd").reshape((-1,) + a3.shape[1:])
    gw_d2 = wgrad_down(p3, df3, "wgrad_ffn2_down")
    gw_g2 = wgrad_gu(u3, dh3, "wgrad_ffn2_gu")
    (dx2, dsc3, dsh3), landed = mod_bwd(dh3, wg2, dr3, xh2, ln2_g, ln2_b, sc3, True, "ffn2_mod_bwd",
                                        comm=chip_sums(layer3, [gw_g2, gw_d2]))
    recv_b.update(zip(layer3, landed))

    dr2, do2, dln2g, dln2b, dg2 = ln_bwd(dx2, xh2, rs2, ln2_g, ln2_b, g2, o2, 1.0, "ln2_bwd")
    gw_out = wgrad_std(merged, do2, "wgrad_out")
    dysb, dyco, dga, dgb, dyatt, dcs = merge_bwd(do2, proj, ysb, yco, wsb, wco, wout, "merge_bwd")
    gw_sb = wgrad_std(yatt, dysb, "wgrad_sb")
    gw_co = wgrad_std(cs, dyco, "wgrad_conv_out")
    dq, dk, dv = att_bwd(proj, dyatt, car, Bl, S, D, "att_bwd")
    dconv, dcg, dcbeta, dcb = conv_ln_bwd(dcs, xhc, rsc, conv_ln_g, conv_ln_b, "conv_ln_bwd")
    dglu_a, dglu_b, dcw = conv_bwd(dconv, proj, cw32, Bl, S, D, kw, "conv_bwd")
    dproj = jnp.concatenate([dq, dk, dv, dglu_a, dglu_b, dga, dgb], axis=1)
    gw_in = wgrad_in(u2, dproj, "wgrad_in")
    (dx1, dsc2, dsh2), _ = mod_bwd(dproj, win, dr2, xh1, ln1_g, ln1_b, sc2, False, "mix_mod_bwd")
    plan2 = chip_sums(layer2, [gw_in, gw_sb, gw_co, gw_out])

    dr1, df1, dln1g, dln1b, dg1 = ln_bwd(dx1, xh1, rs1, ln1_g, ln1_b, g1, f1, MACARON_WEIGHT, "ln1_bwd")
    dh1 = swiglu_bwd(df1, wd1, a1, gg1, "ffn1_swiglu_bwd").reshape((-1,) + a1.shape[1:])
    gw_d1 = wgrad_down(p1, df1, "wgrad_ffn1_down")
    gw_g1, landed = wgrad_gu(u1, dh1, "wgrad_ffn1_gu", comm=plan2)
    recv_b.update(zip(layer2, landed))
    (grad_x, dsc1, dsh1), landed = mod_bwd(dh1, wg1, dr1, xf, ones, zeros, sc1, True, "ffn1_mod_bwd",
                                           comm=chip_sums(layer1, [gw_g1, gw_d1]))
    recv_b.update(zip(layer1, landed))

    dmod = jnp.concatenate([dsh1, dsc1, dg1, dsh2, dsc2, dg2, dsh3, dsc3, dg3], axis=1).reshape(Bl, N_MOD_ROWS * D)
    dmod_all = small_allgather(dmod, "allgather_dmod").reshape(N_DEV * Bl, N_MOD_ROWS * D)
    dmod_cols = lax.dynamic_slice(dmod_all, (0, me * ncol), (N_DEV * Bl, ncol))
    g_w_ada, g_b_ada = ada_bwd(s_all, dmod_cols, dmod_all, "ada_bwd")

    npad = 16
    small_rows = [dln1g, dln1b, dcb, dcg, dcbeta, dln2g, dln2b, dln3g, dln3b,
                  jnp.broadcast_to(lossp[0:1, 0:1], (1, D))]
    pack = jnp.concatenate(small_rows + [jnp.zeros((npad - len(small_rows), D), F32), dcw], axis=0)
    small = small_sum(small_allgather(pack, "allgather_small"), "small_sum")
    loss = small[9, 0]
    small_w = [ln1_g, ln1_b, conv_b, conv_ln_g, conv_ln_b, ln2_g, ln2_b, ln3_g, ln3_b]
    small_m = [m_ln1_g, m_ln1_b, m_conv_b, m_conv_ln_g, m_conv_ln_b, m_ln2_g, m_ln2_b, m_ln3_g, m_ln3_b]
    small_v = [v_ln1_g, v_ln1_b, v_conv_b, v_conv_ln_g, v_conv_ln_b, v_ln2_g, v_ln2_b, v_ln3_g, v_ln3_b]
    padrows = jnp.zeros((npad - len(small_w), D), F32)
    s_g, s_d, s_m, s_v = adamw(jnp.concatenate(small_w + [padrows], axis=0),
                               jnp.concatenate(small_m + [padrows], axis=0),
                               jnp.concatenate(small_v + [padrows], axis=0),
                               [_plain_part(small[:npad])], "adamw_small")
    dcw_mine = lax.dynamic_slice(small[npad:npad + kw], (0, me * (D // N_DEV)), (kw, D // N_DEV))
    cw_g, cw_d, cw_m, cw_v = adamw(conv_w[0], m_conv_w[0], v_conv_w[0], [_plain_part(dcw_mine)], "adamw_conv_w")
    ada_g, ada_d, ada_m, ada_v = adamw(w_ada[0], m_w_ada[0], v_w_ada[0], [_plain_part(g_w_ada)], "adamw_w_ada")
    bada_g, bada_d, bada_m, bada_v = adamw(b_ada, m_b_ada, v_b_ada, [_plain_part(g_b_ada)], "adamw_b_ada")

    res = {}
    for n in names:
        rb = recv_b[n]
        parts = [_plain_part(own_sum[n]), _slot_part(rb, 0), _slot_part(rb, 1), _slot_part(rb, 2)]
        res[n] = adamw(big[n], big_m[n], big_v[n], parts, "adamw_" + n)

    def small_out(k):
        order = dict(ln1_g=0, ln1_b=1, conv_b=2, conv_ln_g=3, conv_ln_b=4, ln2_g=5, ln2_b=6, ln3_g=7, ln3_b=8)
        return lambda arr: arr[order[k]:order[k] + 1]

    weight_order = ["w_ada", "b_ada", "ffn1_w_gu", "ffn1_w_down", "ln1_g", "ln1_b", "w_in", "w_sb_out", "conv_w",
                    "conv_b", "conv_ln_g", "conv_ln_b", "w_conv_out", "w_out", "ln2_g", "ln2_b", "ffn2_w_gu",
                    "ffn2_w_down", "ln3_g", "ln3_b"]

    shapes = dict(w_ada=w_ada.shape, b_ada=b_ada.shape, conv_w=conv_w.shape, ln1_g=ln1_g.shape,
                  **{n: (1,) + big[n].shape for n in names})

    def pick(which):
        outs = []
        for n in weight_order:
            if n == "w_ada":
                a = (ada_g, ada_d, ada_m, ada_v)[which]
            elif n == "b_ada":
                a = (bada_g, bada_d, bada_m, bada_v)[which]
            elif n == "conv_w":
                a = (cw_g, cw_d, cw_m, cw_v)[which]
            elif n in res:
                a = res[n][which]
            else:
                a = small_out(n)((s_g, s_d, s_m, s_v)[which])
            outs.append(a.reshape(shapes.get(n, ln1_g.shape)))
        return outs

    return (loss, grad_x.reshape(Bl, S, D), *pick(0), *pick(1), *pick(2), *pick(3))
```

```python
import functools
import math

import jax
import jax.numpy as jnp
from jax import lax
from jax.experimental import pallas as pl
from jax.experimental.pallas import tpu as pltpu

F32 = jnp.float32
BF = jnp.bfloat16
SDS = jax.ShapeDtypeStruct
MESH = pl.DeviceIdType.MESH

N_DEV = 8
SB_HEAD_DIM = 64
N_MOD_ROWS = 9
MACARON_WEIGHT = 0.5
DEEPNORM_ALPHA = 2.0 ** 0.25
LN_EPS = 1e-5
ADAM_LR = 0.001
ADAM_B1 = 0.9
ADAM_B2 = 0.999
ADAM_EPS = 1e-08
ADAM_WD = 0.01
ADAM_STEP = 10

V7X_VMEM_LIMIT = 52 * 1024 * 1024
LANE = 128
SUBLANES = 8
HALO = 32


def _cparams(sem=None):
    return pltpu.CompilerParams(dimension_semantics=sem, vmem_limit_bytes=V7X_VMEM_LIMIT)


def _dot_nn(a, b):
    return lax.dot_general(a, b, (((1,), (0,)), ((), ())), preferred_element_type=F32)


def _dot_nt(a, b):
    return lax.dot_general(a, b, (((1,), (1,)), ((), ())), preferred_element_type=F32)


def _dot_tn(a, b):
    return lax.dot_general(a, b, (((0,), (0,)), ((), ())), preferred_element_type=F32)


def _sig(x):
    return 1.0 / (1.0 + jnp.exp(-x))


def _ln_stats(r):
    mu = jnp.mean(r, axis=-1, keepdims=True)
    d = r - mu
    var = jnp.mean(d * d, axis=-1, keepdims=True)
    rstd = lax.rsqrt(var + LN_EPS)
    return d * rstd, rstd


def _ln_bwd(dxh, xh, rstd):
    m1 = jnp.mean(dxh, axis=-1, keepdims=True)
    m2 = jnp.mean(dxh * xh, axis=-1, keepdims=True)
    return rstd * (dxh - m1 - xh * m2)


def _rowsum(v):
    return jnp.sum(v, axis=0, keepdims=True)


def _row_tile(n, cap):
    if n <= cap:
        return n
    best = None
    for t in range(8, cap + 1, 8):
        if n % t == 0:
            best = t
    assert best is not None, (n, cap)
    return best


def _coords():
    x, y, c = lax.axis_index("x"), lax.axis_index("y"), lax.axis_index("c")
    return x, y, c


def _flip(v, bit):
    return 1 - v if bit else v


def small_allgather(blk, name):
    r, n = blk.shape

    def body(x_ref, out_ref, send_sems, recv_sems):
        x, y, c = _coords()
        me = 4 * x + 2 * y + c
        out_ref[me] = x_ref[...]
        copies = []
        for k in range(1, N_DEV):
            peer = (_flip(x, k & 4), _flip(y, k & 2), _flip(c, k & 1))
            cp = pltpu.make_async_remote_copy(
                src_ref=x_ref, dst_ref=out_ref.at[me], send_sem=send_sems.at[k - 1],
                recv_sem=recv_sems.at[k - 1], device_id=peer, device_id_type=MESH)
            cp.start()
            copies.append(cp)
        for k in range(1, N_DEV):
            px, py, pc = _flip(x, k & 4), _flip(y, k & 2), _flip(c, k & 1)
            slot = 4 * px + 2 * py + pc
            pltpu.make_async_remote_copy(
                src_ref=x_ref, dst_ref=out_ref.at[slot], send_sem=send_sems.at[k - 1],
                recv_sem=recv_sems.at[k - 1], device_id=(px, py, pc), device_id_type=MESH).wait_recv()
        for cp in copies:
            cp.wait_send()

    return pl.pallas_call(
        body, name=name,
        out_shape=SDS((N_DEV, r, n), blk.dtype),
        in_specs=[pl.BlockSpec(memory_space=pltpu.VMEM)],
        out_specs=pl.BlockSpec(memory_space=pltpu.VMEM),
        scratch_shapes=[pltpu.SemaphoreType.DMA((N_DEV - 1,)), pltpu.SemaphoreType.DMA((N_DEV - 1,))],
    )(blk)


class CommPlan:
    def __init__(self, ins, out_shape, scratch, emit):
        self.ins, self.out_shape, self.scratch, self.emit = list(ins), list(out_shape), list(scratch), emit


def _phase(step, at, fn):
    if step is None:
        fn()
    else:
        pl.when(step == at)(fn)


def gather_plan(shards):
    n = len(shards)
    per = 7

    def emit(ins, outs, sems, step, nsteps):
        send_sems, recv_sems, local_sems = sems
        x, y, c = _coords()
        me = 4 * x + 2 * y + c
        sibling = (x, y, 1 - c)
        chips = [(1 - x, y), (x, 1 - y), (1 - x, 1 - y)]

        def slot(px, py, pc):
            return 4 * px + 2 * py + pc

        def copy(t, k, block, to, src=None):
            dst = outs[t].at[slot(*block)]
            return pltpu.make_async_remote_copy(
                src_ref=dst if src is None else src, dst_ref=dst,
                send_sem=send_sems.at[per * t + k], recv_sem=recv_sems.at[per * t + k],
                device_id=to, device_id_type=MESH)

        def local(t):
            return pltpu.make_async_copy(ins[t], outs[t].at[me], local_sems.at[t])

        def first(t):
            return [copy(t, 0, (x, y, c), sibling, src=ins[t])] + [
                copy(t, 1 + j, (x, y, c), (*chip, c), src=ins[t]) for j, chip in enumerate(chips)]

        def passed(t):
            return [copy(t, 4 + j, (*chip, c), sibling) for j, chip in enumerate(chips)]

        def start():
            for t in range(n):
                local(t).start()
                for cp in first(t):
                    cp.start()

        def forward():
            for t in range(n):
                for j, chip in enumerate(chips):
                    copy(t, 1 + j, (*chip, c), (x, y, c)).wait_recv()
                    passed(t)[j].start()

        def finish():
            for t in range(n):
                copy(t, 0, (x, y, 1 - c), (x, y, c)).wait_recv()
                for j, chip in enumerate(chips):
                    copy(t, 4 + j, (*chip, 1 - c), (x, y, c)).wait_recv()
            for t in range(n):
                for cp in first(t) + passed(t):
                    cp.wait_send()
                local(t).wait()

        _phase(step, 0, start)
        _phase(step, None if step is None else max(nsteps - 2, 0), forward)
        _phase(step, None if step is None else nsteps - 1, finish)

    return CommPlan(shards, [SDS((N_DEV,) + s.shape, s.dtype) for s in shards],
                    [pltpu.SemaphoreType.DMA((per * n,)), pltpu.SemaphoreType.DMA((per * n,)),
                     pltpu.SemaphoreType.DMA((n,))], emit)


def chips_plan(sums):
    n = len(sums)

    def emit(ins, outs, sems, step, nsteps):
        send_sems, recv_sems = sems
        x, y, c = _coords()

        def copies():
            return [pltpu.make_async_remote_copy(
                src_ref=ins[t].at[j - 1], dst_ref=outs[t].at[j - 1], send_sem=send_sems.at[3 * t + j - 1],
                recv_sem=recv_sems.at[3 * t + j - 1], device_id=(_flip(x, j & 2), _flip(y, j & 1), c),
                device_id_type=MESH) for t in range(n) for j in range(1, 4)]

        def start():
            for cp in copies():
                cp.start()

        def finish():
            for cp in copies():
                cp.wait_recv()
            for cp in copies():
                cp.wait_send()

        _phase(step, 0, start)
        _phase(step, None if step is None else nsteps - 1, finish)

    return CommPlan(sums, [SDS(s.shape, s.dtype) for s in sums],
                    [pltpu.SemaphoreType.DMA((3 * n,)), pltpu.SemaphoreType.DMA((3 * n,))], emit)


def run_comm(plan, name):
    n, m = len(plan.ins), len(plan.out_shape)

    def body(*refs):
        plan.emit(refs[:n], refs[n:n + m], refs[n + m:], None, 1)

    anyspec = pl.BlockSpec(memory_space=pl.ANY)
    return pl.pallas_call(body, name=name, out_shape=plan.out_shape, in_specs=[anyspec] * n,
                          out_specs=[anyspec] * m, scratch_shapes=plan.scratch)(*plan.ins)


class _Rider:
    def __init__(self, plan):
        self.plan = plan
        anyspec = pl.BlockSpec(memory_space=pl.ANY)
        self.ins = plan.ins if plan else []
        self.in_specs = [anyspec] * len(self.ins)
        self.out_specs = [anyspec] * (len(plan.out_shape) if plan else 0)
        self.out_shape = plan.out_shape if plan else []
        self.scratch = plan.scratch if plan else []

    def split(self, refs, n_in, n_out, n_scratch=0):
        ni, no = len(self.ins), len(self.out_shape)
        own_in = refs[:n_in]
        c_in = refs[n_in:n_in + ni]
        own_out = refs[n_in + ni:n_in + ni + n_out]
        c_out = refs[n_in + ni + n_out:n_in + ni + n_out + no]
        rest = refs[n_in + ni + n_out + no:]
        own_scr, c_scr = rest[:n_scratch], rest[n_scratch:]

        def ride(step, nsteps):
            if self.plan:
                self.plan.emit(c_in, c_out, c_scr, step, nsteps)

        return tuple(own_in) + tuple(own_out) + tuple(own_scr), ride

    def result(self, outs, n_out):
        outs = list(outs) if isinstance(outs, (list, tuple)) else [outs]
        return outs[:n_out], (outs[n_out:] if self.plan else None)

    def sem(self, sem):
        return tuple("arbitrary" for _ in sem) if self.plan else sem


def sibling_plan(grads):
    n = len(grads)

    def emit(ins, outs, sems, step, nsteps):
        send_sems, recv_sems = sems
        x, y, c = _coords()

        def copies():
            return [pltpu.make_async_remote_copy(
                src_ref=ins[t].at[:, 1 - c], dst_ref=outs[t], send_sem=send_sems.at[t],
                recv_sem=recv_sems.at[t], device_id=(x, y, 1 - c), device_id_type=MESH) for t in range(n)]

        def start():
            for cp in copies():
                cp.start()

        def finish():
            for cp in copies():
                cp.wait_recv()
            for cp in copies():
                cp.wait_send()

        _phase(step, 0, start)
        _phase(step, None if step is None else nsteps - 1, finish)

    return CommPlan(grads, [SDS((4,) + g.shape[2:], g.dtype) for g in grads],
                    [pltpu.SemaphoreType.DMA((n,)), pltpu.SemaphoreType.DMA((n,))], emit)


def chip_sum(g42, recv, qc, name):
    _, _, R, C = g42.shape
    tr = _row_tile(R, 256)

    def body(qc_ref, a_ref, b_ref, own_ref, send_ref):
        j = pl.program_id(1)
        s = a_ref[...] + b_ref[...]

        @pl.when(j == 0)
        def _():
            own_ref[...] = s

        @pl.when(j > 0)
        def _():
            send_ref[...] = s.astype(BF)

    gs = pltpu.PrefetchScalarGridSpec(
        num_scalar_prefetch=1, grid=(R // tr, 4),
        in_specs=[pl.BlockSpec((None, None, tr, C), lambda i, j, s: (jnp.bitwise_xor(s[0], j), s[1], i, 0)),
                  pl.BlockSpec((None, tr, C), lambda i, j, s: (jnp.bitwise_xor(s[0], j), i, 0))],
        out_specs=[pl.BlockSpec((tr, C), lambda i, j, s: (i, 0)),
                   pl.BlockSpec((None, tr, C), lambda i, j, s: (jnp.maximum(j - 1, 0), i, 0))])
    return pl.pallas_call(body, name=name, grid_spec=gs, out_shape=[SDS((R, C), F32), SDS((3, R, C), BF)],
                          compiler_params=_cparams(("arbitrary", "arbitrary")))(qc, g42, recv)


def small_sum(g8, name):
    def body(g_ref, o_ref):
        acc = g_ref[0]
        for k in range(1, N_DEV):
            acc = acc + g_ref[k]
        o_ref[...] = acc
    return pl.pallas_call(body, name=name, out_shape=SDS(g8.shape[1:], F32))(g8)


def adamw(w, m, v, parts, name):
    R, C = w.shape
    tr = _row_tile(R, 256)
    npart = len(parts)
    c1 = 1.0 / (1.0 - ADAM_B1 ** ADAM_STEP)
    c2 = 1.0 / (1.0 - ADAM_B2 ** ADAM_STEP)

    def body(*refs):
        w_ref, m_ref, v_ref = refs[:3]
        p_refs = refs[3:3 + npart]
        g_ref, d_ref, nm_ref, nv_ref = refs[3 + npart:]
        g = p_refs[0][...].astype(F32)
        for p in p_refs[1:]:
            g = g + p[...].astype(F32)
        nm = ADAM_B1 * m_ref[...] + (1.0 - ADAM_B1) * g
        nv = ADAM_B2 * v_ref[...] + (1.0 - ADAM_B2) * (g * g)
        mh = nm * c1
        vh = nv * c2
        g_ref[...] = g
        nm_ref[...] = nm
        nv_ref[...] = nv
        d_ref[...] = -ADAM_LR * (mh / (jnp.sqrt(vh) + ADAM_EPS) + ADAM_WD * w_ref[...])

    wspec = pl.BlockSpec((tr, C), lambda i: (i, 0))
    pspecs = [pl.BlockSpec(bs(tr, C), im) for (_, bs, im) in parts]
    outs = pl.pallas_call(
        body, name=name, grid=(R // tr,),
        in_specs=[wspec] * 3 + pspecs, out_specs=[wspec] * 4,
        out_shape=[SDS((R, C), F32)] * 4,
        compiler_params=_cparams(("parallel",)))(w, m, v, *[p[0] for p in parts])
    return outs


def _plain_part(g):
    return (g, lambda tr, C: (tr, C), lambda i: (i, 0))


def _slot_part(g, slot):
    return (g, lambda tr, C: (None, tr, C), lambda i, s=slot: (s, i, 0))


def ada_fwd(c_all, w_cols, b_cols, name):
    Bg, D = c_all.shape
    n = w_cols.shape[1]

    def body(c_ref, w_ref, b_ref, o_ref, s_ref):
        cc = c_ref[...]
        s = cc * _sig(cc)
        s_ref[...] = s
        o_ref[...] = jnp.dot(s, w_ref[...], preferred_element_type=F32, precision=lax.Precision.HIGHEST) + b_ref[...]

    return pl.pallas_call(body, name=name, out_shape=[SDS((Bg, n), F32), SDS((Bg, D), F32)],
                          compiler_params=_cparams())(c_all, w_cols, b_cols)


def ada_bwd(s_all, dmod_cols, dmod_all, name):
    Bg, D = s_all.shape
    n = dmod_cols.shape[1]

    def body(s_ref, dc_ref, da_ref, gw_ref, gb_ref):
        gw_ref[...] = lax.dot_general(s_ref[...], dc_ref[...], (((0,), (0,)), ((), ())),
                                      preferred_element_type=F32, precision=lax.Precision.HIGHEST)
        acc = da_ref[0:1, :]
        for r in range(1, Bg):
            acc = acc + da_ref[r:r + 1, :]
        gb_ref[...] = acc

    return pl.pallas_call(body, name=name, out_shape=[SDS((D, n), F32), SDS((1, dmod_all.shape[1]), F32)],
                          compiler_params=_cparams())(s_all, dmod_cols, dmod_all)


def _vec(D, rank):
    return pl.BlockSpec((1, D), (lambda i: (0, 0)) if rank == 1 else (lambda i, j: (0, 0)))


def _modspec(D, tpb, rank):
    if rank == 1:
        return pl.BlockSpec((None, 1, D), lambda i: (i // tpb, 0, 0))
    return pl.BlockSpec((None, 1, D), lambda i, j: (i // tpb, 0, 0))


def _resident(shape):
    return pl.BlockSpec(shape, lambda *_: (0,) * len(shape), pipeline_mode=pl.Buffered(1))


def ffn_up(xs, pg, pb, sc, sh, wg8, name, comm=None):
    T, D = xs.shape
    n2, _, nb = wg8.shape
    nj = n2 // 2
    S = T // sc.shape[0]
    tm = min(512, S)
    tpb = S // tm
    rider = _Rider(comm)

    def body(*refs):
        (x_ref, pg_ref, pb_ref, sc_ref, sh_ref, w_ref, u_ref, a_ref, g_ref, p_ref), ride = rider.split(refs, 6, 4)
        ride(pl.program_id(0), T // tm)
        xin = x_ref[...] * pg_ref[...] + pb_ref[...]
        u_ref[...] = (xin * (1.0 + sc_ref[...]) + sh_ref[...]).astype(BF)

        def col_block(j, _):
            u = u_ref[...]
            a = _dot_nn(u, w_ref[j])
            g = _dot_nn(u, w_ref[j + nj])
            a_ref[j] = a.astype(BF)
            g_ref[j] = g.astype(BF)
            p_ref[j] = ((a * _sig(a)) * g).astype(BF)
            return 0

        lax.fori_loop(0, nj, col_block, 0)

    blk = pl.BlockSpec((nj, tm, nb), lambda i: (0, i, 0))
    row = pl.BlockSpec((tm, D), lambda i: (i, 0))
    outs = pl.pallas_call(
        body, name=name, grid=(T // tm,),
        in_specs=[row, _vec(D, 1), _vec(D, 1), _modspec(D, tpb, 1), _modspec(D, tpb, 1), _resident(wg8.shape)]
        + rider.in_specs,
        out_specs=[row, blk, blk, blk] + rider.out_specs,
        out_shape=[SDS((T, D), BF)] + [SDS((nj, T, nb), BF)] * 3 + rider.out_shape,
        scratch_shapes=rider.scratch,
        compiler_params=_cparams(rider.sem(("parallel",))))(xs, pg, pb, sc, sh, wg8, *rider.ins)
    return rider.result(outs, 4)


def ffn_down_ln(p4, wd3, xs, pg, pb, gate, name):
    nj, T, nb = p4.shape
    D = wd3.shape[2]
    S = T // gate.shape[0]
    tm = min(512, S)
    tpb = S // tm

    def body(p_ref, wd_ref, x_ref, pg_ref, pb_ref, gate_ref, xh_ref, rs_ref, f_ref):
        f = _dot_nn(p_ref[0], wd_ref[0])
        for k in range(1, nj):
            f = f + _dot_nn(p_ref[k], wd_ref[k])
        xin = x_ref[...] * pg_ref[...] + pb_ref[...]
        r = DEEPNORM_ALPHA * xin + gate_ref[...] * (MACARON_WEIGHT * f)
        xh, rstd = _ln_stats(r)
        xh_ref[...] = xh
        rs_ref[...] = rstd
        f_ref[...] = f.astype(BF)

    row = pl.BlockSpec((tm, D), lambda i: (i, 0))
    return pl.pallas_call(
        body, name=name, grid=(T // tm,),
        in_specs=[pl.BlockSpec((nj, tm, nb), lambda i: (0, i, 0)), _resident(wd3.shape),
                  row, _vec(D, 1), _vec(D, 1), _modspec(D, tpb, 1)],
        out_specs=[row, pl.BlockSpec((tm, 1), lambda i: (i, 0)), row],
        out_shape=[SDS((T, D), F32), SDS((T, 1), F32), SDS((T, D), BF)],
        compiler_params=_cparams(("parallel",)))(p4, wd3, xs, pg, pb, gate)


def mod_matmul(xs, pg, pb, sc, sh, w, name, comm=None):
    T, D = xs.shape
    N = w.shape[1]
    S = T // sc.shape[0]
    tm = min(256, S)
    tpb = S // tm
    rider = _Rider(comm)

    def body(*refs):
        (x_ref, pg_ref, pb_ref, sc_ref, sh_ref, w_ref, u_ref, o_ref), ride = rider.split(refs, 6, 2)
        ride(pl.program_id(0), T // tm)
        xin = x_ref[...] * pg_ref[...] + pb_ref[...]
        u = (xin * (1.0 + sc_ref[...]) + sh_ref[...]).astype(BF)
        u_ref[...] = u
        for n in range(N // D):
            o_ref[:, n * D:(n + 1) * D] = _dot_nn(u, w_ref[:, n * D:(n + 1) * D])

    row = pl.BlockSpec((tm, D), lambda i: (i, 0))
    outs = pl.pallas_call(
        body, name=name, grid=(T // tm,),
        in_specs=[row, _vec(D, 1), _vec(D, 1), _modspec(D, tpb, 1), _modspec(D, tpb, 1), _resident(w.shape)]
        + rider.in_specs,
        out_specs=[row, pl.BlockSpec((tm, N), lambda i: (i, 0))] + rider.out_specs,
        out_shape=[SDS((T, D), BF), SDS((T, N), F32)] + rider.out_shape,
        scratch_shapes=rider.scratch,
        compiler_params=_cparams(rider.sem(("parallel",))))(xs, pg, pb, sc, sh, w, *rider.ins)
    return rider.result(outs, 2)


ATT_TQ = 1024
ATT_TK = 256


def _att_consts(tk):
    r = lax.broadcasted_iota(jnp.int32, (tk + 8, tk), 0)
    c = lax.broadcasted_iota(jnp.int32, (tk + 8, tk), 1)
    usum = jnp.where((r >= tk) | (c > r), 1.0, 0.0).astype(BF)
    lsum = jnp.where((r >= tk) | (c < r), 1.0, 0.0).astype(BF)
    dmask = lax.broadcasted_iota(jnp.int32, (tk, tk), 0) < lax.broadcasted_iota(jnp.int32, (tk, tk), 1)
    return usum, lsum, dmask


def _split_dot(m, v):
    hi = v.astype(BF)
    lo = (v - hi.astype(F32)).astype(BF)
    return _dot_nn(m, hi) + _dot_nn(m, lo)


def _softplus(z):
    return jnp.maximum(z, 0.0) + jnp.log(1.0 + jnp.exp(-jnp.abs(z)))


def _att_dims(S, D):
    dh = SB_HEAD_DIM
    cw = min(LANE, D)
    tq = min(ATT_TQ, S)
    tk = min(ATT_TK, tq)
    assert tq % tk == 0 and S % tq == 0
    return dh, cw, cw // dh, D // cw, tq, tk, S // tq, S // tk


def att_fwd(proj, Bl, S, D, name):
    dh, cw, hp, nblk, tq, tk, nq, nk = _att_dims(S, D)
    scale = 1.0 / math.sqrt(dh)
    assert math.log2(scale) == int(math.log2(scale))
    H = D // dh

    def body(q_ref, k_ref, v_ref, o_ref, car_ref, qs, ks, vts):
        usum, _, dmask = _att_consts(tk)
        for hh in range(hp):
            sl = slice(hh * dh, (hh + 1) * dh)
            qs[hh] = (q_ref[:, sl] * scale).astype(BF)
            ks[hh] = k_ref[:, sl].astype(BF)
            for kb in range(nk):
                vts[hh, kb] = v_ref[kb * tk:(kb + 1) * tk, sl].T.astype(BF)
        nch = tq // tk

        def qloop(qb, _):
            qo = pl.multiple_of(qb * tq, tq)
            n_full = qb * nch

            def blk(kb, state, diag):
                ko = pl.multiple_of(kb * tk, tk)
                chains = [(hh, c) for hh in range(hp) for c in range(0 if diag is None else diag, nch)]

                def masked(ch, val):
                    return jnp.where(dmask, val, 0.0) if ch[1] == diag else val

                z = {ch: _dot_nt(ks[ch[0], pl.ds(ko, tk), :], qs[ch[0], pl.ds(pl.multiple_of(qo + ch[1] * tk, tk), tk), :])
                     for ch in chains}
                sp = {ch: _softplus(z[ch]) for ch in chains}
                lk = {ch: masked(ch, -sp[ch]) for ch in chains}
                for hh, c in chains:
                    car_ref[hh, qb * nk + kb, :, c * tk:(c + 1) * tk] = state[hh][c][0]
                cs = {ch: _split_dot(usum, lk[ch]) for ch in chains}
                w = {ch: masked(ch, jnp.exp((z[ch] - sp[ch]) + state[ch[0]][ch[1]][0][0:1, :] + cs[ch][:tk]))
                     for ch in chains}
                pv = {ch: _dot_nn(vts[ch[0], kb], w[ch].astype(BF)) for ch in chains}
                return tuple(tuple(
                    (state[hh][c][0] + cs[(hh, c)][tk:], state[hh][c][1] + pv[(hh, c)]) if (hh, c) in z else state[hh][c]
                    for c in range(nch)) for hh in range(hp))

            state = tuple(tuple((jnp.zeros((8, tk), F32), jnp.zeros((dh, tk), F32)) for _ in range(nch))
                          for _ in range(hp))
            for i in reversed(range(nch)):
                state = blk(n_full + i, state, i)
            state = lax.fori_loop(0, n_full, lambda j, st: blk(n_full - 1 - j, st, None), state)
            for hh in range(hp):
                for c in range(nch):
                    o_ref[pl.ds(pl.multiple_of(qo + c * tk, tk), tk), hh * dh:(hh + 1) * dh] = (
                        state[hh][c][1].T.astype(BF))
            return 0

        lax.fori_loop(0, nq, qloop, 0)

    def seg(s):
        return pl.BlockSpec((S, cw), lambda b, h: (b, s * nblk + h))

    return pl.pallas_call(
        body, name=name, grid=(Bl, nblk),
        in_specs=[seg(0), seg(1), seg(2)],
        out_specs=[pl.BlockSpec((S, cw), lambda b, h: (b, h)),
                   pl.BlockSpec((None, hp, nq * nk, 8, tq), lambda b, h: (b, h, 0, 0, 0))],
        out_shape=[SDS((Bl * S, D), BF), SDS((Bl, H, nq * nk, 8, tq), F32)],
        scratch_shapes=[pltpu.VMEM((hp, S, dh), BF)] * 2 + [pltpu.VMEM((hp, nk, dh, tk), BF)],
        compiler_params=_cparams(("parallel", "parallel")))(proj, proj, proj)


def conv_fwd(proj, cw32, cb, cg, cbeta, Bl, S, D, kw, name):
    T = Bl * S
    ts = min(128, S)
    ns = S // ts
    off = HALO - (kw - 1)
    rc = min(64, ts)
    cw = min(LANE, D)

    def body(a_ref, b_ref, ha_ref, hb_ref, w_ref, cb_ref, g_ref, be_ref, cs_ref, xh_ref, rs_ref, hsh, conv_s):
        i = pl.program_id(1)
        h = a_ref[...] * _sig(b_ref[...])
        hh = jnp.where(i == 0, 0.0, ha_ref[...] * _sig(hb_ref[...]))
        for cb_ in range(D // cw):
            cols = slice(cb_ * cw, (cb_ + 1) * cw)
            hsh[0, cb_, pl.ds(HALO, ts), :] = h[:, cols]
            hsh[0, cb_, pl.ds(0, HALO), :] = hh[:, cols]
            for s in range(1, SUBLANES):
                hsh[s, cb_, pl.ds(0, ts + HALO - SUBLANES), :] = hsh[0, cb_, pl.ds(s, ts + HALO - SUBLANES), :]
            accs = [jnp.zeros((rc, cw), F32) for _ in range(ts // rc)]
            for k in range(kw):
                wk = w_ref[k:k + 1, cols]
                s = (off + k) % SUBLANES
                for r in range(ts // rc):
                    accs[r] = accs[r] + wk * hsh[s, cb_, pl.ds(r * rc + off + k - s, rc), :]
            for r in range(ts // rc):
                conv_s[pl.ds(r * rc, rc), cols] = accs[r]
        conv = conv_s[...] + cb_ref[...]
        xh, rstd = _ln_stats(conv)
        xh_ref[...] = xh
        rs_ref[...] = rstd
        cl = xh * g_ref[...] + be_ref[...]
        cs_ref[...] = (cl * _sig(cl)).astype(BF)

    hpb = ts // HALO

    def tile(seg):
        return pl.BlockSpec((ts, D), lambda b, i: (b * ns + i, seg))

    def halo(seg):
        return pl.BlockSpec((HALO, D), lambda b, i: (jnp.maximum((b * ns + i) * hpb - 1, 0), seg))

    row = pl.BlockSpec((ts, D), lambda b, i: (b * ns + i, 0))
    vec = pl.BlockSpec((1, D), lambda b, i: (0, 0))
    return pl.pallas_call(
        body, name=name, grid=(Bl, ns),
        in_specs=[tile(3), tile(4), halo(3), halo(4), pl.BlockSpec((HALO, D), lambda b, i: (0, 0)), vec, vec, vec],
        out_specs=[row, row, pl.BlockSpec((ts, 1), lambda b, i: (b * ns + i, 0))],
        out_shape=[SDS((T, D), BF), SDS((T, D), F32), SDS((T, 1), F32)],
        scratch_shapes=[pltpu.VMEM((SUBLANES, D // cw, ts + HALO, cw), F32), pltpu.VMEM((ts, D), F32)],
        compiler_params=_cparams(("parallel", "arbitrary")))(proj, proj, proj, proj, cw32, cb, cg, cbeta)


def mix_fwd(yatt, cs, proj, wsb, wco, wout, xs, pg, pb, gate, name):
    T, D = yatt.shape
    S = T // gate.shape[0]
    tm = min(256, S)
    tpb = S // tm

    def body(ya_ref, cs_ref, ga_ref, gb_ref, wsb_ref, wco_ref, wout_ref, x_ref, pg_ref, pb_ref, gate_ref,
             xh_ref, rs_ref, ysb_ref, yco_ref, mg_ref, o_ref):
        ysb = _dot_nn(ya_ref[...], wsb_ref[...])
        yco = _dot_nn(cs_ref[...], wco_ref[...])
        merged = _sig(ga_ref[...]) * ysb + _sig(gb_ref[...]) * yco
        mg = merged.astype(BF)
        o = _dot_nn(mg, wout_ref[...])
        xin = x_ref[...] * pg_ref[...] + pb_ref[...]
        r = DEEPNORM_ALPHA * xin + gate_ref[...] * o
        xh, rstd = _ln_stats(r)
        xh_ref[...] = xh
        rs_ref[...] = rstd
        ysb_ref[...] = ysb.astype(BF)
        yco_ref[...] = yco.astype(BF)
        mg_ref[...] = mg
        o_ref[...] = o.astype(BF)

    row = pl.BlockSpec((tm, D), lambda i: (i, 0))
    wfull = pl.BlockSpec((D, D), lambda i: (0, 0))
    return pl.pallas_call(
        body, name=name, grid=(T // tm,),
        in_specs=[row, row, pl.BlockSpec((tm, D), lambda i: (i, 5)), pl.BlockSpec((tm, D), lambda i: (i, 6)),
                  wfull, wfull, wfull, row, _vec(D, 1), _vec(D, 1), _modspec(D, tpb, 1)],
        out_specs=[row, pl.BlockSpec((tm, 1), lambda i: (i, 0)), row, row, row, row],
        out_shape=[SDS((T, D), F32), SDS((T, 1), F32)] + [SDS((T, D), BF)] * 4,
        compiler_params=_cparams(("parallel",)))(yatt, cs, proj, proj, wsb, wco, wout, xs, pg, pb, gate)


def ln_bwd(dout, xh, rstd, lng, lnb, gate, sub, res_w, name, target=None):
    T, D = xh.shape
    Bl = gate.shape[0]
    S = T // Bl
    tm = min(256, S)
    tpb = S // tm
    first = target is not None

    def body(*refs):
        if first:
            tg_ref, xh_ref, rs_ref, g_ref, b_ref, gate_ref, sub_ref = refs[:7]
            dr_ref, ds_ref, dg_ref, db_ref, dgate_ref, loss_ref = refs[7:]
        else:
            do_ref, xh_ref, rs_ref, g_ref, b_ref, gate_ref, sub_ref = refs[:7]
            dr_ref, ds_ref, dg_ref, db_ref, dgate_ref = refs[7:]
        i = pl.program_id(0)
        xh_ = xh_ref[...]
        if first:
            diff = (xh_ * g_ref[...] + b_ref[...]) - tg_ref[...]
            lsum = jnp.sum(jnp.sum(diff * diff, axis=1, keepdims=True), axis=0, keepdims=True) * (0.5 / D)
            do = diff * (1.0 / D)
        else:
            do = do_ref[...]

        @pl.when(i == 0)
        def _():
            dg_ref[...] = jnp.zeros_like(dg_ref)
            db_ref[...] = jnp.zeros_like(db_ref)
            if first:
                loss_ref[...] = jnp.zeros_like(loss_ref)

        @pl.when(i % tpb == 0)
        def _():
            dgate_ref[...] = jnp.zeros_like(dgate_ref)

        if first:
            loss_ref[...] += jnp.broadcast_to(lsum, loss_ref.shape)
        dg_ref[...] += _rowsum(do * xh_)
        db_ref[...] += _rowsum(do)
        dr = _ln_bwd(do * g_ref[...], xh_, rs_ref[...])
        dr_ref[...] = dr
        ds_ref[...] = (dr * gate_ref[...] * res_w).astype(BF)
        dgate_ref[...] += _rowsum(dr * (res_w * sub_ref[...].astype(F32)))

    row = pl.BlockSpec((tm, D), lambda i: (i, 0))
    vec = _vec(D, 1)
    mod = _modspec(D, tpb, 1)
    out_specs = [row, row, vec, vec, mod]
    out_shape = [SDS((T, D), F32), SDS((T, D), BF), SDS((1, D), F32), SDS((1, D), F32), SDS((Bl, 1, D), F32)]
    if first:
        out_specs.append(pl.BlockSpec((8, LANE), lambda i: (0, 0)))
        out_shape.append(SDS((8, LANE), F32))
    return pl.pallas_call(
        body, name=name, grid=(T // tm,),
        in_specs=[row, row, pl.BlockSpec((tm, 1), lambda i: (i, 0)), vec, vec, mod, row],
        out_specs=out_specs, out_shape=out_shape,
        compiler_params=_cparams(("arbitrary",)))(target if first else dout, xh, rstd, lng, lnb, gate, sub)


def swiglu_bwd(df, wd3, a4, g4, name):
    nj, T, nb = a4.shape
    D = df.shape[1]
    tm = min(512, T)

    def body(df_ref, wd_ref, a_ref, g_ref, o_ref):
        def col_block(j, _):
            dp = _dot_nt(df_ref[...], wd_ref[j])
            a = a_ref[j].astype(F32)
            g = g_ref[j].astype(F32)
            s = _sig(a)
            o_ref[0, j] = (dp * g * (s * (1.0 + a * (1.0 - s)))).astype(BF)
            o_ref[1, j] = (dp * (a * s)).astype(BF)
            return 0

        lax.fori_loop(0, nj, col_block, 0)

    blk = pl.BlockSpec((nj, tm, nb), lambda i: (0, i, 0))
    return pl.pallas_call(
        body, name=name, grid=(T // tm,),
        in_specs=[pl.BlockSpec((tm, D), lambda i: (i, 0)), _resident(wd3.shape), blk, blk],
        out_specs=pl.BlockSpec((2, nj, tm, nb), lambda i: (0, 0, i, 0)),
        out_shape=SDS((2, nj, T, nb), BF),
        compiler_params=_cparams(("parallel",)))(df, wd3, a4, g4)


def mod_bwd(dh, w, dr, xs, pg, pb, sc, blocked, name, comm=None):
    T, D = dr.shape
    Bl = sc.shape[0]
    S = T // Bl
    tm = min(512 if blocked else 256, S)
    tpb = S // tm
    if blocked:
        nk, _, kb = dh.shape
        dh_spec = pl.BlockSpec((nk, tm, kb), lambda i: (0, i, 0))
    else:
        kb = D
        nk = dh.shape[1] // kb
        dh_spec = pl.BlockSpec((tm, nk * kb), lambda i: (i, 0))

    rider = _Rider(comm)

    def body(*refs):
        (dh_ref, w_ref, dr_ref, x_ref, pg_ref, pb_ref, sc_ref, dx_ref, dsc_ref, dsh_ref), ride = rider.split(refs, 7, 3)
        i = pl.program_id(0)
        ride(i, T // tm)

        def part(k):
            if blocked:
                return _dot_nt(dh_ref[k], w_ref[k])
            return _dot_nt(dh_ref[:, k * kb:(k + 1) * kb], w_ref[:, k * kb:(k + 1) * kb])

        du = part(0)
        for k in range(1, nk):
            du = du + part(k)

        @pl.when(i % tpb == 0)
        def _():
            dsc_ref[...] = jnp.zeros_like(dsc_ref)
            dsh_ref[...] = jnp.zeros_like(dsh_ref)

        xin = x_ref[...] * pg_ref[...] + pb_ref[...]
        dx_ref[...] = DEEPNORM_ALPHA * dr_ref[...] + du * (1.0 + sc_ref[...])
        dsc_ref[...] += _rowsum(du * xin)
        dsh_ref[...] += _rowsum(du)

    row = pl.BlockSpec((tm, D), lambda i: (i, 0))
    mod = _modspec(D, tpb, 1)
    outs = pl.pallas_call(
        body, name=name, grid=(T // tm,),
        in_specs=[dh_spec, _resident(w.shape), row, row, _vec(D, 1), _vec(D, 1), mod] + rider.in_specs,
        out_specs=[row, mod, mod] + rider.out_specs,
        out_shape=[SDS((T, D), F32), SDS((Bl, 1, D), F32), SDS((Bl, 1, D), F32)] + rider.out_shape,
        scratch_shapes=rider.scratch,
        compiler_params=_cparams(("arbitrary",)))(dh, w, dr, xs, pg, pb, sc, *rider.ins)
    return rider.result(outs, 3)


def merge_bwd(do2, proj, ysb, yco, wsb, wco, wout, name):
    T, D = do2.shape
    tm = min(256, T)

    def body(do_ref, ga_ref, gb_ref, ysb_ref, yco_ref, wsb_ref, wco_ref, wout_ref,
             dysb_ref, dyco_ref, dga_ref, dgb_ref, dya_ref, dcs_ref):
        dm = _dot_nt(do_ref[...], wout_ref[...])
        sa = _sig(ga_ref[...])
        sb = _sig(gb_ref[...])
        dysb = (dm * sa).astype(BF)
        dyco = (dm * sb).astype(BF)
        dysb_ref[...] = dysb
        dyco_ref[...] = dyco
        dga_ref[...] = (dm * ysb_ref[...].astype(F32) * (sa * (1.0 - sa))).astype(BF)
        dgb_ref[...] = (dm * yco_ref[...].astype(F32) * (sb * (1.0 - sb))).astype(BF)
        dya_ref[...] = _dot_nt(dysb, wsb_ref[...]).astype(BF)
        dcs_ref[...] = _dot_nt(dyco, wco_ref[...])

    row = pl.BlockSpec((tm, D), lambda i: (i, 0))
    wfull = pl.BlockSpec((D, D), lambda i: (0, 0))
    return pl.pallas_call(
        body, name=name, grid=(T // tm,),
        in_specs=[row, pl.BlockSpec((tm, D), lambda i: (i, 5)), pl.BlockSpec((tm, D), lambda i: (i, 6)),
                  row, row, wfull, wfull, wfull],
        out_specs=[row] * 6,
        out_shape=[SDS((T, D), BF)] * 5 + [SDS((T, D), F32)],
        compiler_params=_cparams(("parallel",)))(do2, proj, proj, ysb, yco, wsb, wco, wout)


def att_bwd(proj, dyatt, car, Bl, S, D, name, comm=None):
    dh, cw, hp, nblk, tq, tk, nq, nk = _att_dims(S, D)
    scale = 1.0 / math.sqrt(dh)
    rider = _Rider(comm)

    def body(*refs):
        (q_ref, k_ref, v_ref, do_ref, car_ref, dq_ref, dk_ref, dv_ref,
         qs, ks, vs, dos, kts, dk_acc, dv_acc), ride = rider.split(refs, 5, 3, 7)
        ride(pl.program_id(0) * nblk + pl.program_id(1), Bl * nblk)
        usum, lsum, dmask = _att_consts(tk)
        for hh in range(hp):
            sl = slice(hh * dh, (hh + 1) * dh)
            qs[hh] = (q_ref[:, sl] * scale).astype(BF)
            ks[hh] = k_ref[:, sl].astype(BF)
            vs[hh] = v_ref[:, sl].astype(BF)
            dos[hh] = do_ref[:, sl]
            for kb in range(nk):
                kts[hh, kb] = k_ref[kb * tk:(kb + 1) * tk, sl].T.astype(BF)
        dk_acc[...] = jnp.zeros_like(dk_acc)
        dv_acc[...] = jnp.zeros_like(dv_acc)
        nch = tq // tk

        def qloop(qb, _):
            qo = pl.multiple_of(qb * tq, tq)
            n_full = qb * nch

            def blk(kb, state, diag):
                ko = pl.multiple_of(kb * tk, tk)
                chains = [(hh, c) for hh in range(hp) for c in range(0 if diag is None else diag, nch)]

                def masked(ch, val):
                    return jnp.where(dmask, val, 0.0) if ch[1] == diag else val

                def qrows(ref, ch):
                    return ref[ch[0], pl.ds(pl.multiple_of(qo + ch[1] * tk, tk), tk), :]

                k = [ks[hh, pl.ds(ko, tk), :] for hh in range(hp)]
                v = [vs[hh, pl.ds(ko, tk), :] for hh in range(hp)]
                z = {ch: _dot_nt(k[ch[0]], qrows(qs, ch)) for ch in chains}
                dw = {ch: _dot_nt(v[ch[0]], qrows(dos, ch)) for ch in chains}
                sp = {ch: _softplus(z[ch]) for ch in chains}
                lk = {ch: masked(ch, -sp[ch]) for ch in chains}
                cs = {ch: _split_dot(usum, lk[ch]) for ch in chains}
                w = {ch: masked(ch, jnp.exp((z[ch] - sp[ch])
                                            + car_ref[ch[0], qb * nk + kb, 0:1, ch[1] * tk:(ch[1] + 1) * tk]
                                            + cs[ch][:tk])) for ch in chains}
                dlw = {ch: dw[ch] * w[ch] for ch in chains}
                gs = {ch: _split_dot(lsum, dlw[ch]) for ch in chains}
                sg = {ch: jnp.exp(z[ch] - sp[ch]) for ch in chains}
                dzb = {ch: masked(ch, dlw[ch] * (1.0 - sg[ch])
                                  - sg[ch] * (state[ch[0]][ch[1]][0][0:1, :] + gs[ch][:tk])).astype(BF)
                       for ch in chains}
                wb = {ch: w[ch].astype(BF) for ch in chains}
                for hh in range(hp):
                    mine = [ch for ch in chains if ch[0] == hh]
                    dk_acc[hh, kb] += sum(_dot_nn(dzb[ch], qrows(qs, ch)) for ch in mine)
                    dv_acc[hh, kb] += sum(_dot_nn(wb[ch], qrows(dos, ch)) for ch in mine)
                dq = {ch: _dot_nn(kts[ch[0], kb], dzb[ch]) for ch in chains}
                return tuple(tuple(
                    (state[hh][c][0] + gs[(hh, c)][tk:], state[hh][c][1] + dq[(hh, c)]) if (hh, c) in z else state[hh][c]
                    for c in range(nch)) for hh in range(hp))

            state = tuple(tuple((jnp.zeros((8, tk), F32), jnp.zeros((dh, tk), F32)) for _ in range(nch))
                          for _ in range(hp))
            state = lax.fori_loop(0, n_full, lambda kb, st: blk(kb, st, None), state)
            for i in range(nch):
                state = blk(n_full + i, state, i)
            for hh in range(hp):
                for c in range(nch):
                    dq_ref[pl.ds(pl.multiple_of(qo + c * tk, tk), tk), hh * dh:(hh + 1) * dh] = (
                        (state[hh][c][1].T * scale).astype(BF))
            return 0

        lax.fori_loop(0, nq, qloop, 0)
        for hh in range(hp):
            sl = slice(hh * dh, (hh + 1) * dh)
            for kb in range(nk):
                dk_ref[kb * tk:(kb + 1) * tk, sl] = dk_acc[hh, kb].astype(BF)
                dv_ref[kb * tk:(kb + 1) * tk, sl] = dv_acc[hh, kb].astype(BF)

    def seg(s):
        return pl.BlockSpec((S, cw), lambda b, h: (b, s * nblk + h))

    blk_spec = pl.BlockSpec((S, cw), lambda b, h: (b, h))
    outs = pl.pallas_call(
        body, name=name, grid=(Bl, nblk),
        in_specs=[seg(0), seg(1), seg(2), blk_spec,
                  pl.BlockSpec((None, hp, nq * nk, 8, tq), lambda b, h: (b, h, 0, 0, 0))] + rider.in_specs,
        out_specs=[blk_spec, blk_spec, blk_spec] + rider.out_specs,
        out_shape=[SDS((Bl * S, D), BF)] * 3 + rider.out_shape,
        scratch_shapes=[pltpu.VMEM((hp, S, dh), BF)] * 4 + [pltpu.VMEM((hp, nk, dh, tk), BF)]
        + [pltpu.VMEM((hp, nk, tk, dh), F32)] * 2 + rider.scratch,
        compiler_params=_cparams(rider.sem(("parallel", "parallel"))))(proj, proj, proj, dyatt, car, *rider.ins)
    return rider.result(outs, 3)


def conv_ln_bwd(dcs, xhc, rstd_c, cg, cbeta, name):
    T, D = dcs.shape
    tm = min(256, T)

    def body(dcs_ref, xh_ref, rs_ref, g_ref, b_ref, dconv_ref, dg_ref, db_ref, dcb_ref):
        @pl.when(pl.program_id(0) == 0)
        def _():
            dg_ref[...] = jnp.zeros_like(dg_ref)
            db_ref[...] = jnp.zeros_like(db_ref)
            dcb_ref[...] = jnp.zeros_like(dcb_ref)
        xh = xh_ref[...]
        cl = xh * g_ref[...] + b_ref[...]
        s = _sig(cl)
        dcl = dcs_ref[...] * (s * (1.0 + cl * (1.0 - s)))
        dg_ref[...] += _rowsum(dcl * xh)
        db_ref[...] += _rowsum(dcl)
        dconv = _ln_bwd(dcl * g_ref[...], xh, rs_ref[...])
        dconv_ref[...] = dconv
        dcb_ref[...] += _rowsum(dconv)

    row = pl.BlockSpec((tm, D), lambda i: (i, 0))
    vec = _vec(D, 1)
    return pl.pallas_call(
        body, name=name, grid=(T // tm,),
        in_specs=[row, row, pl.BlockSpec((tm, 1), lambda i: (i, 0)), vec, vec],
        out_specs=[row, vec, vec, vec],
        out_shape=[SDS((T, D), F32)] + [SDS((1, D), F32)] * 3,
        compiler_params=_cparams(("arbitrary",)))(dcs, xhc, rstd_c, cg, cbeta)


def conv_bwd(dconv, proj, cw32, Bl, S, D, kw, name):
    T = Bl * S
    ts = min(128, S)
    ns = S // ts
    off = HALO - (kw - 1)
    rc = min(64, ts)
    cw = min(LANE, D)
    hpb = ts // HALO
    nhb = T // HALO

    def body(dc_ref, dcn_ref, a_ref, b_ref, ha_ref, hb_ref, w_ref, da_ref, db_ref, dw_ref, hsh, dsh, dh_s):
        b_ = pl.program_id(0)
        i = pl.program_id(1)
        span = ts + HALO - SUBLANES

        @pl.when((b_ == 0) & (i == 0))
        def _():
            dw_ref[...] = jnp.zeros_like(dw_ref)
        a = a_ref[...]
        sb = _sig(b_ref[...])
        h = a * sb
        hh = jnp.where(i == 0, 0.0, ha_ref[...] * _sig(hb_ref[...]))
        dc = dc_ref[...]
        dcn = jnp.where(i == ns - 1, 0.0, dcn_ref[...])
        for cb_ in range(D // cw):
            cols = slice(cb_ * cw, (cb_ + 1) * cw)
            hsh[0, cb_, pl.ds(HALO, ts), :] = h[:, cols]
            hsh[0, cb_, pl.ds(0, HALO), :] = hh[:, cols]
            dsh[0, cb_, pl.ds(0, ts), :] = dc[:, cols]
            dsh[0, cb_, pl.ds(ts, HALO), :] = dcn[:, cols]
            for s in range(1, SUBLANES):
                hsh[s, cb_, pl.ds(0, span), :] = hsh[0, cb_, pl.ds(s, span), :]
                dsh[s, cb_, pl.ds(0, span), :] = dsh[0, cb_, pl.ds(s, span), :]
            accs = [jnp.zeros((rc, cw), F32) for _ in range(ts // rc)]
            d0 = [dsh[0, cb_, pl.ds(r * rc, rc), :] for r in range(ts // rc)]
            for k in range(kw):
                wk = w_ref[k:k + 1, cols]
                wsum = jnp.zeros((rc, cw), F32)
                sd = ((kw - 1) - k) % SUBLANES
                sh_ = (off + k) % SUBLANES
                for r in range(ts // rc):
                    accs[r] = accs[r] + wk * dsh[sd, cb_, pl.ds(r * rc + (kw - 1) - k - sd, rc), :]
                    wsum = wsum + d0[r] * hsh[sh_, cb_, pl.ds(r * rc + off + k - sh_, rc), :]
                dw_ref[k:k + 1, cols] += _rowsum(wsum)
            for r in range(ts // rc):
                dh_s[pl.ds(r * rc, rc), cols] = accs[r]
        dhc = dh_s[...]
        da_ref[...] = (dhc * sb).astype(BF)
        db_ref[...] = (dhc * a * (sb * (1.0 - sb))).astype(BF)

    def tile(seg):
        return pl.BlockSpec((ts, D), lambda b, i: (b * ns + i, seg))

    def halo(seg):
        return pl.BlockSpec((HALO, D), lambda b, i: (jnp.maximum((b * ns + i) * hpb - 1, 0), seg))

    row = pl.BlockSpec((ts, D), lambda b, i: (b * ns + i, 0))
    nxt = pl.BlockSpec((HALO, D), lambda b, i: (jnp.minimum((b * ns + i + 1) * hpb, nhb - 1), 0))
    wspec = pl.BlockSpec((HALO, D), lambda b, i: (0, 0))
    return pl.pallas_call(
        body, name=name, grid=(Bl, ns),
        in_specs=[row, nxt, tile(3), tile(4), halo(3), halo(4), wspec],
        out_specs=[row, row, wspec],
        out_shape=[SDS((T, D), BF), SDS((T, D), BF), SDS((HALO, D), F32)],
        scratch_shapes=[pltpu.VMEM((SUBLANES, D // cw, ts + HALO, cw), F32)] * 2 + [pltpu.VMEM((ts, D), F32)],
        compiler_params=_cparams(("arbitrary", "arbitrary")))(dconv, dconv, proj, proj, proj, proj, cw32)


def matmul_tn(xa, ga, x_spec, g_spec, out_shape, out_spec, acc_shape, grid, name, comm=None):
    nk = grid[-1]
    rider = _Rider(comm)

    def body(*refs):
        (x_ref, g_ref, o_ref, acc), ride = rider.split(refs, 2, 1, 1)
        k = pl.program_id(1)
        ride(pl.program_id(0) * nk + k, grid[0] * nk)

        @pl.when(k == 0)
        def _():
            acc[...] = jnp.zeros_like(acc)
        acc[...] += _dot_tn(x_ref[...], g_ref[...])

        @pl.when(k == nk - 1)
        def _():
            o_ref[...] = acc[...]

    outs = pl.pallas_call(
        body, name=name, grid=grid, in_specs=[x_spec, g_spec] + rider.in_specs,
        out_specs=[out_spec] + rider.out_specs,
        out_shape=[SDS(out_shape, F32)] + rider.out_shape,
        scratch_shapes=[pltpu.VMEM(acc_shape, F32)] + rider.scratch,
        compiler_params=_cparams(rider.sem(("parallel", "arbitrary"))))(xa, ga, *rider.ins)
    own, landed = rider.result(outs, 1)
    return own[0] if comm is None else (own[0], landed)


def wgrad_std(xa, ga, name):
    T, M = xa.shape
    N = ga.shape[1]
    tk = min(1024, T)
    return matmul_tn(xa, ga, pl.BlockSpec((tk, M), lambda n, k: (k, 0)), pl.BlockSpec((tk, N), lambda n, k: (k, 0)),
                     (M, N), pl.BlockSpec((M, N), lambda n, k: (0, 0)), (M, N), (1, T // tk), name)


def wgrad_down(p4, df, name):
    nj, T, nb = p4.shape
    D = df.shape[1]
    tk = min(1024, T)
    return matmul_tn(p4, df, pl.BlockSpec((None, tk, nb), lambda j, k: (j, k, 0)),
                     pl.BlockSpec((tk, D), lambda j, k: (k, 0)),
                     (nj * nb, D), pl.BlockSpec((nb, D), lambda j, k: (j, 0)), (nb, D), (nj, T // tk), name)


def wgrad_gu(u, dh8, name, comm=None):
    n8, T, nb = dh8.shape
    D = u.shape[1]
    tk = min(1024, T)
    return matmul_tn(u, dh8, pl.BlockSpec((tk, D), lambda j, k: (k, 0)),
                     pl.BlockSpec((None, tk, nb), lambda j, k: (j, k, 0)),
                     (n8, D, nb), pl.BlockSpec((None, D, nb), lambda j, k: (j, 0, 0)), (D, nb), (n8, T // tk), name,
                     comm=comm)


def wgrad_in(u, dproj, name):
    T, D = u.shape
    bw = dproj.shape[1] // N_DEV
    tk = min(1024, T)
    return matmul_tn(u, dproj, pl.BlockSpec((tk, D), lambda j, k: (k, 0)),
                     pl.BlockSpec((tk, bw), lambda j, k: (k, j)),
                     (N_DEV, D, bw), pl.BlockSpec((None, D, bw), lambda j, k: (j, 0, 0)), (D, bw),
                     (N_DEV, T // tk), name)


def kernel(x, c, w_ada, b_ada, ffn1_w_gu, ffn1_w_down, ln1_g, ln1_b, w_in, w_sb_out, conv_w, conv_b, conv_ln_g, conv_ln_b, w_conv_out, w_out, ln2_g, ln2_b, ffn2_w_gu, ffn2_w_down, ln3_g, ln3_b, loss_target, m_w_ada, m_b_ada, m_ffn1_w_gu, m_ffn1_w_down, m_ln1_g, m_ln1_b, m_w_in, m_w_sb_out, m_conv_w, m_conv_b, m_conv_ln_g, m_conv_ln_b, m_w_conv_out, m_w_out, m_ln2_g, m_ln2_b, m_ffn2_w_gu, m_ffn2_w_down, m_ln3_g, m_ln3_b, v_w_ada, v_b_ada, v_ffn1_w_gu, v_ffn1_w_down, v_ln1_g, v_ln1_b, v_w_in, v_w_sb_out, v_conv_w, v_conv_b, v_conv_ln_g, v_conv_ln_b, v_w_conv_out, v_w_out, v_ln2_g, v_ln2_b, v_ffn2_w_gu, v_ffn2_w_down, v_ln3_g, v_ln3_b):
    Bl, S, D = x.shape
    T = Bl * S
    kw = conv_w.shape[1]
    ax, ay, ac = lax.axis_index("x"), lax.axis_index("y"), lax.axis_index("c")
    me = 4 * ax + 2 * ay + ac
    qc = jnp.stack([2 * ax + ay, ac]).astype(jnp.int32)

    big = dict(ffn1_w_gu=ffn1_w_gu[0], ffn1_w_down=ffn1_w_down[0], w_in=w_in[0], w_sb_out=w_sb_out[0],
               w_conv_out=w_conv_out[0], w_out=w_out[0], ffn2_w_gu=ffn2_w_gu[0], ffn2_w_down=ffn2_w_down[0])
    big_m = dict(ffn1_w_gu=m_ffn1_w_gu[0], ffn1_w_down=m_ffn1_w_down[0], w_in=m_w_in[0], w_sb_out=m_w_sb_out[0],
                 w_conv_out=m_w_conv_out[0], w_out=m_w_out[0], ffn2_w_gu=m_ffn2_w_gu[0], ffn2_w_down=m_ffn2_w_down[0])
    big_v = dict(ffn1_w_gu=v_ffn1_w_gu[0], ffn1_w_down=v_ffn1_w_down[0], w_in=v_w_in[0], w_sb_out=v_w_sb_out[0],
                 w_conv_out=v_w_conv_out[0], w_out=v_w_out[0], ffn2_w_gu=v_ffn2_w_gu[0], ffn2_w_down=v_ffn2_w_down[0])
    names = list(big)

    layer1, layer2, layer3 = ["ffn1_w_gu", "ffn1_w_down"], ["w_in", "w_sb_out", "w_conv_out", "w_out"], \
        ["ffn2_w_gu", "ffn2_w_down"]

    def shards(group):
        return [big[n].astype(BF) for n in group]

    G = dict(zip(layer1, run_comm(gather_plan(shards(layer1)), "allgather_ffn1")))
    wg1 = G["ffn1_w_gu"]
    wd1 = G["ffn1_w_down"].reshape(wg1.shape[0] // 2, wg1.shape[2], D)

    cw8 = small_allgather(conv_w[0], "allgather_conv_w")
    cw_full = jnp.transpose(cw8, (1, 0, 2)).reshape(kw, D)
    cw32 = jnp.concatenate([cw_full, jnp.zeros((HALO - kw, D), F32)], axis=0)

    c_all = small_allgather(c, "allgather_c").reshape(N_DEV * Bl, D)
    ncol = w_ada.shape[2]
    b_cols = lax.dynamic_slice(b_ada, (0, me * ncol), (1, ncol))
    mod_cols, s_all = ada_fwd(c_all, w_ada[0], b_cols, "ada_fwd")
    mod8 = small_allgather(mod_cols, "allgather_mod")
    mod_mine = lax.dynamic_slice(mod8, (0, me * Bl, 0), (N_DEV, Bl, ncol))
    mod = jnp.transpose(mod_mine, (1, 0, 2)).reshape(Bl, N_MOD_ROWS, 1, D)
    sh1, sc1, g1, sh2, sc2, g2, sh3, sc3, g3 = [mod[:, i] for i in range(N_MOD_ROWS)]

    ones = jnp.ones((1, D), F32)
    zeros = jnp.zeros((1, D), F32)
    xf = x.reshape(T, D)
    tgt = loss_target.reshape(T, D)

    (u1, a1, gg1, p1), landed = ffn_up(xf, ones, zeros, sc1, sh1, wg1, "ffn1_up", comm=gather_plan(shards(layer2)))
    G.update(zip(layer2, landed))
    win = jnp.transpose(G["w_in"], (1, 0, 2)).reshape(D, -1)
    wsb = G["w_sb_out"].reshape(D, D)
    wco = G["w_conv_out"].reshape(-1, D)
    wout = G["w_out"].reshape(D, D)
    xh1, rs1, f1 = ffn_down_ln(p1, wd1, xf, ones, zeros, g1, "ffn1_down_ln")
    (u2, proj), landed = mod_matmul(xh1, ln1_g, ln1_b, sc2, sh2, win, "in_proj", comm=gather_plan(shards(layer3)))
    G.update(zip(layer3, landed))
    wg2 = G["ffn2_w_gu"]
    wd2 = G["ffn2_w_down"].reshape(wg2.shape[0] // 2, wg2.shape[2], D)
    yatt, car = att_fwd(proj, Bl, S, D, "att_fwd")
    cs, xhc, rsc = conv_fwd(proj, cw32, conv_b, conv_ln_g, conv_ln_b, Bl, S, D, kw, "conv_fwd")
    xh2, rs2, ysb, yco, merged, o2 = mix_fwd(yatt, cs, proj, wsb, wco, wout, xh1, ln1_g, ln1_b, g2, "mix_fwd")
    (u3, a3, gg3, p3), _ = ffn_up(xh2, ln2_g, ln2_b, sc3, sh3, wg2, "ffn2_up")
    xh3, rs3, f3 = ffn_down_ln(p3, wd2, xh2, ln2_g, ln2_b, g3, "ffn2_down_ln")

    own_sum, recv_b = {}, {}

    def by_owner(group, grads):
        return [g.reshape((4, 2) + big[n].shape) for n, g in zip(group, grads)]

    def chip_sums(group, g42, recv_a):
        sums = [chip_sum(g, r, qc, "chip_sum_" + n) for g, r, n in zip(g42, recv_a, group)]
        own_sum.update({n: s[0] for n, s in zip(group, sums)})
        return chips_plan([s[1] for s in sums])

    dr3, df3, dln3g, dln3b, dg3, lossp = ln_bwd(None, xh3, rs3, ln3_g, ln3_b, g3, f3, MACARON_WEIGHT,
                                                 "ln3_bwd", target=tgt)
    dh3 = swiglu_bwd(df3, wd2, a3, gg3, "ffn2_swiglu_bwd").reshape((-1,) + a3.shape[1:])
    gw_d2 = wgrad_down(p3, df3, "wgrad_ffn2_down")
    gw_g2 = wgrad_gu(u3, dh3, "wgrad_ffn2_gu")
    g42 = by_owner(layer3, [gw_g2, gw_d2])
    (dx2, dsc3, dsh3), recv_a = mod_bwd(dh3, wg2, dr3, xh2, ln2_g, ln2_b, sc3, True, "ffn2_mod_bwd",
                                        comm=sibling_plan(g42))
    plan3 = chip_sums(layer3, g42, recv_a)

    dr2, do2, dln2g, dln2b, dg2 = ln_bwd(dx2, xh2, rs2, ln2_g, ln2_b, g2, o2, 1.0, "ln2_bwd")
    gw_out = wgrad_std(merged, do2, "wgrad_out")
    dysb, dyco, dga, dgb, dyatt, dcs = merge_bwd(do2, proj, ysb, yco, wsb, wco, wout, "merge_bwd")
    gw_sb = wgrad_std(yatt, dysb, "wgrad_sb")
    gw_co = wgrad_std(cs, dyco, "wgrad_conv_out")
    (dq, dk, dv), landed = att_bwd(proj, dyatt, car, Bl, S, D, "att_bwd", comm=plan3)
    recv_b.update(zip(layer3, landed))
    dconv, dcg, dcbeta, dcb = conv_ln_bwd(dcs, xhc, rsc, conv_ln_g, conv_ln_b, "conv_ln_bwd")
    dglu_a, dglu_b, dcw = conv_bwd(dconv, proj, cw32, Bl, S, D, kw, "conv_bwd")
    dproj = jnp.concatenate([dq, dk, dv, dglu_a, dglu_b, dga, dgb], axis=1)
    gw_in = wgrad_in(u2, dproj, "wgrad_in")
    g42 = by_owner(layer2, [gw_in, gw_sb, gw_co, gw_out])
    (dx1, dsc2, dsh2), recv_a = mod_bwd(dproj, win, dr2, xh1, ln1_g, ln1_b, sc2, False, "mix_mod_bwd",
                                        comm=sibling_plan(g42))
    plan2 = chip_sums(layer2, g42, recv_a)

    dr1, df1, dln1g, dln1b, dg1 = ln_bwd(dx1, xh1, rs1, ln1_g, ln1_b, g1, f1, MACARON_WEIGHT, "ln1_bwd")
    dh1 = swiglu_bwd(df1, wd1, a1, gg1, "ffn1_swiglu_bwd").reshape((-1,) + a1.shape[1:])
    gw_d1 = wgrad_down(p1, df1, "wgrad_ffn1_down")
    gw_g1, landed = wgrad_gu(u1, dh1, "wgrad_ffn1_gu", comm=plan2)
    recv_b.update(zip(layer2, landed))
    g42 = by_owner(layer1, [gw_g1, gw_d1])
    plan1 = chip_sums(layer1, g42, run_comm(sibling_plan(g42), "rs_sibling_ffn1"))
    (grad_x, dsc1, dsh1), landed = mod_bwd(dh1, wg1, dr1, xf, ones, zeros, sc1, True, "ffn1_mod_bwd", comm=plan1)
    recv_b.update(zip(layer1, landed))

    dmod = jnp.concatenate([dsh1, dsc1, dg1, dsh2, dsc2, dg2, dsh3, dsc3, dg3], axis=1).reshape(Bl, N_MOD_ROWS * D)
    dmod_all = small_allgather(dmod, "allgather_dmod").reshape(N_DEV * Bl, N_MOD_ROWS * D)
    dmod_cols = lax.dynamic_slice(dmod_all, (0, me * ncol), (N_DEV * Bl, ncol))
    g_w_ada, g_b_ada = ada_bwd(s_all, dmod_cols, dmod_all, "ada_bwd")

    npad = 16
    small_rows = [dln1g, dln1b, dcb, dcg, dcbeta, dln2g, dln2b, dln3g, dln3b,
                  jnp.broadcast_to(lossp[0:1, 0:1], (1, D))]
    pack = jnp.concatenate(small_rows + [jnp.zeros((npad - len(small_rows), D), F32), dcw], axis=0)
    small = small_sum(small_allgather(pack, "allgather_small"), "small_sum")
    loss = small[9, 0]
    small_w = [ln1_g, ln1_b, conv_b, conv_ln_g, conv_ln_b, ln2_g, ln2_b, ln3_g, ln3_b]
    small_m = [m_ln1_g, m_ln1_b, m_conv_b, m_conv_ln_g, m_conv_ln_b, m_ln2_g, m_ln2_b, m_ln3_g, m_ln3_b]
    small_v = [v_ln1_g, v_ln1_b, v_conv_b, v_conv_ln_g, v_conv_ln_b, v_ln2_g, v_ln2_b, v_ln3_g, v_ln3_b]
    padrows = jnp.zeros((npad - len(small_w), D), F32)
    s_g, s_d, s_m, s_v = adamw(jnp.concatenate(small_w + [padrows], axis=0),
                               jnp.concatenate(small_m + [padrows], axis=0),
                               jnp.concatenate(small_v + [padrows], axis=0),
                               [_plain_part(small[:npad])], "adamw_small")
    dcw_mine = lax.dynamic_slice(small[npad:npad + kw], (0, me * (D // N_DEV)), (kw, D // N_DEV))
    cw_g, cw_d, cw_m, cw_v = adamw(conv_w[0], m_conv_w[0], v_conv_w[0], [_plain_part(dcw_mine)], "adamw_conv_w")
    ada_g, ada_d, ada_m, ada_v = adamw(w_ada[0], m_w_ada[0], v_w_ada[0], [_plain_part(g_w_ada)], "adamw_w_ada")
    bada_g, bada_d, bada_m, bada_v = adamw(b_ada, m_b_ada, v_b_ada, [_plain_part(g_b_ada)], "adamw_b_ada")

    res = {}
    for n in names:
        rb = recv_b[n]
        parts = [_plain_part(own_sum[n]), _slot_part(rb, 0), _slot_part(rb, 1), _slot_part(rb, 2)]
        res[n] = adamw(big[n], big_m[n], big_v[n], parts, "adamw_" + n)

    def small_out(k):
        order = dict(ln1_g=0, ln1_b=1, conv_b=2, conv_ln_g=3, conv_ln_b=4, ln2_g=5, ln2_b=6, ln3_g=7, ln3_b=8)
        return lambda arr: arr[order[k]:order[k] + 1]

    weight_order = ["w_ada", "b_ada", "ffn1_w_gu", "ffn1_w_down", "ln1_g", "ln1_b", "w_in", "w_sb_out", "conv_w",
                    "conv_b", "conv_ln_g", "conv_ln_b", "w_conv_out", "w_out", "ln2_g", "ln2_b", "ffn2_w_gu",
                    "ffn2_w_down", "ln3_g", "ln3_b"]

    shapes = dict(w_ada=w_ada.shape, b_ada=b_ada.shape, conv_w=conv_w.shape, ln1_g=ln1_g.shape,
                  **{n: (1,) + big[n].shape for n in names})

    def pick(which):
        outs = []
        for n in weight_order:
            if n == "w_ada":
                a = (ada_g, ada_d, ada_m, ada_v)[which]
            elif n == "b_ada":
                a = (bada_g, bada_d, bada_m, bada_v)[which]
            elif n == "conv_w":
                a = (cw_g, cw_d, cw_m, cw_v)[which]
            elif n in res:
                a = res[n][which]
            else:
                a = small_out(n)((s_g, s_d, s_m, s_v)[which])
            outs.append(a.reshape(shapes.get(n, ln1_g.shape)))
        return outs

    return (loss, grad_x.reshape(Bl, S, D), *pick(0), *pick(1), *pick(2), *pick(3))
```

```python
import functools
import math

import jax
import jax.numpy as jnp
from jax import lax
from jax.experimental import pallas as pl
from jax.experimental.pallas import tpu as pltpu

F32 = jnp.float32
BF = jnp.bfloat16
SDS = jax.ShapeDtypeStruct
MESH = pl.DeviceIdType.MESH

N_DEV = 8
SB_HEAD_DIM = 64
N_MOD_ROWS = 9
MACARON_WEIGHT = 0.5
DEEPNORM_ALPHA = 2.0 ** 0.25
LN_EPS = 1e-5
ADAM_LR = 0.001
ADAM_B1 = 0.9
ADAM_B2 = 0.999
ADAM_EPS = 1e-08
ADAM_WD = 0.01
ADAM_STEP = 10

V7X_VMEM_LIMIT = 52 * 1024 * 1024
LANE = 128
SUBLANES = 8
HALO = 32


def _cparams(sem=None):
    return pltpu.CompilerParams(dimension_semantics=sem, vmem_limit_bytes=V7X_VMEM_LIMIT)


def _dot_nn(a, b):
    return lax.dot_general(a, b, (((1,), (0,)), ((), ())), preferred_element_type=F32)


def _dot_nt(a, b):
    return lax.dot_general(a, b, (((1,), (1,)), ((), ())), preferred_element_type=F32)


def _dot_tn(a, b):
    return lax.dot_general(a, b, (((0,), (0,)), ((), ())), preferred_element_type=F32)


def _sig(x):
    return 1.0 / (1.0 + jnp.exp(-x))


def _ln_stats(r):
    mu = jnp.mean(r, axis=-1, keepdims=True)
    d = r - mu
    var = jnp.mean(d * d, axis=-1, keepdims=True)
    rstd = lax.rsqrt(var + LN_EPS)
    return d * rstd, rstd


def _ln_bwd(dxh, xh, rstd):
    m1 = jnp.mean(dxh, axis=-1, keepdims=True)
    m2 = jnp.mean(dxh * xh, axis=-1, keepdims=True)
    return rstd * (dxh - m1 - xh * m2)


def _rowsum(v):
    return jnp.sum(v, axis=0, keepdims=True)


def _row_tile(n, cap):
    if n <= cap:
        return n
    best = None
    for t in range(8, cap + 1, 8):
        if n % t == 0:
            best = t
    assert best is not None, (n, cap)
    return best


def _coords():
    x, y, c = lax.axis_index("x"), lax.axis_index("y"), lax.axis_index("c")
    return x, y, c


def _flip(v, bit):
    return 1 - v if bit else v


def small_allgather(blk, name):
    r, n = blk.shape

    def body(x_ref, out_ref, send_sems, recv_sems):
        x, y, c = _coords()
        me = 4 * x + 2 * y + c
        out_ref[me] = x_ref[...]
        copies = []
        for k in range(1, N_DEV):
            peer = (_flip(x, k & 4), _flip(y, k & 2), _flip(c, k & 1))
            cp = pltpu.make_async_remote_copy(
                src_ref=x_ref, dst_ref=out_ref.at[me], send_sem=send_sems.at[k - 1],
                recv_sem=recv_sems.at[k - 1], device_id=peer, device_id_type=MESH)
            cp.start()
            copies.append(cp)
        for k in range(1, N_DEV):
            px, py, pc = _flip(x, k & 4), _flip(y, k & 2), _flip(c, k & 1)
            slot = 4 * px + 2 * py + pc
            pltpu.make_async_remote_copy(
                src_ref=x_ref, dst_ref=out_ref.at[slot], send_sem=send_sems.at[k - 1],
                recv_sem=recv_sems.at[k - 1], device_id=(px, py, pc), device_id_type=MESH).wait_recv()
        for cp in copies:
            cp.wait_send()

    return pl.pallas_call(
        body, name=name,
        out_shape=SDS((N_DEV, r, n), blk.dtype),
        in_specs=[pl.BlockSpec(memory_space=pltpu.VMEM)],
        out_specs=pl.BlockSpec(memory_space=pltpu.VMEM),
        scratch_shapes=[pltpu.SemaphoreType.DMA((N_DEV - 1,)), pltpu.SemaphoreType.DMA((N_DEV - 1,))],
    )(blk)


class CommPlan:
    def __init__(self, ins, out_shape, scratch, emit):
        self.ins, self.out_shape, self.scratch, self.emit = list(ins), list(out_shape), list(scratch), emit


def _phase(step, at, fn):
    if step is None:
        fn()
    else:
        pl.when(step == at)(fn)


def gather_plan(shards):
    n = len(shards)
    per = 7

    def emit(ins, outs, sems, step, nsteps):
        send_sems, recv_sems, local_sems = sems
        x, y, c = _coords()
        me = 4 * x + 2 * y + c
        sibling = (x, y, 1 - c)
        chips = [(1 - x, y), (x, 1 - y), (1 - x, 1 - y)]

        def slot(px, py, pc):
            return 4 * px + 2 * py + pc

        def copy(t, k, block, to, src=None):
            dst = outs[t].at[slot(*block)]
            return pltpu.make_async_remote_copy(
                src_ref=dst if src is None else src, dst_ref=dst,
                send_sem=send_sems.at[per * t + k], recv_sem=recv_sems.at[per * t + k],
                device_id=to, device_id_type=MESH)

        def local(t):
            return pltpu.make_async_copy(ins[t], outs[t].at[me], local_sems.at[t])

        def first(t):
            return [copy(t, 0, (x, y, c), sibling, src=ins[t])] + [
                copy(t, 1 + j, (x, y, c), (*chip, c), src=ins[t]) for j, chip in enumerate(chips)]

        def passed(t):
            return [copy(t, 4 + j, (*chip, c), sibling) for j, chip in enumerate(chips)]

        def start():
            for t in range(n):
                local(t).start()
                for cp in first(t):
                    cp.start()

        def forward():
            for t in range(n):
                for j, chip in enumerate(chips):
                    copy(t, 1 + j, (*chip, c), (x, y, c)).wait_recv()
                    passed(t)[j].start()

        def finish():
            for t in range(n):
                copy(t, 0, (x, y, 1 - c), (x, y, c)).wait_recv()
                for j, chip in enumerate(chips):
                    copy(t, 4 + j, (*chip, 1 - c), (x, y, c)).wait_recv()
            for t in range(n):
                for cp in first(t) + passed(t):
                    cp.wait_send()
                local(t).wait()

        _phase(step, 0, start)
        _phase(step, None if step is None else max(nsteps - 2, 0), forward)
        _phase(step, None if step is None else nsteps - 1, finish)

    return CommPlan(shards, [SDS((N_DEV,) + s.shape, s.dtype) for s in shards],
                    [pltpu.SemaphoreType.DMA((per * n,)), pltpu.SemaphoreType.DMA((per * n,)),
                     pltpu.SemaphoreType.DMA((n,))], emit)


def chips_plan(sums):
    n = len(sums)

    def emit(ins, outs, sems, step, nsteps):
        send_sems, recv_sems = sems
        x, y, c = _coords()

        def copies():
            return [pltpu.make_async_remote_copy(
                src_ref=ins[t].at[j - 1], dst_ref=outs[t].at[j - 1], send_sem=send_sems.at[3 * t + j - 1],
                recv_sem=recv_sems.at[3 * t + j - 1], device_id=(_flip(x, j & 2), _flip(y, j & 1), c),
                device_id_type=MESH) for t in range(n) for j in range(1, 4)]

        def start():
            for cp in copies():
                cp.start()

        def finish():
            for cp in copies():
                cp.wait_recv()
            for cp in copies():
                cp.wait_send()

        _phase(step, 0, start)
        _phase(step, None if step is None else nsteps - 1, finish)

    return CommPlan(sums, [SDS(s.shape, s.dtype) for s in sums],
                    [pltpu.SemaphoreType.DMA((3 * n,)), pltpu.SemaphoreType.DMA((3 * n,))], emit)


def run_comm(plan, name):
    n, m = len(plan.ins), len(plan.out_shape)

    def body(*refs):
        plan.emit(refs[:n], refs[n:n + m], refs[n + m:], None, 1)

    anyspec = pl.BlockSpec(memory_space=pl.ANY)
    return pl.pallas_call(body, name=name, out_shape=plan.out_shape, in_specs=[anyspec] * n,
                          out_specs=[anyspec] * m, scratch_shapes=plan.scratch)(*plan.ins)


class _Rider:
    def __init__(self, plan):
        self.plan = plan
        anyspec = pl.BlockSpec(memory_space=pl.ANY)
        self.ins = plan.ins if plan else []
        self.in_specs = [anyspec] * len(self.ins)
        self.out_specs = [anyspec] * (len(plan.out_shape) if plan else 0)
        self.out_shape = plan.out_shape if plan else []
        self.scratch = plan.scratch if plan else []

    def split(self, refs, n_in, n_out, n_scratch=0):
        ni, no = len(self.ins), len(self.out_shape)
        own_in = refs[:n_in]
        c_in = refs[n_in:n_in + ni]
        own_out = refs[n_in + ni:n_in + ni + n_out]
        c_out = refs[n_in + ni + n_out:n_in + ni + n_out + no]
        rest = refs[n_in + ni + n_out + no:]
        own_scr, c_scr = rest[:n_scratch], rest[n_scratch:]

        def ride(step, nsteps):
            if self.plan:
                self.plan.emit(c_in, c_out, c_scr, step, nsteps)

        return tuple(own_in) + tuple(own_out) + tuple(own_scr), ride

    def result(self, outs, n_out):
        outs = list(outs) if isinstance(outs, (list, tuple)) else [outs]
        return outs[:n_out], (outs[n_out:] if self.plan else None)

    def sem(self, sem):
        return tuple("arbitrary" for _ in sem) if self.plan else sem


def sibling_plan(grads):
    n = len(grads)

    def emit(ins, outs, sems, step, nsteps):
        send_sems, recv_sems = sems
        x, y, c = _coords()

        def copies():
            return [pltpu.make_async_remote_copy(
                src_ref=ins[t].at[:, 1 - c], dst_ref=outs[t], send_sem=send_sems.at[t],
                recv_sem=recv_sems.at[t], device_id=(x, y, 1 - c), device_id_type=MESH) for t in range(n)]

        def start():
            for cp in copies():
                cp.start()

        def finish():
            for cp in copies():
                cp.wait_recv()
            for cp in copies():
                cp.wait_send()

        _phase(step, 0, start)
        _phase(step, None if step is None else nsteps - 1, finish)

    return CommPlan(grads, [SDS((4,) + g.shape[2:], g.dtype) for g in grads],
                    [pltpu.SemaphoreType.DMA((n,)), pltpu.SemaphoreType.DMA((n,))], emit)


def chip_sum(g42, recv, qc, name):
    _, _, R, C = g42.shape
    tr = _row_tile(R, 256)

    def body(qc_ref, a_ref, b_ref, own_ref, send_ref):
        j = pl.program_id(1)
        s = a_ref[...] + b_ref[...]

        @pl.when(j == 0)
        def _():
            own_ref[...] = s

        @pl.when(j > 0)
        def _():
            send_ref[...] = s.astype(BF)

    gs = pltpu.PrefetchScalarGridSpec(
        num_scalar_prefetch=1, grid=(R // tr, 4),
        in_specs=[pl.BlockSpec((None, None, tr, C), lambda i, j, s: (jnp.bitwise_xor(s[0], j), s[1], i, 0)),
                  pl.BlockSpec((None, tr, C), lambda i, j, s: (jnp.bitwise_xor(s[0], j), i, 0))],
        out_specs=[pl.BlockSpec((tr, C), lambda i, j, s: (i, 0)),
                   pl.BlockSpec((None, tr, C), lambda i, j, s: (jnp.maximum(j - 1, 0), i, 0))])
    return pl.pallas_call(body, name=name, grid_spec=gs, out_shape=[SDS((R, C), F32), SDS((3, R, C), BF)],
                          compiler_params=_cparams(("arbitrary", "arbitrary")))(qc, g42, recv)


def small_sum(g8, name):
    def body(g_ref, o_ref):
        acc = g_ref[0]
        for k in range(1, N_DEV):
            acc = acc + g_ref[k]
        o_ref[...] = acc
    return pl.pallas_call(body, name=name, out_shape=SDS(g8.shape[1:], F32))(g8)


def adamw(w, m, v, parts, name):
    R, C = w.shape
    tr = _row_tile(R, 256)
    npart = len(parts)
    c1 = 1.0 / (1.0 - ADAM_B1 ** ADAM_STEP)
    c2 = 1.0 / (1.0 - ADAM_B2 ** ADAM_STEP)

    def body(*refs):
        w_ref, m_ref, v_ref = refs[:3]
        p_refs = refs[3:3 + npart]
        g_ref, d_ref, nm_ref, nv_ref = refs[3 + npart:]
        g = p_refs[0][...].astype(F32)
        for p in p_refs[1:]:
            g = g + p[...].astype(F32)
        nm = ADAM_B1 * m_ref[...] + (1.0 - ADAM_B1) * g
        nv = ADAM_B2 * v_ref[...] + (1.0 - ADAM_B2) * (g * g)
        mh = nm * c1
        vh = nv * c2
        g_ref[...] = g
        nm_ref[...] = nm
        nv_ref[...] = nv
        d_ref[...] = -ADAM_LR * (mh / (jnp.sqrt(vh) + ADAM_EPS) + ADAM_WD * w_ref[...])

    wspec = pl.BlockSpec((tr, C), lambda i: (i, 0))
    pspecs = [pl.BlockSpec(bs(tr, C), im) for (_, bs, im) in parts]
    outs = pl.pallas_call(
        body, name=name, grid=(R // tr,),
        in_specs=[wspec] * 3 + pspecs, out_specs=[wspec] * 4,
        out_shape=[SDS((R, C), F32)] * 4,
        compiler_params=_cparams(("parallel",)))(w, m, v, *[p[0] for p in parts])
    return outs


def _plain_part(g):
    return (g, lambda tr, C: (tr, C), lambda i: (i, 0))


def _slot_part(g, slot):
    return (g, lambda tr, C: (None, tr, C), lambda i, s=slot: (s, i, 0))


def ada_fwd(c_all, w_cols, b_cols, name):
    Bg, D = c_all.shape
    n = w_cols.shape[1]

    def body(c_ref, w_ref, b_ref, o_ref, s_ref):
        cc = c_ref[...]
        s = cc * _sig(cc)
        s_ref[...] = s
        o_ref[...] = jnp.dot(s, w_ref[...], preferred_element_type=F32, precision=lax.Precision.HIGHEST) + b_ref[...]

    return pl.pallas_call(body, name=name, out_shape=[SDS((Bg, n), F32), SDS((Bg, D), F32)],
                          compiler_params=_cparams())(c_all, w_cols, b_cols)


def ada_bwd(s_all, dmod_cols, dmod_all, name):
    Bg, D = s_all.shape
    n = dmod_cols.shape[1]

    def body(s_ref, dc_ref, da_ref, gw_ref, gb_ref):
        gw_ref[...] = lax.dot_general(s_ref[...], dc_ref[...], (((0,), (0,)), ((), ())),
                                      preferred_element_type=F32, precision=lax.Precision.HIGHEST)
        acc = da_ref[0:1, :]
        for r in range(1, Bg):
            acc = acc + da_ref[r:r + 1, :]
        gb_ref[...] = acc

    return pl.pallas_call(body, name=name, out_shape=[SDS((D, n), F32), SDS((1, dmod_all.shape[1]), F32)],
                          compiler_params=_cparams())(s_all, dmod_cols, dmod_all)


def _vec(D, rank):
    return pl.BlockSpec((1, D), (lambda i: (0, 0)) if rank == 1 else (lambda i, j: (0, 0)))


def _modspec(D, tpb, rank):
    if rank == 1:
        return pl.BlockSpec((None, 1, D), lambda i: (i // tpb, 0, 0))
    return pl.BlockSpec((None, 1, D), lambda i, j: (i // tpb, 0, 0))


def _resident(shape):
    return pl.BlockSpec(shape, lambda *_: (0,) * len(shape), pipeline_mode=pl.Buffered(1))


def ffn_up(xs, pg, pb, sc, sh, wg8, name, comm=None):
    T, D = xs.shape
    n2, _, nb = wg8.shape
    nj = n2 // 2
    S = T // sc.shape[0]
    tm = min(512, S)
    tpb = S // tm
    rider = _Rider(comm)

    def body(*refs):
        (x_ref, pg_ref, pb_ref, sc_ref, sh_ref, w_ref, u_ref, a_ref, g_ref, p_ref), ride = rider.split(refs, 6, 4)
        ride(pl.program_id(0), T // tm)
        xin = x_ref[...] * pg_ref[...] + pb_ref[...]
        u_ref[...] = (xin * (1.0 + sc_ref[...]) + sh_ref[...]).astype(BF)

        def col_block(j, _):
            u = u_ref[...]
            a = _dot_nn(u, w_ref[j])
            g = _dot_nn(u, w_ref[j + nj])
            s = _sig(a)
            silu = a * s
            a_ref[j] = (g * (s * (1.0 + a * (1.0 - s)))).astype(BF)
            g_ref[j] = silu.astype(BF)
            p_ref[j] = (silu * g).astype(BF)
            return 0

        lax.fori_loop(0, nj, col_block, 0)

    blk = pl.BlockSpec((nj, tm, nb), lambda i: (0, i, 0))
    row = pl.BlockSpec((tm, D), lambda i: (i, 0))
    outs = pl.pallas_call(
        body, name=name, grid=(T // tm,),
        in_specs=[row, _vec(D, 1), _vec(D, 1), _modspec(D, tpb, 1), _modspec(D, tpb, 1), _resident(wg8.shape)]
        + rider.in_specs,
        out_specs=[row, blk, blk, blk] + rider.out_specs,
        out_shape=[SDS((T, D), BF)] + [SDS((nj, T, nb), BF)] * 3 + rider.out_shape,
        scratch_shapes=rider.scratch,
        compiler_params=_cparams(rider.sem(("parallel",))))(xs, pg, pb, sc, sh, wg8, *rider.ins)
    return rider.result(outs, 4)


def ffn_down_ln(p4, wd3, xs, pg, pb, gate, name):
    nj, T, nb = p4.shape
    D = wd3.shape[2]
    S = T // gate.shape[0]
    tm = min(512, S)
    tpb = S // tm

    def body(p_ref, wd_ref, x_ref, pg_ref, pb_ref, gate_ref, xh_ref, rs_ref, f_ref):
        f = _dot_nn(p_ref[0], wd_ref[0])
        for k in range(1, nj):
            f = f + _dot_nn(p_ref[k], wd_ref[k])
        xin = x_ref[...] * pg_ref[...] + pb_ref[...]
        r = DEEPNORM_ALPHA * xin + gate_ref[...] * (MACARON_WEIGHT * f)
        xh, rstd = _ln_stats(r)
        xh_ref[...] = xh
        rs_ref[...] = rstd
        f_ref[...] = f.astype(BF)

    row = pl.BlockSpec((tm, D), lambda i: (i, 0))
    return pl.pallas_call(
        body, name=name, grid=(T // tm,),
        in_specs=[pl.BlockSpec((nj, tm, nb), lambda i: (0, i, 0)), _resident(wd3.shape),
                  row, _vec(D, 1), _vec(D, 1), _modspec(D, tpb, 1)],
        out_specs=[row, pl.BlockSpec((tm, 1), lambda i: (i, 0)), row],
        out_shape=[SDS((T, D), F32), SDS((T, 1), F32), SDS((T, D), BF)],
        compiler_params=_cparams(("parallel",)))(p4, wd3, xs, pg, pb, gate)


N_QKV = 3


def mod_matmul(xs, pg, pb, sc, sh, w, name, comm=None):
    T, D = xs.shape
    N = w.shape[1]
    S = T // sc.shape[0]
    tm = min(256, S)
    tpb = S // tm
    rider = _Rider(comm)

    def body(*refs):
        (x_ref, pg_ref, pb_ref, sc_ref, sh_ref, w_ref, u_ref, qkv_ref, o_ref), ride = rider.split(refs, 6, 3)
        ride(pl.program_id(0), T // tm)
        xin = x_ref[...] * pg_ref[...] + pb_ref[...]
        u = (xin * (1.0 + sc_ref[...]) + sh_ref[...]).astype(BF)
        u_ref[...] = u
        for n in range(N // D):
            y = _dot_nn(u, w_ref[:, n * D:(n + 1) * D])
            if n < N_QKV:
                qkv_ref[:, n * D:(n + 1) * D] = y.astype(BF)
            else:
                o_ref[:, (n - N_QKV) * D:(n - N_QKV + 1) * D] = y

    row = pl.BlockSpec((tm, D), lambda i: (i, 0))
    outs = pl.pallas_call(
        body, name=name, grid=(T // tm,),
        in_specs=[row, _vec(D, 1), _vec(D, 1), _modspec(D, tpb, 1), _modspec(D, tpb, 1), _resident(w.shape)]
        + rider.in_specs,
        out_specs=[row, pl.BlockSpec((tm, N_QKV * D), lambda i: (i, 0)),
                   pl.BlockSpec((tm, N - N_QKV * D), lambda i: (i, 0))] + rider.out_specs,
        out_shape=[SDS((T, D), BF), SDS((T, N_QKV * D), BF), SDS((T, N - N_QKV * D), F32)] + rider.out_shape,
        scratch_shapes=rider.scratch,
        compiler_params=_cparams(rider.sem(("parallel",))))(xs, pg, pb, sc, sh, w, *rider.ins)
    return rider.result(outs, 3)


ATT_TQ = 1024
ATT_TK = 256


def _att_consts(tk):
    r = lax.broadcasted_iota(jnp.int32, (tk + 8, tk), 0)
    c = lax.broadcasted_iota(jnp.int32, (tk + 8, tk), 1)
    usum = jnp.where((r >= tk) | (c > r), 1.0, 0.0).astype(BF)
    lsum = jnp.where((r >= tk) | (c < r), 1.0, 0.0).astype(BF)
    dmask = lax.broadcasted_iota(jnp.int32, (tk, tk), 0) < lax.broadcasted_iota(jnp.int32, (tk, tk), 1)
    return usum, lsum, dmask


def _split_dot(m, v):
    hi = v.astype(BF)
    lo = (v - hi.astype(F32)).astype(BF)
    return _dot_nn(m, hi) + _dot_nn(m, lo)


def _softplus(z):
    return jnp.maximum(z, 0.0) + jnp.log(1.0 + jnp.exp(-jnp.abs(z)))


def _att_dims(S, D):
    dh = SB_HEAD_DIM
    cw = min(LANE, D)
    tq = min(ATT_TQ, S)
    tk = min(ATT_TK, tq)
    assert tq % tk == 0 and S % tq == 0
    return dh, cw, cw // dh, D // cw, tq, tk, S // tq, S // tk


def att_fwd(proj, Bl, S, D, name):
    dh, cw, hp, nblk, tq, tk, nq, nk = _att_dims(S, D)
    scale = 1.0 / math.sqrt(dh)
    assert math.log2(scale) == int(math.log2(scale))
    H = D // dh

    def body(q_ref, k_ref, v_ref, o_ref, car_ref, qs, ks, vts):
        usum, _, dmask = _att_consts(tk)
        for hh in range(hp):
            sl = slice(hh * dh, (hh + 1) * dh)
            qs[hh] = (q_ref[:, sl] * scale).astype(BF)
            ks[hh] = k_ref[:, sl].astype(BF)
            for kb in range(nk):
                vts[hh, kb] = v_ref[kb * tk:(kb + 1) * tk, sl].astype(F32).T.astype(BF)
        nch = tq // tk

        def qloop(qb, _):
            qo = pl.multiple_of(qb * tq, tq)
            n_full = qb * nch

            def blk(kb, state, diag):
                ko = pl.multiple_of(kb * tk, tk)
                chains = [(hh, c) for hh in range(hp) for c in range(0 if diag is None else diag, nch)]

                def masked(ch, val):
                    return jnp.where(dmask, val, 0.0) if ch[1] == diag else val

                z = {ch: _dot_nt(ks[ch[0], pl.ds(ko, tk), :], qs[ch[0], pl.ds(pl.multiple_of(qo + ch[1] * tk, tk), tk), :])
                     for ch in chains}
                sp = {ch: _softplus(z[ch]) for ch in chains}
                lk = {ch: masked(ch, -sp[ch]) for ch in chains}
                for hh, c in chains:
                    car_ref[hh, qb * nk + kb, :, c * tk:(c + 1) * tk] = state[hh][c][0]
                cs = {ch: _split_dot(usum, lk[ch]) for ch in chains}
                w = {ch: masked(ch, jnp.exp((z[ch] - sp[ch]) + state[ch[0]][ch[1]][0][0:1, :] + cs[ch][:tk]))
                     for ch in chains}
                pv = {ch: _dot_nn(vts[ch[0], kb], w[ch].astype(BF)) for ch in chains}
                return tuple(tuple(
                    (state[hh][c][0] + cs[(hh, c)][tk:], state[hh][c][1] + pv[(hh, c)]) if (hh, c) in z else state[hh][c]
                    for c in range(nch)) for hh in range(hp))

            state = tuple(tuple((jnp.zeros((8, tk), F32), jnp.zeros((dh, tk), F32)) for _ in range(nch))
                          for _ in range(hp))
            for i in reversed(range(nch)):
                state = blk(n_full + i, state, i)
            state = lax.fori_loop(0, n_full, lambda j, st: blk(n_full - 1 - j, st, None), state)
            for hh in range(hp):
                for c in range(nch):
                    o_ref[pl.ds(pl.multiple_of(qo + c * tk, tk), tk), hh * dh:(hh + 1) * dh] = (
                        state[hh][c][1].T.astype(BF))
            return 0

        lax.fori_loop(0, nq, qloop, 0)

    def seg(s):
        return pl.BlockSpec((S, cw), lambda b, h: (b, s * nblk + h))

    return pl.pallas_call(
        body, name=name, grid=(Bl, nblk),
        in_specs=[seg(0), seg(1), seg(2)],
        out_specs=[pl.BlockSpec((S, cw), lambda b, h: (b, h)),
                   pl.BlockSpec((None, hp, nq * nk, 8, tq), lambda b, h: (b, h, 0, 0, 0))],
        out_shape=[SDS((Bl * S, D), BF), SDS((Bl, H, nq * nk, 8, tq), F32)],
        scratch_shapes=[pltpu.VMEM((hp, S, dh), BF)] * 2 + [pltpu.VMEM((hp, nk, dh, tk), BF)],
        compiler_params=_cparams(("parallel", "parallel")))(proj, proj, proj)


def conv_fwd(proj, cw32, cb, cg, cbeta, Bl, S, D, kw, name):
    T = Bl * S
    ts = min(128, S)
    ns = S // ts
    off = HALO - (kw - 1)
    rc = min(64, ts)
    cw = min(LANE, D)

    def body(a_ref, b_ref, ha_ref, hb_ref, w_ref, cb_ref, g_ref, be_ref, cs_ref, xh_ref, rs_ref, hsh, conv_s):
        i = pl.program_id(1)
        h = a_ref[...] * _sig(b_ref[...])
        hh = jnp.where(i == 0, 0.0, ha_ref[...] * _sig(hb_ref[...]))
        for cb_ in range(D // cw):
            cols = slice(cb_ * cw, (cb_ + 1) * cw)
            hsh[0, cb_, pl.ds(HALO, ts), :] = h[:, cols]
            hsh[0, cb_, pl.ds(0, HALO), :] = hh[:, cols]
            for s in range(1, SUBLANES):
                hsh[s, cb_, pl.ds(0, ts + HALO - SUBLANES), :] = hsh[0, cb_, pl.ds(s, ts + HALO - SUBLANES), :]
            accs = [jnp.zeros((rc, cw), F32) for _ in range(ts // rc)]
            for k in range(kw):
                wk = w_ref[k:k + 1, cols]
                s = (off + k) % SUBLANES
                for r in range(ts // rc):
                    accs[r] = accs[r] + wk * hsh[s, cb_, pl.ds(r * rc + off + k - s, rc), :]
            for r in range(ts // rc):
                conv_s[pl.ds(r * rc, rc), cols] = accs[r]
        conv = conv_s[...] + cb_ref[...]
        xh, rstd = _ln_stats(conv)
        xh_ref[...] = xh
        rs_ref[...] = rstd
        cl = xh * g_ref[...] + be_ref[...]
        cs_ref[...] = (cl * _sig(cl)).astype(BF)

    hpb = ts // HALO

    def tile(seg):
        return pl.BlockSpec((ts, D), lambda b, i: (b * ns + i, seg))

    def halo(seg):
        return pl.BlockSpec((HALO, D), lambda b, i: (jnp.maximum((b * ns + i) * hpb - 1, 0), seg))

    row = pl.BlockSpec((ts, D), lambda b, i: (b * ns + i, 0))
    vec = pl.BlockSpec((1, D), lambda b, i: (0, 0))
    return pl.pallas_call(
        body, name=name, grid=(Bl, ns),
        in_specs=[tile(0), tile(1), halo(0), halo(1), pl.BlockSpec((HALO, D), lambda b, i: (0, 0)), vec, vec, vec],
        out_specs=[row, row, pl.BlockSpec((ts, 1), lambda b, i: (b * ns + i, 0))],
        out_shape=[SDS((T, D), BF), SDS((T, D), F32), SDS((T, 1), F32)],
        scratch_shapes=[pltpu.VMEM((SUBLANES, D // cw, ts + HALO, cw), F32), pltpu.VMEM((ts, D), F32)],
        compiler_params=_cparams(("parallel", "arbitrary")))(proj, proj, proj, proj, cw32, cb, cg, cbeta)


def mix_fwd(yatt, cs, proj, wsb, wco, wout, xs, pg, pb, gate, name):
    T, D = yatt.shape
    S = T // gate.shape[0]
    tm = min(256, S)
    tpb = S // tm

    def body(ya_ref, cs_ref, ga_ref, gb_ref, wsb_ref, wco_ref, wout_ref, x_ref, pg_ref, pb_ref, gate_ref,
             xh_ref, rs_ref, ysb_ref, yco_ref, mg_ref, o_ref):
        ysb = _dot_nn(ya_ref[...], wsb_ref[...])
        yco = _dot_nn(cs_ref[...], wco_ref[...])
        merged = _sig(ga_ref[...]) * ysb + _sig(gb_ref[...]) * yco
        mg = merged.astype(BF)
        o = _dot_nn(mg, wout_ref[...])
        xin = x_ref[...] * pg_ref[...] + pb_ref[...]
        r = DEEPNORM_ALPHA * xin + gate_ref[...] * o
        xh, rstd = _ln_stats(r)
        xh_ref[...] = xh
        rs_ref[...] = rstd
        ysb_ref[...] = ysb.astype(BF)
        yco_ref[...] = yco.astype(BF)
        mg_ref[...] = mg
        o_ref[...] = o.astype(BF)

    row = pl.BlockSpec((tm, D), lambda i: (i, 0))
    wfull = pl.BlockSpec((D, D), lambda i: (0, 0))
    return pl.pallas_call(
        body, name=name, grid=(T // tm,),
        in_specs=[row, row, pl.BlockSpec((tm, D), lambda i: (i, 2)), pl.BlockSpec((tm, D), lambda i: (i, 3)),
                  wfull, wfull, wfull, row, _vec(D, 1), _vec(D, 1), _modspec(D, tpb, 1)],
        out_specs=[row, pl.BlockSpec((tm, 1), lambda i: (i, 0)), row, row, row, row],
        out_shape=[SDS((T, D), F32), SDS((T, 1), F32)] + [SDS((T, D), BF)] * 4,
        compiler_params=_cparams(("parallel",)))(yatt, cs, proj, proj, wsb, wco, wout, xs, pg, pb, gate)


def ln_bwd(dout, xh, rstd, lng, lnb, gate, sub, res_w, name, target=None):
    T, D = xh.shape
    Bl = gate.shape[0]
    S = T // Bl
    tm = min(256, S)
    tpb = S // tm
    first = target is not None

    def body(*refs):
        if first:
            tg_ref, xh_ref, rs_ref, g_ref, b_ref, gate_ref, sub_ref = refs[:7]
            dr_ref, ds_ref, dg_ref, db_ref, dgate_ref, loss_ref = refs[7:]
        else:
            do_ref, xh_ref, rs_ref, g_ref, b_ref, gate_ref, sub_ref = refs[:7]
            dr_ref, ds_ref, dg_ref, db_ref, dgate_ref = refs[7:]
        i = pl.program_id(0)
        xh_ = xh_ref[...]
        if first:
            diff = (xh_ * g_ref[...] + b_ref[...]) - tg_ref[...]
            lsum = jnp.sum(jnp.sum(diff * diff, axis=1, keepdims=True), axis=0, keepdims=True) * (0.5 / D)
            do = diff * (1.0 / D)
        else:
            do = do_ref[...]

        @pl.when(i == 0)
        def _():
            dg_ref[...] = jnp.zeros_like(dg_ref)
            db_ref[...] = jnp.zeros_like(db_ref)
            if first:
                loss_ref[...] = jnp.zeros_like(loss_ref)

        @pl.when(i % tpb == 0)
        def _():
            dgate_ref[...] = jnp.zeros_like(dgate_ref)

        if first:
            loss_ref[...] += jnp.broadcast_to(lsum, loss_ref.shape)
        dg_ref[...] += _rowsum(do * xh_)
        db_ref[...] += _rowsum(do)
        dr = _ln_bwd(do * g_ref[...], xh_, rs_ref[...])
        dr_ref[...] = dr
        ds_ref[...] = (dr * gate_ref[...] * res_w).astype(BF)
        dgate_ref[...] += _rowsum(dr * (res_w * sub_ref[...].astype(F32)))

    row = pl.BlockSpec((tm, D), lambda i: (i, 0))
    vec = _vec(D, 1)
    mod = _modspec(D, tpb, 1)
    out_specs = [row, row, vec, vec, mod]
    out_shape = [SDS((T, D), F32), SDS((T, D), BF), SDS((1, D), F32), SDS((1, D), F32), SDS((Bl, 1, D), F32)]
    if first:
        out_specs.append(pl.BlockSpec((8, LANE), lambda i: (0, 0)))
        out_shape.append(SDS((8, LANE), F32))
    return pl.pallas_call(
        body, name=name, grid=(T // tm,),
        in_specs=[row, row, pl.BlockSpec((tm, 1), lambda i: (i, 0)), vec, vec, mod, row],
        out_specs=out_specs, out_shape=out_shape,
        compiler_params=_cparams(("arbitrary",)))(target if first else dout, xh, rstd, lng, lnb, gate, sub)


def swiglu_bwd(df, wd3, a4, g4, name):
    nj, T, nb = a4.shape
    D = df.shape[1]
    tm = min(512, T)

    def body(df_ref, wd_ref, a_ref, g_ref, o_ref):
        def col_block(j, _):
            dp = _dot_nt(df_ref[...], wd_ref[j])
            o_ref[0, j] = (dp * a_ref[j].astype(F32)).astype(BF)
            o_ref[1, j] = (dp * g_ref[j].astype(F32)).astype(BF)
            return 0

        lax.fori_loop(0, nj, col_block, 0)

    blk = pl.BlockSpec((nj, tm, nb), lambda i: (0, i, 0))
    return pl.pallas_call(
        body, name=name, grid=(T // tm,),
        in_specs=[pl.BlockSpec((tm, D), lambda i: (i, 0)), _resident(wd3.shape), blk, blk],
        out_specs=pl.BlockSpec((2, nj, tm, nb), lambda i: (0, 0, i, 0)),
        out_shape=SDS((2, nj, T, nb), BF),
        compiler_params=_cparams(("parallel",)))(df, wd3, a4, g4)


def mod_bwd(dh, w, dr, xs, pg, pb, sc, blocked, name, comm=None):
    T, D = dr.shape
    Bl = sc.shape[0]
    S = T // Bl
    tm = min(512 if blocked else 256, S)
    tpb = S // tm
    row = pl.BlockSpec((tm, D), lambda i: (i, 0))
    if blocked:
        nk, _, kb = dh.shape
        dh_list, dh_specs = [dh], [pl.BlockSpec((nk, tm, kb), lambda i: (0, i, 0))]
    else:
        nk = len(dh)
        dh_list, dh_specs = list(dh), [row] * nk
    nd = len(dh_list)
    rider = _Rider(comm)

    def body(*refs):
        own, ride = rider.split(refs, nd + 6, 3)
        dh_refs = own[:nd]
        w_ref, dr_ref, x_ref, pg_ref, pb_ref, sc_ref, dx_ref, dsc_ref, dsh_ref = own[nd:]
        i = pl.program_id(0)
        ride(i, T // tm)

        def part(k):
            if blocked:
                return _dot_nt(dh_refs[0][k], w_ref[k])
            return _dot_nt(dh_refs[k][...], w_ref[:, k * D:(k + 1) * D])

        du = part(0)
        for k in range(1, nk):
            du = du + part(k)

        @pl.when(i % tpb == 0)
        def _():
            dsc_ref[...] = jnp.zeros_like(dsc_ref)
            dsh_ref[...] = jnp.zeros_like(dsh_ref)

        xin = x_ref[...] * pg_ref[...] + pb_ref[...]
        dx_ref[...] = DEEPNORM_ALPHA * dr_ref[...] + du * (1.0 + sc_ref[...])
        dsc_ref[...] += _rowsum(du * xin)
        dsh_ref[...] += _rowsum(du)

    mod = _modspec(D, tpb, 1)
    outs = pl.pallas_call(
        body, name=name, grid=(T // tm,),
        in_specs=dh_specs + [_resident(w.shape), row, row, _vec(D, 1), _vec(D, 1), mod] + rider.in_specs,
        out_specs=[row, mod, mod] + rider.out_specs,
        out_shape=[SDS((T, D), F32), SDS((Bl, 1, D), F32), SDS((Bl, 1, D), F32)] + rider.out_shape,
        scratch_shapes=rider.scratch,
        compiler_params=_cparams(("arbitrary",)))(*dh_list, w, dr, xs, pg, pb, sc, *rider.ins)
    return rider.result(outs, 3)


def merge_bwd(do2, proj, ysb, yco, wsb, wco, wout, name):
    T, D = do2.shape
    tm = min(256, T)

    def body(do_ref, ga_ref, gb_ref, ysb_ref, yco_ref, wsb_ref, wco_ref, wout_ref,
             dysb_ref, dyco_ref, dga_ref, dgb_ref, dya_ref, dcs_ref):
        dm = _dot_nt(do_ref[...], wout_ref[...])
        sa = _sig(ga_ref[...])
        sb = _sig(gb_ref[...])
        dysb = (dm * sa).astype(BF)
        dyco = (dm * sb).astype(BF)
        dysb_ref[...] = dysb
        dyco_ref[...] = dyco
        dga_ref[...] = (dm * ysb_ref[...].astype(F32) * (sa * (1.0 - sa))).astype(BF)
        dgb_ref[...] = (dm * yco_ref[...].astype(F32) * (sb * (1.0 - sb))).astype(BF)
        dya_ref[...] = _dot_nt(dysb, wsb_ref[...]).astype(BF)
        dcs_ref[...] = _dot_nt(dyco, wco_ref[...])

    row = pl.BlockSpec((tm, D), lambda i: (i, 0))
    wfull = pl.BlockSpec((D, D), lambda i: (0, 0))
    return pl.pallas_call(
        body, name=name, grid=(T // tm,),
        in_specs=[row, pl.BlockSpec((tm, D), lambda i: (i, 2)), pl.BlockSpec((tm, D), lambda i: (i, 3)),
                  row, row, wfull, wfull, wfull],
        out_specs=[row] * 6,
        out_shape=[SDS((T, D), BF)] * 5 + [SDS((T, D), F32)],
        compiler_params=_cparams(("parallel",)))(do2, proj, proj, ysb, yco, wsb, wco, wout)


def att_bwd(proj, dyatt, car, Bl, S, D, name, comm=None):
    dh, cw, hp, nblk, tq, tk, nq, nk = _att_dims(S, D)
    scale = 1.0 / math.sqrt(dh)
    rider = _Rider(comm)

    def body(*refs):
        (q_ref, k_ref, v_ref, do_ref, car_ref, dq_ref, dk_ref, dv_ref,
         qs, ks, vs, dos, kts, dk_acc, dv_acc), ride = rider.split(refs, 5, 3, 7)
        ride(pl.program_id(0) * nblk + pl.program_id(1), Bl * nblk)
        usum, lsum, dmask = _att_consts(tk)
        for hh in range(hp):
            sl = slice(hh * dh, (hh + 1) * dh)
            qs[hh] = (q_ref[:, sl] * scale).astype(BF)
            ks[hh] = k_ref[:, sl].astype(BF)
            vs[hh] = v_ref[:, sl].astype(BF)
            dos[hh] = do_ref[:, sl]
            for kb in range(nk):
                kts[hh, kb] = k_ref[kb * tk:(kb + 1) * tk, sl].astype(F32).T.astype(BF)
        dk_acc[...] = jnp.zeros_like(dk_acc)
        dv_acc[...] = jnp.zeros_like(dv_acc)
        nch = tq // tk

        def qloop(qb, _):
            qo = pl.multiple_of(qb * tq, tq)
            n_full = qb * nch

            def blk(kb, state, diag):
                ko = pl.multiple_of(kb * tk, tk)
                chains = [(hh, c) for hh in range(hp) for c in range(0 if diag is None else diag, nch)]

                def masked(ch, val):
                    return jnp.where(dmask, val, 0.0) if ch[1] == diag else val

                def qrows(ref, ch):
                    return ref[ch[0], pl.ds(pl.multiple_of(qo + ch[1] * tk, tk), tk), :]

                k = [ks[hh, pl.ds(ko, tk), :] for hh in range(hp)]
                v = [vs[hh, pl.ds(ko, tk), :] for hh in range(hp)]
                z = {ch: _dot_nt(k[ch[0]], qrows(qs, ch)) for ch in chains}
                dw = {ch: _dot_nt(v[ch[0]], qrows(dos, ch)) for ch in chains}
                sp = {ch: _softplus(z[ch]) for ch in chains}
                lk = {ch: masked(ch, -sp[ch]) for ch in chains}
                cs = {ch: _split_dot(usum, lk[ch]) for ch in chains}
                w = {ch: masked(ch, jnp.exp((z[ch] - sp[ch])
                                            + car_ref[ch[0], qb * nk + kb, 0:1, ch[1] * tk:(ch[1] + 1) * tk]
                                            + cs[ch][:tk])) for ch in chains}
                dlw = {ch: dw[ch] * w[ch] for ch in chains}
                gs = {ch: _split_dot(lsum, dlw[ch]) for ch in chains}
                sg = {ch: jnp.exp(z[ch] - sp[ch]) for ch in chains}
                dzb = {ch: masked(ch, dlw[ch] - sg[ch] * (dlw[ch] + state[ch[0]][ch[1]][0][0:1, :] + gs[ch][:tk])
                                  ).astype(BF) for ch in chains}
                wb = {ch: w[ch].astype(BF) for ch in chains}
                for hh in range(hp):
                    mine = [ch for ch in chains if ch[0] == hh]
                    dk_acc[hh, kb] += sum(_dot_nn(dzb[ch], qrows(qs, ch)) for ch in mine)
                    dv_acc[hh, kb] += sum(_dot_nn(wb[ch], qrows(dos, ch)) for ch in mine)
                dq = {ch: _dot_nn(kts[ch[0], kb], dzb[ch]) for ch in chains}
                return tuple(tuple(
                    (state[hh][c][0] + gs[(hh, c)][tk:], state[hh][c][1] + dq[(hh, c)]) if (hh, c) in z else state[hh][c]
                    for c in range(nch)) for hh in range(hp))

            state = tuple(tuple((jnp.zeros((8, tk), F32), jnp.zeros((dh, tk), F32)) for _ in range(nch))
                          for _ in range(hp))
            state = lax.fori_loop(0, n_full, lambda kb, st: blk(kb, st, None), state)
            for i in range(nch):
                state = blk(n_full + i, state, i)
            for hh in range(hp):
                for c in range(nch):
                    dq_ref[pl.ds(pl.multiple_of(qo + c * tk, tk), tk), hh * dh:(hh + 1) * dh] = (
                        (state[hh][c][1].T * scale).astype(BF))
            return 0

        lax.fori_loop(0, nq, qloop, 0)
        for hh in range(hp):
            sl = slice(hh * dh, (hh + 1) * dh)
            for kb in range(nk):
                dk_ref[kb * tk:(kb + 1) * tk, sl] = dk_acc[hh, kb].astype(BF)
                dv_ref[kb * tk:(kb + 1) * tk, sl] = dv_acc[hh, kb].astype(BF)

    def seg(s):
        return pl.BlockSpec((S, cw), lambda b, h: (b, s * nblk + h))

    blk_spec = pl.BlockSpec((S, cw), lambda b, h: (b, h))
    outs = pl.pallas_call(
        body, name=name, grid=(Bl, nblk),
        in_specs=[seg(0), seg(1), seg(2), blk_spec,
                  pl.BlockSpec((None, hp, nq * nk, 8, tq), lambda b, h: (b, h, 0, 0, 0))] + rider.in_specs,
        out_specs=[blk_spec, blk_spec, blk_spec] + rider.out_specs,
        out_shape=[SDS((Bl * S, D), BF)] * 3 + rider.out_shape,
        scratch_shapes=[pltpu.VMEM((hp, S, dh), BF)] * 4 + [pltpu.VMEM((hp, nk, dh, tk), BF)]
        + [pltpu.VMEM((hp, nk, tk, dh), F32)] * 2 + rider.scratch,
        compiler_params=_cparams(rider.sem(("parallel", "parallel"))))(proj, proj, proj, dyatt, car, *rider.ins)
    return rider.result(outs, 3)


def conv_ln_bwd(dcs, xhc, rstd_c, cg, cbeta, name):
    T, D = dcs.shape
    tm = min(256, T)

    def body(dcs_ref, xh_ref, rs_ref, g_ref, b_ref, dconv_ref, dg_ref, db_ref, dcb_ref):
        @pl.when(pl.program_id(0) == 0)
        def _():
            dg_ref[...] = jnp.zeros_like(dg_ref)
            db_ref[...] = jnp.zeros_like(db_ref)
            dcb_ref[...] = jnp.zeros_like(dcb_ref)
        xh = xh_ref[...]
        cl = xh * g_ref[...] + b_ref[...]
        s = _sig(cl)
        dcl = dcs_ref[...] * (s * (1.0 + cl * (1.0 - s)))
        dg_ref[...] += _rowsum(dcl * xh)
        db_ref[...] += _rowsum(dcl)
        dconv = _ln_bwd(dcl * g_ref[...], xh, rs_ref[...])
        dconv_ref[...] = dconv
        dcb_ref[...] += _rowsum(dconv)

    row = pl.BlockSpec((tm, D), lambda i: (i, 0))
    vec = _vec(D, 1)
    return pl.pallas_call(
        body, name=name, grid=(T // tm,),
        in_specs=[row, row, pl.BlockSpec((tm, 1), lambda i: (i, 0)), vec, vec],
        out_specs=[row, vec, vec, vec],
        out_shape=[SDS((T, D), F32)] + [SDS((1, D), F32)] * 3,
        compiler_params=_cparams(("arbitrary",)))(dcs, xhc, rstd_c, cg, cbeta)


def conv_bwd(dconv, proj, cw32, Bl, S, D, kw, name):
    T = Bl * S
    ts = min(128, S)
    ns = S // ts
    off = HALO - (kw - 1)
    rc = min(64, ts)
    cw = min(LANE, D)
    hpb = ts // HALO
    nhb = T // HALO

    def body(dc_ref, dcn_ref, a_ref, b_ref, ha_ref, hb_ref, w_ref, da_ref, db_ref, dw_ref, hsh, dsh, dh_s):
        b_ = pl.program_id(0)
        i = pl.program_id(1)
        span = ts + HALO - SUBLANES

        @pl.when((b_ == 0) & (i == 0))
        def _():
            dw_ref[...] = jnp.zeros_like(dw_ref)
        a = a_ref[...]
        sb = _sig(b_ref[...])
        h = a * sb
        hh = jnp.where(i == 0, 0.0, ha_ref[...] * _sig(hb_ref[...]))
        dc = dc_ref[...]
        dcn = jnp.where(i == ns - 1, 0.0, dcn_ref[...])
        for cb_ in range(D // cw):
            cols = slice(cb_ * cw, (cb_ + 1) * cw)
            hsh[0, cb_, pl.ds(HALO, ts), :] = h[:, cols]
            hsh[0, cb_, pl.ds(0, HALO), :] = hh[:, cols]
            dsh[0, cb_, pl.ds(0, ts), :] = dc[:, cols]
            dsh[0, cb_, pl.ds(ts, HALO), :] = dcn[:, cols]
            for s in range(1, SUBLANES):
                hsh[s, cb_, pl.ds(0, span), :] = hsh[0, cb_, pl.ds(s, span), :]
                dsh[s, cb_, pl.ds(0, span), :] = dsh[0, cb_, pl.ds(s, span), :]
            accs = [jnp.zeros((rc, cw), F32) for _ in range(ts // rc)]
            d0 = [dsh[0, cb_, pl.ds(r * rc, rc), :] for r in range(ts // rc)]
            for k in range(kw):
                wk = w_ref[k:k + 1, cols]
                wsum = jnp.zeros((rc, cw), F32)
                sd = ((kw - 1) - k) % SUBLANES
                sh_ = (off + k) % SUBLANES
                for r in range(ts // rc):
                    accs[r] = accs[r] + wk * dsh[sd, cb_, pl.ds(r * rc + (kw - 1) - k - sd, rc), :]
                    wsum = wsum + d0[r] * hsh[sh_, cb_, pl.ds(r * rc + off + k - sh_, rc), :]
                dw_ref[k:k + 1, cols] += _rowsum(wsum)
            for r in range(ts // rc):
                dh_s[pl.ds(r * rc, rc), cols] = accs[r]
        dhc = dh_s[...]
        da_ref[...] = (dhc * sb).astype(BF)
        db_ref[...] = (dhc * a * (sb * (1.0 - sb))).astype(BF)

    def tile(seg):
        return pl.BlockSpec((ts, D), lambda b, i: (b * ns + i, seg))

    def halo(seg):
        return pl.BlockSpec((HALO, D), lambda b, i: (jnp.maximum((b * ns + i) * hpb - 1, 0), seg))

    row = pl.BlockSpec((ts, D), lambda b, i: (b * ns + i, 0))
    nxt = pl.BlockSpec((HALO, D), lambda b, i: (jnp.minimum((b * ns + i + 1) * hpb, nhb - 1), 0))
    wspec = pl.BlockSpec((HALO, D), lambda b, i: (0, 0))
    return pl.pallas_call(
        body, name=name, grid=(Bl, ns),
        in_specs=[row, nxt, tile(0), tile(1), halo(0), halo(1), wspec],
        out_specs=[row, row, wspec],
        out_shape=[SDS((T, D), BF), SDS((T, D), BF), SDS((HALO, D), F32)],
        scratch_shapes=[pltpu.VMEM((SUBLANES, D // cw, ts + HALO, cw), F32)] * 2 + [pltpu.VMEM((ts, D), F32)],
        compiler_params=_cparams(("arbitrary", "arbitrary")))(dconv, dconv, proj, proj, proj, proj, cw32)


def matmul_tn(xa, ga, x_spec, g_spec, out_shape, out_spec, acc_shape, grid, name, comm=None):
    nk = grid[-1]
    rider = _Rider(comm)

    def body(*refs):
        (x_ref, g_ref, o_ref, acc), ride = rider.split(refs, 2, 1, 1)
        k = pl.program_id(1)
        ride(pl.program_id(0) * nk + k, grid[0] * nk)

        @pl.when(k == 0)
        def _():
            acc[...] = jnp.zeros_like(acc)
        acc[...] += _dot_tn(x_ref[...], g_ref[...])

        @pl.when(k == nk - 1)
        def _():
            o_ref[...] = acc[...]

    outs = pl.pallas_call(
        body, name=name, grid=grid, in_specs=[x_spec, g_spec] + rider.in_specs,
        out_specs=[out_spec] + rider.out_specs,
        out_shape=[SDS(out_shape, F32)] + rider.out_shape,
        scratch_shapes=[pltpu.VMEM(acc_shape, F32)] + rider.scratch,
        compiler_params=_cparams(rider.sem(("parallel", "arbitrary"))))(xa, ga, *rider.ins)
    own, landed = rider.result(outs, 1)
    return own[0] if comm is None else (own[0], landed)


def wgrad_std(xa, ga, name):
    T, M = xa.shape
    N = ga.shape[1]
    tk = min(1024, T)
    return matmul_tn(xa, ga, pl.BlockSpec((tk, M), lambda n, k: (k, 0)), pl.BlockSpec((tk, N), lambda n, k: (k, 0)),
                     (M, N), pl.BlockSpec((M, N), lambda n, k: (0, 0)), (M, N), (1, T // tk), name)


def wgrad_down(p4, df, name):
    nj, T, nb = p4.shape
    D = df.shape[1]
    tk = min(1024, T)
    return matmul_tn(p4, df, pl.BlockSpec((None, tk, nb), lambda j, k: (j, k, 0)),
                     pl.BlockSpec((tk, D), lambda j, k: (k, 0)),
                     (nj * nb, D), pl.BlockSpec((nb, D), lambda j, k: (j, 0)), (nb, D), (nj, T // tk), name)


def wgrad_gu(u, dh8, name, comm=None):
    n8, T, nb = dh8.shape
    D = u.shape[1]
    tk = min(1024, T)
    return matmul_tn(u, dh8, pl.BlockSpec((tk, D), lambda j, k: (k, 0)),
                     pl.BlockSpec((None, tk, nb), lambda j, k: (j, k, 0)),
                     (n8, D, nb), pl.BlockSpec((None, D, nb), lambda j, k: (j, 0, 0)), (D, nb), (n8, T // tk), name,
                     comm=comm)


def wgrad_segments(u, segs, name):
    T, D = u.shape
    ns = len(segs)
    tk = min(1024, T)
    nk = T // tk

    def body(*refs):
        x_ref, g_refs, o_ref, acc = refs[0], refs[1:1 + ns], refs[1 + ns], refs[2 + ns]
        s = pl.program_id(0)
        k = pl.program_id(1)

        @pl.when(k == 0)
        def _():
            acc[...] = jnp.zeros_like(acc)
        for i in range(ns):
            @pl.when(s == i)
            def _(i=i):
                acc[...] += _dot_tn(x_ref[...], g_refs[i][...])

        @pl.when(k == nk - 1)
        def _():
            o_ref[...] = acc[...]

    seg_specs = [pl.BlockSpec((tk, D), lambda s, k, i=i: (jnp.where(s == i, k, 0), 0)) for i in range(ns)]
    return pl.pallas_call(
        body, name=name, grid=(ns, nk),
        in_specs=[pl.BlockSpec((tk, D), lambda s, k: (k, 0))] + seg_specs,
        out_specs=pl.BlockSpec((D, D), lambda s, k: (0, s)),
        out_shape=SDS((D, ns * D), F32), scratch_shapes=[pltpu.VMEM((D, D), F32)],
        compiler_params=_cparams(("parallel", "arbitrary")))(u, *segs)


def kernel(x, c, w_ada, b_ada, ffn1_w_gu, ffn1_w_down, ln1_g, ln1_b, w_in, w_sb_out, conv_w, conv_b, conv_ln_g, conv_ln_b, w_conv_out, w_out, ln2_g, ln2_b, ffn2_w_gu, ffn2_w_down, ln3_g, ln3_b, loss_target, m_w_ada, m_b_ada, m_ffn1_w_gu, m_ffn1_w_down, m_ln1_g, m_ln1_b, m_w_in, m_w_sb_out, m_conv_w, m_conv_b, m_conv_ln_g, m_conv_ln_b, m_w_conv_out, m_w_out, m_ln2_g, m_ln2_b, m_ffn2_w_gu, m_ffn2_w_down, m_ln3_g, m_ln3_b, v_w_ada, v_b_ada, v_ffn1_w_gu, v_ffn1_w_down, v_ln1_g, v_ln1_b, v_w_in, v_w_sb_out, v_conv_w, v_conv_b, v_conv_ln_g, v_conv_ln_b, v_w_conv_out, v_w_out, v_ln2_g, v_ln2_b, v_ffn2_w_gu, v_ffn2_w_down, v_ln3_g, v_ln3_b):
    Bl, S, D = x.shape
    T = Bl * S
    kw = conv_w.shape[1]
    ax, ay, ac = lax.axis_index("x"), lax.axis_index("y"), lax.axis_index("c")
    me = 4 * ax + 2 * ay + ac
    qc = jnp.stack([2 * ax + ay, ac]).astype(jnp.int32)

    big = dict(ffn1_w_gu=ffn1_w_gu[0], ffn1_w_down=ffn1_w_down[0], w_in=w_in[0], w_sb_out=w_sb_out[0],
               w_conv_out=w_conv_out[0], w_out=w_out[0], ffn2_w_gu=ffn2_w_gu[0], ffn2_w_down=ffn2_w_down[0])
    big_m = dict(ffn1_w_gu=m_ffn1_w_gu[0], ffn1_w_down=m_ffn1_w_down[0], w_in=m_w_in[0], w_sb_out=m_w_sb_out[0],
                 w_conv_out=m_w_conv_out[0], w_out=m_w_out[0], ffn2_w_gu=m_ffn2_w_gu[0], ffn2_w_down=m_ffn2_w_down[0])
    big_v = dict(ffn1_w_gu=v_ffn1_w_gu[0], ffn1_w_down=v_ffn1_w_down[0], w_in=v_w_in[0], w_sb_out=v_w_sb_out[0],
                 w_conv_out=v_w_conv_out[0], w_out=v_w_out[0], ffn2_w_gu=v_ffn2_w_gu[0], ffn2_w_down=v_ffn2_w_down[0])
    names = list(big)

    layer1, layer2, layer3 = ["ffn1_w_gu", "ffn1_w_down"], ["w_in", "w_sb_out", "w_conv_out", "w_out"], \
        ["ffn2_w_gu", "ffn2_w_down"]

    def shards(group):
        return [big[n].astype(BF) for n in group]

    wave0, wave1 = ["ffn1_w_gu"], ["ffn1_w_down", "w_in"]
    wave2 = ["ffn2_w_gu", "ffn2_w_down", "w_sb_out", "w_conv_out", "w_out"]
    G = dict(zip(wave0, run_comm(gather_plan(shards(wave0)), "allgather_ffn1")))
    wg1 = G["ffn1_w_gu"]

    cw8 = small_allgather(conv_w[0], "allgather_conv_w")
    cw_full = jnp.transpose(cw8, (1, 0, 2)).reshape(kw, D)
    cw32 = jnp.concatenate([cw_full, jnp.zeros((HALO - kw, D), F32)], axis=0)

    c_all = small_allgather(c, "allgather_c").reshape(N_DEV * Bl, D)
    ncol = w_ada.shape[2]
    b_cols = lax.dynamic_slice(b_ada, (0, me * ncol), (1, ncol))
    mod_cols, s_all = ada_fwd(c_all, w_ada[0], b_cols, "ada_fwd")
    mod8 = small_allgather(mod_cols, "allgather_mod")
    mod_mine = lax.dynamic_slice(mod8, (0, me * Bl, 0), (N_DEV, Bl, ncol))
    mod = jnp.transpose(mod_mine, (1, 0, 2)).reshape(Bl, N_MOD_ROWS, 1, D)
    sh1, sc1, g1, sh2, sc2, g2, sh3, sc3, g3 = [mod[:, i] for i in range(N_MOD_ROWS)]

    ones = jnp.ones((1, D), F32)
    zeros = jnp.zeros((1, D), F32)
    xf = x.reshape(T, D)
    tgt = loss_target.reshape(T, D)

    (u1, a1, gg1, p1), landed = ffn_up(xf, ones, zeros, sc1, sh1, wg1, "ffn1_up", comm=gather_plan(shards(wave1)))
    G.update(zip(wave1, landed))
    wd1 = G["ffn1_w_down"].reshape(wg1.shape[0] // 2, wg1.shape[2], D)
    win = jnp.transpose(G["w_in"], (1, 0, 2)).reshape(D, -1)
    xh1, rs1, f1 = ffn_down_ln(p1, wd1, xf, ones, zeros, g1, "ffn1_down_ln")
    (u2, qkv, proj), landed = mod_matmul(xh1, ln1_g, ln1_b, sc2, sh2, win, "in_proj",
                                         comm=gather_plan(shards(wave2)))
    G.update(zip(wave2, landed))
    wg2 = G["ffn2_w_gu"]
    wd2 = G["ffn2_w_down"].reshape(wg2.shape[0] // 2, wg2.shape[2], D)
    wsb = G["w_sb_out"].reshape(D, D)
    wco = G["w_conv_out"].reshape(-1, D)
    wout = G["w_out"].reshape(D, D)
    yatt, car = att_fwd(qkv, Bl, S, D, "att_fwd")
    cs, xhc, rsc = conv_fwd(proj, cw32, conv_b, conv_ln_g, conv_ln_b, Bl, S, D, kw, "conv_fwd")
    xh2, rs2, ysb, yco, merged, o2 = mix_fwd(yatt, cs, proj, wsb, wco, wout, xh1, ln1_g, ln1_b, g2, "mix_fwd")
    (u3, a3, gg3, p3), _ = ffn_up(xh2, ln2_g, ln2_b, sc3, sh3, wg2, "ffn2_up")
    xh3, rs3, f3 = ffn_down_ln(p3, wd2, xh2, ln2_g, ln2_b, g3, "ffn2_down_ln")

    own_sum, recv_b = {}, {}

    def by_owner(group, grads):
        return [g.reshape((4, 2) + big[n].shape) for n, g in zip(group, grads)]

    def chip_sums(group, g42, recv_a):
        sums = [chip_sum(g, r, qc, "chip_sum_" + n) for g, r, n in zip(g42, recv_a, group)]
        own_sum.update({n: s[0] for n, s in zip(group, sums)})
        return chips_plan([s[1] for s in sums])

    dr3, df3, dln3g, dln3b, dg3, lossp = ln_bwd(None, xh3, rs3, ln3_g, ln3_b, g3, f3, MACARON_WEIGHT,
                                                 "ln3_bwd", target=tgt)
    dh3 = swiglu_bwd(df3, wd2, a3, gg3, "ffn2_swiglu_bwd").reshape((-1,) + a3.shape[1:])
    gw_d2 = wgrad_down(p3, df3, "wgrad_ffn2_down")
    gw_g2 = wgrad_gu(u3, dh3, "wgrad_ffn2_gu")
    g42 = by_owner(layer3, [gw_g2, gw_d2])
    (dx2, dsc3, dsh3), recv_a = mod_bwd(dh3, wg2, dr3, xh2, ln2_g, ln2_b, sc3, True, "ffn2_mod_bwd",
                                        comm=sibling_plan(g42))
    plan3 = chip_sums(layer3, g42, recv_a)

    dr2, do2, dln2g, dln2b, dg2 = ln_bwd(dx2, xh2, rs2, ln2_g, ln2_b, g2, o2, 1.0, "ln2_bwd")
    gw_out = wgrad_std(merged, do2, "wgrad_out")
    dysb, dyco, dga, dgb, dyatt, dcs = merge_bwd(do2, proj, ysb, yco, wsb, wco, wout, "merge_bwd")
    gw_sb = wgrad_std(yatt, dysb, "wgrad_sb")
    gw_co = wgrad_std(cs, dyco, "wgrad_conv_out")
    (dq, dk, dv), landed = att_bwd(qkv, dyatt, car, Bl, S, D, "att_bwd", comm=plan3)
    recv_b.update(zip(layer3, landed))
    dconv, dcg, dcbeta, dcb = conv_ln_bwd(dcs, xhc, rsc, conv_ln_g, conv_ln_b, "conv_ln_bwd")
    dglu_a, dglu_b, dcw = conv_bwd(dconv, proj, cw32, Bl, S, D, kw, "conv_bwd")
    dproj = [dq, dk, dv, dglu_a, dglu_b, dga, dgb]
    gw_in = wgrad_segments(u2, dproj, "wgrad_in")
    gw_in = jnp.transpose(gw_in.reshape(D, N_DEV, -1), (1, 0, 2))
    g42 = by_owner(layer2, [gw_in, gw_sb, gw_co, gw_out])
    (dx1, dsc2, dsh2), recv_a = mod_bwd(dproj, win, dr2, xh1, ln1_g, ln1_b, sc2, False, "mix_mod_bwd",
                                        comm=sibling_plan(g42))
    plan2 = chip_sums(layer2, g42, recv_a)

    dr1, df1, dln1g, dln1b, dg1 = ln_bwd(dx1, xh1, rs1, ln1_g, ln1_b, g1, f1, MACARON_WEIGHT, "ln1_bwd")
    dh1 = swiglu_bwd(df1, wd1, a1, gg1, "ffn1_swiglu_bwd").reshape((-1,) + a1.shape[1:])
    gw_d1 = wgrad_down(p1, df1, "wgrad_ffn1_down")
    gw_g1, landed = wgrad_gu(u1, dh1, "wgrad_ffn1_gu", comm=plan2)
    recv_b.update(zip(layer2, landed))
    g42 = by_owner(layer1, [gw_g1, gw_d1])
    plan1 = chip_sums(layer1, g42, run_comm(sibling_plan(g42), "rs_sibling_ffn1"))
    (grad_x, dsc1, dsh1), landed = mod_bwd(dh1, wg1, dr1, xf, ones, zeros, sc1, True, "ffn1_mod_bwd", comm=plan1)
    recv_b.update(zip(layer1, landed))

    dmod = jnp.concatenate([dsh1, dsc1, dg1, dsh2, dsc2, dg2, dsh3, dsc3, dg3], axis=1).reshape(Bl, N_MOD_ROWS * D)
    dmod_all = small_allgather(dmod, "allgather_dmod").reshape(N_DEV * Bl, N_MOD_ROWS * D)
    dmod_cols = lax.dynamic_slice(dmod_all, (0, me * ncol), (N_DEV * Bl, ncol))
    g_w_ada, g_b_ada = ada_bwd(s_all, dmod_cols, dmod_all, "ada_bwd")

    npad = 16
    small_rows = [dln1g, dln1b, dcb, dcg, dcbeta, dln2g, dln2b, dln3g, dln3b,
                  jnp.broadcast_to(lossp[0:1, 0:1], (1, D))]
    pack = jnp.concatenate(small_rows + [jnp.zeros((npad - len(small_rows), D), F32), dcw], axis=0)
    small = small_sum(small_allgather(pack, "allgather_small"), "small_sum")
    loss = small[9, 0]
    small_w = [ln1_g, ln1_b, conv_b, conv_ln_g, conv_ln_b, ln2_g, ln2_b, ln3_g, ln3_b]
    small_m = [m_ln1_g, m_ln1_b, m_conv_b, m_conv_ln_g, m_conv_ln_b, m_ln2_g, m_ln2_b, m_ln3_g, m_ln3_b]
    small_v = [v_ln1_g, v_ln1_b, v_conv_b, v_conv_ln_g, v_conv_ln_b, v_ln2_g, v_ln2_b, v_ln3_g, v_ln3_b]
    padrows = jnp.zeros((npad - len(small_w), D), F32)
    s_g, s_d, s_m, s_v = adamw(jnp.concatenate(small_w + [padrows], axis=0),
                               jnp.concatenate(small_m + [padrows], axis=0),
                               jnp.concatenate(small_v + [padrows], axis=0),
                               [_plain_part(small[:npad])], "adamw_small")
    dcw_mine = lax.dynamic_slice(small[npad:npad + kw], (0, me * (D // N_DEV)), (kw, D // N_DEV))
    cw_g, cw_d, cw_m, cw_v = adamw(conv_w[0], m_conv_w[0], v_conv_w[0], [_plain_part(dcw_mine)], "adamw_conv_w")
    ada_g, ada_d, ada_m, ada_v = adamw(w_ada[0], m_w_ada[0], v_w_ada[0], [_plain_part(g_w_ada)], "adamw_w_ada")
    bada_g, bada_d, bada_m, bada_v = adamw(b_ada, m_b_ada, v_b_ada, [_plain_part(g_b_ada)], "adamw_b_ada")

    res = {}
    for n in names:
        rb = recv_b[n]
        parts = [_plain_part(own_sum[n]), _slot_part(rb, 0), _slot_part(rb, 1), _slot_part(rb, 2)]
        res[n] = adamw(big[n], big_m[n], big_v[n], parts, "adamw_" + n)

    def small_out(k):
        order = dict(ln1_g=0, ln1_b=1, conv_b=2, conv_ln_g=3, conv_ln_b=4, ln2_g=5, ln2_b=6, ln3_g=7, ln3_b=8)
        return lambda arr: arr[order[k]:order[k] + 1]

    weight_order = ["w_ada", "b_ada", "ffn1_w_gu", "ffn1_w_down", "ln1_g", "ln1_b", "w_in", "w_sb_out", "conv_w",
                    "conv_b", "conv_ln_g", "conv_ln_b", "w_conv_out", "w_out", "ln2_g", "ln2_b", "ffn2_w_gu",
                    "ffn2_w_down", "ln3_g", "ln3_b"]

    shapes = dict(w_ada=w_ada.shape, b_ada=b_ada.shape, conv_w=conv_w.shape, ln1_g=ln1_g.shape,
                  **{n: (1,) + big[n].shape for n in names})

    def pick(which):
        outs = []
        for n in weight_order:
            if n == "w_ada":
                a = (ada_g, ada_d, ada_m, ada_v)[which]
            elif n == "b_ada":
                a = (bada_g, bada_d, bada_m, bada_v)[which]
            elif n == "conv_w":
                a = (cw_g, cw_d, cw_m, cw_v)[which]
            elif n in res:
                a = res[n][which]
            else:
                a = small_out(n)((s_g, s_d, s_m, s_v)[which])
            outs.append(a.reshape(shapes.get(n, ln1_g.shape)))
        return outs

    return (loss, grad_x.reshape(Bl, S, D), *pick(0), *pick(1), *pick(2), *pick(3))
```

```python
import functools
import math

import jax
import jax.numpy as jnp
from jax import lax
from jax.experimental import pallas as pl
from jax.experimental.pallas import tpu as pltpu

F32 = jnp.float32
BF = jnp.bfloat16
SDS = jax.ShapeDtypeStruct
MESH = pl.DeviceIdType.MESH

N_DEV = 8
SB_HEAD_DIM = 64
N_MOD_ROWS = 9
MACARON_WEIGHT = 0.5
DEEPNORM_ALPHA = 2.0 ** 0.25
LN_EPS = 1e-5
ADAM_LR = 0.001
ADAM_B1 = 0.9
ADAM_B2 = 0.999
ADAM_EPS = 1e-08
ADAM_WD = 0.01
ADAM_STEP = 10

V7X_VMEM_LIMIT = 52 * 1024 * 1024
LANE = 128
SUBLANES = 8
HALO = 32


def _cparams(sem=None):
    return pltpu.CompilerParams(dimension_semantics=sem, vmem_limit_bytes=V7X_VMEM_LIMIT)


def _dot_nn(a, b):
    return lax.dot_general(a, b, (((1,), (0,)), ((), ())), preferred_element_type=F32)


def _dot_nt(a, b):
    return lax.dot_general(a, b, (((1,), (1,)), ((), ())), preferred_element_type=F32)


def _dot_tn(a, b):
    return lax.dot_general(a, b, (((0,), (0,)), ((), ())), preferred_element_type=F32)


def _sig(x):
    return 1.0 / (1.0 + jnp.exp(-x))


def _ln_stats(r):
    mu = jnp.mean(r, axis=-1, keepdims=True)
    d = r - mu
    var = jnp.mean(d * d, axis=-1, keepdims=True)
    rstd = lax.rsqrt(var + LN_EPS)
    return d * rstd, rstd


def _ln_bwd(dxh, xh, rstd):
    m1 = jnp.mean(dxh, axis=-1, keepdims=True)
    m2 = jnp.mean(dxh * xh, axis=-1, keepdims=True)
    return rstd * (dxh - m1 - xh * m2)


def _rowsum(v):
    return jnp.sum(v, axis=0, keepdims=True)


def _row_tile(n, cap):
    if n <= cap:
        return n
    best = None
    for t in range(8, cap + 1, 8):
        if n % t == 0:
            best = t
    assert best is not None, (n, cap)
    return best


def _coords():
    x, y, c = lax.axis_index("x"), lax.axis_index("y"), lax.axis_index("c")
    return x, y, c


def _flip(v, bit):
    return 1 - v if bit else v


def small_allgather(blk, name):
    r, n = blk.shape

    def body(x_ref, out_ref, send_sems, recv_sems):
        x, y, c = _coords()
        me = 4 * x + 2 * y + c
        out_ref[me] = x_ref[...]
        copies = []
        for k in range(1, N_DEV):
            peer = (_flip(x, k & 4), _flip(y, k & 2), _flip(c, k & 1))
            cp = pltpu.make_async_remote_copy(
                src_ref=x_ref, dst_ref=out_ref.at[me], send_sem=send_sems.at[k - 1],
                recv_sem=recv_sems.at[k - 1], device_id=peer, device_id_type=MESH)
            cp.start()
            copies.append(cp)
        for k in range(1, N_DEV):
            px, py, pc = _flip(x, k & 4), _flip(y, k & 2), _flip(c, k & 1)
            slot = 4 * px + 2 * py + pc
            pltpu.make_async_remote_copy(
                src_ref=x_ref, dst_ref=out_ref.at[slot], send_sem=send_sems.at[k - 1],
                recv_sem=recv_sems.at[k - 1], device_id=(px, py, pc), device_id_type=MESH).wait_recv()
        for cp in copies:
            cp.wait_send()

    return pl.pallas_call(
        body, name=name,
        out_shape=SDS((N_DEV, r, n), blk.dtype),
        in_specs=[pl.BlockSpec(memory_space=pltpu.VMEM)],
        out_specs=pl.BlockSpec(memory_space=pltpu.VMEM),
        scratch_shapes=[pltpu.SemaphoreType.DMA((N_DEV - 1,)), pltpu.SemaphoreType.DMA((N_DEV - 1,))],
    )(blk)


class CommPlan:
    def __init__(self, ins, out_shape, scratch, emit):
        self.ins, self.out_shape, self.scratch, self.emit = list(ins), list(out_shape), list(scratch), emit


def _phase(step, at, fn):
    if step is None:
        fn()
    else:
        pl.when(step == at)(fn)


def gather_plan(shards):
    n = len(shards)
    per = 7

    def emit(ins, outs, sems, step, nsteps):
        send_sems, recv_sems, local_sems = sems
        x, y, c = _coords()
        me = 4 * x + 2 * y + c
        sibling = (x, y, 1 - c)
        chips = [(1 - x, y), (x, 1 - y), (1 - x, 1 - y)]

        def slot(px, py, pc):
            return 4 * px + 2 * py + pc

        def copy(t, k, block, to, src=None):
            dst = outs[t].at[slot(*block)]
            return pltpu.make_async_remote_copy(
                src_ref=dst if src is None else src, dst_ref=dst,
                send_sem=send_sems.at[per * t + k], recv_sem=recv_sems.at[per * t + k],
                device_id=to, device_id_type=MESH)

        def local(t):
            return pltpu.make_async_copy(ins[t], outs[t].at[me], local_sems.at[t])

        def first(t):
            return [copy(t, 0, (x, y, c), sibling, src=ins[t])] + [
                copy(t, 1 + j, (x, y, c), (*chip, c), src=ins[t]) for j, chip in enumerate(chips)]

        def passed(t):
            return [copy(t, 4 + j, (*chip, c), sibling) for j, chip in enumerate(chips)]

        def start():
            for t in range(n):
                local(t).start()
                for cp in first(t):
                    cp.start()

        def forward():
            for t in range(n):
                for j, chip in enumerate(chips):
                    copy(t, 1 + j, (*chip, c), (x, y, c)).wait_recv()
                    passed(t)[j].start()

        def finish():
            for t in range(n):
                copy(t, 0, (x, y, 1 - c), (x, y, c)).wait_recv()
                for j, chip in enumerate(chips):
                    copy(t, 4 + j, (*chip, 1 - c), (x, y, c)).wait_recv()
            for t in range(n):
                for cp in first(t) + passed(t):
                    cp.wait_send()
                local(t).wait()

        _phase(step, 0, start)
        _phase(step, None if step is None else max(nsteps - 2, 0), forward)
        _phase(step, None if step is None else nsteps - 1, finish)

    return CommPlan(shards, [SDS((N_DEV,) + s.shape, s.dtype) for s in shards],
                    [pltpu.SemaphoreType.DMA((per * n,)), pltpu.SemaphoreType.DMA((per * n,)),
                     pltpu.SemaphoreType.DMA((n,))], emit)


def chips_plan(sums):
    n = len(sums)

    def emit(ins, outs, sems, step, nsteps):
        send_sems, recv_sems = sems
        x, y, c = _coords()

        def copies():
            return [pltpu.make_async_remote_copy(
                src_ref=ins[t].at[j - 1], dst_ref=outs[t].at[j - 1], send_sem=send_sems.at[3 * t + j - 1],
                recv_sem=recv_sems.at[3 * t + j - 1], device_id=(_flip(x, j & 2), _flip(y, j & 1), c),
                device_id_type=MESH) for t in range(n) for j in range(1, 4)]

        def start():
            for cp in copies():
                cp.start()

        def finish():
            for cp in copies():
                cp.wait_recv()
            for cp in copies():
                cp.wait_send()

        _phase(step, 0, start)
        _phase(step, None if step is None else nsteps - 1, finish)

    return CommPlan(sums, [SDS(s.shape, s.dtype) for s in sums],
                    [pltpu.SemaphoreType.DMA((3 * n,)), pltpu.SemaphoreType.DMA((3 * n,))], emit)


def run_comm(plan, name):
    n, m = len(plan.ins), len(plan.out_shape)

    def body(*refs):
        plan.emit(refs[:n], refs[n:n + m], refs[n + m:], None, 1)

    anyspec = pl.BlockSpec(memory_space=pl.ANY)
    return pl.pallas_call(body, name=name, out_shape=plan.out_shape, in_specs=[anyspec] * n,
                          out_specs=[anyspec] * m, scratch_shapes=plan.scratch)(*plan.ins)


class _Rider:
    def __init__(self, plan):
        self.plan = plan
        anyspec = pl.BlockSpec(memory_space=pl.ANY)
        self.ins = plan.ins if plan else []
        self.in_specs = [anyspec] * len(self.ins)
        self.out_specs = [anyspec] * (len(plan.out_shape) if plan else 0)
        self.out_shape = plan.out_shape if plan else []
        self.scratch = plan.scratch if plan else []

    def split(self, refs, n_in, n_out, n_scratch=0):
        ni, no = len(self.ins), len(self.out_shape)
        own_in = refs[:n_in]
        c_in = refs[n_in:n_in + ni]
        own_out = refs[n_in + ni:n_in + ni + n_out]
        c_out = refs[n_in + ni + n_out:n_in + ni + n_out + no]
        rest = refs[n_in + ni + n_out + no:]
        own_scr, c_scr = rest[:n_scratch], rest[n_scratch:]

        def ride(step, nsteps):
            if self.plan:
                self.plan.emit(c_in, c_out, c_scr, step, nsteps)

        return tuple(own_in) + tuple(own_out) + tuple(own_scr), ride

    def result(self, outs, n_out):
        outs = list(outs) if isinstance(outs, (list, tuple)) else [outs]
        return outs[:n_out], (outs[n_out:] if self.plan else None)

    def sem(self, sem):
        return tuple("arbitrary" for _ in sem) if self.plan else sem


def sibling_plan(grads):
    n = len(grads)

    def emit(ins, outs, sems, step, nsteps):
        send_sems, recv_sems = sems
        x, y, c = _coords()

        def copies():
            return [pltpu.make_async_remote_copy(
                src_ref=ins[t].at[:, 1 - c], dst_ref=outs[t], send_sem=send_sems.at[t],
                recv_sem=recv_sems.at[t], device_id=(x, y, 1 - c), device_id_type=MESH) for t in range(n)]

        def start():
            for cp in copies():
                cp.start()

        def finish():
            for cp in copies():
                cp.wait_recv()
            for cp in copies():
                cp.wait_send()

        _phase(step, 0, start)
        _phase(step, None if step is None else nsteps - 1, finish)

    return CommPlan(grads, [SDS((4,) + g.shape[2:], g.dtype) for g in grads],
                    [pltpu.SemaphoreType.DMA((n,)), pltpu.SemaphoreType.DMA((n,))], emit)


def chip_sum(g42, recv, qc, name):
    _, _, R, C = g42.shape
    tr = _row_tile(R, 256)

    def body(qc_ref, a_ref, b_ref, own_ref, send_ref):
        j = pl.program_id(1)
        s = a_ref[...] + b_ref[...]

        @pl.when(j == 0)
        def _():
            own_ref[...] = s

        @pl.when(j > 0)
        def _():
            send_ref[...] = s.astype(BF)

    gs = pltpu.PrefetchScalarGridSpec(
        num_scalar_prefetch=1, grid=(R // tr, 4),
        in_specs=[pl.BlockSpec((None, None, tr, C), lambda i, j, s: (jnp.bitwise_xor(s[0], j), s[1], i, 0)),
                  pl.BlockSpec((None, tr, C), lambda i, j, s: (jnp.bitwise_xor(s[0], j), i, 0))],
        out_specs=[pl.BlockSpec((tr, C), lambda i, j, s: (i, 0)),
                   pl.BlockSpec((None, tr, C), lambda i, j, s: (jnp.maximum(j - 1, 0), i, 0))])
    return pl.pallas_call(body, name=name, grid_spec=gs, out_shape=[SDS((R, C), F32), SDS((3, R, C), BF)],
                          compiler_params=_cparams(("arbitrary", "arbitrary")))(qc, g42, recv)


def small_sum(g8, name):
    def body(g_ref, o_ref):
        acc = g_ref[0]
        for k in range(1, N_DEV):
            acc = acc + g_ref[k]
        o_ref[...] = acc
    return pl.pallas_call(body, name=name, out_shape=SDS(g8.shape[1:], F32))(g8)


def adamw(w, m, v, parts, name):
    R, C = w.shape
    tr = _row_tile(R, 256)
    npart = len(parts)
    c1 = 1.0 / (1.0 - ADAM_B1 ** ADAM_STEP)
    c2 = 1.0 / (1.0 - ADAM_B2 ** ADAM_STEP)

    def body(*refs):
        w_ref, m_ref, v_ref = refs[:3]
        p_refs = refs[3:3 + npart]
        g_ref, d_ref, nm_ref, nv_ref = refs[3 + npart:]
        g = p_refs[0][...].astype(F32)
        for p in p_refs[1:]:
            g = g + p[...].astype(F32)
        nm = ADAM_B1 * m_ref[...] + (1.0 - ADAM_B1) * g
        nv = ADAM_B2 * v_ref[...] + (1.0 - ADAM_B2) * (g * g)
        mh = nm * c1
        vh = nv * c2
        g_ref[...] = g
        nm_ref[...] = nm
        nv_ref[...] = nv
        d_ref[...] = -ADAM_LR * (mh / (jnp.sqrt(vh) + ADAM_EPS) + ADAM_WD * w_ref[...])

    wspec = pl.BlockSpec((tr, C), lambda i: (i, 0))
    pspecs = [pl.BlockSpec(bs(tr, C), im) for (_, bs, im) in parts]
    outs = pl.pallas_call(
        body, name=name, grid=(R // tr,),
        in_specs=[wspec] * 3 + pspecs, out_specs=[wspec] * 4,
        out_shape=[SDS((R, C), F32)] * 4,
        compiler_params=_cparams(("parallel",)))(w, m, v, *[p[0] for p in parts])
    return outs


def _plain_part(g):
    return (g, lambda tr, C: (tr, C), lambda i: (i, 0))


def _slot_part(g, slot):
    return (g, lambda tr, C: (None, tr, C), lambda i, s=slot: (s, i, 0))


def ada_fwd(c_all, w_cols, b_cols, name):
    Bg, D = c_all.shape
    n = w_cols.shape[1]

    def body(c_ref, w_ref, b_ref, o_ref, s_ref):
        cc = c_ref[...]
        s = cc * _sig(cc)
        s_ref[...] = s
        o_ref[...] = jnp.dot(s, w_ref[...], preferred_element_type=F32, precision=lax.Precision.HIGHEST) + b_ref[...]

    return pl.pallas_call(body, name=name, out_shape=[SDS((Bg, n), F32), SDS((Bg, D), F32)],
                          compiler_params=_cparams())(c_all, w_cols, b_cols)


def ada_bwd(s_all, dmod_cols, dmod_all, name):
    Bg, D = s_all.shape
    n = dmod_cols.shape[1]

    def body(s_ref, dc_ref, da_ref, gw_ref, gb_ref):
        gw_ref[...] = lax.dot_general(s_ref[...], dc_ref[...], (((0,), (0,)), ((), ())),
                                      preferred_element_type=F32, precision=lax.Precision.HIGHEST)
        acc = da_ref[0:1, :]
        for r in range(1, Bg):
            acc = acc + da_ref[r:r + 1, :]
        gb_ref[...] = acc

    return pl.pallas_call(body, name=name, out_shape=[SDS((D, n), F32), SDS((1, dmod_all.shape[1]), F32)],
                          compiler_params=_cparams())(s_all, dmod_cols, dmod_all)


def _vec(D, rank):
    return pl.BlockSpec((1, D), (lambda i: (0, 0)) if rank == 1 else (lambda i, j: (0, 0)))


def _modspec(D, tpb, rank):
    if rank == 1:
        return pl.BlockSpec((None, 1, D), lambda i: (i // tpb, 0, 0))
    return pl.BlockSpec((None, 1, D), lambda i, j: (i // tpb, 0, 0))


def _resident(shape):
    return pl.BlockSpec(shape, lambda *_: (0,) * len(shape), pipeline_mode=pl.Buffered(1))


def ffn_up(xs, pg, pb, sc, sh, wg8, name, comm=None):
    T, D = xs.shape
    n2, _, nb = wg8.shape
    nj = n2 // 2
    S = T // sc.shape[0]
    tm = min(512, S)
    tpb = S // tm
    rider = _Rider(comm)

    def body(*refs):
        (x_ref, pg_ref, pb_ref, sc_ref, sh_ref, w_ref, u_ref, a_ref, g_ref, p_ref), ride = rider.split(refs, 6, 4)
        ride(pl.program_id(0), T // tm)
        xin = x_ref[...] * pg_ref[...] + pb_ref[...]
        u_ref[...] = (xin * (1.0 + sc_ref[...]) + sh_ref[...]).astype(BF)

        def col_block(j, _):
            u = u_ref[...]
            a = _dot_nn(u, w_ref[j])
            g = _dot_nn(u, w_ref[j + nj])
            s = _sig(a)
            silu = a * s
            a_ref[j] = (g * (s * (1.0 + a * (1.0 - s)))).astype(BF)
            g_ref[j] = silu.astype(BF)
            p_ref[j] = (silu * g).astype(BF)
            return 0

        lax.fori_loop(0, nj, col_block, 0)

    blk = pl.BlockSpec((nj, tm, nb), lambda i: (0, i, 0))
    row = pl.BlockSpec((tm, D), lambda i: (i, 0))
    outs = pl.pallas_call(
        body, name=name, grid=(T // tm,),
        in_specs=[row, _vec(D, 1), _vec(D, 1), _modspec(D, tpb, 1), _modspec(D, tpb, 1), _resident(wg8.shape)]
        + rider.in_specs,
        out_specs=[row, blk, blk, blk] + rider.out_specs,
        out_shape=[SDS((T, D), BF)] + [SDS((nj, T, nb), BF)] * 3 + rider.out_shape,
        scratch_shapes=rider.scratch,
        compiler_params=_cparams(rider.sem(("parallel",))))(xs, pg, pb, sc, sh, wg8, *rider.ins)
    return rider.result(outs, 4)


def ffn_down_ln(p4, wd3, xs, pg, pb, gate, name):
    nj, T, nb = p4.shape
    D = wd3.shape[2]
    S = T // gate.shape[0]
    tm = min(512, S)
    tpb = S // tm

    def body(p_ref, wd_ref, x_ref, pg_ref, pb_ref, gate_ref, xh_ref, rs_ref, f_ref):
        f = _dot_nn(p_ref[0], wd_ref[0])
        for k in range(1, nj):
            f = f + _dot_nn(p_ref[k], wd_ref[k])
        xin = x_ref[...] * pg_ref[...] + pb_ref[...]
        r = DEEPNORM_ALPHA * xin + gate_ref[...] * (MACARON_WEIGHT * f)
        xh, rstd = _ln_stats(r)
        xh_ref[...] = xh
        rs_ref[...] = rstd
        f_ref[...] = f.astype(BF)

    row = pl.BlockSpec((tm, D), lambda i: (i, 0))
    return pl.pallas_call(
        body, name=name, grid=(T // tm,),
        in_specs=[pl.BlockSpec((nj, tm, nb), lambda i: (0, i, 0)), _resident(wd3.shape),
                  row, _vec(D, 1), _vec(D, 1), _modspec(D, tpb, 1)],
        out_specs=[row, pl.BlockSpec((tm, 1), lambda i: (i, 0)), row],
        out_shape=[SDS((T, D), F32), SDS((T, 1), F32), SDS((T, D), BF)],
        compiler_params=_cparams(("parallel",)))(p4, wd3, xs, pg, pb, gate)


N_QKV = 3


def mod_matmul(xs, pg, pb, sc, sh, w, name, comm=None):
    T, D = xs.shape
    N = w.shape[1]
    S = T // sc.shape[0]
    tm = min(256, S)
    tpb = S // tm
    rider = _Rider(comm)

    def body(*refs):
        (x_ref, pg_ref, pb_ref, sc_ref, sh_ref, w_ref, u_ref, qkv_ref, o_ref), ride = rider.split(refs, 6, 3)
        ride(pl.program_id(0), T // tm)
        xin = x_ref[...] * pg_ref[...] + pb_ref[...]
        u = (xin * (1.0 + sc_ref[...]) + sh_ref[...]).astype(BF)
        u_ref[...] = u
        for n in range(N // D):
            y = _dot_nn(u, w_ref[:, n * D:(n + 1) * D])
            if n < N_QKV:
                qkv_ref[:, n * D:(n + 1) * D] = y.astype(BF)
            else:
                o_ref[:, (n - N_QKV) * D:(n - N_QKV + 1) * D] = y

    row = pl.BlockSpec((tm, D), lambda i: (i, 0))
    outs = pl.pallas_call(
        body, name=name, grid=(T // tm,),
        in_specs=[row, _vec(D, 1), _vec(D, 1), _modspec(D, tpb, 1), _modspec(D, tpb, 1), _resident(w.shape)]
        + rider.in_specs,
        out_specs=[row, pl.BlockSpec((tm, N_QKV * D), lambda i: (i, 0)),
                   pl.BlockSpec((tm, N - N_QKV * D), lambda i: (i, 0))] + rider.out_specs,
        out_shape=[SDS((T, D), BF), SDS((T, N_QKV * D), BF), SDS((T, N - N_QKV * D), F32)] + rider.out_shape,
        scratch_shapes=rider.scratch,
        compiler_params=_cparams(rider.sem(("parallel",))))(xs, pg, pb, sc, sh, w, *rider.ins)
    return rider.result(outs, 3)


ATT_TQ = 1024
ATT_TK = 256


def _att_consts(tk):
    r = lax.broadcasted_iota(jnp.int32, (tk + 8, tk), 0)
    c = lax.broadcasted_iota(jnp.int32, (tk + 8, tk), 1)
    usum = jnp.where((r >= tk) | (c > r), 1.0, 0.0).astype(BF)
    lsum = jnp.where((r >= tk) | (c < r), 1.0, 0.0).astype(BF)
    dmask = lax.broadcasted_iota(jnp.int32, (tk, tk), 0) < lax.broadcasted_iota(jnp.int32, (tk, tk), 1)
    return usum, lsum, dmask


def _split_dot(m, v):
    hi = v.astype(BF)
    lo = (v - hi.astype(F32)).astype(BF)
    return _dot_nn(m, hi) + _dot_nn(m, lo)


def _softplus(z):
    return jnp.maximum(z, 0.0) + jnp.log(1.0 + jnp.exp(-jnp.abs(z)))


def _att_dims(S, D):
    dh = SB_HEAD_DIM
    cw = min(LANE, D)
    tq = min(ATT_TQ, S)
    tk = min(ATT_TK, tq)
    assert tq % tk == 0 and S % tq == 0
    return dh, cw, cw // dh, D // cw, tq, tk, S // tq, S // tk


def att_fwd(proj, Bl, S, D, name):
    dh, cw, hp, nblk, tq, tk, nq, nk = _att_dims(S, D)
    scale = 1.0 / math.sqrt(dh)
    assert math.log2(scale) == int(math.log2(scale))
    H = D // dh

    def body(q_ref, k_ref, v_ref, o_ref, car_ref, qs, ks, vts):
        usum, _, dmask = _att_consts(tk)
        for hh in range(hp):
            sl = slice(hh * dh, (hh + 1) * dh)
            qs[hh] = (q_ref[:, sl] * scale).astype(BF)
            ks[hh] = k_ref[:, sl].astype(BF)
            for kb in range(nk):
                vts[hh, kb] = v_ref[kb * tk:(kb + 1) * tk, sl].astype(F32).T.astype(BF)
        nch = tq // tk

        def qloop(qb, _):
            qo = pl.multiple_of(qb * tq, tq)
            n_full = qb * nch

            def blk(kb, state, diag):
                ko = pl.multiple_of(kb * tk, tk)
                chains = [(hh, c) for hh in range(hp) for c in range(0 if diag is None else diag, nch)]

                def masked(ch, val):
                    return jnp.where(dmask, val, 0.0) if ch[1] == diag else val

                z = {ch: _dot_nt(ks[ch[0], pl.ds(ko, tk), :], qs[ch[0], pl.ds(pl.multiple_of(qo + ch[1] * tk, tk), tk), :])
                     for ch in chains}
                sp = {ch: _softplus(z[ch]) for ch in chains}
                lk = {ch: masked(ch, -sp[ch]) for ch in chains}
                for hh, c in chains:
                    car_ref[hh, qb * nk + kb, :, c * tk:(c + 1) * tk] = state[hh][c][0]
                cs = {ch: _split_dot(usum, lk[ch]) for ch in chains}
                w = {ch: masked(ch, jnp.exp((z[ch] - sp[ch]) + state[ch[0]][ch[1]][0][0:1, :] + cs[ch][:tk]))
                     for ch in chains}
                pv = {ch: _dot_nn(vts[ch[0], kb], w[ch].astype(BF)) for ch in chains}
                return tuple(tuple(
                    (state[hh][c][0] + cs[(hh, c)][tk:], state[hh][c][1] + pv[(hh, c)]) if (hh, c) in z else state[hh][c]
                    for c in range(nch)) for hh in range(hp))

            state = tuple(tuple((jnp.zeros((8, tk), F32), jnp.zeros((dh, tk), F32)) for _ in range(nch))
                          for _ in range(hp))
            for i in reversed(range(nch)):
                state = blk(n_full + i, state, i)
            state = lax.fori_loop(0, n_full, lambda j, st: blk(n_full - 1 - j, st, None), state)
            for hh in range(hp):
                for c in range(nch):
                    o_ref[pl.ds(pl.multiple_of(qo + c * tk, tk), tk), hh * dh:(hh + 1) * dh] = (
                        state[hh][c][1].T.astype(BF))
            return 0

        lax.fori_loop(0, nq, qloop, 0)

    def seg(s):
        return pl.BlockSpec((S, cw), lambda b, h: (b, s * nblk + h))

    return pl.pallas_call(
        body, name=name, grid=(Bl, nblk),
        in_specs=[seg(0), seg(1), seg(2)],
        out_specs=[pl.BlockSpec((S, cw), lambda b, h: (b, h)),
                   pl.BlockSpec((None, hp, nq * nk, 8, tq), lambda b, h: (b, h, 0, 0, 0))],
        out_shape=[SDS((Bl * S, D), BF), SDS((Bl, H, nq * nk, 8, tq), F32)],
        scratch_shapes=[pltpu.VMEM((hp, S, dh), BF)] * 2 + [pltpu.VMEM((hp, nk, dh, tk), BF)],
        compiler_params=_cparams(("parallel", "parallel")))(proj, proj, proj)


def conv_fwd(proj, cw32, cb, cg, cbeta, Bl, S, D, kw, name):
    T = Bl * S
    ts = min(128, S)
    ns = S // ts
    off = HALO - (kw - 1)
    rc = min(64, ts)
    cw = min(LANE, D)

    def body(a_ref, b_ref, ha_ref, hb_ref, w_ref, cb_ref, g_ref, be_ref, cs_ref, xh_ref, rs_ref, hsh, conv_s):
        i = pl.program_id(1)
        h = a_ref[...] * _sig(b_ref[...])
        hh = jnp.where(i == 0, 0.0, ha_ref[...] * _sig(hb_ref[...]))
        for cb_ in range(D // cw):
            cols = slice(cb_ * cw, (cb_ + 1) * cw)
            hsh[0, cb_, pl.ds(HALO, ts), :] = h[:, cols]
            hsh[0, cb_, pl.ds(0, HALO), :] = hh[:, cols]
            for s in range(1, SUBLANES):
                hsh[s, cb_, pl.ds(0, ts + HALO - SUBLANES), :] = hsh[0, cb_, pl.ds(s, ts + HALO - SUBLANES), :]
            accs = [jnp.zeros((rc, cw), F32) for _ in range(ts // rc)]
            for k in range(kw):
                wk = w_ref[k:k + 1, cols]
                s = (off + k) % SUBLANES
                for r in range(ts // rc):
                    accs[r] = accs[r] + wk * hsh[s, cb_, pl.ds(r * rc + off + k - s, rc), :]
            for r in range(ts // rc):
                conv_s[pl.ds(r * rc, rc), cols] = accs[r]
        conv = conv_s[...] + cb_ref[...]
        xh, rstd = _ln_stats(conv)
        xh_ref[...] = xh
        rs_ref[...] = rstd
        cl = xh * g_ref[...] + be_ref[...]
        cs_ref[...] = (cl * _sig(cl)).astype(BF)

    hpb = ts // HALO

    def tile(seg):
        return pl.BlockSpec((ts, D), lambda b, i: (b * ns + i, seg))

    def halo(seg):
        return pl.BlockSpec((HALO, D), lambda b, i: (jnp.maximum((b * ns + i) * hpb - 1, 0), seg))

    row = pl.BlockSpec((ts, D), lambda b, i: (b * ns + i, 0))
    vec = pl.BlockSpec((1, D), lambda b, i: (0, 0))
    return pl.pallas_call(
        body, name=name, grid=(Bl, ns),
        in_specs=[tile(0), tile(1), halo(0), halo(1), pl.BlockSpec((HALO, D), lambda b, i: (0, 0)), vec, vec, vec],
        out_specs=[row, row, pl.BlockSpec((ts, 1), lambda b, i: (b * ns + i, 0))],
        out_shape=[SDS((T, D), BF), SDS((T, D), F32), SDS((T, 1), F32)],
        scratch_shapes=[pltpu.VMEM((SUBLANES, D // cw, ts + HALO, cw), F32), pltpu.VMEM((ts, D), F32)],
        compiler_params=_cparams(("parallel", "arbitrary")))(proj, proj, proj, proj, cw32, cb, cg, cbeta)


def mix_fwd(yatt, cs, proj, wsb, wco, wout, xs, pg, pb, gate, name):
    T, D = yatt.shape
    S = T // gate.shape[0]
    tm = min(256, S)
    tpb = S // tm

    def body(ya_ref, cs_ref, ga_ref, gb_ref, wsb_ref, wco_ref, wout_ref, x_ref, pg_ref, pb_ref, gate_ref,
             xh_ref, rs_ref, ysb_ref, yco_ref, mg_ref, o_ref):
        ysb = _dot_nn(ya_ref[...], wsb_ref[...])
        yco = _dot_nn(cs_ref[...], wco_ref[...])
        merged = _sig(ga_ref[...]) * ysb + _sig(gb_ref[...]) * yco
        mg = merged.astype(BF)
        o = _dot_nn(mg, wout_ref[...])
        xin = x_ref[...] * pg_ref[...] + pb_ref[...]
        r = DEEPNORM_ALPHA * xin + gate_ref[...] * o
        xh, rstd = _ln_stats(r)
        xh_ref[...] = xh
        rs_ref[...] = rstd
        ysb_ref[...] = ysb.astype(BF)
        yco_ref[...] = yco.astype(BF)
        mg_ref[...] = mg
        o_ref[...] = o.astype(BF)

    row = pl.BlockSpec((tm, D), lambda i: (i, 0))
    wfull = pl.BlockSpec((D, D), lambda i: (0, 0))
    return pl.pallas_call(
        body, name=name, grid=(T // tm,),
        in_specs=[row, row, pl.BlockSpec((tm, D), lambda i: (i, 2)), pl.BlockSpec((tm, D), lambda i: (i, 3)),
                  wfull, wfull, wfull, row, _vec(D, 1), _vec(D, 1), _modspec(D, tpb, 1)],
        out_specs=[row, pl.BlockSpec((tm, 1), lambda i: (i, 0)), row, row, row, row],
        out_shape=[SDS((T, D), F32), SDS((T, 1), F32)] + [SDS((T, D), BF)] * 4,
        compiler_params=_cparams(("parallel",)))(yatt, cs, proj, proj, wsb, wco, wout, xs, pg, pb, gate)


def ln_bwd(dout, xh, rstd, lng, lnb, gate, sub, res_w, name, target=None, swiglu=None):
    T, D = xh.shape
    Bl = gate.shape[0]
    S = T // Bl
    tm = min(256, S)
    tpb = S // tm
    first = target is not None
    n_in = 10 if swiglu else 7

    def body(*refs):
        do_ref, xh_ref, rs_ref, g_ref, b_ref, gate_ref, sub_ref = refs[:7]
        tg_ref = do_ref
        outs = refs[n_in:]
        dr_ref, ds_ref, dg_ref, db_ref, dgate_ref = outs[:5]
        loss_ref = outs[5] if first else None
        i = pl.program_id(0)
        xh_ = xh_ref[...]
        if first:
            diff = (xh_ * g_ref[...] + b_ref[...]) - tg_ref[...]
            lsum = jnp.sum(jnp.sum(diff * diff, axis=1, keepdims=True), axis=0, keepdims=True) * (0.5 / D)
            do = diff * (1.0 / D)
        else:
            do = do_ref[...]

        @pl.when(i == 0)
        def _():
            dg_ref[...] = jnp.zeros_like(dg_ref)
            db_ref[...] = jnp.zeros_like(db_ref)
            if first:
                loss_ref[...] = jnp.zeros_like(loss_ref)

        @pl.when(i % tpb == 0)
        def _():
            dgate_ref[...] = jnp.zeros_like(dgate_ref)

        if first:
            loss_ref[...] += jnp.broadcast_to(lsum, loss_ref.shape)
        dg_ref[...] += _rowsum(do * xh_)
        db_ref[...] += _rowsum(do)
        dr = _ln_bwd(do * g_ref[...], xh_, rs_ref[...])
        dr_ref[...] = dr
        ds_ref[...] = (dr * gate_ref[...] * res_w).astype(BF)
        dgate_ref[...] += _rowsum(dr * (res_w * sub_ref[...].astype(F32)))
        if swiglu:
            wd_ref, a_ref, gg_ref = refs[7:10]
            dh_ref = outs[-1]

            def col_block(j, _):
                dp = _dot_nt(ds_ref[...], wd_ref[j])
                dh_ref[0, j] = (dp * a_ref[j].astype(F32)).astype(BF)
                dh_ref[1, j] = (dp * gg_ref[j].astype(F32)).astype(BF)
                return 0

            lax.fori_loop(0, swiglu[1].shape[0], col_block, 0)

    row = pl.BlockSpec((tm, D), lambda i: (i, 0))
    vec = _vec(D, 1)
    mod = _modspec(D, tpb, 1)
    in_specs = [row, row, pl.BlockSpec((tm, 1), lambda i: (i, 0)), vec, vec, mod, row]
    out_specs = [row, row, vec, vec, mod]
    out_shape = [SDS((T, D), F32), SDS((T, D), BF), SDS((1, D), F32), SDS((1, D), F32), SDS((Bl, 1, D), F32)]
    if first:
        out_specs.append(pl.BlockSpec((8, LANE), lambda i: (0, 0)))
        out_shape.append(SDS((8, LANE), F32))
    extra = []
    if swiglu:
        wd3, a4, g4 = swiglu
        nj, _, nb = a4.shape
        blk = pl.BlockSpec((nj, tm, nb), lambda i: (0, i, 0))
        in_specs += [_resident(wd3.shape), blk, blk]
        out_specs.append(pl.BlockSpec((2, nj, tm, nb), lambda i: (0, 0, i, 0)))
        out_shape.append(SDS((2, nj, T, nb), BF))
        extra = [wd3, a4, g4]
    return pl.pallas_call(
        body, name=name, grid=(T // tm,), in_specs=in_specs, out_specs=out_specs, out_shape=out_shape,
        compiler_params=_cparams(("arbitrary",)))(target if first else dout, xh, rstd, lng, lnb, gate, sub, *extra)


def mod_bwd(dh, w, dr, xs, pg, pb, sc, blocked, name, comm=None):
    T, D = dr.shape
    Bl = sc.shape[0]
    S = T // Bl
    tm = min(512 if blocked else 256, S)
    tpb = S // tm
    row = pl.BlockSpec((tm, D), lambda i: (i, 0))
    if blocked:
        nk, _, kb = dh.shape
        dh_list, dh_specs = [dh], [pl.BlockSpec((nk, tm, kb), lambda i: (0, i, 0))]
    else:
        nk = len(dh)
        dh_list, dh_specs = list(dh), [row] * nk
    nd = len(dh_list)
    rider = _Rider(comm)

    def body(*refs):
        own, ride = rider.split(refs, nd + 6, 3)
        dh_refs = own[:nd]
        w_ref, dr_ref, x_ref, pg_ref, pb_ref, sc_ref, dx_ref, dsc_ref, dsh_ref = own[nd:]
        i = pl.program_id(0)
        ride(i, T // tm)

        def part(k):
            if blocked:
                return _dot_nt(dh_refs[0][k], w_ref[k])
            return _dot_nt(dh_refs[k][...], w_ref[:, k * D:(k + 1) * D])

        du = part(0)
        for k in range(1, nk):
            du = du + part(k)

        @pl.when(i % tpb == 0)
        def _():
            dsc_ref[...] = jnp.zeros_like(dsc_ref)
            dsh_ref[...] = jnp.zeros_like(dsh_ref)

        xin = x_ref[...] * pg_ref[...] + pb_ref[...]
        dx_ref[...] = DEEPNORM_ALPHA * dr_ref[...] + du * (1.0 + sc_ref[...])
        dsc_ref[...] += _rowsum(du * xin)
        dsh_ref[...] += _rowsum(du)

    mod = _modspec(D, tpb, 1)
    outs = pl.pallas_call(
        body, name=name, grid=(T // tm,),
        in_specs=dh_specs + [_resident(w.shape), row, row, _vec(D, 1), _vec(D, 1), mod] + rider.in_specs,
        out_specs=[row, mod, mod] + rider.out_specs,
        out_shape=[SDS((T, D), F32), SDS((Bl, 1, D), F32), SDS((Bl, 1, D), F32)] + rider.out_shape,
        scratch_shapes=rider.scratch,
        compiler_params=_cparams(("arbitrary",)))(*dh_list, w, dr, xs, pg, pb, sc, *rider.ins)
    return rider.result(outs, 3)


def merge_bwd(do2, proj, ysb, yco, wsb, wco, wout, name):
    T, D = do2.shape
    tm = min(256, T)

    def body(do_ref, ga_ref, gb_ref, ysb_ref, yco_ref, wsb_ref, wco_ref, wout_ref,
             dysb_ref, dyco_ref, dga_ref, dgb_ref, dya_ref, dcs_ref):
        dm = _dot_nt(do_ref[...], wout_ref[...])
        sa = _sig(ga_ref[...])
        sb = _sig(gb_ref[...])
        dysb = (dm * sa).astype(BF)
        dyco = (dm * sb).astype(BF)
        dysb_ref[...] = dysb
        dyco_ref[...] = dyco
        dga_ref[...] = (dm * ysb_ref[...].astype(F32) * (sa * (1.0 - sa))).astype(BF)
        dgb_ref[...] = (dm * yco_ref[...].astype(F32) * (sb * (1.0 - sb))).astype(BF)
        dya_ref[...] = _dot_nt(dysb, wsb_ref[...]).astype(BF)
        dcs_ref[...] = _dot_nt(dyco, wco_ref[...])

    row = pl.BlockSpec((tm, D), lambda i: (i, 0))
    wfull = pl.BlockSpec((D, D), lambda i: (0, 0))
    return pl.pallas_call(
        body, name=name, grid=(T // tm,),
        in_specs=[row, pl.BlockSpec((tm, D), lambda i: (i, 2)), pl.BlockSpec((tm, D), lambda i: (i, 3)),
                  row, row, wfull, wfull, wfull],
        out_specs=[row] * 6,
        out_shape=[SDS((T, D), BF)] * 5 + [SDS((T, D), F32)],
        compiler_params=_cparams(("parallel",)))(do2, proj, proj, ysb, yco, wsb, wco, wout)


def att_bwd(proj, dyatt, car, Bl, S, D, name, comm=None):
    dh, cw, hp, nblk, tq, tk, nq, nk = _att_dims(S, D)
    scale = 1.0 / math.sqrt(dh)
    rider = _Rider(comm)

    def body(*refs):
        (q_ref, k_ref, v_ref, do_ref, car_ref, dq_ref, dk_ref, dv_ref,
         qs, ks, vs, dos, kts, dk_acc, dv_acc), ride = rider.split(refs, 5, 3, 7)
        ride(pl.program_id(0) * nblk + pl.program_id(1), Bl * nblk)
        usum, lsum, dmask = _att_consts(tk)
        for hh in range(hp):
            sl = slice(hh * dh, (hh + 1) * dh)
            qs[hh] = (q_ref[:, sl] * scale).astype(BF)
            ks[hh] = k_ref[:, sl].astype(BF)
            vs[hh] = v_ref[:, sl].astype(BF)
            dos[hh] = do_ref[:, sl]
            for kb in range(nk):
                kts[hh, kb] = k_ref[kb * tk:(kb + 1) * tk, sl].astype(F32).T.astype(BF)
        dk_acc[...] = jnp.zeros_like(dk_acc)
        dv_acc[...] = jnp.zeros_like(dv_acc)
        nch = tq // tk

        def qloop(qb, _):
            qo = pl.multiple_of(qb * tq, tq)
            n_full = qb * nch

            def blk(kb, state, diag):
                ko = pl.multiple_of(kb * tk, tk)
                chains = [(hh, c) for hh in range(hp) for c in range(0 if diag is None else diag, nch)]

                def masked(ch, val):
                    return jnp.where(dmask, val, 0.0) if ch[1] == diag else val

                def qrows(ref, ch):
                    return ref[ch[0], pl.ds(pl.multiple_of(qo + ch[1] * tk, tk), tk), :]

                k = [ks[hh, pl.ds(ko, tk), :] for hh in range(hp)]
                v = [vs[hh, pl.ds(ko, tk), :] for hh in range(hp)]
                z = {ch: _dot_nt(k[ch[0]], qrows(qs, ch)) for ch in chains}
                dw = {ch: _dot_nt(v[ch[0]], qrows(dos, ch)) for ch in chains}
                sp = {ch: _softplus(z[ch]) for ch in chains}
                lk = {ch: masked(ch, -sp[ch]) for ch in chains}
                cs = {ch: _split_dot(usum, lk[ch]) for ch in chains}
                w = {ch: masked(ch, jnp.exp((z[ch] - sp[ch])
                                            + car_ref[ch[0], qb * nk + kb, 0:1, ch[1] * tk:(ch[1] + 1) * tk]
                                            + cs[ch][:tk])) for ch in chains}
                dlw = {ch: dw[ch] * w[ch] for ch in chains}
                gs = {ch: _split_dot(lsum, dlw[ch]) for ch in chains}
                sg = {ch: jnp.exp(z[ch] - sp[ch]) for ch in chains}
                dzb = {ch: masked(ch, dlw[ch] - sg[ch] * (dlw[ch] + state[ch[0]][ch[1]][0][0:1, :] + gs[ch][:tk])
                                  ).astype(BF) for ch in chains}
                wb = {ch: w[ch].astype(BF) for ch in chains}
                for hh in range(hp):
                    mine = [ch for ch in chains if ch[0] == hh]
                    dk_acc[hh, kb] += sum(_dot_nn(dzb[ch], qrows(qs, ch)) for ch in mine)
                    dv_acc[hh, kb] += sum(_dot_nn(wb[ch], qrows(dos, ch)) for ch in mine)
                dq = {ch: _dot_nn(kts[ch[0], kb], dzb[ch]) for ch in chains}
                return tuple(tuple(
                    (state[hh][c][0] + gs[(hh, c)][tk:], state[hh][c][1] + dq[(hh, c)]) if (hh, c) in z else state[hh][c]
                    for c in range(nch)) for hh in range(hp))

            state = tuple(tuple((jnp.zeros((8, tk), F32), jnp.zeros((dh, tk), F32)) for _ in range(nch))
                          for _ in range(hp))
            state = lax.fori_loop(0, n_full, lambda kb, st: blk(kb, st, None), state)
            for i in range(nch):
                state = blk(n_full + i, state, i)
            for hh in range(hp):
                for c in range(nch):
                    dq_ref[pl.ds(pl.multiple_of(qo + c * tk, tk), tk), hh * dh:(hh + 1) * dh] = (
                        (state[hh][c][1].T * scale).astype(BF))
            return 0

        lax.fori_loop(0, nq, qloop, 0)
        for hh in range(hp):
            sl = slice(hh * dh, (hh + 1) * dh)
            for kb in range(nk):
                dk_ref[kb * tk:(kb + 1) * tk, sl] = dk_acc[hh, kb].astype(BF)
                dv_ref[kb * tk:(kb + 1) * tk, sl] = dv_acc[hh, kb].astype(BF)

    def seg(s):
        return pl.BlockSpec((S, cw), lambda b, h: (b, s * nblk + h))

    blk_spec = pl.BlockSpec((S, cw), lambda b, h: (b, h))
    outs = pl.pallas_call(
        body, name=name, grid=(Bl, nblk),
        in_specs=[seg(0), seg(1), seg(2), blk_spec,
                  pl.BlockSpec((None, hp, nq * nk, 8, tq), lambda b, h: (b, h, 0, 0, 0))] + rider.in_specs,
        out_specs=[blk_spec, blk_spec, blk_spec] + rider.out_specs,
        out_shape=[SDS((Bl * S, D), BF)] * 3 + rider.out_shape,
        scratch_shapes=[pltpu.VMEM((hp, S, dh), BF)] * 4 + [pltpu.VMEM((hp, nk, dh, tk), BF)]
        + [pltpu.VMEM((hp, nk, tk, dh), F32)] * 2 + rider.scratch,
        compiler_params=_cparams(rider.sem(("parallel", "parallel"))))(proj, proj, proj, dyatt, car, *rider.ins)
    return rider.result(outs, 3)


def conv_ln_bwd(dcs, xhc, rstd_c, cg, cbeta, name):
    T, D = dcs.shape
    tm = min(256, T)

    def body(dcs_ref, xh_ref, rs_ref, g_ref, b_ref, dconv_ref, dg_ref, db_ref, dcb_ref):
        @pl.when(pl.program_id(0) == 0)
        def _():
            dg_ref[...] = jnp.zeros_like(dg_ref)
            db_ref[...] = jnp.zeros_like(db_ref)
            dcb_ref[...] = jnp.zeros_like(dcb_ref)
        xh = xh_ref[...]
        cl = xh * g_ref[...] + b_ref[...]
        s = _sig(cl)
        dcl = dcs_ref[...] * (s * (1.0 + cl * (1.0 - s)))
        dg_ref[...] += _rowsum(dcl * xh)
        db_ref[...] += _rowsum(dcl)
        dconv = _ln_bwd(dcl * g_ref[...], xh, rs_ref[...])
        dconv_ref[...] = dconv
        dcb_ref[...] += _rowsum(dconv)

    row = pl.BlockSpec((tm, D), lambda i: (i, 0))
    vec = _vec(D, 1)
    return pl.pallas_call(
        body, name=name, grid=(T // tm,),
        in_specs=[row, row, pl.BlockSpec((tm, 1), lambda i: (i, 0)), vec, vec],
        out_specs=[row, vec, vec, vec],
        out_shape=[SDS((T, D), F32)] + [SDS((1, D), F32)] * 3,
        compiler_params=_cparams(("arbitrary",)))(dcs, xhc, rstd_c, cg, cbeta)


def conv_bwd(dconv, proj, cw32, Bl, S, D, kw, name):
    T = Bl * S
    ts = min(128, S)
    ns = S // ts
    off = HALO - (kw - 1)
    rc = min(64, ts)
    cw = min(LANE, D)
    hpb = ts // HALO
    nhb = T // HALO

    def body(dc_ref, dcn_ref, a_ref, b_ref, ha_ref, hb_ref, w_ref, da_ref, db_ref, dw_ref, hsh, dsh, dh_s):
        b_ = pl.program_id(0)
        i = pl.program_id(1)
        span = ts + HALO - SUBLANES

        @pl.when((b_ == 0) & (i == 0))
        def _():
            dw_ref[...] = jnp.zeros_like(dw_ref)
        a = a_ref[...]
        sb = _sig(b_ref[...])
        h = a * sb
        hh = jnp.where(i == 0, 0.0, ha_ref[...] * _sig(hb_ref[...]))
        dc = dc_ref[...]
        dcn = jnp.where(i == ns - 1, 0.0, dcn_ref[...])
        for cb_ in range(D // cw):
            cols = slice(cb_ * cw, (cb_ + 1) * cw)
            hsh[0, cb_, pl.ds(HALO, ts), :] = h[:, cols]
            hsh[0, cb_, pl.ds(0, HALO), :] = hh[:, cols]
            dsh[0, cb_, pl.ds(0, ts), :] = dc[:, cols]
            dsh[0, cb_, pl.ds(ts, HALO), :] = dcn[:, cols]
            for s in range(1, SUBLANES):
                hsh[s, cb_, pl.ds(0, span), :] = hsh[0, cb_, pl.ds(s, span), :]
                dsh[s, cb_, pl.ds(0, span), :] = dsh[0, cb_, pl.ds(s, span), :]
            accs = [jnp.zeros((rc, cw), F32) for _ in range(ts // rc)]
            d0 = [dsh[0, cb_, pl.ds(r * rc, rc), :] for r in range(ts // rc)]
            for k in range(kw):
                wk = w_ref[k:k + 1, cols]
                wsum = jnp.zeros((rc, cw), F32)
                sd = ((kw - 1) - k) % SUBLANES
                sh_ = (off + k) % SUBLANES
                for r in range(ts // rc):
                    accs[r] = accs[r] + wk * dsh[sd, cb_, pl.ds(r * rc + (kw - 1) - k - sd, rc), :]
                    wsum = wsum + d0[r] * hsh[sh_, cb_, pl.ds(r * rc + off + k - sh_, rc), :]
                dw_ref[k:k + 1, cols] += _rowsum(wsum)
            for r in range(ts // rc):
                dh_s[pl.ds(r * rc, rc), cols] = accs[r]
        dhc = dh_s[...]
        da_ref[...] = (dhc * sb).astype(BF)
        db_ref[...] = (dhc * a * (sb * (1.0 - sb))).astype(BF)

    def tile(seg):
        return pl.BlockSpec((ts, D), lambda b, i: (b * ns + i, seg))

    def halo(seg):
        return pl.BlockSpec((HALO, D), lambda b, i: (jnp.maximum((b * ns + i) * hpb - 1, 0), seg))

    row = pl.BlockSpec((ts, D), lambda b, i: (b * ns + i, 0))
    nxt = pl.BlockSpec((HALO, D), lambda b, i: (jnp.minimum((b * ns + i + 1) * hpb, nhb - 1), 0))
    wspec = pl.BlockSpec((HALO, D), lambda b, i: (0, 0))
    return pl.pallas_call(
        body, name=name, grid=(Bl, ns),
        in_specs=[row, nxt, tile(0), tile(1), halo(0), halo(1), wspec],
        out_specs=[row, row, wspec],
        out_shape=[SDS((T, D), BF), SDS((T, D), BF), SDS((HALO, D), F32)],
        scratch_shapes=[pltpu.VMEM((SUBLANES, D // cw, ts + HALO, cw), F32)] * 2 + [pltpu.VMEM((ts, D), F32)],
        compiler_params=_cparams(("arbitrary", "arbitrary")))(dconv, dconv, proj, proj, proj, proj, cw32)


def matmul_tn(xa, ga, x_spec, g_spec, out_shape, out_spec, acc_shape, grid, name, comm=None):
    nk = grid[-1]
    rider = _Rider(comm)

    def body(*refs):
        (x_ref, g_ref, o_ref, acc), ride = rider.split(refs, 2, 1, 1)
        k = pl.program_id(1)
        ride(pl.program_id(0) * nk + k, grid[0] * nk)

        @pl.when(k == 0)
        def _():
            acc[...] = jnp.zeros_like(acc)
        acc[...] += _dot_tn(x_ref[...], g_ref[...])

        @pl.when(k == nk - 1)
        def _():
            o_ref[...] = acc[...]

    outs = pl.pallas_call(
        body, name=name, grid=grid, in_specs=[x_spec, g_spec] + rider.in_specs,
        out_specs=[out_spec] + rider.out_specs,
        out_shape=[SDS(out_shape, F32)] + rider.out_shape,
        scratch_shapes=[pltpu.VMEM(acc_shape, F32)] + rider.scratch,
        compiler_params=_cparams(rider.sem(("parallel", "arbitrary"))))(xa, ga, *rider.ins)
    own, landed = rider.result(outs, 1)
    return own[0] if comm is None else (own[0], landed)


def wgrad_std(xa, ga, name):
    T, M = xa.shape
    N = ga.shape[1]
    tk = min(2048, T)
    return matmul_tn(xa, ga, pl.BlockSpec((tk, M), lambda n, k: (k, 0)), pl.BlockSpec((tk, N), lambda n, k: (k, 0)),
                     (M, N), pl.BlockSpec((M, N), lambda n, k: (0, 0)), (M, N), (1, T // tk), name)


def wgrad_down(p4, df, name):
    nj, T, nb = p4.shape
    D = df.shape[1]
    tk = min(2048, T)
    return matmul_tn(p4, df, pl.BlockSpec((None, tk, nb), lambda j, k: (j, k, 0)),
                     pl.BlockSpec((tk, D), lambda j, k: (k, 0)),
                     (nj * nb, D), pl.BlockSpec((nb, D), lambda j, k: (j, 0)), (nb, D), (nj, T // tk), name)


def wgrad_gu(u, dh8, name, comm=None):
    n8, T, nb = dh8.shape
    D = u.shape[1]
    tk = min(2048, T)
    return matmul_tn(u, dh8, pl.BlockSpec((tk, D), lambda j, k: (k, 0)),
                     pl.BlockSpec((None, tk, nb), lambda j, k: (j, k, 0)),
                     (n8, D, nb), pl.BlockSpec((None, D, nb), lambda j, k: (j, 0, 0)), (D, nb), (n8, T // tk), name,
                     comm=comm)


def wgrad_segments(u, segs, name):
    T, D = u.shape
    ns = len(segs)
    tk = min(1024, T)
    nk = T // tk

    def body(*refs):
        x_ref, g_refs, o_ref, acc = refs[0], refs[1:1 + ns], refs[1 + ns], refs[2 + ns]
        s = pl.program_id(0)
        k = pl.program_id(1)

        @pl.when(k == 0)
        def _():
            acc[...] = jnp.zeros_like(acc)
        for i in range(ns):
            @pl.when(s == i)
            def _(i=i):
                acc[...] += _dot_tn(x_ref[...], g_refs[i][...])

        @pl.when(k == nk - 1)
        def _():
            o_ref[...] = acc[...]

    seg_specs = [pl.BlockSpec((tk, D), lambda s, k, i=i: (jnp.where(s == i, k, 0), 0)) for i in range(ns)]
    return pl.pallas_call(
        body, name=name, grid=(ns, nk),
        in_specs=[pl.BlockSpec((tk, D), lambda s, k: (k, 0))] + seg_specs,
        out_specs=pl.BlockSpec((D, D), lambda s, k: (0, s)),
        out_shape=SDS((D, ns * D), F32), scratch_shapes=[pltpu.VMEM((D, D), F32)],
        compiler_params=_cparams(("parallel", "arbitrary")))(u, *segs)


def kernel(x, c, w_ada, b_ada, ffn1_w_gu, ffn1_w_down, ln1_g, ln1_b, w_in, w_sb_out, conv_w, conv_b, conv_ln_g, conv_ln_b, w_conv_out, w_out, ln2_g, ln2_b, ffn2_w_gu, ffn2_w_down, ln3_g, ln3_b, loss_target, m_w_ada, m_b_ada, m_ffn1_w_gu, m_ffn1_w_down, m_ln1_g, m_ln1_b, m_w_in, m_w_sb_out, m_conv_w, m_conv_b, m_conv_ln_g, m_conv_ln_b, m_w_conv_out, m_w_out, m_ln2_g, m_ln2_b, m_ffn2_w_gu, m_ffn2_w_down, m_ln3_g, m_ln3_b, v_w_ada, v_b_ada, v_ffn1_w_gu, v_ffn1_w_down, v_ln1_g, v_ln1_b, v_w_in, v_w_sb_out, v_conv_w, v_conv_b, v_conv_ln_g, v_conv_ln_b, v_w_conv_out, v_w_out, v_ln2_g, v_ln2_b, v_ffn2_w_gu, v_ffn2_w_down, v_ln3_g, v_ln3_b):
    Bl, S, D = x.shape
    T = Bl * S
    kw = conv_w.shape[1]
    ax, ay, ac = lax.axis_index("x"), lax.axis_index("y"), lax.axis_index("c")
    me = 4 * ax + 2 * ay + ac
    qc = jnp.stack([2 * ax + ay, ac]).astype(jnp.int32)

    big = dict(ffn1_w_gu=ffn1_w_gu[0], ffn1_w_down=ffn1_w_down[0], w_in=w_in[0], w_sb_out=w_sb_out[0],
               w_conv_out=w_conv_out[0], w_out=w_out[0], ffn2_w_gu=ffn2_w_gu[0], ffn2_w_down=ffn2_w_down[0])
    big_m = dict(ffn1_w_gu=m_ffn1_w_gu[0], ffn1_w_down=m_ffn1_w_down[0], w_in=m_w_in[0], w_sb_out=m_w_sb_out[0],
                 w_conv_out=m_w_conv_out[0], w_out=m_w_out[0], ffn2_w_gu=m_ffn2_w_gu[0], ffn2_w_down=m_ffn2_w_down[0])
    big_v = dict(ffn1_w_gu=v_ffn1_w_gu[0], ffn1_w_down=v_ffn1_w_down[0], w_in=v_w_in[0], w_sb_out=v_w_sb_out[0],
                 w_conv_out=v_w_conv_out[0], w_out=v_w_out[0], ffn2_w_gu=v_ffn2_w_gu[0], ffn2_w_down=v_ffn2_w_down[0])
    names = list(big)

    layer1, layer2, layer3 = ["ffn1_w_gu", "ffn1_w_down"], ["w_in", "w_sb_out", "w_conv_out", "w_out"], \
        ["ffn2_w_gu", "ffn2_w_down"]

    def shards(group):
        return [big[n].astype(BF) for n in group]

    wave0, wave1 = ["ffn1_w_gu"], ["ffn1_w_down", "w_in"]
    wave2 = ["ffn2_w_gu", "ffn2_w_down", "w_sb_out", "w_conv_out", "w_out"]
    G = dict(zip(wave0, run_comm(gather_plan(shards(wave0)), "allgather_ffn1")))
    wg1 = G["ffn1_w_gu"]

    cw8 = small_allgather(conv_w[0], "allgather_conv_w")
    cw_full = jnp.transpose(cw8, (1, 0, 2)).reshape(kw, D)
    cw32 = jnp.concatenate([cw_full, jnp.zeros((HALO - kw, D), F32)], axis=0)

    c_all = small_allgather(c, "allgather_c").reshape(N_DEV * Bl, D)
    ncol = w_ada.shape[2]
    b_cols = lax.dynamic_slice(b_ada, (0, me * ncol), (1, ncol))
    mod_cols, s_all = ada_fwd(c_all, w_ada[0], b_cols, "ada_fwd")
    mod8 = small_allgather(mod_cols, "allgather_mod")
    mod_mine = lax.dynamic_slice(mod8, (0, me * Bl, 0), (N_DEV, Bl, ncol))
    mod = jnp.transpose(mod_mine, (1, 0, 2)).reshape(Bl, N_MOD_ROWS, 1, D)
    sh1, sc1, g1, sh2, sc2, g2, sh3, sc3, g3 = [mod[:, i] for i in range(N_MOD_ROWS)]

    ones = jnp.ones((1, D), F32)
    zeros = jnp.zeros((1, D), F32)
    xf = x.reshape(T, D)
    tgt = loss_target.reshape(T, D)

    (u1, a1, gg1, p1), landed = ffn_up(xf, ones, zeros, sc1, sh1, wg1, "ffn1_up", comm=gather_plan(shards(wave1)))
    G.update(zip(wave1, landed))
    wd1 = G["ffn1_w_down"].reshape(wg1.shape[0] // 2, wg1.shape[2], D)
    win = jnp.transpose(G["w_in"], (1, 0, 2)).reshape(D, -1)
    xh1, rs1, f1 = ffn_down_ln(p1, wd1, xf, ones, zeros, g1, "ffn1_down_ln")
    (u2, qkv, proj), landed = mod_matmul(xh1, ln1_g, ln1_b, sc2, sh2, win, "in_proj",
                                         comm=gather_plan(shards(wave2)))
    G.update(zip(wave2, landed))
    wg2 = G["ffn2_w_gu"]
    wd2 = G["ffn2_w_down"].reshape(wg2.shape[0] // 2, wg2.shape[2], D)
    wsb = G["w_sb_out"].reshape(D, D)
    wco = G["w_conv_out"].reshape(-1, D)
    wout = G["w_out"].reshape(D, D)
    yatt, car = att_fwd(qkv, Bl, S, D, "att_fwd")
    cs, xhc, rsc = conv_fwd(proj, cw32, conv_b, conv_ln_g, conv_ln_b, Bl, S, D, kw, "conv_fwd")
    xh2, rs2, ysb, yco, merged, o2 = mix_fwd(yatt, cs, proj, wsb, wco, wout, xh1, ln1_g, ln1_b, g2, "mix_fwd")
    (u3, a3, gg3, p3), _ = ffn_up(xh2, ln2_g, ln2_b, sc3, sh3, wg2, "ffn2_up")
    xh3, rs3, f3 = ffn_down_ln(p3, wd2, xh2, ln2_g, ln2_b, g3, "ffn2_down_ln")

    own_sum, recv_b = {}, {}

    def by_owner(group, grads):
        return [g.reshape((4, 2) + big[n].shape) for n, g in zip(group, grads)]

    def chip_sums(group, g42, recv_a):
        sums = [chip_sum(g, r, qc, "chip_sum_" + n) for g, r, n in zip(g42, recv_a, group)]
        own_sum.update({n: s[0] for n, s in zip(group, sums)})
        return chips_plan([s[1] for s in sums])

    dr3, df3, dln3g, dln3b, dg3, lossp, dh3 = ln_bwd(None, xh3, rs3, ln3_g, ln3_b, g3, f3, MACARON_WEIGHT,
                                                      "ln3_swiglu_bwd", target=tgt, swiglu=(wd2, a3, gg3))
    dh3 = dh3.reshape((-1,) + a3.shape[1:])
    gw_d2 = wgrad_down(p3, df3, "wgrad_ffn2_down")
    gw_g2 = wgrad_gu(u3, dh3, "wgrad_ffn2_gu")
    g42 = by_owner(layer3, [gw_g2, gw_d2])
    (dx2, dsc3, dsh3), recv_a = mod_bwd(dh3, wg2, dr3, xh2, ln2_g, ln2_b, sc3, True, "ffn2_mod_bwd",
                                        comm=sibling_plan(g42))
    plan3 = chip_sums(layer3, g42, recv_a)

    dr2, do2, dln2g, dln2b, dg2 = ln_bwd(dx2, xh2, rs2, ln2_g, ln2_b, g2, o2, 1.0, "ln2_bwd")
    gw_out = wgrad_std(merged, do2, "wgrad_out")
    dysb, dyco, dga, dgb, dyatt, dcs = merge_bwd(do2, proj, ysb, yco, wsb, wco, wout, "merge_bwd")
    gw_sb = wgrad_std(yatt, dysb, "wgrad_sb")
    gw_co = wgrad_std(cs, dyco, "wgrad_conv_out")
    (dq, dk, dv), landed = att_bwd(qkv, dyatt, car, Bl, S, D, "att_bwd", comm=plan3)
    recv_b.update(zip(layer3, landed))
    dconv, dcg, dcbeta, dcb = conv_ln_bwd(dcs, xhc, rsc, conv_ln_g, conv_ln_b, "conv_ln_bwd")
    dglu_a, dglu_b, dcw = conv_bwd(dconv, proj, cw32, Bl, S, D, kw, "conv_bwd")
    dproj = [dq, dk, dv, dglu_a, dglu_b, dga, dgb]
    gw_in = wgrad_segments(u2, dproj, "wgrad_in")
    gw_in = jnp.transpose(gw_in.reshape(D, N_DEV, -1), (1, 0, 2))
    g42 = by_owner(layer2, [gw_in, gw_sb, gw_co, gw_out])
    (dx1, dsc2, dsh2), recv_a = mod_bwd(dproj, win, dr2, xh1, ln1_g, ln1_b, sc2, False, "mix_mod_bwd",
                                        comm=sibling_plan(g42))
    plan2 = chip_sums(layer2, g42, recv_a)

    dr1, df1, dln1g, dln1b, dg1, dh1 = ln_bwd(dx1, xh1, rs1, ln1_g, ln1_b, g1, f1, MACARON_WEIGHT,
                                              "ln1_swiglu_bwd", swiglu=(wd1, a1, gg1))
    dh1 = dh1.reshape((-1,) + a1.shape[1:])
    gw_d1 = wgrad_down(p1, df1, "wgrad_ffn1_down")
    gw_g1, landed = wgrad_gu(u1, dh1, "wgrad_ffn1_gu", comm=plan2)
    recv_b.update(zip(layer2, landed))
    g42 = by_owner(layer1, [gw_g1, gw_d1])
    plan1 = chip_sums(layer1, g42, run_comm(sibling_plan(g42), "rs_sibling_ffn1"))
    (grad_x, dsc1, dsh1), landed = mod_bwd(dh1, wg1, dr1, xf, ones, zeros, sc1, True, "ffn1_mod_bwd", comm=plan1)
    recv_b.update(zip(layer1, landed))

    dmod = jnp.concatenate([dsh1, dsc1, dg1, dsh2, dsc2, dg2, dsh3, dsc3, dg3], axis=1).reshape(Bl, N_MOD_ROWS * D)
    dmod_all = small_allgather(dmod, "allgather_dmod").reshape(N_DEV * Bl, N_MOD_ROWS * D)
    dmod_cols = lax.dynamic_slice(dmod_all, (0, me * ncol), (N_DEV * Bl, ncol))
    g_w_ada, g_b_ada = ada_bwd(s_all, dmod_cols, dmod_all, "ada_bwd")

    npad = 16
    small_rows = [dln1g, dln1b, dcb, dcg, dcbeta, dln2g, dln2b, dln3g, dln3b,
                  jnp.broadcast_to(lossp[0:1, 0:1], (1, D))]
    pack = jnp.concatenate(small_rows + [jnp.zeros((npad - len(small_rows), D), F32), dcw], axis=0)
    small = small_sum(small_allgather(pack, "allgather_small"), "small_sum")
    loss = small[9, 0]
    small_w = [ln1_g, ln1_b, conv_b, conv_ln_g, conv_ln_b, ln2_g, ln2_b, ln3_g, ln3_b]
    small_m = [m_ln1_g, m_ln1_b, m_conv_b, m_conv_ln_g, m_conv_ln_b, m_ln2_g, m_ln2_b, m_ln3_g, m_ln3_b]
    small_v = [v_ln1_g, v_ln1_b, v_conv_b, v_conv_ln_g, v_conv_ln_b, v_ln2_g, v_ln2_b, v_ln3_g, v_ln3_b]
    padrows = jnp.zeros((npad - len(small_w), D), F32)
    s_g, s_d, s_m, s_v = adamw(jnp.concatenate(small_w + [padrows], axis=0),
                               jnp.concatenate(small_m + [padrows], axis=0),
                               jnp.concatenate(small_v + [padrows], axis=0),
                               [_plain_part(small[:npad])], "adamw_small")
    dcw_mine = lax.dynamic_slice(small[npad:npad + kw], (0, me * (D // N_DEV)), (kw, D // N_DEV))
    cw_g, cw_d, cw_m, cw_v = adamw(conv_w[0], m_conv_w[0], v_conv_w[0], [_plain_part(dcw_mine)], "adamw_conv_w")
    ada_g, ada_d, ada_m, ada_v = adamw(w_ada[0], m_w_ada[0], v_w_ada[0], [_plain_part(g_w_ada)], "adamw_w_ada")
    bada_g, bada_d, bada_m, bada_v = adamw(b_ada, m_b_ada, v_b_ada, [_plain_part(g_b_ada)], "adamw_b_ada")

    res = {}
    for n in names:
        rb = recv_b[n]
        parts = [_plain_part(own_sum[n]), _slot_part(rb, 0), _slot_part(rb, 1), _slot_part(rb, 2)]
        res[n] = adamw(big[n], big_m[n], big_v[n], parts, "adamw_" + n)

    def small_out(k):
        order = dict(ln1_g=0, ln1_b=1, conv_b=2, conv_ln_g=3, conv_ln_b=4, ln2_g=5, ln2_b=6, ln3_g=7, ln3_b=8)
        return lambda arr: arr[order[k]:order[k] + 1]

    weight_order = ["w_ada", "b_ada", "ffn1_w_gu", "ffn1_w_down", "ln1_g", "ln1_b", "w_in", "w_sb_out", "conv_w",
                    "conv_b", "conv_ln_g", "conv_ln_b", "w_conv_out", "w_out", "ln2_g", "ln2_b", "ffn2_w_gu",
                    "ffn2_w_down", "ln3_g", "ln3_b"]

    shapes = dict(w_ada=w_ada.shape, b_ada=b_ada.shape, conv_w=conv_w.shape, ln1_g=ln1_g.shape,
                  **{n: (1,) + big[n].shape for n in names})

    def pick(which):
        outs = []
        for n in weight_order:
            if n == "w_ada":
                a = (ada_g, ada_d, ada_m, ada_v)[which]
            elif n == "b_ada":
                a = (bada_g, bada_d, bada_m, bada_v)[which]
            elif n == "conv_w":
                a = (cw_g, cw_d, cw_m, cw_v)[which]
            elif n in res:
                a = res[n][which]
            else:
                a = small_out(n)((s_g, s_d, s_m, s_v)[which])
            outs.append(a.reshape(shapes.get(n, ln1_g.shape)))
        return outs

    return (loss, grad_x.reshape(Bl, S, D), *pick(0), *pick(1), *pick(2), *pick(3))
```

```python
import functools
import math

import jax
import jax.numpy as jnp
from jax import lax
from jax.experimental import pallas as pl
from jax.experimental.pallas import tpu as pltpu

F32 = jnp.float32
BF = jnp.bfloat16
SDS = jax.ShapeDtypeStruct
MESH = pl.DeviceIdType.MESH

N_DEV = 8
SB_HEAD_DIM = 64
N_MOD_ROWS = 9
MACARON_WEIGHT = 0.5
DEEPNORM_ALPHA = 2.0 ** 0.25
LN_EPS = 1e-5
ADAM_LR = 0.001
ADAM_B1 = 0.9
ADAM_B2 = 0.999
ADAM_EPS = 1e-08
ADAM_WD = 0.01
ADAM_STEP = 10

V7X_VMEM_LIMIT = 52 * 1024 * 1024
LANE = 128
SUBLANES = 8
HALO = 32


def _cparams(sem=None):
    return pltpu.CompilerParams(dimension_semantics=sem, vmem_limit_bytes=V7X_VMEM_LIMIT)


def _dot_nn(a, b):
    return lax.dot_general(a, b, (((1,), (0,)), ((), ())), preferred_element_type=F32)


def _dot_nt(a, b):
    return lax.dot_general(a, b, (((1,), (1,)), ((), ())), preferred_element_type=F32)


def _dot_tn(a, b):
    return lax.dot_general(a, b, (((0,), (0,)), ((), ())), preferred_element_type=F32)


def _sig(x):
    return 1.0 / (1.0 + jnp.exp(-x))


def _ln_stats(r):
    mu = jnp.mean(r, axis=-1, keepdims=True)
    d = r - mu
    var = jnp.mean(d * d, axis=-1, keepdims=True)
    rstd = lax.rsqrt(var + LN_EPS)
    return d * rstd, rstd


def _ln_bwd(dxh, xh, rstd):
    m1 = jnp.mean(dxh, axis=-1, keepdims=True)
    m2 = jnp.mean(dxh * xh, axis=-1, keepdims=True)
    return rstd * (dxh - m1 - xh * m2)


def _rowsum(v):
    return jnp.sum(v, axis=0, keepdims=True)


def _row_tile(n, cap):
    if n <= cap:
        return n
    best = None
    for t in range(8, cap + 1, 8):
        if n % t == 0:
            best = t
    assert best is not None, (n, cap)
    return best


def _coords():
    x, y, c = lax.axis_index("x"), lax.axis_index("y"), lax.axis_index("c")
    return x, y, c


def _flip(v, bit):
    return 1 - v if bit else v


def small_allgather(blk, name):
    r, n = blk.shape

    def body(x_ref, out_ref, send_sems, recv_sems):
        x, y, c = _coords()
        me = 4 * x + 2 * y + c
        out_ref[me] = x_ref[...]
        copies = []
        for k in range(1, N_DEV):
            peer = (_flip(x, k & 4), _flip(y, k & 2), _flip(c, k & 1))
            cp = pltpu.make_async_remote_copy(
                src_ref=x_ref, dst_ref=out_ref.at[me], send_sem=send_sems.at[k - 1],
                recv_sem=recv_sems.at[k - 1], device_id=peer, device_id_type=MESH)
            cp.start()
            copies.append(cp)
        for k in range(1, N_DEV):
            px, py, pc = _flip(x, k & 4), _flip(y, k & 2), _flip(c, k & 1)
            slot = 4 * px + 2 * py + pc
            pltpu.make_async_remote_copy(
                src_ref=x_ref, dst_ref=out_ref.at[slot], send_sem=send_sems.at[k - 1],
                recv_sem=recv_sems.at[k - 1], device_id=(px, py, pc), device_id_type=MESH).wait_recv()
        for cp in copies:
            cp.wait_send()

    return pl.pallas_call(
        body, name=name,
        out_shape=SDS((N_DEV, r, n), blk.dtype),
        in_specs=[pl.BlockSpec(memory_space=pltpu.VMEM)],
        out_specs=pl.BlockSpec(memory_space=pltpu.VMEM),
        scratch_shapes=[pltpu.SemaphoreType.DMA((N_DEV - 1,)), pltpu.SemaphoreType.DMA((N_DEV - 1,))],
    )(blk)


class CommPlan:
    def __init__(self, ins, out_shape, scratch, emit):
        self.ins, self.out_shape, self.scratch, self.emit = list(ins), list(out_shape), list(scratch), emit


def _phase(step, at, fn):
    if step is None:
        fn()
    else:
        pl.when(step == at)(fn)


def gather_plan(shards):
    n = len(shards)
    per = 7

    def emit(ins, outs, sems, step, nsteps):
        send_sems, recv_sems, local_sems = sems
        x, y, c = _coords()
        me = 4 * x + 2 * y + c
        sibling = (x, y, 1 - c)
        chips = [(1 - x, y), (x, 1 - y), (1 - x, 1 - y)]

        def slot(px, py, pc):
            return 4 * px + 2 * py + pc

        def copy(t, k, block, to, src=None):
            dst = outs[t].at[slot(*block)]
            return pltpu.make_async_remote_copy(
                src_ref=dst if src is None else src, dst_ref=dst,
                send_sem=send_sems.at[per * t + k], recv_sem=recv_sems.at[per * t + k],
                device_id=to, device_id_type=MESH)

        def local(t):
            return pltpu.make_async_copy(ins[t], outs[t].at[me], local_sems.at[t])

        def first(t):
            return [copy(t, 0, (x, y, c), sibling, src=ins[t])] + [
                copy(t, 1 + j, (x, y, c), (*chip, c), src=ins[t]) for j, chip in enumerate(chips)]

        def passed(t):
            return [copy(t, 4 + j, (*chip, c), sibling) for j, chip in enumerate(chips)]

        def start():
            for t in range(n):
                local(t).start()
                for cp in first(t):
                    cp.start()

        def forward():
            for t in range(n):
                for j, chip in enumerate(chips):
                    copy(t, 1 + j, (*chip, c), (x, y, c)).wait_recv()
                    passed(t)[j].start()

        def finish():
            for t in range(n):
                copy(t, 0, (x, y, 1 - c), (x, y, c)).wait_recv()
                for j, chip in enumerate(chips):
                    copy(t, 4 + j, (*chip, 1 - c), (x, y, c)).wait_recv()
            for t in range(n):
                for cp in first(t) + passed(t):
                    cp.wait_send()
                local(t).wait()

        _phase(step, 0, start)
        _phase(step, None if step is None else max(nsteps - 2, 0), forward)
        _phase(step, None if step is None else nsteps - 1, finish)

    return CommPlan(shards, [SDS((N_DEV,) + s.shape, s.dtype) for s in shards],
                    [pltpu.SemaphoreType.DMA((per * n,)), pltpu.SemaphoreType.DMA((per * n,)),
                     pltpu.SemaphoreType.DMA((n,))], emit)


def chips_plan(sums):
    n = len(sums)

    def emit(ins, outs, sems, step, nsteps):
        send_sems, recv_sems = sems
        x, y, c = _coords()

        def copies():
            return [pltpu.make_async_remote_copy(
                src_ref=ins[t].at[j - 1], dst_ref=outs[t].at[j - 1], send_sem=send_sems.at[3 * t + j - 1],
                recv_sem=recv_sems.at[3 * t + j - 1], device_id=(_flip(x, j & 2), _flip(y, j & 1), c),
                device_id_type=MESH) for t in range(n) for j in range(1, 4)]

        def start():
            for cp in copies():
                cp.start()

        def finish():
            for cp in copies():
                cp.wait_recv()
            for cp in copies():
                cp.wait_send()

        _phase(step, 0, start)
        _phase(step, None if step is None else nsteps - 1, finish)

    return CommPlan(sums, [SDS(s.shape, s.dtype) for s in sums],
                    [pltpu.SemaphoreType.DMA((3 * n,)), pltpu.SemaphoreType.DMA((3 * n,))], emit)


def run_comm(plan, name):
    n, m = len(plan.ins), len(plan.out_shape)

    def body(*refs):
        plan.emit(refs[:n], refs[n:n + m], refs[n + m:], None, 1)

    anyspec = pl.BlockSpec(memory_space=pl.ANY)
    return pl.pallas_call(body, name=name, out_shape=plan.out_shape, in_specs=[anyspec] * n,
                          out_specs=[anyspec] * m, scratch_shapes=plan.scratch)(*plan.ins)


class _Rider:
    def __init__(self, plan):
        self.plan = plan
        anyspec = pl.BlockSpec(memory_space=pl.ANY)
        self.ins = plan.ins if plan else []
        self.in_specs = [anyspec] * len(self.ins)
        self.out_specs = [anyspec] * (len(plan.out_shape) if plan else 0)
        self.out_shape = plan.out_shape if plan else []
        self.scratch = plan.scratch if plan else []

    def split(self, refs, n_in, n_out, n_scratch=0):
        ni, no = len(self.ins), len(self.out_shape)
        own_in = refs[:n_in]
        c_in = refs[n_in:n_in + ni]
        own_out = refs[n_in + ni:n_in + ni + n_out]
        c_out = refs[n_in + ni + n_out:n_in + ni + n_out + no]
        rest = refs[n_in + ni + n_out + no:]
        own_scr, c_scr = rest[:n_scratch], rest[n_scratch:]

        def ride(step, nsteps):
            if self.plan:
                self.plan.emit(c_in, c_out, c_scr, step, nsteps)

        return tuple(own_in) + tuple(own_out) + tuple(own_scr), ride

    def result(self, outs, n_out):
        outs = list(outs) if isinstance(outs, (list, tuple)) else [outs]
        return outs[:n_out], (outs[n_out:] if self.plan else None)

    def sem(self, sem):
        return tuple("arbitrary" for _ in sem) if self.plan else sem


def sibling_plan(grads):
    n = len(grads)

    def emit(ins, outs, sems, step, nsteps):
        send_sems, recv_sems = sems
        x, y, c = _coords()

        def copies():
            return [pltpu.make_async_remote_copy(
                src_ref=ins[t].at[:, 1 - c], dst_ref=outs[t], send_sem=send_sems.at[t],
                recv_sem=recv_sems.at[t], device_id=(x, y, 1 - c), device_id_type=MESH) for t in range(n)]

        def start():
            for cp in copies():
                cp.start()

        def finish():
            for cp in copies():
                cp.wait_recv()
            for cp in copies():
                cp.wait_send()

        _phase(step, 0, start)
        _phase(step, None if step is None else nsteps - 1, finish)

    return CommPlan(grads, [SDS((4,) + g.shape[2:], g.dtype) for g in grads],
                    [pltpu.SemaphoreType.DMA((n,)), pltpu.SemaphoreType.DMA((n,))], emit)


def chip_sum(g42, recv, qc, name):
    _, _, R, C = g42.shape
    tr = _row_tile(R, 256)

    def body(qc_ref, a_ref, b_ref, own_ref, send_ref):
        j = pl.program_id(1)
        s = a_ref[...] + b_ref[...]

        @pl.when(j == 0)
        def _():
            own_ref[...] = s

        @pl.when(j > 0)
        def _():
            send_ref[...] = s.astype(BF)

    gs = pltpu.PrefetchScalarGridSpec(
        num_scalar_prefetch=1, grid=(R // tr, 4),
        in_specs=[pl.BlockSpec((None, None, tr, C), lambda i, j, s: (jnp.bitwise_xor(s[0], j), s[1], i, 0)),
                  pl.BlockSpec((None, tr, C), lambda i, j, s: (jnp.bitwise_xor(s[0], j), i, 0))],
        out_specs=[pl.BlockSpec((tr, C), lambda i, j, s: (i, 0)),
                   pl.BlockSpec((None, tr, C), lambda i, j, s: (jnp.maximum(j - 1, 0), i, 0))])
    return pl.pallas_call(body, name=name, grid_spec=gs, out_shape=[SDS((R, C), F32), SDS((3, R, C), BF)],
                          compiler_params=_cparams(("arbitrary", "arbitrary")))(qc, g42, recv)


def small_sum(g8, name):
    def body(g_ref, o_ref):
        acc = g_ref[0]
        for k in range(1, N_DEV):
            acc = acc + g_ref[k]
        o_ref[...] = acc
    return pl.pallas_call(body, name=name, out_shape=SDS(g8.shape[1:], F32))(g8)


def adamw(w, m, v, parts, name):
    R, C = w.shape
    tr = _row_tile(R, 256)
    npart = len(parts)
    c1 = 1.0 / (1.0 - ADAM_B1 ** ADAM_STEP)
    c2 = 1.0 / (1.0 - ADAM_B2 ** ADAM_STEP)

    def body(*refs):
        w_ref, m_ref, v_ref = refs[:3]
        p_refs = refs[3:3 + npart]
        g_ref, d_ref, nm_ref, nv_ref = refs[3 + npart:]
        g = p_refs[0][...].astype(F32)
        for p in p_refs[1:]:
            g = g + p[...].astype(F32)
        nm = ADAM_B1 * m_ref[...] + (1.0 - ADAM_B1) * g
        nv = ADAM_B2 * v_ref[...] + (1.0 - ADAM_B2) * (g * g)
        mh = nm * c1
        vh = nv * c2
        g_ref[...] = g
        nm_ref[...] = nm
        nv_ref[...] = nv
        d_ref[...] = -ADAM_LR * (mh / (jnp.sqrt(vh) + ADAM_EPS) + ADAM_WD * w_ref[...])

    wspec = pl.BlockSpec((tr, C), lambda i: (i, 0))
    pspecs = [pl.BlockSpec(bs(tr, C), im) for (_, bs, im) in parts]
    outs = pl.pallas_call(
        body, name=name, grid=(R // tr,),
        in_specs=[wspec] * 3 + pspecs, out_specs=[wspec] * 4,
        out_shape=[SDS((R, C), F32)] * 4,
        compiler_params=_cparams(("parallel",)))(w, m, v, *[p[0] for p in parts])
    return outs


def _plain_part(g):
    return (g, lambda tr, C: (tr, C), lambda i: (i, 0))


def _slot_part(g, slot):
    return (g, lambda tr, C: (None, tr, C), lambda i, s=slot: (s, i, 0))


def ada_fwd(c_all, w_cols, b_cols, name):
    Bg, D = c_all.shape
    n = w_cols.shape[1]

    def body(c_ref, w_ref, b_ref, o_ref, s_ref):
        cc = c_ref[...]
        s = cc * _sig(cc)
        s_ref[...] = s
        o_ref[...] = jnp.dot(s, w_ref[...], preferred_element_type=F32, precision=lax.Precision.HIGHEST) + b_ref[...]

    return pl.pallas_call(body, name=name, out_shape=[SDS((Bg, n), F32), SDS((Bg, D), F32)],
                          compiler_params=_cparams())(c_all, w_cols, b_cols)


def ada_bwd(s_all, dmod_cols, dmod_all, name):
    Bg, D = s_all.shape
    n = dmod_cols.shape[1]

    def body(s_ref, dc_ref, da_ref, gw_ref, gb_ref):
        gw_ref[...] = lax.dot_general(s_ref[...], dc_ref[...], (((0,), (0,)), ((), ())),
                                      preferred_element_type=F32, precision=lax.Precision.HIGHEST)
        acc = da_ref[0:1, :]
        for r in range(1, Bg):
            acc = acc + da_ref[r:r + 1, :]
        gb_ref[...] = acc

    return pl.pallas_call(body, name=name, out_shape=[SDS((D, n), F32), SDS((1, dmod_all.shape[1]), F32)],
                          compiler_params=_cparams())(s_all, dmod_cols, dmod_all)


def _vec(D, rank):
    return pl.BlockSpec((1, D), (lambda i: (0, 0)) if rank == 1 else (lambda i, j: (0, 0)))


def _modspec(D, tpb, rank):
    if rank == 1:
        return pl.BlockSpec((None, 1, D), lambda i: (i // tpb, 0, 0))
    return pl.BlockSpec((None, 1, D), lambda i, j: (i // tpb, 0, 0))


def _resident(shape):
    return pl.BlockSpec(shape, lambda *_: (0,) * len(shape), pipeline_mode=pl.Buffered(1))


def ffn_up(xs, pg, pb, sc, sh, wg8, name, comm=None):
    T, D = xs.shape
    n2, _, nb = wg8.shape
    nj = n2 // 2
    S = T // sc.shape[0]
    tm = min(512, S)
    tpb = S // tm
    rider = _Rider(comm)

    def body(*refs):
        (x_ref, pg_ref, pb_ref, sc_ref, sh_ref, w_ref, u_ref, a_ref, g_ref, p_ref), ride = rider.split(refs, 6, 4)
        ride(pl.program_id(0), T // tm)
        xin = x_ref[...] * pg_ref[...] + pb_ref[...]
        u_ref[...] = (xin * (1.0 + sc_ref[...]) + sh_ref[...]).astype(BF)

        def col_block(j, _):
            u = u_ref[...]
            a = _dot_nn(u, w_ref[j])
            g = _dot_nn(u, w_ref[j + nj])
            s = _sig(a)
            silu = a * s
            a_ref[j] = (g * (s * (1.0 + a * (1.0 - s)))).astype(BF)
            g_ref[j] = silu.astype(BF)
            p_ref[j] = (silu * g).astype(BF)
            return 0

        lax.fori_loop(0, nj, col_block, 0)

    blk = pl.BlockSpec((nj, tm, nb), lambda i: (0, i, 0))
    row = pl.BlockSpec((tm, D), lambda i: (i, 0))
    outs = pl.pallas_call(
        body, name=name, grid=(T // tm,),
        in_specs=[row, _vec(D, 1), _vec(D, 1), _modspec(D, tpb, 1), _modspec(D, tpb, 1), _resident(wg8.shape)]
        + rider.in_specs,
        out_specs=[row, blk, blk, blk] + rider.out_specs,
        out_shape=[SDS((T, D), BF)] + [SDS((nj, T, nb), BF)] * 3 + rider.out_shape,
        scratch_shapes=rider.scratch,
        compiler_params=_cparams(rider.sem(("parallel",))))(xs, pg, pb, sc, sh, wg8, *rider.ins)
    return rider.result(outs, 4)


def ffn_down_ln(p4, wd3, xs, pg, pb, gate, name):
    nj, T, nb = p4.shape
    D = wd3.shape[2]
    S = T // gate.shape[0]
    tm = min(512, S)
    tpb = S // tm

    def body(p_ref, wd_ref, x_ref, pg_ref, pb_ref, gate_ref, xh_ref, rs_ref, f_ref):
        f = _dot_nn(p_ref[0], wd_ref[0])
        for k in range(1, nj):
            f = f + _dot_nn(p_ref[k], wd_ref[k])
        xin = x_ref[...] * pg_ref[...] + pb_ref[...]
        r = DEEPNORM_ALPHA * xin + gate_ref[...] * (MACARON_WEIGHT * f)
        xh, rstd = _ln_stats(r)
        xh_ref[...] = xh
        rs_ref[...] = rstd
        f_ref[...] = f.astype(BF)

    row = pl.BlockSpec((tm, D), lambda i: (i, 0))
    return pl.pallas_call(
        body, name=name, grid=(T // tm,),
        in_specs=[pl.BlockSpec((nj, tm, nb), lambda i: (0, i, 0)), _resident(wd3.shape),
                  row, _vec(D, 1), _vec(D, 1), _modspec(D, tpb, 1)],
        out_specs=[row, pl.BlockSpec((tm, 1), lambda i: (i, 0)), row],
        out_shape=[SDS((T, D), F32), SDS((T, 1), F32), SDS((T, D), BF)],
        compiler_params=_cparams(("parallel",)))(p4, wd3, xs, pg, pb, gate)


N_QKV = 3


def mod_matmul(xs, pg, pb, sc, sh, w, name, comm=None):
    T, D = xs.shape
    N = w.shape[1]
    S = T // sc.shape[0]
    tm = min(256, S)
    tpb = S // tm
    rider = _Rider(comm)

    def body(*refs):
        (x_ref, pg_ref, pb_ref, sc_ref, sh_ref, w_ref, u_ref, qkv_ref, o_ref), ride = rider.split(refs, 6, 3)
        ride(pl.program_id(0), T // tm)
        xin = x_ref[...] * pg_ref[...] + pb_ref[...]
        u = (xin * (1.0 + sc_ref[...]) + sh_ref[...]).astype(BF)
        u_ref[...] = u
        for n in range(N // D):
            y = _dot_nn(u, w_ref[:, n * D:(n + 1) * D])
            if n < N_QKV:
                qkv_ref[:, n * D:(n + 1) * D] = y.astype(BF)
            else:
                o_ref[:, (n - N_QKV) * D:(n - N_QKV + 1) * D] = y

    row = pl.BlockSpec((tm, D), lambda i: (i, 0))
    outs = pl.pallas_call(
        body, name=name, grid=(T // tm,),
        in_specs=[row, _vec(D, 1), _vec(D, 1), _modspec(D, tpb, 1), _modspec(D, tpb, 1), _resident(w.shape)]
        + rider.in_specs,
        out_specs=[row, pl.BlockSpec((tm, N_QKV * D), lambda i: (i, 0)),
                   pl.BlockSpec((tm, N - N_QKV * D), lambda i: (i, 0))] + rider.out_specs,
        out_shape=[SDS((T, D), BF), SDS((T, N_QKV * D), BF), SDS((T, N - N_QKV * D), F32)] + rider.out_shape,
        scratch_shapes=rider.scratch,
        compiler_params=_cparams(rider.sem(("parallel",))))(xs, pg, pb, sc, sh, w, *rider.ins)
    return rider.result(outs, 3)


ATT_TQ = 1024
ATT_TK = 256


def _att_consts(tk):
    r = lax.broadcasted_iota(jnp.int32, (tk + 8, tk), 0)
    c = lax.broadcasted_iota(jnp.int32, (tk + 8, tk), 1)
    usum = jnp.where((r >= tk) | (c > r), 1.0, 0.0).astype(BF)
    lsum = jnp.where((r >= tk) | (c < r), 1.0, 0.0).astype(BF)
    dmask = lax.broadcasted_iota(jnp.int32, (tk, tk), 0) < lax.broadcasted_iota(jnp.int32, (tk, tk), 1)
    return usum, lsum, dmask


def _split_dot(m, v):
    hi = v.astype(BF)
    lo = (v - hi.astype(F32)).astype(BF)
    return _dot_nn(m, hi) + _dot_nn(m, lo)


def _softplus(z):
    return jnp.maximum(z, 0.0) + jnp.log(1.0 + jnp.exp(-jnp.abs(z)))


def _att_dims(S, D):
    dh = SB_HEAD_DIM
    cw = min(LANE, D)
    tq = min(ATT_TQ, S)
    tk = min(ATT_TK, tq)
    assert tq % tk == 0 and S % tq == 0
    return dh, cw, cw // dh, D // cw, tq, tk, S // tq, S // tk


def att_fwd(proj, Bl, S, D, name):
    dh, cw, hp, nblk, tq, tk, nq, nk = _att_dims(S, D)
    scale = 1.0 / math.sqrt(dh)
    assert math.log2(scale) == int(math.log2(scale))
    H = D // dh

    def body(q_ref, k_ref, v_ref, o_ref, car_ref, qs, ks, vts):
        usum, _, dmask = _att_consts(tk)
        for hh in range(hp):
            sl = slice(hh * dh, (hh + 1) * dh)
            qs[hh] = (q_ref[:, sl] * scale).astype(BF)
            ks[hh] = k_ref[:, sl].astype(BF)
            for kb in range(nk):
                vts[hh, kb] = v_ref[kb * tk:(kb + 1) * tk, sl].astype(F32).T.astype(BF)
        nch = tq // tk

        def qloop(qb, _):
            qo = pl.multiple_of(qb * tq, tq)
            n_full = qb * nch

            def blk(kb, state, diag):
                ko = pl.multiple_of(kb * tk, tk)
                chains = [(hh, c) for hh in range(hp) for c in range(0 if diag is None else diag, nch)]

                def masked(ch, val):
                    return jnp.where(dmask, val, 0.0) if ch[1] == diag else val

                z = {ch: _dot_nt(ks[ch[0], pl.ds(ko, tk), :], qs[ch[0], pl.ds(pl.multiple_of(qo + ch[1] * tk, tk), tk), :])
                     for ch in chains}
                sp = {ch: _softplus(z[ch]) for ch in chains}
                lk = {ch: masked(ch, -sp[ch]) for ch in chains}
                for hh, c in chains:
                    car_ref[hh, qb * nk + kb, :, c * tk:(c + 1) * tk] = state[hh][c][0]
                cs = {ch: _split_dot(usum, lk[ch]) for ch in chains}
                w = {ch: masked(ch, jnp.exp((z[ch] - sp[ch]) + state[ch[0]][ch[1]][0][0:1, :] + cs[ch][:tk]))
                     for ch in chains}
                pv = {ch: _dot_nn(vts[ch[0], kb], w[ch].astype(BF)) for ch in chains}
                return tuple(tuple(
                    (state[hh][c][0] + cs[(hh, c)][tk:], state[hh][c][1] + pv[(hh, c)]) if (hh, c) in z else state[hh][c]
                    for c in range(nch)) for hh in range(hp))

            state = tuple(tuple((jnp.zeros((8, tk), F32), jnp.zeros((dh, tk), F32)) for _ in range(nch))
                          for _ in range(hp))
            for i in reversed(range(nch)):
                state = blk(n_full + i, state, i)
            state = lax.fori_loop(0, n_full, lambda j, st: blk(n_full - 1 - j, st, None), state)
            for hh in range(hp):
                for c in range(nch):
                    o_ref[pl.ds(pl.multiple_of(qo + c * tk, tk), tk), hh * dh:(hh + 1) * dh] = (
                        state[hh][c][1].T.astype(BF))
            return 0

        lax.fori_loop(0, nq, qloop, 0)

    def seg(s):
        return pl.BlockSpec((S, cw), lambda b, h: (b, s * nblk + h))

    return pl.pallas_call(
        body, name=name, grid=(Bl, nblk),
        in_specs=[seg(0), seg(1), seg(2)],
        out_specs=[pl.BlockSpec((S, cw), lambda b, h: (b, h)),
                   pl.BlockSpec((None, hp, nq * nk, 8, tq), lambda b, h: (b, h, 0, 0, 0))],
        out_shape=[SDS((Bl * S, D), BF), SDS((Bl, H, nq * nk, 8, tq), F32)],
        scratch_shapes=[pltpu.VMEM((hp, S, dh), BF)] * 2 + [pltpu.VMEM((hp, nk, dh, tk), BF)],
        compiler_params=_cparams(("parallel", "parallel")))(proj, proj, proj)


def conv_fwd(proj, cw32, cb, cg, cbeta, Bl, S, D, kw, name):
    T = Bl * S
    ts = min(128, S)
    ns = S // ts
    off = HALO - (kw - 1)
    rc = min(64, ts)
    cw = min(LANE, D)

    def body(a_ref, b_ref, ha_ref, hb_ref, w_ref, cb_ref, g_ref, be_ref, cs_ref, xh_ref, rs_ref, hsh, conv_s):
        i = pl.program_id(1)
        h = a_ref[...] * _sig(b_ref[...])
        hh = jnp.where(i == 0, 0.0, ha_ref[...] * _sig(hb_ref[...]))
        for cb_ in range(D // cw):
            cols = slice(cb_ * cw, (cb_ + 1) * cw)
            hsh[0, cb_, pl.ds(HALO, ts), :] = h[:, cols]
            hsh[0, cb_, pl.ds(0, HALO), :] = hh[:, cols]
            for s in range(1, SUBLANES):
                hsh[s, cb_, pl.ds(0, ts + HALO - SUBLANES), :] = hsh[0, cb_, pl.ds(s, ts + HALO - SUBLANES), :]
            accs = [jnp.zeros((rc, cw), F32) for _ in range(ts // rc)]
            for k in range(kw):
                wk = w_ref[k:k + 1, cols]
                s = (off + k) % SUBLANES
                for r in range(ts // rc):
                    accs[r] = accs[r] + wk * hsh[s, cb_, pl.ds(r * rc + off + k - s, rc), :]
            for r in range(ts // rc):
                conv_s[pl.ds(r * rc, rc), cols] = accs[r]
        conv = conv_s[...] + cb_ref[...]
        xh, rstd = _ln_stats(conv)
        xh_ref[...] = xh
        rs_ref[...] = rstd
        cl = xh * g_ref[...] + be_ref[...]
        cs_ref[...] = (cl * _sig(cl)).astype(BF)

    hpb = ts // HALO

    def tile(seg):
        return pl.BlockSpec((ts, D), lambda b, i: (b * ns + i, seg))

    def halo(seg):
        return pl.BlockSpec((HALO, D), lambda b, i: (jnp.maximum((b * ns + i) * hpb - 1, 0), seg))

    row = pl.BlockSpec((ts, D), lambda b, i: (b * ns + i, 0))
    vec = pl.BlockSpec((1, D), lambda b, i: (0, 0))
    return pl.pallas_call(
        body, name=name, grid=(Bl, ns),
        in_specs=[tile(0), tile(1), halo(0), halo(1), pl.BlockSpec((HALO, D), lambda b, i: (0, 0)), vec, vec, vec],
        out_specs=[row, row, pl.BlockSpec((ts, 1), lambda b, i: (b * ns + i, 0))],
        out_shape=[SDS((T, D), BF), SDS((T, D), F32), SDS((T, 1), F32)],
        scratch_shapes=[pltpu.VMEM((SUBLANES, D // cw, ts + HALO, cw), F32), pltpu.VMEM((ts, D), F32)],
        compiler_params=_cparams(("parallel", "arbitrary")))(proj, proj, proj, proj, cw32, cb, cg, cbeta)


def mix_fwd(yatt, cs, proj, wsb, wco, wout, xs, pg, pb, gate, name):
    T, D = yatt.shape
    S = T // gate.shape[0]
    tm = min(256, S)
    tpb = S // tm

    def body(ya_ref, cs_ref, ga_ref, gb_ref, wsb_ref, wco_ref, wout_ref, x_ref, pg_ref, pb_ref, gate_ref,
             xh_ref, rs_ref, ysb_ref, yco_ref, mg_ref, o_ref):
        ysb = _dot_nn(ya_ref[...], wsb_ref[...])
        yco = _dot_nn(cs_ref[...], wco_ref[...])
        merged = _sig(ga_ref[...]) * ysb + _sig(gb_ref[...]) * yco
        mg = merged.astype(BF)
        o = _dot_nn(mg, wout_ref[...])
        xin = x_ref[...] * pg_ref[...] + pb_ref[...]
        r = DEEPNORM_ALPHA * xin + gate_ref[...] * o
        xh, rstd = _ln_stats(r)
        xh_ref[...] = xh
        rs_ref[...] = rstd
        ysb_ref[...] = ysb.astype(BF)
        yco_ref[...] = yco.astype(BF)
        mg_ref[...] = mg
        o_ref[...] = o.astype(BF)

    row = pl.BlockSpec((tm, D), lambda i: (i, 0))
    wfull = pl.BlockSpec((D, D), lambda i: (0, 0))
    return pl.pallas_call(
        body, name=name, grid=(T // tm,),
        in_specs=[row, row, pl.BlockSpec((tm, D), lambda i: (i, 2)), pl.BlockSpec((tm, D), lambda i: (i, 3)),
                  wfull, wfull, wfull, row, _vec(D, 1), _vec(D, 1), _modspec(D, tpb, 1)],
        out_specs=[row, pl.BlockSpec((tm, 1), lambda i: (i, 0)), row, row, row, row],
        out_shape=[SDS((T, D), F32), SDS((T, 1), F32)] + [SDS((T, D), BF)] * 4,
        compiler_params=_cparams(("parallel",)))(yatt, cs, proj, proj, wsb, wco, wout, xs, pg, pb, gate)


def ln_bwd(dout, xh, rstd, lng, lnb, gate, sub, res_w, name, target=None, swiglu=None):
    T, D = xh.shape
    Bl = gate.shape[0]
    S = T // Bl
    tm = min(256, S)
    tpb = S // tm
    first = target is not None
    n_in = 10 if swiglu else 7

    def body(*refs):
        do_ref, xh_ref, rs_ref, g_ref, b_ref, gate_ref, sub_ref = refs[:7]
        tg_ref = do_ref
        outs = refs[n_in:]
        dr_ref, ds_ref, dg_ref, db_ref, dgate_ref = outs[:5]
        loss_ref = outs[5] if first else None
        i = pl.program_id(0)
        xh_ = xh_ref[...]
        if first:
            diff = (xh_ * g_ref[...] + b_ref[...]) - tg_ref[...]
            lsum = jnp.sum(jnp.sum(diff * diff, axis=1, keepdims=True), axis=0, keepdims=True) * (0.5 / D)
            do = diff * (1.0 / D)
        else:
            do = do_ref[...]

        @pl.when(i == 0)
        def _():
            dg_ref[...] = jnp.zeros_like(dg_ref)
            db_ref[...] = jnp.zeros_like(db_ref)
            if first:
                loss_ref[...] = jnp.zeros_like(loss_ref)

        @pl.when(i % tpb == 0)
        def _():
            dgate_ref[...] = jnp.zeros_like(dgate_ref)

        if first:
            loss_ref[...] += jnp.broadcast_to(lsum, loss_ref.shape)
        dg_ref[...] += _rowsum(do * xh_)
        db_ref[...] += _rowsum(do)
        dr = _ln_bwd(do * g_ref[...], xh_, rs_ref[...])
        dr_ref[...] = dr
        ds_ref[...] = (dr * gate_ref[...] * res_w).astype(BF)
        dgate_ref[...] += _rowsum(dr * (res_w * sub_ref[...].astype(F32)))
        if swiglu:
            wd_ref, a_ref, gg_ref = refs[7:10]
            dh_ref = outs[-1]

            def col_block(j, _):
                dp = _dot_nt(ds_ref[...], wd_ref[j])
                dh_ref[0, j] = (dp * a_ref[j].astype(F32)).astype(BF)
                dh_ref[1, j] = (dp * gg_ref[j].astype(F32)).astype(BF)
                return 0

            lax.fori_loop(0, swiglu[1].shape[0], col_block, 0)

    row = pl.BlockSpec((tm, D), lambda i: (i, 0))
    vec = _vec(D, 1)
    mod = _modspec(D, tpb, 1)
    in_specs = [row, row, pl.BlockSpec((tm, 1), lambda i: (i, 0)), vec, vec, mod, row]
    out_specs = [row, row, vec, vec, mod]
    out_shape = [SDS((T, D), F32), SDS((T, D), BF), SDS((1, D), F32), SDS((1, D), F32), SDS((Bl, 1, D), F32)]
    if first:
        out_specs.append(pl.BlockSpec((8, LANE), lambda i: (0, 0)))
        out_shape.append(SDS((8, LANE), F32))
    extra = []
    if swiglu:
        wd3, a4, g4 = swiglu
        nj, _, nb = a4.shape
        blk = pl.BlockSpec((nj, tm, nb), lambda i: (0, i, 0))
        in_specs += [_resident(wd3.shape), blk, blk]
        out_specs.append(pl.BlockSpec((2, nj, tm, nb), lambda i: (0, 0, i, 0)))
        out_shape.append(SDS((2, nj, T, nb), BF))
        extra = [wd3, a4, g4]
    return pl.pallas_call(
        body, name=name, grid=(T // tm,), in_specs=in_specs, out_specs=out_specs, out_shape=out_shape,
        compiler_params=_cparams(("arbitrary",)))(target if first else dout, xh, rstd, lng, lnb, gate, sub, *extra)


def mod_bwd(dh, w, dr, xs, pg, pb, sc, blocked, name, comm=None):
    T, D = dr.shape
    Bl = sc.shape[0]
    S = T // Bl
    tm = min(512 if blocked else 256, S)
    tpb = S // tm
    row = pl.BlockSpec((tm, D), lambda i: (i, 0))
    if blocked:
        nk, _, kb = dh.shape
        dh_list, dh_specs = [dh], [pl.BlockSpec((nk, tm, kb), lambda i: (0, i, 0))]
    else:
        nk = len(dh)
        dh_list, dh_specs = list(dh), [row] * nk
    nd = len(dh_list)
    rider = _Rider(comm)

    def body(*refs):
        own, ride = rider.split(refs, nd + 6, 3)
        dh_refs = own[:nd]
        w_ref, dr_ref, x_ref, pg_ref, pb_ref, sc_ref, dx_ref, dsc_ref, dsh_ref = own[nd:]
        i = pl.program_id(0)
        ride(i, T // tm)

        def part(k):
            if blocked:
                return _dot_nt(dh_refs[0][k], w_ref[k])
            return _dot_nt(dh_refs[k][...], w_ref[:, k * D:(k + 1) * D])

        du = part(0)
        for k in range(1, nk):
            du = du + part(k)

        @pl.when(i % tpb == 0)
        def _():
            dsc_ref[...] = jnp.zeros_like(dsc_ref)
            dsh_ref[...] = jnp.zeros_like(dsh_ref)

        xin = x_ref[...] * pg_ref[...] + pb_ref[...]
        dx_ref[...] = DEEPNORM_ALPHA * dr_ref[...] + du * (1.0 + sc_ref[...])
        dsc_ref[...] += _rowsum(du * xin)
        dsh_ref[...] += _rowsum(du)

    mod = _modspec(D, tpb, 1)
    outs = pl.pallas_call(
        body, name=name, grid=(T // tm,),
        in_specs=dh_specs + [_resident(w.shape), row, row, _vec(D, 1), _vec(D, 1), mod] + rider.in_specs,
        out_specs=[row, mod, mod] + rider.out_specs,
        out_shape=[SDS((T, D), F32), SDS((Bl, 1, D), F32), SDS((Bl, 1, D), F32)] + rider.out_shape,
        scratch_shapes=rider.scratch,
        compiler_params=_cparams(("arbitrary",)))(*dh_list, w, dr, xs, pg, pb, sc, *rider.ins)
    return rider.result(outs, 3)


def merge_bwd(do2, proj, ysb, yco, wsb, wco, wout, name):
    T, D = do2.shape
    tm = min(256, T)

    def body(do_ref, ga_ref, gb_ref, ysb_ref, yco_ref, wsb_ref, wco_ref, wout_ref,
             dysb_ref, dyco_ref, dga_ref, dgb_ref, dya_ref, dcs_ref):
        dm = _dot_nt(do_ref[...], wout_ref[...])
        sa = _sig(ga_ref[...])
        sb = _sig(gb_ref[...])
        dysb = (dm * sa).astype(BF)
        dyco = (dm * sb).astype(BF)
        dysb_ref[...] = dysb
        dyco_ref[...] = dyco
        dga_ref[...] = (dm * ysb_ref[...].astype(F32) * (sa * (1.0 - sa))).astype(BF)
        dgb_ref[...] = (dm * yco_ref[...].astype(F32) * (sb * (1.0 - sb))).astype(BF)
        dya_ref[...] = _dot_nt(dysb, wsb_ref[...]).astype(BF)
        dcs_ref[...] = _dot_nt(dyco, wco_ref[...])

    row = pl.BlockSpec((tm, D), lambda i: (i, 0))
    wfull = pl.BlockSpec((D, D), lambda i: (0, 0))
    return pl.pallas_call(
        body, name=name, grid=(T // tm,),
        in_specs=[row, pl.BlockSpec((tm, D), lambda i: (i, 2)), pl.BlockSpec((tm, D), lambda i: (i, 3)),
                  row, row, wfull, wfull, wfull],
        out_specs=[row] * 6,
        out_shape=[SDS((T, D), BF)] * 5 + [SDS((T, D), F32)],
        compiler_params=_cparams(("parallel",)))(do2, proj, proj, ysb, yco, wsb, wco, wout)


def att_bwd(proj, dyatt, car, Bl, S, D, name, comm=None):
    dh, cw, hp, nblk, tq, tk, nq, nk = _att_dims(S, D)
    scale = 1.0 / math.sqrt(dh)
    rider = _Rider(comm)

    def body(*refs):
        (q_ref, k_ref, v_ref, do_ref, car_ref, dq_ref, dk_ref, dv_ref,
         qs, ks, vs, dos, kts, dk_acc, dv_acc), ride = rider.split(refs, 5, 3, 7)
        ride(pl.program_id(0) * nblk + pl.program_id(1), Bl * nblk)
        usum, lsum, dmask = _att_consts(tk)
        for hh in range(hp):
            sl = slice(hh * dh, (hh + 1) * dh)
            qs[hh] = (q_ref[:, sl] * scale).astype(BF)
            ks[hh] = k_ref[:, sl].astype(BF)
            vs[hh] = v_ref[:, sl].astype(BF)
            dos[hh] = do_ref[:, sl]
            for kb in range(nk):
                kts[hh, kb] = k_ref[kb * tk:(kb + 1) * tk, sl].astype(F32).T.astype(BF)
        dk_acc[...] = jnp.zeros_like(dk_acc)
        dv_acc[...] = jnp.zeros_like(dv_acc)
        nch = tq // tk

        def qloop(qb, _):
            qo = pl.multiple_of(qb * tq, tq)
            n_full = qb * nch

            def blk(kb, state, diag):
                ko = pl.multiple_of(kb * tk, tk)
                chains = [(hh, c) for hh in range(hp) for c in range(0 if diag is None else diag, nch)]

                def masked(ch, val):
                    return jnp.where(dmask, val, 0.0) if ch[1] == diag else val

                def qrows(ref, ch):
                    return ref[ch[0], pl.ds(pl.multiple_of(qo + ch[1] * tk, tk), tk), :]

                k = [ks[hh, pl.ds(ko, tk), :] for hh in range(hp)]
                v = [vs[hh, pl.ds(ko, tk), :] for hh in range(hp)]
                z = {ch: _dot_nt(k[ch[0]], qrows(qs, ch)) for ch in chains}
                dw = {ch: _dot_nt(v[ch[0]], qrows(dos, ch)) for ch in chains}
                sp = {ch: _softplus(z[ch]) for ch in chains}
                lk = {ch: masked(ch, -sp[ch]) for ch in chains}
                cs = {ch: _split_dot(usum, lk[ch]) for ch in chains}
                w = {ch: masked(ch, jnp.exp((z[ch] - sp[ch])
                                            + car_ref[ch[0], qb * nk + kb, 0:1, ch[1] * tk:(ch[1] + 1) * tk]
                                            + cs[ch][:tk])) for ch in chains}
                dlw = {ch: dw[ch] * w[ch] for ch in chains}
                gs = {ch: _dot_nn(lsum, dlw[ch].astype(BF)) for ch in chains}
                sg = {ch: jnp.exp(z[ch] - sp[ch]) for ch in chains}
                dzb = {ch: masked(ch, dlw[ch] - sg[ch] * (dlw[ch] + state[ch[0]][ch[1]][0][0:1, :] + gs[ch][:tk])
                                  ).astype(BF) for ch in chains}
                wb = {ch: w[ch].astype(BF) for ch in chains}
                for hh in range(hp):
                    mine = [ch for ch in chains if ch[0] == hh]
                    dk_acc[hh, kb] += sum(_dot_nn(dzb[ch], qrows(qs, ch)) for ch in mine)
                    dv_acc[hh, kb] += sum(_dot_nn(wb[ch], qrows(dos, ch)) for ch in mine)
                dq = {ch: _dot_nn(kts[ch[0], kb], dzb[ch]) for ch in chains}
                return tuple(tuple(
                    (state[hh][c][0] + gs[(hh, c)][tk:], state[hh][c][1] + dq[(hh, c)]) if (hh, c) in z else state[hh][c]
                    for c in range(nch)) for hh in range(hp))

            state = tuple(tuple((jnp.zeros((8, tk), F32), jnp.zeros((dh, tk), F32)) for _ in range(nch))
                          for _ in range(hp))
            state = lax.fori_loop(0, n_full, lambda kb, st: blk(kb, st, None), state)
            for i in range(nch):
                state = blk(n_full + i, state, i)
            for hh in range(hp):
                for c in range(nch):
                    dq_ref[pl.ds(pl.multiple_of(qo + c * tk, tk), tk), hh * dh:(hh + 1) * dh] = (
                        (state[hh][c][1].T * scale).astype(BF))
            return 0

        lax.fori_loop(0, nq, qloop, 0)
        for hh in range(hp):
            sl = slice(hh * dh, (hh + 1) * dh)
            for kb in range(nk):
                dk_ref[kb * tk:(kb + 1) * tk, sl] = dk_acc[hh, kb].astype(BF)
                dv_ref[kb * tk:(kb + 1) * tk, sl] = dv_acc[hh, kb].astype(BF)

    def seg(s):
        return pl.BlockSpec((S, cw), lambda b, h: (b, s * nblk + h))

    blk_spec = pl.BlockSpec((S, cw), lambda b, h: (b, h))
    outs = pl.pallas_call(
        body, name=name, grid=(Bl, nblk),
        in_specs=[seg(0), seg(1), seg(2), blk_spec,
                  pl.BlockSpec((None, hp, nq * nk, 8, tq), lambda b, h: (b, h, 0, 0, 0))] + rider.in_specs,
        out_specs=[blk_spec, blk_spec, blk_spec] + rider.out_specs,
        out_shape=[SDS((Bl * S, D), BF)] * 3 + rider.out_shape,
        scratch_shapes=[pltpu.VMEM((hp, S, dh), BF)] * 4 + [pltpu.VMEM((hp, nk, dh, tk), BF)]
        + [pltpu.VMEM((hp, nk, tk, dh), F32)] * 2 + rider.scratch,
        compiler_params=_cparams(rider.sem(("parallel", "parallel"))))(proj, proj, proj, dyatt, car, *rider.ins)
    return rider.result(outs, 3)


def conv_ln_bwd(dcs, xhc, rstd_c, cg, cbeta, name):
    T, D = dcs.shape
    tm = min(256, T)

    def body(dcs_ref, xh_ref, rs_ref, g_ref, b_ref, dconv_ref, dg_ref, db_ref, dcb_ref):
        @pl.when(pl.program_id(0) == 0)
        def _():
            dg_ref[...] = jnp.zeros_like(dg_ref)
            db_ref[...] = jnp.zeros_like(db_ref)
            dcb_ref[...] = jnp.zeros_like(dcb_ref)
        xh = xh_ref[...]
        cl = xh * g_ref[...] + b_ref[...]
        s = _sig(cl)
        dcl = dcs_ref[...] * (s * (1.0 + cl * (1.0 - s)))
        dg_ref[...] += _rowsum(dcl * xh)
        db_ref[...] += _rowsum(dcl)
        dconv = _ln_bwd(dcl * g_ref[...], xh, rs_ref[...])
        dconv_ref[...] = dconv
        dcb_ref[...] += _rowsum(dconv)

    row = pl.BlockSpec((tm, D), lambda i: (i, 0))
    vec = _vec(D, 1)
    return pl.pallas_call(
        body, name=name, grid=(T // tm,),
        in_specs=[row, row, pl.BlockSpec((tm, 1), lambda i: (i, 0)), vec, vec],
        out_specs=[row, vec, vec, vec],
        out_shape=[SDS((T, D), F32)] + [SDS((1, D), F32)] * 3,
        compiler_params=_cparams(("arbitrary",)))(dcs, xhc, rstd_c, cg, cbeta)


def conv_bwd(dconv, proj, cw32, Bl, S, D, kw, name):
    T = Bl * S
    ts = min(128, S)
    ns = S // ts
    off = HALO - (kw - 1)
    rc = min(64, ts)
    cw = min(LANE, D)
    hpb = ts // HALO
    nhb = T // HALO

    def body(dc_ref, dcn_ref, a_ref, b_ref, ha_ref, hb_ref, w_ref, da_ref, db_ref, dw_ref, hsh, dsh, dh_s):
        b_ = pl.program_id(0)
        i = pl.program_id(1)
        span = ts + HALO - SUBLANES

        @pl.when((b_ == 0) & (i == 0))
        def _():
            dw_ref[...] = jnp.zeros_like(dw_ref)
        a = a_ref[...]
        sb = _sig(b_ref[...])
        h = a * sb
        hh = jnp.where(i == 0, 0.0, ha_ref[...] * _sig(hb_ref[...]))
        dc = dc_ref[...]
        dcn = jnp.where(i == ns - 1, 0.0, dcn_ref[...])
        for cb_ in range(D // cw):
            cols = slice(cb_ * cw, (cb_ + 1) * cw)
            hsh[0, cb_, pl.ds(HALO, ts), :] = h[:, cols]
            hsh[0, cb_, pl.ds(0, HALO), :] = hh[:, cols]
            dsh[0, cb_, pl.ds(0, ts), :] = dc[:, cols]
            dsh[0, cb_, pl.ds(ts, HALO), :] = dcn[:, cols]
            for s in range(1, SUBLANES):
                hsh[s, cb_, pl.ds(0, span), :] = hsh[0, cb_, pl.ds(s, span), :]
                dsh[s, cb_, pl.ds(0, span), :] = dsh[0, cb_, pl.ds(s, span), :]
            accs = [jnp.zeros((rc, cw), F32) for _ in range(ts // rc)]
            d0 = [dsh[0, cb_, pl.ds(r * rc, rc), :] for r in range(ts // rc)]
            for k in range(kw):
                wk = w_ref[k:k + 1, cols]
                wsum = jnp.zeros((rc, cw), F32)
                sd = ((kw - 1) - k) % SUBLANES
                sh_ = (off + k) % SUBLANES
                for r in range(ts // rc):
                    accs[r] = accs[r] + wk * dsh[sd, cb_, pl.ds(r * rc + (kw - 1) - k - sd, rc), :]
                    wsum = wsum + d0[r] * hsh[sh_, cb_, pl.ds(r * rc + off + k - sh_, rc), :]
                dw_ref[k:k + 1, cols] += _rowsum(wsum)
            for r in range(ts // rc):
                dh_s[pl.ds(r * rc, rc), cols] = accs[r]
        dhc = dh_s[...]
        da_ref[...] = (dhc * sb).astype(BF)
        db_ref[...] = (dhc * a * (sb * (1.0 - sb))).astype(BF)

    def tile(seg):
        return pl.BlockSpec((ts, D), lambda b, i: (b * ns + i, seg))

    def halo(seg):
        return pl.BlockSpec((HALO, D), lambda b, i: (jnp.maximum((b * ns + i) * hpb - 1, 0), seg))

    row = pl.BlockSpec((ts, D), lambda b, i: (b * ns + i, 0))
    nxt = pl.BlockSpec((HALO, D), lambda b, i: (jnp.minimum((b * ns + i + 1) * hpb, nhb - 1), 0))
    wspec = pl.BlockSpec((HALO, D), lambda b, i: (0, 0))
    return pl.pallas_call(
        body, name=name, grid=(Bl, ns),
        in_specs=[row, nxt, tile(0), tile(1), halo(0), halo(1), wspec],
        out_specs=[row, row, wspec],
        out_shape=[SDS((T, D), BF), SDS((T, D), BF), SDS((HALO, D), F32)],
        scratch_shapes=[pltpu.VMEM((SUBLANES, D // cw, ts + HALO, cw), F32)] * 2 + [pltpu.VMEM((ts, D), F32)],
        compiler_params=_cparams(("arbitrary", "arbitrary")))(dconv, dconv, proj, proj, proj, proj, cw32)


def matmul_tn(xa, ga, x_spec, g_spec, out_shape, out_spec, acc_shape, grid, name, comm=None):
    nk = grid[-1]
    rider = _Rider(comm)

    def body(*refs):
        (x_ref, g_ref, o_ref, acc), ride = rider.split(refs, 2, 1, 1)
        k = pl.program_id(1)
        ride(pl.program_id(0) * nk + k, grid[0] * nk)

        @pl.when(k == 0)
        def _():
            acc[...] = jnp.zeros_like(acc)
        acc[...] += _dot_tn(x_ref[...], g_ref[...])

        @pl.when(k == nk - 1)
        def _():
            o_ref[...] = acc[...]

    outs = pl.pallas_call(
        body, name=name, grid=grid, in_specs=[x_spec, g_spec] + rider.in_specs,
        out_specs=[out_spec] + rider.out_specs,
        out_shape=[SDS(out_shape, F32)] + rider.out_shape,
        scratch_shapes=[pltpu.VMEM(acc_shape, F32)] + rider.scratch,
        compiler_params=_cparams(rider.sem(("parallel", "arbitrary"))))(xa, ga, *rider.ins)
    own, landed = rider.result(outs, 1)
    return own[0] if comm is None else (own[0], landed)


def wgrad_std(xa, ga, name):
    T, M = xa.shape
    N = ga.shape[1]
    tk = min(2048, T)
    return matmul_tn(xa, ga, pl.BlockSpec((tk, M), lambda n, k: (k, 0)), pl.BlockSpec((tk, N), lambda n, k: (k, 0)),
                     (M, N), pl.BlockSpec((M, N), lambda n, k: (0, 0)), (M, N), (1, T // tk), name)


def wgrad_down(p4, df, name):
    nj, T, nb = p4.shape
    D = df.shape[1]
    tk = min(2048, T)
    return matmul_tn(p4, df, pl.BlockSpec((None, tk, nb), lambda j, k: (j, k, 0)),
                     pl.BlockSpec((tk, D), lambda j, k: (k, 0)),
                     (nj * nb, D), pl.BlockSpec((nb, D), lambda j, k: (j, 0)), (nb, D), (nj, T // tk), name)


def wgrad_gu(u, dh8, name, comm=None):
    n8, T, nb = dh8.shape
    D = u.shape[1]
    tk = min(2048, T)
    return matmul_tn(u, dh8, pl.BlockSpec((tk, D), lambda j, k: (k, 0)),
                     pl.BlockSpec((None, tk, nb), lambda j, k: (j, k, 0)),
                     (n8, D, nb), pl.BlockSpec((None, D, nb), lambda j, k: (j, 0, 0)), (D, nb), (n8, T // tk), name,
                     comm=comm)


def wgrad_segments(u, segs, name):
    T, D = u.shape
    ns = len(segs)
    tk = min(1024, T)
    nk = T // tk

    def body(*refs):
        x_ref, g_refs, o_ref, acc = refs[0], refs[1:1 + ns], refs[1 + ns], refs[2 + ns]
        s = pl.program_id(0)
        k = pl.program_id(1)

        @pl.when(k == 0)
        def _():
            acc[...] = jnp.zeros_like(acc)
        for i in range(ns):
            @pl.when(s == i)
            def _(i=i):
                acc[...] += _dot_tn(x_ref[...], g_refs[i][...])

        @pl.when(k == nk - 1)
        def _():
            o_ref[...] = acc[...]

    seg_specs = [pl.BlockSpec((tk, D), lambda s, k, i=i: (jnp.where(s == i, k, 0), 0)) for i in range(ns)]
    return pl.pallas_call(
        body, name=name, grid=(ns, nk),
        in_specs=[pl.BlockSpec((tk, D), lambda s, k: (k, 0))] + seg_specs,
        out_specs=pl.BlockSpec((D, D), lambda s, k: (0, s)),
        out_shape=SDS((D, ns * D), F32), scratch_shapes=[pltpu.VMEM((D, D), F32)],
        compiler_params=_cparams(("parallel", "arbitrary")))(u, *segs)


def kernel(x, c, w_ada, b_ada, ffn1_w_gu, ffn1_w_down, ln1_g, ln1_b, w_in, w_sb_out, conv_w, conv_b, conv_ln_g, conv_ln_b, w_conv_out, w_out, ln2_g, ln2_b, ffn2_w_gu, ffn2_w_down, ln3_g, ln3_b, loss_target, m_w_ada, m_b_ada, m_ffn1_w_gu, m_ffn1_w_down, m_ln1_g, m_ln1_b, m_w_in, m_w_sb_out, m_conv_w, m_conv_b, m_conv_ln_g, m_conv_ln_b, m_w_conv_out, m_w_out, m_ln2_g, m_ln2_b, m_ffn2_w_gu, m_ffn2_w_down, m_ln3_g, m_ln3_b, v_w_ada, v_b_ada, v_ffn1_w_gu, v_ffn1_w_down, v_ln1_g, v_ln1_b, v_w_in, v_w_sb_out, v_conv_w, v_conv_b, v_conv_ln_g, v_conv_ln_b, v_w_conv_out, v_w_out, v_ln2_g, v_ln2_b, v_ffn2_w_gu, v_ffn2_w_down, v_ln3_g, v_ln3_b):
    Bl, S, D = x.shape
    T = Bl * S
    kw = conv_w.shape[1]
    ax, ay, ac = lax.axis_index("x"), lax.axis_index("y"), lax.axis_index("c")
    me = 4 * ax + 2 * ay + ac
    qc = jnp.stack([2 * ax + ay, ac]).astype(jnp.int32)

    big = dict(ffn1_w_gu=ffn1_w_gu[0], ffn1_w_down=ffn1_w_down[0], w_in=w_in[0], w_sb_out=w_sb_out[0],
               w_conv_out=w_conv_out[0], w_out=w_out[0], ffn2_w_gu=ffn2_w_gu[0], ffn2_w_down=ffn2_w_down[0])
    big_m = dict(ffn1_w_gu=m_ffn1_w_gu[0], ffn1_w_down=m_ffn1_w_down[0], w_in=m_w_in[0], w_sb_out=m_w_sb_out[0],
                 w_conv_out=m_w_conv_out[0], w_out=m_w_out[0], ffn2_w_gu=m_ffn2_w_gu[0], ffn2_w_down=m_ffn2_w_down[0])
    big_v = dict(ffn1_w_gu=v_ffn1_w_gu[0], ffn1_w_down=v_ffn1_w_down[0], w_in=v_w_in[0], w_sb_out=v_w_sb_out[0],
                 w_conv_out=v_w_conv_out[0], w_out=v_w_out[0], ffn2_w_gu=v_ffn2_w_gu[0], ffn2_w_down=v_ffn2_w_down[0])
    names = list(big)

    layer1, layer2, layer3 = ["ffn1_w_gu", "ffn1_w_down"], ["w_in", "w_sb_out", "w_conv_out", "w_out"], \
        ["ffn2_w_gu", "ffn2_w_down"]

    def shards(group):
        return [big[n].astype(BF) for n in group]

    wave0, wave1 = ["ffn1_w_gu"], ["ffn1_w_down", "w_in"]
    wave2 = ["ffn2_w_gu", "ffn2_w_down", "w_sb_out", "w_conv_out", "w_out"]
    G = dict(zip(wave0, run_comm(gather_plan(shards(wave0)), "allgather_ffn1")))
    wg1 = G["ffn1_w_gu"]

    cw8 = small_allgather(conv_w[0], "allgather_conv_w")
    cw_full = jnp.transpose(cw8, (1, 0, 2)).reshape(kw, D)
    cw32 = jnp.concatenate([cw_full, jnp.zeros((HALO - kw, D), F32)], axis=0)

    c_all = small_allgather(c, "allgather_c").reshape(N_DEV * Bl, D)
    ncol = w_ada.shape[2]
    b_cols = lax.dynamic_slice(b_ada, (0, me * ncol), (1, ncol))
    mod_cols, s_all = ada_fwd(c_all, w_ada[0], b_cols, "ada_fwd")
    mod8 = small_allgather(mod_cols, "allgather_mod")
    mod_mine = lax.dynamic_slice(mod8, (0, me * Bl, 0), (N_DEV, Bl, ncol))
    mod = jnp.transpose(mod_mine, (1, 0, 2)).reshape(Bl, N_MOD_ROWS, 1, D)
    sh1, sc1, g1, sh2, sc2, g2, sh3, sc3, g3 = [mod[:, i] for i in range(N_MOD_ROWS)]

    ones = jnp.ones((1, D), F32)
    zeros = jnp.zeros((1, D), F32)
    xf = x.reshape(T, D)
    tgt = loss_target.reshape(T, D)

    (u1, a1, gg1, p1), landed = ffn_up(xf, ones, zeros, sc1, sh1, wg1, "ffn1_up", comm=gather_plan(shards(wave1)))
    G.update(zip(wave1, landed))
    wd1 = G["ffn1_w_down"].reshape(wg1.shape[0] // 2, wg1.shape[2], D)
    win = jnp.transpose(G["w_in"], (1, 0, 2)).reshape(D, -1)
    xh1, rs1, f1 = ffn_down_ln(p1, wd1, xf, ones, zeros, g1, "ffn1_down_ln")
    (u2, qkv, proj), landed = mod_matmul(xh1, ln1_g, ln1_b, sc2, sh2, win, "in_proj",
                                         comm=gather_plan(shards(wave2)))
    G.update(zip(wave2, landed))
    wg2 = G["ffn2_w_gu"]
    wd2 = G["ffn2_w_down"].reshape(wg2.shape[0] // 2, wg2.shape[2], D)
    wsb = G["w_sb_out"].reshape(D, D)
    wco = G["w_conv_out"].reshape(-1, D)
    wout = G["w_out"].reshape(D, D)
    yatt, car = att_fwd(qkv, Bl, S, D, "att_fwd")
    cs, xhc, rsc = conv_fwd(proj, cw32, conv_b, conv_ln_g, conv_ln_b, Bl, S, D, kw, "conv_fwd")
    xh2, rs2, ysb, yco, merged, o2 = mix_fwd(yatt, cs, proj, wsb, wco, wout, xh1, ln1_g, ln1_b, g2, "mix_fwd")
    (u3, a3, gg3, p3), _ = ffn_up(xh2, ln2_g, ln2_b, sc3, sh3, wg2, "ffn2_up")
    xh3, rs3, f3 = ffn_down_ln(p3, wd2, xh2, ln2_g, ln2_b, g3, "ffn2_down_ln")

    own_sum, recv_b = {}, {}

    def by_owner(group, grads):
        return [g.reshape((4, 2) + big[n].shape) for n, g in zip(group, grads)]

    def chip_sums(group, g42, recv_a):
        sums = [chip_sum(g, r, qc, "chip_sum_" + n) for g, r, n in zip(g42, recv_a, group)]
        own_sum.update({n: s[0] for n, s in zip(group, sums)})
        return chips_plan([s[1] for s in sums])

    dr3, df3, dln3g, dln3b, dg3, lossp, dh3 = ln_bwd(None, xh3, rs3, ln3_g, ln3_b, g3, f3, MACARON_WEIGHT,
                                                      "ln3_swiglu_bwd", target=tgt, swiglu=(wd2, a3, gg3))
    dh3 = dh3.reshape((-1,) + a3.shape[1:])
    gw_d2 = wgrad_down(p3, df3, "wgrad_ffn2_down")
    gw_g2 = wgrad_gu(u3, dh3, "wgrad_ffn2_gu")
    g42 = by_owner(layer3, [gw_g2, gw_d2])
    (dx2, dsc3, dsh3), recv_a = mod_bwd(dh3, wg2, dr3, xh2, ln2_g, ln2_b, sc3, True, "ffn2_mod_bwd",
                                        comm=sibling_plan(g42))
    plan3 = chip_sums(layer3, g42, recv_a)

    dr2, do2, dln2g, dln2b, dg2 = ln_bwd(dx2, xh2, rs2, ln2_g, ln2_b, g2, o2, 1.0, "ln2_bwd")
    gw_out = wgrad_std(merged, do2, "wgrad_out")
    dysb, dyco, dga, dgb, dyatt, dcs = merge_bwd(do2, proj, ysb, yco, wsb, wco, wout, "merge_bwd")
    gw_sb = wgrad_std(yatt, dysb, "wgrad_sb")
    gw_co = wgrad_std(cs, dyco, "wgrad_conv_out")
    (dq, dk, dv), landed = att_bwd(qkv, dyatt, car, Bl, S, D, "att_bwd", comm=plan3)
    recv_b.update(zip(layer3, landed))
    dconv, dcg, dcbeta, dcb = conv_ln_bwd(dcs, xhc, rsc, conv_ln_g, conv_ln_b, "conv_ln_bwd")
    dglu_a, dglu_b, dcw = conv_bwd(dconv, proj, cw32, Bl, S, D, kw, "conv_bwd")
    dproj = [dq, dk, dv, dglu_a, dglu_b, dga, dgb]
    gw_in = wgrad_segments(u2, dproj, "wgrad_in")
    gw_in = jnp.transpose(gw_in.reshape(D, N_DEV, -1), (1, 0, 2))
    g42 = by_owner(layer2, [gw_in, gw_sb, gw_co, gw_out])
    (dx1, dsc2, dsh2), recv_a = mod_bwd(dproj, win, dr2, xh1, ln1_g, ln1_b, sc2, False, "mix_mod_bwd",
                                        comm=sibling_plan(g42))
    plan2 = chip_sums(layer2, g42, recv_a)

    dr1, df1, dln1g, dln1b, dg1, dh1 = ln_bwd(dx1, xh1, rs1, ln1_g, ln1_b, g1, f1, MACARON_WEIGHT,
                                              "ln1_swiglu_bwd", swiglu=(wd1, a1, gg1))
    dh1 = dh1.reshape((-1,) + a1.shape[1:])
    gw_d1 = wgrad_down(p1, df1, "wgrad_ffn1_down")
    gw_g1, landed = wgrad_gu(u1, dh1, "wgrad_ffn1_gu", comm=plan2)
    recv_b.update(zip(layer2, landed))
    g42 = by_owner(layer1, [gw_g1, gw_d1])
    plan1 = chip_sums(layer1, g42, run_comm(sibling_plan(g42), "rs_sibling_ffn1"))
    (grad_x, dsc1, dsh1), landed = mod_bwd(dh1, wg1, dr1, xf, ones, zeros, sc1, True, "ffn1_mod_bwd", comm=plan1)
    recv_b.update(zip(layer1, landed))

    dmod = jnp.concatenate([dsh1, dsc1, dg1, dsh2, dsc2, dg2, dsh3, dsc3, dg3], axis=1).reshape(Bl, N_MOD_ROWS * D)
    dmod_all = small_allgather(dmod, "allgather_dmod").reshape(N_DEV * Bl, N_MOD_ROWS * D)
    dmod_cols = lax.dynamic_slice(dmod_all, (0, me * ncol), (N_DEV * Bl, ncol))
    g_w_ada, g_b_ada = ada_bwd(s_all, dmod_cols, dmod_all, "ada_bwd")

    npad = 16
    small_rows = [dln1g, dln1b, dcb, dcg, dcbeta, dln2g, dln2b, dln3g, dln3b,
                  jnp.broadcast_to(lossp[0:1, 0:1], (1, D))]
    pack = jnp.concatenate(small_rows + [jnp.zeros((npad - len(small_rows), D), F32), dcw], axis=0)
    small = small_sum(small_allgather(pack, "allgather_small"), "small_sum")
    loss = small[9, 0]
    small_w = [ln1_g, ln1_b, conv_b, conv_ln_g, conv_ln_b, ln2_g, ln2_b, ln3_g, ln3_b]
    small_m = [m_ln1_g, m_ln1_b, m_conv_b, m_conv_ln_g, m_conv_ln_b, m_ln2_g, m_ln2_b, m_ln3_g, m_ln3_b]
    small_v = [v_ln1_g, v_ln1_b, v_conv_b, v_conv_ln_g, v_conv_ln_b, v_ln2_g, v_ln2_b, v_ln3_g, v_ln3_b]
    padrows = jnp.zeros((npad - len(small_w), D), F32)
    s_g, s_d, s_m, s_v = adamw(jnp.concatenate(small_w + [padrows], axis=0),
                               jnp.concatenate(small_m + [padrows], axis=0),
                               jnp.concatenate(small_v + [padrows], axis=0),
                               [_plain_part(small[:npad])], "adamw_small")
    dcw_mine = lax.dynamic_slice(small[npad:npad + kw], (0, me * (D // N_DEV)), (kw, D // N_DEV))
    cw_g, cw_d, cw_m, cw_v = adamw(conv_w[0], m_conv_w[0], v_conv_w[0], [_plain_part(dcw_mine)], "adamw_conv_w")
    ada_g, ada_d, ada_m, ada_v = adamw(w_ada[0], m_w_ada[0], v_w_ada[0], [_plain_part(g_w_ada)], "adamw_w_ada")
    bada_g, bada_d, bada_m, bada_v = adamw(b_ada, m_b_ada, v_b_ada, [_plain_part(g_b_ada)], "adamw_b_ada")

    res = {}
    for n in names:
        rb = recv_b[n]
        parts = [_plain_part(own_sum[n]), _slot_part(rb, 0), _slot_part(rb, 1), _slot_part(rb, 2)]
        res[n] = adamw(big[n], big_m[n], big_v[n], parts, "adamw_" + n)

    def small_out(k):
        order = dict(ln1_g=0, ln1_b=1, conv_b=2, conv_ln_g=3, conv_ln_b=4, ln2_g=5, ln2_b=6, ln3_g=7, ln3_b=8)
        return lambda arr: arr[order[k]:order[k] + 1]

    weight_order = ["w_ada", "b_ada", "ffn1_w_gu", "ffn1_w_down", "ln1_g", "ln1_b", "w_in", "w_sb_out", "conv_w",
                    "conv_b", "conv_ln_g", "conv_ln_b", "w_conv_out", "w_out", "ln2_g", "ln2_b", "ffn2_w_gu",
                    "ffn2_w_down", "ln3_g", "ln3_b"]

    shapes = dict(w_ada=w_ada.shape, b_ada=b_ada.shape, conv_w=conv_w.shape, ln1_g=ln1_g.shape,
                  **{n: (1,) + big[n].shape for n in names})

    def pick(which):
        outs = []
        for n in weight_order:
            if n == "w_ada":
                a = (ada_g, ada_d, ada_m, ada_v)[which]
            elif n == "b_ada":
                a = (bada_g, bada_d, bada_m, bada_v)[which]
            elif n == "conv_w":
                a = (cw_g, cw_d, cw_m, cw_v)[which]
            elif n in res:
                a = res[n][which]
            else:
                a = small_out(n)((s_g, s_d, s_m, s_v)[which])
            outs.append(a.reshape(shapes.get(n, ln1_g.shape)))
        return outs

    return (loss, grad_x.reshape(Bl, S, D), *pick(0), *pick(1), *pick(2), *pick(3))
```

```python
import math

import jax
import jax.numpy as jnp
from jax import lax
from jax.experimental import pallas as pl
from jax.experimental.pallas import tpu as pltpu

F32 = jnp.float32
BF = jnp.bfloat16
SDS = jax.ShapeDtypeStruct
MESH = pl.DeviceIdType.MESH

N_DEV = 8
SB_HEAD_DIM = 64
N_MOD_ROWS = 9
MACARON_WEIGHT = 0.5
DEEPNORM_ALPHA = 2.0 ** 0.25
LN_EPS = 1e-5
ADAM_LR = 0.001
ADAM_B1 = 0.9
ADAM_B2 = 0.999
ADAM_EPS = 1e-08
ADAM_WD = 0.01
ADAM_STEP = 10

V7X_VMEM_LIMIT = 52 * 1024 * 1024
LANE = 128
SUBLANES = 8
HALO = 32


def _cparams(sem=None):
    return pltpu.CompilerParams(dimension_semantics=sem, vmem_limit_bytes=V7X_VMEM_LIMIT)


def _dot_nn(a, b):
    return lax.dot_general(a, b, (((1,), (0,)), ((), ())), preferred_element_type=F32)


def _dot_nt(a, b):
    return lax.dot_general(a, b, (((1,), (1,)), ((), ())), preferred_element_type=F32)


def _dot_tn(a, b):
    return lax.dot_general(a, b, (((0,), (0,)), ((), ())), preferred_element_type=F32)


def _sig(x):
    return 1.0 / (1.0 + jnp.exp(-x))


def _ln_stats(r):
    mu = jnp.mean(r, axis=-1, keepdims=True)
    d = r - mu
    var = jnp.mean(d * d, axis=-1, keepdims=True)
    rstd = lax.rsqrt(var + LN_EPS)
    return d * rstd, rstd


def _ln_bwd(dxh, xh, rstd):
    m1 = jnp.mean(dxh, axis=-1, keepdims=True)
    m2 = jnp.mean(dxh * xh, axis=-1, keepdims=True)
    return rstd * (dxh - m1 - xh * m2)


def _rowsum(v):
    return jnp.sum(v, axis=0, keepdims=True)


def _row_tile(n, cap):
    if n <= cap:
        return n
    best = None
    for t in range(8, cap + 1, 8):
        if n % t == 0:
            best = t
    assert best is not None, (n, cap)
    return best


def _coords():
    x, y, c = lax.axis_index("x"), lax.axis_index("y"), lax.axis_index("c")
    return x, y, c


def _flip(v, bit):
    return 1 - v if bit else v


def small_allgather(blk, name):
    r, n = blk.shape

    def body(x_ref, out_ref, send_sems, recv_sems):
        x, y, c = _coords()
        me = 4 * x + 2 * y + c
        out_ref[me] = x_ref[...]
        copies = []
        for k in range(1, N_DEV):
            peer = (_flip(x, k & 4), _flip(y, k & 2), _flip(c, k & 1))
            cp = pltpu.make_async_remote_copy(
                src_ref=x_ref, dst_ref=out_ref.at[me], send_sem=send_sems.at[k - 1],
                recv_sem=recv_sems.at[k - 1], device_id=peer, device_id_type=MESH)
            cp.start()
            copies.append(cp)
        for k in range(1, N_DEV):
            px, py, pc = _flip(x, k & 4), _flip(y, k & 2), _flip(c, k & 1)
            slot = 4 * px + 2 * py + pc
            pltpu.make_async_remote_copy(
                src_ref=x_ref, dst_ref=out_ref.at[slot], send_sem=send_sems.at[k - 1],
                recv_sem=recv_sems.at[k - 1], device_id=(px, py, pc), device_id_type=MESH).wait_recv()
        for cp in copies:
            cp.wait_send()

    return pl.pallas_call(
        body, name=name,
        out_shape=SDS((N_DEV, r, n), blk.dtype),
        in_specs=[pl.BlockSpec(memory_space=pltpu.VMEM)],
        out_specs=pl.BlockSpec(memory_space=pltpu.VMEM),
        scratch_shapes=[pltpu.SemaphoreType.DMA((N_DEV - 1,)), pltpu.SemaphoreType.DMA((N_DEV - 1,))],
    )(blk)


class CommPlan:
    def __init__(self, ins, out_shape, scratch, emit):
        self.ins, self.out_shape, self.scratch, self.emit = list(ins), list(out_shape), list(scratch), emit


def _phase(step, at, fn):
    if step is None:
        fn()
    else:
        pl.when(step == at)(fn)


def gather_plan(shards):
    n = len(shards)
    per = 7

    def emit(ins, outs, sems, step, nsteps):
        send_sems, recv_sems, local_sems = sems
        x, y, c = _coords()
        me = 4 * x + 2 * y + c
        sibling = (x, y, 1 - c)
        chips = [(1 - x, y), (x, 1 - y), (1 - x, 1 - y)]

        def slot(px, py, pc):
            return 4 * px + 2 * py + pc

        def copy(t, k, block, to, src=None):
            dst = outs[t].at[slot(*block)]
            return pltpu.make_async_remote_copy(
                src_ref=dst if src is None else src, dst_ref=dst,
                send_sem=send_sems.at[per * t + k], recv_sem=recv_sems.at[per * t + k],
                device_id=to, device_id_type=MESH)

        def local(t):
            return pltpu.make_async_copy(ins[t], outs[t].at[me], local_sems.at[t])

        def first(t):
            return [copy(t, 0, (x, y, c), sibling, src=ins[t])] + [
                copy(t, 1 + j, (x, y, c), (*chip, c), src=ins[t]) for j, chip in enumerate(chips)]

        def passed(t):
            return [copy(t, 4 + j, (*chip, c), sibling) for j, chip in enumerate(chips)]

        def start():
            for t in range(n):
                local(t).start()
                for cp in first(t):
                    cp.start()

        def forward():
            for t in range(n):
                for j, chip in enumerate(chips):
                    copy(t, 1 + j, (*chip, c), (x, y, c)).wait_recv()
                    passed(t)[j].start()

        def finish():
            for t in range(n):
                copy(t, 0, (x, y, 1 - c), (x, y, c)).wait_recv()
                for j, chip in enumerate(chips):
                    copy(t, 4 + j, (*chip, 1 - c), (x, y, c)).wait_recv()
            for t in range(n):
                for cp in first(t) + passed(t):
                    cp.wait_send()
                local(t).wait()

        _phase(step, 0, start)
        _phase(step, None if step is None else max(nsteps - max(2, nsteps // 8), 0), forward)
        _phase(step, None if step is None else nsteps - 1, finish)

    return CommPlan(shards, [SDS((N_DEV,) + s.shape, s.dtype) for s in shards],
                    [pltpu.SemaphoreType.DMA((per * n,)), pltpu.SemaphoreType.DMA((per * n,)),
                     pltpu.SemaphoreType.DMA((n,))], emit)


def chips_plan(sums):
    n = len(sums)

    def emit(ins, outs, sems, step, nsteps):
        send_sems, recv_sems = sems
        x, y, c = _coords()

        def copies():
            return [pltpu.make_async_remote_copy(
                src_ref=ins[t].at[j - 1], dst_ref=outs[t].at[j - 1], send_sem=send_sems.at[3 * t + j - 1],
                recv_sem=recv_sems.at[3 * t + j - 1], device_id=(_flip(x, j & 2), _flip(y, j & 1), c),
                device_id_type=MESH) for t in range(n) for j in range(1, 4)]

        def start():
            for cp in copies():
                cp.start()

        def finish():
            for cp in copies():
                cp.wait_recv()
            for cp in copies():
                cp.wait_send()

        _phase(step, 0, start)
        _phase(step, None if step is None else nsteps - 1, finish)

    return CommPlan(sums, [SDS(s.shape, s.dtype) for s in sums],
                    [pltpu.SemaphoreType.DMA((3 * n,)), pltpu.SemaphoreType.DMA((3 * n,))], emit)


def run_comm(plan, name):
    n, m = len(plan.ins), len(plan.out_shape)

    def body(*refs):
        plan.emit(refs[:n], refs[n:n + m], refs[n + m:], None, 1)

    anyspec = pl.BlockSpec(memory_space=pl.ANY)
    return pl.pallas_call(body, name=name, out_shape=plan.out_shape, in_specs=[anyspec] * n,
                          out_specs=[anyspec] * m, scratch_shapes=plan.scratch)(*plan.ins)


class _Rider:
    def __init__(self, plan):
        self.plan = plan
        anyspec = pl.BlockSpec(memory_space=pl.ANY)
        self.ins = plan.ins if plan else []
        self.in_specs = [anyspec] * len(self.ins)
        self.out_specs = [anyspec] * (len(plan.out_shape) if plan else 0)
        self.out_shape = plan.out_shape if plan else []
        self.scratch = plan.scratch if plan else []

    def split(self, refs, n_in, n_out, n_scratch=0):
        ni, no = len(self.ins), len(self.out_shape)
        own_in = refs[:n_in]
        c_in = refs[n_in:n_in + ni]
        own_out = refs[n_in + ni:n_in + ni + n_out]
        c_out = refs[n_in + ni + n_out:n_in + ni + n_out + no]
        rest = refs[n_in + ni + n_out + no:]
        own_scr, c_scr = rest[:n_scratch], rest[n_scratch:]

        def ride(step, nsteps):
            if self.plan:
                self.plan.emit(c_in, c_out, c_scr, step, nsteps)

        return tuple(own_in) + tuple(own_out) + tuple(own_scr), ride

    def result(self, outs, n_out):
        outs = list(outs) if isinstance(outs, (list, tuple)) else [outs]
        return outs[:n_out], (outs[n_out:] if self.plan else None)

    def sem(self, sem):
        return tuple("arbitrary" for _ in sem) if self.plan else sem


def sibling_plan(grads):
    n = len(grads)

    def emit(ins, outs, sems, step, nsteps):
        send_sems, recv_sems = sems
        x, y, c = _coords()

        def copies():
            return [pltpu.make_async_remote_copy(
                src_ref=ins[t].at[:, 1 - c], dst_ref=outs[t], send_sem=send_sems.at[t],
                recv_sem=recv_sems.at[t], device_id=(x, y, 1 - c), device_id_type=MESH) for t in range(n)]

        def start():
            for cp in copies():
                cp.start()

        def finish():
            for cp in copies():
                cp.wait_recv()
            for cp in copies():
                cp.wait_send()

        _phase(step, 0, start)
        _phase(step, None if step is None else nsteps - 1, finish)

    return CommPlan(grads, [SDS((4,) + g.shape[2:], g.dtype) for g in grads],
                    [pltpu.SemaphoreType.DMA((n,)), pltpu.SemaphoreType.DMA((n,))], emit)


def chip_sum(g42, recv, qc, name):
    _, _, R, C = g42.shape
    tr = _row_tile(R, 512)

    def body(qc_ref, a_ref, b_ref, own_ref, send_ref):
        j = pl.program_id(1)
        s = a_ref[...] + b_ref[...]

        @pl.when(j == 0)
        def _():
            own_ref[...] = s

        @pl.when(j > 0)
        def _():
            send_ref[...] = s.astype(BF)

    gs = pltpu.PrefetchScalarGridSpec(
        num_scalar_prefetch=1, grid=(R // tr, 4),
        in_specs=[pl.BlockSpec((None, None, tr, C), lambda i, j, s: (jnp.bitwise_xor(s[0], j), s[1], i, 0)),
                  pl.BlockSpec((None, tr, C), lambda i, j, s: (jnp.bitwise_xor(s[0], j), i, 0))],
        out_specs=[pl.BlockSpec((tr, C), lambda i, j, s: (i, 0)),
                   pl.BlockSpec((None, tr, C), lambda i, j, s: (jnp.maximum(j - 1, 0), i, 0))])
    return pl.pallas_call(body, name=name, grid_spec=gs, out_shape=[SDS((R, C), F32), SDS((3, R, C), BF)],
                          compiler_params=_cparams(("arbitrary", "arbitrary")))(qc, g42, recv)


def small_sum(g8, name):
    def body(g_ref, o_ref):
        acc = g_ref[0]
        for k in range(1, N_DEV):
            acc = acc + g_ref[k]
        o_ref[...] = acc
    return pl.pallas_call(body, name=name, out_shape=SDS(g8.shape[1:], F32))(g8)


def adamw(w, m, v, parts, name):
    R, C = w.shape
    tr = _row_tile(R, 512)
    npart = len(parts)
    c1 = 1.0 / (1.0 - ADAM_B1 ** ADAM_STEP)
    c2 = 1.0 / (1.0 - ADAM_B2 ** ADAM_STEP)

    def body(*refs):
        w_ref, m_ref, v_ref = refs[:3]
        p_refs = refs[3:3 + npart]
        g_ref, d_ref, nm_ref, nv_ref = refs[3 + npart:]
        g = p_refs[0][...].astype(F32)
        for p in p_refs[1:]:
            g = g + p[...].astype(F32)
        nm = ADAM_B1 * m_ref[...] + (1.0 - ADAM_B1) * g
        nv = ADAM_B2 * v_ref[...] + (1.0 - ADAM_B2) * (g * g)
        mh = nm * c1
        vh = nv * c2
        g_ref[...] = g
        nm_ref[...] = nm
        nv_ref[...] = nv
        d_ref[...] = -ADAM_LR * (mh / (jnp.sqrt(vh) + ADAM_EPS) + ADAM_WD * w_ref[...])

    wspec = pl.BlockSpec((tr, C), lambda i: (i, 0))
    pspecs = [pl.BlockSpec(bs(tr, C), im) for (_, bs, im) in parts]
    outs = pl.pallas_call(
        body, name=name, grid=(R // tr,),
        in_specs=[wspec] * 3 + pspecs, out_specs=[wspec] * 4,
        out_shape=[SDS((R, C), F32)] * 4,
        compiler_params=_cparams(("parallel",)))(w, m, v, *[p[0] for p in parts])
    return outs


def _plain_part(g):
    return (g, lambda tr, C: (tr, C), lambda i: (i, 0))


def _slot_part(g, slot):
    return (g, lambda tr, C: (None, tr, C), lambda i, s=slot: (s, i, 0))


def ada_fwd(c_all, w_cols, b_cols, name):
    Bg, D = c_all.shape
    n = w_cols.shape[1]

    def body(c_ref, w_ref, b_ref, o_ref, s_ref):
        cc = c_ref[...]
        s = cc * _sig(cc)
        s_ref[...] = s
        o_ref[...] = jnp.dot(s, w_ref[...], preferred_element_type=F32, precision=lax.Precision.HIGHEST) + b_ref[...]

    return pl.pallas_call(body, name=name, out_shape=[SDS((Bg, n), F32), SDS((Bg, D), F32)],
                          compiler_params=_cparams())(c_all, w_cols, b_cols)


def ada_bwd(s_all, dmod_cols, dmod_all, name):
    Bg, D = s_all.shape
    n = dmod_cols.shape[1]

    def body(s_ref, dc_ref, da_ref, gw_ref, gb_ref):
        gw_ref[...] = lax.dot_general(s_ref[...], dc_ref[...], (((0,), (0,)), ((), ())),
                                      preferred_element_type=F32, precision=lax.Precision.HIGHEST)
        acc = da_ref[0:1, :]
        for r in range(1, Bg):
            acc = acc + da_ref[r:r + 1, :]
        gb_ref[...] = acc

    return pl.pallas_call(body, name=name, out_shape=[SDS((D, n), F32), SDS((1, dmod_all.shape[1]), F32)],
                          compiler_params=_cparams())(s_all, dmod_cols, dmod_all)


def _vec(D, rank):
    return pl.BlockSpec((1, D), (lambda i: (0, 0)) if rank == 1 else (lambda i, j: (0, 0)))


def _modspec(D, tpb, rank):
    if rank == 1:
        return pl.BlockSpec((None, 1, D), lambda i: (i // tpb, 0, 0))
    return pl.BlockSpec((None, 1, D), lambda i, j: (i // tpb, 0, 0))


def _resident(shape):
    return pl.BlockSpec(shape, lambda *_: (0,) * len(shape), pipeline_mode=pl.Buffered(1))


def ffn_up(xs, pg, pb, sc, sh, wg8, name, comm=None):
    T, D = xs.shape
    n2, _, nb = wg8.shape
    nj = n2 // 2
    S = T // sc.shape[0]
    tm = min(512, S)
    tpb = S // tm
    rider = _Rider(comm)

    def body(*refs):
        (x_ref, pg_ref, pb_ref, sc_ref, sh_ref, w_ref, u_ref, a_ref, g_ref, p_ref), ride = rider.split(refs, 6, 4)
        ride(pl.program_id(0), T // tm)
        xin = x_ref[...] * pg_ref[...] + pb_ref[...]
        u_ref[...] = (xin * (1.0 + sc_ref[...]) + sh_ref[...]).astype(BF)

        def col_block(j, _):
            u = u_ref[...]
            a = _dot_nn(u, w_ref[j])
            g = _dot_nn(u, w_ref[j + nj])
            s = _sig(a)
            silu = a * s
            a_ref[j] = (g * (s * (1.0 + a * (1.0 - s)))).astype(BF)
            g_ref[j] = silu.astype(BF)
            p_ref[j] = (silu * g).astype(BF)
            return 0

        lax.fori_loop(0, nj, col_block, 0)

    blk = pl.BlockSpec((nj, tm, nb), lambda i: (0, i, 0))
    row = pl.BlockSpec((tm, D), lambda i: (i, 0))
    outs = pl.pallas_call(
        body, name=name, grid=(T // tm,),
        in_specs=[row, _vec(D, 1), _vec(D, 1), _modspec(D, tpb, 1), _modspec(D, tpb, 1), _resident(wg8.shape)]
        + rider.in_specs,
        out_specs=[row, blk, blk, blk] + rider.out_specs,
        out_shape=[SDS((T, D), BF)] + [SDS((nj, T, nb), BF)] * 3 + rider.out_shape,
        scratch_shapes=rider.scratch,
        compiler_params=_cparams(rider.sem(("parallel",))))(xs, pg, pb, sc, sh, wg8, *rider.ins)
    return rider.result(outs, 4)


def ffn_down_ln(p4, wd3, xs, pg, pb, gate, name, comm=None):
    nj, T, nb = p4.shape
    D = wd3.shape[2]
    S = T // gate.shape[0]
    tm = min(512, S)
    tpb = S // tm
    rider = _Rider(comm)

    def body(*refs):
        (p_ref, wd_ref, x_ref, pg_ref, pb_ref, gate_ref, xh_ref, rs_ref, f_ref), ride = rider.split(refs, 6, 3)
        ride(pl.program_id(0), T // tm)
        f = _dot_nn(p_ref[0], wd_ref[0])
        for k in range(1, nj):
            f = f + _dot_nn(p_ref[k], wd_ref[k])
        xin = x_ref[...] * pg_ref[...] + pb_ref[...]
        r = DEEPNORM_ALPHA * xin + gate_ref[...] * (MACARON_WEIGHT * f)
        xh, rstd = _ln_stats(r)
        xh_ref[...] = xh
        rs_ref[...] = rstd
        f_ref[...] = f.astype(BF)

    row = pl.BlockSpec((tm, D), lambda i: (i, 0))
    outs = pl.pallas_call(
        body, name=name, grid=(T // tm,),
        in_specs=[pl.BlockSpec((nj, tm, nb), lambda i: (0, i, 0)), _resident(wd3.shape),
                  row, _vec(D, 1), _vec(D, 1), _modspec(D, tpb, 1)] + rider.in_specs,
        out_specs=[row, pl.BlockSpec((tm, 1), lambda i: (i, 0)), row] + rider.out_specs,
        out_shape=[SDS((T, D), F32), SDS((T, 1), F32), SDS((T, D), BF)] + rider.out_shape,
        scratch_shapes=rider.scratch,
        compiler_params=_cparams(rider.sem(("parallel",))))(p4, wd3, xs, pg, pb, gate, *rider.ins)
    return rider.result(outs, 3)


N_QKV = 3


def mod_matmul(xs, pg, pb, sc, sh, w, name, comm=None):
    T, D = xs.shape
    N = w.shape[1]
    S = T // sc.shape[0]
    tm = min(256, S)
    tpb = S // tm
    rider = _Rider(comm)

    def body(*refs):
        (x_ref, pg_ref, pb_ref, sc_ref, sh_ref, w_ref, u_ref, qkv_ref, o_ref), ride = rider.split(refs, 6, 3)
        ride(pl.program_id(0), T // tm)
        xin = x_ref[...] * pg_ref[...] + pb_ref[...]
        u = (xin * (1.0 + sc_ref[...]) + sh_ref[...]).astype(BF)
        u_ref[...] = u
        for n in range(N // D):
            y = _dot_nn(u, w_ref[:, n * D:(n + 1) * D])
            if n < N_QKV:
                qkv_ref[:, n * D:(n + 1) * D] = y.astype(BF)
            else:
                o_ref[:, (n - N_QKV) * D:(n - N_QKV + 1) * D] = y

    row = pl.BlockSpec((tm, D), lambda i: (i, 0))
    outs = pl.pallas_call(
        body, name=name, grid=(T // tm,),
        in_specs=[row, _vec(D, 1), _vec(D, 1), _modspec(D, tpb, 1), _modspec(D, tpb, 1), _resident(w.shape)]
        + rider.in_specs,
        out_specs=[row, pl.BlockSpec((tm, N_QKV * D), lambda i: (i, 0)),
                   pl.BlockSpec((tm, N - N_QKV * D), lambda i: (i, 0))] + rider.out_specs,
        out_shape=[SDS((T, D), BF), SDS((T, N_QKV * D), BF), SDS((T, N - N_QKV * D), F32)] + rider.out_shape,
        scratch_shapes=rider.scratch,
        compiler_params=_cparams(rider.sem(("parallel",))))(xs, pg, pb, sc, sh, w, *rider.ins)
    return rider.result(outs, 3)


ATT_TQ = 1024
ATT_TK = 256


def _att_consts(tk):
    r = lax.broadcasted_iota(jnp.int32, (tk + 8, tk), 0)
    c = lax.broadcasted_iota(jnp.int32, (tk + 8, tk), 1)
    usum = jnp.where((r >= tk) | (c > r), 1.0, 0.0).astype(BF)
    lsum = jnp.where((r >= tk) | (c < r), 1.0, 0.0).astype(BF)
    dmask = lax.broadcasted_iota(jnp.int32, (tk, tk), 0) < lax.broadcasted_iota(jnp.int32, (tk, tk), 1)
    return usum, lsum, dmask


def _split_dot(m, v):
    hi = v.astype(BF)
    lo = (v - hi.astype(F32)).astype(BF)
    return _dot_nn(m, hi) + _dot_nn(m, lo)


def _softplus(z):
    return jnp.maximum(z, 0.0) + jnp.log(1.0 + jnp.exp(-jnp.abs(z)))


def _att_dims(S, D):
    dh = SB_HEAD_DIM
    cw = min(LANE, D)
    tq = min(ATT_TQ, S)
    tk = min(ATT_TK, tq)
    assert tq % tk == 0 and S % tq == 0
    return dh, cw, cw // dh, D // cw, tq, tk, S // tq, S // tk


def att_fwd(proj, Bl, S, D, name):
    dh, cw, hp, nblk, tq, tk, nq, nk = _att_dims(S, D)
    scale = 1.0 / math.sqrt(dh)
    assert math.log2(scale) == int(math.log2(scale))
    H = D // dh

    def body(q_ref, k_ref, v_ref, o_ref, car_ref, qs, ks, vts):
        usum, _, dmask = _att_consts(tk)
        for hh in range(hp):
            sl = slice(hh * dh, (hh + 1) * dh)
            qs[hh] = (q_ref[:, sl] * scale).astype(BF)
            ks[hh] = k_ref[:, sl].astype(BF)
            for kb in range(nk):
                vts[hh, kb] = v_ref[kb * tk:(kb + 1) * tk, sl].astype(F32).T.astype(BF)
        nch = tq // tk

        def qloop(qb, _):
            qo = pl.multiple_of(qb * tq, tq)
            n_full = qb * nch

            def blk(kb, state, diag):
                ko = pl.multiple_of(kb * tk, tk)
                chains = [(hh, c) for hh in range(hp) for c in range(0 if diag is None else diag, nch)]

                def masked(ch, val):
                    return jnp.where(dmask, val, 0.0) if ch[1] == diag else val

                z = {ch: _dot_nt(ks[ch[0], pl.ds(ko, tk), :], qs[ch[0], pl.ds(pl.multiple_of(qo + ch[1] * tk, tk), tk), :])
                     for ch in chains}
                sp = {ch: _softplus(z[ch]) for ch in chains}
                lk = {ch: masked(ch, -sp[ch]) for ch in chains}
                for hh, c in chains:
                    car_ref[hh, qb * nk + kb, :, c * tk:(c + 1) * tk] = state[hh][c][0]
                cs = {ch: _split_dot(usum, lk[ch]) for ch in chains}
                w = {ch: masked(ch, jnp.exp((z[ch] - sp[ch]) + state[ch[0]][ch[1]][0][0:1, :] + cs[ch][:tk]))
                     for ch in chains}
                pv = {ch: _dot_nn(vts[ch[0], kb], w[ch].astype(BF)) for ch in chains}
                return tuple(tuple(
                    (state[hh][c][0] + cs[(hh, c)][tk:], state[hh][c][1] + pv[(hh, c)]) if (hh, c) in z else state[hh][c]
                    for c in range(nch)) for hh in range(hp))

            state = tuple(tuple((jnp.zeros((8, tk), F32), jnp.zeros((dh, tk), F32)) for _ in range(nch))
                          for _ in range(hp))
            for i in reversed(range(nch)):
                state = blk(n_full + i, state, i)
            state = lax.fori_loop(0, n_full, lambda j, st: blk(n_full - 1 - j, st, None), state)
            for hh in range(hp):
                for c in range(nch):
                    o_ref[pl.ds(pl.multiple_of(qo + c * tk, tk), tk), hh * dh:(hh + 1) * dh] = (
                        state[hh][c][1].T.astype(BF))
            return 0

        lax.fori_loop(0, nq, qloop, 0)

    def seg(s):
        return pl.BlockSpec((S, cw), lambda b, h: (b, s * nblk + h))

    return pl.pallas_call(
        body, name=name, grid=(Bl, nblk),
        in_specs=[seg(0), seg(1), seg(2)],
        out_specs=[pl.BlockSpec((S, cw), lambda b, h: (b, h)),
                   pl.BlockSpec((None, hp, nq * nk, 8, tq), lambda b, h: (b, h, 0, 0, 0))],
        out_shape=[SDS((Bl * S, D), BF), SDS((Bl, H, nq * nk, 8, tq), F32)],
        scratch_shapes=[pltpu.VMEM((hp, S, dh), BF)] * 2 + [pltpu.VMEM((hp, nk, dh, tk), BF)],
        compiler_params=_cparams(("parallel", "parallel")))(proj, proj, proj)


def conv_fwd(proj, cw32, cb, cg, cbeta, Bl, S, D, kw, name):
    T = Bl * S
    ts = min(128, S)
    ns = S // ts
    off = HALO - (kw - 1)
    rc = min(64, ts)
    cw = min(LANE, D)

    def body(a_ref, b_ref, ha_ref, hb_ref, w_ref, cb_ref, g_ref, be_ref, cs_ref, xh_ref, rs_ref, hsh, conv_s):
        i = pl.program_id(1)
        h = a_ref[...] * _sig(b_ref[...])
        hh = jnp.where(i == 0, 0.0, ha_ref[...] * _sig(hb_ref[...]))
        for cb_ in range(D // cw):
            cols = slice(cb_ * cw, (cb_ + 1) * cw)
            hsh[0, cb_, pl.ds(HALO, ts), :] = h[:, cols]
            hsh[0, cb_, pl.ds(0, HALO), :] = hh[:, cols]
            for s in range(1, SUBLANES):
                hsh[s, cb_, pl.ds(0, ts + HALO - SUBLANES), :] = hsh[0, cb_, pl.ds(s, ts + HALO - SUBLANES), :]
            accs = [jnp.zeros((rc, cw), F32) for _ in range(ts // rc)]
            for k in range(kw):
                wk = w_ref[k:k + 1, cols]
                s = (off + k) % SUBLANES
                for r in range(ts // rc):
                    accs[r] = accs[r] + wk * hsh[s, cb_, pl.ds(r * rc + off + k - s, rc), :]
            for r in range(ts // rc):
                conv_s[pl.ds(r * rc, rc), cols] = accs[r]
        conv = conv_s[...] + cb_ref[...]
        xh, rstd = _ln_stats(conv)
        xh_ref[...] = xh
        rs_ref[...] = rstd
        cl = xh * g_ref[...] + be_ref[...]
        cs_ref[...] = (cl * _sig(cl)).astype(BF)

    hpb = ts // HALO

    def tile(seg):
        return pl.BlockSpec((ts, D), lambda b, i: (b * ns + i, seg))

    def halo(seg):
        return pl.BlockSpec((HALO, D), lambda b, i: (jnp.maximum((b * ns + i) * hpb - 1, 0), seg))

    row = pl.BlockSpec((ts, D), lambda b, i: (b * ns + i, 0))
    vec = pl.BlockSpec((1, D), lambda b, i: (0, 0))
    return pl.pallas_call(
        body, name=name, grid=(Bl, ns),
        in_specs=[tile(0), tile(1), halo(0), halo(1), pl.BlockSpec((HALO, D), lambda b, i: (0, 0)), vec, vec, vec],
        out_specs=[row, row, pl.BlockSpec((ts, 1), lambda b, i: (b * ns + i, 0))],
        out_shape=[SDS((T, D), BF), SDS((T, D), F32), SDS((T, 1), F32)],
        scratch_shapes=[pltpu.VMEM((SUBLANES, D // cw, ts + HALO, cw), F32), pltpu.VMEM((ts, D), F32)],
        compiler_params=_cparams(("parallel", "arbitrary")))(proj, proj, proj, proj, cw32, cb, cg, cbeta)


def mix_fwd(yatt, cs, proj, wsb, wco, wout, xs, pg, pb, gate, name):
    T, D = yatt.shape
    S = T // gate.shape[0]
    tm = min(256, S)
    tpb = S // tm

    def body(ya_ref, cs_ref, ga_ref, gb_ref, wsb_ref, wco_ref, wout_ref, x_ref, pg_ref, pb_ref, gate_ref,
             xh_ref, rs_ref, ysb_ref, yco_ref, mg_ref, o_ref):
        ysb = _dot_nn(ya_ref[...], wsb_ref[...])
        yco = _dot_nn(cs_ref[...], wco_ref[...])
        merged = _sig(ga_ref[...]) * ysb + _sig(gb_ref[...]) * yco
        mg = merged.astype(BF)
        o = _dot_nn(mg, wout_ref[...])
        xin = x_ref[...] * pg_ref[...] + pb_ref[...]
        r = DEEPNORM_ALPHA * xin + gate_ref[...] * o
        xh, rstd = _ln_stats(r)
        xh_ref[...] = xh
        rs_ref[...] = rstd
        ysb_ref[...] = ysb.astype(BF)
        yco_ref[...] = yco.astype(BF)
        mg_ref[...] = mg
        o_ref[...] = o.astype(BF)

    row = pl.BlockSpec((tm, D), lambda i: (i, 0))
    wfull = pl.BlockSpec((D, D), lambda i: (0, 0))
    return pl.pallas_call(
        body, name=name, grid=(T // tm,),
        in_specs=[row, row, pl.BlockSpec((tm, D), lambda i: (i, 2)), pl.BlockSpec((tm, D), lambda i: (i, 3)),
                  wfull, wfull, wfull, row, _vec(D, 1), _vec(D, 1), _modspec(D, tpb, 1)],
        out_specs=[row, pl.BlockSpec((tm, 1), lambda i: (i, 0)), row, row, row, row],
        out_shape=[SDS((T, D), F32), SDS((T, 1), F32)] + [SDS((T, D), BF)] * 4,
        compiler_params=_cparams(("parallel",)))(yatt, cs, proj, proj, wsb, wco, wout, xs, pg, pb, gate)


def ln_bwd(dout, xh, rstd, lng, lnb, gate, sub, res_w, name, target=None, swiglu=None):
    T, D = xh.shape
    Bl = gate.shape[0]
    S = T // Bl
    tm = min(256, S)
    tpb = S // tm
    first = target is not None
    n_in = 10 if swiglu else 7

    def body(*refs):
        do_ref, xh_ref, rs_ref, g_ref, b_ref, gate_ref, sub_ref = refs[:7]
        tg_ref = do_ref
        outs = refs[n_in:]
        dr_ref, ds_ref, dg_ref, db_ref, dgate_ref = outs[:5]
        loss_ref = outs[5] if first else None
        i = pl.program_id(0)
        xh_ = xh_ref[...]
        if first:
            diff = (xh_ * g_ref[...] + b_ref[...]) - tg_ref[...]
            lsum = jnp.sum(jnp.sum(diff * diff, axis=1, keepdims=True), axis=0, keepdims=True) * (0.5 / D)
            do = diff * (1.0 / D)
        else:
            do = do_ref[...]

        @pl.when(i == 0)
        def _():
            dg_ref[...] = jnp.zeros_like(dg_ref)
            db_ref[...] = jnp.zeros_like(db_ref)
            if first:
                loss_ref[...] = jnp.zeros_like(loss_ref)

        @pl.when(i % tpb == 0)
        def _():
            dgate_ref[...] = jnp.zeros_like(dgate_ref)

        if first:
            loss_ref[...] += jnp.broadcast_to(lsum, loss_ref.shape)
        dg_ref[...] += _rowsum(do * xh_)
        db_ref[...] += _rowsum(do)
        dr = _ln_bwd(do * g_ref[...], xh_, rs_ref[...])
        dr_ref[...] = dr
        ds_ref[...] = (dr * gate_ref[...] * res_w).astype(BF)
        dgate_ref[...] += _rowsum(dr * (res_w * sub_ref[...].astype(F32)))
        if swiglu:
            wd_ref, a_ref, gg_ref = refs[7:10]
            dh_ref = outs[-1]

            def col_block(j, _):
                dp = _dot_nt(ds_ref[...], wd_ref[j])
                dh_ref[0, j] = (dp * a_ref[j].astype(F32)).astype(BF)
                dh_ref[1, j] = (dp * gg_ref[j].astype(F32)).astype(BF)
                return 0

            lax.fori_loop(0, swiglu[1].shape[0], col_block, 0)

    row = pl.BlockSpec((tm, D), lambda i: (i, 0))
    vec = _vec(D, 1)
    mod = _modspec(D, tpb, 1)
    in_specs = [row, row, pl.BlockSpec((tm, 1), lambda i: (i, 0)), vec, vec, mod, row]
    out_specs = [row, row, vec, vec, mod]
    out_shape = [SDS((T, D), F32), SDS((T, D), BF), SDS((1, D), F32), SDS((1, D), F32), SDS((Bl, 1, D), F32)]
    if first:
        out_specs.append(pl.BlockSpec((8, LANE), lambda i: (0, 0)))
        out_shape.append(SDS((8, LANE), F32))
    extra = []
    if swiglu:
        wd3, a4, g4 = swiglu
        nj, _, nb = a4.shape
        blk = pl.BlockSpec((nj, tm, nb), lambda i: (0, i, 0))
        in_specs += [_resident(wd3.shape), blk, blk]
        out_specs.append(pl.BlockSpec((2, nj, tm, nb), lambda i: (0, 0, i, 0)))
        out_shape.append(SDS((2, nj, T, nb), BF))
        extra = [wd3, a4, g4]
    return pl.pallas_call(
        body, name=name, grid=(T // tm,), in_specs=in_specs, out_specs=out_specs, out_shape=out_shape,
        compiler_params=_cparams(("arbitrary",)))(target if first else dout, xh, rstd, lng, lnb, gate, sub, *extra)


def mod_bwd(dh, w, dr, xs, pg, pb, sc, blocked, name, comm=None):
    T, D = dr.shape
    Bl = sc.shape[0]
    S = T // Bl
    tm = min(512 if blocked else 256, S)
    tpb = S // tm
    row = pl.BlockSpec((tm, D), lambda i: (i, 0))
    if blocked:
        nk, _, kb = dh.shape
        dh_list, dh_specs = [dh], [pl.BlockSpec((nk, tm, kb), lambda i: (0, i, 0))]
    else:
        nk = len(dh)
        dh_list, dh_specs = list(dh), [row] * nk
    nd = len(dh_list)
    rider = _Rider(comm)

    def body(*refs):
        own, ride = rider.split(refs, nd + 6, 3)
        dh_refs = own[:nd]
        w_ref, dr_ref, x_ref, pg_ref, pb_ref, sc_ref, dx_ref, dsc_ref, dsh_ref = own[nd:]
        i = pl.program_id(0)
        ride(i, T // tm)

        def part(k):
            if blocked:
                return _dot_nt(dh_refs[0][k], w_ref[k])
            return _dot_nt(dh_refs[k][...], w_ref[:, k * D:(k + 1) * D])

        du = part(0)
        for k in range(1, nk):
            du = du + part(k)

        @pl.when(i % tpb == 0)
        def _():
            dsc_ref[...] = jnp.zeros_like(dsc_ref)
            dsh_ref[...] = jnp.zeros_like(dsh_ref)

        xin = x_ref[...] * pg_ref[...] + pb_ref[...]
        dx_ref[...] = DEEPNORM_ALPHA * dr_ref[...] + du * (1.0 + sc_ref[...])
        dsc_ref[...] += _rowsum(du * xin)
        dsh_ref[...] += _rowsum(du)

    mod = _modspec(D, tpb, 1)
    outs = pl.pallas_call(
        body, name=name, grid=(T // tm,),
        in_specs=dh_specs + [_resident(w.shape), row, row, _vec(D, 1), _vec(D, 1), mod] + rider.in_specs,
        out_specs=[row, mod, mod] + rider.out_specs,
        out_shape=[SDS((T, D), F32), SDS((Bl, 1, D), F32), SDS((Bl, 1, D), F32)] + rider.out_shape,
        scratch_shapes=rider.scratch,
        compiler_params=_cparams(("arbitrary",)))(*dh_list, w, dr, xs, pg, pb, sc, *rider.ins)
    return rider.result(outs, 3)


def merge_bwd(do2, proj, ysb, yco, wsb, wco, wout, name):
    T, D = do2.shape
    tm = min(256, T)

    def body(do_ref, ga_ref, gb_ref, ysb_ref, yco_ref, wsb_ref, wco_ref, wout_ref,
             dysb_ref, dyco_ref, dga_ref, dgb_ref, dya_ref, dcs_ref):
        dm = _dot_nt(do_ref[...], wout_ref[...])
        sa = _sig(ga_ref[...])
        sb = _sig(gb_ref[...])
        dysb = (dm * sa).astype(BF)
        dyco = (dm * sb).astype(BF)
        dysb_ref[...] = dysb
        dyco_ref[...] = dyco
        dga_ref[...] = (dm * ysb_ref[...].astype(F32) * (sa * (1.0 - sa))).astype(BF)
        dgb_ref[...] = (dm * yco_ref[...].astype(F32) * (sb * (1.0 - sb))).astype(BF)
        dya_ref[...] = _dot_nt(dysb, wsb_ref[...]).astype(BF)
        dcs_ref[...] = _dot_nt(dyco, wco_ref[...])

    row = pl.BlockSpec((tm, D), lambda i: (i, 0))
    wfull = pl.BlockSpec((D, D), lambda i: (0, 0))
    return pl.pallas_call(
        body, name=name, grid=(T // tm,),
        in_specs=[row, pl.BlockSpec((tm, D), lambda i: (i, 2)), pl.BlockSpec((tm, D), lambda i: (i, 3)),
                  row, row, wfull, wfull, wfull],
        out_specs=[row] * 6,
        out_shape=[SDS((T, D), BF)] * 5 + [SDS((T, D), F32)],
        compiler_params=_cparams(("parallel",)))(do2, proj, proj, ysb, yco, wsb, wco, wout)


def att_bwd(proj, dyatt, car, Bl, S, D, name, comm=None):
    dh, cw, hp, nblk, tq, tk, nq, nk = _att_dims(S, D)
    scale = 1.0 / math.sqrt(dh)
    rider = _Rider(comm)

    def body(*refs):
        (q_ref, k_ref, v_ref, do_ref, car_ref, dq_ref, dk_ref, dv_ref,
         qs, ks, vs, dos, kts, dk_acc, dv_acc), ride = rider.split(refs, 5, 3, 7)
        ride(pl.program_id(0) * nblk + pl.program_id(1), Bl * nblk)
        usum, lsum, dmask = _att_consts(tk)
        for hh in range(hp):
            sl = slice(hh * dh, (hh + 1) * dh)
            qs[hh] = (q_ref[:, sl] * scale).astype(BF)
            ks[hh] = k_ref[:, sl].astype(BF)
            vs[hh] = v_ref[:, sl].astype(BF)
            dos[hh] = do_ref[:, sl]
            for kb in range(nk):
                kts[hh, kb] = k_ref[kb * tk:(kb + 1) * tk, sl].astype(F32).T.astype(BF)
        dk_acc[...] = jnp.zeros_like(dk_acc)
        dv_acc[...] = jnp.zeros_like(dv_acc)
        nch = tq // tk

        def qloop(qb, _):
            qo = pl.multiple_of(qb * tq, tq)
            n_full = qb * nch

            def blk(kb, state, diag):
                ko = pl.multiple_of(kb * tk, tk)
                chains = [(hh, c) for hh in range(hp) for c in range(0 if diag is None else diag, nch)]

                def masked(ch, val):
                    return jnp.where(dmask, val, 0.0) if ch[1] == diag else val

                def qrows(ref, ch):
                    return ref[ch[0], pl.ds(pl.multiple_of(qo + ch[1] * tk, tk), tk), :]

                k = [ks[hh, pl.ds(ko, tk), :] for hh in range(hp)]
                v = [vs[hh, pl.ds(ko, tk), :] for hh in range(hp)]
                z = {ch: _dot_nt(k[ch[0]], qrows(qs, ch)) for ch in chains}
                dw = {ch: _dot_nt(v[ch[0]], qrows(dos, ch)) for ch in chains}
                sp = {ch: _softplus(z[ch]) for ch in chains}
                lk = {ch: masked(ch, -sp[ch]) for ch in chains}
                cs = {ch: _split_dot(usum, lk[ch]) for ch in chains}
                w = {ch: masked(ch, jnp.exp((z[ch] - sp[ch])
                                            + car_ref[ch[0], qb * nk + kb, 0:1, ch[1] * tk:(ch[1] + 1) * tk]
                                            + cs[ch][:tk])) for ch in chains}
                dlw = {ch: dw[ch] * w[ch] for ch in chains}
                gs = {ch: _dot_nn(lsum, dlw[ch].astype(BF)) for ch in chains}
                sg = {ch: jnp.exp(z[ch] - sp[ch]) for ch in chains}
                dzb = {ch: masked(ch, dlw[ch] - sg[ch] * (dlw[ch] + state[ch[0]][ch[1]][0][0:1, :] + gs[ch][:tk])
                                  ).astype(BF) for ch in chains}
                wb = {ch: w[ch].astype(BF) for ch in chains}
                for hh in range(hp):
                    mine = [ch for ch in chains if ch[0] == hh]
                    dk_acc[hh, kb] += sum(_dot_nn(dzb[ch], qrows(qs, ch)) for ch in mine)
                    dv_acc[hh, kb] += sum(_dot_nn(wb[ch], qrows(dos, ch)) for ch in mine)
                dq = {ch: _dot_nn(kts[ch[0], kb], dzb[ch]) for ch in chains}
                return tuple(tuple(
                    (state[hh][c][0] + gs[(hh, c)][tk:], state[hh][c][1] + dq[(hh, c)]) if (hh, c) in z else state[hh][c]
                    for c in range(nch)) for hh in range(hp))

            state = tuple(tuple((jnp.zeros((8, tk), F32), jnp.zeros((dh, tk), F32)) for _ in range(nch))
                          for _ in range(hp))
            state = lax.fori_loop(0, n_full, lambda kb, st: blk(kb, st, None), state)
            for i in range(nch):
                state = blk(n_full + i, state, i)
            for hh in range(hp):
                for c in range(nch):
                    dq_ref[pl.ds(pl.multiple_of(qo + c * tk, tk), tk), hh * dh:(hh + 1) * dh] = (
                        (state[hh][c][1].T * scale).astype(BF))
            return 0

        lax.fori_loop(0, nq, qloop, 0)
        for hh in range(hp):
            sl = slice(hh * dh, (hh + 1) * dh)
            for kb in range(nk):
                dk_ref[kb * tk:(kb + 1) * tk, sl] = dk_acc[hh, kb].astype(BF)
                dv_ref[kb * tk:(kb + 1) * tk, sl] = dv_acc[hh, kb].astype(BF)

    def seg(s):
        return pl.BlockSpec((S, cw), lambda b, h: (b, s * nblk + h))

    blk_spec = pl.BlockSpec((S, cw), lambda b, h: (b, h))
    outs = pl.pallas_call(
        body, name=name, grid=(Bl, nblk),
        in_specs=[seg(0), seg(1), seg(2), blk_spec,
                  pl.BlockSpec((None, hp, nq * nk, 8, tq), lambda b, h: (b, h, 0, 0, 0))] + rider.in_specs,
        out_specs=[blk_spec, blk_spec, blk_spec] + rider.out_specs,
        out_shape=[SDS((Bl * S, D), BF)] * 3 + rider.out_shape,
        scratch_shapes=[pltpu.VMEM((hp, S, dh), BF)] * 4 + [pltpu.VMEM((hp, nk, dh, tk), BF)]
        + [pltpu.VMEM((hp, nk, tk, dh), F32)] * 2 + rider.scratch,
        compiler_params=_cparams(rider.sem(("parallel", "parallel"))))(proj, proj, proj, dyatt, car, *rider.ins)
    return rider.result(outs, 3)


def conv_ln_bwd(dcs, xhc, rstd_c, cg, cbeta, name):
    T, D = dcs.shape
    tm = min(256, T)

    def body(dcs_ref, xh_ref, rs_ref, g_ref, b_ref, dconv_ref, dg_ref, db_ref, dcb_ref):
        @pl.when(pl.program_id(0) == 0)
        def _():
            dg_ref[...] = jnp.zeros_like(dg_ref)
            db_ref[...] = jnp.zeros_like(db_ref)
            dcb_ref[...] = jnp.zeros_like(dcb_ref)
        xh = xh_ref[...]
        cl = xh * g_ref[...] + b_ref[...]
        s = _sig(cl)
        dcl = dcs_ref[...] * (s * (1.0 + cl * (1.0 - s)))
        dg_ref[...] += _rowsum(dcl * xh)
        db_ref[...] += _rowsum(dcl)
        dconv = _ln_bwd(dcl * g_ref[...], xh, rs_ref[...])
        dconv_ref[...] = dconv
        dcb_ref[...] += _rowsum(dconv)

    row = pl.BlockSpec((tm, D), lambda i: (i, 0))
    vec = _vec(D, 1)
    return pl.pallas_call(
        body, name=name, grid=(T // tm,),
        in_specs=[row, row, pl.BlockSpec((tm, 1), lambda i: (i, 0)), vec, vec],
        out_specs=[row, vec, vec, vec],
        out_shape=[SDS((T, D), F32)] + [SDS((1, D), F32)] * 3,
        compiler_params=_cparams(("arbitrary",)))(dcs, xhc, rstd_c, cg, cbeta)


def conv_bwd(dconv, proj, cw32, Bl, S, D, kw, name):
    T = Bl * S
    ts = min(128, S)
    ns = S // ts
    off = HALO - (kw - 1)
    rc = min(64, ts)
    cw = min(LANE, D)
    hpb = ts // HALO
    nhb = T // HALO

    def body(dc_ref, dcn_ref, a_ref, b_ref, ha_ref, hb_ref, w_ref, da_ref, db_ref, dw_ref, hsh, dsh, dh_s):
        b_ = pl.program_id(0)
        i = pl.program_id(1)
        span = ts + HALO - SUBLANES

        @pl.when((b_ == 0) & (i == 0))
        def _():
            dw_ref[...] = jnp.zeros_like(dw_ref)
        a = a_ref[...]
        sb = _sig(b_ref[...])
        h = a * sb
        hh = jnp.where(i == 0, 0.0, ha_ref[...] * _sig(hb_ref[...]))
        dc = dc_ref[...]
        dcn = jnp.where(i == ns - 1, 0.0, dcn_ref[...])
        for cb_ in range(D // cw):
            cols = slice(cb_ * cw, (cb_ + 1) * cw)
            hsh[0, cb_, pl.ds(HALO, ts), :] = h[:, cols]
            hsh[0, cb_, pl.ds(0, HALO), :] = hh[:, cols]
            dsh[0, cb_, pl.ds(0, ts), :] = dc[:, cols]
            dsh[0, cb_, pl.ds(ts, HALO), :] = dcn[:, cols]
            for s in range(1, SUBLANES):
                hsh[s, cb_, pl.ds(0, span), :] = hsh[0, cb_, pl.ds(s, span), :]
                dsh[s, cb_, pl.ds(0, span), :] = dsh[0, cb_, pl.ds(s, span), :]
            accs = [jnp.zeros((rc, cw), F32) for _ in range(ts // rc)]
            d0 = [dsh[0, cb_, pl.ds(r * rc, rc), :] for r in range(ts // rc)]
            for k in range(kw):
                wk = w_ref[k:k + 1, cols]
                wsum = jnp.zeros((rc, cw), F32)
                sd = ((kw - 1) - k) % SUBLANES
                sh_ = (off + k) % SUBLANES
                for r in range(ts // rc):
                    accs[r] = accs[r] + wk * dsh[sd, cb_, pl.ds(r * rc + (kw - 1) - k - sd, rc), :]
                    wsum = wsum + d0[r] * hsh[sh_, cb_, pl.ds(r * rc + off + k - sh_, rc), :]
                dw_ref[k:k + 1, cols] += _rowsum(wsum)
            for r in range(ts // rc):
                dh_s[pl.ds(r * rc, rc), cols] = accs[r]
        dhc = dh_s[...]
        da_ref[...] = (dhc * sb).astype(BF)
        db_ref[...] = (dhc * a * (sb * (1.0 - sb))).astype(BF)

    def tile(seg):
        return pl.BlockSpec((ts, D), lambda b, i: (b * ns + i, seg))

    def halo(seg):
        return pl.BlockSpec((HALO, D), lambda b, i: (jnp.maximum((b * ns + i) * hpb - 1, 0), seg))

    row = pl.BlockSpec((ts, D), lambda b, i: (b * ns + i, 0))
    nxt = pl.BlockSpec((HALO, D), lambda b, i: (jnp.minimum((b * ns + i + 1) * hpb, nhb - 1), 0))
    wspec = pl.BlockSpec((HALO, D), lambda b, i: (0, 0))
    return pl.pallas_call(
        body, name=name, grid=(Bl, ns),
        in_specs=[row, nxt, tile(0), tile(1), halo(0), halo(1), wspec],
        out_specs=[row, row, wspec],
        out_shape=[SDS((T, D), BF), SDS((T, D), BF), SDS((HALO, D), F32)],
        scratch_shapes=[pltpu.VMEM((SUBLANES, D // cw, ts + HALO, cw), F32)] * 2 + [pltpu.VMEM((ts, D), F32)],
        compiler_params=_cparams(("arbitrary", "arbitrary")))(dconv, dconv, proj, proj, proj, proj, cw32)


def matmul_tn(xa, ga, x_spec, g_spec, out_shape, out_spec, acc_shape, grid, name, comm=None):
    nk = grid[-1]
    rider = _Rider(comm)

    def body(*refs):
        (x_ref, g_ref, o_ref, acc), ride = rider.split(refs, 2, 1, 1)
        k = pl.program_id(1)
        ride(pl.program_id(0) * nk + k, grid[0] * nk)

        @pl.when(k == 0)
        def _():
            acc[...] = jnp.zeros_like(acc)
        acc[...] += _dot_tn(x_ref[...], g_ref[...])

        @pl.when(k == nk - 1)
        def _():
            o_ref[...] = acc[...]

    outs = pl.pallas_call(
        body, name=name, grid=grid, in_specs=[x_spec, g_spec] + rider.in_specs,
        out_specs=[out_spec] + rider.out_specs,
        out_shape=[SDS(out_shape, F32)] + rider.out_shape,
        scratch_shapes=[pltpu.VMEM(acc_shape, F32)] + rider.scratch,
        compiler_params=_cparams(rider.sem(("parallel", "arbitrary"))))(xa, ga, *rider.ins)
    own, landed = rider.result(outs, 1)
    return own[0] if comm is None else (own[0], landed)


def wgrad_std(xa, ga, name):
    T, M = xa.shape
    N = ga.shape[1]
    tk = min(2048, T)
    return matmul_tn(xa, ga, pl.BlockSpec((tk, M), lambda n, k: (k, 0)), pl.BlockSpec((tk, N), lambda n, k: (k, 0)),
                     (M, N), pl.BlockSpec((M, N), lambda n, k: (0, 0)), (M, N), (1, T // tk), name)


def wgrad_down(p4, df, name):
    nj, T, nb = p4.shape
    D = df.shape[1]
    tk = min(2048, T)
    return matmul_tn(p4, df, pl.BlockSpec((None, tk, nb), lambda j, k: (j, k, 0)),
                     pl.BlockSpec((tk, D), lambda j, k: (k, 0)),
                     (nj * nb, D), pl.BlockSpec((nb, D), lambda j, k: (j, 0)), (nb, D), (nj, T // tk), name)


def wgrad_gu(u, dh8, name, comm=None):
    n8, T, nb = dh8.shape
    D = u.shape[1]
    tk = min(2048, T)
    return matmul_tn(u, dh8, pl.BlockSpec((tk, D), lambda j, k: (k, 0)),
                     pl.BlockSpec((None, tk, nb), lambda j, k: (j, k, 0)),
                     (n8, D, nb), pl.BlockSpec((None, D, nb), lambda j, k: (j, 0, 0)), (D, nb), (n8, T // tk), name,
                     comm=comm)


def wgrad_segments(u, segs, name):
    T, D = u.shape
    ns = len(segs)
    tk = min(1024, T)
    nk = T // tk

    def body(*refs):
        x_ref, g_refs, o_ref, acc = refs[0], refs[1:1 + ns], refs[1 + ns], refs[2 + ns]
        s = pl.program_id(0)
        k = pl.program_id(1)

        @pl.when(k == 0)
        def _():
            acc[...] = jnp.zeros_like(acc)
        for i in range(ns):
            @pl.when(s == i)
            def _(i=i):
                acc[...] += _dot_tn(x_ref[...], g_refs[i][...])

        @pl.when(k == nk - 1)
        def _():
            o_ref[...] = acc[...]

    seg_specs = [pl.BlockSpec((tk, D), lambda s, k, i=i: (jnp.where(s == i, k, 0), 0)) for i in range(ns)]
    return pl.pallas_call(
        body, name=name, grid=(ns, nk),
        in_specs=[pl.BlockSpec((tk, D), lambda s, k: (k, 0))] + seg_specs,
        out_specs=pl.BlockSpec((D, D), lambda s, k: (0, s)),
        out_shape=SDS((D, ns * D), F32), scratch_shapes=[pltpu.VMEM((D, D), F32)],
        compiler_params=_cparams(("parallel", "arbitrary")))(u, *segs)


def kernel(x, c, w_ada, b_ada, ffn1_w_gu, ffn1_w_down, ln1_g, ln1_b, w_in, w_sb_out, conv_w, conv_b, conv_ln_g, conv_ln_b, w_conv_out, w_out, ln2_g, ln2_b, ffn2_w_gu, ffn2_w_down, ln3_g, ln3_b, loss_target, m_w_ada, m_b_ada, m_ffn1_w_gu, m_ffn1_w_down, m_ln1_g, m_ln1_b, m_w_in, m_w_sb_out, m_conv_w, m_conv_b, m_conv_ln_g, m_conv_ln_b, m_w_conv_out, m_w_out, m_ln2_g, m_ln2_b, m_ffn2_w_gu, m_ffn2_w_down, m_ln3_g, m_ln3_b, v_w_ada, v_b_ada, v_ffn1_w_gu, v_ffn1_w_down, v_ln1_g, v_ln1_b, v_w_in, v_w_sb_out, v_conv_w, v_conv_b, v_conv_ln_g, v_conv_ln_b, v_w_conv_out, v_w_out, v_ln2_g, v_ln2_b, v_ffn2_w_gu, v_ffn2_w_down, v_ln3_g, v_ln3_b):
    Bl, S, D = x.shape
    T = Bl * S
    kw = conv_w.shape[1]
    ax, ay, ac = lax.axis_index("x"), lax.axis_index("y"), lax.axis_index("c")
    me = 4 * ax + 2 * ay + ac
    qc = jnp.stack([2 * ax + ay, ac]).astype(jnp.int32)

    big = dict(ffn1_w_gu=ffn1_w_gu[0], ffn1_w_down=ffn1_w_down[0], w_in=w_in[0], w_sb_out=w_sb_out[0],
               w_conv_out=w_conv_out[0], w_out=w_out[0], ffn2_w_gu=ffn2_w_gu[0], ffn2_w_down=ffn2_w_down[0])
    big_m = dict(ffn1_w_gu=m_ffn1_w_gu[0], ffn1_w_down=m_ffn1_w_down[0], w_in=m_w_in[0], w_sb_out=m_w_sb_out[0],
                 w_conv_out=m_w_conv_out[0], w_out=m_w_out[0], ffn2_w_gu=m_ffn2_w_gu[0], ffn2_w_down=m_ffn2_w_down[0])
    big_v = dict(ffn1_w_gu=v_ffn1_w_gu[0], ffn1_w_down=v_ffn1_w_down[0], w_in=v_w_in[0], w_sb_out=v_w_sb_out[0],
                 w_conv_out=v_w_conv_out[0], w_out=v_w_out[0], ffn2_w_gu=v_ffn2_w_gu[0], ffn2_w_down=v_ffn2_w_down[0])
    names = list(big)

    layer1, layer2, layer3 = ["ffn1_w_gu", "ffn1_w_down"], ["w_in", "w_sb_out", "w_conv_out", "w_out"], \
        ["ffn2_w_gu", "ffn2_w_down"]

    def shards(group):
        return [big[n].astype(BF) for n in group]

    wave0, wave1 = ["ffn1_w_gu"], ["ffn1_w_down", "w_in"]
    wave2, wave3 = ["w_sb_out", "w_conv_out", "w_out"], ["ffn2_w_gu", "ffn2_w_down"]
    G = dict(zip(wave0, run_comm(gather_plan(shards(wave0)), "allgather_ffn1")))
    wg1 = G["ffn1_w_gu"]

    cw8 = small_allgather(conv_w[0], "allgather_conv_w")
    cw_full = jnp.transpose(cw8, (1, 0, 2)).reshape(kw, D)
    cw32 = jnp.concatenate([cw_full, jnp.zeros((HALO - kw, D), F32)], axis=0)

    c_all = small_allgather(c, "allgather_c").reshape(N_DEV * Bl, D)
    ncol = w_ada.shape[2]
    b_cols = lax.dynamic_slice(b_ada, (0, me * ncol), (1, ncol))
    mod_cols, s_all = ada_fwd(c_all, w_ada[0], b_cols, "ada_fwd")
    mod8 = small_allgather(mod_cols, "allgather_mod")
    mod_mine = lax.dynamic_slice(mod8, (0, me * Bl, 0), (N_DEV, Bl, ncol))
    mod = jnp.transpose(mod_mine, (1, 0, 2)).reshape(Bl, N_MOD_ROWS, 1, D)
    sh1, sc1, g1, sh2, sc2, g2, sh3, sc3, g3 = [mod[:, i] for i in range(N_MOD_ROWS)]

    ones = jnp.ones((1, D), F32)
    zeros = jnp.zeros((1, D), F32)
    xf = x.reshape(T, D)
    tgt = loss_target.reshape(T, D)

    (u1, a1, gg1, p1), landed = ffn_up(xf, ones, zeros, sc1, sh1, wg1, "ffn1_up", comm=gather_plan(shards(wave1)))
    G.update(zip(wave1, landed))
    wd1 = G["ffn1_w_down"].reshape(wg1.shape[0] // 2, wg1.shape[2], D)
    win = jnp.transpose(G["w_in"], (1, 0, 2)).reshape(D, -1)
    (xh1, rs1, f1), landed = ffn_down_ln(p1, wd1, xf, ones, zeros, g1, "ffn1_down_ln",
                                         comm=gather_plan(shards(wave2)))
    G.update(zip(wave2, landed))
    wsb = G["w_sb_out"].reshape(D, D)
    wco = G["w_conv_out"].reshape(-1, D)
    wout = G["w_out"].reshape(D, D)
    (u2, qkv, proj), landed = mod_matmul(xh1, ln1_g, ln1_b, sc2, sh2, win, "in_proj",
                                         comm=gather_plan(shards(wave3)))
    G.update(zip(wave3, landed))
    wg2 = G["ffn2_w_gu"]
    wd2 = G["ffn2_w_down"].reshape(wg2.shape[0] // 2, wg2.shape[2], D)
    yatt, car = att_fwd(qkv, Bl, S, D, "att_fwd")
    cs, xhc, rsc = conv_fwd(proj, cw32, conv_b, conv_ln_g, conv_ln_b, Bl, S, D, kw, "conv_fwd")
    xh2, rs2, ysb, yco, merged, o2 = mix_fwd(yatt, cs, proj, wsb, wco, wout, xh1, ln1_g, ln1_b, g2, "mix_fwd")
    (u3, a3, gg3, p3), _ = ffn_up(xh2, ln2_g, ln2_b, sc3, sh3, wg2, "ffn2_up")
    (xh3, rs3, f3), _ = ffn_down_ln(p3, wd2, xh2, ln2_g, ln2_b, g3, "ffn2_down_ln")

    own_sum, recv_b = {}, {}

    def by_owner(group, grads):
        return [g.reshape((4, 2) + big[n].shape) for n, g in zip(group, grads)]

    def chip_sums(group, g42, recv_a):
        sums = [chip_sum(g, r, qc, "chip_sum_" + n) for g, r, n in zip(g42, recv_a, group)]
        own_sum.update({n: s[0] for n, s in zip(group, sums)})
        return chips_plan([s[1] for s in sums])

    dr3, df3, dln3g, dln3b, dg3, lossp, dh3 = ln_bwd(None, xh3, rs3, ln3_g, ln3_b, g3, f3, MACARON_WEIGHT,
                                                      "ln3_swiglu_bwd", target=tgt, swiglu=(wd2, a3, gg3))
    dh3 = dh3.reshape((-1,) + a3.shape[1:])
    gw_d2 = wgrad_down(p3, df3, "wgrad_ffn2_down")
    gw_g2 = wgrad_gu(u3, dh3, "wgrad_ffn2_gu")
    g42 = by_owner(layer3, [gw_g2, gw_d2])
    (dx2, dsc3, dsh3), recv_a = mod_bwd(dh3, wg2, dr3, xh2, ln2_g, ln2_b, sc3, True, "ffn2_mod_bwd",
                                        comm=sibling_plan(g42))
    plan3 = chip_sums(layer3, g42, recv_a)

    dr2, do2, dln2g, dln2b, dg2 = ln_bwd(dx2, xh2, rs2, ln2_g, ln2_b, g2, o2, 1.0, "ln2_bwd")
    gw_out = wgrad_std(merged, do2, "wgrad_out")
    dysb, dyco, dga, dgb, dyatt, dcs = merge_bwd(do2, proj, ysb, yco, wsb, wco, wout, "merge_bwd")
    gw_sb = wgrad_std(yatt, dysb, "wgrad_sb")
    gw_co = wgrad_std(cs, dyco, "wgrad_conv_out")
    (dq, dk, dv), landed = att_bwd(qkv, dyatt, car, Bl, S, D, "att_bwd", comm=plan3)
    recv_b.update(zip(layer3, landed))
    dconv, dcg, dcbeta, dcb = conv_ln_bwd(dcs, xhc, rsc, conv_ln_g, conv_ln_b, "conv_ln_bwd")
    dglu_a, dglu_b, dcw = conv_bwd(dconv, proj, cw32, Bl, S, D, kw, "conv_bwd")
    dproj = [dq, dk, dv, dglu_a, dglu_b, dga, dgb]
    gw_in = wgrad_segments(u2, dproj, "wgrad_in")
    gw_in = jnp.transpose(gw_in.reshape(D, N_DEV, -1), (1, 0, 2))
    g42 = by_owner(layer2, [gw_in, gw_sb, gw_co, gw_out])
    (dx1, dsc2, dsh2), recv_a = mod_bwd(dproj, win, dr2, xh1, ln1_g, ln1_b, sc2, False, "mix_mod_bwd",
                                        comm=sibling_plan(g42))
    plan2 = chip_sums(layer2, g42, recv_a)

    dr1, df1, dln1g, dln1b, dg1, dh1 = ln_bwd(dx1, xh1, rs1, ln1_g, ln1_b, g1, f1, MACARON_WEIGHT,
                                              "ln1_swiglu_bwd", swiglu=(wd1, a1, gg1))
    dh1 = dh1.reshape((-1,) + a1.shape[1:])
    gw_d1 = wgrad_down(p1, df1, "wgrad_ffn1_down")
    gw_g1, landed = wgrad_gu(u1, dh1, "wgrad_ffn1_gu", comm=plan2)
    recv_b.update(zip(layer2, landed))
    g42 = by_owner(layer1, [gw_g1, gw_d1])
    plan1 = chip_sums(layer1, g42, run_comm(sibling_plan(g42), "rs_sibling_ffn1"))
    (grad_x, dsc1, dsh1), landed = mod_bwd(dh1, wg1, dr1, xf, ones, zeros, sc1, True, "ffn1_mod_bwd", comm=plan1)
    recv_b.update(zip(layer1, landed))

    dmod = jnp.concatenate([dsh1, dsc1, dg1, dsh2, dsc2, dg2, dsh3, dsc3, dg3], axis=1).reshape(Bl, N_MOD_ROWS * D)
    dmod_all = small_allgather(dmod, "allgather_dmod").reshape(N_DEV * Bl, N_MOD_ROWS * D)
    dmod_cols = lax.dynamic_slice(dmod_all, (0, me * ncol), (N_DEV * Bl, ncol))
    g_w_ada, g_b_ada = ada_bwd(s_all, dmod_cols, dmod_all, "ada_bwd")

    npad = 16
    small_rows = [dln1g, dln1b, dcb, dcg, dcbeta, dln2g, dln2b, dln3g, dln3b,
                  jnp.broadcast_to(lossp[0:1, 0:1], (1, D))]
    pack = jnp.concatenate(small_rows + [jnp.zeros((npad - len(small_rows), D), F32), dcw], axis=0)
    small = small_sum(small_allgather(pack, "allgather_small"), "small_sum")
    loss = small[9, 0]
    small_w = [ln1_g, ln1_b, conv_b, conv_ln_g, conv_ln_b, ln2_g, ln2_b, ln3_g, ln3_b]
    small_m = [m_ln1_g, m_ln1_b, m_conv_b, m_conv_ln_g, m_conv_ln_b, m_ln2_g, m_ln2_b, m_ln3_g, m_ln3_b]
    small_v = [v_ln1_g, v_ln1_b, v_conv_b, v_conv_ln_g, v_conv_ln_b, v_ln2_g, v_ln2_b, v_ln3_g, v_ln3_b]
    padrows = jnp.zeros((npad - len(small_w), D), F32)
    s_g, s_d, s_m, s_v = adamw(jnp.concatenate(small_w + [padrows], axis=0),
                               jnp.concatenate(small_m + [padrows], axis=0),
                               jnp.concatenate(small_v + [padrows], axis=0),
                               [_plain_part(small[:npad])], "adamw_small")
    dcw_mine = lax.dynamic_slice(small[npad:npad + kw], (0, me * (D // N_DEV)), (kw, D // N_DEV))
    cw_g, cw_d, cw_m, cw_v = adamw(conv_w[0], m_conv_w[0], v_conv_w[0], [_plain_part(dcw_mine)], "adamw_conv_w")
    ada_g, ada_d, ada_m, ada_v = adamw(w_ada[0], m_w_ada[0], v_w_ada[0], [_plain_part(g_w_ada)], "adamw_w_ada")
    bada_g, bada_d, bada_m, bada_v = adamw(b_ada, m_b_ada, v_b_ada, [_plain_part(g_b_ada)], "adamw_b_ada")

    res = {}
    for n in names:
        rb = recv_b[n]
        parts = [_plain_part(own_sum[n]), _slot_part(rb, 0), _slot_part(rb, 1), _slot_part(rb, 2)]
        res[n] = adamw(big[n], big_m[n], big_v[n], parts, "adamw_" + n)

    def small_out(k):
        order = dict(ln1_g=0, ln1_b=1, conv_b=2, conv_ln_g=3, conv_ln_b=4, ln2_g=5, ln2_b=6, ln3_g=7, ln3_b=8)
        return lambda arr: arr[order[k]:order[k] + 1]

    weight_order = ["w_ada", "b_ada", "ffn1_w_gu", "ffn1_w_down", "ln1_g", "ln1_b", "w_in", "w_sb_out", "conv_w",
                    "conv_b", "conv_ln_g", "conv_ln_b", "w_conv_out", "w_out", "ln2_g", "ln2_b", "ffn2_w_gu",
                    "ffn2_w_down", "ln3_g", "ln3_b"]

    shapes = dict(w_ada=w_ada.shape, b_ada=b_ada.shape, conv_w=conv_w.shape, ln1_g=ln1_g.shape,
                  **{n: (1,) + big[n].shape for n in names})

    def pick(which):
        outs = []
        for n in weight_order:
            if n == "w_ada":
                a = (ada_g, ada_d, ada_m, ada_v)[which]
            elif n == "b_ada":
                a = (bada_g, bada_d, bada_m, bada_v)[which]
            elif n == "conv_w":
                a = (cw_g, cw_d, cw_m, cw_v)[which]
            elif n in res:
                a = res[n][which]
            else:
                a = small_out(n)((s_g, s_d, s_m, s_v)[which])
            outs.append(a.reshape(shapes.get(n, ln1_g.shape)))
        return outs

    return (loss, grad_x.reshape(Bl, S, D), *pick(0), *pick(1), *pick(2), *pick(3))
```

```python
import math

import jax
import jax.numpy as jnp
from jax import lax
from jax.experimental import pallas as pl
from jax.experimental.pallas import tpu as pltpu

F32 = jnp.float32
BF = jnp.bfloat16
SDS = jax.ShapeDtypeStruct
MESH = pl.DeviceIdType.MESH

N_DEV = 8
SB_HEAD_DIM = 64
N_MOD_ROWS = 9
MACARON_WEIGHT = 0.5
DEEPNORM_ALPHA = 2.0 ** 0.25
LN_EPS = 1e-5
ADAM_LR = 0.001
ADAM_B1 = 0.9
ADAM_B2 = 0.999
ADAM_EPS = 1e-08
ADAM_WD = 0.01
ADAM_STEP = 10

V7X_VMEM_LIMIT = 52 * 1024 * 1024
LANE = 128
SUBLANES = 8
HALO = 32


def _cparams(sem=None):
    return pltpu.CompilerParams(dimension_semantics=sem, vmem_limit_bytes=V7X_VMEM_LIMIT)


def _dot_nn(a, b):
    return lax.dot_general(a, b, (((1,), (0,)), ((), ())), preferred_element_type=F32)


def _dot_nt(a, b):
    return lax.dot_general(a, b, (((1,), (1,)), ((), ())), preferred_element_type=F32)


def _dot_tn(a, b):
    return lax.dot_general(a, b, (((0,), (0,)), ((), ())), preferred_element_type=F32)


def _sig(x):
    return 1.0 / (1.0 + jnp.exp(-x))


def _ln_stats(r):
    mu = jnp.mean(r, axis=-1, keepdims=True)
    d = r - mu
    var = jnp.mean(d * d, axis=-1, keepdims=True)
    rstd = lax.rsqrt(var + LN_EPS)
    return d * rstd, rstd


def _ln_bwd(dxh, xh, rstd):
    m1 = jnp.mean(dxh, axis=-1, keepdims=True)
    m2 = jnp.mean(dxh * xh, axis=-1, keepdims=True)
    return rstd * (dxh - m1 - xh * m2)


def _rowsum(v):
    return jnp.sum(v, axis=0, keepdims=True)


def _row_tile(n, cap):
    if n <= cap:
        return n
    best = None
    for t in range(8, cap + 1, 8):
        if n % t == 0:
            best = t
    assert best is not None, (n, cap)
    return best


def _coords():
    x, y, c = lax.axis_index("x"), lax.axis_index("y"), lax.axis_index("c")
    return x, y, c


def _flip(v, bit):
    return 1 - v if bit else v


def small_allgather(blk, name):
    r, n = blk.shape

    def body(x_ref, out_ref, send_sems, recv_sems):
        x, y, c = _coords()
        me = 4 * x + 2 * y + c
        out_ref[me] = x_ref[...]
        copies = []
        for k in range(1, N_DEV):
            peer = (_flip(x, k & 4), _flip(y, k & 2), _flip(c, k & 1))
            cp = pltpu.make_async_remote_copy(
                src_ref=x_ref, dst_ref=out_ref.at[me], send_sem=send_sems.at[k - 1],
                recv_sem=recv_sems.at[k - 1], device_id=peer, device_id_type=MESH)
            cp.start()
            copies.append(cp)
        for k in range(1, N_DEV):
            px, py, pc = _flip(x, k & 4), _flip(y, k & 2), _flip(c, k & 1)
            slot = 4 * px + 2 * py + pc
            pltpu.make_async_remote_copy(
                src_ref=x_ref, dst_ref=out_ref.at[slot], send_sem=send_sems.at[k - 1],
                recv_sem=recv_sems.at[k - 1], device_id=(px, py, pc), device_id_type=MESH).wait_recv()
        for cp in copies:
            cp.wait_send()

    return pl.pallas_call(
        body, name=name,
        out_shape=SDS((N_DEV, r, n), blk.dtype),
        in_specs=[pl.BlockSpec(memory_space=pltpu.VMEM)],
        out_specs=pl.BlockSpec(memory_space=pltpu.VMEM),
        scratch_shapes=[pltpu.SemaphoreType.DMA((N_DEV - 1,)), pltpu.SemaphoreType.DMA((N_DEV - 1,))],
    )(blk)


class CommPlan:
    def __init__(self, ins, out_shape, scratch, emit):
        self.ins, self.out_shape, self.scratch, self.emit = list(ins), list(out_shape), list(scratch), emit


def _phase(step, at, fn):
    if step is None:
        fn()
    else:
        pl.when(step == at)(fn)


def gather_plan(shards):
    n = len(shards)
    per = 7

    def emit(ins, outs, sems, step, nsteps):
        send_sems, recv_sems, local_sems = sems
        x, y, c = _coords()
        me = 4 * x + 2 * y + c
        sibling = (x, y, 1 - c)
        chips = [(1 - x, y), (x, 1 - y), (1 - x, 1 - y)]

        def slot(px, py, pc):
            return 4 * px + 2 * py + pc

        def copy(t, k, block, to, src=None):
            dst = outs[t].at[slot(*block)]
            return pltpu.make_async_remote_copy(
                src_ref=dst if src is None else src, dst_ref=dst,
                send_sem=send_sems.at[per * t + k], recv_sem=recv_sems.at[per * t + k],
                device_id=to, device_id_type=MESH)

        def local(t):
            return pltpu.make_async_copy(ins[t], outs[t].at[me], local_sems.at[t])

        def first(t):
            return [copy(t, 0, (x, y, c), sibling, src=ins[t])] + [
                copy(t, 1 + j, (x, y, c), (*chip, c), src=ins[t]) for j, chip in enumerate(chips)]

        def passed(t):
            return [copy(t, 4 + j, (*chip, c), sibling) for j, chip in enumerate(chips)]

        def start():
            for t in range(n):
                local(t).start()
                for cp in first(t):
                    cp.start()

        def forward():
            for t in range(n):
                for j, chip in enumerate(chips):
                    copy(t, 1 + j, (*chip, c), (x, y, c)).wait_recv()
                    passed(t)[j].start()

        def finish():
            for t in range(n):
                copy(t, 0, (x, y, 1 - c), (x, y, c)).wait_recv()
                for j, chip in enumerate(chips):
                    copy(t, 4 + j, (*chip, 1 - c), (x, y, c)).wait_recv()
            for t in range(n):
                for cp in first(t) + passed(t):
                    cp.wait_send()
                local(t).wait()

        _phase(step, 0, start)
        _phase(step, None if step is None else max(nsteps - max(2, nsteps // 8), 0), forward)
        _phase(step, None if step is None else nsteps - 1, finish)

    return CommPlan(shards, [SDS((N_DEV,) + s.shape, s.dtype) for s in shards],
                    [pltpu.SemaphoreType.DMA((per * n,)), pltpu.SemaphoreType.DMA((per * n,)),
                     pltpu.SemaphoreType.DMA((n,))], emit)


def chips_plan(sums):
    n = len(sums)

    def emit(ins, outs, sems, step, nsteps):
        send_sems, recv_sems = sems
        x, y, c = _coords()

        def copies():
            return [pltpu.make_async_remote_copy(
                src_ref=ins[t].at[j - 1], dst_ref=outs[t].at[j - 1], send_sem=send_sems.at[3 * t + j - 1],
                recv_sem=recv_sems.at[3 * t + j - 1], device_id=(_flip(x, j & 2), _flip(y, j & 1), c),
                device_id_type=MESH) for t in range(n) for j in range(1, 4)]

        def start():
            for cp in copies():
                cp.start()

        def finish():
            for cp in copies():
                cp.wait_recv()
            for cp in copies():
                cp.wait_send()

        _phase(step, 0, start)
        _phase(step, None if step is None else nsteps - 1, finish)

    return CommPlan(sums, [SDS(s.shape, s.dtype) for s in sums],
                    [pltpu.SemaphoreType.DMA((3 * n,)), pltpu.SemaphoreType.DMA((3 * n,))], emit)


def run_comm(plan, name):
    n, m = len(plan.ins), len(plan.out_shape)

    def body(*refs):
        plan.emit(refs[:n], refs[n:n + m], refs[n + m:], None, 1)

    anyspec = pl.BlockSpec(memory_space=pl.ANY)
    return pl.pallas_call(body, name=name, out_shape=plan.out_shape, in_specs=[anyspec] * n,
                          out_specs=[anyspec] * m, scratch_shapes=plan.scratch)(*plan.ins)


class _Rider:
    def __init__(self, plan):
        self.plan = plan
        anyspec = pl.BlockSpec(memory_space=pl.ANY)
        self.ins = plan.ins if plan else []
        self.in_specs = [anyspec] * len(self.ins)
        self.out_specs = [anyspec] * (len(plan.out_shape) if plan else 0)
        self.out_shape = plan.out_shape if plan else []
        self.scratch = plan.scratch if plan else []

    def split(self, refs, n_in, n_out, n_scratch=0):
        ni, no = len(self.ins), len(self.out_shape)
        own_in = refs[:n_in]
        c_in = refs[n_in:n_in + ni]
        own_out = refs[n_in + ni:n_in + ni + n_out]
        c_out = refs[n_in + ni + n_out:n_in + ni + n_out + no]
        rest = refs[n_in + ni + n_out + no:]
        own_scr, c_scr = rest[:n_scratch], rest[n_scratch:]

        def ride(step, nsteps):
            if self.plan:
                self.plan.emit(c_in, c_out, c_scr, step, nsteps)

        return tuple(own_in) + tuple(own_out) + tuple(own_scr), ride

    def result(self, outs, n_out):
        outs = list(outs) if isinstance(outs, (list, tuple)) else [outs]
        return outs[:n_out], (outs[n_out:] if self.plan else None)

    def sem(self, sem):
        return tuple("arbitrary" for _ in sem) if self.plan else sem


def sibling_plan(grads):
    n = len(grads)

    def emit(ins, outs, sems, step, nsteps):
        send_sems, recv_sems = sems
        x, y, c = _coords()

        def copies():
            return [pltpu.make_async_remote_copy(
                src_ref=ins[t].at[:, 1 - c], dst_ref=outs[t], send_sem=send_sems.at[t],
                recv_sem=recv_sems.at[t], device_id=(x, y, 1 - c), device_id_type=MESH) for t in range(n)]

        def start():
            for cp in copies():
                cp.start()

        def finish():
            for cp in copies():
                cp.wait_recv()
            for cp in copies():
                cp.wait_send()

        _phase(step, 0, start)
        _phase(step, None if step is None else nsteps - 1, finish)

    return CommPlan(grads, [SDS((4,) + g.shape[2:], g.dtype) for g in grads],
                    [pltpu.SemaphoreType.DMA((n,)), pltpu.SemaphoreType.DMA((n,))], emit)


def chip_sum(g42, recv, qc, name):
    _, _, R, C = g42.shape
    tr = _row_tile(R, 512)

    def body(qc_ref, a_ref, b_ref, own_ref, send_ref):
        j = pl.program_id(1)
        s = a_ref[...] + b_ref[...]

        @pl.when(j == 0)
        def _():
            own_ref[...] = s

        @pl.when(j > 0)
        def _():
            send_ref[...] = s.astype(BF)

    gs = pltpu.PrefetchScalarGridSpec(
        num_scalar_prefetch=1, grid=(R // tr, 4),
        in_specs=[pl.BlockSpec((None, None, tr, C), lambda i, j, s: (jnp.bitwise_xor(s[0], j), s[1], i, 0)),
                  pl.BlockSpec((None, tr, C), lambda i, j, s: (jnp.bitwise_xor(s[0], j), i, 0))],
        out_specs=[pl.BlockSpec((tr, C), lambda i, j, s: (i, 0)),
                   pl.BlockSpec((None, tr, C), lambda i, j, s: (jnp.maximum(j - 1, 0), i, 0))])
    return pl.pallas_call(body, name=name, grid_spec=gs, out_shape=[SDS((R, C), F32), SDS((3, R, C), BF)],
                          compiler_params=_cparams(("arbitrary", "arbitrary")))(qc, g42, recv)


def small_sum(g8, name):
    def body(g_ref, o_ref):
        acc = g_ref[0]
        for k in range(1, N_DEV):
            acc = acc + g_ref[k]
        o_ref[...] = acc
    return pl.pallas_call(body, name=name, out_shape=SDS(g8.shape[1:], F32))(g8)


def adamw(w, m, v, parts, name):
    R, C = w.shape
    tr = _row_tile(R, 512)
    npart = len(parts)
    c1 = 1.0 / (1.0 - ADAM_B1 ** ADAM_STEP)
    c2 = 1.0 / (1.0 - ADAM_B2 ** ADAM_STEP)

    def body(*refs):
        w_ref, m_ref, v_ref = refs[:3]
        p_refs = refs[3:3 + npart]
        g_ref, d_ref, nm_ref, nv_ref = refs[3 + npart:]
        g = p_refs[0][...].astype(F32)
        for p in p_refs[1:]:
            g = g + p[...].astype(F32)
        nm = ADAM_B1 * m_ref[...] + (1.0 - ADAM_B1) * g
        nv = ADAM_B2 * v_ref[...] + (1.0 - ADAM_B2) * (g * g)
        mh = nm * c1
        vh = nv * c2
        g_ref[...] = g
        nm_ref[...] = nm
        nv_ref[...] = nv
        d_ref[...] = -ADAM_LR * (mh / (jnp.sqrt(vh) + ADAM_EPS) + ADAM_WD * w_ref[...])

    wspec = pl.BlockSpec((tr, C), lambda i: (i, 0))
    pspecs = [pl.BlockSpec(bs(tr, C), im) for (_, bs, im) in parts]
    outs = pl.pallas_call(
        body, name=name, grid=(R // tr,),
        in_specs=[wspec] * 3 + pspecs, out_specs=[wspec] * 4,
        out_shape=[SDS((R, C), F32)] * 4,
        compiler_params=_cparams(("parallel",)))(w, m, v, *[p[0] for p in parts])
    return outs


def _plain_part(g):
    return (g, lambda tr, C: (tr, C), lambda i: (i, 0))


def _slot_part(g, slot):
    return (g, lambda tr, C: (None, tr, C), lambda i, s=slot: (s, i, 0))


def ada_fwd(c_all, w_cols, b_cols, name):
    Bg, D = c_all.shape
    n = w_cols.shape[1]

    def body(c_ref, w_ref, b_ref, o_ref, s_ref):
        cc = c_ref[...]
        s = cc * _sig(cc)
        s_ref[...] = s
        o_ref[...] = jnp.dot(s, w_ref[...], preferred_element_type=F32, precision=lax.Precision.HIGHEST) + b_ref[...]

    return pl.pallas_call(body, name=name, out_shape=[SDS((Bg, n), F32), SDS((Bg, D), F32)],
                          compiler_params=_cparams())(c_all, w_cols, b_cols)


def ada_bwd(s_all, dmod_cols, dmod_all, name):
    Bg, D = s_all.shape
    n = dmod_cols.shape[1]

    def body(s_ref, dc_ref, da_ref, gw_ref, gb_ref):
        gw_ref[...] = lax.dot_general(s_ref[...], dc_ref[...], (((0,), (0,)), ((), ())),
                                      preferred_element_type=F32, precision=lax.Precision.HIGHEST)
        acc = da_ref[0:1, :]
        for r in range(1, Bg):
            acc = acc + da_ref[r:r + 1, :]
        gb_ref[...] = acc

    return pl.pallas_call(body, name=name, out_shape=[SDS((D, n), F32), SDS((1, dmod_all.shape[1]), F32)],
                          compiler_params=_cparams())(s_all, dmod_cols, dmod_all)


def _vec(D, rank):
    return pl.BlockSpec((1, D), (lambda i: (0, 0)) if rank == 1 else (lambda i, j: (0, 0)))


def _modspec(D, tpb, rank):
    if rank == 1:
        return pl.BlockSpec((None, 1, D), lambda i: (i // tpb, 0, 0))
    return pl.BlockSpec((None, 1, D), lambda i, j: (i // tpb, 0, 0))


def _resident(shape):
    return pl.BlockSpec(shape, lambda *_: (0,) * len(shape), pipeline_mode=pl.Buffered(1))


def ffn_up(xs, pg, pb, sc, sh, wg8, name, comm=None):
    T, D = xs.shape
    n2, _, nb = wg8.shape
    nj = n2 // 2
    S = T // sc.shape[0]
    tm = min(512, S)
    tpb = S // tm
    rider = _Rider(comm)

    def body(*refs):
        (x_ref, pg_ref, pb_ref, sc_ref, sh_ref, w_ref, u_ref, a_ref, g_ref, p_ref), ride = rider.split(refs, 6, 4)
        ride(pl.program_id(0), T // tm)
        xin = x_ref[...] * pg_ref[...] + pb_ref[...]
        u_ref[...] = (xin * (1.0 + sc_ref[...]) + sh_ref[...]).astype(BF)

        def col_block(j, _):
            u = u_ref[...]
            a = _dot_nn(u, w_ref[j])
            g = _dot_nn(u, w_ref[j + nj])
            s = _sig(a)
            silu = a * s
            a_ref[j] = (g * (s * (1.0 + a * (1.0 - s)))).astype(BF)
            g_ref[j] = silu.astype(BF)
            p_ref[j] = (silu * g).astype(BF)
            return 0

        lax.fori_loop(0, nj, col_block, 0)

    blk = pl.BlockSpec((nj, tm, nb), lambda i: (0, i, 0))
    row = pl.BlockSpec((tm, D), lambda i: (i, 0))
    outs = pl.pallas_call(
        body, name=name, grid=(T // tm,),
        in_specs=[row, _vec(D, 1), _vec(D, 1), _modspec(D, tpb, 1), _modspec(D, tpb, 1), _resident(wg8.shape)]
        + rider.in_specs,
        out_specs=[row, blk, blk, blk] + rider.out_specs,
        out_shape=[SDS((T, D), BF)] + [SDS((nj, T, nb), BF)] * 3 + rider.out_shape,
        scratch_shapes=rider.scratch,
        compiler_params=_cparams(rider.sem(("parallel",))))(xs, pg, pb, sc, sh, wg8, *rider.ins)
    return rider.result(outs, 4)


def ffn_down_ln(p4, wd3, xs, pg, pb, gate, name, comm=None):
    nj, T, nb = p4.shape
    D = wd3.shape[2]
    S = T // gate.shape[0]
    tm = min(512, S)
    tpb = S // tm
    rider = _Rider(comm)

    def body(*refs):
        (p_ref, wd_ref, x_ref, pg_ref, pb_ref, gate_ref, xh_ref, rs_ref, f_ref), ride = rider.split(refs, 6, 3)
        ride(pl.program_id(0), T // tm)
        f = _dot_nn(p_ref[0], wd_ref[0])
        for k in range(1, nj):
            f = f + _dot_nn(p_ref[k], wd_ref[k])
        xin = x_ref[...] * pg_ref[...] + pb_ref[...]
        r = DEEPNORM_ALPHA * xin + gate_ref[...] * (MACARON_WEIGHT * f)
        xh, rstd = _ln_stats(r)
        xh_ref[...] = xh
        rs_ref[...] = rstd
        f_ref[...] = f.astype(BF)

    row = pl.BlockSpec((tm, D), lambda i: (i, 0))
    outs = pl.pallas_call(
        body, name=name, grid=(T // tm,),
        in_specs=[pl.BlockSpec((nj, tm, nb), lambda i: (0, i, 0)), _resident(wd3.shape),
                  row, _vec(D, 1), _vec(D, 1), _modspec(D, tpb, 1)] + rider.in_specs,
        out_specs=[row, pl.BlockSpec((tm, 1), lambda i: (i, 0)), row] + rider.out_specs,
        out_shape=[SDS((T, D), F32), SDS((T, 1), F32), SDS((T, D), BF)] + rider.out_shape,
        scratch_shapes=rider.scratch,
        compiler_params=_cparams(rider.sem(("parallel",))))(p4, wd3, xs, pg, pb, gate, *rider.ins)
    return rider.result(outs, 3)


N_QKV = 3


def mod_matmul(xs, pg, pb, sc, sh, w, name, comm=None):
    T, D = xs.shape
    N = w.shape[1]
    S = T // sc.shape[0]
    tm = min(256, S)
    tpb = S // tm
    rider = _Rider(comm)

    def body(*refs):
        (x_ref, pg_ref, pb_ref, sc_ref, sh_ref, w_ref, u_ref, qkv_ref, o_ref), ride = rider.split(refs, 6, 3)
        ride(pl.program_id(0), T // tm)
        xin = x_ref[...] * pg_ref[...] + pb_ref[...]
        u = (xin * (1.0 + sc_ref[...]) + sh_ref[...]).astype(BF)
        u_ref[...] = u
        for n in range(N // D):
            y = _dot_nn(u, w_ref[:, n * D:(n + 1) * D])
            if n < N_QKV:
                qkv_ref[:, n * D:(n + 1) * D] = y.astype(BF)
            else:
                o_ref[:, (n - N_QKV) * D:(n - N_QKV + 1) * D] = y

    row = pl.BlockSpec((tm, D), lambda i: (i, 0))
    outs = pl.pallas_call(
        body, name=name, grid=(T // tm,),
        in_specs=[row, _vec(D, 1), _vec(D, 1), _modspec(D, tpb, 1), _modspec(D, tpb, 1), _resident(w.shape)]
        + rider.in_specs,
        out_specs=[row, pl.BlockSpec((tm, N_QKV * D), lambda i: (i, 0)),
                   pl.BlockSpec((tm, N - N_QKV * D), lambda i: (i, 0))] + rider.out_specs,
        out_shape=[SDS((T, D), BF), SDS((T, N_QKV * D), BF), SDS((T, N - N_QKV * D), F32)] + rider.out_shape,
        scratch_shapes=rider.scratch,
        compiler_params=_cparams(rider.sem(("parallel",))))(xs, pg, pb, sc, sh, w, *rider.ins)
    return rider.result(outs, 3)


ATT_TQ = 2048
ATT_TK = 256


def _att_consts(tk):
    r = lax.broadcasted_iota(jnp.int32, (tk + 8, tk), 0)
    c = lax.broadcasted_iota(jnp.int32, (tk + 8, tk), 1)
    usum = jnp.where((r >= tk) | (c > r), 1.0, 0.0).astype(BF)
    lsum = jnp.where((r >= tk) | (c < r), 1.0, 0.0).astype(BF)
    dmask = lax.broadcasted_iota(jnp.int32, (tk, tk), 0) < lax.broadcasted_iota(jnp.int32, (tk, tk), 1)
    return usum, lsum, dmask


def _split_dot(m, v):
    hi = v.astype(BF)
    lo = (v - hi.astype(F32)).astype(BF)
    return _dot_nn(m, hi) + _dot_nn(m, lo)


def _softplus(z):
    return jnp.maximum(z, 0.0) + jnp.log(1.0 + jnp.exp(-jnp.abs(z)))


def _att_dims(S, D):
    dh = SB_HEAD_DIM
    cw = min(LANE, D)
    tq = min(ATT_TQ, S)
    tk = min(ATT_TK, tq)
    assert tq % tk == 0 and S % tq == 0
    return dh, cw, cw // dh, D // cw, tq, tk, S // tq, S // tk


def att_fwd(proj, Bl, S, D, name):
    dh, cw, hp, nblk, tq, tk, nq, nk = _att_dims(S, D)
    scale = 1.0 / math.sqrt(dh)
    assert math.log2(scale) == int(math.log2(scale))
    H = D // dh

    def body(q_ref, k_ref, v_ref, o_ref, car_ref, qs, ks, vts):
        usum, _, dmask = _att_consts(tk)
        for hh in range(hp):
            sl = slice(hh * dh, (hh + 1) * dh)
            qs[hh] = (q_ref[:, sl] * scale).astype(BF)
            ks[hh] = k_ref[:, sl].astype(BF)
            for kb in range(nk):
                vts[hh, kb] = v_ref[kb * tk:(kb + 1) * tk, sl].astype(F32).T.astype(BF)
        nch = tq // tk

        def qloop(qb, _):
            qo = pl.multiple_of(qb * tq, tq)
            n_full = qb * nch

            def blk(kb, state, diag):
                ko = pl.multiple_of(kb * tk, tk)
                chains = [(hh, c) for hh in range(hp) for c in range(0 if diag is None else diag, nch)]

                def masked(ch, val):
                    return jnp.where(dmask, val, 0.0) if ch[1] == diag else val

                z = {ch: _dot_nt(ks[ch[0], pl.ds(ko, tk), :], qs[ch[0], pl.ds(pl.multiple_of(qo + ch[1] * tk, tk), tk), :])
                     for ch in chains}
                sp = {ch: _softplus(z[ch]) for ch in chains}
                lk = {ch: masked(ch, -sp[ch]) for ch in chains}
                for hh, c in chains:
                    car_ref[hh, qb * nk + kb, :, c * tk:(c + 1) * tk] = state[hh][c][0]
                cs = {ch: _split_dot(usum, lk[ch]) for ch in chains}
                w = {ch: masked(ch, jnp.exp((z[ch] - sp[ch]) + state[ch[0]][ch[1]][0][0:1, :] + cs[ch][:tk]))
                     for ch in chains}
                pv = {ch: _dot_nn(vts[ch[0], kb], w[ch].astype(BF)) for ch in chains}
                return tuple(tuple(
                    (state[hh][c][0] + cs[(hh, c)][tk:], state[hh][c][1] + pv[(hh, c)]) if (hh, c) in z else state[hh][c]
                    for c in range(nch)) for hh in range(hp))

            state = tuple(tuple((jnp.zeros((8, tk), F32), jnp.zeros((dh, tk), F32)) for _ in range(nch))
                          for _ in range(hp))
            for i in reversed(range(nch)):
                state = blk(n_full + i, state, i)
            state = lax.fori_loop(0, n_full, lambda j, st: blk(n_full - 1 - j, st, None), state)
            for hh in range(hp):
                for c in range(nch):
                    o_ref[pl.ds(pl.multiple_of(qo + c * tk, tk), tk), hh * dh:(hh + 1) * dh] = (
                        state[hh][c][1].T.astype(BF))
            return 0

        lax.fori_loop(0, nq, qloop, 0)

    def seg(s):
        return pl.BlockSpec((S, cw), lambda b, h: (b, s * nblk + h))

    return pl.pallas_call(
        body, name=name, grid=(Bl, nblk),
        in_specs=[seg(0), seg(1), seg(2)],
        out_specs=[pl.BlockSpec((S, cw), lambda b, h: (b, h)),
                   pl.BlockSpec((None, hp, nq * nk, 8, tq), lambda b, h: (b, h, 0, 0, 0))],
        out_shape=[SDS((Bl * S, D), BF), SDS((Bl, H, nq * nk, 8, tq), F32)],
        scratch_shapes=[pltpu.VMEM((hp, S, dh), BF)] * 2 + [pltpu.VMEM((hp, nk, dh, tk), BF)],
        compiler_params=_cparams(("parallel", "parallel")))(proj, proj, proj)


def conv_fwd(proj, cw32, cb, cg, cbeta, Bl, S, D, kw, name):
    T = Bl * S
    ts = min(128, S)
    ns = S // ts
    off = HALO - (kw - 1)
    rc = min(64, ts)
    cw = min(LANE, D)

    def body(a_ref, b_ref, ha_ref, hb_ref, w_ref, cb_ref, g_ref, be_ref, cs_ref, xh_ref, rs_ref, hsh, conv_s):
        i = pl.program_id(1)
        h = a_ref[...] * _sig(b_ref[...])
        hh = jnp.where(i == 0, 0.0, ha_ref[...] * _sig(hb_ref[...]))
        for cb_ in range(D // cw):
            cols = slice(cb_ * cw, (cb_ + 1) * cw)
            hsh[0, cb_, pl.ds(HALO, ts), :] = h[:, cols]
            hsh[0, cb_, pl.ds(0, HALO), :] = hh[:, cols]
            for s in range(1, SUBLANES):
                hsh[s, cb_, pl.ds(0, ts + HALO - SUBLANES), :] = hsh[0, cb_, pl.ds(s, ts + HALO - SUBLANES), :]
            accs = [jnp.zeros((rc, cw), F32) for _ in range(ts // rc)]
            for k in range(kw):
                wk = w_ref[k:k + 1, cols]
                s = (off + k) % SUBLANES
                for r in range(ts // rc):
                    accs[r] = accs[r] + wk * hsh[s, cb_, pl.ds(r * rc + off + k - s, rc), :]
            for r in range(ts // rc):
                conv_s[pl.ds(r * rc, rc), cols] = accs[r]
        conv = conv_s[...] + cb_ref[...]
        xh, rstd = _ln_stats(conv)
        xh_ref[...] = xh
        rs_ref[...] = rstd
        cl = xh * g_ref[...] + be_ref[...]
        cs_ref[...] = (cl * _sig(cl)).astype(BF)

    hpb = ts // HALO

    def tile(seg):
        return pl.BlockSpec((ts, D), lambda b, i: (b * ns + i, seg))

    def halo(seg):
        return pl.BlockSpec((HALO, D), lambda b, i: (jnp.maximum((b * ns + i) * hpb - 1, 0), seg))

    row = pl.BlockSpec((ts, D), lambda b, i: (b * ns + i, 0))
    vec = pl.BlockSpec((1, D), lambda b, i: (0, 0))
    return pl.pallas_call(
        body, name=name, grid=(Bl, ns),
        in_specs=[tile(0), tile(1), halo(0), halo(1), pl.BlockSpec((HALO, D), lambda b, i: (0, 0)), vec, vec, vec],
        out_specs=[row, row, pl.BlockSpec((ts, 1), lambda b, i: (b * ns + i, 0))],
        out_shape=[SDS((T, D), BF), SDS((T, D), F32), SDS((T, 1), F32)],
        scratch_shapes=[pltpu.VMEM((SUBLANES, D // cw, ts + HALO, cw), F32), pltpu.VMEM((ts, D), F32)],
        compiler_params=_cparams(("parallel", "arbitrary")))(proj, proj, proj, proj, cw32, cb, cg, cbeta)


def mix_fwd(yatt, cs, proj, wsb, wco, wout, xs, pg, pb, gate, name):
    T, D = yatt.shape
    S = T // gate.shape[0]
    tm = min(256, S)
    tpb = S // tm

    def body(ya_ref, cs_ref, ga_ref, gb_ref, wsb_ref, wco_ref, wout_ref, x_ref, pg_ref, pb_ref, gate_ref,
             xh_ref, rs_ref, ysb_ref, yco_ref, mg_ref, o_ref):
        ysb = _dot_nn(ya_ref[...], wsb_ref[...])
        yco = _dot_nn(cs_ref[...], wco_ref[...])
        merged = _sig(ga_ref[...]) * ysb + _sig(gb_ref[...]) * yco
        mg = merged.astype(BF)
        o = _dot_nn(mg, wout_ref[...])
        xin = x_ref[...] * pg_ref[...] + pb_ref[...]
        r = DEEPNORM_ALPHA * xin + gate_ref[...] * o
        xh, rstd = _ln_stats(r)
        xh_ref[...] = xh
        rs_ref[...] = rstd
        ysb_ref[...] = ysb.astype(BF)
        yco_ref[...] = yco.astype(BF)
        mg_ref[...] = mg
        o_ref[...] = o.astype(BF)

    row = pl.BlockSpec((tm, D), lambda i: (i, 0))
    wfull = pl.BlockSpec((D, D), lambda i: (0, 0))
    return pl.pallas_call(
        body, name=name, grid=(T // tm,),
        in_specs=[row, row, pl.BlockSpec((tm, D), lambda i: (i, 2)), pl.BlockSpec((tm, D), lambda i: (i, 3)),
                  wfull, wfull, wfull, row, _vec(D, 1), _vec(D, 1), _modspec(D, tpb, 1)],
        out_specs=[row, pl.BlockSpec((tm, 1), lambda i: (i, 0)), row, row, row, row],
        out_shape=[SDS((T, D), F32), SDS((T, 1), F32)] + [SDS((T, D), BF)] * 4,
        compiler_params=_cparams(("parallel",)))(yatt, cs, proj, proj, wsb, wco, wout, xs, pg, pb, gate)


def ln_bwd(dout, xh, rstd, lng, lnb, gate, sub, res_w, name, target=None, swiglu=None):
    T, D = xh.shape
    Bl = gate.shape[0]
    S = T // Bl
    tm = min(256, S)
    tpb = S // tm
    first = target is not None
    n_in = 10 if swiglu else 7

    def body(*refs):
        do_ref, xh_ref, rs_ref, g_ref, b_ref, gate_ref, sub_ref = refs[:7]
        tg_ref = do_ref
        outs = refs[n_in:]
        dr_ref, ds_ref, dg_ref, db_ref, dgate_ref = outs[:5]
        loss_ref = outs[5] if first else None
        i = pl.program_id(0)
        xh_ = xh_ref[...]
        if first:
            diff = (xh_ * g_ref[...] + b_ref[...]) - tg_ref[...]
            lsum = jnp.sum(jnp.sum(diff * diff, axis=1, keepdims=True), axis=0, keepdims=True) * (0.5 / D)
            do = diff * (1.0 / D)
        else:
            do = do_ref[...]

        @pl.when(i == 0)
        def _():
            dg_ref[...] = jnp.zeros_like(dg_ref)
            db_ref[...] = jnp.zeros_like(db_ref)
            if first:
                loss_ref[...] = jnp.zeros_like(loss_ref)

        @pl.when(i % tpb == 0)
        def _():
            dgate_ref[...] = jnp.zeros_like(dgate_ref)

        if first:
            loss_ref[...] += jnp.broadcast_to(lsum, loss_ref.shape)
        dg_ref[...] += _rowsum(do * xh_)
        db_ref[...] += _rowsum(do)
        dr = _ln_bwd(do * g_ref[...], xh_, rs_ref[...])
        dr_ref[...] = dr
        ds_ref[...] = (dr * gate_ref[...] * res_w).astype(BF)
        dgate_ref[...] += _rowsum(dr * (res_w * sub_ref[...].astype(F32)))
        if swiglu:
            wd_ref, a_ref, gg_ref = refs[7:10]
            dh_ref = outs[-1]

            def col_block(j, _):
                dp = _dot_nt(ds_ref[...], wd_ref[j])
                dh_ref[0, j] = (dp * a_ref[j].astype(F32)).astype(BF)
                dh_ref[1, j] = (dp * gg_ref[j].astype(F32)).astype(BF)
                return 0

            lax.fori_loop(0, swiglu[1].shape[0], col_block, 0)

    row = pl.BlockSpec((tm, D), lambda i: (i, 0))
    vec = _vec(D, 1)
    mod = _modspec(D, tpb, 1)
    in_specs = [row, row, pl.BlockSpec((tm, 1), lambda i: (i, 0)), vec, vec, mod, row]
    out_specs = [row, row, vec, vec, mod]
    out_shape = [SDS((T, D), F32), SDS((T, D), BF), SDS((1, D), F32), SDS((1, D), F32), SDS((Bl, 1, D), F32)]
    if first:
        out_specs.append(pl.BlockSpec((8, LANE), lambda i: (0, 0)))
        out_shape.append(SDS((8, LANE), F32))
    extra = []
    if swiglu:
        wd3, a4, g4 = swiglu
        nj, _, nb = a4.shape
        blk = pl.BlockSpec((nj, tm, nb), lambda i: (0, i, 0))
        in_specs += [_resident(wd3.shape), blk, blk]
        out_specs.append(pl.BlockSpec((2, nj, tm, nb), lambda i: (0, 0, i, 0)))
        out_shape.append(SDS((2, nj, T, nb), BF))
        extra = [wd3, a4, g4]
    return pl.pallas_call(
        body, name=name, grid=(T // tm,), in_specs=in_specs, out_specs=out_specs, out_shape=out_shape,
        compiler_params=_cparams(("arbitrary",)))(target if first else dout, xh, rstd, lng, lnb, gate, sub, *extra)


def mod_bwd(dh, w, dr, xs, pg, pb, sc, blocked, name, comm=None):
    T, D = dr.shape
    Bl = sc.shape[0]
    S = T // Bl
    tm = min(512 if blocked else 256, S)
    tpb = S // tm
    row = pl.BlockSpec((tm, D), lambda i: (i, 0))
    if blocked:
        nk, _, kb = dh.shape
        dh_list, dh_specs = [dh], [pl.BlockSpec((nk, tm, kb), lambda i: (0, i, 0))]
    else:
        nk = len(dh)
        dh_list, dh_specs = list(dh), [row] * nk
    nd = len(dh_list)
    rider = _Rider(comm)

    def body(*refs):
        own, ride = rider.split(refs, nd + 6, 3)
        dh_refs = own[:nd]
        w_ref, dr_ref, x_ref, pg_ref, pb_ref, sc_ref, dx_ref, dsc_ref, dsh_ref = own[nd:]
        i = pl.program_id(0)
        ride(i, T // tm)

        def part(k):
            if blocked:
                return _dot_nt(dh_refs[0][k], w_ref[k])
            return _dot_nt(dh_refs[k][...], w_ref[:, k * D:(k + 1) * D])

        du = part(0)
        for k in range(1, nk):
            du = du + part(k)

        @pl.when(i % tpb == 0)
        def _():
            dsc_ref[...] = jnp.zeros_like(dsc_ref)
            dsh_ref[...] = jnp.zeros_like(dsh_ref)

        xin = x_ref[...] * pg_ref[...] + pb_ref[...]
        dx_ref[...] = DEEPNORM_ALPHA * dr_ref[...] + du * (1.0 + sc_ref[...])
        dsc_ref[...] += _rowsum(du * xin)
        dsh_ref[...] += _rowsum(du)

    mod = _modspec(D, tpb, 1)
    outs = pl.pallas_call(
        body, name=name, grid=(T // tm,),
        in_specs=dh_specs + [_resident(w.shape), row, row, _vec(D, 1), _vec(D, 1), mod] + rider.in_specs,
        out_specs=[row, mod, mod] + rider.out_specs,
        out_shape=[SDS((T, D), F32), SDS((Bl, 1, D), F32), SDS((Bl, 1, D), F32)] + rider.out_shape,
        scratch_shapes=rider.scratch,
        compiler_params=_cparams(("arbitrary",)))(*dh_list, w, dr, xs, pg, pb, sc, *rider.ins)
    return rider.result(outs, 3)


def merge_bwd(do2, proj, ysb, yco, wsb, wco, wout, name):
    T, D = do2.shape
    tm = min(256, T)

    def body(do_ref, ga_ref, gb_ref, ysb_ref, yco_ref, wsb_ref, wco_ref, wout_ref,
             dysb_ref, dyco_ref, dga_ref, dgb_ref, dya_ref, dcs_ref):
        dm = _dot_nt(do_ref[...], wout_ref[...])
        sa = _sig(ga_ref[...])
        sb = _sig(gb_ref[...])
        dysb = (dm * sa).astype(BF)
        dyco = (dm * sb).astype(BF)
        dysb_ref[...] = dysb
        dyco_ref[...] = dyco
        dga_ref[...] = (dm * ysb_ref[...].astype(F32) * (sa * (1.0 - sa))).astype(BF)
        dgb_ref[...] = (dm * yco_ref[...].astype(F32) * (sb * (1.0 - sb))).astype(BF)
        dya_ref[...] = _dot_nt(dysb, wsb_ref[...]).astype(BF)
        dcs_ref[...] = _dot_nt(dyco, wco_ref[...])

    row = pl.BlockSpec((tm, D), lambda i: (i, 0))
    wfull = pl.BlockSpec((D, D), lambda i: (0, 0))
    return pl.pallas_call(
        body, name=name, grid=(T // tm,),
        in_specs=[row, pl.BlockSpec((tm, D), lambda i: (i, 2)), pl.BlockSpec((tm, D), lambda i: (i, 3)),
                  row, row, wfull, wfull, wfull],
        out_specs=[row] * 6,
        out_shape=[SDS((T, D), BF)] * 5 + [SDS((T, D), F32)],
        compiler_params=_cparams(("parallel",)))(do2, proj, proj, ysb, yco, wsb, wco, wout)


def att_bwd(proj, dyatt, car, Bl, S, D, name, comm=None):
    dh, cw, hp, nblk, tq, tk, nq, nk = _att_dims(S, D)
    scale = 1.0 / math.sqrt(dh)
    rider = _Rider(comm)

    def body(*refs):
        (q_ref, k_ref, v_ref, do_ref, car_ref, dq_ref, dk_ref, dv_ref,
         qs, ks, vs, dos, kts, dk_acc, dv_acc), ride = rider.split(refs, 5, 3, 7)
        ride(pl.program_id(0) * nblk + pl.program_id(1), Bl * nblk)
        usum, lsum, dmask = _att_consts(tk)
        for hh in range(hp):
            sl = slice(hh * dh, (hh + 1) * dh)
            qs[hh] = (q_ref[:, sl] * scale).astype(BF)
            ks[hh] = k_ref[:, sl].astype(BF)
            vs[hh] = v_ref[:, sl].astype(BF)
            dos[hh] = do_ref[:, sl]
            for kb in range(nk):
                kts[hh, kb] = k_ref[kb * tk:(kb + 1) * tk, sl].astype(F32).T.astype(BF)
        dk_acc[...] = jnp.zeros_like(dk_acc)
        dv_acc[...] = jnp.zeros_like(dv_acc)
        nch = tq // tk

        def qloop(qb, _):
            qo = pl.multiple_of(qb * tq, tq)
            n_full = qb * nch

            def blk(kb, state, diag):
                ko = pl.multiple_of(kb * tk, tk)
                chains = [(hh, c) for hh in range(hp) for c in range(0 if diag is None else diag, nch)]

                def masked(ch, val):
                    return jnp.where(dmask, val, 0.0) if ch[1] == diag else val

                def qrows(ref, ch):
                    return ref[ch[0], pl.ds(pl.multiple_of(qo + ch[1] * tk, tk), tk), :]

                k = [ks[hh, pl.ds(ko, tk), :] for hh in range(hp)]
                v = [vs[hh, pl.ds(ko, tk), :] for hh in range(hp)]
                z = {ch: _dot_nt(k[ch[0]], qrows(qs, ch)) for ch in chains}
                dw = {ch: _dot_nt(v[ch[0]], qrows(dos, ch)) for ch in chains}
                sp = {ch: _softplus(z[ch]) for ch in chains}
                lk = {ch: masked(ch, -sp[ch]) for ch in chains}
                cs = {ch: _split_dot(usum, lk[ch]) for ch in chains}
                w = {ch: masked(ch, jnp.exp((z[ch] - sp[ch])
                                            + car_ref[ch[0], qb * nk + kb, 0:1, ch[1] * tk:(ch[1] + 1) * tk]
                                            + cs[ch][:tk])) for ch in chains}
                dlw = {ch: dw[ch] * w[ch] for ch in chains}
                gs = {ch: _dot_nn(lsum, dlw[ch].astype(BF)) for ch in chains}
                sg = {ch: jnp.exp(z[ch] - sp[ch]) for ch in chains}
                dzb = {ch: masked(ch, dlw[ch] - sg[ch] * (dlw[ch] + state[ch[0]][ch[1]][0][0:1, :] + gs[ch][:tk])
                                  ).astype(BF) for ch in chains}
                wb = {ch: w[ch].astype(BF) for ch in chains}
                for hh in range(hp):
                    mine = [ch for ch in chains if ch[0] == hh]
                    dk_acc[hh, kb] += sum(_dot_nn(dzb[ch], qrows(qs, ch)) for ch in mine)
                    dv_acc[hh, kb] += sum(_dot_nn(wb[ch], qrows(dos, ch)) for ch in mine)
                dq = {ch: _dot_nn(kts[ch[0], kb], dzb[ch]) for ch in chains}
                return tuple(tuple(
                    (state[hh][c][0] + gs[(hh, c)][tk:], state[hh][c][1] + dq[(hh, c)]) if (hh, c) in z else state[hh][c]
                    for c in range(nch)) for hh in range(hp))

            state = tuple(tuple((jnp.zeros((8, tk), F32), jnp.zeros((dh, tk), F32)) for _ in range(nch))
                          for _ in range(hp))
            state = lax.fori_loop(0, n_full, lambda kb, st: blk(kb, st, None), state)
            for i in range(nch):
                state = blk(n_full + i, state, i)
            for hh in range(hp):
                for c in range(nch):
                    dq_ref[pl.ds(pl.multiple_of(qo + c * tk, tk), tk), hh * dh:(hh + 1) * dh] = (
                        (state[hh][c][1].T * scale).astype(BF))
            return 0

        lax.fori_loop(0, nq, qloop, 0)
        for hh in range(hp):
            sl = slice(hh * dh, (hh + 1) * dh)
            for kb in range(nk):
                dk_ref[kb * tk:(kb + 1) * tk, sl] = dk_acc[hh, kb].astype(BF)
                dv_ref[kb * tk:(kb + 1) * tk, sl] = dv_acc[hh, kb].astype(BF)

    def seg(s):
        return pl.BlockSpec((S, cw), lambda b, h: (b, s * nblk + h))

    blk_spec = pl.BlockSpec((S, cw), lambda b, h: (b, h))
    outs = pl.pallas_call(
        body, name=name, grid=(Bl, nblk),
        in_specs=[seg(0), seg(1), seg(2), blk_spec,
                  pl.BlockSpec((None, hp, nq * nk, 8, tq), lambda b, h: (b, h, 0, 0, 0))] + rider.in_specs,
        out_specs=[blk_spec, blk_spec, blk_spec] + rider.out_specs,
        out_shape=[SDS((Bl * S, D), BF)] * 3 + rider.out_shape,
        scratch_shapes=[pltpu.VMEM((hp, S, dh), BF)] * 4 + [pltpu.VMEM((hp, nk, dh, tk), BF)]
        + [pltpu.VMEM((hp, nk, tk, dh), F32)] * 2 + rider.scratch,
        compiler_params=_cparams(rider.sem(("parallel", "parallel"))))(proj, proj, proj, dyatt, car, *rider.ins)
    return rider.result(outs, 3)


def conv_ln_bwd(dcs, xhc, rstd_c, cg, cbeta, name):
    T, D = dcs.shape
    tm = min(256, T)

    def body(dcs_ref, xh_ref, rs_ref, g_ref, b_ref, dconv_ref, dg_ref, db_ref, dcb_ref):
        @pl.when(pl.program_id(0) == 0)
        def _():
            dg_ref[...] = jnp.zeros_like(dg_ref)
            db_ref[...] = jnp.zeros_like(db_ref)
            dcb_ref[...] = jnp.zeros_like(dcb_ref)
        xh = xh_ref[...]
        cl = xh * g_ref[...] + b_ref[...]
        s = _sig(cl)
        dcl = dcs_ref[...] * (s * (1.0 + cl * (1.0 - s)))
        dg_ref[...] += _rowsum(dcl * xh)
        db_ref[...] += _rowsum(dcl)
        dconv = _ln_bwd(dcl * g_ref[...], xh, rs_ref[...])
        dconv_ref[...] = dconv
        dcb_ref[...] += _rowsum(dconv)

    row = pl.BlockSpec((tm, D), lambda i: (i, 0))
    vec = _vec(D, 1)
    return pl.pallas_call(
        body, name=name, grid=(T // tm,),
        in_specs=[row, row, pl.BlockSpec((tm, 1), lambda i: (i, 0)), vec, vec],
        out_specs=[row, vec, vec, vec],
        out_shape=[SDS((T, D), F32)] + [SDS((1, D), F32)] * 3,
        compiler_params=_cparams(("arbitrary",)))(dcs, xhc, rstd_c, cg, cbeta)


def conv_bwd(dconv, proj, cw32, Bl, S, D, kw, name):
    T = Bl * S
    ts = min(128, S)
    ns = S // ts
    off = HALO - (kw - 1)
    rc = min(64, ts)
    cw = min(LANE, D)
    hpb = ts // HALO
    nhb = T // HALO

    def body(dc_ref, dcn_ref, a_ref, b_ref, ha_ref, hb_ref, w_ref, da_ref, db_ref, dw_ref, hsh, dsh, dh_s):
        b_ = pl.program_id(0)
        i = pl.program_id(1)
        span = ts + HALO - SUBLANES

        @pl.when((b_ == 0) & (i == 0))
        def _():
            dw_ref[...] = jnp.zeros_like(dw_ref)
        a = a_ref[...]
        sb = _sig(b_ref[...])
        h = a * sb
        hh = jnp.where(i == 0, 0.0, ha_ref[...] * _sig(hb_ref[...]))
        dc = dc_ref[...]
        dcn = jnp.where(i == ns - 1, 0.0, dcn_ref[...])
        for cb_ in range(D // cw):
            cols = slice(cb_ * cw, (cb_ + 1) * cw)
            hsh[0, cb_, pl.ds(HALO, ts), :] = h[:, cols]
            hsh[0, cb_, pl.ds(0, HALO), :] = hh[:, cols]
            dsh[0, cb_, pl.ds(0, ts), :] = dc[:, cols]
            dsh[0, cb_, pl.ds(ts, HALO), :] = dcn[:, cols]
            for s in range(1, SUBLANES):
                hsh[s, cb_, pl.ds(0, span), :] = hsh[0, cb_, pl.ds(s, span), :]
                dsh[s, cb_, pl.ds(0, span), :] = dsh[0, cb_, pl.ds(s, span), :]
            accs = [jnp.zeros((rc, cw), F32) for _ in range(ts // rc)]
            d0 = [dsh[0, cb_, pl.ds(r * rc, rc), :] for r in range(ts // rc)]
            for k in range(kw):
                wk = w_ref[k:k + 1, cols]
                wsum = jnp.zeros((rc, cw), F32)
                sd = ((kw - 1) - k) % SUBLANES
                sh_ = (off + k) % SUBLANES
                for r in range(ts // rc):
                    accs[r] = accs[r] + wk * dsh[sd, cb_, pl.ds(r * rc + (kw - 1) - k - sd, rc), :]
                    wsum = wsum + d0[r] * hsh[sh_, cb_, pl.ds(r * rc + off + k - sh_, rc), :]
                dw_ref[k:k + 1, cols] += _rowsum(wsum)
            for r in range(ts // rc):
                dh_s[pl.ds(r * rc, rc), cols] = accs[r]
        dhc = dh_s[...]
        da_ref[...] = (dhc * sb).astype(BF)
        db_ref[...] = (dhc * a * (sb * (1.0 - sb))).astype(BF)

    def tile(seg):
        return pl.BlockSpec((ts, D), lambda b, i: (b * ns + i, seg))

    def halo(seg):
        return pl.BlockSpec((HALO, D), lambda b, i: (jnp.maximum((b * ns + i) * hpb - 1, 0), seg))

    row = pl.BlockSpec((ts, D), lambda b, i: (b * ns + i, 0))
    nxt = pl.BlockSpec((HALO, D), lambda b, i: (jnp.minimum((b * ns + i + 1) * hpb, nhb - 1), 0))
    wspec = pl.BlockSpec((HALO, D), lambda b, i: (0, 0))
    return pl.pallas_call(
        body, name=name, grid=(Bl, ns),
        in_specs=[row, nxt, tile(0), tile(1), halo(0), halo(1), wspec],
        out_specs=[row, row, wspec],
        out_shape=[SDS((T, D), BF), SDS((T, D), BF), SDS((HALO, D), F32)],
        scratch_shapes=[pltpu.VMEM((SUBLANES, D // cw, ts + HALO, cw), F32)] * 2 + [pltpu.VMEM((ts, D), F32)],
        compiler_params=_cparams(("arbitrary", "arbitrary")))(dconv, dconv, proj, proj, proj, proj, cw32)


def matmul_tn(xa, ga, x_spec, g_spec, out_shape, out_spec, acc_shape, grid, name, comm=None):
    nk = grid[-1]
    rider = _Rider(comm)

    def body(*refs):
        (x_ref, g_ref, o_ref, acc), ride = rider.split(refs, 2, 1, 1)
        k = pl.program_id(1)
        ride(pl.program_id(0) * nk + k, grid[0] * nk)

        @pl.when(k == 0)
        def _():
            acc[...] = jnp.zeros_like(acc)
        acc[...] += _dot_tn(x_ref[...], g_ref[...])

        @pl.when(k == nk - 1)
        def _():
            o_ref[...] = acc[...]

    outs = pl.pallas_call(
        body, name=name, grid=grid, in_specs=[x_spec, g_spec] + rider.in_specs,
        out_specs=[out_spec] + rider.out_specs,
        out_shape=[SDS(out_shape, F32)] + rider.out_shape,
        scratch_shapes=[pltpu.VMEM(acc_shape, F32)] + rider.scratch,
        compiler_params=_cparams(rider.sem(("parallel", "arbitrary"))))(xa, ga, *rider.ins)
    own, landed = rider.result(outs, 1)
    return own[0] if comm is None else (own[0], landed)


def wgrad_std(xa, ga, name):
    T, M = xa.shape
    N = ga.shape[1]
    tk = min(2048, T)
    return matmul_tn(xa, ga, pl.BlockSpec((tk, M), lambda n, k: (k, 0)), pl.BlockSpec((tk, N), lambda n, k: (k, 0)),
                     (M, N), pl.BlockSpec((M, N), lambda n, k: (0, 0)), (M, N), (1, T // tk), name)


def wgrad_down(p4, df, name):
    nj, T, nb = p4.shape
    D = df.shape[1]
    tk = min(2048, T)
    return matmul_tn(p4, df, pl.BlockSpec((None, tk, nb), lambda j, k: (j, k, 0)),
                     pl.BlockSpec((tk, D), lambda j, k: (k, 0)),
                     (nj * nb, D), pl.BlockSpec((nb, D), lambda j, k: (j, 0)), (nb, D), (nj, T // tk), name)


def wgrad_gu(u, dh8, name, comm=None):
    n8, T, nb = dh8.shape
    D = u.shape[1]
    tk = min(2048, T)
    return matmul_tn(u, dh8, pl.BlockSpec((tk, D), lambda j, k: (k, 0)),
                     pl.BlockSpec((None, tk, nb), lambda j, k: (j, k, 0)),
                     (n8, D, nb), pl.BlockSpec((None, D, nb), lambda j, k: (j, 0, 0)), (D, nb), (n8, T // tk), name,
                     comm=comm)


def wgrad_segments(u, segs, name):
    T, D = u.shape
    ns = len(segs)
    tk = min(1024, T)
    nk = T // tk

    def body(*refs):
        x_ref, g_refs, o_ref, acc = refs[0], refs[1:1 + ns], refs[1 + ns], refs[2 + ns]
        s = pl.program_id(0)
        k = pl.program_id(1)

        @pl.when(k == 0)
        def _():
            acc[...] = jnp.zeros_like(acc)
        for i in range(ns):
            @pl.when(s == i)
            def _(i=i):
                acc[...] += _dot_tn(x_ref[...], g_refs[i][...])

        @pl.when(k == nk - 1)
        def _():
            o_ref[...] = acc[...]

    seg_specs = [pl.BlockSpec((tk, D), lambda s, k, i=i: (jnp.where(s == i, k, 0), 0)) for i in range(ns)]
    return pl.pallas_call(
        body, name=name, grid=(ns, nk),
        in_specs=[pl.BlockSpec((tk, D), lambda s, k: (k, 0))] + seg_specs,
        out_specs=pl.BlockSpec((D, D), lambda s, k: (0, s)),
        out_shape=SDS((D, ns * D), F32), scratch_shapes=[pltpu.VMEM((D, D), F32)],
        compiler_params=_cparams(("parallel", "arbitrary")))(u, *segs)


def kernel(x, c, w_ada, b_ada, ffn1_w_gu, ffn1_w_down, ln1_g, ln1_b, w_in, w_sb_out, conv_w, conv_b, conv_ln_g, conv_ln_b, w_conv_out, w_out, ln2_g, ln2_b, ffn2_w_gu, ffn2_w_down, ln3_g, ln3_b, loss_target, m_w_ada, m_b_ada, m_ffn1_w_gu, m_ffn1_w_down, m_ln1_g, m_ln1_b, m_w_in, m_w_sb_out, m_conv_w, m_conv_b, m_conv_ln_g, m_conv_ln_b, m_w_conv_out, m_w_out, m_ln2_g, m_ln2_b, m_ffn2_w_gu, m_ffn2_w_down, m_ln3_g, m_ln3_b, v_w_ada, v_b_ada, v_ffn1_w_gu, v_ffn1_w_down, v_ln1_g, v_ln1_b, v_w_in, v_w_sb_out, v_conv_w, v_conv_b, v_conv_ln_g, v_conv_ln_b, v_w_conv_out, v_w_out, v_ln2_g, v_ln2_b, v_ffn2_w_gu, v_ffn2_w_down, v_ln3_g, v_ln3_b):
    Bl, S, D = x.shape
    T = Bl * S
    kw = conv_w.shape[1]
    ax, ay, ac = lax.axis_index("x"), lax.axis_index("y"), lax.axis_index("c")
    me = 4 * ax + 2 * ay + ac
    qc = jnp.stack([2 * ax + ay, ac]).astype(jnp.int32)

    big = dict(ffn1_w_gu=ffn1_w_gu[0], ffn1_w_down=ffn1_w_down[0], w_in=w_in[0], w_sb_out=w_sb_out[0],
               w_conv_out=w_conv_out[0], w_out=w_out[0], ffn2_w_gu=ffn2_w_gu[0], ffn2_w_down=ffn2_w_down[0])
    big_m = dict(ffn1_w_gu=m_ffn1_w_gu[0], ffn1_w_down=m_ffn1_w_down[0], w_in=m_w_in[0], w_sb_out=m_w_sb_out[0],
                 w_conv_out=m_w_conv_out[0], w_out=m_w_out[0], ffn2_w_gu=m_ffn2_w_gu[0], ffn2_w_down=m_ffn2_w_down[0])
    big_v = dict(ffn1_w_gu=v_ffn1_w_gu[0], ffn1_w_down=v_ffn1_w_down[0], w_in=v_w_in[0], w_sb_out=v_w_sb_out[0],
                 w_conv_out=v_w_conv_out[0], w_out=v_w_out[0], ffn2_w_gu=v_ffn2_w_gu[0], ffn2_w_down=v_ffn2_w_down[0])
    names = list(big)

    layer1, layer2, layer3 = ["ffn1_w_gu", "ffn1_w_down"], ["w_in", "w_sb_out", "w_conv_out", "w_out"], \
        ["ffn2_w_gu", "ffn2_w_down"]

    def shards(group):
        return [big[n].astype(BF) for n in group]

    wave0, wave1 = ["ffn1_w_gu"], ["ffn1_w_down", "w_in"]
    wave2, wave3 = ["w_sb_out", "w_conv_out", "w_out"], ["ffn2_w_gu", "ffn2_w_down"]
    G = dict(zip(wave0, run_comm(gather_plan(shards(wave0)), "allgather_ffn1")))
    wg1 = G["ffn1_w_gu"]

    cw8 = small_allgather(conv_w[0], "allgather_conv_w")
    cw_full = jnp.transpose(cw8, (1, 0, 2)).reshape(kw, D)
    cw32 = jnp.concatenate([cw_full, jnp.zeros((HALO - kw, D), F32)], axis=0)

    c_all = small_allgather(c, "allgather_c").reshape(N_DEV * Bl, D)
    ncol = w_ada.shape[2]
    b_cols = lax.dynamic_slice(b_ada, (0, me * ncol), (1, ncol))
    mod_cols, s_all = ada_fwd(c_all, w_ada[0], b_cols, "ada_fwd")
    mod8 = small_allgather(mod_cols, "allgather_mod")
    mod_mine = lax.dynamic_slice(mod8, (0, me * Bl, 0), (N_DEV, Bl, ncol))
    mod = jnp.transpose(mod_mine, (1, 0, 2)).reshape(Bl, N_MOD_ROWS, 1, D)
    sh1, sc1, g1, sh2, sc2, g2, sh3, sc3, g3 = [mod[:, i] for i in range(N_MOD_ROWS)]

    ones = jnp.ones((1, D), F32)
    zeros = jnp.zeros((1, D), F32)
    xf = x.reshape(T, D)
    tgt = loss_target.reshape(T, D)

    (u1, a1, gg1, p1), landed = ffn_up(xf, ones, zeros, sc1, sh1, wg1, "ffn1_up", comm=gather_plan(shards(wave1)))
    G.update(zip(wave1, landed))
    wd1 = G["ffn1_w_down"].reshape(wg1.shape[0] // 2, wg1.shape[2], D)
    win = jnp.transpose(G["w_in"], (1, 0, 2)).reshape(D, -1)
    (xh1, rs1, f1), landed = ffn_down_ln(p1, wd1, xf, ones, zeros, g1, "ffn1_down_ln",
                                         comm=gather_plan(shards(wave2)))
    G.update(zip(wave2, landed))
    wsb = G["w_sb_out"].reshape(D, D)
    wco = G["w_conv_out"].reshape(-1, D)
    wout = G["w_out"].reshape(D, D)
    (u2, qkv, proj), landed = mod_matmul(xh1, ln1_g, ln1_b, sc2, sh2, win, "in_proj",
                                         comm=gather_plan(shards(wave3)))
    G.update(zip(wave3, landed))
    wg2 = G["ffn2_w_gu"]
    wd2 = G["ffn2_w_down"].reshape(wg2.shape[0] // 2, wg2.shape[2], D)
    yatt, car = att_fwd(qkv, Bl, S, D, "att_fwd")
    cs, xhc, rsc = conv_fwd(proj, cw32, conv_b, conv_ln_g, conv_ln_b, Bl, S, D, kw, "conv_fwd")
    xh2, rs2, ysb, yco, merged, o2 = mix_fwd(yatt, cs, proj, wsb, wco, wout, xh1, ln1_g, ln1_b, g2, "mix_fwd")
    (u3, a3, gg3, p3), _ = ffn_up(xh2, ln2_g, ln2_b, sc3, sh3, wg2, "ffn2_up")
    (xh3, rs3, f3), _ = ffn_down_ln(p3, wd2, xh2, ln2_g, ln2_b, g3, "ffn2_down_ln")

    own_sum, recv_b = {}, {}

    def by_owner(group, grads):
        return [g.reshape((4, 2) + big[n].shape) for n, g in zip(group, grads)]

    def chip_sums(group, g42, recv_a):
        sums = [chip_sum(g, r, qc, "chip_sum_" + n) for g, r, n in zip(g42, recv_a, group)]
        own_sum.update({n: s[0] for n, s in zip(group, sums)})
        return chips_plan([s[1] for s in sums])

    dr3, df3, dln3g, dln3b, dg3, lossp, dh3 = ln_bwd(None, xh3, rs3, ln3_g, ln3_b, g3, f3, MACARON_WEIGHT,
                                                      "ln3_swiglu_bwd", target=tgt, swiglu=(wd2, a3, gg3))
    dh3 = dh3.reshape((-1,) + a3.shape[1:])
    gw_d2 = wgrad_down(p3, df3, "wgrad_ffn2_down")
    gw_g2 = wgrad_gu(u3, dh3, "wgrad_ffn2_gu")
    g42 = by_owner(layer3, [gw_g2, gw_d2])
    (dx2, dsc3, dsh3), recv_a = mod_bwd(dh3, wg2, dr3, xh2, ln2_g, ln2_b, sc3, True, "ffn2_mod_bwd",
                                        comm=sibling_plan(g42))
    plan3 = chip_sums(layer3, g42, recv_a)

    dr2, do2, dln2g, dln2b, dg2 = ln_bwd(dx2, xh2, rs2, ln2_g, ln2_b, g2, o2, 1.0, "ln2_bwd")
    gw_out = wgrad_std(merged, do2, "wgrad_out")
    dysb, dyco, dga, dgb, dyatt, dcs = merge_bwd(do2, proj, ysb, yco, wsb, wco, wout, "merge_bwd")
    gw_sb = wgrad_std(yatt, dysb, "wgrad_sb")
    gw_co = wgrad_std(cs, dyco, "wgrad_conv_out")
    (dq, dk, dv), landed = att_bwd(qkv, dyatt, car, Bl, S, D, "att_bwd", comm=plan3)
    recv_b.update(zip(layer3, landed))
    dconv, dcg, dcbeta, dcb = conv_ln_bwd(dcs, xhc, rsc, conv_ln_g, conv_ln_b, "conv_ln_bwd")
    dglu_a, dglu_b, dcw = conv_bwd(dconv, proj, cw32, Bl, S, D, kw, "conv_bwd")
    dproj = [dq, dk, dv, dglu_a, dglu_b, dga, dgb]
    gw_in = wgrad_segments(u2, dproj, "wgrad_in")
    gw_in = jnp.transpose(gw_in.reshape(D, N_DEV, -1), (1, 0, 2))
    g42 = by_owner(layer2, [gw_in, gw_sb, gw_co, gw_out])
    (dx1, dsc2, dsh2), recv_a = mod_bwd(dproj, win, dr2, xh1, ln1_g, ln1_b, sc2, False, "mix_mod_bwd",
                                        comm=sibling_plan(g42))
    plan2 = chip_sums(layer2, g42, recv_a)

    dr1, df1, dln1g, dln1b, dg1, dh1 = ln_bwd(dx1, xh1, rs1, ln1_g, ln1_b, g1, f1, MACARON_WEIGHT,
                                              "ln1_swiglu_bwd", swiglu=(wd1, a1, gg1))
    dh1 = dh1.reshape((-1,) + a1.shape[1:])
    gw_d1 = wgrad_down(p1, df1, "wgrad_ffn1_down")
    gw_g1, landed = wgrad_gu(u1, dh1, "wgrad_ffn1_gu", comm=plan2)
    recv_b.update(zip(layer2, landed))
    g42 = by_owner(layer1, [gw_g1, gw_d1])
    plan1 = chip_sums(layer1, g42, run_comm(sibling_plan(g42), "rs_sibling_ffn1"))
    (grad_x, dsc1, dsh1), landed = mod_bwd(dh1, wg1, dr1, xf, ones, zeros, sc1, True, "ffn1_mod_bwd", comm=plan1)
    recv_b.update(zip(layer1, landed))

    dmod = jnp.concatenate([dsh1, dsc1, dg1, dsh2, dsc2, dg2, dsh3, dsc3, dg3], axis=1).reshape(Bl, N_MOD_ROWS * D)
    dmod_all = small_allgather(dmod, "allgather_dmod").reshape(N_DEV * Bl, N_MOD_ROWS * D)
    dmod_cols = lax.dynamic_slice(dmod_all, (0, me * ncol), (N_DEV * Bl, ncol))
    g_w_ada, g_b_ada = ada_bwd(s_all, dmod_cols, dmod_all, "ada_bwd")

    npad = 16
    small_rows = [dln1g, dln1b, dcb, dcg, dcbeta, dln2g, dln2b, dln3g, dln3b,
                  jnp.broadcast_to(lossp[0:1, 0:1], (1, D))]
    pack = jnp.concatenate(small_rows + [jnp.zeros((npad - len(small_rows), D), F32), dcw], axis=0)
    small = small_sum(small_allgather(pack, "allgather_small"), "small_sum")
    loss = small[9, 0]
    small_w = [ln1_g, ln1_b, conv_b, conv_ln_g, conv_ln_b, ln2_g, ln2_b, ln3_g, ln3_b]
    small_m = [m_ln1_g, m_ln1_b, m_conv_b, m_conv_ln_g, m_conv_ln_b, m_ln2_g, m_ln2_b, m_ln3_g, m_ln3_b]
    small_v = [v_ln1_g, v_ln1_b, v_conv_b, v_conv_ln_g, v_conv_ln_b, v_ln2_g, v_ln2_b, v_ln3_g, v_ln3_b]
    padrows = jnp.zeros((npad - len(small_w), D), F32)
    s_g, s_d, s_m, s_v = adamw(jnp.concatenate(small_w + [padrows], axis=0),
                               jnp.concatenate(small_m + [padrows], axis=0),
                               jnp.concatenate(small_v + [padrows], axis=0),
                               [_plain_part(small[:npad])], "adamw_small")
    dcw_mine = lax.dynamic_slice(small[npad:npad + kw], (0, me * (D // N_DEV)), (kw, D // N_DEV))
    cw_g, cw_d, cw_m, cw_v = adamw(conv_w[0], m_conv_w[0], v_conv_w[0], [_plain_part(dcw_mine)], "adamw_conv_w")
    ada_g, ada_d, ada_m, ada_v = adamw(w_ada[0], m_w_ada[0], v_w_ada[0], [_plain_part(g_w_ada)], "adamw_w_ada")
    bada_g, bada_d, bada_m, bada_v = adamw(b_ada, m_b_ada, v_b_ada, [_plain_part(g_b_ada)], "adamw_b_ada")

    res = {}
    for n in names:
        rb = recv_b[n]
        parts = [_plain_part(own_sum[n]), _slot_part(rb, 0), _slot_part(rb, 1), _slot_part(rb, 2)]
        res[n] = adamw(big[n], big_m[n], big_v[n], parts, "adamw_" + n)

    def small_out(k):
        order = dict(ln1_g=0, ln1_b=1, conv_b=2, conv_ln_g=3, conv_ln_b=4, ln2_g=5, ln2_b=6, ln3_g=7, ln3_b=8)
        return lambda arr: arr[order[k]:order[k] + 1]

    weight_order = ["w_ada", "b_ada", "ffn1_w_gu", "ffn1_w_down", "ln1_g", "ln1_b", "w_in", "w_sb_out", "conv_w",
                    "conv_b", "conv_ln_g", "conv_ln_b", "w_conv_out", "w_out", "ln2_g", "ln2_b", "ffn2_w_gu",
                    "ffn2_w_down", "ln3_g", "ln3_b"]

    shapes = dict(w_ada=w_ada.shape, b_ada=b_ada.shape, conv_w=conv_w.shape, ln1_g=ln1_g.shape,
                  **{n: (1,) + big[n].shape for n in names})

    def pick(which):
        outs = []
        for n in weight_order:
            if n == "w_ada":
                a = (ada_g, ada_d, ada_m, ada_v)[which]
            elif n == "b_ada":
                a = (bada_g, bada_d, bada_m, bada_v)[which]
            elif n == "conv_w":
                a = (cw_g, cw_d, cw_m, cw_v)[which]
            elif n in res:
                a = res[n][which]
            else:
                a = small_out(n)((s_g, s_d, s_m, s_v)[which])
            outs.append(a.reshape(shapes.get(n, ln1_g.shape)))
        return outs

    return (loss, grad_x.reshape(Bl, S, D), *pick(0), *pick(1), *pick(2), *pick(3))
```

```python
import math

import jax
import jax.numpy as jnp
from jax import lax
from jax.experimental import pallas as pl
from jax.experimental.pallas import tpu as pltpu

F32 = jnp.float32
BF = jnp.bfloat16
SDS = jax.ShapeDtypeStruct
MESH = pl.DeviceIdType.MESH

N_DEV = 8
SB_HEAD_DIM = 64
N_MOD_ROWS = 9
MACARON_WEIGHT = 0.5
DEEPNORM_ALPHA = 2.0 ** 0.25
LN_EPS = 1e-5
ADAM_LR = 0.001
ADAM_B1 = 0.9
ADAM_B2 = 0.999
ADAM_EPS = 1e-08
ADAM_WD = 0.01
ADAM_STEP = 10

V7X_VMEM_LIMIT = 52 * 1024 * 1024
LANE = 128
SUBLANES = 8
HALO = 32


def _cparams(sem=None):
    return pltpu.CompilerParams(dimension_semantics=sem, vmem_limit_bytes=V7X_VMEM_LIMIT)


def _dot_nn(a, b):
    return lax.dot_general(a, b, (((1,), (0,)), ((), ())), preferred_element_type=F32)


def _dot_nt(a, b):
    return lax.dot_general(a, b, (((1,), (1,)), ((), ())), preferred_element_type=F32)


def _dot_tn(a, b):
    return lax.dot_general(a, b, (((0,), (0,)), ((), ())), preferred_element_type=F32)


def _sig(x):
    return 1.0 / (1.0 + jnp.exp(-x))


def _ln_stats(r):
    mu = jnp.mean(r, axis=-1, keepdims=True)
    d = r - mu
    var = jnp.mean(d * d, axis=-1, keepdims=True)
    rstd = lax.rsqrt(var + LN_EPS)
    return d * rstd, rstd


def _ln_bwd(dxh, xh, rstd):
    m1 = jnp.mean(dxh, axis=-1, keepdims=True)
    m2 = jnp.mean(dxh * xh, axis=-1, keepdims=True)
    return rstd * (dxh - m1 - xh * m2)


def _rowsum(v):
    return jnp.sum(v, axis=0, keepdims=True)


def _row_tile(n, cap):
    if n <= cap:
        return n
    best = None
    for t in range(8, cap + 1, 8):
        if n % t == 0:
            best = t
    assert best is not None, (n, cap)
    return best


def _coords():
    x, y, c = lax.axis_index("x"), lax.axis_index("y"), lax.axis_index("c")
    return x, y, c


def _flip(v, bit):
    return 1 - v if bit else v


def small_allgather(blk, name):
    r, n = blk.shape

    def body(x_ref, out_ref, send_sems, recv_sems):
        x, y, c = _coords()
        me = 4 * x + 2 * y + c
        out_ref[me] = x_ref[...]
        copies = []
        for k in range(1, N_DEV):
            peer = (_flip(x, k & 4), _flip(y, k & 2), _flip(c, k & 1))
            cp = pltpu.make_async_remote_copy(
                src_ref=x_ref, dst_ref=out_ref.at[me], send_sem=send_sems.at[k - 1],
                recv_sem=recv_sems.at[k - 1], device_id=peer, device_id_type=MESH)
            cp.start()
            copies.append(cp)
        for k in range(1, N_DEV):
            px, py, pc = _flip(x, k & 4), _flip(y, k & 2), _flip(c, k & 1)
            slot = 4 * px + 2 * py + pc
            pltpu.make_async_remote_copy(
                src_ref=x_ref, dst_ref=out_ref.at[slot], send_sem=send_sems.at[k - 1],
                recv_sem=recv_sems.at[k - 1], device_id=(px, py, pc), device_id_type=MESH).wait_recv()
        for cp in copies:
            cp.wait_send()

    return pl.pallas_call(
        body, name=name,
        out_shape=SDS((N_DEV, r, n), blk.dtype),
        in_specs=[pl.BlockSpec(memory_space=pltpu.VMEM)],
        out_specs=pl.BlockSpec(memory_space=pltpu.VMEM),
        scratch_shapes=[pltpu.SemaphoreType.DMA((N_DEV - 1,)), pltpu.SemaphoreType.DMA((N_DEV - 1,))],
    )(blk)


class CommPlan:
    def __init__(self, ins, out_shape, scratch, emit):
        self.ins, self.out_shape, self.scratch, self.emit = list(ins), list(out_shape), list(scratch), emit


def _phase(step, at, fn):
    if step is None:
        fn()
    else:
        pl.when(step == at)(fn)


def gather_plan(shards):
    n = len(shards)
    per = 7

    def emit(ins, outs, sems, step, nsteps):
        send_sems, recv_sems, local_sems = sems
        x, y, c = _coords()
        me = 4 * x + 2 * y + c
        sibling = (x, y, 1 - c)
        chips = [(1 - x, y), (x, 1 - y), (1 - x, 1 - y)]

        def slot(px, py, pc):
            return 4 * px + 2 * py + pc

        def copy(t, k, block, to, src=None):
            dst = outs[t].at[slot(*block)]
            return pltpu.make_async_remote_copy(
                src_ref=dst if src is None else src, dst_ref=dst,
                send_sem=send_sems.at[per * t + k], recv_sem=recv_sems.at[per * t + k],
                device_id=to, device_id_type=MESH)

        def local(t):
            return pltpu.make_async_copy(ins[t], outs[t].at[me], local_sems.at[t])

        def first(t):
            return [copy(t, 0, (x, y, c), sibling, src=ins[t])] + [
                copy(t, 1 + j, (x, y, c), (*chip, c), src=ins[t]) for j, chip in enumerate(chips)]

        def passed(t):
            return [copy(t, 4 + j, (*chip, c), sibling) for j, chip in enumerate(chips)]

        def start():
            for t in range(n):
                local(t).start()
                for cp in first(t):
                    cp.start()

        def forward():
            for t in range(n):
                for j, chip in enumerate(chips):
                    copy(t, 1 + j, (*chip, c), (x, y, c)).wait_recv()
                    passed(t)[j].start()

        def finish():
            for t in range(n):
                copy(t, 0, (x, y, 1 - c), (x, y, c)).wait_recv()
                for j, chip in enumerate(chips):
                    copy(t, 4 + j, (*chip, 1 - c), (x, y, c)).wait_recv()
            for t in range(n):
                for cp in first(t) + passed(t):
                    cp.wait_send()
                local(t).wait()

        _phase(step, 0, start)
        _phase(step, None if step is None else max(nsteps - max(2, nsteps // 8), 0), forward)
        _phase(step, None if step is None else nsteps - 1, finish)

    return CommPlan(shards, [SDS((N_DEV,) + s.shape, s.dtype) for s in shards],
                    [pltpu.SemaphoreType.DMA((per * n,)), pltpu.SemaphoreType.DMA((per * n,)),
                     pltpu.SemaphoreType.DMA((n,))], emit)


def chips_plan(sums):
    n = len(sums)

    def emit(ins, outs, sems, step, nsteps):
        send_sems, recv_sems = sems
        x, y, c = _coords()

        def copies():
            return [pltpu.make_async_remote_copy(
                src_ref=ins[t].at[j - 1], dst_ref=outs[t].at[j - 1], send_sem=send_sems.at[3 * t + j - 1],
                recv_sem=recv_sems.at[3 * t + j - 1], device_id=(_flip(x, j & 2), _flip(y, j & 1), c),
                device_id_type=MESH) for t in range(n) for j in range(1, 4)]

        def start():
            for cp in copies():
                cp.start()

        def finish():
            for cp in copies():
                cp.wait_recv()
            for cp in copies():
                cp.wait_send()

        _phase(step, 0, start)
        _phase(step, None if step is None else nsteps - 1, finish)

    return CommPlan(sums, [SDS(s.shape, s.dtype) for s in sums],
                    [pltpu.SemaphoreType.DMA((3 * n,)), pltpu.SemaphoreType.DMA((3 * n,))], emit)


def run_comm(plan, name):
    n, m = len(plan.ins), len(plan.out_shape)

    def body(*refs):
        plan.emit(refs[:n], refs[n:n + m], refs[n + m:], None, 1)

    anyspec = pl.BlockSpec(memory_space=pl.ANY)
    return pl.pallas_call(body, name=name, out_shape=plan.out_shape, in_specs=[anyspec] * n,
                          out_specs=[anyspec] * m, scratch_shapes=plan.scratch)(*plan.ins)


class _Rider:
    def __init__(self, plan):
        self.plan = plan
        anyspec = pl.BlockSpec(memory_space=pl.ANY)
        self.ins = plan.ins if plan else []
        self.in_specs = [anyspec] * len(self.ins)
        self.out_specs = [anyspec] * (len(plan.out_shape) if plan else 0)
        self.out_shape = plan.out_shape if plan else []
        self.scratch = plan.scratch if plan else []

    def split(self, refs, n_in, n_out, n_scratch=0):
        ni, no = len(self.ins), len(self.out_shape)
        own_in = refs[:n_in]
        c_in = refs[n_in:n_in + ni]
        own_out = refs[n_in + ni:n_in + ni + n_out]
        c_out = refs[n_in + ni + n_out:n_in + ni + n_out + no]
        rest = refs[n_in + ni + n_out + no:]
        own_scr, c_scr = rest[:n_scratch], rest[n_scratch:]

        def ride(step, nsteps):
            if self.plan:
                self.plan.emit(c_in, c_out, c_scr, step, nsteps)

        return tuple(own_in) + tuple(own_out) + tuple(own_scr), ride

    def result(self, outs, n_out):
        outs = list(outs) if isinstance(outs, (list, tuple)) else [outs]
        return outs[:n_out], (outs[n_out:] if self.plan else None)

    def sem(self, sem):
        return tuple("arbitrary" for _ in sem) if self.plan else sem


def sibling_plan(grads):
    n = len(grads)

    def emit(ins, outs, sems, step, nsteps):
        send_sems, recv_sems = sems
        x, y, c = _coords()

        def copies():
            return [pltpu.make_async_remote_copy(
                src_ref=ins[t].at[:, 1 - c], dst_ref=outs[t], send_sem=send_sems.at[t],
                recv_sem=recv_sems.at[t], device_id=(x, y, 1 - c), device_id_type=MESH) for t in range(n)]

        def start():
            for cp in copies():
                cp.start()

        def finish():
            for cp in copies():
                cp.wait_recv()
            for cp in copies():
                cp.wait_send()

        _phase(step, 0, start)
        _phase(step, None if step is None else nsteps - 1, finish)

    return CommPlan(grads, [SDS((4,) + g.shape[2:], g.dtype) for g in grads],
                    [pltpu.SemaphoreType.DMA((n,)), pltpu.SemaphoreType.DMA((n,))], emit)


def chip_sum(g42, recv, qc, name):
    _, _, R, C = g42.shape
    tr = _row_tile(R, 512)

    def body(qc_ref, a_ref, b_ref, own_ref, send_ref):
        j = pl.program_id(1)
        s = a_ref[...] + b_ref[...]

        @pl.when(j == 0)
        def _():
            own_ref[...] = s

        @pl.when(j > 0)
        def _():
            send_ref[...] = s.astype(BF)

    gs = pltpu.PrefetchScalarGridSpec(
        num_scalar_prefetch=1, grid=(R // tr, 4),
        in_specs=[pl.BlockSpec((None, None, tr, C), lambda i, j, s: (jnp.bitwise_xor(s[0], j), s[1], i, 0)),
                  pl.BlockSpec((None, tr, C), lambda i, j, s: (jnp.bitwise_xor(s[0], j), i, 0))],
        out_specs=[pl.BlockSpec((tr, C), lambda i, j, s: (i, 0)),
                   pl.BlockSpec((None, tr, C), lambda i, j, s: (jnp.maximum(j - 1, 0), i, 0))])
    return pl.pallas_call(body, name=name, grid_spec=gs, out_shape=[SDS((R, C), F32), SDS((3, R, C), BF)],
                          compiler_params=_cparams(("arbitrary", "arbitrary")))(qc, g42, recv)


def small_sum(g8, name):
    def body(g_ref, o_ref):
        acc = g_ref[0]
        for k in range(1, N_DEV):
            acc = acc + g_ref[k]
        o_ref[...] = acc
    return pl.pallas_call(body, name=name, out_shape=SDS(g8.shape[1:], F32))(g8)


def adamw(w, m, v, parts, name):
    R, C = w.shape
    tr = _row_tile(R, 512)
    npart = len(parts)
    c1 = 1.0 / (1.0 - ADAM_B1 ** ADAM_STEP)
    c2 = 1.0 / (1.0 - ADAM_B2 ** ADAM_STEP)

    def body(*refs):
        w_ref, m_ref, v_ref = refs[:3]
        p_refs = refs[3:3 + npart]
        g_ref, d_ref, nm_ref, nv_ref = refs[3 + npart:]
        g = p_refs[0][...].astype(F32)
        for p in p_refs[1:]:
            g = g + p[...].astype(F32)
        nm = ADAM_B1 * m_ref[...] + (1.0 - ADAM_B1) * g
        nv = ADAM_B2 * v_ref[...] + (1.0 - ADAM_B2) * (g * g)
        mh = nm * c1
        vh = nv * c2
        g_ref[...] = g
        nm_ref[...] = nm
        nv_ref[...] = nv
        d_ref[...] = -ADAM_LR * (mh / (jnp.sqrt(vh) + ADAM_EPS) + ADAM_WD * w_ref[...])

    wspec = pl.BlockSpec((tr, C), lambda i: (i, 0))
    pspecs = [pl.BlockSpec(bs(tr, C), im) for (_, bs, im) in parts]
    outs = pl.pallas_call(
        body, name=name, grid=(R // tr,),
        in_specs=[wspec] * 3 + pspecs, out_specs=[wspec] * 4,
        out_shape=[SDS((R, C), F32)] * 4,
        compiler_params=_cparams(("parallel",)))(w, m, v, *[p[0] for p in parts])
    return outs


def _plain_part(g):
    return (g, lambda tr, C: (tr, C), lambda i: (i, 0))


def _slot_part(g, slot):
    return (g, lambda tr, C: (None, tr, C), lambda i, s=slot: (s, i, 0))


def ada_fwd(c_all, w_cols, b_cols, name):
    Bg, D = c_all.shape
    n = w_cols.shape[1]

    def body(c_ref, w_ref, b_ref, o_ref, s_ref):
        cc = c_ref[...]
        s = cc * _sig(cc)
        s_ref[...] = s
        o_ref[...] = jnp.dot(s, w_ref[...], preferred_element_type=F32, precision=lax.Precision.HIGHEST) + b_ref[...]

    return pl.pallas_call(body, name=name, out_shape=[SDS((Bg, n), F32), SDS((Bg, D), F32)],
                          compiler_params=_cparams())(c_all, w_cols, b_cols)


def ada_bwd(s_all, dmod_cols, dmod_all, name):
    Bg, D = s_all.shape
    n = dmod_cols.shape[1]

    def body(s_ref, dc_ref, da_ref, gw_ref, gb_ref):
        gw_ref[...] = lax.dot_general(s_ref[...], dc_ref[...], (((0,), (0,)), ((), ())),
                                      preferred_element_type=F32, precision=lax.Precision.HIGHEST)
        acc = da_ref[0:1, :]
        for r in range(1, Bg):
            acc = acc + da_ref[r:r + 1, :]
        gb_ref[...] = acc

    return pl.pallas_call(body, name=name, out_shape=[SDS((D, n), F32), SDS((1, dmod_all.shape[1]), F32)],
                          compiler_params=_cparams())(s_all, dmod_cols, dmod_all)


def _vec(D, rank):
    return pl.BlockSpec((1, D), (lambda i: (0, 0)) if rank == 1 else (lambda i, j: (0, 0)))


def _modspec(D, tpb, rank):
    if rank == 1:
        return pl.BlockSpec((None, 1, D), lambda i: (i // tpb, 0, 0))
    return pl.BlockSpec((None, 1, D), lambda i, j: (i // tpb, 0, 0))


def _resident(shape):
    return pl.BlockSpec(shape, lambda *_: (0,) * len(shape), pipeline_mode=pl.Buffered(1))


def ffn_up(xs, pg, pb, sc, sh, wg8, name, comm=None):
    T, D = xs.shape
    n2, _, nb = wg8.shape
    nj = n2 // 2
    S = T // sc.shape[0]
    tm = min(512, S)
    tpb = S // tm
    rider = _Rider(comm)

    def body(*refs):
        (x_ref, pg_ref, pb_ref, sc_ref, sh_ref, w_ref, u_ref, a_ref, g_ref, p_ref), ride = rider.split(refs, 6, 4)
        ride(pl.program_id(0), T // tm)
        xin = x_ref[...] * pg_ref[...] + pb_ref[...]
        u_ref[...] = (xin * (1.0 + sc_ref[...]) + sh_ref[...]).astype(BF)

        def col_block(j, _):
            u = u_ref[...]
            a = _dot_nn(u, w_ref[j])
            g = _dot_nn(u, w_ref[j + nj])
            s = _sig(a)
            silu = a * s
            a_ref[j] = (g * (s * (1.0 + a * (1.0 - s)))).astype(BF)
            g_ref[j] = silu.astype(BF)
            p_ref[j] = (silu * g).astype(BF)
            return 0

        lax.fori_loop(0, nj, col_block, 0)

    blk = pl.BlockSpec((nj, tm, nb), lambda i: (0, i, 0))
    row = pl.BlockSpec((tm, D), lambda i: (i, 0))
    outs = pl.pallas_call(
        body, name=name, grid=(T // tm,),
        in_specs=[row, _vec(D, 1), _vec(D, 1), _modspec(D, tpb, 1), _modspec(D, tpb, 1), _resident(wg8.shape)]
        + rider.in_specs,
        out_specs=[row, blk, blk, blk] + rider.out_specs,
        out_shape=[SDS((T, D), BF)] + [SDS((nj, T, nb), BF)] * 3 + rider.out_shape,
        scratch_shapes=rider.scratch,
        compiler_params=_cparams(rider.sem(("parallel",))))(xs, pg, pb, sc, sh, wg8, *rider.ins)
    return rider.result(outs, 4)


def ffn_down_ln(p4, wd3, xs, pg, pb, gate, name, comm=None):
    nj, T, nb = p4.shape
    D = wd3.shape[2]
    S = T // gate.shape[0]
    tm = min(512, S)
    tpb = S // tm
    rider = _Rider(comm)

    def body(*refs):
        (p_ref, wd_ref, x_ref, pg_ref, pb_ref, gate_ref, xh_ref, rs_ref, f_ref), ride = rider.split(refs, 6, 3)
        ride(pl.program_id(0), T // tm)
        f = _dot_nn(p_ref[0], wd_ref[0])
        for k in range(1, nj):
            f = f + _dot_nn(p_ref[k], wd_ref[k])
        xin = x_ref[...] * pg_ref[...] + pb_ref[...]
        r = DEEPNORM_ALPHA * xin + gate_ref[...] * (MACARON_WEIGHT * f)
        xh, rstd = _ln_stats(r)
        xh_ref[...] = xh
        rs_ref[...] = rstd
        f_ref[...] = f.astype(BF)

    row = pl.BlockSpec((tm, D), lambda i: (i, 0))
    outs = pl.pallas_call(
        body, name=name, grid=(T // tm,),
        in_specs=[pl.BlockSpec((nj, tm, nb), lambda i: (0, i, 0)), _resident(wd3.shape),
                  row, _vec(D, 1), _vec(D, 1), _modspec(D, tpb, 1)] + rider.in_specs,
        out_specs=[row, pl.BlockSpec((tm, 1), lambda i: (i, 0)), row] + rider.out_specs,
        out_shape=[SDS((T, D), F32), SDS((T, 1), F32), SDS((T, D), BF)] + rider.out_shape,
        scratch_shapes=rider.scratch,
        compiler_params=_cparams(rider.sem(("parallel",))))(p4, wd3, xs, pg, pb, gate, *rider.ins)
    return rider.result(outs, 3)


N_QKV = 3


def mod_matmul(xs, pg, pb, sc, sh, w, name, comm=None):
    T, D = xs.shape
    N = w.shape[1]
    S = T // sc.shape[0]
    tm = min(256, S)
    tpb = S // tm
    rider = _Rider(comm)

    def body(*refs):
        (x_ref, pg_ref, pb_ref, sc_ref, sh_ref, w_ref, u_ref, qkv_ref, o_ref), ride = rider.split(refs, 6, 3)
        ride(pl.program_id(0), T // tm)
        xin = x_ref[...] * pg_ref[...] + pb_ref[...]
        u = (xin * (1.0 + sc_ref[...]) + sh_ref[...]).astype(BF)
        u_ref[...] = u
        for n in range(N // D):
            y = _dot_nn(u, w_ref[:, n * D:(n + 1) * D])
            if n < N_QKV:
                qkv_ref[:, n * D:(n + 1) * D] = y.astype(BF)
            else:
                o_ref[:, (n - N_QKV) * D:(n - N_QKV + 1) * D] = y

    row = pl.BlockSpec((tm, D), lambda i: (i, 0))
    outs = pl.pallas_call(
        body, name=name, grid=(T // tm,),
        in_specs=[row, _vec(D, 1), _vec(D, 1), _modspec(D, tpb, 1), _modspec(D, tpb, 1), _resident(w.shape)]
        + rider.in_specs,
        out_specs=[row, pl.BlockSpec((tm, N_QKV * D), lambda i: (i, 0)),
                   pl.BlockSpec((tm, N - N_QKV * D), lambda i: (i, 0))] + rider.out_specs,
        out_shape=[SDS((T, D), BF), SDS((T, N_QKV * D), BF), SDS((T, N - N_QKV * D), F32)] + rider.out_shape,
        scratch_shapes=rider.scratch,
        compiler_params=_cparams(rider.sem(("parallel",))))(xs, pg, pb, sc, sh, w, *rider.ins)
    return rider.result(outs, 3)


ATT_TQ = 2048
ATT_TK = 256


def _att_consts(tk):
    r = lax.broadcasted_iota(jnp.int32, (tk + 8, tk), 0)
    c = lax.broadcasted_iota(jnp.int32, (tk + 8, tk), 1)
    usum = jnp.where((r >= tk) | (c > r), 1.0, 0.0).astype(BF)
    lsum = jnp.where((r >= tk) | (c < r), 1.0, 0.0).astype(BF)
    dmask = lax.broadcasted_iota(jnp.int32, (tk, tk), 0) < lax.broadcasted_iota(jnp.int32, (tk, tk), 1)
    return usum, lsum, dmask


def _split_dot(m, v):
    hi = v.astype(BF)
    lo = (v - hi.astype(F32)).astype(BF)
    return _dot_nn(m, hi) + _dot_nn(m, lo)


def _softplus(z):
    return jnp.maximum(z, 0.0) + jnp.log(1.0 + jnp.exp(-jnp.abs(z)))


def _att_dims(S, D):
    dh = SB_HEAD_DIM
    cw = min(LANE, D)
    tq = min(ATT_TQ, S)
    tk = min(ATT_TK, tq)
    assert tq % tk == 0 and S % tq == 0
    return dh, cw, cw // dh, D // cw, tq, tk, S // tq, S // tk


def att_fwd(proj, Bl, S, D, name):
    dh, cw, hp, nblk, tq, tk, nq, nk = _att_dims(S, D)
    scale = 1.0 / math.sqrt(dh)
    assert math.log2(scale) == int(math.log2(scale))
    H = D // dh

    def body(q_ref, k_ref, v_ref, o_ref, car_ref, qs, ks, vts):
        usum, _, dmask = _att_consts(tk)
        for hh in range(hp):
            sl = slice(hh * dh, (hh + 1) * dh)
            qs[hh] = (q_ref[:, sl] * scale).astype(BF)
            ks[hh] = k_ref[:, sl].astype(BF)
            for kb in range(nk):
                vts[hh, kb] = v_ref[kb * tk:(kb + 1) * tk, sl].astype(F32).T.astype(BF)
        nch = tq // tk

        def qloop(qb, _):
            qo = pl.multiple_of(qb * tq, tq)
            n_full = qb * nch

            def blk(kb, state, diag):
                ko = pl.multiple_of(kb * tk, tk)
                chains = [(hh, c) for hh in range(hp) for c in range(0 if diag is None else diag, nch)]

                def masked(ch, val):
                    return jnp.where(dmask, val, 0.0) if ch[1] == diag else val

                z = {ch: _dot_nt(ks[ch[0], pl.ds(ko, tk), :], qs[ch[0], pl.ds(pl.multiple_of(qo + ch[1] * tk, tk), tk), :])
                     for ch in chains}
                sp = {ch: _softplus(z[ch]) for ch in chains}
                lk = {ch: masked(ch, -sp[ch]) for ch in chains}
                for hh, c in chains:
                    car_ref[hh, qb * nk + kb, :, c * tk:(c + 1) * tk] = state[hh][c][0]
                cs = {ch: _split_dot(usum, lk[ch]) for ch in chains}
                w = {ch: masked(ch, jnp.exp((z[ch] - sp[ch]) + state[ch[0]][ch[1]][0][0:1, :] + cs[ch][:tk]))
                     for ch in chains}
                pv = {ch: _dot_nn(vts[ch[0], kb], w[ch].astype(BF)) for ch in chains}
                return tuple(tuple(
                    (state[hh][c][0] + cs[(hh, c)][tk:], state[hh][c][1] + pv[(hh, c)]) if (hh, c) in z else state[hh][c]
                    for c in range(nch)) for hh in range(hp))

            state = tuple(tuple((jnp.zeros((8, tk), F32), jnp.zeros((dh, tk), F32)) for _ in range(nch))
                          for _ in range(hp))
            for i in reversed(range(nch)):
                state = blk(n_full + i, state, i)
            state = lax.fori_loop(0, n_full, lambda j, st: blk(n_full - 1 - j, st, None), state)
            for hh in range(hp):
                for c in range(nch):
                    o_ref[pl.ds(pl.multiple_of(qo + c * tk, tk), tk), hh * dh:(hh + 1) * dh] = (
                        state[hh][c][1].T.astype(BF))
            return 0

        lax.fori_loop(0, nq, qloop, 0)

    def seg(s):
        return pl.BlockSpec((S, cw), lambda b, h: (b, s * nblk + h))

    return pl.pallas_call(
        body, name=name, grid=(Bl, nblk),
        in_specs=[seg(0), seg(1), seg(2)],
        out_specs=[pl.BlockSpec((S, cw), lambda b, h: (b, h)),
                   pl.BlockSpec((None, hp, nq * nk, 8, tq), lambda b, h: (b, h, 0, 0, 0))],
        out_shape=[SDS((Bl * S, D), BF), SDS((Bl, H, nq * nk, 8, tq), F32)],
        scratch_shapes=[pltpu.VMEM((hp, S, dh), BF)] * 2 + [pltpu.VMEM((hp, nk, dh, tk), BF)],
        compiler_params=_cparams(("parallel", "parallel")))(proj, proj, proj)


def conv_fwd(proj, cw32, cb, cg, cbeta, Bl, S, D, kw, name):
    T = Bl * S
    ts = min(128, S)
    ns = S // ts
    off = HALO - (kw - 1)
    rc = min(64, ts)
    cw = min(LANE, D)

    def body(a_ref, b_ref, ha_ref, hb_ref, w_ref, cb_ref, g_ref, be_ref, cs_ref, xh_ref, rs_ref, hsh, conv_s):
        i = pl.program_id(1)
        h = a_ref[...] * _sig(b_ref[...])
        hh = jnp.where(i == 0, 0.0, ha_ref[...] * _sig(hb_ref[...]))
        for cb_ in range(D // cw):
            cols = slice(cb_ * cw, (cb_ + 1) * cw)
            hsh[0, cb_, pl.ds(HALO, ts), :] = h[:, cols]
            hsh[0, cb_, pl.ds(0, HALO), :] = hh[:, cols]
            for s in range(1, SUBLANES):
                hsh[s, cb_, pl.ds(0, ts + HALO - SUBLANES), :] = hsh[0, cb_, pl.ds(s, ts + HALO - SUBLANES), :]
            accs = [jnp.zeros((rc, cw), F32) for _ in range(ts // rc)]
            for k in range(kw):
                wk = w_ref[k:k + 1, cols]
                s = (off + k) % SUBLANES
                for r in range(ts // rc):
                    accs[r] = accs[r] + wk * hsh[s, cb_, pl.ds(r * rc + off + k - s, rc), :]
            for r in range(ts // rc):
                conv_s[pl.ds(r * rc, rc), cols] = accs[r]
        conv = conv_s[...] + cb_ref[...]
        xh, rstd = _ln_stats(conv)
        xh_ref[...] = xh
        rs_ref[...] = rstd
        cl = xh * g_ref[...] + be_ref[...]
        cs_ref[...] = (cl * _sig(cl)).astype(BF)

    hpb = ts // HALO

    def tile(seg):
        return pl.BlockSpec((ts, D), lambda b, i: (b * ns + i, seg))

    def halo(seg):
        return pl.BlockSpec((HALO, D), lambda b, i: (jnp.maximum((b * ns + i) * hpb - 1, 0), seg))

    row = pl.BlockSpec((ts, D), lambda b, i: (b * ns + i, 0))
    vec = pl.BlockSpec((1, D), lambda b, i: (0, 0))
    return pl.pallas_call(
        body, name=name, grid=(Bl, ns),
        in_specs=[tile(0), tile(1), halo(0), halo(1), pl.BlockSpec((HALO, D), lambda b, i: (0, 0)), vec, vec, vec],
        out_specs=[row, row, pl.BlockSpec((ts, 1), lambda b, i: (b * ns + i, 0))],
        out_shape=[SDS((T, D), BF), SDS((T, D), F32), SDS((T, 1), F32)],
        scratch_shapes=[pltpu.VMEM((SUBLANES, D // cw, ts + HALO, cw), F32), pltpu.VMEM((ts, D), F32)],
        compiler_params=_cparams(("parallel", "arbitrary")))(proj, proj, proj, proj, cw32, cb, cg, cbeta)


def mix_fwd(yatt, cs, proj, wsb, wco, wout, xs, pg, pb, gate, name):
    T, D = yatt.shape
    S = T // gate.shape[0]
    tm = min(256, S)
    tpb = S // tm

    def body(ya_ref, cs_ref, ga_ref, gb_ref, wsb_ref, wco_ref, wout_ref, x_ref, pg_ref, pb_ref, gate_ref,
             xh_ref, rs_ref, ysb_ref, yco_ref, mg_ref, o_ref):
        ysb = _dot_nn(ya_ref[...], wsb_ref[...])
        yco = _dot_nn(cs_ref[...], wco_ref[...])
        merged = _sig(ga_ref[...]) * ysb + _sig(gb_ref[...]) * yco
        mg = merged.astype(BF)
        o = _dot_nn(mg, wout_ref[...])
        xin = x_ref[...] * pg_ref[...] + pb_ref[...]
        r = DEEPNORM_ALPHA * xin + gate_ref[...] * o
        xh, rstd = _ln_stats(r)
        xh_ref[...] = xh
        rs_ref[...] = rstd
        ysb_ref[...] = ysb.astype(BF)
        yco_ref[...] = yco.astype(BF)
        mg_ref[...] = mg
        o_ref[...] = o.astype(BF)

    row = pl.BlockSpec((tm, D), lambda i: (i, 0))
    wfull = pl.BlockSpec((D, D), lambda i: (0, 0))
    return pl.pallas_call(
        body, name=name, grid=(T // tm,),
        in_specs=[row, row, pl.BlockSpec((tm, D), lambda i: (i, 2)), pl.BlockSpec((tm, D), lambda i: (i, 3)),
                  wfull, wfull, wfull, row, _vec(D, 1), _vec(D, 1), _modspec(D, tpb, 1)],
        out_specs=[row, pl.BlockSpec((tm, 1), lambda i: (i, 0)), row, row, row, row],
        out_shape=[SDS((T, D), F32), SDS((T, 1), F32)] + [SDS((T, D), BF)] * 4,
        compiler_params=_cparams(("parallel",)))(yatt, cs, proj, proj, wsb, wco, wout, xs, pg, pb, gate)


def ln_bwd(dout, xh, rstd, lng, lnb, gate, sub, res_w, name, target=None, swiglu=None):
    T, D = xh.shape
    Bl = gate.shape[0]
    S = T // Bl
    tm = min(256, S)
    tpb = S // tm
    first = target is not None
    n_in = 10 if swiglu else 7

    def body(*refs):
        do_ref, xh_ref, rs_ref, g_ref, b_ref, gate_ref, sub_ref = refs[:7]
        tg_ref = do_ref
        outs = refs[n_in:]
        dr_ref, ds_ref, dg_ref, db_ref, dgate_ref = outs[:5]
        loss_ref = outs[5] if first else None
        i = pl.program_id(0)
        xh_ = xh_ref[...]
        if first:
            diff = (xh_ * g_ref[...] + b_ref[...]) - tg_ref[...]
            lsum = jnp.sum(jnp.sum(diff * diff, axis=1, keepdims=True), axis=0, keepdims=True) * (0.5 / D)
            do = diff * (1.0 / D)
        else:
            do = do_ref[...]

        @pl.when(i == 0)
        def _():
            dg_ref[...] = jnp.zeros_like(dg_ref)
            db_ref[...] = jnp.zeros_like(db_ref)
            if first:
                loss_ref[...] = jnp.zeros_like(loss_ref)

        @pl.when(i % tpb == 0)
        def _():
            dgate_ref[...] = jnp.zeros_like(dgate_ref)

        if first:
            loss_ref[...] += jnp.broadcast_to(lsum, loss_ref.shape)
        dg_ref[...] += _rowsum(do * xh_)
        db_ref[...] += _rowsum(do)
        dr = _ln_bwd(do * g_ref[...], xh_, rs_ref[...])
        dr_ref[...] = dr
        ds_ref[...] = (dr * gate_ref[...] * res_w).astype(BF)
        dgate_ref[...] += _rowsum(dr * (res_w * sub_ref[...].astype(F32)))
        if swiglu:
            wd_ref, a_ref, gg_ref = refs[7:10]
            dh_ref = outs[-1]

            def col_block(j, _):
                dp = _dot_nt(ds_ref[...], wd_ref[j])
                dh_ref[0, j] = (dp * a_ref[j].astype(F32)).astype(BF)
                dh_ref[1, j] = (dp * gg_ref[j].astype(F32)).astype(BF)
                return 0

            lax.fori_loop(0, swiglu[1].shape[0], col_block, 0)

    row = pl.BlockSpec((tm, D), lambda i: (i, 0))
    vec = _vec(D, 1)
    mod = _modspec(D, tpb, 1)
    in_specs = [row, row, pl.BlockSpec((tm, 1), lambda i: (i, 0)), vec, vec, mod, row]
    out_specs = [row, row, vec, vec, mod]
    out_shape = [SDS((T, D), F32), SDS((T, D), BF), SDS((1, D), F32), SDS((1, D), F32), SDS((Bl, 1, D), F32)]
    if first:
        out_specs.append(pl.BlockSpec((8, LANE), lambda i: (0, 0)))
        out_shape.append(SDS((8, LANE), F32))
    extra = []
    if swiglu:
        wd3, a4, g4 = swiglu
        nj, _, nb = a4.shape
        blk = pl.BlockSpec((nj, tm, nb), lambda i: (0, i, 0))
        in_specs += [_resident(wd3.shape), blk, blk]
        out_specs.append(pl.BlockSpec((2, nj, tm, nb), lambda i: (0, 0, i, 0)))
        out_shape.append(SDS((2, nj, T, nb), BF))
        extra = [wd3, a4, g4]
    return pl.pallas_call(
        body, name=name, grid=(T // tm,), in_specs=in_specs, out_specs=out_specs, out_shape=out_shape,
        compiler_params=_cparams(("arbitrary",)))(target if first else dout, xh, rstd, lng, lnb, gate, sub, *extra)


def mod_bwd(dh, w, dr, xs, pg, pb, sc, blocked, name, comm=None):
    T, D = dr.shape
    Bl = sc.shape[0]
    S = T // Bl
    tm = min(512 if blocked else 256, S)
    tpb = S // tm
    row = pl.BlockSpec((tm, D), lambda i: (i, 0))
    if blocked:
        nk, _, kb = dh.shape
        dh_list, dh_specs = [dh], [pl.BlockSpec((nk, tm, kb), lambda i: (0, i, 0))]
    else:
        nk = len(dh)
        dh_list, dh_specs = list(dh), [row] * nk
    nd = len(dh_list)
    rider = _Rider(comm)

    def body(*refs):
        own, ride = rider.split(refs, nd + 6, 3)
        dh_refs = own[:nd]
        w_ref, dr_ref, x_ref, pg_ref, pb_ref, sc_ref, dx_ref, dsc_ref, dsh_ref = own[nd:]
        i = pl.program_id(0)
        ride(i, T // tm)

        def part(k):
            if blocked:
                return _dot_nt(dh_refs[0][k], w_ref[k])
            return _dot_nt(dh_refs[k][...], w_ref[:, k * D:(k + 1) * D])

        du = part(0)
        for k in range(1, nk):
            du = du + part(k)

        @pl.when(i % tpb == 0)
        def _():
            dsc_ref[...] = jnp.zeros_like(dsc_ref)
            dsh_ref[...] = jnp.zeros_like(dsh_ref)

        xin = x_ref[...] * pg_ref[...] + pb_ref[...]
        dx_ref[...] = DEEPNORM_ALPHA * dr_ref[...] + du * (1.0 + sc_ref[...])
        dsc_ref[...] += _rowsum(du * xin)
        dsh_ref[...] += _rowsum(du)

    mod = _modspec(D, tpb, 1)
    outs = pl.pallas_call(
        body, name=name, grid=(T // tm,),
        in_specs=dh_specs + [_resident(w.shape), row, row, _vec(D, 1), _vec(D, 1), mod] + rider.in_specs,
        out_specs=[row, mod, mod] + rider.out_specs,
        out_shape=[SDS((T, D), F32), SDS((Bl, 1, D), F32), SDS((Bl, 1, D), F32)] + rider.out_shape,
        scratch_shapes=rider.scratch,
        compiler_params=_cparams(("arbitrary",)))(*dh_list, w, dr, xs, pg, pb, sc, *rider.ins)
    return rider.result(outs, 3)


def merge_bwd(do2, proj, ysb, yco, wsb, wco, wout, name):
    T, D = do2.shape
    tm = min(256, T)

    def body(do_ref, ga_ref, gb_ref, ysb_ref, yco_ref, wsb_ref, wco_ref, wout_ref,
             dysb_ref, dyco_ref, dga_ref, dgb_ref, dya_ref, dcs_ref):
        dm = _dot_nt(do_ref[...], wout_ref[...])
        sa = _sig(ga_ref[...])
        sb = _sig(gb_ref[...])
        dysb = (dm * sa).astype(BF)
        dyco = (dm * sb).astype(BF)
        dysb_ref[...] = dysb
        dyco_ref[...] = dyco
        dga_ref[...] = (dm * ysb_ref[...].astype(F32) * (sa * (1.0 - sa))).astype(BF)
        dgb_ref[...] = (dm * yco_ref[...].astype(F32) * (sb * (1.0 - sb))).astype(BF)
        dya_ref[...] = _dot_nt(dysb, wsb_ref[...]).astype(BF)
        dcs_ref[...] = _dot_nt(dyco, wco_ref[...])

    row = pl.BlockSpec((tm, D), lambda i: (i, 0))
    wfull = pl.BlockSpec((D, D), lambda i: (0, 0))
    return pl.pallas_call(
        body, name=name, grid=(T // tm,),
        in_specs=[row, pl.BlockSpec((tm, D), lambda i: (i, 2)), pl.BlockSpec((tm, D), lambda i: (i, 3)),
                  row, row, wfull, wfull, wfull],
        out_specs=[row] * 6,
        out_shape=[SDS((T, D), BF)] * 5 + [SDS((T, D), F32)],
        compiler_params=_cparams(("parallel",)))(do2, proj, proj, ysb, yco, wsb, wco, wout)


def att_bwd(proj, dyatt, car, Bl, S, D, name, comm=None):
    dh, cw, hp, nblk, tq, tk, nq, nk = _att_dims(S, D)
    scale = 1.0 / math.sqrt(dh)
    rider = _Rider(comm)

    def body(*refs):
        (q_ref, k_ref, v_ref, do_ref, car_ref, dq_ref, dk_ref, dv_ref,
         qs, ks, vs, dos, kts, dk_acc, dv_acc), ride = rider.split(refs, 5, 3, 7)
        ride(pl.program_id(0) * nblk + pl.program_id(1), Bl * nblk)
        usum, lsum, dmask = _att_consts(tk)
        for hh in range(hp):
            sl = slice(hh * dh, (hh + 1) * dh)
            qs[hh] = (q_ref[:, sl] * scale).astype(BF)
            ks[hh] = k_ref[:, sl].astype(BF)
            vs[hh] = v_ref[:, sl].astype(BF)
            dos[hh] = do_ref[:, sl]
            for kb in range(nk):
                kts[hh, kb] = k_ref[kb * tk:(kb + 1) * tk, sl].astype(F32).T.astype(BF)
        if nq > 1:
            dk_acc[...] = jnp.zeros_like(dk_acc)
            dv_acc[...] = jnp.zeros_like(dv_acc)
        nch = tq // tk

        def qloop(qb, _):
            qo = pl.multiple_of(qb * tq, tq)
            n_full = qb * nch

            def blk(kb, state, diag):
                ko = pl.multiple_of(kb * tk, tk)
                chains = [(hh, c) for hh in range(hp) for c in range(0 if diag is None else diag, nch)]

                def masked(ch, val):
                    return jnp.where(dmask, val, 0.0) if ch[1] == diag else val

                def qrows(ref, ch):
                    return ref[ch[0], pl.ds(pl.multiple_of(qo + ch[1] * tk, tk), tk), :]

                k = [ks[hh, pl.ds(ko, tk), :] for hh in range(hp)]
                v = [vs[hh, pl.ds(ko, tk), :] for hh in range(hp)]
                z = {ch: _dot_nt(k[ch[0]], qrows(qs, ch)) for ch in chains}
                dw = {ch: _dot_nt(v[ch[0]], qrows(dos, ch)) for ch in chains}
                sp = {ch: _softplus(z[ch]) for ch in chains}
                lk = {ch: masked(ch, -sp[ch]) for ch in chains}
                cs = {ch: _split_dot(usum, lk[ch]) for ch in chains}
                w = {ch: masked(ch, jnp.exp((z[ch] - sp[ch])
                                            + car_ref[ch[0], qb * nk + kb, 0:1, ch[1] * tk:(ch[1] + 1) * tk]
                                            + cs[ch][:tk])) for ch in chains}
                dlw = {ch: dw[ch] * w[ch] for ch in chains}
                gs = {ch: _dot_nn(lsum, dlw[ch].astype(BF)) for ch in chains}
                sg = {ch: jnp.exp(z[ch] - sp[ch]) for ch in chains}
                dzb = {ch: masked(ch, dlw[ch] - sg[ch] * (dlw[ch] + state[ch[0]][ch[1]][0][0:1, :] + gs[ch][:tk])
                                  ).astype(BF) for ch in chains}
                wb = {ch: w[ch].astype(BF) for ch in chains}
                for hh in range(hp):
                    mine = [ch for ch in chains if ch[0] == hh]
                    dk_blk = sum(_dot_nn(dzb[ch], qrows(qs, ch)) for ch in mine)
                    dv_blk = sum(_dot_nn(wb[ch], qrows(dos, ch)) for ch in mine)
                    if nq > 1:
                        dk_acc[hh, kb] += dk_blk
                        dv_acc[hh, kb] += dv_blk
                    else:
                        dk_ref[pl.ds(ko, tk), hh * dh:(hh + 1) * dh] = dk_blk.astype(BF)
                        dv_ref[pl.ds(ko, tk), hh * dh:(hh + 1) * dh] = dv_blk.astype(BF)
                dq = {ch: _dot_nn(kts[ch[0], kb], dzb[ch]) for ch in chains}
                return tuple(tuple(
                    (state[hh][c][0] + gs[(hh, c)][tk:], state[hh][c][1] + dq[(hh, c)]) if (hh, c) in z else state[hh][c]
                    for c in range(nch)) for hh in range(hp))

            state = tuple(tuple((jnp.zeros((8, tk), F32), jnp.zeros((dh, tk), F32)) for _ in range(nch))
                          for _ in range(hp))
            state = lax.fori_loop(0, n_full, lambda kb, st: blk(kb, st, None), state)
            for i in range(nch):
                state = blk(n_full + i, state, i)
            for hh in range(hp):
                for c in range(nch):
                    dq_ref[pl.ds(pl.multiple_of(qo + c * tk, tk), tk), hh * dh:(hh + 1) * dh] = (
                        (state[hh][c][1].T * scale).astype(BF))
            return 0

        lax.fori_loop(0, nq, qloop, 0)
        if nq > 1:
            for hh in range(hp):
                sl = slice(hh * dh, (hh + 1) * dh)
                for kb in range(nk):
                    dk_ref[kb * tk:(kb + 1) * tk, sl] = dk_acc[hh, kb].astype(BF)
                    dv_ref[kb * tk:(kb + 1) * tk, sl] = dv_acc[hh, kb].astype(BF)

    def seg(s):
        return pl.BlockSpec((S, cw), lambda b, h: (b, s * nblk + h))

    blk_spec = pl.BlockSpec((S, cw), lambda b, h: (b, h))
    outs = pl.pallas_call(
        body, name=name, grid=(Bl, nblk),
        in_specs=[seg(0), seg(1), seg(2), blk_spec,
                  pl.BlockSpec((None, hp, nq * nk, 8, tq), lambda b, h: (b, h, 0, 0, 0))] + rider.in_specs,
        out_specs=[blk_spec, blk_spec, blk_spec] + rider.out_specs,
        out_shape=[SDS((Bl * S, D), BF)] * 3 + rider.out_shape,
        scratch_shapes=[pltpu.VMEM((hp, S, dh), BF)] * 4 + [pltpu.VMEM((hp, nk, dh, tk), BF)]
        + [pltpu.VMEM((hp, nk, tk, dh), F32)] * 2 + rider.scratch,
        compiler_params=_cparams(rider.sem(("parallel", "parallel"))))(proj, proj, proj, dyatt, car, *rider.ins)
    return rider.result(outs, 3)


def conv_ln_bwd(dcs, xhc, rstd_c, cg, cbeta, name):
    T, D = dcs.shape
    tm = min(256, T)

    def body(dcs_ref, xh_ref, rs_ref, g_ref, b_ref, dconv_ref, dg_ref, db_ref, dcb_ref):
        @pl.when(pl.program_id(0) == 0)
        def _():
            dg_ref[...] = jnp.zeros_like(dg_ref)
            db_ref[...] = jnp.zeros_like(db_ref)
            dcb_ref[...] = jnp.zeros_like(dcb_ref)
        xh = xh_ref[...]
        cl = xh * g_ref[...] + b_ref[...]
        s = _sig(cl)
        dcl = dcs_ref[...] * (s * (1.0 + cl * (1.0 - s)))
        dg_ref[...] += _rowsum(dcl * xh)
        db_ref[...] += _rowsum(dcl)
        dconv = _ln_bwd(dcl * g_ref[...], xh, rs_ref[...])
        dconv_ref[...] = dconv
        dcb_ref[...] += _rowsum(dconv)

    row = pl.BlockSpec((tm, D), lambda i: (i, 0))
    vec = _vec(D, 1)
    return pl.pallas_call(
        body, name=name, grid=(T // tm,),
        in_specs=[row, row, pl.BlockSpec((tm, 1), lambda i: (i, 0)), vec, vec],
        out_specs=[row, vec, vec, vec],
        out_shape=[SDS((T, D), F32)] + [SDS((1, D), F32)] * 3,
        compiler_params=_cparams(("arbitrary",)))(dcs, xhc, rstd_c, cg, cbeta)


def conv_bwd(dconv, proj, cw32, Bl, S, D, kw, name):
    T = Bl * S
    ts = min(128, S)
    ns = S // ts
    off = HALO - (kw - 1)
    rc = min(64, ts)
    cw = min(LANE, D)
    hpb = ts // HALO
    nhb = T // HALO

    def body(dc_ref, dcn_ref, a_ref, b_ref, ha_ref, hb_ref, w_ref, da_ref, db_ref, dw_ref, hsh, dsh, dh_s):
        b_ = pl.program_id(0)
        i = pl.program_id(1)
        span = ts + HALO - SUBLANES

        @pl.when((b_ == 0) & (i == 0))
        def _():
            dw_ref[...] = jnp.zeros_like(dw_ref)
        a = a_ref[...]
        sb = _sig(b_ref[...])
        h = a * sb
        hh = jnp.where(i == 0, 0.0, ha_ref[...] * _sig(hb_ref[...]))
        dc = dc_ref[...]
        dcn = jnp.where(i == ns - 1, 0.0, dcn_ref[...])
        for cb_ in range(D // cw):
            cols = slice(cb_ * cw, (cb_ + 1) * cw)
            hsh[0, cb_, pl.ds(HALO, ts), :] = h[:, cols]
            hsh[0, cb_, pl.ds(0, HALO), :] = hh[:, cols]
            dsh[0, cb_, pl.ds(0, ts), :] = dc[:, cols]
            dsh[0, cb_, pl.ds(ts, HALO), :] = dcn[:, cols]
            for s in range(1, SUBLANES):
                hsh[s, cb_, pl.ds(0, span), :] = hsh[0, cb_, pl.ds(s, span), :]
                dsh[s, cb_, pl.ds(0, span), :] = dsh[0, cb_, pl.ds(s, span), :]
            accs = [jnp.zeros((rc, cw), F32) for _ in range(ts // rc)]
            d0 = [dsh[0, cb_, pl.ds(r * rc, rc), :] for r in range(ts // rc)]
            for k in range(kw):
                wk = w_ref[k:k + 1, cols]
                wsum = jnp.zeros((rc, cw), F32)
                sd = ((kw - 1) - k) % SUBLANES
                sh_ = (off + k) % SUBLANES
                for r in range(ts // rc):
                    accs[r] = accs[r] + wk * dsh[sd, cb_, pl.ds(r * rc + (kw - 1) - k - sd, rc), :]
                    wsum = wsum + d0[r] * hsh[sh_, cb_, pl.ds(r * rc + off + k - sh_, rc), :]
                dw_ref[k:k + 1, cols] += _rowsum(wsum)
            for r in range(ts // rc):
                dh_s[pl.ds(r * rc, rc), cols] = accs[r]
        dhc = dh_s[...]
        da_ref[...] = (dhc * sb).astype(BF)
        db_ref[...] = (dhc * a * (sb * (1.0 - sb))).astype(BF)

    def tile(seg):
        return pl.BlockSpec((ts, D), lambda b, i: (b * ns + i, seg))

    def halo(seg):
        return pl.BlockSpec((HALO, D), lambda b, i: (jnp.maximum((b * ns + i) * hpb - 1, 0), seg))

    row = pl.BlockSpec((ts, D), lambda b, i: (b * ns + i, 0))
    nxt = pl.BlockSpec((HALO, D), lambda b, i: (jnp.minimum((b * ns + i + 1) * hpb, nhb - 1), 0))
    wspec = pl.BlockSpec((HALO, D), lambda b, i: (0, 0))
    return pl.pallas_call(
        body, name=name, grid=(Bl, ns),
        in_specs=[row, nxt, tile(0), tile(1), halo(0), halo(1), wspec],
        out_specs=[row, row, wspec],
        out_shape=[SDS((T, D), BF), SDS((T, D), BF), SDS((HALO, D), F32)],
        scratch_shapes=[pltpu.VMEM((SUBLANES, D // cw, ts + HALO, cw), F32)] * 2 + [pltpu.VMEM((ts, D), F32)],
        compiler_params=_cparams(("arbitrary", "arbitrary")))(dconv, dconv, proj, proj, proj, proj, cw32)


def matmul_tn(xa, ga, x_spec, g_spec, out_shape, out_spec, acc_shape, grid, name, comm=None):
    nk = grid[-1]
    rider = _Rider(comm)

    def body(*refs):
        (x_ref, g_ref, o_ref, acc), ride = rider.split(refs, 2, 1, 1)
        k = pl.program_id(1)
        ride(pl.program_id(0) * nk + k, grid[0] * nk)

        @pl.when(k == 0)
        def _():
            acc[...] = jnp.zeros_like(acc)
        acc[...] += _dot_tn(x_ref[...], g_ref[...])

        @pl.when(k == nk - 1)
        def _():
            o_ref[...] = acc[...]

    outs = pl.pallas_call(
        body, name=name, grid=grid, in_specs=[x_spec, g_spec] + rider.in_specs,
        out_specs=[out_spec] + rider.out_specs,
        out_shape=[SDS(out_shape, F32)] + rider.out_shape,
        scratch_shapes=[pltpu.VMEM(acc_shape, F32)] + rider.scratch,
        compiler_params=_cparams(rider.sem(("parallel", "arbitrary"))))(xa, ga, *rider.ins)
    own, landed = rider.result(outs, 1)
    return own[0] if comm is None else (own[0], landed)


def wgrad_std(xa, ga, name):
    T, M = xa.shape
    N = ga.shape[1]
    tk = min(2048, T)
    return matmul_tn(xa, ga, pl.BlockSpec((tk, M), lambda n, k: (k, 0)), pl.BlockSpec((tk, N), lambda n, k: (k, 0)),
                     (M, N), pl.BlockSpec((M, N), lambda n, k: (0, 0)), (M, N), (1, T // tk), name)


def wgrad_down(p4, df, name):
    nj, T, nb = p4.shape
    D = df.shape[1]
    tk = min(2048, T)
    return matmul_tn(p4, df, pl.BlockSpec((None, tk, nb), lambda j, k: (j, k, 0)),
                     pl.BlockSpec((tk, D), lambda j, k: (k, 0)),
                     (nj * nb, D), pl.BlockSpec((nb, D), lambda j, k: (j, 0)), (nb, D), (nj, T // tk), name)


def wgrad_gu(u, dh8, name, comm=None):
    n8, T, nb = dh8.shape
    D = u.shape[1]
    tk = min(2048, T)
    return matmul_tn(u, dh8, pl.BlockSpec((tk, D), lambda j, k: (k, 0)),
                     pl.BlockSpec((None, tk, nb), lambda j, k: (j, k, 0)),
                     (n8, D, nb), pl.BlockSpec((None, D, nb), lambda j, k: (j, 0, 0)), (D, nb), (n8, T // tk), name,
                     comm=comm)


def wgrad_segments(u, segs, name):
    T, D = u.shape
    ns = len(segs)
    tk = min(1024, T)
    nk = T // tk

    def body(*refs):
        x_ref, g_refs, o_ref, acc = refs[0], refs[1:1 + ns], refs[1 + ns], refs[2 + ns]
        s = pl.program_id(0)
        k = pl.program_id(1)

        @pl.when(k == 0)
        def _():
            acc[...] = jnp.zeros_like(acc)
        for i in range(ns):
            @pl.when(s == i)
            def _(i=i):
                acc[...] += _dot_tn(x_ref[...], g_refs[i][...])

        @pl.when(k == nk - 1)
        def _():
            o_ref[...] = acc[...]

    seg_specs = [pl.BlockSpec((tk, D), lambda s, k, i=i: (jnp.where(s == i, k, 0), 0)) for i in range(ns)]
    return pl.pallas_call(
        body, name=name, grid=(ns, nk),
        in_specs=[pl.BlockSpec((tk, D), lambda s, k: (k, 0))] + seg_specs,
        out_specs=pl.BlockSpec((D, D), lambda s, k: (0, s)),
        out_shape=SDS((D, ns * D), F32), scratch_shapes=[pltpu.VMEM((D, D), F32)],
        compiler_params=_cparams(("parallel", "arbitrary")))(u, *segs)


def kernel(x, c, w_ada, b_ada, ffn1_w_gu, ffn1_w_down, ln1_g, ln1_b, w_in, w_sb_out, conv_w, conv_b, conv_ln_g, conv_ln_b, w_conv_out, w_out, ln2_g, ln2_b, ffn2_w_gu, ffn2_w_down, ln3_g, ln3_b, loss_target, m_w_ada, m_b_ada, m_ffn1_w_gu, m_ffn1_w_down, m_ln1_g, m_ln1_b, m_w_in, m_w_sb_out, m_conv_w, m_conv_b, m_conv_ln_g, m_conv_ln_b, m_w_conv_out, m_w_out, m_ln2_g, m_ln2_b, m_ffn2_w_gu, m_ffn2_w_down, m_ln3_g, m_ln3_b, v_w_ada, v_b_ada, v_ffn1_w_gu, v_ffn1_w_down, v_ln1_g, v_ln1_b, v_w_in, v_w_sb_out, v_conv_w, v_conv_b, v_conv_ln_g, v_conv_ln_b, v_w_conv_out, v_w_out, v_ln2_g, v_ln2_b, v_ffn2_w_gu, v_ffn2_w_down, v_ln3_g, v_ln3_b):
    Bl, S, D = x.shape
    T = Bl * S
    kw = conv_w.shape[1]
    ax, ay, ac = lax.axis_index("x"), lax.axis_index("y"), lax.axis_index("c")
    me = 4 * ax + 2 * ay + ac
    qc = jnp.stack([2 * ax + ay, ac]).astype(jnp.int32)

    big = dict(ffn1_w_gu=ffn1_w_gu[0], ffn1_w_down=ffn1_w_down[0], w_in=w_in[0], w_sb_out=w_sb_out[0],
               w_conv_out=w_conv_out[0], w_out=w_out[0], ffn2_w_gu=ffn2_w_gu[0], ffn2_w_down=ffn2_w_down[0])
    big_m = dict(ffn1_w_gu=m_ffn1_w_gu[0], ffn1_w_down=m_ffn1_w_down[0], w_in=m_w_in[0], w_sb_out=m_w_sb_out[0],
                 w_conv_out=m_w_conv_out[0], w_out=m_w_out[0], ffn2_w_gu=m_ffn2_w_gu[0], ffn2_w_down=m_ffn2_w_down[0])
    big_v = dict(ffn1_w_gu=v_ffn1_w_gu[0], ffn1_w_down=v_ffn1_w_down[0], w_in=v_w_in[0], w_sb_out=v_w_sb_out[0],
                 w_conv_out=v_w_conv_out[0], w_out=v_w_out[0], ffn2_w_gu=v_ffn2_w_gu[0], ffn2_w_down=v_ffn2_w_down[0])
    names = list(big)

    layer1, layer2, layer3 = ["ffn1_w_gu", "ffn1_w_down"], ["w_in", "w_sb_out", "w_conv_out", "w_out"], \
        ["ffn2_w_gu", "ffn2_w_down"]

    def shards(group):
        return [big[n].astype(BF) for n in group]

    wave0, wave1 = ["ffn1_w_gu"], ["ffn1_w_down", "w_in"]
    wave2, wave3 = ["w_sb_out", "w_conv_out", "w_out"], ["ffn2_w_gu", "ffn2_w_down"]
    G = dict(zip(wave0, run_comm(gather_plan(shards(wave0)), "allgather_ffn1")))
    wg1 = G["ffn1_w_gu"]

    cw8 = small_allgather(conv_w[0], "allgather_conv_w")
    cw_full = jnp.transpose(cw8, (1, 0, 2)).reshape(kw, D)
    cw32 = jnp.concatenate([cw_full, jnp.zeros((HALO - kw, D), F32)], axis=0)

    c_all = small_allgather(c, "allgather_c").reshape(N_DEV * Bl, D)
    ncol = w_ada.shape[2]
    b_cols = lax.dynamic_slice(b_ada, (0, me * ncol), (1, ncol))
    mod_cols, s_all = ada_fwd(c_all, w_ada[0], b_cols, "ada_fwd")
    mod8 = small_allgather(mod_cols, "allgather_mod")
    mod_mine = lax.dynamic_slice(mod8, (0, me * Bl, 0), (N_DEV, Bl, ncol))
    mod = jnp.transpose(mod_mine, (1, 0, 2)).reshape(Bl, N_MOD_ROWS, 1, D)
    sh1, sc1, g1, sh2, sc2, g2, sh3, sc3, g3 = [mod[:, i] for i in range(N_MOD_ROWS)]

    ones = jnp.ones((1, D), F32)
    zeros = jnp.zeros((1, D), F32)
    xf = x.reshape(T, D)
    tgt = loss_target.reshape(T, D)

    (u1, a1, gg1, p1), landed = ffn_up(xf, ones, zeros, sc1, sh1, wg1, "ffn1_up", comm=gather_plan(shards(wave1)))
    G.update(zip(wave1, landed))
    wd1 = G["ffn1_w_down"].reshape(wg1.shape[0] // 2, wg1.shape[2], D)
    win = jnp.transpose(G["w_in"], (1, 0, 2)).reshape(D, -1)
    (xh1, rs1, f1), landed = ffn_down_ln(p1, wd1, xf, ones, zeros, g1, "ffn1_down_ln",
                                         comm=gather_plan(shards(wave2)))
    G.update(zip(wave2, landed))
    wsb = G["w_sb_out"].reshape(D, D)
    wco = G["w_conv_out"].reshape(-1, D)
    wout = G["w_out"].reshape(D, D)
    (u2, qkv, proj), landed = mod_matmul(xh1, ln1_g, ln1_b, sc2, sh2, win, "in_proj",
                                         comm=gather_plan(shards(wave3)))
    G.update(zip(wave3, landed))
    wg2 = G["ffn2_w_gu"]
    wd2 = G["ffn2_w_down"].reshape(wg2.shape[0] // 2, wg2.shape[2], D)
    yatt, car = att_fwd(qkv, Bl, S, D, "att_fwd")
    cs, xhc, rsc = conv_fwd(proj, cw32, conv_b, conv_ln_g, conv_ln_b, Bl, S, D, kw, "conv_fwd")
    xh2, rs2, ysb, yco, merged, o2 = mix_fwd(yatt, cs, proj, wsb, wco, wout, xh1, ln1_g, ln1_b, g2, "mix_fwd")
    (u3, a3, gg3, p3), _ = ffn_up(xh2, ln2_g, ln2_b, sc3, sh3, wg2, "ffn2_up")
    (xh3, rs3, f3), _ = ffn_down_ln(p3, wd2, xh2, ln2_g, ln2_b, g3, "ffn2_down_ln")

    own_sum, recv_b = {}, {}

    def by_owner(group, grads):
        return [g.reshape((4, 2) + big[n].shape) for n, g in zip(group, grads)]

    def chip_sums(group, g42, recv_a):
        sums = [chip_sum(g, r, qc, "chip_sum_" + n) for g, r, n in zip(g42, recv_a, group)]
        own_sum.update({n: s[0] for n, s in zip(group, sums)})
        return chips_plan([s[1] for s in sums])

    dr3, df3, dln3g, dln3b, dg3, lossp, dh3 = ln_bwd(None, xh3, rs3, ln3_g, ln3_b, g3, f3, MACARON_WEIGHT,
                                                      "ln3_swiglu_bwd", target=tgt, swiglu=(wd2, a3, gg3))
    dh3 = dh3.reshape((-1,) + a3.shape[1:])
    gw_d2 = wgrad_down(p3, df3, "wgrad_ffn2_down")
    gw_g2 = wgrad_gu(u3, dh3, "wgrad_ffn2_gu")
    g42 = by_owner(layer3, [gw_g2, gw_d2])
    (dx2, dsc3, dsh3), recv_a = mod_bwd(dh3, wg2, dr3, xh2, ln2_g, ln2_b, sc3, True, "ffn2_mod_bwd",
                                        comm=sibling_plan(g42))
    plan3 = chip_sums(layer3, g42, recv_a)

    dr2, do2, dln2g, dln2b, dg2 = ln_bwd(dx2, xh2, rs2, ln2_g, ln2_b, g2, o2, 1.0, "ln2_bwd")
    gw_out = wgrad_std(merged, do2, "wgrad_out")
    dysb, dyco, dga, dgb, dyatt, dcs = merge_bwd(do2, proj, ysb, yco, wsb, wco, wout, "merge_bwd")
    gw_sb = wgrad_std(yatt, dysb, "wgrad_sb")
    gw_co = wgrad_std(cs, dyco, "wgrad_conv_out")
    (dq, dk, dv), landed = att_bwd(qkv, dyatt, car, Bl, S, D, "att_bwd", comm=plan3)
    recv_b.update(zip(layer3, landed))
    dconv, dcg, dcbeta, dcb = conv_ln_bwd(dcs, xhc, rsc, conv_ln_g, conv_ln_b, "conv_ln_bwd")
    dglu_a, dglu_b, dcw = conv_bwd(dconv, proj, cw32, Bl, S, D, kw, "conv_bwd")
    dproj = [dq, dk, dv, dglu_a, dglu_b, dga, dgb]
    gw_in = wgrad_segments(u2, dproj, "wgrad_in")
    gw_in = jnp.transpose(gw_in.reshape(D, N_DEV, -1), (1, 0, 2))
    g42 = by_owner(layer2, [gw_in, gw_sb, gw_co, gw_out])
    (dx1, dsc2, dsh2), recv_a = mod_bwd(dproj, win, dr2, xh1, ln1_g, ln1_b, sc2, False, "mix_mod_bwd",
                                        comm=sibling_plan(g42))
    plan2 = chip_sums(layer2, g42, recv_a)

    dr1, df1, dln1g, dln1b, dg1, dh1 = ln_bwd(dx1, xh1, rs1, ln1_g, ln1_b, g1, f1, MACARON_WEIGHT,
                                              "ln1_swiglu_bwd", swiglu=(wd1, a1, gg1))
    dh1 = dh1.reshape((-1,) + a1.shape[1:])
    gw_d1 = wgrad_down(p1, df1, "wgrad_ffn1_down")
    gw_g1, landed = wgrad_gu(u1, dh1, "wgrad_ffn1_gu", comm=plan2)
    recv_b.update(zip(layer2, landed))
    g42 = by_owner(layer1, [gw_g1, gw_d1])
    plan1 = chip_sums(layer1, g42, run_comm(sibling_plan(g42), "rs_sibling_ffn1"))
    (grad_x, dsc1, dsh1), landed = mod_bwd(dh1, wg1, dr1, xf, ones, zeros, sc1, True, "ffn1_mod_bwd", comm=plan1)
    recv_b.update(zip(layer1, landed))

    dmod = jnp.concatenate([dsh1, dsc1, dg1, dsh2, dsc2, dg2, dsh3, dsc3, dg3], axis=1).reshape(Bl, N_MOD_ROWS * D)
    dmod_all = small_allgather(dmod, "allgather_dmod").reshape(N_DEV * Bl, N_MOD_ROWS * D)
    dmod_cols = lax.dynamic_slice(dmod_all, (0, me * ncol), (N_DEV * Bl, ncol))
    g_w_ada, g_b_ada = ada_bwd(s_all, dmod_cols, dmod_all, "ada_bwd")

    npad = 16
    small_rows = [dln1g, dln1b, dcb, dcg, dcbeta, dln2g, dln2b, dln3g, dln3b,
                  jnp.broadcast_to(lossp[0:1, 0:1], (1, D))]
    pack = jnp.concatenate(small_rows + [jnp.zeros((npad - len(small_rows), D), F32), dcw], axis=0)
    small = small_sum(small_allgather(pack, "allgather_small"), "small_sum")
    loss = small[9, 0]
    small_w = [ln1_g, ln1_b, conv_b, conv_ln_g, conv_ln_b, ln2_g, ln2_b, ln3_g, ln3_b]
    small_m = [m_ln1_g, m_ln1_b, m_conv_b, m_conv_ln_g, m_conv_ln_b, m_ln2_g, m_ln2_b, m_ln3_g, m_ln3_b]
    small_v = [v_ln1_g, v_ln1_b, v_conv_b, v_conv_ln_g, v_conv_ln_b, v_ln2_g, v_ln2_b, v_ln3_g, v_ln3_b]
    padrows = jnp.zeros((npad - len(small_w), D), F32)
    s_g, s_d, s_m, s_v = adamw(jnp.concatenate(small_w + [padrows], axis=0),
                               jnp.concatenate(small_m + [padrows], axis=0),
                               jnp.concatenate(small_v + [padrows], axis=0),
                               [_plain_part(small[:npad])], "adamw_small")
    dcw_mine = lax.dynamic_slice(small[npad:npad + kw], (0, me * (D // N_DEV)), (kw, D // N_DEV))
    cw_g, cw_d, cw_m, cw_v = adamw(conv_w[0], m_conv_w[0], v_conv_w[0], [_plain_part(dcw_mine)], "adamw_conv_w")
    ada_g, ada_d, ada_m, ada_v = adamw(w_ada[0], m_w_ada[0], v_w_ada[0], [_plain_part(g_w_ada)], "adamw_w_ada")
    bada_g, bada_d, bada_m, bada_v = adamw(b_ada, m_b_ada, v_b_ada, [_plain_part(g_b_ada)], "adamw_b_ada")

    res = {}
    for n in names:
        rb = recv_b[n]
        parts = [_plain_part(own_sum[n]), _slot_part(rb, 0), _slot_part(rb, 1), _slot_part(rb, 2)]
        res[n] = adamw(big[n], big_m[n], big_v[n], parts, "adamw_" + n)

    def small_out(k):
        order = dict(ln1_g=0, ln1_b=1, conv_b=2, conv_ln_g=3, conv_ln_b=4, ln2_g=5, ln2_b=6, ln3_g=7, ln3_b=8)
        return lambda arr: arr[order[k]:order[k] + 1]

    weight_order = ["w_ada", "b_ada", "ffn1_w_gu", "ffn1_w_down", "ln1_g", "ln1_b", "w_in", "w_sb_out", "conv_w",
                    "conv_b", "conv_ln_g", "conv_ln_b", "w_conv_out", "w_out", "ln2_g", "ln2_b", "ffn2_w_gu",
                    "ffn2_w_down", "ln3_g", "ln3_b"]

    shapes = dict(w_ada=w_ada.shape, b_ada=b_ada.shape, conv_w=conv_w.shape, ln1_g=ln1_g.shape,
                  **{n: (1,) + big[n].shape for n in names})

    def pick(which):
        outs = []
        for n in weight_order:
            if n == "w_ada":
                a = (ada_g, ada_d, ada_m, ada_v)[which]
            elif n == "b_ada":
                a = (bada_g, bada_d, bada_m, bada_v)[which]
            elif n == "conv_w":
                a = (cw_g, cw_d, cw_m, cw_v)[which]
            elif n in res:
                a = res[n][which]
            else:
                a = small_out(n)((s_g, s_d, s_m, s_v)[which])
            outs.append(a.reshape(shapes.get(n, ln1_g.shape)))
        return outs

    return (loss, grad_x.reshape(Bl, S, D), *pick(0), *pick(1), *pick(2), *pick(3))
```

```python
import math

import jax
import jax.numpy as jnp
from jax import lax
from jax.experimental import pallas as pl
from jax.experimental.pallas import tpu as pltpu

F32 = jnp.float32
BF = jnp.bfloat16
SDS = jax.ShapeDtypeStruct
MESH = pl.DeviceIdType.MESH

N_DEV = 8
SB_HEAD_DIM = 64
N_MOD_ROWS = 9
MACARON_WEIGHT = 0.5
DEEPNORM_ALPHA = 2.0 ** 0.25
LN_EPS = 1e-5
ADAM_LR = 0.001
ADAM_B1 = 0.9
ADAM_B2 = 0.999
ADAM_EPS = 1e-08
ADAM_WD = 0.01
ADAM_STEP = 10

V7X_VMEM_LIMIT = 52 * 1024 * 1024
LANE = 128
SUBLANES = 8
HALO = 32


def _cparams(sem=None):
    return pltpu.CompilerParams(dimension_semantics=sem, vmem_limit_bytes=V7X_VMEM_LIMIT)


def _dot_nn(a, b):
    return lax.dot_general(a, b, (((1,), (0,)), ((), ())), preferred_element_type=F32)


def _dot_nt(a, b):
    return lax.dot_general(a, b, (((1,), (1,)), ((), ())), preferred_element_type=F32)


def _dot_tn(a, b):
    return lax.dot_general(a, b, (((0,), (0,)), ((), ())), preferred_element_type=F32)


def _sig(x):
    return 1.0 / (1.0 + jnp.exp(-x))


def _ln_stats(r):
    mu = jnp.mean(r, axis=-1, keepdims=True)
    d = r - mu
    var = jnp.mean(d * d, axis=-1, keepdims=True)
    rstd = lax.rsqrt(var + LN_EPS)
    return d * rstd, rstd


def _ln_bwd(dxh, xh, rstd):
    m1 = jnp.mean(dxh, axis=-1, keepdims=True)
    m2 = jnp.mean(dxh * xh, axis=-1, keepdims=True)
    return rstd * (dxh - m1 - xh * m2)


def _rowsum(v):
    return jnp.sum(v, axis=0, keepdims=True)


def _row_tile(n, cap):
    if n <= cap:
        return n
    best = None
    for t in range(8, cap + 1, 8):
        if n % t == 0:
            best = t
    assert best is not None, (n, cap)
    return best


def _coords():
    x, y, c = lax.axis_index("x"), lax.axis_index("y"), lax.axis_index("c")
    return x, y, c


def _flip(v, bit):
    return 1 - v if bit else v


def small_allgather(blk, name):
    r, n = blk.shape

    def body(x_ref, out_ref, send_sems, recv_sems):
        x, y, c = _coords()
        me = 4 * x + 2 * y + c
        out_ref[me] = x_ref[...]
        copies = []
        for k in range(1, N_DEV):
            peer = (_flip(x, k & 4), _flip(y, k & 2), _flip(c, k & 1))
            cp = pltpu.make_async_remote_copy(
                src_ref=x_ref, dst_ref=out_ref.at[me], send_sem=send_sems.at[k - 1],
                recv_sem=recv_sems.at[k - 1], device_id=peer, device_id_type=MESH)
            cp.start()
            copies.append(cp)
        for k in range(1, N_DEV):
            px, py, pc = _flip(x, k & 4), _flip(y, k & 2), _flip(c, k & 1)
            slot = 4 * px + 2 * py + pc
            pltpu.make_async_remote_copy(
                src_ref=x_ref, dst_ref=out_ref.at[slot], send_sem=send_sems.at[k - 1],
                recv_sem=recv_sems.at[k - 1], device_id=(px, py, pc), device_id_type=MESH).wait_recv()
        for cp in copies:
            cp.wait_send()

    return pl.pallas_call(
        body, name=name,
        out_shape=SDS((N_DEV, r, n), blk.dtype),
        in_specs=[pl.BlockSpec(memory_space=pltpu.VMEM)],
        out_specs=pl.BlockSpec(memory_space=pltpu.VMEM),
        scratch_shapes=[pltpu.SemaphoreType.DMA((N_DEV - 1,)), pltpu.SemaphoreType.DMA((N_DEV - 1,))],
    )(blk)


class CommPlan:
    def __init__(self, ins, out_shape, scratch, emit):
        self.ins, self.out_shape, self.scratch, self.emit = list(ins), list(out_shape), list(scratch), emit


def _phase(step, at, fn):
    if step is None:
        fn()
    else:
        pl.when(step == at)(fn)


def gather_plan(shards):
    n = len(shards)
    per = 7

    def emit(ins, outs, sems, step, nsteps):
        send_sems, recv_sems, local_sems = sems
        x, y, c = _coords()
        me = 4 * x + 2 * y + c
        sibling = (x, y, 1 - c)
        chips = [(1 - x, y), (x, 1 - y), (1 - x, 1 - y)]

        def slot(px, py, pc):
            return 4 * px + 2 * py + pc

        def copy(t, k, block, to, src=None):
            dst = outs[t].at[slot(*block)]
            return pltpu.make_async_remote_copy(
                src_ref=dst if src is None else src, dst_ref=dst,
                send_sem=send_sems.at[per * t + k], recv_sem=recv_sems.at[per * t + k],
                device_id=to, device_id_type=MESH)

        def local(t):
            return pltpu.make_async_copy(ins[t], outs[t].at[me], local_sems.at[t])

        def first(t):
            return [copy(t, 0, (x, y, c), sibling, src=ins[t])] + [
                copy(t, 1 + j, (x, y, c), (*chip, c), src=ins[t]) for j, chip in enumerate(chips)]

        def passed(t):
            return [copy(t, 4 + j, (*chip, c), sibling) for j, chip in enumerate(chips)]

        def start():
            for t in range(n):
                local(t).start()
                for cp in first(t):
                    cp.start()

        def forward():
            for t in range(n):
                for j, chip in enumerate(chips):
                    copy(t, 1 + j, (*chip, c), (x, y, c)).wait_recv()
                    passed(t)[j].start()

        def finish():
            for t in range(n):
                copy(t, 0, (x, y, 1 - c), (x, y, c)).wait_recv()
                for j, chip in enumerate(chips):
                    copy(t, 4 + j, (*chip, 1 - c), (x, y, c)).wait_recv()
            for t in range(n):
                for cp in first(t) + passed(t):
                    cp.wait_send()
                local(t).wait()

        _phase(step, 0, start)
        _phase(step, None if step is None else max(nsteps - max(2, nsteps // 8), 0), forward)
        _phase(step, None if step is None else nsteps - 1, finish)

    return CommPlan(shards, [SDS((N_DEV,) + s.shape, s.dtype) for s in shards],
                    [pltpu.SemaphoreType.DMA((per * n,)), pltpu.SemaphoreType.DMA((per * n,)),
                     pltpu.SemaphoreType.DMA((n,))], emit)


def chips_plan(sums):
    n = len(sums)

    def emit(ins, outs, sems, step, nsteps):
        send_sems, recv_sems = sems
        x, y, c = _coords()

        def copies():
            return [pltpu.make_async_remote_copy(
                src_ref=ins[t].at[j - 1], dst_ref=outs[t].at[j - 1], send_sem=send_sems.at[3 * t + j - 1],
                recv_sem=recv_sems.at[3 * t + j - 1], device_id=(_flip(x, j & 2), _flip(y, j & 1), c),
                device_id_type=MESH) for t in range(n) for j in range(1, 4)]

        def start():
            for cp in copies():
                cp.start()

        def finish():
            for cp in copies():
                cp.wait_recv()
            for cp in copies():
                cp.wait_send()

        _phase(step, 0, start)
        _phase(step, None if step is None else nsteps - 1, finish)

    return CommPlan(sums, [SDS(s.shape, s.dtype) for s in sums],
                    [pltpu.SemaphoreType.DMA((3 * n,)), pltpu.SemaphoreType.DMA((3 * n,))], emit)


def run_comm(plan, name):
    n, m = len(plan.ins), len(plan.out_shape)

    def body(*refs):
        plan.emit(refs[:n], refs[n:n + m], refs[n + m:], None, 1)

    anyspec = pl.BlockSpec(memory_space=pl.ANY)
    return pl.pallas_call(body, name=name, out_shape=plan.out_shape, in_specs=[anyspec] * n,
                          out_specs=[anyspec] * m, scratch_shapes=plan.scratch)(*plan.ins)


class _Rider:
    def __init__(self, plan):
        self.plan = plan
        anyspec = pl.BlockSpec(memory_space=pl.ANY)
        self.ins = plan.ins if plan else []
        self.in_specs = [anyspec] * len(self.ins)
        self.out_specs = [anyspec] * (len(plan.out_shape) if plan else 0)
        self.out_shape = plan.out_shape if plan else []
        self.scratch = plan.scratch if plan else []

    def split(self, refs, n_in, n_out, n_scratch=0):
        ni, no = len(self.ins), len(self.out_shape)
        own_in = refs[:n_in]
        c_in = refs[n_in:n_in + ni]
        own_out = refs[n_in + ni:n_in + ni + n_out]
        c_out = refs[n_in + ni + n_out:n_in + ni + n_out + no]
        rest = refs[n_in + ni + n_out + no:]
        own_scr, c_scr = rest[:n_scratch], rest[n_scratch:]

        def ride(step, nsteps):
            if self.plan:
                self.plan.emit(c_in, c_out, c_scr, step, nsteps)

        return tuple(own_in) + tuple(own_out) + tuple(own_scr), ride

    def result(self, outs, n_out):
        outs = list(outs) if isinstance(outs, (list, tuple)) else [outs]
        return outs[:n_out], (outs[n_out:] if self.plan else None)

    def sem(self, sem):
        return tuple("arbitrary" for _ in sem) if self.plan else sem


def sibling_plan(grads):
    n = len(grads)

    def emit(ins, outs, sems, step, nsteps):
        send_sems, recv_sems = sems
        x, y, c = _coords()

        def copies():
            return [pltpu.make_async_remote_copy(
                src_ref=ins[t].at[:, 1 - c], dst_ref=outs[t], send_sem=send_sems.at[t],
                recv_sem=recv_sems.at[t], device_id=(x, y, 1 - c), device_id_type=MESH) for t in range(n)]

        def start():
            for cp in copies():
                cp.start()

        def finish():
            for cp in copies():
                cp.wait_recv()
            for cp in copies():
                cp.wait_send()

        _phase(step, 0, start)
        _phase(step, None if step is None else nsteps - 1, finish)

    return CommPlan(grads, [SDS((4,) + g.shape[2:], g.dtype) for g in grads],
                    [pltpu.SemaphoreType.DMA((n,)), pltpu.SemaphoreType.DMA((n,))], emit)


def chip_sum(g42, recv, qc, name):
    _, _, R, C = g42.shape
    tr = _row_tile(R, 512)

    def body(qc_ref, a_ref, b_ref, own_ref, send_ref):
        j = pl.program_id(1)
        s = a_ref[...] + b_ref[...]

        @pl.when(j == 0)
        def _():
            own_ref[...] = s

        @pl.when(j > 0)
        def _():
            send_ref[...] = s.astype(BF)

    gs = pltpu.PrefetchScalarGridSpec(
        num_scalar_prefetch=1, grid=(R // tr, 4),
        in_specs=[pl.BlockSpec((None, None, tr, C), lambda i, j, s: (jnp.bitwise_xor(s[0], j), s[1], i, 0)),
                  pl.BlockSpec((None, tr, C), lambda i, j, s: (jnp.bitwise_xor(s[0], j), i, 0))],
        out_specs=[pl.BlockSpec((tr, C), lambda i, j, s: (i, 0)),
                   pl.BlockSpec((None, tr, C), lambda i, j, s: (jnp.maximum(j - 1, 0), i, 0))])
    return pl.pallas_call(body, name=name, grid_spec=gs, out_shape=[SDS((R, C), F32), SDS((3, R, C), BF)],
                          compiler_params=_cparams(("arbitrary", "arbitrary")))(qc, g42, recv)


def small_sum(g8, name):
    def body(g_ref, o_ref):
        acc = g_ref[0]
        for k in range(1, N_DEV):
            acc = acc + g_ref[k]
        o_ref[...] = acc
    return pl.pallas_call(body, name=name, out_shape=SDS(g8.shape[1:], F32))(g8)


def adamw(w, m, v, parts, name):
    R, C = w.shape
    tr = _row_tile(R, 512)
    npart = len(parts)
    c1 = 1.0 / (1.0 - ADAM_B1 ** ADAM_STEP)
    c2 = 1.0 / (1.0 - ADAM_B2 ** ADAM_STEP)

    def body(*refs):
        w_ref, m_ref, v_ref = refs[:3]
        p_refs = refs[3:3 + npart]
        g_ref, d_ref, nm_ref, nv_ref = refs[3 + npart:]
        g = p_refs[0][...].astype(F32)
        for p in p_refs[1:]:
            g = g + p[...].astype(F32)
        nm = ADAM_B1 * m_ref[...] + (1.0 - ADAM_B1) * g
        nv = ADAM_B2 * v_ref[...] + (1.0 - ADAM_B2) * (g * g)
        mh = nm * c1
        vh = nv * c2
        g_ref[...] = g
        nm_ref[...] = nm
        nv_ref[...] = nv
        d_ref[...] = -ADAM_LR * (mh / (jnp.sqrt(vh) + ADAM_EPS) + ADAM_WD * w_ref[...])

    wspec = pl.BlockSpec((tr, C), lambda i: (i, 0))
    pspecs = [pl.BlockSpec(bs(tr, C), im) for (_, bs, im) in parts]
    outs = pl.pallas_call(
        body, name=name, grid=(R // tr,),
        in_specs=[wspec] * 3 + pspecs, out_specs=[wspec] * 4,
        out_shape=[SDS((R, C), F32)] * 4,
        compiler_params=_cparams(("parallel",)))(w, m, v, *[p[0] for p in parts])
    return outs


def _plain_part(g):
    return (g, lambda tr, C: (tr, C), lambda i: (i, 0))


def _slot_part(g, slot):
    return (g, lambda tr, C: (None, tr, C), lambda i, s=slot: (s, i, 0))


def ada_fwd(c_all, w_cols, b_cols, name):
    Bg, D = c_all.shape
    n = w_cols.shape[1]

    def body(c_ref, w_ref, b_ref, o_ref, s_ref):
        cc = c_ref[...]
        s = cc * _sig(cc)
        s_ref[...] = s
        o_ref[...] = jnp.dot(s, w_ref[...], preferred_element_type=F32, precision=lax.Precision.HIGHEST) + b_ref[...]

    return pl.pallas_call(body, name=name, out_shape=[SDS((Bg, n), F32), SDS((Bg, D), F32)],
                          compiler_params=_cparams())(c_all, w_cols, b_cols)


def ada_bwd(s_all, dmod_cols, dmod_all, name):
    Bg, D = s_all.shape
    n = dmod_cols.shape[1]

    def body(s_ref, dc_ref, da_ref, gw_ref, gb_ref):
        gw_ref[...] = lax.dot_general(s_ref[...], dc_ref[...], (((0,), (0,)), ((), ())),
                                      preferred_element_type=F32, precision=lax.Precision.HIGHEST)
        acc = da_ref[0:1, :]
        for r in range(1, Bg):
            acc = acc + da_ref[r:r + 1, :]
        gb_ref[...] = acc

    return pl.pallas_call(body, name=name, out_shape=[SDS((D, n), F32), SDS((1, dmod_all.shape[1]), F32)],
                          compiler_params=_cparams())(s_all, dmod_cols, dmod_all)


def _vec(D, rank):
    return pl.BlockSpec((1, D), (lambda i: (0, 0)) if rank == 1 else (lambda i, j: (0, 0)))


def _modspec(D, tpb, rank):
    if rank == 1:
        return pl.BlockSpec((None, 1, D), lambda i: (i // tpb, 0, 0))
    return pl.BlockSpec((None, 1, D), lambda i, j: (i // tpb, 0, 0))


def _resident(shape):
    return pl.BlockSpec(shape, lambda *_: (0,) * len(shape), pipeline_mode=pl.Buffered(1))


def ffn_up(xs, pg, pb, sc, sh, wg8, name, comm=None):
    T, D = xs.shape
    n2, _, nb = wg8.shape
    nj = n2 // 2
    S = T // sc.shape[0]
    tm = min(512, S)
    tpb = S // tm
    rider = _Rider(comm)

    def body(*refs):
        (x_ref, pg_ref, pb_ref, sc_ref, sh_ref, w_ref, u_ref, a_ref, g_ref, p_ref), ride = rider.split(refs, 6, 4)
        ride(pl.program_id(0), T // tm)
        xin = x_ref[...] * pg_ref[...] + pb_ref[...]
        u_ref[...] = (xin * (1.0 + sc_ref[...]) + sh_ref[...]).astype(BF)

        def col_block(j, _):
            u = u_ref[...]
            a = _dot_nn(u, w_ref[j])
            g = _dot_nn(u, w_ref[j + nj])
            s = _sig(a)
            silu = a * s
            a_ref[j] = (g * (s * (1.0 + a * (1.0 - s)))).astype(BF)
            g_ref[j] = silu.astype(BF)
            p_ref[j] = (silu * g).astype(BF)
            return 0

        lax.fori_loop(0, nj, col_block, 0)

    blk = pl.BlockSpec((nj, tm, nb), lambda i: (0, i, 0))
    row = pl.BlockSpec((tm, D), lambda i: (i, 0))
    outs = pl.pallas_call(
        body, name=name, grid=(T // tm,),
        in_specs=[row, _vec(D, 1), _vec(D, 1), _modspec(D, tpb, 1), _modspec(D, tpb, 1), _resident(wg8.shape)]
        + rider.in_specs,
        out_specs=[row, blk, blk, blk] + rider.out_specs,
        out_shape=[SDS((T, D), BF)] + [SDS((nj, T, nb), BF)] * 3 + rider.out_shape,
        scratch_shapes=rider.scratch,
        compiler_params=_cparams(rider.sem(("parallel",))))(xs, pg, pb, sc, sh, wg8, *rider.ins)
    return rider.result(outs, 4)


def ffn_down_ln(p4, wd3, xs, pg, pb, gate, name, comm=None):
    nj, T, nb = p4.shape
    D = wd3.shape[2]
    S = T // gate.shape[0]
    tm = min(512, S)
    tpb = S // tm
    rider = _Rider(comm)

    def body(*refs):
        (p_ref, wd_ref, x_ref, pg_ref, pb_ref, gate_ref, xh_ref, rs_ref, f_ref), ride = rider.split(refs, 6, 3)
        ride(pl.program_id(0), T // tm)
        f = _dot_nn(p_ref[0], wd_ref[0])
        for k in range(1, nj):
            f = f + _dot_nn(p_ref[k], wd_ref[k])
        xin = x_ref[...] * pg_ref[...] + pb_ref[...]
        r = DEEPNORM_ALPHA * xin + gate_ref[...] * (MACARON_WEIGHT * f)
        xh, rstd = _ln_stats(r)
        xh_ref[...] = xh
        rs_ref[...] = rstd
        f_ref[...] = f.astype(BF)

    row = pl.BlockSpec((tm, D), lambda i: (i, 0))
    outs = pl.pallas_call(
        body, name=name, grid=(T // tm,),
        in_specs=[pl.BlockSpec((nj, tm, nb), lambda i: (0, i, 0)), _resident(wd3.shape),
                  row, _vec(D, 1), _vec(D, 1), _modspec(D, tpb, 1)] + rider.in_specs,
        out_specs=[row, pl.BlockSpec((tm, 1), lambda i: (i, 0)), row] + rider.out_specs,
        out_shape=[SDS((T, D), F32), SDS((T, 1), F32), SDS((T, D), BF)] + rider.out_shape,
        scratch_shapes=rider.scratch,
        compiler_params=_cparams(rider.sem(("parallel",))))(p4, wd3, xs, pg, pb, gate, *rider.ins)
    return rider.result(outs, 3)


N_QKV = 3


def mod_matmul(xs, pg, pb, sc, sh, w, name, comm=None):
    T, D = xs.shape
    N = w.shape[1]
    S = T // sc.shape[0]
    tm = min(256, S)
    tpb = S // tm
    rider = _Rider(comm)

    def body(*refs):
        (x_ref, pg_ref, pb_ref, sc_ref, sh_ref, w_ref, u_ref, qkv_ref, o_ref), ride = rider.split(refs, 6, 3)
        ride(pl.program_id(0), T // tm)
        xin = x_ref[...] * pg_ref[...] + pb_ref[...]
        u = (xin * (1.0 + sc_ref[...]) + sh_ref[...]).astype(BF)
        u_ref[...] = u
        for n in range(N // D):
            y = _dot_nn(u, w_ref[:, n * D:(n + 1) * D])
            if n < N_QKV:
                qkv_ref[:, n * D:(n + 1) * D] = y.astype(BF)
            else:
                o_ref[:, (n - N_QKV) * D:(n - N_QKV + 1) * D] = y

    row = pl.BlockSpec((tm, D), lambda i: (i, 0))
    outs = pl.pallas_call(
        body, name=name, grid=(T // tm,),
        in_specs=[row, _vec(D, 1), _vec(D, 1), _modspec(D, tpb, 1), _modspec(D, tpb, 1), _resident(w.shape)]
        + rider.in_specs,
        out_specs=[row, pl.BlockSpec((tm, N_QKV * D), lambda i: (i, 0)),
                   pl.BlockSpec((tm, N - N_QKV * D), lambda i: (i, 0))] + rider.out_specs,
        out_shape=[SDS((T, D), BF), SDS((T, N_QKV * D), BF), SDS((T, N - N_QKV * D), F32)] + rider.out_shape,
        scratch_shapes=rider.scratch,
        compiler_params=_cparams(rider.sem(("parallel",))))(xs, pg, pb, sc, sh, w, *rider.ins)
    return rider.result(outs, 3)


ATT_TQ = 2048
ATT_TK = 256


def _att_consts(tk):
    r = lax.broadcasted_iota(jnp.int32, (tk + 8, tk), 0)
    c = lax.broadcasted_iota(jnp.int32, (tk + 8, tk), 1)
    usum = jnp.where((r >= tk) | (c > r), 1.0, 0.0).astype(BF)
    lsum = jnp.where((r >= tk) | (c < r), 1.0, 0.0).astype(BF)
    dmask = lax.broadcasted_iota(jnp.int32, (tk, tk), 0) < lax.broadcasted_iota(jnp.int32, (tk, tk), 1)
    return usum, lsum, dmask


def _split_dot(m, v):
    hi = v.astype(BF)
    lo = (v - hi.astype(F32)).astype(BF)
    return _dot_nn(m, hi) + _dot_nn(m, lo)


def _softplus(z):
    return jnp.maximum(z, 0.0) + jnp.log(1.0 + jnp.exp(-jnp.abs(z)))


def _att_dims(S, D):
    dh = SB_HEAD_DIM
    cw = min(LANE, D)
    tq = min(ATT_TQ, S)
    tk = min(ATT_TK, tq)
    assert tq % tk == 0 and S % tq == 0
    return dh, cw, cw // dh, D // cw, tq, tk, S // tq, S // tk


def att_fwd(proj, Bl, S, D, name):
    dh, cw, hp, nblk, tq, tk, nq, nk = _att_dims(S, D)
    scale = 1.0 / math.sqrt(dh)
    assert math.log2(scale) == int(math.log2(scale))
    H = D // dh

    def body(q_ref, k_ref, v_ref, o_ref, car_ref, qs, ks, vts):
        usum, _, dmask = _att_consts(tk)
        for hh in range(hp):
            sl = slice(hh * dh, (hh + 1) * dh)
            qs[hh] = (q_ref[:, sl] * scale).astype(BF)
            ks[hh] = k_ref[:, sl].astype(BF)
            for kb in range(nk):
                vts[hh, kb] = v_ref[kb * tk:(kb + 1) * tk, sl].astype(F32).T.astype(BF)
        nch = tq // tk

        def qloop(qb, _):
            qo = pl.multiple_of(qb * tq, tq)
            n_full = qb * nch

            def blk(kb, state, diag):
                ko = pl.multiple_of(kb * tk, tk)
                chains = [(hh, c) for hh in range(hp) for c in range(0 if diag is None else diag, nch)]

                def masked(ch, val):
                    return jnp.where(dmask, val, 0.0) if ch[1] == diag else val

                z = {ch: _dot_nt(ks[ch[0], pl.ds(ko, tk), :], qs[ch[0], pl.ds(pl.multiple_of(qo + ch[1] * tk, tk), tk), :])
                     for ch in chains}
                sp = {ch: _softplus(z[ch]) for ch in chains}
                lk = {ch: masked(ch, -sp[ch]) for ch in chains}
                for hh, c in chains:
                    car_ref[hh, qb * nk + kb, :, c * tk:(c + 1) * tk] = state[hh][c][0]
                cs = {ch: _split_dot(usum, lk[ch]) for ch in chains}
                w = {ch: masked(ch, jnp.exp((z[ch] - sp[ch]) + state[ch[0]][ch[1]][0][0:1, :] + cs[ch][:tk]))
                     for ch in chains}
                pv = {ch: _dot_nn(vts[ch[0], kb], w[ch].astype(BF)) for ch in chains}
                return tuple(tuple(
                    (state[hh][c][0] + cs[(hh, c)][tk:], state[hh][c][1] + pv[(hh, c)]) if (hh, c) in z else state[hh][c]
                    for c in range(nch)) for hh in range(hp))

            state = tuple(tuple((jnp.zeros((8, tk), F32), jnp.zeros((dh, tk), F32)) for _ in range(nch))
                          for _ in range(hp))
            for i in reversed(range(nch)):
                state = blk(n_full + i, state, i)
            state = lax.fori_loop(0, n_full, lambda j, st: blk(n_full - 1 - j, st, None), state)
            for hh in range(hp):
                for c in range(nch):
                    o_ref[pl.ds(pl.multiple_of(qo + c * tk, tk), tk), hh * dh:(hh + 1) * dh] = (
                        state[hh][c][1].T.astype(BF))
            return 0

        lax.fori_loop(0, nq, qloop, 0)

    def seg(s):
        return pl.BlockSpec((S, cw), lambda b, h: (b, s * nblk + h))

    return pl.pallas_call(
        body, name=name, grid=(Bl, nblk),
        in_specs=[seg(0), seg(1), seg(2)],
        out_specs=[pl.BlockSpec((S, cw), lambda b, h: (b, h)),
                   pl.BlockSpec((None, hp, nq * nk, 8, tq), lambda b, h: (b, h, 0, 0, 0))],
        out_shape=[SDS((Bl * S, D), BF), SDS((Bl, H, nq * nk, 8, tq), F32)],
        scratch_shapes=[pltpu.VMEM((hp, S, dh), BF)] * 2 + [pltpu.VMEM((hp, nk, dh, tk), BF)],
        compiler_params=_cparams(("parallel", "parallel")))(proj, proj, proj)


def conv_fwd(proj, cw32, cb, cg, cbeta, Bl, S, D, kw, name):
    T = Bl * S
    ts = min(128, S)
    ns = S // ts
    off = HALO - (kw - 1)
    rc = min(64, ts)
    cw = min(LANE, D)

    def body(a_ref, b_ref, ha_ref, hb_ref, w_ref, cb_ref, g_ref, be_ref, cs_ref, xh_ref, rs_ref, hsh, conv_s):
        i = pl.program_id(1)
        h = a_ref[...] * _sig(b_ref[...])
        hh = jnp.where(i == 0, 0.0, ha_ref[...] * _sig(hb_ref[...]))
        for cb_ in range(D // cw):
            cols = slice(cb_ * cw, (cb_ + 1) * cw)
            hsh[0, cb_, pl.ds(HALO, ts), :] = h[:, cols]
            hsh[0, cb_, pl.ds(0, HALO), :] = hh[:, cols]
            for s in range(1, SUBLANES):
                hsh[s, cb_, pl.ds(0, ts + HALO - SUBLANES), :] = hsh[0, cb_, pl.ds(s, ts + HALO - SUBLANES), :]
            accs = [jnp.zeros((rc, cw), F32) for _ in range(ts // rc)]
            for k in range(kw):
                wk = w_ref[k:k + 1, cols]
                s = (off + k) % SUBLANES
                for r in range(ts // rc):
                    accs[r] = accs[r] + wk * hsh[s, cb_, pl.ds(r * rc + off + k - s, rc), :]
            for r in range(ts // rc):
                conv_s[pl.ds(r * rc, rc), cols] = accs[r]
        conv = conv_s[...] + cb_ref[...]
        xh, rstd = _ln_stats(conv)
        xh_ref[...] = xh
        rs_ref[...] = rstd
        cl = xh * g_ref[...] + be_ref[...]
        cs_ref[...] = (cl * _sig(cl)).astype(BF)

    hpb = ts // HALO

    def tile(seg):
        return pl.BlockSpec((ts, D), lambda b, i: (b * ns + i, seg))

    def halo(seg):
        return pl.BlockSpec((HALO, D), lambda b, i: (jnp.maximum((b * ns + i) * hpb - 1, 0), seg))

    row = pl.BlockSpec((ts, D), lambda b, i: (b * ns + i, 0))
    vec = pl.BlockSpec((1, D), lambda b, i: (0, 0))
    return pl.pallas_call(
        body, name=name, grid=(Bl, ns),
        in_specs=[tile(0), tile(1), halo(0), halo(1), pl.BlockSpec((HALO, D), lambda b, i: (0, 0)), vec, vec, vec],
        out_specs=[row, row, pl.BlockSpec((ts, 1), lambda b, i: (b * ns + i, 0))],
        out_shape=[SDS((T, D), BF), SDS((T, D), F32), SDS((T, 1), F32)],
        scratch_shapes=[pltpu.VMEM((SUBLANES, D // cw, ts + HALO, cw), F32), pltpu.VMEM((ts, D), F32)],
        compiler_params=_cparams(("parallel", "arbitrary")))(proj, proj, proj, proj, cw32, cb, cg, cbeta)


def mix_fwd(yatt, cs, proj, wsb, wco, wout, xs, pg, pb, gate, name):
    T, D = yatt.shape
    S = T // gate.shape[0]
    tm = min(256, S)
    tpb = S // tm

    def body(ya_ref, cs_ref, ga_ref, gb_ref, wsb_ref, wco_ref, wout_ref, x_ref, pg_ref, pb_ref, gate_ref,
             xh_ref, rs_ref, ysb_ref, yco_ref, mg_ref, o_ref):
        ysb = _dot_nn(ya_ref[...], wsb_ref[...])
        yco = _dot_nn(cs_ref[...], wco_ref[...])
        merged = _sig(ga_ref[...]) * ysb + _sig(gb_ref[...]) * yco
        mg = merged.astype(BF)
        o = _dot_nn(mg, wout_ref[...])
        xin = x_ref[...] * pg_ref[...] + pb_ref[...]
        r = DEEPNORM_ALPHA * xin + gate_ref[...] * o
        xh, rstd = _ln_stats(r)
        xh_ref[...] = xh
        rs_ref[...] = rstd
        ysb_ref[...] = ysb.astype(BF)
        yco_ref[...] = yco.astype(BF)
        mg_ref[...] = mg
        o_ref[...] = o.astype(BF)

    row = pl.BlockSpec((tm, D), lambda i: (i, 0))
    wfull = pl.BlockSpec((D, D), lambda i: (0, 0))
    return pl.pallas_call(
        body, name=name, grid=(T // tm,),
        in_specs=[row, row, pl.BlockSpec((tm, D), lambda i: (i, 2)), pl.BlockSpec((tm, D), lambda i: (i, 3)),
                  wfull, wfull, wfull, row, _vec(D, 1), _vec(D, 1), _modspec(D, tpb, 1)],
        out_specs=[row, pl.BlockSpec((tm, 1), lambda i: (i, 0)), row, row, row, row],
        out_shape=[SDS((T, D), F32), SDS((T, 1), F32)] + [SDS((T, D), BF)] * 4,
        compiler_params=_cparams(("parallel",)))(yatt, cs, proj, proj, wsb, wco, wout, xs, pg, pb, gate)


def ln_bwd(dout, xh, rstd, lng, lnb, gate, sub, res_w, name, target=None, swiglu=None):
    T, D = xh.shape
    Bl = gate.shape[0]
    S = T // Bl
    tm = min(256, S)
    tpb = S // tm
    first = target is not None
    n_in = 10 if swiglu else 7

    def body(*refs):
        do_ref, xh_ref, rs_ref, g_ref, b_ref, gate_ref, sub_ref = refs[:7]
        tg_ref = do_ref
        outs = refs[n_in:]
        dr_ref, ds_ref, dg_ref, db_ref, dgate_ref = outs[:5]
        loss_ref = outs[5] if first else None
        i = pl.program_id(0)
        xh_ = xh_ref[...]
        if first:
            diff = (xh_ * g_ref[...] + b_ref[...]) - tg_ref[...]
            lsum = jnp.sum(jnp.sum(diff * diff, axis=1, keepdims=True), axis=0, keepdims=True) * (0.5 / D)
            do = diff * (1.0 / D)
        else:
            do = do_ref[...]

        @pl.when(i == 0)
        def _():
            dg_ref[...] = jnp.zeros_like(dg_ref)
            db_ref[...] = jnp.zeros_like(db_ref)
            if first:
                loss_ref[...] = jnp.zeros_like(loss_ref)

        @pl.when(i % tpb == 0)
        def _():
            dgate_ref[...] = jnp.zeros_like(dgate_ref)

        if first:
            loss_ref[...] += jnp.broadcast_to(lsum, loss_ref.shape)
        dg_ref[...] += _rowsum(do * xh_)
        db_ref[...] += _rowsum(do)
        dr = _ln_bwd(do * g_ref[...], xh_, rs_ref[...])
        dr_ref[...] = dr
        ds_ref[...] = (dr * gate_ref[...] * res_w).astype(BF)
        dgate_ref[...] += _rowsum(dr * (res_w * sub_ref[...].astype(F32)))
        if swiglu:
            wd_ref, a_ref, gg_ref = refs[7:10]
            dh_ref = outs[-1]

            def col_block(j, _):
                dp = _dot_nt(ds_ref[...], wd_ref[j])
                dh_ref[0, j] = (dp * a_ref[j].astype(F32)).astype(BF)
                dh_ref[1, j] = (dp * gg_ref[j].astype(F32)).astype(BF)
                return 0

            lax.fori_loop(0, swiglu[1].shape[0], col_block, 0)

    row = pl.BlockSpec((tm, D), lambda i: (i, 0))
    vec = _vec(D, 1)
    mod = _modspec(D, tpb, 1)
    in_specs = [row, row, pl.BlockSpec((tm, 1), lambda i: (i, 0)), vec, vec, mod, row]
    out_specs = [row, row, vec, vec, mod]
    out_shape = [SDS((T, D), F32), SDS((T, D), BF), SDS((1, D), F32), SDS((1, D), F32), SDS((Bl, 1, D), F32)]
    if first:
        out_specs.append(pl.BlockSpec((8, LANE), lambda i: (0, 0)))
        out_shape.append(SDS((8, LANE), F32))
    extra = []
    if swiglu:
        wd3, a4, g4 = swiglu
        nj, _, nb = a4.shape
        blk = pl.BlockSpec((nj, tm, nb), lambda i: (0, i, 0))
        in_specs += [_resident(wd3.shape), blk, blk]
        out_specs.append(pl.BlockSpec((2, nj, tm, nb), lambda i: (0, 0, i, 0)))
        out_shape.append(SDS((2, nj, T, nb), BF))
        extra = [wd3, a4, g4]
    return pl.pallas_call(
        body, name=name, grid=(T // tm,), in_specs=in_specs, out_specs=out_specs, out_shape=out_shape,
        compiler_params=_cparams(("arbitrary",)))(target if first else dout, xh, rstd, lng, lnb, gate, sub, *extra)


def mod_bwd(dh, w, dr, xs, pg, pb, sc, blocked, name, comm=None):
    T, D = dr.shape
    Bl = sc.shape[0]
    S = T // Bl
    tm = min(512 if blocked else 256, S)
    tpb = S // tm
    row = pl.BlockSpec((tm, D), lambda i: (i, 0))
    if blocked:
        nk, _, kb = dh.shape
        dh_list, dh_specs = [dh], [pl.BlockSpec((nk, tm, kb), lambda i: (0, i, 0))]
    else:
        nk = len(dh)
        dh_list, dh_specs = list(dh), [row] * nk
    nd = len(dh_list)
    rider = _Rider(comm)

    def body(*refs):
        own, ride = rider.split(refs, nd + 6, 3)
        dh_refs = own[:nd]
        w_ref, dr_ref, x_ref, pg_ref, pb_ref, sc_ref, dx_ref, dsc_ref, dsh_ref = own[nd:]
        i = pl.program_id(0)
        ride(i, T // tm)

        def part(k):
            if blocked:
                return _dot_nt(dh_refs[0][k], w_ref[k])
            return _dot_nt(dh_refs[k][...], w_ref[:, k * D:(k + 1) * D])

        du = part(0)
        for k in range(1, nk):
            du = du + part(k)

        @pl.when(i % tpb == 0)
        def _():
            dsc_ref[...] = jnp.zeros_like(dsc_ref)
            dsh_ref[...] = jnp.zeros_like(dsh_ref)

        xin = x_ref[...] * pg_ref[...] + pb_ref[...]
        dx_ref[...] = DEEPNORM_ALPHA * dr_ref[...] + du * (1.0 + sc_ref[...])
        dsc_ref[...] += _rowsum(du * xin)
        dsh_ref[...] += _rowsum(du)

    mod = _modspec(D, tpb, 1)
    outs = pl.pallas_call(
        body, name=name, grid=(T // tm,),
        in_specs=dh_specs + [_resident(w.shape), row, row, _vec(D, 1), _vec(D, 1), mod] + rider.in_specs,
        out_specs=[row, mod, mod] + rider.out_specs,
        out_shape=[SDS((T, D), F32), SDS((Bl, 1, D), F32), SDS((Bl, 1, D), F32)] + rider.out_shape,
        scratch_shapes=rider.scratch,
        compiler_params=_cparams(("arbitrary",)))(*dh_list, w, dr, xs, pg, pb, sc, *rider.ins)
    return rider.result(outs, 3)


def merge_bwd(do2, proj, ysb, yco, wsb, wco, wout, name):
    T, D = do2.shape
    tm = min(256, T)

    def body(do_ref, ga_ref, gb_ref, ysb_ref, yco_ref, wsb_ref, wco_ref, wout_ref,
             dysb_ref, dyco_ref, dga_ref, dgb_ref, dya_ref, dcs_ref):
        dm = _dot_nt(do_ref[...], wout_ref[...])
        sa = _sig(ga_ref[...])
        sb = _sig(gb_ref[...])
        dysb = (dm * sa).astype(BF)
        dyco = (dm * sb).astype(BF)
        dysb_ref[...] = dysb
        dyco_ref[...] = dyco
        dga_ref[...] = (dm * ysb_ref[...].astype(F32) * (sa * (1.0 - sa))).astype(BF)
        dgb_ref[...] = (dm * yco_ref[...].astype(F32) * (sb * (1.0 - sb))).astype(BF)
        dya_ref[...] = _dot_nt(dysb, wsb_ref[...]).astype(BF)
        dcs_ref[...] = _dot_nt(dyco, wco_ref[...])

    row = pl.BlockSpec((tm, D), lambda i: (i, 0))
    wfull = pl.BlockSpec((D, D), lambda i: (0, 0))
    return pl.pallas_call(
        body, name=name, grid=(T // tm,),
        in_specs=[row, pl.BlockSpec((tm, D), lambda i: (i, 2)), pl.BlockSpec((tm, D), lambda i: (i, 3)),
                  row, row, wfull, wfull, wfull],
        out_specs=[row] * 6,
        out_shape=[SDS((T, D), BF)] * 5 + [SDS((T, D), F32)],
        compiler_params=_cparams(("parallel",)))(do2, proj, proj, ysb, yco, wsb, wco, wout)


def att_bwd(proj, dyatt, car, Bl, S, D, name, comm=None):
    dh, cw, hp, nblk, tq, tk, nq, nk = _att_dims(S, D)
    scale = 1.0 / math.sqrt(dh)
    rider = _Rider(comm)

    def body(*refs):
        (q_ref, k_ref, v_ref, do_ref, car_ref, dq_ref, dk_ref, dv_ref,
         qs, ks, vs, dos, kts, dk_acc, dv_acc), ride = rider.split(refs, 5, 3, 7)
        ride(pl.program_id(0) * nblk + pl.program_id(1), Bl * nblk)
        usum, lsum, dmask = _att_consts(tk)
        for hh in range(hp):
            sl = slice(hh * dh, (hh + 1) * dh)
            qs[hh] = (q_ref[:, sl] * scale).astype(BF)
            ks[hh] = k_ref[:, sl].astype(BF)
            vs[hh] = v_ref[:, sl].astype(BF)
            dos[hh] = do_ref[:, sl]
            for kb in range(nk):
                kts[hh, kb] = k_ref[kb * tk:(kb + 1) * tk, sl].astype(F32).T.astype(BF)
        if nq > 1:
            dk_acc[...] = jnp.zeros_like(dk_acc)
            dv_acc[...] = jnp.zeros_like(dv_acc)
        nch = tq // tk

        def qloop(qb, _):
            qo = pl.multiple_of(qb * tq, tq)
            n_full = qb * nch

            def blk(kb, state, diag):
                ko = pl.multiple_of(kb * tk, tk)
                chains = [(hh, c) for hh in range(hp) for c in range(0 if diag is None else diag, nch)]

                def masked(ch, val):
                    return jnp.where(dmask, val, 0.0) if ch[1] == diag else val

                def qrows(ref, ch):
                    return ref[ch[0], pl.ds(pl.multiple_of(qo + ch[1] * tk, tk), tk), :]

                k = [ks[hh, pl.ds(ko, tk), :] for hh in range(hp)]
                v = [vs[hh, pl.ds(ko, tk), :] for hh in range(hp)]
                z = {ch: _dot_nt(k[ch[0]], qrows(qs, ch)) for ch in chains}
                dw = {ch: _dot_nt(v[ch[0]], qrows(dos, ch)) for ch in chains}
                sp = {ch: _softplus(z[ch]) for ch in chains}
                lk = {ch: masked(ch, -sp[ch]) for ch in chains}
                cs = {ch: _split_dot(usum, lk[ch]) for ch in chains}
                w = {ch: masked(ch, jnp.exp((z[ch] - sp[ch])
                                            + car_ref[ch[0], qb * nk + kb, 0:1, ch[1] * tk:(ch[1] + 1) * tk]
                                            + cs[ch][:tk])) for ch in chains}
                dlw = {ch: dw[ch] * w[ch] for ch in chains}
                gs = {ch: _dot_nn(lsum, dlw[ch].astype(BF)) for ch in chains}
                sg = {ch: jnp.exp(z[ch] - sp[ch]) for ch in chains}
                dzb = {ch: masked(ch, dlw[ch] - sg[ch] * (dlw[ch] + state[ch[0]][ch[1]][0][0:1, :] + gs[ch][:tk])
                                  ).astype(BF) for ch in chains}
                wb = {ch: w[ch].astype(BF) for ch in chains}
                for hh in range(hp):
                    mine = [ch for ch in chains if ch[0] == hh]
                    dk_blk = sum(_dot_nn(dzb[ch], qrows(qs, ch)) for ch in mine)
                    dv_blk = sum(_dot_nn(wb[ch], qrows(dos, ch)) for ch in mine)
                    if nq > 1:
                        dk_acc[hh, kb] += dk_blk
                        dv_acc[hh, kb] += dv_blk
                    else:
                        dk_ref[pl.ds(ko, tk), hh * dh:(hh + 1) * dh] = dk_blk.astype(BF)
                        dv_ref[pl.ds(ko, tk), hh * dh:(hh + 1) * dh] = dv_blk.astype(BF)
                dq = {ch: _dot_nn(kts[ch[0], kb], dzb[ch]) for ch in chains}
                return tuple(tuple(
                    (state[hh][c][0] + gs[(hh, c)][tk:], state[hh][c][1] + dq[(hh, c)]) if (hh, c) in z else state[hh][c]
                    for c in range(nch)) for hh in range(hp))

            state = tuple(tuple((jnp.zeros((8, tk), F32), jnp.zeros((dh, tk), F32)) for _ in range(nch))
                          for _ in range(hp))
            state = lax.fori_loop(0, n_full, lambda kb, st: blk(kb, st, None), state)
            for i in range(nch):
                state = blk(n_full + i, state, i)
            for hh in range(hp):
                for c in range(nch):
                    dq_ref[pl.ds(pl.multiple_of(qo + c * tk, tk), tk), hh * dh:(hh + 1) * dh] = (
                        (state[hh][c][1].T * scale).astype(BF))
            return 0

        lax.fori_loop(0, nq, qloop, 0)
        if nq > 1:
            for hh in range(hp):
                sl = slice(hh * dh, (hh + 1) * dh)
                for kb in range(nk):
                    dk_ref[kb * tk:(kb + 1) * tk, sl] = dk_acc[hh, kb].astype(BF)
                    dv_ref[kb * tk:(kb + 1) * tk, sl] = dv_acc[hh, kb].astype(BF)

    def seg(s):
        return pl.BlockSpec((S, cw), lambda b, h: (b, s * nblk + h))

    blk_spec = pl.BlockSpec((S, cw), lambda b, h: (b, h))
    outs = pl.pallas_call(
        body, name=name, grid=(Bl, nblk),
        in_specs=[seg(0), seg(1), seg(2), blk_spec,
                  pl.BlockSpec((None, hp, nq * nk, 8, tq), lambda b, h: (b, h, 0, 0, 0))] + rider.in_specs,
        out_specs=[blk_spec, blk_spec, blk_spec] + rider.out_specs,
        out_shape=[SDS((Bl * S, D), BF)] * 3 + rider.out_shape,
        scratch_shapes=[pltpu.VMEM((hp, S, dh), BF)] * 4 + [pltpu.VMEM((hp, nk, dh, tk), BF)]
        + [pltpu.VMEM((hp, nk, tk, dh), F32)] * 2 + rider.scratch,
        compiler_params=_cparams(rider.sem(("parallel", "parallel"))))(proj, proj, proj, dyatt, car, *rider.ins)
    return rider.result(outs, 3)


def conv_ln_bwd(dcs, xhc, rstd_c, cg, cbeta, name):
    T, D = dcs.shape
    tm = min(256, T)

    def body(dcs_ref, xh_ref, rs_ref, g_ref, b_ref, dconv_ref, dg_ref, db_ref, dcb_ref):
        @pl.when(pl.program_id(0) == 0)
        def _():
            dg_ref[...] = jnp.zeros_like(dg_ref)
            db_ref[...] = jnp.zeros_like(db_ref)
            dcb_ref[...] = jnp.zeros_like(dcb_ref)
        xh = xh_ref[...]
        cl = xh * g_ref[...] + b_ref[...]
        s = _sig(cl)
        dcl = dcs_ref[...] * (s * (1.0 + cl * (1.0 - s)))
        dg_ref[...] += _rowsum(dcl * xh)
        db_ref[...] += _rowsum(dcl)
        dconv = _ln_bwd(dcl * g_ref[...], xh, rs_ref[...])
        dconv_ref[...] = dconv
        dcb_ref[...] += _rowsum(dconv)

    row = pl.BlockSpec((tm, D), lambda i: (i, 0))
    vec = _vec(D, 1)
    return pl.pallas_call(
        body, name=name, grid=(T // tm,),
        in_specs=[row, row, pl.BlockSpec((tm, 1), lambda i: (i, 0)), vec, vec],
        out_specs=[row, vec, vec, vec],
        out_shape=[SDS((T, D), F32)] + [SDS((1, D), F32)] * 3,
        compiler_params=_cparams(("arbitrary",)))(dcs, xhc, rstd_c, cg, cbeta)


def conv_bwd(dconv, proj, cw32, Bl, S, D, kw, name):
    T = Bl * S
    ts = min(128, S)
    ns = S // ts
    off = HALO - (kw - 1)
    rc = min(64, ts)
    cw = min(LANE, D)
    hpb = ts // HALO
    nhb = T // HALO

    def body(dc_ref, dcn_ref, a_ref, b_ref, ha_ref, hb_ref, w_ref, da_ref, db_ref, dw_ref, hsh, dsh, dh_s):
        b_ = pl.program_id(0)
        i = pl.program_id(1)
        span = ts + HALO - SUBLANES

        @pl.when((b_ == 0) & (i == 0))
        def _():
            dw_ref[...] = jnp.zeros_like(dw_ref)
        a = a_ref[...]
        sb = _sig(b_ref[...])
        h = a * sb
        hh = jnp.where(i == 0, 0.0, ha_ref[...] * _sig(hb_ref[...]))
        dc = dc_ref[...]
        dcn = jnp.where(i == ns - 1, 0.0, dcn_ref[...])
        for cb_ in range(D // cw):
            cols = slice(cb_ * cw, (cb_ + 1) * cw)
            hsh[0, cb_, pl.ds(HALO, ts), :] = h[:, cols]
            hsh[0, cb_, pl.ds(0, HALO), :] = hh[:, cols]
            dsh[0, cb_, pl.ds(0, ts), :] = dc[:, cols]
            dsh[0, cb_, pl.ds(ts, HALO), :] = dcn[:, cols]
            for s in range(1, SUBLANES):
                hsh[s, cb_, pl.ds(0, span), :] = hsh[0, cb_, pl.ds(s, span), :]
                dsh[s, cb_, pl.ds(0, span), :] = dsh[0, cb_, pl.ds(s, span), :]
            accs = [jnp.zeros((rc, cw), F32) for _ in range(ts // rc)]
            d0 = [dsh[0, cb_, pl.ds(r * rc, rc), :] for r in range(ts // rc)]
            for k in range(kw):
                wk = w_ref[k:k + 1, cols]
                wsum = jnp.zeros((rc, cw), F32)
                sd = ((kw - 1) - k) % SUBLANES
                sh_ = (off + k) % SUBLANES
                for r in range(ts // rc):
                    accs[r] = accs[r] + wk * dsh[sd, cb_, pl.ds(r * rc + (kw - 1) - k - sd, rc), :]
                    wsum = wsum + d0[r] * hsh[sh_, cb_, pl.ds(r * rc + off + k - sh_, rc), :]
                dw_ref[k:k + 1, cols] += _rowsum(wsum)
            for r in range(ts // rc):
                dh_s[pl.ds(r * rc, rc), cols] = accs[r]
        dhc = dh_s[...]
        da_ref[...] = (dhc * sb).astype(BF)
        db_ref[...] = (dhc * a * (sb * (1.0 - sb))).astype(BF)

    def tile(seg):
        return pl.BlockSpec((ts, D), lambda b, i: (b * ns + i, seg))

    def halo(seg):
        return pl.BlockSpec((HALO, D), lambda b, i: (jnp.maximum((b * ns + i) * hpb - 1, 0), seg))

    row = pl.BlockSpec((ts, D), lambda b, i: (b * ns + i, 0))
    nxt = pl.BlockSpec((HALO, D), lambda b, i: (jnp.minimum((b * ns + i + 1) * hpb, nhb - 1), 0))
    wspec = pl.BlockSpec((HALO, D), lambda b, i: (0, 0))
    return pl.pallas_call(
        body, name=name, grid=(Bl, ns),
        in_specs=[row, nxt, tile(0), tile(1), halo(0), halo(1), wspec],
        out_specs=[row, row, wspec],
        out_shape=[SDS((T, D), BF), SDS((T, D), BF), SDS((HALO, D), F32)],
        scratch_shapes=[pltpu.VMEM((SUBLANES, D // cw, ts + HALO, cw), F32)] * 2 + [pltpu.VMEM((ts, D), F32)],
        compiler_params=_cparams(("arbitrary", "arbitrary")))(dconv, dconv, proj, proj, proj, proj, cw32)


def matmul_tn(xa, ga, x_spec, g_spec, out_shape, out_spec, acc_shape, grid, name, comm=None):
    nk = grid[-1]
    rider = _Rider(comm)

    def body(*refs):
        (x_ref, g_ref, o_ref, acc), ride = rider.split(refs, 2, 1, 1)
        k = pl.program_id(1)
        ride(pl.program_id(0) * nk + k, grid[0] * nk)

        @pl.when(k == 0)
        def _():
            acc[...] = jnp.zeros_like(acc)
        acc[...] += _dot_tn(x_ref[...], g_ref[...])

        @pl.when(k == nk - 1)
        def _():
            o_ref[...] = acc[...]

    outs = pl.pallas_call(
        body, name=name, grid=grid, in_specs=[x_spec, g_spec] + rider.in_specs,
        out_specs=[out_spec] + rider.out_specs,
        out_shape=[SDS(out_shape, F32)] + rider.out_shape,
        scratch_shapes=[pltpu.VMEM(acc_shape, F32)] + rider.scratch,
        compiler_params=_cparams(rider.sem(("parallel", "arbitrary"))))(xa, ga, *rider.ins)
    own, landed = rider.result(outs, 1)
    return own[0] if comm is None else (own[0], landed)


def wgrad_std(xa, ga, name):
    T, M = xa.shape
    N = ga.shape[1]
    tk = min(2048, T)
    return matmul_tn(xa, ga, pl.BlockSpec((tk, M), lambda n, k: (k, 0)), pl.BlockSpec((tk, N), lambda n, k: (k, 0)),
                     (M, N), pl.BlockSpec((M, N), lambda n, k: (0, 0)), (M, N), (1, T // tk), name)


def wgrad_down(p4, df, name):
    nj, T, nb = p4.shape
    D = df.shape[1]
    tk = min(2048, T)
    return matmul_tn(p4, df, pl.BlockSpec((None, tk, nb), lambda j, k: (j, k, 0)),
                     pl.BlockSpec((tk, D), lambda j, k: (k, 0)),
                     (nj * nb, D), pl.BlockSpec((nb, D), lambda j, k: (j, 0)), (nb, D), (nj, T // tk), name)


def wgrad_gu(u, dh8, name, comm=None):
    n8, T, nb = dh8.shape
    D = u.shape[1]
    tk = min(2048, T)
    return matmul_tn(u, dh8, pl.BlockSpec((tk, D), lambda j, k: (k, 0)),
                     pl.BlockSpec((None, tk, nb), lambda j, k: (j, k, 0)),
                     (n8, D, nb), pl.BlockSpec((None, D, nb), lambda j, k: (j, 0, 0)), (D, nb), (n8, T // tk), name,
                     comm=comm)


def wgrad_segments(u, segs, name):
    T, D = u.shape
    ns = len(segs)
    tk = min(1024, T)
    nk = T // tk

    def body(*refs):
        x_ref, g_refs, o_ref, acc = refs[0], refs[1:1 + ns], refs[1 + ns], refs[2 + ns]
        s = pl.program_id(0)
        k = pl.program_id(1)

        @pl.when(k == 0)
        def _():
            acc[...] = jnp.zeros_like(acc)
        for i in range(ns):
            @pl.when(s == i)
            def _(i=i):
                acc[...] += _dot_tn(x_ref[...], g_refs[i][...])

        @pl.when(k == nk - 1)
        def _():
            o_ref[...] = acc[...]

    seg_specs = [pl.BlockSpec((tk, D), lambda s, k, i=i: (jnp.where(s == i, k, 0), 0)) for i in range(ns)]
    return pl.pallas_call(
        body, name=name, grid=(ns, nk),
        in_specs=[pl.BlockSpec((tk, D), lambda s, k: (k, 0))] + seg_specs,
        out_specs=pl.BlockSpec((D, D), lambda s, k: (0, s)),
        out_shape=SDS((D, ns * D), F32), scratch_shapes=[pltpu.VMEM((D, D), F32)],
        compiler_params=_cparams(("parallel", "arbitrary")))(u, *segs)


def kernel(x, c, w_ada, b_ada, ffn1_w_gu, ffn1_w_down, ln1_g, ln1_b, w_in, w_sb_out, conv_w, conv_b, conv_ln_g, conv_ln_b, w_conv_out, w_out, ln2_g, ln2_b, ffn2_w_gu, ffn2_w_down, ln3_g, ln3_b, loss_target, m_w_ada, m_b_ada, m_ffn1_w_gu, m_ffn1_w_down, m_ln1_g, m_ln1_b, m_w_in, m_w_sb_out, m_conv_w, m_conv_b, m_conv_ln_g, m_conv_ln_b, m_w_conv_out, m_w_out, m_ln2_g, m_ln2_b, m_ffn2_w_gu, m_ffn2_w_down, m_ln3_g, m_ln3_b, v_w_ada, v_b_ada, v_ffn1_w_gu, v_ffn1_w_down, v_ln1_g, v_ln1_b, v_w_in, v_w_sb_out, v_conv_w, v_conv_b, v_conv_ln_g, v_conv_ln_b, v_w_conv_out, v_w_out, v_ln2_g, v_ln2_b, v_ffn2_w_gu, v_ffn2_w_down, v_ln3_g, v_ln3_b):
    Bl, S, D = x.shape
    T = Bl * S
    kw = conv_w.shape[1]
    ax, ay, ac = lax.axis_index("x"), lax.axis_index("y"), lax.axis_index("c")
    me = 4 * ax + 2 * ay + ac
    qc = jnp.stack([2 * ax + ay, ac]).astype(jnp.int32)

    big = dict(ffn1_w_gu=ffn1_w_gu[0], ffn1_w_down=ffn1_w_down[0], w_in=w_in[0], w_sb_out=w_sb_out[0],
               w_conv_out=w_conv_out[0], w_out=w_out[0], ffn2_w_gu=ffn2_w_gu[0], ffn2_w_down=ffn2_w_down[0])
    big_m = dict(ffn1_w_gu=m_ffn1_w_gu[0], ffn1_w_down=m_ffn1_w_down[0], w_in=m_w_in[0], w_sb_out=m_w_sb_out[0],
                 w_conv_out=m_w_conv_out[0], w_out=m_w_out[0], ffn2_w_gu=m_ffn2_w_gu[0], ffn2_w_down=m_ffn2_w_down[0])
    big_v = dict(ffn1_w_gu=v_ffn1_w_gu[0], ffn1_w_down=v_ffn1_w_down[0], w_in=v_w_in[0], w_sb_out=v_w_sb_out[0],
                 w_conv_out=v_w_conv_out[0], w_out=v_w_out[0], ffn2_w_gu=v_ffn2_w_gu[0], ffn2_w_down=v_ffn2_w_down[0])
    names = list(big)

    layer1, layer2, layer3 = ["ffn1_w_gu", "ffn1_w_down"], ["w_in", "w_sb_out", "w_conv_out", "w_out"], \
        ["ffn2_w_gu", "ffn2_w_down"]

    def shards(group):
        return [big[n].astype(BF) for n in group]

    wave0, wave1 = ["ffn1_w_gu"], ["ffn1_w_down", "w_in"]
    wave2, wave3 = ["w_sb_out", "w_conv_out", "w_out"], ["ffn2_w_gu", "ffn2_w_down"]
    wg1, cw8, c8 = run_comm(gather_plan(shards(wave0) + [conv_w[0], c]), "allgather_start")
    G = {"ffn1_w_gu": wg1}

    cw_full = jnp.transpose(cw8, (1, 0, 2)).reshape(kw, D)
    cw32 = jnp.concatenate([cw_full, jnp.zeros((HALO - kw, D), F32)], axis=0)

    c_all = c8.reshape(N_DEV * Bl, D)
    ncol = w_ada.shape[2]
    b_cols = lax.dynamic_slice(b_ada, (0, me * ncol), (1, ncol))
    mod_cols, s_all = ada_fwd(c_all, w_ada[0], b_cols, "ada_fwd")
    mod8 = small_allgather(mod_cols, "allgather_mod")
    mod_mine = lax.dynamic_slice(mod8, (0, me * Bl, 0), (N_DEV, Bl, ncol))
    mod = jnp.transpose(mod_mine, (1, 0, 2)).reshape(Bl, N_MOD_ROWS, 1, D)
    sh1, sc1, g1, sh2, sc2, g2, sh3, sc3, g3 = [mod[:, i] for i in range(N_MOD_ROWS)]

    ones = jnp.ones((1, D), F32)
    zeros = jnp.zeros((1, D), F32)
    xf = x.reshape(T, D)
    tgt = loss_target.reshape(T, D)

    (u1, a1, gg1, p1), landed = ffn_up(xf, ones, zeros, sc1, sh1, wg1, "ffn1_up", comm=gather_plan(shards(wave1)))
    G.update(zip(wave1, landed))
    wd1 = G["ffn1_w_down"].reshape(wg1.shape[0] // 2, wg1.shape[2], D)
    win = jnp.transpose(G["w_in"], (1, 0, 2)).reshape(D, -1)
    (xh1, rs1, f1), landed = ffn_down_ln(p1, wd1, xf, ones, zeros, g1, "ffn1_down_ln",
                                         comm=gather_plan(shards(wave2)))
    G.update(zip(wave2, landed))
    wsb = G["w_sb_out"].reshape(D, D)
    wco = G["w_conv_out"].reshape(-1, D)
    wout = G["w_out"].reshape(D, D)
    (u2, qkv, proj), landed = mod_matmul(xh1, ln1_g, ln1_b, sc2, sh2, win, "in_proj",
                                         comm=gather_plan(shards(wave3)))
    G.update(zip(wave3, landed))
    wg2 = G["ffn2_w_gu"]
    wd2 = G["ffn2_w_down"].reshape(wg2.shape[0] // 2, wg2.shape[2], D)
    yatt, car = att_fwd(qkv, Bl, S, D, "att_fwd")
    cs, xhc, rsc = conv_fwd(proj, cw32, conv_b, conv_ln_g, conv_ln_b, Bl, S, D, kw, "conv_fwd")
    xh2, rs2, ysb, yco, merged, o2 = mix_fwd(yatt, cs, proj, wsb, wco, wout, xh1, ln1_g, ln1_b, g2, "mix_fwd")
    (u3, a3, gg3, p3), _ = ffn_up(xh2, ln2_g, ln2_b, sc3, sh3, wg2, "ffn2_up")
    (xh3, rs3, f3), _ = ffn_down_ln(p3, wd2, xh2, ln2_g, ln2_b, g3, "ffn2_down_ln")

    own_sum, recv_b = {}, {}

    def by_owner(group, grads):
        return [g.reshape((4, 2) + big[n].shape) for n, g in zip(group, grads)]

    def chip_sums(group, g42, recv_a):
        sums = [chip_sum(g, r, qc, "chip_sum_" + n) for g, r, n in zip(g42, recv_a, group)]
        own_sum.update({n: s[0] for n, s in zip(group, sums)})
        return chips_plan([s[1] for s in sums])

    dr3, df3, dln3g, dln3b, dg3, lossp, dh3 = ln_bwd(None, xh3, rs3, ln3_g, ln3_b, g3, f3, MACARON_WEIGHT,
                                                      "ln3_swiglu_bwd", target=tgt, swiglu=(wd2, a3, gg3))
    dh3 = dh3.reshape((-1,) + a3.shape[1:])
    gw_d2 = wgrad_down(p3, df3, "wgrad_ffn2_down")
    gw_g2 = wgrad_gu(u3, dh3, "wgrad_ffn2_gu")
    g42 = by_owner(layer3, [gw_g2, gw_d2])
    (dx2, dsc3, dsh3), recv_a = mod_bwd(dh3, wg2, dr3, xh2, ln2_g, ln2_b, sc3, True, "ffn2_mod_bwd",
                                        comm=sibling_plan(g42))
    plan3 = chip_sums(layer3, g42, recv_a)

    dr2, do2, dln2g, dln2b, dg2 = ln_bwd(dx2, xh2, rs2, ln2_g, ln2_b, g2, o2, 1.0, "ln2_bwd")
    gw_out = wgrad_std(merged, do2, "wgrad_out")
    dysb, dyco, dga, dgb, dyatt, dcs = merge_bwd(do2, proj, ysb, yco, wsb, wco, wout, "merge_bwd")
    gw_sb = wgrad_std(yatt, dysb, "wgrad_sb")
    gw_co = wgrad_std(cs, dyco, "wgrad_conv_out")
    (dq, dk, dv), landed = att_bwd(qkv, dyatt, car, Bl, S, D, "att_bwd", comm=plan3)
    recv_b.update(zip(layer3, landed))
    dconv, dcg, dcbeta, dcb = conv_ln_bwd(dcs, xhc, rsc, conv_ln_g, conv_ln_b, "conv_ln_bwd")
    dglu_a, dglu_b, dcw = conv_bwd(dconv, proj, cw32, Bl, S, D, kw, "conv_bwd")
    dproj = [dq, dk, dv, dglu_a, dglu_b, dga, dgb]
    gw_in = wgrad_segments(u2, dproj, "wgrad_in")
    gw_in = jnp.transpose(gw_in.reshape(D, N_DEV, -1), (1, 0, 2))
    g42 = by_owner(layer2, [gw_in, gw_sb, gw_co, gw_out])
    (dx1, dsc2, dsh2), recv_a = mod_bwd(dproj, win, dr2, xh1, ln1_g, ln1_b, sc2, False, "mix_mod_bwd",
                                        comm=sibling_plan(g42))
    plan2 = chip_sums(layer2, g42, recv_a)

    dr1, df1, dln1g, dln1b, dg1, dh1 = ln_bwd(dx1, xh1, rs1, ln1_g, ln1_b, g1, f1, MACARON_WEIGHT,
                                              "ln1_swiglu_bwd", swiglu=(wd1, a1, gg1))
    dh1 = dh1.reshape((-1,) + a1.shape[1:])
    gw_d1 = wgrad_down(p1, df1, "wgrad_ffn1_down")
    gw_g1, landed = wgrad_gu(u1, dh1, "wgrad_ffn1_gu", comm=plan2)
    recv_b.update(zip(layer2, landed))
    g42 = by_owner(layer1, [gw_g1, gw_d1])
    plan1 = chip_sums(layer1, g42, run_comm(sibling_plan(g42), "rs_sibling_ffn1"))
    (grad_x, dsc1, dsh1), landed = mod_bwd(dh1, wg1, dr1, xf, ones, zeros, sc1, True, "ffn1_mod_bwd", comm=plan1)
    recv_b.update(zip(layer1, landed))

    dmod = jnp.concatenate([dsh1, dsc1, dg1, dsh2, dsc2, dg2, dsh3, dsc3, dg3], axis=1).reshape(Bl, N_MOD_ROWS * D)
    dmod_all = small_allgather(dmod, "allgather_dmod").reshape(N_DEV * Bl, N_MOD_ROWS * D)
    dmod_cols = lax.dynamic_slice(dmod_all, (0, me * ncol), (N_DEV * Bl, ncol))
    g_w_ada, g_b_ada = ada_bwd(s_all, dmod_cols, dmod_all, "ada_bwd")

    npad = 16
    small_rows = [dln1g, dln1b, dcb, dcg, dcbeta, dln2g, dln2b, dln3g, dln3b,
                  jnp.broadcast_to(lossp[0:1, 0:1], (1, D))]
    pack = jnp.concatenate(small_rows + [jnp.zeros((npad - len(small_rows), D), F32), dcw], axis=0)
    small = small_sum(small_allgather(pack, "allgather_small"), "small_sum")
    loss = small[9, 0]
    small_w = [ln1_g, ln1_b, conv_b, conv_ln_g, conv_ln_b, ln2_g, ln2_b, ln3_g, ln3_b]
    small_m = [m_ln1_g, m_ln1_b, m_conv_b, m_conv_ln_g, m_conv_ln_b, m_ln2_g, m_ln2_b, m_ln3_g, m_ln3_b]
    small_v = [v_ln1_g, v_ln1_b, v_conv_b, v_conv_ln_g, v_conv_ln_b, v_ln2_g, v_ln2_b, v_ln3_g, v_ln3_b]
    padrows = jnp.zeros((npad - len(small_w), D), F32)
    s_g, s_d, s_m, s_v = adamw(jnp.concatenate(small_w + [padrows], axis=0),
                               jnp.concatenate(small_m + [padrows], axis=0),
                               jnp.concatenate(small_v + [padrows], axis=0),
                               [_plain_part(small[:npad])], "adamw_small")
    dcw_mine = lax.dynamic_slice(small[npad:npad + kw], (0, me * (D // N_DEV)), (kw, D // N_DEV))
    cw_g, cw_d, cw_m, cw_v = adamw(conv_w[0], m_conv_w[0], v_conv_w[0], [_plain_part(dcw_mine)], "adamw_conv_w")
    ada_g, ada_d, ada_m, ada_v = adamw(w_ada[0], m_w_ada[0], v_w_ada[0], [_plain_part(g_w_ada)], "adamw_w_ada")
    bada_g, bada_d, bada_m, bada_v = adamw(b_ada, m_b_ada, v_b_ada, [_plain_part(g_b_ada)], "adamw_b_ada")

    res = {}
    for n in names:
        rb = recv_b[n]
        parts = [_plain_part(own_sum[n]), _slot_part(rb, 0), _slot_part(rb, 1), _slot_part(rb, 2)]
        res[n] = adamw(big[n], big_m[n], big_v[n], parts, "adamw_" + n)

    def small_out(k):
        order = dict(ln1_g=0, ln1_b=1, conv_b=2, conv_ln_g=3, conv_ln_b=4, ln2_g=5, ln2_b=6, ln3_g=7, ln3_b=8)
        return lambda arr: arr[order[k]:order[k] + 1]

    weight_order = ["w_ada", "b_ada", "ffn1_w_gu", "ffn1_w_down", "ln1_g", "ln1_b", "w_in", "w_sb_out", "conv_w",
                    "conv_b", "conv_ln_g", "conv_ln_b", "w_conv_out", "w_out", "ln2_g", "ln2_b", "ffn2_w_gu",
                    "ffn2_w_down", "ln3_g", "ln3_b"]

    shapes = dict(w_ada=w_ada.shape, b_ada=b_ada.shape, conv_w=conv_w.shape, ln1_g=ln1_g.shape,
                  **{n: (1,) + big[n].shape for n in names})

    def pick(which):
        outs = []
        for n in weight_order:
            if n == "w_ada":
                a = (ada_g, ada_d, ada_m, ada_v)[which]
            elif n == "b_ada":
                a = (bada_g, bada_d, bada_m, bada_v)[which]
            elif n == "conv_w":
                a = (cw_g, cw_d, cw_m, cw_v)[which]
            elif n in res:
                a = res[n][which]
            else:
                a = small_out(n)((s_g, s_d, s_m, s_v)[which])
            outs.append(a.reshape(shapes.get(n, ln1_g.shape)))
        return outs

    return (loss, grad_x.reshape(Bl, S, D), *pick(0), *pick(1), *pick(2), *pick(3))
```
